```python
import math
import jax
import jax.numpy as jnp
from jax import lax
import numpy as np

D_MODEL = 1024
BATCH = 4
SEQ = 8192
DEPTH = 1

CHUNK = 64
Q_BLOCK = 128
EPS = 1e-6

A_HEADS = 8
A_HEAD_DIM = 64
A_Q_LORA = 256
A_KV_LORA = 256
IDX_HEADS = 8
IDX_DIM = 64
IDX_TOPK_MAX = 256
ATTN_SCALE = A_HEAD_DIM ** -0.5
IDX_SCALE = (IDX_HEADS * IDX_DIM) ** -0.5

B_HEADS = 4
B_HEAD_DIM = 128
B_QK = B_HEADS * B_HEAD_DIM
CONV_WIDTH = 4

REL_BUCKETS = 32
REL_MAX_DIST = 128

MEM_LEN = 256
X_HEADS = 4
X_HEAD_DIM = 128

P_HEADS = 8
N_KEYS = 128
N_EXPERTS = N_KEYS * N_KEYS
P_TOPK = 16
P_QDIM = 256
P_TOKEN_BLOCK = 128

COL_WIDTHS = (A_Q_LORA, A_KV_LORA, IDX_DIM, IDX_HEADS, B_QK, B_QK, B_QK, B_QK, B_HEADS, B_HEADS)
IN_COLS = sum(COL_WIDTHS)
MIX_WIDTH = A_HEADS * A_HEAD_DIM + B_HEADS * B_HEAD_DIM

kernel_name = 'hybrid_dsa_gdn_peer_block'


def rmsnorm(x, g):
    xf = x.astype(jnp.float32)
    y = xf * lax.rsqrt(jnp.mean(xf * xf, axis=-1, keepdims=True) + EPS)
    return (y * g.astype(jnp.float32)).astype(x.dtype)


def l2norm(x):
    return x * lax.rsqrt(jnp.sum(x * x, axis=-1, keepdims=True) + EPS)


def split_columns(p, widths):
    out, start = [], 0
    for w in widths:
        out.append(p[..., start:start + w])
        start += w
    return out


def t5_bucket(rel):
    half = REL_BUCKETS // 2
    max_exact = half // 2
    n = jnp.abs(rel)
    nf = jnp.maximum(n, max_exact).astype(jnp.float32)
    large = max_exact + (jnp.log(nf / max_exact) / math.log(REL_MAX_DIST / max_exact)
                         * (half - max_exact)).astype(jnp.int32)
    large = jnp.minimum(large, half - 1)
    return jnp.where(rel > 0, half, 0) + jnp.where(n < max_exact, n, large)


def dsa_attention(cq, ckv, kidx, widx, g_cq, g_ckv, g_kidx, w_uq, w_iq, w_uk, w_uv, rel_bias):
    bsz, seq, _ = cq.shape
    topk = min(IDX_TOPK_MAX, seq // 4)
    n_blocks = seq // Q_BLOCK
    cq = rmsnorm(cq, g_cq)
    ckv = rmsnorm(ckv, g_ckv)
    kidx = rmsnorm(kidx, g_kidx).astype(jnp.float32)
    key_pos = jnp.arange(seq, dtype=jnp.int32)

    def blockify(a):
        return jnp.moveaxis(a.reshape(bsz, n_blocks, Q_BLOCK, a.shape[-1]), 1, 0)

    def one_block(args):
        cq_b, w_b, start = args
        qpos = start + jnp.arange(Q_BLOCK, dtype=jnp.int32)
        limit = (qpos // CHUNK + 1) * CHUNK
        admissible = key_pos[None, :] < limit[:, None]
        qi = jnp.einsum('bqc,chd->bqhd', cq_b, w_iq).astype(jnp.float32)
        dots = jax.nn.relu(jnp.einsum('bqhd,bsd->bqhs', qi, kidx))
        score = jnp.einsum('bqh,bqhs->bqs', w_b.astype(jnp.float32) * IDX_SCALE, dots)
        score = jnp.where(admissible[None], score, -jnp.inf)
        _, sel = lax.top_k(score, topk)
        valid = sel < limit[None, :, None]
        c_sel = jax.vmap(lambda c, i: c[i])(ckv, sel)
        q = jnp.einsum('bqc,chd->bqhd', cq_b, w_uq)
        q_lat = jnp.einsum('bqhd,chd->bqhc', q, w_uk)
        logits = jnp.einsum('bqhc,bqkc->bhqk', q_lat, c_sel).astype(jnp.float32) * ATTN_SCALE
        bias = rel_bias[t5_bucket(sel - qpos[None, :, None])].astype(jnp.float32)
        logits = logits + jnp.transpose(bias, (0, 3, 1, 2))
        logits = jnp.where(valid[:, None], logits, -jnp.inf)
        p = jax.nn.softmax(logits, axis=-1).astype(c_sel.dtype)
        o_lat = jnp.einsum('bhqk,bqkc->bqhc', p, c_sel)
        o = jnp.einsum('bqhc,chd->bqhd', o_lat, w_uv)
        return o.reshape(bsz, Q_BLOCK, A_HEADS * A_HEAD_DIM)

    starts = jnp.arange(n_blocks, dtype=jnp.int32) * Q_BLOCK
    o = lax.map(one_block, (blockify(cq), blockify(widx), starts))
    return jnp.moveaxis(o, 0, 1).reshape(bsz, seq, A_HEADS * A_HEAD_DIM)


def causal_dwconv(x, w):
    ch = x.shape[-1]
    return lax.conv_general_dilated(x, w[:, None, :].astype(x.dtype), window_strides=(1,),
                                    padding=[(CONV_WIDTH - 1, 0)],
                                    dimension_numbers=('NWC', 'WIO', 'NWC'),
                                    feature_group_count=ch)


def gated_deltanet(q, k, v, z, a, b, conv_w, A_log, dt_bias, g_onorm):
    bsz, seq, _ = q.shape
    f32 = jnp.float32
    H, Dh = B_HEADS, B_HEAD_DIM
    nc = seq // CHUNK
    qkv = jax.nn.silu(causal_dwconv(jnp.concatenate([q, k, v], axis=-1), conv_w)).astype(f32)
    q, k, v = [t.reshape(bsz, seq, H, Dh) for t in jnp.split(qkv, 3, axis=-1)]
    q = l2norm(q) * (Dh ** -0.5)
    k = l2norm(k)
    beta = jax.nn.sigmoid(b.astype(f32))
    g = -jnp.exp(A_log.astype(f32)) * jax.nn.softplus(a.astype(f32) + dt_bias.astype(f32))

    def to_chunks(t):
        t = t.reshape(bsz, nc, CHUNK, *t.shape[2:])
        return jnp.moveaxis(t, 3, 1)

    q, k, v, beta = to_chunks(q), to_chunks(k), to_chunks(v), to_chunks(beta)
    g = jnp.cumsum(to_chunks(g), axis=-1)
    pos = jnp.arange(CHUNK)
    causal = pos[:, None] >= pos[None, :]
    strict = pos[:, None] > pos[None, :]
    decay = jnp.exp(jnp.where(causal, g[..., :, None] - g[..., None, :], -jnp.inf))
    m = jnp.where(strict, beta[..., :, None] * jnp.einsum('bhnid,bhnjd->bhnij', k, k) * decay, 0.0)
    rhs = jnp.concatenate([v * beta[..., None], k * (beta * jnp.exp(g))[..., None]], axis=-1)
    sol = lax.linalg.triangular_solve(m, rhs, left_side=True, lower=True, unit_diagonal=True)
    u, w = sol[..., :Dh], sol[..., Dh:]
    intra = jnp.einsum('bhnid,bhnjd->bhnij', q, k) * decay
    q_dec = q * jnp.exp(g)[..., None]
    k_tail = k * jnp.exp(g[..., -1:] - g)[..., None]
    chunk_decay = jnp.exp(g[..., -1])

    def step(state, inp):
        q_c, u_c, w_c, a_c, kt_c, d_c = inp
        v_new = u_c - jnp.einsum('bhcd,bhde->bhce', w_c, state)
        o_c = jnp.einsum('bhcd,bhde->bhce', q_c, state) + jnp.einsum('bhij,bhje->bhie', a_c, v_new)
        state = state * d_c[..., None, None] + jnp.einsum('bhcd,bhce->bhde', kt_c, v_new)
        return state, o_c

    xs = tuple(jnp.moveaxis(t, 2, 0) for t in (q_dec, u, w, intra, k_tail, chunk_decay))
    _, o = lax.scan(step, jnp.zeros((bsz, H, Dh, Dh), f32), xs)
    o = jnp.transpose(o, (1, 0, 3, 2, 4)).reshape(bsz, seq, H, Dh)
    o = o * lax.rsqrt(jnp.mean(o * o, axis=-1, keepdims=True) + EPS) * g_onorm.astype(f32)
    o = o * jax.nn.silu(z.astype(f32).reshape(bsz, seq, H, Dh))
    return o.reshape(bsz, seq, H * Dh).astype(z.dtype)


def memory_cross_attention(h, mem_n, wq, wk, wv, wo):
    q = jnp.einsum('bsd,dhe->bhse', h, wq)
    k = jnp.einsum('bmd,dhe->bhme', mem_n, wk)
    v = jnp.einsum('bmd,dhe->bhme', mem_n, wv)
    logits = jnp.einsum('bhse,bhme->bhsm', q, k).astype(jnp.float32) * (X_HEAD_DIM ** -0.5)
    p = jax.nn.softmax(logits, axis=-1).astype(v.dtype)
    o = jnp.einsum('bhsm,bhme->bshe', p, v)
    return jnp.einsum('bshe,hed->bsd', o, wo)


def peer_ffn(h, w_pq, sub_keys, u_emb, v_emb):
    bsz, seq, d = h.shape
    n_tok = bsz * seq
    hb = h.reshape(n_tok // P_TOKEN_BLOCK, P_TOKEN_BLOCK, d)

    def one_block(xb):
        n = xb.shape[0]
        qr = jnp.einsum('nd,dhe->nhe', xb, w_pq).reshape(n, P_HEADS, 2, P_QDIM // 2)
        s = jnp.einsum('nhpe,hpke->nhpk', qr, sub_keys).astype(jnp.float32)
        s1, i1 = lax.top_k(s[:, :, 0], P_TOPK)
        s2, i2 = lax.top_k(s[:, :, 1], P_TOPK)
        cand = (s1[..., :, None] + s2[..., None, :]).reshape(n, P_HEADS, P_TOPK * P_TOPK)
        cidx = (i1[..., :, None] * N_KEYS + i2[..., None, :]).reshape(n, P_HEADS, P_TOPK * P_TOPK)
        best, pos = lax.top_k(cand, P_TOPK)
        eidx = jnp.take_along_axis(cidx, pos, axis=-1)
        gate = jax.nn.softmax(best, axis=-1).astype(xb.dtype)
        act = jax.nn.gelu(jnp.einsum('nhkd,nd->nhk', u_emb[eidx], xb), approximate=False)
        return jnp.einsum('nhk,nhkd->nd', gate * act, v_emb[eidx])

    return lax.map(one_block, hb).reshape(bsz, seq, d)


def setup_inputs(seed: int = 0) -> dict:
    key = jax.random.key(seed)
    ks = jax.random.split(key, 32)
    f32 = jnp.float32
    L = DEPTH

    def nrm(k, shape, scale):
        return jax.random.normal(k, shape, f32) * scale

    def gain(k, shape):
        return 1.0 + 0.02 * jax.random.normal(k, shape, f32)

    dt = jnp.exp(jax.random.uniform(ks[14], (L, B_HEADS), f32, math.log(1e-3), math.log(1e-1)))
    return {
        'x': nrm(ks[0], (BATCH, SEQ, D_MODEL), 1.0),
        'mem': nrm(ks[1], (BATCH, MEM_LEN, D_MODEL), 1.0),
        'g_mix': gain(ks[2], (L, D_MODEL)),
        'w_in': nrm(ks[3], (L, D_MODEL, IN_COLS), D_MODEL ** -0.5),
        'g_cq': gain(ks[4], (L, A_Q_LORA)),
        'g_ckv': gain(ks[5], (L, A_KV_LORA)),
        'g_kidx': gain(ks[6], (L, IDX_DIM)),
        'w_uq': nrm(ks[7], (L, A_Q_LORA, A_HEADS, A_HEAD_DIM), A_Q_LORA ** -0.5),
        'w_iq': nrm(ks[8], (L, A_Q_LORA, IDX_HEADS, IDX_DIM), A_Q_LORA ** -0.5),
        'w_uk': nrm(ks[9], (L, A_KV_LORA, A_HEADS, A_HEAD_DIM), A_KV_LORA ** -0.5),
        'w_uv': nrm(ks[10], (L, A_KV_LORA, A_HEADS, A_HEAD_DIM), A_KV_LORA ** -0.5),
        'rel_bias': nrm(ks[11], (REL_BUCKETS, A_HEADS), 0.3),
        'conv_w': nrm(ks[12], (L, CONV_WIDTH, 3 * B_QK), CONV_WIDTH ** -0.5),
        'A_log': jnp.log(jax.random.uniform(ks[13], (L, B_HEADS), f32, 1.0, 16.0)),
        'dt_bias': dt + jnp.log(-jnp.expm1(-dt)),
        'g_onorm': gain(ks[15], (L, B_HEAD_DIM)),
        'w_out': nrm(ks[16], (L, MIX_WIDTH, D_MODEL), MIX_WIDTH ** -0.5),
        'g_cross': gain(ks[17], (L, D_MODEL)),
        'g_mem': gain(ks[18], (L, D_MODEL)),
        'wq_x': nrm(ks[19], (L, D_MODEL, X_HEADS, X_HEAD_DIM), D_MODEL ** -0.5),
        'wk_x': nrm(ks[20], (L, D_MODEL, X_HEADS, X_HEAD_DIM), D_MODEL ** -0.5),
        'wv_x': nrm(ks[21], (L, D_MODEL, X_HEADS, X_HEAD_DIM), D_MODEL ** -0.5),
        'wo_x': nrm(ks[22], (L, X_HEADS, X_HEAD_DIM, D_MODEL), (X_HEADS * X_HEAD_DIM) ** -0.5),
        'g_ffn': gain(ks[23], (L, D_MODEL)),
        'w_pq': nrm(ks[24], (L, D_MODEL, P_HEADS, P_QDIM), D_MODEL ** -0.5),
        'sub_keys': nrm(ks[25], (L, P_HEADS, 2, N_KEYS, P_QDIM // 2), (P_QDIM // 2) ** -0.5),
        'u_emb': nrm(ks[26], (L, N_EXPERTS, D_MODEL), D_MODEL ** -0.5),
        'v_emb': nrm(ks[27], (L, N_EXPERTS, D_MODEL), P_HEADS ** -0.5),
        'g_final': gain(ks[28], (D_MODEL,)),
    }


def reference(x, mem, g_mix, w_in, g_cq, g_ckv, g_kidx, w_uq, w_iq, w_uk, w_uv, rel_bias, conv_w, A_log, dt_bias, g_onorm, w_out, g_cross, g_mem, wq_x, wk_x, wv_x, wo_x, g_ffn, w_pq, sub_keys, u_emb, v_emb, g_final):
    for l in range(DEPTH):
        h = rmsnorm(x, g_mix[l])
        proj = jnp.einsum('bsd,dc->bsc', h, w_in[l])
        cq, ckv, kidx, widx, bq, bk, bv, bz, ba, bb = split_columns(proj, COL_WIDTHS)
        o_a = dsa_attention(cq, ckv, kidx, widx, g_cq[l], g_ckv[l], g_kidx[l],
                            w_uq[l], w_iq[l], w_uk[l], w_uv[l], rel_bias)
        o_b = gated_deltanet(bq, bk, bv, bz, ba, bb, conv_w[l], A_log[l], dt_bias[l], g_onorm[l])
        x = x + jnp.einsum('bsc,cd->bsd', jnp.concatenate([o_a, o_b], axis=-1), w_out[l])
        x = x + memory_cross_attention(rmsnorm(x, g_cross[l]), rmsnorm(mem, g_mem[l]),
                                       wq_x[l], wk_x[l], wv_x[l], wo_x[l])
        x = x + peer_ffn(rmsnorm(x, g_ffn[l]), w_pq[l], sub_keys[l], u_emb[l], v_emb[l])
    return rmsnorm(x, g_final)
```

```python
import math
from functools import partial
import jax
import jax.numpy as jnp
from jax import lax
import numpy as np
from jax.experimental import pallas as pl
from jax.experimental.pallas import tpu as pltpu

D_MODEL = 1024
BATCH = 4
SEQ = 8192
DEPTH = 1

CHUNK = 64
Q_BLOCK = 128
EPS = 1e-6

A_HEADS = 8
A_HEAD_DIM = 64
A_Q_LORA = 256
A_KV_LORA = 256
IDX_HEADS = 8
IDX_DIM = 64
IDX_TOPK_MAX = 256
ATTN_SCALE = A_HEAD_DIM ** -0.5
IDX_SCALE = (IDX_HEADS * IDX_DIM) ** -0.5

B_HEADS = 4
B_HEAD_DIM = 128
B_QK = B_HEADS * B_HEAD_DIM
CONV_WIDTH = 4

REL_BUCKETS = 32
REL_MAX_DIST = 128

MEM_LEN = 256
X_HEADS = 4
X_HEAD_DIM = 128

P_HEADS = 8
N_KEYS = 128
N_EXPERTS = N_KEYS * N_KEYS
P_TOPK = 16
P_QDIM = 256
P_TOKEN_BLOCK = 128

COL_WIDTHS = (A_Q_LORA, A_KV_LORA, IDX_DIM, IDX_HEADS, B_QK, B_QK, B_QK, B_QK, B_HEADS, B_HEADS)
IN_COLS = sum(COL_WIDTHS)
MIX_WIDTH = A_HEADS * A_HEAD_DIM + B_HEADS * B_HEAD_DIM


def rmsnorm(x, g):
    xf = x.astype(jnp.float32)
    y = xf * lax.rsqrt(jnp.mean(xf * xf, axis=-1, keepdims=True) + EPS)
    return (y * g.astype(jnp.float32)).astype(x.dtype)


def l2norm(x):
    return x * lax.rsqrt(jnp.sum(x * x, axis=-1, keepdims=True) + EPS)


def split_columns(p, widths):
    out, start = [], 0
    for w in widths:
        out.append(p[..., start:start + w])
        start += w
    return out


def t5_bucket(rel):
    half = REL_BUCKETS // 2
    max_exact = half // 2
    n = jnp.abs(rel)
    nf = jnp.maximum(n, max_exact).astype(jnp.float32)
    large = max_exact + (jnp.log(nf / max_exact) / math.log(REL_MAX_DIST / max_exact)
                         * (half - max_exact)).astype(jnp.int32)
    large = jnp.minimum(large, half - 1)
    return jnp.where(rel > 0, half, 0) + jnp.where(n < max_exact, n, large)


def dsa_attention(cq, ckv, kidx, widx, g_cq, g_ckv, g_kidx, w_uq, w_iq, w_uk, w_uv, rel_bias):
    bsz, seq, _ = cq.shape
    topk = min(IDX_TOPK_MAX, seq // 4)
    n_blocks = seq // Q_BLOCK
    cq = rmsnorm(cq, g_cq)
    ckv = rmsnorm(ckv, g_ckv)
    kidx = rmsnorm(kidx, g_kidx).astype(jnp.float32)
    key_pos = jnp.arange(seq, dtype=jnp.int32)

    def blockify(a):
        return jnp.moveaxis(a.reshape(bsz, n_blocks, Q_BLOCK, a.shape[-1]), 1, 0)

    def one_block(args):
        cq_b, w_b, start = args
        qpos = start + jnp.arange(Q_BLOCK, dtype=jnp.int32)
        limit = (qpos // CHUNK + 1) * CHUNK
        admissible = key_pos[None, :] < limit[:, None]
        qi = jnp.einsum('bqc,chd->bqhd', cq_b, w_iq).astype(jnp.float32)
        dots = jax.nn.relu(jnp.einsum('bqhd,bsd->bqhs', qi, kidx))
        score = jnp.einsum('bqh,bqhs->bqs', w_b.astype(jnp.float32) * IDX_SCALE, dots)
        score = jnp.where(admissible[None], score, -jnp.inf)
        _, sel = lax.top_k(score, topk)
        valid = sel < limit[None, :, None]
        c_sel = jax.vmap(lambda c, i: c[i])(ckv, sel)
        q = jnp.einsum('bqc,chd->bqhd', cq_b, w_uq)
        q_lat = jnp.einsum('bqhd,chd->bqhc', q, w_uk)
        logits = jnp.einsum('bqhc,bqkc->bhqk', q_lat, c_sel).astype(jnp.float32) * ATTN_SCALE
        bias = rel_bias[t5_bucket(sel - qpos[None, :, None])].astype(jnp.float32)
        logits = logits + jnp.transpose(bias, (0, 3, 1, 2))
        logits = jnp.where(valid[:, None], logits, -jnp.inf)
        p = jax.nn.softmax(logits, axis=-1).astype(c_sel.dtype)
        o_lat = jnp.einsum('bhqk,bqkc->bqhc', p, c_sel)
        o = jnp.einsum('bqhc,chd->bqhd', o_lat, w_uv)
        return o.reshape(bsz, Q_BLOCK, A_HEADS * A_HEAD_DIM)

    starts = jnp.arange(n_blocks, dtype=jnp.int32) * Q_BLOCK
    o = lax.map(one_block, (blockify(cq), blockify(widx), starts))
    return jnp.moveaxis(o, 0, 1).reshape(bsz, seq, A_HEADS * A_HEAD_DIM)


def causal_dwconv(x, w):
    ch = x.shape[-1]
    return lax.conv_general_dilated(x, w[:, None, :].astype(x.dtype), window_strides=(1,),
                                    padding=[(CONV_WIDTH - 1, 0)],
                                    dimension_numbers=('NWC', 'WIO', 'NWC'),
                                    feature_group_count=ch)


def gated_deltanet(q, k, v, z, a, b, conv_w, A_log, dt_bias, g_onorm):
    bsz, seq, _ = q.shape
    f32 = jnp.float32
    H, Dh = B_HEADS, B_HEAD_DIM
    nc = seq // CHUNK
    qkv = jax.nn.silu(causal_dwconv(jnp.concatenate([q, k, v], axis=-1), conv_w)).astype(f32)
    q, k, v = [t.reshape(bsz, seq, H, Dh) for t in jnp.split(qkv, 3, axis=-1)]
    q = l2norm(q) * (Dh ** -0.5)
    k = l2norm(k)
    beta = jax.nn.sigmoid(b.astype(f32))
    g = -jnp.exp(A_log.astype(f32)) * jax.nn.softplus(a.astype(f32) + dt_bias.astype(f32))

    def to_chunks(t):
        t = t.reshape(bsz, nc, CHUNK, *t.shape[2:])
        return jnp.moveaxis(t, 3, 1)

    q, k, v, beta = to_chunks(q), to_chunks(k), to_chunks(v), to_chunks(beta)
    g = jnp.cumsum(to_chunks(g), axis=-1)
    pos = jnp.arange(CHUNK)
    causal = pos[:, None] >= pos[None, :]
    strict = pos[:, None] > pos[None, :]
    decay = jnp.exp(jnp.where(causal, g[..., :, None] - g[..., None, :], -jnp.inf))
    m = jnp.where(strict, beta[..., :, None] * jnp.einsum('bhnid,bhnjd->bhnij', k, k) * decay, 0.0)
    rhs = jnp.concatenate([v * beta[..., None], k * (beta * jnp.exp(g))[..., None]], axis=-1)
    sol = lax.linalg.triangular_solve(m, rhs, left_side=True, lower=True, unit_diagonal=True)
    u, w = sol[..., :Dh], sol[..., Dh:]
    intra = jnp.einsum('bhnid,bhnjd->bhnij', q, k) * decay
    q_dec = q * jnp.exp(g)[..., None]
    k_tail = k * jnp.exp(g[..., -1:] - g)[..., None]
    chunk_decay = jnp.exp(g[..., -1])

    def step(state, inp):
        q_c, u_c, w_c, a_c, kt_c, d_c = inp
        v_new = u_c - jnp.einsum('bhcd,bhde->bhce', w_c, state)
        o_c = jnp.einsum('bhcd,bhde->bhce', q_c, state) + jnp.einsum('bhij,bhje->bhie', a_c, v_new)
        state = state * d_c[..., None, None] + jnp.einsum('bhcd,bhce->bhde', kt_c, v_new)
        return state, o_c

    xs = tuple(jnp.moveaxis(t, 2, 0) for t in (q_dec, u, w, intra, k_tail, chunk_decay))
    _, o = lax.scan(step, jnp.zeros((bsz, H, Dh, Dh), f32), xs)
    o = jnp.transpose(o, (1, 0, 3, 2, 4)).reshape(bsz, seq, H, Dh)
    o = o * lax.rsqrt(jnp.mean(o * o, axis=-1, keepdims=True) + EPS) * g_onorm.astype(f32)
    o = o * jax.nn.silu(z.astype(f32).reshape(bsz, seq, H, Dh))
    return o.reshape(bsz, seq, H * Dh).astype(z.dtype)


def memory_cross_attention(h, mem_n, wq, wk, wv, wo):
    q = jnp.einsum('bsd,dhe->bhse', h, wq)
    k = jnp.einsum('bmd,dhe->bhme', mem_n, wk)
    v = jnp.einsum('bmd,dhe->bhme', mem_n, wv)
    logits = jnp.einsum('bhse,bhme->bhsm', q, k).astype(jnp.float32) * (X_HEAD_DIM ** -0.5)
    p = jax.nn.softmax(logits, axis=-1).astype(v.dtype)
    o = jnp.einsum('bhsm,bhme->bshe', p, v)
    return jnp.einsum('bshe,hed->bsd', o, wo)


def peer_ffn(h, w_pq, sub_keys, u_emb, v_emb):
    bsz, seq, d = h.shape
    n_tok = bsz * seq
    hb = h.reshape(n_tok // P_TOKEN_BLOCK, P_TOKEN_BLOCK, d)

    def one_block(xb):
        n = xb.shape[0]
        qr = jnp.einsum('nd,dhe->nhe', xb, w_pq).reshape(n, P_HEADS, 2, P_QDIM // 2)
        s = jnp.einsum('nhpe,hpke->nhpk', qr, sub_keys).astype(jnp.float32)
        s1, i1 = lax.top_k(s[:, :, 0], P_TOPK)
        s2, i2 = lax.top_k(s[:, :, 1], P_TOPK)
        cand = (s1[..., :, None] + s2[..., None, :]).reshape(n, P_HEADS, P_TOPK * P_TOPK)
        cidx = (i1[..., :, None] * N_KEYS + i2[..., None, :]).reshape(n, P_HEADS, P_TOPK * P_TOPK)
        best, pos = lax.top_k(cand, P_TOPK)
        eidx = jnp.take_along_axis(cidx, pos, axis=-1)
        gate = jax.nn.softmax(best, axis=-1).astype(xb.dtype)
        act = jax.nn.gelu(jnp.einsum('nhkd,nd->nhk', u_emb[eidx], xb), approximate=False)
        return jnp.einsum('nhk,nhkd->nd', gate * act, v_emb[eidx])

    return lax.map(one_block, hb).reshape(bsz, seq, d)


def _final_rmsnorm_kernel(x_ref, g_ref, o_ref):
    xf = x_ref[...]
    y = xf * lax.rsqrt(jnp.mean(xf * xf, axis=-1, keepdims=True) + EPS)
    o_ref[...] = y * g_ref[...]


def final_rmsnorm(x, g):
    bsz, seq, d = x.shape
    n = bsz * seq
    tm = 512
    out = pl.pallas_call(
        _final_rmsnorm_kernel,
        grid=(n // tm,),
        in_specs=[pl.BlockSpec((tm, d), lambda i: (i, 0)),
                  pl.BlockSpec((1, d), lambda i: (0, 0))],
        out_specs=pl.BlockSpec((tm, d), lambda i: (i, 0)),
        out_shape=jax.ShapeDtypeStruct((n, d), x.dtype),
        name="final_rmsnorm",
    )(x.reshape(n, d), g.reshape(1, d))
    return out.reshape(bsz, seq, d)


def kernel(x, mem, g_mix, w_in, g_cq, g_ckv, g_kidx, w_uq, w_iq, w_uk, w_uv, rel_bias, conv_w, A_log, dt_bias, g_onorm, w_out, g_cross, g_mem, wq_x, wk_x, wv_x, wo_x, g_ffn, w_pq, sub_keys, u_emb, v_emb, g_final):
    for l in range(DEPTH):
        h = rmsnorm(x, g_mix[l])
        proj = jnp.einsum('bsd,dc->bsc', h, w_in[l])
        cq, ckv, kidx, widx, bq, bk, bv, bz, ba, bb = split_columns(proj, COL_WIDTHS)
        o_a = dsa_attention(cq, ckv, kidx, widx, g_cq[l], g_ckv[l], g_kidx[l],
                            w_uq[l], w_iq[l], w_uk[l], w_uv[l], rel_bias)
        o_b = gated_deltanet(bq, bk, bv, bz, ba, bb, conv_w[l], A_log[l], dt_bias[l], g_onorm[l])
        x = x + jnp.einsum('bsc,cd->bsd', jnp.concatenate([o_a, o_b], axis=-1), w_out[l])
        x = x + memory_cross_attention(rmsnorm(x, g_cross[l]), rmsnorm(mem, g_mem[l]),
                                       wq_x[l], wk_x[l], wv_x[l], wo_x[l])
        x = x + peer_ffn(rmsnorm(x, g_ffn[l]), w_pq[l], sub_keys[l], u_emb[l], v_emb[l])
    return final_rmsnorm(x, g_final)
```

```python
import math
from functools import partial
import jax
import jax.numpy as jnp
from jax import lax
import numpy as np
from jax.experimental import pallas as pl
from jax.experimental.pallas import tpu as pltpu

D_MODEL = 1024
BATCH = 4
SEQ = 8192
DEPTH = 1

CHUNK = 64
Q_BLOCK = 128
EPS = 1e-6

A_HEADS = 8
A_HEAD_DIM = 64
A_Q_LORA = 256
A_KV_LORA = 256
IDX_HEADS = 8
IDX_DIM = 64
IDX_TOPK_MAX = 256
ATTN_SCALE = A_HEAD_DIM ** -0.5
IDX_SCALE = (IDX_HEADS * IDX_DIM) ** -0.5

B_HEADS = 4
B_HEAD_DIM = 128
B_QK = B_HEADS * B_HEAD_DIM
CONV_WIDTH = 4

REL_BUCKETS = 32
REL_MAX_DIST = 128

MEM_LEN = 256
X_HEADS = 4
X_HEAD_DIM = 128

P_HEADS = 8
N_KEYS = 128
N_EXPERTS = N_KEYS * N_KEYS
P_TOPK = 16
P_QDIM = 256
P_TOKEN_BLOCK = 128

COL_WIDTHS = (A_Q_LORA, A_KV_LORA, IDX_DIM, IDX_HEADS, B_QK, B_QK, B_QK, B_QK, B_HEADS, B_HEADS)
IN_COLS = sum(COL_WIDTHS)
MIX_WIDTH = A_HEADS * A_HEAD_DIM + B_HEADS * B_HEAD_DIM


def rmsnorm(x, g):
    xf = x.astype(jnp.float32)
    y = xf * lax.rsqrt(jnp.mean(xf * xf, axis=-1, keepdims=True) + EPS)
    return (y * g.astype(jnp.float32)).astype(x.dtype)


def l2norm(x):
    return x * lax.rsqrt(jnp.sum(x * x, axis=-1, keepdims=True) + EPS)


def split_columns(p, widths):
    out, start = [], 0
    for w in widths:
        out.append(p[..., start:start + w])
        start += w
    return out


def t5_bucket(rel):
    half = REL_BUCKETS // 2
    max_exact = half // 2
    n = jnp.abs(rel)
    nf = jnp.maximum(n, max_exact).astype(jnp.float32)
    large = max_exact + (jnp.log(nf / max_exact) / math.log(REL_MAX_DIST / max_exact)
                         * (half - max_exact)).astype(jnp.int32)
    large = jnp.minimum(large, half - 1)
    return jnp.where(rel > 0, half, 0) + jnp.where(n < max_exact, n, large)


def dsa_attention(cq, ckv, kidx, widx, g_cq, g_ckv, g_kidx, w_uq, w_iq, w_uk, w_uv, rel_bias):
    bsz, seq, _ = cq.shape
    topk = min(IDX_TOPK_MAX, seq // 4)
    n_blocks = seq // Q_BLOCK
    cq = rmsnorm(cq, g_cq)
    ckv = rmsnorm(ckv, g_ckv)
    kidx = rmsnorm(kidx, g_kidx).astype(jnp.float32)
    key_pos = jnp.arange(seq, dtype=jnp.int32)

    def blockify(a):
        return jnp.moveaxis(a.reshape(bsz, n_blocks, Q_BLOCK, a.shape[-1]), 1, 0)

    def one_block(args):
        cq_b, w_b, start = args
        qpos = start + jnp.arange(Q_BLOCK, dtype=jnp.int32)
        limit = (qpos // CHUNK + 1) * CHUNK
        admissible = key_pos[None, :] < limit[:, None]
        qi = jnp.einsum('bqc,chd->bqhd', cq_b, w_iq).astype(jnp.float32)
        dots = jax.nn.relu(jnp.einsum('bqhd,bsd->bqhs', qi, kidx))
        score = jnp.einsum('bqh,bqhs->bqs', w_b.astype(jnp.float32) * IDX_SCALE, dots)
        score = jnp.where(admissible[None], score, -jnp.inf)
        _, sel = lax.top_k(score, topk)
        valid = sel < limit[None, :, None]
        c_sel = jax.vmap(lambda c, i: c[i])(ckv, sel)
        q = jnp.einsum('bqc,chd->bqhd', cq_b, w_uq)
        q_lat = jnp.einsum('bqhd,chd->bqhc', q, w_uk)
        logits = jnp.einsum('bqhc,bqkc->bhqk', q_lat, c_sel).astype(jnp.float32) * ATTN_SCALE
        bias = rel_bias[t5_bucket(sel - qpos[None, :, None])].astype(jnp.float32)
        logits = logits + jnp.transpose(bias, (0, 3, 1, 2))
        logits = jnp.where(valid[:, None], logits, -jnp.inf)
        p = jax.nn.softmax(logits, axis=-1).astype(c_sel.dtype)
        o_lat = jnp.einsum('bhqk,bqkc->bqhc', p, c_sel)
        o = jnp.einsum('bqhc,chd->bqhd', o_lat, w_uv)
        return o.reshape(bsz, Q_BLOCK, A_HEADS * A_HEAD_DIM)

    starts = jnp.arange(n_blocks, dtype=jnp.int32) * Q_BLOCK
    o = lax.map(one_block, (blockify(cq), blockify(widx), starts))
    return jnp.moveaxis(o, 0, 1).reshape(bsz, seq, A_HEADS * A_HEAD_DIM)


INT_MIN = -2147483648
NEG_BIG = -1e30
KEY_TILE = 512
KEY_PAD = KEY_TILE - Q_BLOCK
VMEM_LIMIT = 56 * 1024 * 1024


def _rms(x, g):
    return x * lax.rsqrt(jnp.mean(x * x, axis=-1, keepdims=True) + EPS) * g


def _nt_dot(a, b):
    return lax.dot_general(a, b, (((1,), (1,)), ((), ())), preferred_element_type=jnp.float32)


def _dsa_prep_kernel(cq_ref, ckv_ref, kidx_ref, widx_ref, gcq_ref, gckv_ref, gk_ref,
                     wiq_ref, wuq_ref, wukt_ref,
                     a_ref, qlat_ref, ckvn_ref, kidxn_ref, w_ref):
    bf16 = jnp.bfloat16
    cqn = _rms(cq_ref[0], gcq_ref[...]).astype(bf16)
    qi = jnp.dot(cqn, wiq_ref[...], preferred_element_type=jnp.float32)
    q = jnp.dot(cqn, wuq_ref[...], preferred_element_type=jnp.float32)
    for h in range(A_HEADS):
        rows = slice(h * Q_BLOCK, (h + 1) * Q_BLOCK)
        a_ref[0, 0, rows, :] = qi[:, h * IDX_DIM:(h + 1) * IDX_DIM].astype(bf16)
        qh = q[:, h * A_HEAD_DIM:(h + 1) * A_HEAD_DIM].astype(bf16)
        ql = jnp.dot(qh, wukt_ref[h], preferred_element_type=jnp.float32) * ATTN_SCALE
        qlat_ref[0, 0, rows, :] = ql.astype(bf16)
    ckvn_ref[0] = _rms(ckv_ref[0], gckv_ref[...]).astype(bf16)
    kidxn_ref[0] = _rms(kidx_ref[0], gk_ref[...]).astype(bf16)
    w_ref[0, 0] = widx_ref[0] * IDX_SCALE


def _dsa_main_kernel(a_ref, qlat_ref, w_ref, kidx_ref, ckv_ref, bias0_ref, wuv_ref, o_ref,
                     sc_ref, big_ref, p_ref, acc_ref, m_ref, l_ref, alpha_ref, wb_ref, *, topk):
    f32, i32, bf16 = jnp.float32, jnp.int32, jnp.bfloat16
    Q, T = Q_BLOCK, KEY_TILE
    NC = T // 128
    i = pl.program_id(1)
    e = (i + 1) * Q
    nt = (i + T // Q) // (T // Q)
    kf = jnp.float32(topk)

    w_blk = w_ref[0, 0]
    for h in range(IDX_HEADS):
        wb_ref[h] = jnp.broadcast_to(w_blk[:, h:h + 1], (Q, 128))
    row = lax.broadcasted_iota(i32, (Q, 128), 0)
    lane = lax.broadcasted_iota(i32, (Q, 128), 1)
    limit = ((i * Q + row) // CHUNK + 1) * CHUNK
    a_mat = a_ref[0, 0]

    def tile_start(j):
        return pl.multiple_of(e + KEY_PAD - (j + 1) * T, 128)

    def score_tile(j, carry):
        start = tile_start(j)
        big_ref[...] = _nt_dot(a_mat, kidx_ref[0, pl.ds(start, T), :])
        for c in range(NC):
            cols = slice(c * 128, (c + 1) * 128)
            s = jnp.zeros((Q, 128), f32)
            for h in range(IDX_HEADS):
                s = s + wb_ref[h] * jnp.maximum(big_ref[h * Q:(h + 1) * Q, cols], 0.0)
            bits = lax.bitcast_convert_type(s, i32)
            key = jnp.where(bits < 0, bits ^ jnp.int32(0x7FFFFFFF), bits)
            key = jnp.where(s == 0.0, 0, key)
            kpos = start - KEY_PAD + c * 128 + lane
            key = jnp.where(kpos >= 0, key, INT_MIN)
            key = jnp.where(kpos < limit, key, INT_MIN)
            sc_ref[j, :, cols] = key
        return carry

    lax.fori_loop(0, nt, score_tile, 0)

    def count_where(pred):
        def body(j, cnt):
            for c in range(NC):
                cols = slice(c * 128, (c + 1) * 128)
                kpos = tile_start(j) - KEY_PAD + c * 128 + lane
                cnt = cnt + jnp.where(pred(sc_ref[j, :, cols], kpos), 1.0, 0.0)
            return cnt
        cnt = lax.fori_loop(0, nt, body, jnp.zeros((Q, 128), f32))
        return jnp.broadcast_to(jnp.sum(cnt, axis=1, keepdims=True), (Q, 128))

    def bit_body(b, carry):
        u, cacc = carry
        cand = u | lax.shift_left(jnp.int32(1), 31 - b)
        tvec = cand ^ jnp.int32(INT_MIN)
        tot = count_where(lambda k, kpos: k >= tvec)
        ok = tot >= kf
        return jnp.where(ok, cand, u), jnp.where(ok, tot, cacc)

    u, cacc = lax.fori_loop(0, 32, bit_body,
                            (jnp.zeros((Q, 128), i32), jnp.zeros((Q, 128), f32)))
    thr = jnp.maximum(u ^ jnp.int32(INT_MIN), INT_MIN + 1)
    overflow = jnp.where(u != 0, cacc, 0.0) > kf
    n_over = jnp.max(jnp.max(jnp.where(overflow, 1.0, 0.0), axis=1, keepdims=True),
                     axis=0, keepdims=True)[0, 0]

    @pl.when(n_over > 0.0)
    def _():
        need = kf - count_where(lambda k, kpos: k > thr)

        def cut_body(b, cut):
            cand = cut | lax.shift_left(jnp.int32(1), 14 - b)
            cnt = count_where(lambda k, kpos: jnp.where(k == thr, kpos, cand) < cand)
            return jnp.where(cnt <= need, cand, cut)

        cut = lax.fori_loop(0, 15, cut_body, jnp.zeros((Q, 128), i32))

        def drop_tile(j, carry):
            for c in range(NC):
                cols = slice(c * 128, (c + 1) * 128)
                kpos = tile_start(j) - KEY_PAD + c * 128 + lane
                k = sc_ref[j, :, cols]
                drop = jnp.where(k == thr, kpos, -1) >= cut
                sc_ref[j, :, cols] = jnp.where(drop, INT_MIN, k)
            return carry

        lax.fori_loop(0, nt, drop_tile, 0)

    m_ref[...] = jnp.full(m_ref.shape, NEG_BIG, f32)
    l_ref[...] = jnp.zeros(l_ref.shape, f32)
    acc_ref[...] = jnp.zeros(acc_ref.shape, f32)
    qlat = qlat_ref[0, 0]

    def attn_tile(j, with_bias):
        start = tile_start(j)
        kv = ckv_ref[0, pl.ds(start, T), :]
        big_ref[...] = _nt_dot(qlat, kv)
        mask_add = [jnp.where(sc_ref[j, :, c * 128:(c + 1) * 128] >= thr, 0.0, NEG_BIG)
                    for c in range(NC)]
        for h in range(A_HEADS):
            rows = slice(h * Q, (h + 1) * Q)
            lg = []
            for c in range(NC):
                x = big_ref[rows, c * 128:(c + 1) * 128] + mask_add[c]
                if with_bias:
                    x = x + bias0_ref[h, :, c * 128:(c + 1) * 128]
                lg.append(x)
            mx = lg[0]
            for c in range(1, NC):
                mx = jnp.maximum(mx, lg[c])
            m_prev = m_ref[rows, :]
            m_new = jnp.maximum(m_prev, jnp.max(mx, axis=1, keepdims=True))
            psum = jnp.zeros((Q, 128), f32)
            for c in range(NC):
                p = jnp.exp(lg[c] - m_new)
                psum = psum + p
                p_ref[rows, c * 128:(c + 1) * 128] = p.astype(bf16)
            alpha = jnp.exp(m_prev - m_new)
            l_ref[rows, :] = alpha * l_ref[rows, :] + jnp.sum(psum, axis=1, keepdims=True)
            m_ref[rows, :] = m_new
            alpha_ref[rows, :] = alpha
        acc_ref[...] = alpha_ref[...] * acc_ref[...] + jnp.dot(
            p_ref[...], kv, preferred_element_type=f32)

    attn_tile(0, True)

    def attn_body(j, carry):
        attn_tile(j, False)
        return carry

    lax.fori_loop(1, nt, attn_body, 0)

    inv_l = 1.0 / l_ref[...]
    for h in range(A_HEADS):
        rows = slice(h * Q, (h + 1) * Q)
        o_lat = (acc_ref[rows, :] * inv_l[rows, :]).astype(bf16)
        o_ref[0, :, h * A_HEAD_DIM:(h + 1) * A_HEAD_DIM] = jnp.dot(
            o_lat, wuv_ref[h], preferred_element_type=f32)


def dsa_pallas(cq, ckv, kidx, widx, g_cq, g_ckv, g_kidx, w_uq, w_iq, w_uk, w_uv, rel_bias):
    bsz, seq, _ = cq.shape
    f32, bf16 = jnp.float32, jnp.bfloat16
    Q, T, H = Q_BLOCK, KEY_TILE, A_HEADS
    nblk = seq // Q
    topk = min(IDX_TOPK_MAX, seq // 4)
    wiq2 = w_iq.reshape(A_Q_LORA, IDX_HEADS * IDX_DIM).astype(bf16)
    wuq2 = w_uq.reshape(A_Q_LORA, H * A_HEAD_DIM).astype(bf16)
    wukt = jnp.transpose(w_uk, (1, 2, 0)).astype(bf16)
    wuv = jnp.transpose(w_uv, (1, 0, 2)).astype(bf16)
    tok = lambda w: pl.BlockSpec((1, Q, w), lambda b, t: (b, t, 0))
    full = lambda shape: pl.BlockSpec(shape, lambda b, t: (0,) * len(shape))
    a_mat, qlat, ckvn, kidxn, wsc = pl.pallas_call(
        _dsa_prep_kernel,
        grid=(bsz, nblk),
        in_specs=[tok(A_Q_LORA), tok(A_KV_LORA), tok(IDX_DIM), tok(IDX_HEADS),
                  full((1, A_Q_LORA)), full((1, A_KV_LORA)), full((1, IDX_DIM)),
                  full(wiq2.shape), full(wuq2.shape), full(wukt.shape)],
        out_specs=[pl.BlockSpec((1, 1, H * Q, IDX_DIM), lambda b, t: (b, t, 0, 0)),
                   pl.BlockSpec((1, 1, H * Q, A_KV_LORA), lambda b, t: (b, t, 0, 0)),
                   tok(A_KV_LORA), tok(IDX_DIM),
                   pl.BlockSpec((1, 1, Q, IDX_HEADS), lambda b, t: (b, t, 0, 0))],
        out_shape=[jax.ShapeDtypeStruct((bsz, nblk, H * Q, IDX_DIM), bf16),
                   jax.ShapeDtypeStruct((bsz, nblk, H * Q, A_KV_LORA), bf16),
                   jax.ShapeDtypeStruct((bsz, seq, A_KV_LORA), bf16),
                   jax.ShapeDtypeStruct((bsz, seq, IDX_DIM), bf16),
                   jax.ShapeDtypeStruct((bsz, nblk, Q, IDX_HEADS), f32)],
        name="dsa_prep",
    )(cq, ckv, kidx, widx, g_cq.reshape(1, -1), g_ckv.reshape(1, -1), g_kidx.reshape(1, -1),
      wiq2, wuq2, wukt)
    ckvp = jnp.pad(ckvn, ((0, 0), (KEY_PAD, 0), (0, 0)))
    kidxp = jnp.pad(kidxn, ((0, 0), (KEY_PAD, 0), (0, 0)))
    rel = (jnp.arange(T, dtype=jnp.int32)[None, :] - jnp.arange(Q, dtype=jnp.int32)[:, None]
           - KEY_PAD)
    far = rel_bias[t5_bucket(jnp.int32(-KEY_PAD - 1))]
    bias0 = jnp.transpose(rel_bias[t5_bucket(rel)] - far, (2, 0, 1)).astype(f32)
    nt_max = (nblk - 1 + T // Q) // (T // Q)
    skey = seq + KEY_PAD
    out = pl.pallas_call(
        partial(_dsa_main_kernel, topk=topk),
        grid=(bsz, nblk),
        in_specs=[pl.BlockSpec((1, 1, H * Q, IDX_DIM), lambda b, t: (b, t, 0, 0)),
                  pl.BlockSpec((1, 1, H * Q, A_KV_LORA), lambda b, t: (b, t, 0, 0)),
                  pl.BlockSpec((1, 1, Q, IDX_HEADS), lambda b, t: (b, t, 0, 0)),
                  pl.BlockSpec((1, skey, IDX_DIM), lambda b, t: (b, 0, 0)),
                  pl.BlockSpec((1, skey, A_KV_LORA), lambda b, t: (b, 0, 0)),
                  full(bias0.shape), full(wuv.shape)],
        out_specs=tok(H * A_HEAD_DIM),
        out_shape=jax.ShapeDtypeStruct((bsz, seq, H * A_HEAD_DIM), f32),
        scratch_shapes=[pltpu.VMEM((nt_max, Q, T), jnp.int32),
                        pltpu.VMEM((H * Q, T), f32),
                        pltpu.VMEM((H * Q, T), bf16),
                        pltpu.VMEM((H * Q, A_KV_LORA), f32),
                        pltpu.VMEM((H * Q, 1), f32),
                        pltpu.VMEM((H * Q, 1), f32),
                        pltpu.VMEM((H * Q, 1), f32),
                        pltpu.VMEM((IDX_HEADS, Q, 128), f32)],
        compiler_params=pltpu.CompilerParams(
            dimension_semantics=("arbitrary", "arbitrary"), vmem_limit_bytes=VMEM_LIMIT),
        name="dsa_main",
    )(a_mat, qlat, wsc, kidxp, ckvp, bias0, wuv)
    return out


def causal_dwconv(x, w):
    ch = x.shape[-1]
    return lax.conv_general_dilated(x, w[:, None, :].astype(x.dtype), window_strides=(1,),
                                    padding=[(CONV_WIDTH - 1, 0)],
                                    dimension_numbers=('NWC', 'WIO', 'NWC'),
                                    feature_group_count=ch)


def gated_deltanet(q, k, v, z, a, b, conv_w, A_log, dt_bias, g_onorm):
    bsz, seq, _ = q.shape
    f32 = jnp.float32
    H, Dh = B_HEADS, B_HEAD_DIM
    nc = seq // CHUNK
    qkv = jax.nn.silu(causal_dwconv(jnp.concatenate([q, k, v], axis=-1), conv_w)).astype(f32)
    q, k, v = [t.reshape(bsz, seq, H, Dh) for t in jnp.split(qkv, 3, axis=-1)]
    q = l2norm(q) * (Dh ** -0.5)
    k = l2norm(k)
    beta = jax.nn.sigmoid(b.astype(f32))
    g = -jnp.exp(A_log.astype(f32)) * jax.nn.softplus(a.astype(f32) + dt_bias.astype(f32))

    def to_chunks(t):
        t = t.reshape(bsz, nc, CHUNK, *t.shape[2:])
        return jnp.moveaxis(t, 3, 1)

    q, k, v, beta = to_chunks(q), to_chunks(k), to_chunks(v), to_chunks(beta)
    g = jnp.cumsum(to_chunks(g), axis=-1)
    pos = jnp.arange(CHUNK)
    causal = pos[:, None] >= pos[None, :]
    strict = pos[:, None] > pos[None, :]
    decay = jnp.exp(jnp.where(causal, g[..., :, None] - g[..., None, :], -jnp.inf))
    m = jnp.where(strict, beta[..., :, None] * jnp.einsum('bhnid,bhnjd->bhnij', k, k) * decay, 0.0)
    rhs = jnp.concatenate([v * beta[..., None], k * (beta * jnp.exp(g))[..., None]], axis=-1)
    sol = lax.linalg.triangular_solve(m, rhs, left_side=True, lower=True, unit_diagonal=True)
    u, w = sol[..., :Dh], sol[..., Dh:]
    intra = jnp.einsum('bhnid,bhnjd->bhnij', q, k) * decay
    q_dec = q * jnp.exp(g)[..., None]
    k_tail = k * jnp.exp(g[..., -1:] - g)[..., None]
    chunk_decay = jnp.exp(g[..., -1])

    def step(state, inp):
        q_c, u_c, w_c, a_c, kt_c, d_c = inp
        v_new = u_c - jnp.einsum('bhcd,bhde->bhce', w_c, state)
        o_c = jnp.einsum('bhcd,bhde->bhce', q_c, state) + jnp.einsum('bhij,bhje->bhie', a_c, v_new)
        state = state * d_c[..., None, None] + jnp.einsum('bhcd,bhce->bhde', kt_c, v_new)
        return state, o_c

    xs = tuple(jnp.moveaxis(t, 2, 0) for t in (q_dec, u, w, intra, k_tail, chunk_decay))
    _, o = lax.scan(step, jnp.zeros((bsz, H, Dh, Dh), f32), xs)
    o = jnp.transpose(o, (1, 0, 3, 2, 4)).reshape(bsz, seq, H, Dh)
    o = o * lax.rsqrt(jnp.mean(o * o, axis=-1, keepdims=True) + EPS) * g_onorm.astype(f32)
    o = o * jax.nn.silu(z.astype(f32).reshape(bsz, seq, H, Dh))
    return o.reshape(bsz, seq, H * Dh).astype(z.dtype)


def memory_cross_attention(h, mem_n, wq, wk, wv, wo):
    q = jnp.einsum('bsd,dhe->bhse', h, wq)
    k = jnp.einsum('bmd,dhe->bhme', mem_n, wk)
    v = jnp.einsum('bmd,dhe->bhme', mem_n, wv)
    logits = jnp.einsum('bhse,bhme->bhsm', q, k).astype(jnp.float32) * (X_HEAD_DIM ** -0.5)
    p = jax.nn.softmax(logits, axis=-1).astype(v.dtype)
    o = jnp.einsum('bhsm,bhme->bshe', p, v)
    return jnp.einsum('bshe,hed->bsd', o, wo)


def peer_ffn(h, w_pq, sub_keys, u_emb, v_emb):
    bsz, seq, d = h.shape
    n_tok = bsz * seq
    hb = h.reshape(n_tok // P_TOKEN_BLOCK, P_TOKEN_BLOCK, d)

    def one_block(xb):
        n = xb.shape[0]
        qr = jnp.einsum('nd,dhe->nhe', xb, w_pq).reshape(n, P_HEADS, 2, P_QDIM // 2)
        s = jnp.einsum('nhpe,hpke->nhpk', qr, sub_keys).astype(jnp.float32)
        s1, i1 = lax.top_k(s[:, :, 0], P_TOPK)
        s2, i2 = lax.top_k(s[:, :, 1], P_TOPK)
        cand = (s1[..., :, None] + s2[..., None, :]).reshape(n, P_HEADS, P_TOPK * P_TOPK)
        cidx = (i1[..., :, None] * N_KEYS + i2[..., None, :]).reshape(n, P_HEADS, P_TOPK * P_TOPK)
        best, pos = lax.top_k(cand, P_TOPK)
        eidx = jnp.take_along_axis(cidx, pos, axis=-1)
        gate = jax.nn.softmax(best, axis=-1).astype(xb.dtype)
        act = jax.nn.gelu(jnp.einsum('nhkd,nd->nhk', u_emb[eidx], xb), approximate=False)
        return jnp.einsum('nhk,nhkd->nd', gate * act, v_emb[eidx])

    return lax.map(one_block, hb).reshape(bsz, seq, d)


def _final_rmsnorm_kernel(x_ref, g_ref, o_ref):
    xf = x_ref[...]
    y = xf * lax.rsqrt(jnp.mean(xf * xf, axis=-1, keepdims=True) + EPS)
    o_ref[...] = y * g_ref[...]


def final_rmsnorm(x, g):
    bsz, seq, d = x.shape
    n = bsz * seq
    tm = 512
    out = pl.pallas_call(
        _final_rmsnorm_kernel,
        grid=(n // tm,),
        in_specs=[pl.BlockSpec((tm, d), lambda i: (i, 0)),
                  pl.BlockSpec((1, d), lambda i: (0, 0))],
        out_specs=pl.BlockSpec((tm, d), lambda i: (i, 0)),
        out_shape=jax.ShapeDtypeStruct((n, d), x.dtype),
        name="final_rmsnorm",
    )(x.reshape(n, d), g.reshape(1, d))
    return out.reshape(bsz, seq, d)


def kernel(x, mem, g_mix, w_in, g_cq, g_ckv, g_kidx, w_uq, w_iq, w_uk, w_uv, rel_bias, conv_w, A_log, dt_bias, g_onorm, w_out, g_cross, g_mem, wq_x, wk_x, wv_x, wo_x, g_ffn, w_pq, sub_keys, u_emb, v_emb, g_final):
    for l in range(DEPTH):
        h = rmsnorm(x, g_mix[l])
        proj = jnp.einsum('bsd,dc->bsc', h, w_in[l])
        cq, ckv, kidx, widx, bq, bk, bv, bz, ba, bb = split_columns(proj, COL_WIDTHS)
        o_a = dsa_pallas(cq, ckv, kidx, widx, g_cq[l], g_ckv[l], g_kidx[l],
                         w_uq[l], w_iq[l], w_uk[l], w_uv[l], rel_bias)
        o_b = gated_deltanet(bq, bk, bv, bz, ba, bb, conv_w[l], A_log[l], dt_bias[l], g_onorm[l])
        x = x + jnp.einsum('bsc,cd->bsd', jnp.concatenate([o_a, o_b], axis=-1), w_out[l])
        x = x + memory_cross_attention(rmsnorm(x, g_cross[l]), rmsnorm(mem, g_mem[l]),
                                       wq_x[l], wk_x[l], wv_x[l], wo_x[l])
        x = x + peer_ffn(rmsnorm(x, g_ffn[l]), w_pq[l], sub_keys[l], u_emb[l], v_emb[l])
    return final_rmsnorm(x, g_final)
```

```python
import math
from functools import partial
import jax
import jax.numpy as jnp
from jax import lax
import numpy as np
from jax.experimental import pallas as pl
from jax.experimental.pallas import tpu as pltpu

D_MODEL = 1024
BATCH = 4
SEQ = 8192
DEPTH = 1

CHUNK = 64
Q_BLOCK = 128
EPS = 1e-6

A_HEADS = 8
A_HEAD_DIM = 64
A_Q_LORA = 256
A_KV_LORA = 256
IDX_HEADS = 8
IDX_DIM = 64
IDX_TOPK_MAX = 256
ATTN_SCALE = A_HEAD_DIM ** -0.5
IDX_SCALE = (IDX_HEADS * IDX_DIM) ** -0.5

B_HEADS = 4
B_HEAD_DIM = 128
B_QK = B_HEADS * B_HEAD_DIM
CONV_WIDTH = 4

REL_BUCKETS = 32
REL_MAX_DIST = 128

MEM_LEN = 256
X_HEADS = 4
X_HEAD_DIM = 128

P_HEADS = 8
N_KEYS = 128
N_EXPERTS = N_KEYS * N_KEYS
P_TOPK = 16
P_QDIM = 256
P_TOKEN_BLOCK = 128

COL_WIDTHS = (A_Q_LORA, A_KV_LORA, IDX_DIM, IDX_HEADS, B_QK, B_QK, B_QK, B_QK, B_HEADS, B_HEADS)
IN_COLS = sum(COL_WIDTHS)
MIX_WIDTH = A_HEADS * A_HEAD_DIM + B_HEADS * B_HEAD_DIM


def rmsnorm(x, g):
    xf = x.astype(jnp.float32)
    y = xf * lax.rsqrt(jnp.mean(xf * xf, axis=-1, keepdims=True) + EPS)
    return (y * g.astype(jnp.float32)).astype(x.dtype)


def l2norm(x):
    return x * lax.rsqrt(jnp.sum(x * x, axis=-1, keepdims=True) + EPS)


def split_columns(p, widths):
    out, start = [], 0
    for w in widths:
        out.append(p[..., start:start + w])
        start += w
    return out


def t5_bucket(rel):
    half = REL_BUCKETS // 2
    max_exact = half // 2
    n = jnp.abs(rel)
    nf = jnp.maximum(n, max_exact).astype(jnp.float32)
    large = max_exact + (jnp.log(nf / max_exact) / math.log(REL_MAX_DIST / max_exact)
                         * (half - max_exact)).astype(jnp.int32)
    large = jnp.minimum(large, half - 1)
    return jnp.where(rel > 0, half, 0) + jnp.where(n < max_exact, n, large)


def dsa_attention(cq, ckv, kidx, widx, g_cq, g_ckv, g_kidx, w_uq, w_iq, w_uk, w_uv, rel_bias):
    bsz, seq, _ = cq.shape
    topk = min(IDX_TOPK_MAX, seq // 4)
    n_blocks = seq // Q_BLOCK
    cq = rmsnorm(cq, g_cq)
    ckv = rmsnorm(ckv, g_ckv)
    kidx = rmsnorm(kidx, g_kidx).astype(jnp.float32)
    key_pos = jnp.arange(seq, dtype=jnp.int32)

    def blockify(a):
        return jnp.moveaxis(a.reshape(bsz, n_blocks, Q_BLOCK, a.shape[-1]), 1, 0)

    def one_block(args):
        cq_b, w_b, start = args
        qpos = start + jnp.arange(Q_BLOCK, dtype=jnp.int32)
        limit = (qpos // CHUNK + 1) * CHUNK
        admissible = key_pos[None, :] < limit[:, None]
        qi = jnp.einsum('bqc,chd->bqhd', cq_b, w_iq).astype(jnp.float32)
        dots = jax.nn.relu(jnp.einsum('bqhd,bsd->bqhs', qi, kidx))
        score = jnp.einsum('bqh,bqhs->bqs', w_b.astype(jnp.float32) * IDX_SCALE, dots)
        score = jnp.where(admissible[None], score, -jnp.inf)
        _, sel = lax.top_k(score, topk)
        valid = sel < limit[None, :, None]
        c_sel = jax.vmap(lambda c, i: c[i])(ckv, sel)
        q = jnp.einsum('bqc,chd->bqhd', cq_b, w_uq)
        q_lat = jnp.einsum('bqhd,chd->bqhc', q, w_uk)
        logits = jnp.einsum('bqhc,bqkc->bhqk', q_lat, c_sel).astype(jnp.float32) * ATTN_SCALE
        bias = rel_bias[t5_bucket(sel - qpos[None, :, None])].astype(jnp.float32)
        logits = logits + jnp.transpose(bias, (0, 3, 1, 2))
        logits = jnp.where(valid[:, None], logits, -jnp.inf)
        p = jax.nn.softmax(logits, axis=-1).astype(c_sel.dtype)
        o_lat = jnp.einsum('bhqk,bqkc->bqhc', p, c_sel)
        o = jnp.einsum('bqhc,chd->bqhd', o_lat, w_uv)
        return o.reshape(bsz, Q_BLOCK, A_HEADS * A_HEAD_DIM)

    starts = jnp.arange(n_blocks, dtype=jnp.int32) * Q_BLOCK
    o = lax.map(one_block, (blockify(cq), blockify(widx), starts))
    return jnp.moveaxis(o, 0, 1).reshape(bsz, seq, A_HEADS * A_HEAD_DIM)


INT_MIN = -2147483648
NEG_BIG = -1e30
KEY_TILE = 512
KEY_PAD = KEY_TILE - Q_BLOCK
VMEM_LIMIT = 56 * 1024 * 1024


def _rms(x, g):
    return x * lax.rsqrt(jnp.mean(x * x, axis=-1, keepdims=True) + EPS) * g


def _nt_dot(a, b):
    return lax.dot_general(a, b, (((1,), (1,)), ((), ())), preferred_element_type=jnp.float32)


def _dsa_prep_kernel(cq_ref, ckv_ref, kidx_ref, widx_ref, gcq_ref, gckv_ref, gk_ref,
                     wiq_ref, wuq_ref, wukt_ref,
                     a_ref, qlat_ref, ckvn_ref, kidxn_ref, w_ref):
    bf16 = jnp.bfloat16
    cqn = _rms(cq_ref[0], gcq_ref[...]).astype(bf16)
    qi = jnp.dot(cqn, wiq_ref[...], preferred_element_type=jnp.float32)
    q = jnp.dot(cqn, wuq_ref[...], preferred_element_type=jnp.float32)
    for h in range(A_HEADS):
        rows = slice(h * Q_BLOCK, (h + 1) * Q_BLOCK)
        a_ref[0, 0, rows, :] = qi[:, h * IDX_DIM:(h + 1) * IDX_DIM].astype(bf16)
        qh = q[:, h * A_HEAD_DIM:(h + 1) * A_HEAD_DIM].astype(bf16)
        ql = jnp.dot(qh, wukt_ref[h], preferred_element_type=jnp.float32) * ATTN_SCALE
        qlat_ref[0, 0, rows, :] = ql.astype(bf16)
    ckvn_ref[0] = _rms(ckv_ref[0], gckv_ref[...]).astype(bf16)
    kidxn_ref[0] = _rms(kidx_ref[0], gk_ref[...]).astype(bf16)
    w_ref[0, 0] = widx_ref[0] * IDX_SCALE


def _dsa_main_kernel(a_ref, qlat_ref, w_ref, kidx_ref, ckv_ref, bias0_ref, wuv_ref, o_ref,
                     sc_ref, big_ref, p_ref, acc_ref, m_ref, l_ref, alpha_ref, wb_ref, *, topk):
    f32, i32, bf16 = jnp.float32, jnp.int32, jnp.bfloat16
    Q, T = Q_BLOCK, KEY_TILE
    NC = T // 128
    i = pl.program_id(1)
    e = (i + 1) * Q
    nt = (i + T // Q) // (T // Q)
    kf = jnp.float32(topk)

    w_blk = w_ref[0, 0]
    for h in range(IDX_HEADS):
        wb_ref[h] = jnp.broadcast_to(w_blk[:, h:h + 1], (Q, 128))
    row = lax.broadcasted_iota(i32, (Q, 128), 0)
    lane = lax.broadcasted_iota(i32, (Q, 128), 1)
    limit = ((i * Q + row) // CHUNK + 1) * CHUNK
    a_mat = a_ref[0, 0]

    def tile_start(j):
        return pl.multiple_of(e + KEY_PAD - (j + 1) * T, 128)

    def score_tile(j, carry):
        start = tile_start(j)
        big_ref[...] = _nt_dot(a_mat, kidx_ref[0, pl.ds(start, T), :])
        for c in range(NC):
            cols = slice(c * 128, (c + 1) * 128)
            s = jnp.zeros((Q, 128), f32)
            for h in range(IDX_HEADS):
                s = s + wb_ref[h] * jnp.maximum(big_ref[h * Q:(h + 1) * Q, cols], 0.0)
            bits = lax.bitcast_convert_type(s, i32)
            key = jnp.where(bits < 0, bits ^ jnp.int32(0x7FFFFFFF), bits)
            key = jnp.where(s == 0.0, 0, key)
            kpos = start - KEY_PAD + c * 128 + lane
            key = jnp.where(kpos >= 0, key, INT_MIN)
            key = jnp.where(kpos < limit, key, INT_MIN)
            sc_ref[j, :, cols] = key
        return carry

    lax.fori_loop(0, nt, score_tile, 0)

    def count_where(pred):
        def body(j, cnt):
            for c in range(NC):
                cols = slice(c * 128, (c + 1) * 128)
                kpos = tile_start(j) - KEY_PAD + c * 128 + lane
                cnt = cnt + jnp.where(pred(sc_ref[j, :, cols], kpos), 1.0, 0.0)
            return cnt
        cnt = lax.fori_loop(0, nt, body, jnp.zeros((Q, 128), f32))
        return jnp.broadcast_to(jnp.sum(cnt, axis=1, keepdims=True), (Q, 128))

    def bit_body(b, carry):
        u, cacc = carry
        cand = u | lax.shift_left(jnp.int32(1), 31 - b)
        tvec = cand ^ jnp.int32(INT_MIN)
        tot = count_where(lambda k, kpos: k >= tvec)
        ok = tot >= kf
        return jnp.where(ok, cand, u), jnp.where(ok, tot, cacc)

    u, cacc = lax.fori_loop(0, 32, bit_body,
                            (jnp.zeros((Q, 128), i32), jnp.zeros((Q, 128), f32)))
    thr = jnp.maximum(u ^ jnp.int32(INT_MIN), INT_MIN + 1)
    overflow = jnp.where(u != 0, cacc, 0.0) > kf
    n_over = jnp.max(jnp.max(jnp.where(overflow, 1.0, 0.0), axis=1, keepdims=True),
                     axis=0, keepdims=True)[0, 0]

    @pl.when(n_over > 0.0)
    def _():
        need = kf - count_where(lambda k, kpos: k > thr)

        def cut_body(b, cut):
            cand = cut | lax.shift_left(jnp.int32(1), 14 - b)
            cnt = count_where(lambda k, kpos: jnp.where(k == thr, kpos, cand) < cand)
            return jnp.where(cnt <= need, cand, cut)

        cut = lax.fori_loop(0, 15, cut_body, jnp.zeros((Q, 128), i32))

        def drop_tile(j, carry):
            for c in range(NC):
                cols = slice(c * 128, (c + 1) * 128)
                kpos = tile_start(j) - KEY_PAD + c * 128 + lane
                k = sc_ref[j, :, cols]
                drop = jnp.where(k == thr, kpos, -1) >= cut
                sc_ref[j, :, cols] = jnp.where(drop, INT_MIN, k)
            return carry

        lax.fori_loop(0, nt, drop_tile, 0)

    m_ref[...] = jnp.full(m_ref.shape, NEG_BIG, f32)
    l_ref[...] = jnp.zeros(l_ref.shape, f32)
    acc_ref[...] = jnp.zeros(acc_ref.shape, f32)
    qlat = qlat_ref[0, 0]

    def attn_tile(j, with_bias):
        start = tile_start(j)
        kv = ckv_ref[0, pl.ds(start, T), :]
        big_ref[...] = _nt_dot(qlat, kv)
        mask_add = [jnp.where(sc_ref[j, :, c * 128:(c + 1) * 128] >= thr, 0.0, NEG_BIG)
                    for c in range(NC)]
        for h in range(A_HEADS):
            rows = slice(h * Q, (h + 1) * Q)
            lg = []
            for c in range(NC):
                x = big_ref[rows, c * 128:(c + 1) * 128] + mask_add[c]
                if with_bias:
                    x = x + bias0_ref[h, :, c * 128:(c + 1) * 128]
                lg.append(x)
            mx = lg[0]
            for c in range(1, NC):
                mx = jnp.maximum(mx, lg[c])
            m_prev = m_ref[rows, :]
            m_new = jnp.maximum(m_prev, jnp.max(mx, axis=1, keepdims=True))
            psum = jnp.zeros((Q, 128), f32)
            for c in range(NC):
                p = jnp.exp(lg[c] - m_new)
                psum = psum + p
                p_ref[rows, c * 128:(c + 1) * 128] = p.astype(bf16)
            alpha = jnp.exp(m_prev - m_new)
            l_ref[rows, :] = alpha * l_ref[rows, :] + jnp.sum(psum, axis=1, keepdims=True)
            m_ref[rows, :] = m_new
            alpha_ref[rows, :] = alpha
        acc_ref[...] = alpha_ref[...] * acc_ref[...] + jnp.dot(
            p_ref[...], kv, preferred_element_type=f32)

    attn_tile(0, True)

    def attn_body(j, carry):
        attn_tile(j, False)
        return carry

    lax.fori_loop(1, nt, attn_body, 0)

    inv_l = 1.0 / l_ref[...]
    for h in range(A_HEADS):
        rows = slice(h * Q, (h + 1) * Q)
        o_lat = (acc_ref[rows, :] * inv_l[rows, :]).astype(bf16)
        o_ref[0, :, h * A_HEAD_DIM:(h + 1) * A_HEAD_DIM] = jnp.dot(
            o_lat, wuv_ref[h], preferred_element_type=f32)


def dsa_pallas(cq, ckv, kidx, widx, g_cq, g_ckv, g_kidx, w_uq, w_iq, w_uk, w_uv, rel_bias):
    bsz, seq, _ = cq.shape
    f32, bf16 = jnp.float32, jnp.bfloat16
    Q, T, H = Q_BLOCK, KEY_TILE, A_HEADS
    nblk = seq // Q
    topk = min(IDX_TOPK_MAX, seq // 4)
    wiq2 = w_iq.reshape(A_Q_LORA, IDX_HEADS * IDX_DIM).astype(bf16)
    wuq2 = w_uq.reshape(A_Q_LORA, H * A_HEAD_DIM).astype(bf16)
    wukt = jnp.transpose(w_uk, (1, 2, 0)).astype(bf16)
    wuv = jnp.transpose(w_uv, (1, 0, 2)).astype(bf16)
    tok = lambda w: pl.BlockSpec((1, Q, w), lambda b, t: (b, t, 0))
    full = lambda shape: pl.BlockSpec(shape, lambda b, t: (0,) * len(shape))
    a_mat, qlat, ckvn, kidxn, wsc = pl.pallas_call(
        _dsa_prep_kernel,
        grid=(bsz, nblk),
        in_specs=[tok(A_Q_LORA), tok(A_KV_LORA), tok(IDX_DIM), tok(IDX_HEADS),
                  full((1, A_Q_LORA)), full((1, A_KV_LORA)), full((1, IDX_DIM)),
                  full(wiq2.shape), full(wuq2.shape), full(wukt.shape)],
        out_specs=[pl.BlockSpec((1, 1, H * Q, IDX_DIM), lambda b, t: (b, t, 0, 0)),
                   pl.BlockSpec((1, 1, H * Q, A_KV_LORA), lambda b, t: (b, t, 0, 0)),
                   tok(A_KV_LORA), tok(IDX_DIM),
                   pl.BlockSpec((1, 1, Q, IDX_HEADS), lambda b, t: (b, t, 0, 0))],
        out_shape=[jax.ShapeDtypeStruct((bsz, nblk, H * Q, IDX_DIM), bf16),
                   jax.ShapeDtypeStruct((bsz, nblk, H * Q, A_KV_LORA), bf16),
                   jax.ShapeDtypeStruct((bsz, seq, A_KV_LORA), bf16),
                   jax.ShapeDtypeStruct((bsz, seq, IDX_DIM), bf16),
                   jax.ShapeDtypeStruct((bsz, nblk, Q, IDX_HEADS), f32)],
        name="dsa_prep",
    )(cq, ckv, kidx, widx, g_cq.reshape(1, -1), g_ckv.reshape(1, -1), g_kidx.reshape(1, -1),
      wiq2, wuq2, wukt)
    ckvp = jnp.pad(ckvn, ((0, 0), (KEY_PAD, 0), (0, 0)))
    kidxp = jnp.pad(kidxn, ((0, 0), (KEY_PAD, 0), (0, 0)))
    rel = (jnp.arange(T, dtype=jnp.int32)[None, :] - jnp.arange(Q, dtype=jnp.int32)[:, None]
           - KEY_PAD)
    far = rel_bias[t5_bucket(jnp.int32(-KEY_PAD - 1))]
    bias0 = jnp.transpose(rel_bias[t5_bucket(rel)] - far, (2, 0, 1)).astype(f32)
    nt_max = (nblk - 1 + T // Q) // (T // Q)
    skey = seq + KEY_PAD
    out = pl.pallas_call(
        partial(_dsa_main_kernel, topk=topk),
        grid=(bsz, nblk),
        in_specs=[pl.BlockSpec((1, 1, H * Q, IDX_DIM), lambda b, t: (b, t, 0, 0)),
                  pl.BlockSpec((1, 1, H * Q, A_KV_LORA), lambda b, t: (b, t, 0, 0)),
                  pl.BlockSpec((1, 1, Q, IDX_HEADS), lambda b, t: (b, t, 0, 0)),
                  pl.BlockSpec((1, skey, IDX_DIM), lambda b, t: (b, 0, 0)),
                  pl.BlockSpec((1, skey, A_KV_LORA), lambda b, t: (b, 0, 0)),
                  full(bias0.shape), full(wuv.shape)],
        out_specs=tok(H * A_HEAD_DIM),
        out_shape=jax.ShapeDtypeStruct((bsz, seq, H * A_HEAD_DIM), f32),
        scratch_shapes=[pltpu.VMEM((nt_max, Q, T), jnp.int32),
                        pltpu.VMEM((H * Q, T), f32),
                        pltpu.VMEM((H * Q, T), bf16),
                        pltpu.VMEM((H * Q, A_KV_LORA), f32),
                        pltpu.VMEM((H * Q, 1), f32),
                        pltpu.VMEM((H * Q, 1), f32),
                        pltpu.VMEM((H * Q, 1), f32),
                        pltpu.VMEM((IDX_HEADS, Q, 128), f32)],
        compiler_params=pltpu.CompilerParams(
            dimension_semantics=("arbitrary", "arbitrary"), vmem_limit_bytes=VMEM_LIMIT),
        name="dsa_main",
    )(a_mat, qlat, wsc, kidxp, ckvp, bias0, wuv)
    return out


def causal_dwconv(x, w):
    ch = x.shape[-1]
    return lax.conv_general_dilated(x, w[:, None, :].astype(x.dtype), window_strides=(1,),
                                    padding=[(CONV_WIDTH - 1, 0)],
                                    dimension_numbers=('NWC', 'WIO', 'NWC'),
                                    feature_group_count=ch)


def gated_deltanet(q, k, v, z, a, b, conv_w, A_log, dt_bias, g_onorm):
    bsz, seq, _ = q.shape
    f32 = jnp.float32
    H, Dh = B_HEADS, B_HEAD_DIM
    nc = seq // CHUNK
    qkv = jax.nn.silu(causal_dwconv(jnp.concatenate([q, k, v], axis=-1), conv_w)).astype(f32)
    q, k, v = [t.reshape(bsz, seq, H, Dh) for t in jnp.split(qkv, 3, axis=-1)]
    q = l2norm(q) * (Dh ** -0.5)
    k = l2norm(k)
    beta = jax.nn.sigmoid(b.astype(f32))
    g = -jnp.exp(A_log.astype(f32)) * jax.nn.softplus(a.astype(f32) + dt_bias.astype(f32))

    def to_chunks(t):
        t = t.reshape(bsz, nc, CHUNK, *t.shape[2:])
        return jnp.moveaxis(t, 3, 1)

    q, k, v, beta = to_chunks(q), to_chunks(k), to_chunks(v), to_chunks(beta)
    g = jnp.cumsum(to_chunks(g), axis=-1)
    pos = jnp.arange(CHUNK)
    causal = pos[:, None] >= pos[None, :]
    strict = pos[:, None] > pos[None, :]
    decay = jnp.exp(jnp.where(causal, g[..., :, None] - g[..., None, :], -jnp.inf))
    m = jnp.where(strict, beta[..., :, None] * jnp.einsum('bhnid,bhnjd->bhnij', k, k) * decay, 0.0)
    rhs = jnp.concatenate([v * beta[..., None], k * (beta * jnp.exp(g))[..., None]], axis=-1)
    sol = lax.linalg.triangular_solve(m, rhs, left_side=True, lower=True, unit_diagonal=True)
    u, w = sol[..., :Dh], sol[..., Dh:]
    intra = jnp.einsum('bhnid,bhnjd->bhnij', q, k) * decay
    q_dec = q * jnp.exp(g)[..., None]
    k_tail = k * jnp.exp(g[..., -1:] - g)[..., None]
    chunk_decay = jnp.exp(g[..., -1])

    def step(state, inp):
        q_c, u_c, w_c, a_c, kt_c, d_c = inp
        v_new = u_c - jnp.einsum('bhcd,bhde->bhce', w_c, state)
        o_c = jnp.einsum('bhcd,bhde->bhce', q_c, state) + jnp.einsum('bhij,bhje->bhie', a_c, v_new)
        state = state * d_c[..., None, None] + jnp.einsum('bhcd,bhce->bhde', kt_c, v_new)
        return state, o_c

    xs = tuple(jnp.moveaxis(t, 2, 0) for t in (q_dec, u, w, intra, k_tail, chunk_decay))
    _, o = lax.scan(step, jnp.zeros((bsz, H, Dh, Dh), f32), xs)
    o = jnp.transpose(o, (1, 0, 3, 2, 4)).reshape(bsz, seq, H, Dh)
    o = o * lax.rsqrt(jnp.mean(o * o, axis=-1, keepdims=True) + EPS) * g_onorm.astype(f32)
    o = o * jax.nn.silu(z.astype(f32).reshape(bsz, seq, H, Dh))
    return o.reshape(bsz, seq, H * Dh).astype(z.dtype)


def memory_cross_attention(h, mem_n, wq, wk, wv, wo):
    q = jnp.einsum('bsd,dhe->bhse', h, wq)
    k = jnp.einsum('bmd,dhe->bhme', mem_n, wk)
    v = jnp.einsum('bmd,dhe->bhme', mem_n, wv)
    logits = jnp.einsum('bhse,bhme->bhsm', q, k).astype(jnp.float32) * (X_HEAD_DIM ** -0.5)
    p = jax.nn.softmax(logits, axis=-1).astype(v.dtype)
    o = jnp.einsum('bhsm,bhme->bshe', p, v)
    return jnp.einsum('bshe,hed->bsd', o, wo)


def peer_ffn(h, w_pq, sub_keys, u_emb, v_emb):
    bsz, seq, d = h.shape
    n_tok = bsz * seq
    hb = h.reshape(n_tok // P_TOKEN_BLOCK, P_TOKEN_BLOCK, d)

    def one_block(xb):
        n = xb.shape[0]
        qr = jnp.einsum('nd,dhe->nhe', xb, w_pq).reshape(n, P_HEADS, 2, P_QDIM // 2)
        s = jnp.einsum('nhpe,hpke->nhpk', qr, sub_keys).astype(jnp.float32)
        s1, i1 = lax.top_k(s[:, :, 0], P_TOPK)
        s2, i2 = lax.top_k(s[:, :, 1], P_TOPK)
        cand = (s1[..., :, None] + s2[..., None, :]).reshape(n, P_HEADS, P_TOPK * P_TOPK)
        cidx = (i1[..., :, None] * N_KEYS + i2[..., None, :]).reshape(n, P_HEADS, P_TOPK * P_TOPK)
        best, pos = lax.top_k(cand, P_TOPK)
        eidx = jnp.take_along_axis(cidx, pos, axis=-1)
        gate = jax.nn.softmax(best, axis=-1).astype(xb.dtype)
        act = jax.nn.gelu(jnp.einsum('nhkd,nd->nhk', u_emb[eidx], xb), approximate=False)
        return jnp.einsum('nhk,nhkd->nd', gate * act, v_emb[eidx])

    return lax.map(one_block, hb).reshape(bsz, seq, d)


PEER_SCORE_TOKENS = 256
PEER_GATHER_TOKENS = 64
PEER_SLOTS = P_HEADS * P_TOPK
ROW_VREG = (8, 128)
WORDS_PER_ROW = 4


def _top16(s, payload=None):
    n = s.shape[0]
    pos_f = lax.broadcasted_iota(jnp.int32, s.shape, 0).astype(jnp.float32)
    vals, picks = [], []
    for _ in range(P_TOPK):
        m = jnp.max(s, axis=0, keepdims=True)
        pos = jnp.min(jnp.where(s == m, pos_f, float(n)), axis=0, keepdims=True)
        hit = pos_f == pos
        vals.append(m)
        if payload is None:
            picks.append(pos)
        else:
            picks.append(jnp.max(jnp.where(hit, payload, -1.0), axis=0, keepdims=True))
        s = jnp.where(hit, -jnp.inf, s)
    return vals, picks


def _peer_score_kernel(x_ref, g_ref, wpqt_ref, sk_ref, hn_ref, eidx_ref, gate_ref):
    f32, bf16 = jnp.float32, jnp.bfloat16
    hb = _rms(x_ref[...], g_ref[...]).astype(bf16)
    hn_ref[...] = hb
    qrt = _nt_dot(wpqt_ref[...], hb)
    half = P_QDIM // 2
    e_rows, g_rows = [], []
    for hd in range(P_HEADS):
        tops = []
        for p in range(2):
            qhp = qrt[(hd * 2 + p) * half:(hd * 2 + p + 1) * half, :].astype(bf16)
            s = jnp.dot(sk_ref[hd * 2 + p], qhp, preferred_element_type=f32)
            tops.append(_top16(s))
        (v1, i1), (v2, i2) = tops
        v2m = jnp.concatenate(v2, axis=0)
        i2m = jnp.concatenate(i2, axis=0)
        cand = jnp.concatenate([v1[a] + v2m for a in range(P_TOPK)], axis=0)
        cidx = jnp.concatenate([i1[a] * float(N_KEYS) + i2m for a in range(P_TOPK)], axis=0)
        best, be = _top16(cand, cidx)
        ex = [jnp.exp(b - best[0]) for b in best]
        den = ex[0]
        for k in range(1, P_TOPK):
            den = den + ex[k]
        inv = 1.0 / den
        e_rows += be
        g_rows += [x * inv for x in ex]
    eidx_ref[...] = jnp.concatenate(e_rows, axis=0).T.astype(jnp.int32)
    gate_ref[...] = jnp.concatenate(g_rows, axis=0).T


def _diag_mask():
    r = lax.broadcasted_iota(jnp.int32, (8, PEER_SLOTS * 8), 0)
    c = lax.broadcasted_iota(jnp.int32, (8, PEER_SLOTS * 8), 1)
    return (c & 7) == r


def _gather_rows(idx_ref, tab_ref, g_ref, t):
    for r in range(PEER_SLOTS):
        e = idx_ref[t, r]
        g_ref[r * WORDS_PER_ROW:(r + 1) * WORDS_PER_ROW, :] = tab_ref[
            pl.ds(pl.multiple_of(e * WORDS_PER_ROW, WORDS_PER_ROW), WORDS_PER_ROW), :]


def _peer_act_kernel(idx_ref, hn_ref, tab_ref, a_ref, ga_ref, gb_ref, m_ref):
    f32, bf16 = jnp.float32, jnp.bfloat16
    tb = a_ref.shape[0]

    def pair_body(tp, carry):
        hp = hn_ref[pl.ds(pl.multiple_of(tp * 16, 16), 16), :]
        for u, g_ref in enumerate((ga_ref, gb_ref)):
            t = tp * 2 + u
            _gather_rows(idx_ref, tab_ref, g_ref, t)
            g = pltpu.bitcast(g_ref[...], bf16)
            m = _nt_dot(hp, g)
            m_ref[pl.ds(pl.multiple_of(t * 8, 8), 8), :] = m[u * 8:(u + 1) * 8, :]
        return carry

    lax.fori_loop(0, tb // 2, pair_body, 0)
    m3 = m_ref[...].reshape(tb, 8, PEER_SLOTS * 8)
    z = jnp.sum(jnp.where(_diag_mask()[None], m3, 0.0), axis=1)
    rr = lax.broadcasted_iota(jnp.int32, (PEER_SLOTS * 8, PEER_SLOTS), 0)
    cc = lax.broadcasted_iota(jnp.int32, (PEER_SLOTS * 8, PEER_SLOTS), 1)
    pool = jnp.where((rr >> 3) == cc, 1.0, 0.0).astype(bf16)
    z_hi = z.astype(bf16)
    z_lo = (z - z_hi.astype(f32)).astype(bf16)
    a_ref[...] = (jnp.dot(z_hi, pool, preferred_element_type=f32)
                  + jnp.dot(z_lo, pool, preferred_element_type=f32))


def _peer_out_kernel(idx_ref, a_ref, gate_ref, tab_ref, o_ref, ga_ref, gb_ref, w_ref):
    f32, bf16 = jnp.float32, jnp.bfloat16
    tb = a_ref.shape[0]
    a = a_ref[...]
    act = 0.5 * a * (1.0 + lax.erf(a * (2.0 ** -0.5)))
    wgt = (gate_ref[...] * act).astype(bf16)
    rr = lax.broadcasted_iota(jnp.int32, (PEER_SLOTS, PEER_SLOTS * 8), 0)
    cc = lax.broadcasted_iota(jnp.int32, (PEER_SLOTS, PEER_SLOTS * 8), 1)
    expand = jnp.where((cc >> 3) == rr, 1.0, 0.0).astype(bf16)
    w_ref[...] = jnp.dot(wgt, expand, preferred_element_type=f32)
    diag = _diag_mask()

    def pair_body(tp, carry):
        for u, g_ref in enumerate((ga_ref, gb_ref)):
            t = tp * 2 + u
            _gather_rows(idx_ref, tab_ref, g_ref, t)
            g = pltpu.bitcast(g_ref[...], bf16)
            wrow = jnp.broadcast_to(w_ref[pl.ds(t, 1), :], (8, PEER_SLOTS * 8))
            wsel = jnp.where(diag, wrow, 0.0).astype(bf16)
            o_ref[pl.ds(pl.multiple_of(t * 8, 8), 8), :] = jnp.dot(
                wsel, g, preferred_element_type=f32)
        return carry

    lax.fori_loop(0, tb // 2, pair_body, 0)


def _pack_table(tab):
    n_e = tab.shape[0]
    tb = lax.bitcast_convert_type(
        tab.astype(jnp.bfloat16).reshape(n_e, WORDS_PER_ROW, 2, 128), jnp.uint16).astype(jnp.uint32)
    word = tb[:, :, 0, :] | (tb[:, :, 1, :] << 16)
    return lax.bitcast_convert_type(word, jnp.int32).reshape(n_e * WORDS_PER_ROW, 128)


def peer_pallas(x, g_ffn, w_pq, sub_keys, u_emb, v_emb):
    n_tok, d = x.shape
    f32, bf16 = jnp.float32, jnp.bfloat16
    ts, tg = PEER_SCORE_TOKENS, PEER_GATHER_TOKENS
    wpqt = w_pq.reshape(d, P_HEADS * P_QDIM).T.astype(bf16)
    sk = sub_keys.reshape(P_HEADS * 2, N_KEYS, P_QDIM // 2).astype(bf16)
    hn, eidx, gate = pl.pallas_call(
        _peer_score_kernel,
        grid=(n_tok // ts,),
        in_specs=[pl.BlockSpec((ts, d), lambda i: (i, 0)),
                  pl.BlockSpec((1, d), lambda i: (0, 0)),
                  pl.BlockSpec(wpqt.shape, lambda i: (0, 0)),
                  pl.BlockSpec(sk.shape, lambda i: (0, 0, 0))],
        out_specs=[pl.BlockSpec((ts, d), lambda i: (i, 0)),
                   pl.BlockSpec((ts, PEER_SLOTS), lambda i: (i, 0)),
                   pl.BlockSpec((ts, PEER_SLOTS), lambda i: (i, 0))],
        out_shape=[jax.ShapeDtypeStruct((n_tok, d), bf16),
                   jax.ShapeDtypeStruct((n_tok, PEER_SLOTS), jnp.int32),
                   jax.ShapeDtypeStruct((n_tok, PEER_SLOTS), f32)],
        compiler_params=pltpu.CompilerParams(vmem_limit_bytes=VMEM_LIMIT),
        name="peer_score",
    )(x, g_ffn.reshape(1, d), wpqt, sk)
    utab, vtab = _pack_table(u_emb), _pack_table(v_emb)
    idx_spec = pl.BlockSpec((tg, PEER_SLOTS), lambda i: (i, 0), memory_space=pltpu.SMEM)
    tab_spec = pl.BlockSpec(memory_space=pltpu.VMEM)
    slot_spec = pl.BlockSpec((tg, PEER_SLOTS), lambda i: (i, 0))
    row_spec = pl.BlockSpec((tg * 8, 128), lambda i: (i, 0))
    gbuf = pltpu.VMEM((PEER_SLOTS * WORDS_PER_ROW, 128), jnp.int32)
    cparams = pltpu.CompilerParams(vmem_limit_bytes=VMEM_LIMIT)
    act = pl.pallas_call(
        _peer_act_kernel,
        grid=(n_tok // tg,),
        in_specs=[idx_spec, row_spec, tab_spec],
        out_specs=slot_spec,
        out_shape=jax.ShapeDtypeStruct((n_tok, PEER_SLOTS), f32),
        scratch_shapes=[gbuf, gbuf, pltpu.VMEM((tg * 8, PEER_SLOTS * 8), f32)],
        compiler_params=cparams,
        name="peer_act",
    )(eidx, hn.reshape(n_tok * 8, 128), utab)
    out = pl.pallas_call(
        _peer_out_kernel,
        grid=(n_tok // tg,),
        in_specs=[idx_spec, slot_spec, slot_spec, tab_spec],
        out_specs=row_spec,
        out_shape=jax.ShapeDtypeStruct((n_tok * 8, 128), f32),
        scratch_shapes=[gbuf, gbuf, pltpu.VMEM((tg, PEER_SLOTS * 8), f32)],
        compiler_params=cparams,
        name="peer_out",
    )(eidx, act, gate, vtab)
    return out.reshape(n_tok, d)


def _final_rmsnorm_kernel(x_ref, g_ref, o_ref):
    xf = x_ref[...]
    y = xf * lax.rsqrt(jnp.mean(xf * xf, axis=-1, keepdims=True) + EPS)
    o_ref[...] = y * g_ref[...]


def final_rmsnorm(x, g):
    bsz, seq, d = x.shape
    n = bsz * seq
    tm = 512
    out = pl.pallas_call(
        _final_rmsnorm_kernel,
        grid=(n // tm,),
        in_specs=[pl.BlockSpec((tm, d), lambda i: (i, 0)),
                  pl.BlockSpec((1, d), lambda i: (0, 0))],
        out_specs=pl.BlockSpec((tm, d), lambda i: (i, 0)),
        out_shape=jax.ShapeDtypeStruct((n, d), x.dtype),
        name="final_rmsnorm",
    )(x.reshape(n, d), g.reshape(1, d))
    return out.reshape(bsz, seq, d)


def kernel(x, mem, g_mix, w_in, g_cq, g_ckv, g_kidx, w_uq, w_iq, w_uk, w_uv, rel_bias, conv_w, A_log, dt_bias, g_onorm, w_out, g_cross, g_mem, wq_x, wk_x, wv_x, wo_x, g_ffn, w_pq, sub_keys, u_emb, v_emb, g_final):
    for l in range(DEPTH):
        h = rmsnorm(x, g_mix[l])
        proj = jnp.einsum('bsd,dc->bsc', h, w_in[l])
        cq, ckv, kidx, widx, bq, bk, bv, bz, ba, bb = split_columns(proj, COL_WIDTHS)
        o_a = dsa_pallas(cq, ckv, kidx, widx, g_cq[l], g_ckv[l], g_kidx[l],
                         w_uq[l], w_iq[l], w_uk[l], w_uv[l], rel_bias)
        o_b = gated_deltanet(bq, bk, bv, bz, ba, bb, conv_w[l], A_log[l], dt_bias[l], g_onorm[l])
        x = x + jnp.einsum('bsc,cd->bsd', jnp.concatenate([o_a, o_b], axis=-1), w_out[l])
        x = x + memory_cross_attention(rmsnorm(x, g_cross[l]), rmsnorm(mem, g_mem[l]),
                                       wq_x[l], wk_x[l], wv_x[l], wo_x[l])
        bsz, seq, d = x.shape
        x = x + peer_pallas(x.reshape(bsz * seq, d), g_ffn[l], w_pq[l], sub_keys[l],
                            u_emb[l], v_emb[l]).reshape(bsz, seq, d)
    return final_rmsnorm(x, g_final)
```

```python
import math
from functools import partial
import jax
import jax.numpy as jnp
from jax import lax
import numpy as np
from jax.experimental import pallas as pl
from jax.experimental.pallas import tpu as pltpu

D_MODEL = 1024
BATCH = 4
SEQ = 8192
DEPTH = 1

CHUNK = 64
Q_BLOCK = 128
EPS = 1e-6

A_HEADS = 8
A_HEAD_DIM = 64
A_Q_LORA = 256
A_KV_LORA = 256
IDX_HEADS = 8
IDX_DIM = 64
IDX_TOPK_MAX = 256
ATTN_SCALE = A_HEAD_DIM ** -0.5
IDX_SCALE = (IDX_HEADS * IDX_DIM) ** -0.5

B_HEADS = 4
B_HEAD_DIM = 128
B_QK = B_HEADS * B_HEAD_DIM
CONV_WIDTH = 4

REL_BUCKETS = 32
REL_MAX_DIST = 128

MEM_LEN = 256
X_HEADS = 4
X_HEAD_DIM = 128

P_HEADS = 8
N_KEYS = 128
N_EXPERTS = N_KEYS * N_KEYS
P_TOPK = 16
P_QDIM = 256
P_TOKEN_BLOCK = 128

COL_WIDTHS = (A_Q_LORA, A_KV_LORA, IDX_DIM, IDX_HEADS, B_QK, B_QK, B_QK, B_QK, B_HEADS, B_HEADS)
IN_COLS = sum(COL_WIDTHS)
MIX_WIDTH = A_HEADS * A_HEAD_DIM + B_HEADS * B_HEAD_DIM


def rmsnorm(x, g):
    xf = x.astype(jnp.float32)
    y = xf * lax.rsqrt(jnp.mean(xf * xf, axis=-1, keepdims=True) + EPS)
    return (y * g.astype(jnp.float32)).astype(x.dtype)


def l2norm(x):
    return x * lax.rsqrt(jnp.sum(x * x, axis=-1, keepdims=True) + EPS)


def split_columns(p, widths):
    out, start = [], 0
    for w in widths:
        out.append(p[..., start:start + w])
        start += w
    return out


def t5_bucket(rel):
    half = REL_BUCKETS // 2
    max_exact = half // 2
    n = jnp.abs(rel)
    nf = jnp.maximum(n, max_exact).astype(jnp.float32)
    large = max_exact + (jnp.log(nf / max_exact) / math.log(REL_MAX_DIST / max_exact)
                         * (half - max_exact)).astype(jnp.int32)
    large = jnp.minimum(large, half - 1)
    return jnp.where(rel > 0, half, 0) + jnp.where(n < max_exact, n, large)


def dsa_attention(cq, ckv, kidx, widx, g_cq, g_ckv, g_kidx, w_uq, w_iq, w_uk, w_uv, rel_bias):
    bsz, seq, _ = cq.shape
    topk = min(IDX_TOPK_MAX, seq // 4)
    n_blocks = seq // Q_BLOCK
    cq = rmsnorm(cq, g_cq)
    ckv = rmsnorm(ckv, g_ckv)
    kidx = rmsnorm(kidx, g_kidx).astype(jnp.float32)
    key_pos = jnp.arange(seq, dtype=jnp.int32)

    def blockify(a):
        return jnp.moveaxis(a.reshape(bsz, n_blocks, Q_BLOCK, a.shape[-1]), 1, 0)

    def one_block(args):
        cq_b, w_b, start = args
        qpos = start + jnp.arange(Q_BLOCK, dtype=jnp.int32)
        limit = (qpos // CHUNK + 1) * CHUNK
        admissible = key_pos[None, :] < limit[:, None]
        qi = jnp.einsum('bqc,chd->bqhd', cq_b, w_iq).astype(jnp.float32)
        dots = jax.nn.relu(jnp.einsum('bqhd,bsd->bqhs', qi, kidx))
        score = jnp.einsum('bqh,bqhs->bqs', w_b.astype(jnp.float32) * IDX_SCALE, dots)
        score = jnp.where(admissible[None], score, -jnp.inf)
        _, sel = lax.top_k(score, topk)
        valid = sel < limit[None, :, None]
        c_sel = jax.vmap(lambda c, i: c[i])(ckv, sel)
        q = jnp.einsum('bqc,chd->bqhd', cq_b, w_uq)
        q_lat = jnp.einsum('bqhd,chd->bqhc', q, w_uk)
        logits = jnp.einsum('bqhc,bqkc->bhqk', q_lat, c_sel).astype(jnp.float32) * ATTN_SCALE
        bias = rel_bias[t5_bucket(sel - qpos[None, :, None])].astype(jnp.float32)
        logits = logits + jnp.transpose(bias, (0, 3, 1, 2))
        logits = jnp.where(valid[:, None], logits, -jnp.inf)
        p = jax.nn.softmax(logits, axis=-1).astype(c_sel.dtype)
        o_lat = jnp.einsum('bhqk,bqkc->bqhc', p, c_sel)
        o = jnp.einsum('bqhc,chd->bqhd', o_lat, w_uv)
        return o.reshape(bsz, Q_BLOCK, A_HEADS * A_HEAD_DIM)

    starts = jnp.arange(n_blocks, dtype=jnp.int32) * Q_BLOCK
    o = lax.map(one_block, (blockify(cq), blockify(widx), starts))
    return jnp.moveaxis(o, 0, 1).reshape(bsz, seq, A_HEADS * A_HEAD_DIM)


INT_MIN = -2147483648
NEG_BIG = -1e30
KEY_TILE = 512
KEY_PAD = KEY_TILE - Q_BLOCK
VMEM_LIMIT = 56 * 1024 * 1024


def _rms(x, g):
    return x * lax.rsqrt(jnp.mean(x * x, axis=-1, keepdims=True) + EPS) * g


def _nt_dot(a, b):
    return lax.dot_general(a, b, (((1,), (1,)), ((), ())), preferred_element_type=jnp.float32)


def _dsa_prep_kernel(cq_ref, ckv_ref, kw_ref, gcq_ref, gckv_ref, gk_ref,
                     wiq_ref, wuq_ref, wukt_ref,
                     a_ref, qlat_ref, ckvn_ref, kidxn_ref, w_ref):
    bf16 = jnp.bfloat16
    kw = kw_ref[0]
    cqn = _rms(cq_ref[0], gcq_ref[...]).astype(bf16)
    qi = jnp.dot(cqn, wiq_ref[...], preferred_element_type=jnp.float32)
    q = jnp.dot(cqn, wuq_ref[...], preferred_element_type=jnp.float32)
    for h in range(A_HEADS):
        rows = slice(h * Q_BLOCK, (h + 1) * Q_BLOCK)
        a_ref[0, 0, rows, :] = qi[:, h * IDX_DIM:(h + 1) * IDX_DIM].astype(bf16)
        qh = q[:, h * A_HEAD_DIM:(h + 1) * A_HEAD_DIM].astype(bf16)
        ql = jnp.dot(qh, wukt_ref[h], preferred_element_type=jnp.float32) * ATTN_SCALE
        qlat_ref[0, 0, rows, :] = ql.astype(bf16)
    ckvn_ref[0] = _rms(ckv_ref[0], gckv_ref[...]).astype(bf16)
    kidxn_ref[0] = _rms(kw[:, :IDX_DIM], gk_ref[...]).astype(bf16)
    w_ref[0, 0] = kw[:, IDX_DIM:IDX_DIM + IDX_HEADS] * IDX_SCALE


def _dsa_main_kernel(a_ref, qlat_ref, w_ref, kidx_ref, ckv_ref, bias0_ref, wuv_ref, o_ref,
                     sc_ref, big_ref, p_ref, acc_ref, m_ref, l_ref, alpha_ref, wb_ref, *, topk):
    f32, i32, bf16 = jnp.float32, jnp.int32, jnp.bfloat16
    Q, T = Q_BLOCK, KEY_TILE
    NC = T // 128
    i = pl.program_id(1)
    e = (i + 1) * Q
    nt = (i + T // Q) // (T // Q)
    kf = jnp.float32(topk)

    w_blk = w_ref[0, 0]
    for h in range(IDX_HEADS):
        wb_ref[h] = jnp.broadcast_to(w_blk[:, h:h + 1], (Q, 128))
    row = lax.broadcasted_iota(i32, (Q, 128), 0)
    lane = lax.broadcasted_iota(i32, (Q, 128), 1)
    limit = ((i * Q + row) // CHUNK + 1) * CHUNK
    a_mat = a_ref[0, 0]

    def tile_start(j):
        return pl.multiple_of(e + KEY_PAD - (j + 1) * T, 128)

    def score_tile(j, carry):
        start = tile_start(j)
        big_ref[...] = _nt_dot(a_mat, kidx_ref[0, pl.ds(start, T), :])
        for c in range(NC):
            cols = slice(c * 128, (c + 1) * 128)
            s = jnp.zeros((Q, 128), f32)
            for h in range(IDX_HEADS):
                s = s + wb_ref[h] * jnp.maximum(big_ref[h * Q:(h + 1) * Q, cols], 0.0)
            bits = lax.bitcast_convert_type(s, i32)
            key = jnp.where(bits < 0, bits ^ jnp.int32(0x7FFFFFFF), bits)
            key = jnp.where(s == 0.0, 0, key)
            kpos = start - KEY_PAD + c * 128 + lane
            key = jnp.where(kpos >= 0, key, INT_MIN)
            key = jnp.where(kpos < limit, key, INT_MIN)
            sc_ref[j, :, cols] = key
        return carry

    lax.fori_loop(0, nt, score_tile, 0)

    def count_where(pred):
        def body(j, cnt):
            for c in range(NC):
                cols = slice(c * 128, (c + 1) * 128)
                kpos = tile_start(j) - KEY_PAD + c * 128 + lane
                cnt = cnt + jnp.where(pred(sc_ref[j, :, cols], kpos), 1.0, 0.0)
            return cnt
        cnt = lax.fori_loop(0, nt, body, jnp.zeros((Q, 128), f32))
        return jnp.broadcast_to(jnp.sum(cnt, axis=1, keepdims=True), (Q, 128))

    def bit_body(b, carry):
        u, cacc = carry
        cand = u | lax.shift_left(jnp.int32(1), 31 - b)
        tvec = cand ^ jnp.int32(INT_MIN)
        tot = count_where(lambda k, kpos: k >= tvec)
        ok = tot >= kf
        return jnp.where(ok, cand, u), jnp.where(ok, tot, cacc)

    u, cacc = lax.fori_loop(0, 32, bit_body,
                            (jnp.zeros((Q, 128), i32), jnp.zeros((Q, 128), f32)))
    thr = jnp.maximum(u ^ jnp.int32(INT_MIN), INT_MIN + 1)
    overflow = jnp.where(u != 0, cacc, 0.0) > kf
    n_over = jnp.max(jnp.max(jnp.where(overflow, 1.0, 0.0), axis=1, keepdims=True),
                     axis=0, keepdims=True)[0, 0]

    @pl.when(n_over > 0.0)
    def _():
        need = kf - count_where(lambda k, kpos: k > thr)

        def cut_body(b, cut):
            cand = cut | lax.shift_left(jnp.int32(1), 14 - b)
            cnt = count_where(lambda k, kpos: jnp.where(k == thr, kpos, cand) < cand)
            return jnp.where(cnt <= need, cand, cut)

        cut = lax.fori_loop(0, 15, cut_body, jnp.zeros((Q, 128), i32))

        def drop_tile(j, carry):
            for c in range(NC):
                cols = slice(c * 128, (c + 1) * 128)
                kpos = tile_start(j) - KEY_PAD + c * 128 + lane
                k = sc_ref[j, :, cols]
                drop = jnp.where(k == thr, kpos, -1) >= cut
                sc_ref[j, :, cols] = jnp.where(drop, INT_MIN, k)
            return carry

        lax.fori_loop(0, nt, drop_tile, 0)

    m_ref[...] = jnp.full(m_ref.shape, NEG_BIG, f32)
    l_ref[...] = jnp.zeros(l_ref.shape, f32)
    acc_ref[...] = jnp.zeros(acc_ref.shape, f32)
    qlat = qlat_ref[0, 0]

    def attn_tile(j, with_bias):
        start = tile_start(j)
        kv = ckv_ref[0, pl.ds(start, T), :]
        big_ref[...] = _nt_dot(qlat, kv)
        mask_add = [jnp.where(sc_ref[j, :, c * 128:(c + 1) * 128] >= thr, 0.0, NEG_BIG)
                    for c in range(NC)]
        for h in range(A_HEADS):
            rows = slice(h * Q, (h + 1) * Q)
            lg = []
            for c in range(NC):
                x = big_ref[rows, c * 128:(c + 1) * 128] + mask_add[c]
                if with_bias:
                    x = x + bias0_ref[h, :, c * 128:(c + 1) * 128]
                lg.append(x)
            mx = lg[0]
            for c in range(1, NC):
                mx = jnp.maximum(mx, lg[c])
            m_prev = m_ref[rows, :]
            m_new = jnp.maximum(m_prev, jnp.max(mx, axis=1, keepdims=True))
            psum = jnp.zeros((Q, 128), f32)
            for c in range(NC):
                p = jnp.exp(lg[c] - m_new)
                psum = psum + p
                p_ref[rows, c * 128:(c + 1) * 128] = p.astype(bf16)
            alpha = jnp.exp(m_prev - m_new)
            l_ref[rows, :] = alpha * l_ref[rows, :] + jnp.sum(psum, axis=1, keepdims=True)
            m_ref[rows, :] = m_new
            alpha_ref[rows, :] = alpha
        acc_ref[...] = alpha_ref[...] * acc_ref[...] + jnp.dot(
            p_ref[...], kv, preferred_element_type=f32)

    attn_tile(0, True)

    def attn_body(j, carry):
        attn_tile(j, False)
        return carry

    lax.fori_loop(1, nt, attn_body, 0)

    inv_l = 1.0 / l_ref[...]
    for h in range(A_HEADS):
        rows = slice(h * Q, (h + 1) * Q)
        o_lat = (acc_ref[rows, :] * inv_l[rows, :]).astype(bf16)
        o_ref[0, :, h * A_HEAD_DIM:(h + 1) * A_HEAD_DIM] = jnp.dot(
            o_lat, wuv_ref[h], preferred_element_type=f32)


def dsa_pallas(cq, ckv, kw, g_cq, g_ckv, g_kidx, w_uq, w_iq, w_uk, w_uv, rel_bias):
    bsz, seq, _ = cq.shape
    f32, bf16 = jnp.float32, jnp.bfloat16
    Q, T, H = Q_BLOCK, KEY_TILE, A_HEADS
    nblk = seq // Q
    topk = min(IDX_TOPK_MAX, seq // 4)
    wiq2 = w_iq.reshape(A_Q_LORA, IDX_HEADS * IDX_DIM).astype(bf16)
    wuq2 = w_uq.reshape(A_Q_LORA, H * A_HEAD_DIM).astype(bf16)
    wukt = jnp.transpose(w_uk, (1, 2, 0)).astype(bf16)
    wuv = jnp.transpose(w_uv, (1, 0, 2)).astype(bf16)
    tok = lambda w: pl.BlockSpec((1, Q, w), lambda b, t: (b, t, 0))
    full = lambda shape: pl.BlockSpec(shape, lambda b, t: (0,) * len(shape))
    a_mat, qlat, ckvn, kidxn, wsc = pl.pallas_call(
        _dsa_prep_kernel,
        grid=(bsz, nblk),
        in_specs=[tok(A_Q_LORA), tok(A_KV_LORA), tok(128),
                  full((1, A_Q_LORA)), full((1, A_KV_LORA)), full((1, IDX_DIM)),
                  full(wiq2.shape), full(wuq2.shape), full(wukt.shape)],
        out_specs=[pl.BlockSpec((1, 1, H * Q, IDX_DIM), lambda b, t: (b, t, 0, 0)),
                   pl.BlockSpec((1, 1, H * Q, A_KV_LORA), lambda b, t: (b, t, 0, 0)),
                   tok(A_KV_LORA), tok(IDX_DIM),
                   pl.BlockSpec((1, 1, Q, IDX_HEADS), lambda b, t: (b, t, 0, 0))],
        out_shape=[jax.ShapeDtypeStruct((bsz, nblk, H * Q, IDX_DIM), bf16),
                   jax.ShapeDtypeStruct((bsz, nblk, H * Q, A_KV_LORA), bf16),
                   jax.ShapeDtypeStruct((bsz, seq, A_KV_LORA), bf16),
                   jax.ShapeDtypeStruct((bsz, seq, IDX_DIM), bf16),
                   jax.ShapeDtypeStruct((bsz, nblk, Q, IDX_HEADS), f32)],
        name="dsa_prep",
    )(cq, ckv, kw, g_cq.reshape(1, -1), g_ckv.reshape(1, -1), g_kidx.reshape(1, -1),
      wiq2, wuq2, wukt)
    ckvp = jnp.pad(ckvn, ((0, 0), (KEY_PAD, 0), (0, 0)))
    kidxp = jnp.pad(kidxn, ((0, 0), (KEY_PAD, 0), (0, 0)))
    rel = (jnp.arange(T, dtype=jnp.int32)[None, :] - jnp.arange(Q, dtype=jnp.int32)[:, None]
           - KEY_PAD)
    far = rel_bias[t5_bucket(jnp.int32(-KEY_PAD - 1))]
    bias0 = jnp.transpose(rel_bias[t5_bucket(rel)] - far, (2, 0, 1)).astype(f32)
    nt_max = (nblk - 1 + T // Q) // (T // Q)
    skey = seq + KEY_PAD
    out = pl.pallas_call(
        partial(_dsa_main_kernel, topk=topk),
        grid=(bsz, nblk),
        in_specs=[pl.BlockSpec((1, 1, H * Q, IDX_DIM), lambda b, t: (b, t, 0, 0)),
                  pl.BlockSpec((1, 1, H * Q, A_KV_LORA), lambda b, t: (b, t, 0, 0)),
                  pl.BlockSpec((1, 1, Q, IDX_HEADS), lambda b, t: (b, t, 0, 0)),
                  pl.BlockSpec((1, skey, IDX_DIM), lambda b, t: (b, 0, 0)),
                  pl.BlockSpec((1, skey, A_KV_LORA), lambda b, t: (b, 0, 0)),
                  full(bias0.shape), full(wuv.shape)],
        out_specs=tok(H * A_HEAD_DIM),
        out_shape=jax.ShapeDtypeStruct((bsz, seq, H * A_HEAD_DIM), f32),
        scratch_shapes=[pltpu.VMEM((nt_max, Q, T), jnp.int32),
                        pltpu.VMEM((H * Q, T), f32),
                        pltpu.VMEM((H * Q, T), bf16),
                        pltpu.VMEM((H * Q, A_KV_LORA), f32),
                        pltpu.VMEM((H * Q, 1), f32),
                        pltpu.VMEM((H * Q, 1), f32),
                        pltpu.VMEM((H * Q, 1), f32),
                        pltpu.VMEM((IDX_HEADS, Q, 128), f32)],
        compiler_params=pltpu.CompilerParams(
            dimension_semantics=("arbitrary", "arbitrary"), vmem_limit_bytes=VMEM_LIMIT),
        name="dsa_main",
    )(a_mat, qlat, wsc, kidxp, ckvp, bias0, wuv)
    return out


def causal_dwconv(x, w):
    ch = x.shape[-1]
    return lax.conv_general_dilated(x, w[:, None, :].astype(x.dtype), window_strides=(1,),
                                    padding=[(CONV_WIDTH - 1, 0)],
                                    dimension_numbers=('NWC', 'WIO', 'NWC'),
                                    feature_group_count=ch)


def gated_deltanet(q, k, v, z, a, b, conv_w, A_log, dt_bias, g_onorm):
    bsz, seq, _ = q.shape
    f32 = jnp.float32
    H, Dh = B_HEADS, B_HEAD_DIM
    nc = seq // CHUNK
    qkv = jax.nn.silu(causal_dwconv(jnp.concatenate([q, k, v], axis=-1), conv_w)).astype(f32)
    q, k, v = [t.reshape(bsz, seq, H, Dh) for t in jnp.split(qkv, 3, axis=-1)]
    q = l2norm(q) * (Dh ** -0.5)
    k = l2norm(k)
    beta = jax.nn.sigmoid(b.astype(f32))
    g = -jnp.exp(A_log.astype(f32)) * jax.nn.softplus(a.astype(f32) + dt_bias.astype(f32))

    def to_chunks(t):
        t = t.reshape(bsz, nc, CHUNK, *t.shape[2:])
        return jnp.moveaxis(t, 3, 1)

    q, k, v, beta = to_chunks(q), to_chunks(k), to_chunks(v), to_chunks(beta)
    g = jnp.cumsum(to_chunks(g), axis=-1)
    pos = jnp.arange(CHUNK)
    causal = pos[:, None] >= pos[None, :]
    strict = pos[:, None] > pos[None, :]
    decay = jnp.exp(jnp.where(causal, g[..., :, None] - g[..., None, :], -jnp.inf))
    m = jnp.where(strict, beta[..., :, None] * jnp.einsum('bhnid,bhnjd->bhnij', k, k) * decay, 0.0)
    rhs = jnp.concatenate([v * beta[..., None], k * (beta * jnp.exp(g))[..., None]], axis=-1)
    sol = lax.linalg.triangular_solve(m, rhs, left_side=True, lower=True, unit_diagonal=True)
    u, w = sol[..., :Dh], sol[..., Dh:]
    intra = jnp.einsum('bhnid,bhnjd->bhnij', q, k) * decay
    q_dec = q * jnp.exp(g)[..., None]
    k_tail = k * jnp.exp(g[..., -1:] - g)[..., None]
    chunk_decay = jnp.exp(g[..., -1])

    def step(state, inp):
        q_c, u_c, w_c, a_c, kt_c, d_c = inp
        v_new = u_c - jnp.einsum('bhcd,bhde->bhce', w_c, state)
        o_c = jnp.einsum('bhcd,bhde->bhce', q_c, state) + jnp.einsum('bhij,bhje->bhie', a_c, v_new)
        state = state * d_c[..., None, None] + jnp.einsum('bhcd,bhce->bhde', kt_c, v_new)
        return state, o_c

    xs = tuple(jnp.moveaxis(t, 2, 0) for t in (q_dec, u, w, intra, k_tail, chunk_decay))
    _, o = lax.scan(step, jnp.zeros((bsz, H, Dh, Dh), f32), xs)
    o = jnp.transpose(o, (1, 0, 3, 2, 4)).reshape(bsz, seq, H, Dh)
    o = o * lax.rsqrt(jnp.mean(o * o, axis=-1, keepdims=True) + EPS) * g_onorm.astype(f32)
    o = o * jax.nn.silu(z.astype(f32).reshape(bsz, seq, H, Dh))
    return o.reshape(bsz, seq, H * Dh).astype(z.dtype)


def memory_cross_attention(h, mem_n, wq, wk, wv, wo):
    q = jnp.einsum('bsd,dhe->bhse', h, wq)
    k = jnp.einsum('bmd,dhe->bhme', mem_n, wk)
    v = jnp.einsum('bmd,dhe->bhme', mem_n, wv)
    logits = jnp.einsum('bhse,bhme->bhsm', q, k).astype(jnp.float32) * (X_HEAD_DIM ** -0.5)
    p = jax.nn.softmax(logits, axis=-1).astype(v.dtype)
    o = jnp.einsum('bhsm,bhme->bshe', p, v)
    return jnp.einsum('bshe,hed->bsd', o, wo)


def peer_ffn(h, w_pq, sub_keys, u_emb, v_emb):
    bsz, seq, d = h.shape
    n_tok = bsz * seq
    hb = h.reshape(n_tok // P_TOKEN_BLOCK, P_TOKEN_BLOCK, d)

    def one_block(xb):
        n = xb.shape[0]
        qr = jnp.einsum('nd,dhe->nhe', xb, w_pq).reshape(n, P_HEADS, 2, P_QDIM // 2)
        s = jnp.einsum('nhpe,hpke->nhpk', qr, sub_keys).astype(jnp.float32)
        s1, i1 = lax.top_k(s[:, :, 0], P_TOPK)
        s2, i2 = lax.top_k(s[:, :, 1], P_TOPK)
        cand = (s1[..., :, None] + s2[..., None, :]).reshape(n, P_HEADS, P_TOPK * P_TOPK)
        cidx = (i1[..., :, None] * N_KEYS + i2[..., None, :]).reshape(n, P_HEADS, P_TOPK * P_TOPK)
        best, pos = lax.top_k(cand, P_TOPK)
        eidx = jnp.take_along_axis(cidx, pos, axis=-1)
        gate = jax.nn.softmax(best, axis=-1).astype(xb.dtype)
        act = jax.nn.gelu(jnp.einsum('nhkd,nd->nhk', u_emb[eidx], xb), approximate=False)
        return jnp.einsum('nhk,nhkd->nd', gate * act, v_emb[eidx])

    return lax.map(one_block, hb).reshape(bsz, seq, d)


PEER_SCORE_TOKENS = 256
PEER_GATHER_TOKENS = 64
PEER_SLOTS = P_HEADS * P_TOPK
ROW_VREG = (8, 128)
WORDS_PER_ROW = 4


def _top16(s, payload=None):
    n = s.shape[0]
    pos_f = lax.broadcasted_iota(jnp.int32, s.shape, 0).astype(jnp.float32)
    vals, picks = [], []
    for _ in range(P_TOPK):
        m = jnp.max(s, axis=0, keepdims=True)
        pos = jnp.min(jnp.where(s == m, pos_f, float(n)), axis=0, keepdims=True)
        hit = pos_f == pos
        vals.append(m)
        if payload is None:
            picks.append(pos)
        else:
            picks.append(jnp.max(jnp.where(hit, payload, -1.0), axis=0, keepdims=True))
        s = jnp.where(hit, -jnp.inf, s)
    return vals, picks


def _peer_score_kernel(x_ref, g_ref, wpqt_ref, sk_ref, hn_ref, eidx_ref, gate_ref):
    f32, bf16 = jnp.float32, jnp.bfloat16
    hb = _rms(x_ref[...], g_ref[...]).astype(bf16)
    hn_ref[...] = hb
    qrt = _nt_dot(wpqt_ref[...], hb)
    half = P_QDIM // 2
    e_rows, g_rows = [], []
    for hd in range(P_HEADS):
        tops = []
        for p in range(2):
            qhp = qrt[(hd * 2 + p) * half:(hd * 2 + p + 1) * half, :].astype(bf16)
            s = jnp.dot(sk_ref[hd * 2 + p], qhp, preferred_element_type=f32)
            tops.append(_top16(s))
        (v1, i1), (v2, i2) = tops
        v2m = jnp.concatenate(v2, axis=0)
        i2m = jnp.concatenate(i2, axis=0)
        cand = jnp.concatenate([v1[a] + v2m for a in range(P_TOPK)], axis=0)
        cidx = jnp.concatenate([i1[a] * float(N_KEYS) + i2m for a in range(P_TOPK)], axis=0)
        best, be = _top16(cand, cidx)
        ex = [jnp.exp(b - best[0]) for b in best]
        den = ex[0]
        for k in range(1, P_TOPK):
            den = den + ex[k]
        inv = 1.0 / den
        e_rows += be
        g_rows += [x * inv for x in ex]
    eidx_ref[...] = jnp.concatenate(e_rows, axis=0).T.astype(jnp.int32)
    gate_ref[...] = jnp.concatenate(g_rows, axis=0).T


def _diag_mask():
    r = lax.broadcasted_iota(jnp.int32, (8, PEER_SLOTS * 8), 0)
    c = lax.broadcasted_iota(jnp.int32, (8, PEER_SLOTS * 8), 1)
    return (c & 7) == r


def _gather_rows(idx_ref, tab_ref, g_ref, t):
    for r in range(PEER_SLOTS):
        e = idx_ref[t, r]
        g_ref[r * WORDS_PER_ROW:(r + 1) * WORDS_PER_ROW, :] = tab_ref[
            pl.ds(pl.multiple_of(e * WORDS_PER_ROW, WORDS_PER_ROW), WORDS_PER_ROW), :]


def _peer_act_kernel(idx_ref, hn_ref, tab_ref, a_ref, ga_ref, gb_ref, m_ref):
    f32, bf16 = jnp.float32, jnp.bfloat16
    tb = a_ref.shape[0]

    def pair_body(tp, carry):
        hp = hn_ref[pl.ds(pl.multiple_of(tp * 16, 16), 16), :]
        for u, g_ref in enumerate((ga_ref, gb_ref)):
            t = tp * 2 + u
            _gather_rows(idx_ref, tab_ref, g_ref, t)
            g = pltpu.bitcast(g_ref[...], bf16)
            m = _nt_dot(hp, g)
            m_ref[pl.ds(pl.multiple_of(t * 8, 8), 8), :] = m[u * 8:(u + 1) * 8, :]
        return carry

    lax.fori_loop(0, tb // 2, pair_body, 0)
    m3 = m_ref[...].reshape(tb, 8, PEER_SLOTS * 8)
    z = jnp.sum(jnp.where(_diag_mask()[None], m3, 0.0), axis=1)
    rr = lax.broadcasted_iota(jnp.int32, (PEER_SLOTS * 8, PEER_SLOTS), 0)
    cc = lax.broadcasted_iota(jnp.int32, (PEER_SLOTS * 8, PEER_SLOTS), 1)
    pool = jnp.where((rr >> 3) == cc, 1.0, 0.0).astype(bf16)
    z_hi = z.astype(bf16)
    z_lo = (z - z_hi.astype(f32)).astype(bf16)
    a_ref[...] = (jnp.dot(z_hi, pool, preferred_element_type=f32)
                  + jnp.dot(z_lo, pool, preferred_element_type=f32))


def _peer_out_kernel(idx_ref, a_ref, gate_ref, tab_ref, o_ref, ga_ref, gb_ref, w_ref):
    f32, bf16 = jnp.float32, jnp.bfloat16
    tb = a_ref.shape[0]
    a = a_ref[...]
    act = 0.5 * a * (1.0 + lax.erf(a * (2.0 ** -0.5)))
    wgt = (gate_ref[...] * act).astype(bf16)
    rr = lax.broadcasted_iota(jnp.int32, (PEER_SLOTS, PEER_SLOTS * 8), 0)
    cc = lax.broadcasted_iota(jnp.int32, (PEER_SLOTS, PEER_SLOTS * 8), 1)
    expand = jnp.where((cc >> 3) == rr, 1.0, 0.0).astype(bf16)
    w_ref[...] = jnp.dot(wgt, expand, preferred_element_type=f32)
    diag = _diag_mask()

    def pair_body(tp, carry):
        for u, g_ref in enumerate((ga_ref, gb_ref)):
            t = tp * 2 + u
            _gather_rows(idx_ref, tab_ref, g_ref, t)
            g = pltpu.bitcast(g_ref[...], bf16)
            wrow = jnp.broadcast_to(w_ref[pl.ds(t, 1), :], (8, PEER_SLOTS * 8))
            wsel = jnp.where(diag, wrow, 0.0).astype(bf16)
            o_ref[pl.ds(pl.multiple_of(t * 8, 8), 8), :] = jnp.dot(
                wsel, g, preferred_element_type=f32)
        return carry

    lax.fori_loop(0, tb // 2, pair_body, 0)


def _pack_table(tab):
    n_e = tab.shape[0]
    tb = lax.bitcast_convert_type(
        tab.astype(jnp.bfloat16).reshape(n_e, WORDS_PER_ROW, 2, 128), jnp.uint16).astype(jnp.uint32)
    word = tb[:, :, 0, :] | (tb[:, :, 1, :] << 16)
    return lax.bitcast_convert_type(word, jnp.int32).reshape(n_e * WORDS_PER_ROW, 128)


def peer_pallas(x, g_ffn, w_pq, sub_keys, u_emb, v_emb):
    n_tok, d = x.shape
    f32, bf16 = jnp.float32, jnp.bfloat16
    ts, tg = PEER_SCORE_TOKENS, PEER_GATHER_TOKENS
    wpqt = w_pq.reshape(d, P_HEADS * P_QDIM).T.astype(bf16)
    sk = sub_keys.reshape(P_HEADS * 2, N_KEYS, P_QDIM // 2).astype(bf16)
    hn, eidx, gate = pl.pallas_call(
        _peer_score_kernel,
        grid=(n_tok // ts,),
        in_specs=[pl.BlockSpec((ts, d), lambda i: (i, 0)),
                  pl.BlockSpec((1, d), lambda i: (0, 0)),
                  pl.BlockSpec(wpqt.shape, lambda i: (0, 0)),
                  pl.BlockSpec(sk.shape, lambda i: (0, 0, 0))],
        out_specs=[pl.BlockSpec((ts, d), lambda i: (i, 0)),
                   pl.BlockSpec((ts, PEER_SLOTS), lambda i: (i, 0)),
                   pl.BlockSpec((ts, PEER_SLOTS), lambda i: (i, 0))],
        out_shape=[jax.ShapeDtypeStruct((n_tok, d), bf16),
                   jax.ShapeDtypeStruct((n_tok, PEER_SLOTS), jnp.int32),
                   jax.ShapeDtypeStruct((n_tok, PEER_SLOTS), f32)],
        compiler_params=pltpu.CompilerParams(vmem_limit_bytes=VMEM_LIMIT),
        name="peer_score",
    )(x, g_ffn.reshape(1, d), wpqt, sk)
    utab, vtab = _pack_table(u_emb), _pack_table(v_emb)
    idx_spec = pl.BlockSpec((tg, PEER_SLOTS), lambda i: (i, 0), memory_space=pltpu.SMEM)
    tab_spec = pl.BlockSpec(memory_space=pltpu.VMEM)
    slot_spec = pl.BlockSpec((tg, PEER_SLOTS), lambda i: (i, 0))
    row_spec = pl.BlockSpec((tg * 8, 128), lambda i: (i, 0))
    gbuf = pltpu.VMEM((PEER_SLOTS * WORDS_PER_ROW, 128), jnp.int32)
    cparams = pltpu.CompilerParams(vmem_limit_bytes=VMEM_LIMIT)
    act = pl.pallas_call(
        _peer_act_kernel,
        grid=(n_tok // tg,),
        in_specs=[idx_spec, row_spec, tab_spec],
        out_specs=slot_spec,
        out_shape=jax.ShapeDtypeStruct((n_tok, PEER_SLOTS), f32),
        scratch_shapes=[gbuf, gbuf, pltpu.VMEM((tg * 8, PEER_SLOTS * 8), f32)],
        compiler_params=cparams,
        name="peer_act",
    )(eidx, hn.reshape(n_tok * 8, 128), utab)
    out = pl.pallas_call(
        _peer_out_kernel,
        grid=(n_tok // tg,),
        in_specs=[idx_spec, slot_spec, slot_spec, tab_spec],
        out_specs=row_spec,
        out_shape=jax.ShapeDtypeStruct((n_tok * 8, 128), f32),
        scratch_shapes=[gbuf, gbuf, pltpu.VMEM((tg, PEER_SLOTS * 8), f32)],
        compiler_params=cparams,
        name="peer_out",
    )(eidx, act, gate, vtab)
    return out.reshape(n_tok, d)


def _final_kernel(x_ref, y_ref, g_ref, o_ref):
    o_ref[...] = _rms(x_ref[...] + y_ref[...], g_ref[...])


def final_residual_rmsnorm(x, y, g):
    n, d = x.shape
    tm = 512
    row = pl.BlockSpec((tm, d), lambda i: (i, 0))
    return pl.pallas_call(
        _final_kernel,
        grid=(n // tm,),
        in_specs=[row, row, pl.BlockSpec((1, d), lambda i: (0, 0))],
        out_specs=row,
        out_shape=jax.ShapeDtypeStruct((n, d), x.dtype),
        name="final_rmsnorm",
    )(x, y, g.reshape(1, d))


IN_PROJ_TOKENS = 512
GDN_COLS = 3 * B_QK
GATE_ROWS = 8


def _in_proj_kernel(x_ref, g_ref, wa_ref, wkw_ref, wqkv_ref, wz_ref, wab_ref, wabt_ref,
                    cq_ref, ckv_ref, kw_ref, qkv_ref, z_ref, ab_ref, abt_ref):
    f32 = jnp.float32
    hb = _rms(x_ref[0], g_ref[...]).astype(jnp.bfloat16)
    a = jnp.dot(hb, wa_ref[...], preferred_element_type=f32)
    cq_ref[0] = a[:, :A_Q_LORA]
    ckv_ref[0] = a[:, A_Q_LORA:]
    kw_ref[0] = jnp.dot(hb, wkw_ref[...], preferred_element_type=f32)
    qkv_ref[0] = jnp.dot(hb, wqkv_ref[...], preferred_element_type=f32)
    z_ref[0] = jnp.dot(hb, wz_ref[...], preferred_element_type=f32)
    ab_ref[0] = jnp.dot(hb, wab_ref[...], preferred_element_type=f32)
    abt_ref[0] = _nt_dot(wabt_ref[...], hb)


def in_proj_pallas(x, g_mix, w_in):
    bsz, seq, d = x.shape
    f32, bf16 = jnp.float32, jnp.bfloat16
    tm = min(IN_PROJ_TOKENS, seq)
    o = np.cumsum((0,) + COL_WIDTHS)
    wb = w_in.astype(bf16)
    pad_cols = lambda w: jnp.pad(w, ((0, 0), (0, 128 - w.shape[1])))
    wa = wb[:, o[0]:o[2]]
    wkw = pad_cols(wb[:, o[2]:o[4]])
    wqkv = wb[:, o[4]:o[7]]
    wz = wb[:, o[7]:o[8]]
    wab = pad_cols(wb[:, o[8]:o[10]])
    wabt = wb[:, o[8]:o[10]].T
    full = lambda w: pl.BlockSpec(w.shape, lambda b, t: (0, 0))
    tok = lambda w: pl.BlockSpec((1, tm, w), lambda b, t: (b, t, 0))
    shp = lambda w: jax.ShapeDtypeStruct((bsz, seq, w), f32)
    return pl.pallas_call(
        _in_proj_kernel,
        grid=(bsz, seq // tm),
        in_specs=[tok(d), pl.BlockSpec((1, d), lambda b, t: (0, 0)),
                  full(wa), full(wkw), full(wqkv), full(wz), full(wab), full(wabt)],
        out_specs=[tok(A_Q_LORA), tok(A_KV_LORA), tok(128), tok(GDN_COLS), tok(B_QK), tok(128),
                   pl.BlockSpec((1, GATE_ROWS, tm), lambda b, t: (b, 0, t))],
        out_shape=[shp(A_Q_LORA), shp(A_KV_LORA), shp(128), shp(GDN_COLS), shp(B_QK), shp(128),
                   jax.ShapeDtypeStruct((bsz, GATE_ROWS, seq), f32)],
        compiler_params=pltpu.CompilerParams(vmem_limit_bytes=VMEM_LIMIT),
        name="in_proj",
    )(x, g_mix.reshape(1, d), wa, wkw, wqkv, wz, wab, wabt)


def _softplus(x):
    return jnp.maximum(x, 0.0) + jnp.log1p(jnp.exp(-jnp.abs(x)))


def _sigmoid(x):
    return 1.0 / (1.0 + jnp.exp(-x))


def _gdn_gates(pre, a_log, dt_bias, is_decay):
    g = -jnp.exp(a_log) * _softplus(pre + dt_bias)
    return jnp.where(is_decay, g, _sigmoid(pre))


def _gdn_prep_kernel(qkv_ref, halo_ref, cw_ref, ab_ref, abt_ref, alc_ref, dtc_ref, alr_ref, dtr_ref,
                     q_ref, k_ref, v_ref, gc_ref, gr_ref):
    tm = qkv_ref.shape[1]
    x = qkv_ref[0]
    halo = jnp.where(pl.program_id(1) > 0, halo_ref[0], 0.0)
    full = jnp.concatenate([halo, x], axis=0)
    y = x * cw_ref[CONV_WIDTH - 1:CONV_WIDTH, :]
    for back in range(1, CONV_WIDTH):
        shifted = pltpu.roll(full, back, axis=0)[8:, :]
        y = y + shifted * cw_ref[CONV_WIDTH - 1 - back:CONV_WIDTH - back, :]
    y = y * _sigmoid(y)
    for h in range(B_HEADS):
        cols = slice(h * B_HEAD_DIM, (h + 1) * B_HEAD_DIM)
        qh = y[:, h * B_HEAD_DIM:(h + 1) * B_HEAD_DIM]
        kh = y[:, B_QK + h * B_HEAD_DIM:B_QK + (h + 1) * B_HEAD_DIM]
        q_ref[0, :, cols] = qh * lax.rsqrt(
            jnp.sum(qh * qh, axis=-1, keepdims=True) + EPS) * (B_HEAD_DIM ** -0.5)
        k_ref[0, :, cols] = kh * lax.rsqrt(jnp.sum(kh * kh, axis=-1, keepdims=True) + EPS)
    v_ref[0] = y[:, 2 * B_QK:]
    lane = lax.broadcasted_iota(jnp.int32, (tm, 128), 1)
    gc_ref[0] = _gdn_gates(ab_ref[0], alc_ref[...], dtc_ref[...], lane < B_HEADS)
    row = lax.broadcasted_iota(jnp.int32, (GATE_ROWS, tm), 0)
    gr_ref[0] = _gdn_gates(abt_ref[0], alr_ref[...], dtr_ref[...], row < B_HEADS)


def _gdn_main_kernel(q_ref, k_ref, v_ref, z_ref, gc_ref, gr_ref, gon_ref, o_ref, s_ref):
    f32, bf16 = jnp.float32, jnp.bfloat16
    C, Dh = CHUNK, B_HEAD_DIM
    hi = lax.Precision.HIGHEST

    @pl.when(pl.program_id(1) == 0)
    def _():
        s_ref[...] = jnp.zeros(s_ref.shape, f32)

    ii = lax.broadcasted_iota(jnp.int32, (C, C), 0)
    jj = lax.broadcasted_iota(jnp.int32, (C, C), 1)
    causal = ii >= jj
    strict = ii > jj
    lower = jnp.where(causal, 1.0, 0.0)
    upper = jnp.where(jj >= ii, 1.0, 0.0)
    mm = lambda a, b: jnp.dot(a, b, preferred_element_type=f32)
    mmh = lambda a, b: jnp.dot(a, b, preferred_element_type=f32, precision=hi)
    for c in range(q_ref.shape[1] // C):
        rows = slice(c * C, (c + 1) * C)
        gates_c = gc_ref[0, rows, :]
        gates_r = gr_ref[0, :, c * C:(c + 1) * C]
        for h in range(B_HEADS):
            cols = slice(h * Dh, (h + 1) * Dh)
            g_col = jnp.broadcast_to(gates_c[:, h:h + 1], (C, Dh))
            beta = jnp.broadcast_to(gates_c[:, B_HEADS + h:B_HEADS + h + 1], (C, Dh))
            g_row = jnp.broadcast_to(gates_r[h:h + 1, :], (C, C))
            gcum = mmh(lower, g_col)
            gcum_r = mmh(g_row, upper)
            decay = jnp.where(causal, jnp.exp(jnp.minimum(gcum[:, :C] - gcum_r, 0.0)), 0.0)
            q, k, v = q_ref[0, rows, cols], k_ref[0, rows, cols], v_ref[0, rows, cols]
            qb, kb = q.astype(bf16), k.astype(bf16)
            kk = _nt_dot(kb, kb)
            qk = _nt_dot(qb, kb)
            neg_m = jnp.where(strict, -(beta[:, :C] * kk * decay), 0.0)
            egc = jnp.exp(gcum)
            sol = jnp.concatenate([v * beta, k * (beta * egc)], axis=1)
            p = neg_m
            for step in range(6):
                sol = sol + mmh(p, sol)
                if step < 5:
                    p = mmh(p, p)
            u, w = sol[:, :Dh], sol[:, Dh:]
            g_last = gcum[C - 1:C, :]
            state = s_ref[h]
            sb = state.astype(bf16)
            v_new = u - mm(w.astype(bf16), sb)
            vb = v_new.astype(bf16)
            o = mm((q * egc).astype(bf16), sb) + mm((qk * decay).astype(bf16), vb)
            k_tail = (k * jnp.exp(g_last - gcum)).astype(bf16)
            s_ref[h] = state * jnp.exp(g_last) + lax.dot_general(
                k_tail, vb, (((0,), (0,)), ((), ())), preferred_element_type=f32)
            o = o * lax.rsqrt(jnp.mean(o * o, axis=-1, keepdims=True) + EPS) * gon_ref[...]
            zz = z_ref[0, rows, cols]
            o_ref[0, rows, cols] = o * (zz * _sigmoid(zz))


def gdn_pallas(qkv, z, ab, abt, conv_w, a_log, dt_bias, g_onorm):
    bsz, seq, _ = qkv.shape
    f32 = jnp.float32
    tm = min(256, seq)
    zero4 = jnp.zeros((B_HEADS,), f32)
    lane_row = lambda v: jnp.pad(jnp.concatenate([v.astype(f32), zero4]), (0, 120)).reshape(1, 128)
    sub_col = lambda v: jnp.concatenate([v.astype(f32), zero4]).reshape(GATE_ROWS, 1)
    tok = lambda w, t=tm: pl.BlockSpec((1, t, w), lambda b, i: (b, i, 0))
    const = lambda shape: pl.BlockSpec(shape, lambda b, i: (0,) * len(shape))
    shp = lambda w: jax.ShapeDtypeStruct((bsz, seq, w), f32)
    q, k, v, gc, gr = pl.pallas_call(
        _gdn_prep_kernel,
        grid=(bsz, seq // tm),
        in_specs=[tok(GDN_COLS),
                  pl.BlockSpec((1, 8, GDN_COLS), lambda b, i: (b, jnp.maximum(i * (tm // 8) - 1, 0), 0)),
                  const((CONV_WIDTH, GDN_COLS)), tok(128),
                  pl.BlockSpec((1, GATE_ROWS, tm), lambda b, i: (b, 0, i)),
                  const((1, 128)), const((1, 128)), const((GATE_ROWS, 1)), const((GATE_ROWS, 1))],
        out_specs=[tok(B_QK), tok(B_QK), tok(B_QK), tok(128),
                   pl.BlockSpec((1, GATE_ROWS, tm), lambda b, i: (b, 0, i))],
        out_shape=[shp(B_QK), shp(B_QK), shp(B_QK), shp(128),
                   jax.ShapeDtypeStruct((bsz, GATE_ROWS, seq), f32)],
        compiler_params=pltpu.CompilerParams(vmem_limit_bytes=VMEM_LIMIT),
        name="gdn_prep",
    )(qkv, qkv, conv_w.astype(f32), ab, abt, lane_row(a_log), lane_row(dt_bias),
      sub_col(a_log), sub_col(dt_bias))
    ts = 2 * CHUNK
    return pl.pallas_call(
        _gdn_main_kernel,
        grid=(bsz, seq // ts),
        in_specs=[tok(B_QK, ts), tok(B_QK, ts), tok(B_QK, ts), tok(B_QK, ts), tok(128, ts),
                  pl.BlockSpec((1, GATE_ROWS, ts), lambda b, i: (b, 0, i)),
                  const((1, B_HEAD_DIM))],
        out_specs=tok(B_QK, ts),
        out_shape=shp(B_QK),
        scratch_shapes=[pltpu.VMEM((B_HEADS, B_HEAD_DIM, B_HEAD_DIM), f32)],
        compiler_params=pltpu.CompilerParams(dimension_semantics=("arbitrary", "arbitrary")),
        name="gdn_main",
    )(q, k, v, z, gc, gr, g_onorm.astype(f32).reshape(1, B_HEAD_DIM))


def _mem_kv_kernel(mem_ref, g_ref, wk_ref, wv_ref, k_ref, v_ref):
    f32, bf16 = jnp.float32, jnp.bfloat16
    mn = _rms(mem_ref[0], g_ref[...]).astype(bf16)
    k = jnp.dot(mn, wk_ref[...], preferred_element_type=f32)
    v = jnp.dot(mn, wv_ref[...], preferred_element_type=f32)
    for h in range(X_HEADS):
        cols = slice(h * X_HEAD_DIM, (h + 1) * X_HEAD_DIM)
        k_ref[0, h] = k[:, cols].astype(bf16)
        v_ref[0, h] = v[:, cols].astype(bf16)


def _mid_kernel(x_ref, oa_ref, ob_ref, wo_ref, gx_ref, wq_ref, k_ref, v_ref, wox_ref, o_ref):
    f32, bf16 = jnp.float32, jnp.bfloat16
    na = oa_ref.shape[2]
    x1 = (x_ref[0]
          + jnp.dot(oa_ref[0].astype(bf16), wo_ref[:na, :], preferred_element_type=f32)
          + jnp.dot(ob_ref[0].astype(bf16), wo_ref[na:, :], preferred_element_type=f32))
    hq = _rms(x1, gx_ref[...]).astype(bf16)
    q = jnp.dot(hq, wq_ref[...], preferred_element_type=f32)
    heads = []
    for h in range(X_HEADS):
        qh = q[:, h * X_HEAD_DIM:(h + 1) * X_HEAD_DIM].astype(bf16)
        lg = _nt_dot(qh, k_ref[0, h]) * (X_HEAD_DIM ** -0.5)
        p = jnp.exp(lg - jnp.max(lg, axis=-1, keepdims=True))
        p = (p / jnp.sum(p, axis=-1, keepdims=True)).astype(bf16)
        heads.append(jnp.dot(p, v_ref[0, h], preferred_element_type=f32).astype(bf16))
    o = jnp.concatenate(heads, axis=1)
    o_ref[0] = x1 + jnp.dot(o, wox_ref[...], preferred_element_type=f32)


def mid_pallas(x, o_a, o_b, w_out, g_cross, mem, g_mem, wq, wk, wv, wo):
    bsz, seq, d = x.shape
    f32, bf16 = jnp.float32, jnp.bfloat16
    hx = X_HEADS * X_HEAD_DIM
    m_len = mem.shape[1]
    const2 = lambda shape: pl.BlockSpec(shape, lambda b: (0,) * len(shape))
    kv_spec = pl.BlockSpec((1, X_HEADS, m_len, X_HEAD_DIM), lambda b: (b, 0, 0, 0))
    kv_shape = jax.ShapeDtypeStruct((bsz, X_HEADS, m_len, X_HEAD_DIM), bf16)
    k, v = pl.pallas_call(
        _mem_kv_kernel,
        grid=(bsz,),
        in_specs=[pl.BlockSpec((1, m_len, d), lambda b: (b, 0, 0)), const2((1, d)),
                  const2((d, hx)), const2((d, hx))],
        out_specs=[kv_spec, kv_spec],
        out_shape=[kv_shape, kv_shape],
        name="mem_kv",
    )(mem, g_mem.reshape(1, d), wk.reshape(d, hx).astype(bf16), wv.reshape(d, hx).astype(bf16))
    tm = min(256, seq)
    tok = lambda w: pl.BlockSpec((1, tm, w), lambda b, t: (b, t, 0))
    const = lambda shape: pl.BlockSpec(shape, lambda b, t: (0,) * len(shape))
    kv_spec2 = pl.BlockSpec((1, X_HEADS, m_len, X_HEAD_DIM), lambda b, t: (b, 0, 0, 0))
    return pl.pallas_call(
        _mid_kernel,
        grid=(bsz, seq // tm),
        in_specs=[tok(d), tok(o_a.shape[2]), tok(o_b.shape[2]), const((MIX_WIDTH, d)),
                  const((1, d)), const((d, hx)), kv_spec2, kv_spec2, const((hx, d))],
        out_specs=tok(d),
        out_shape=jax.ShapeDtypeStruct((bsz, seq, d), f32),
        compiler_params=pltpu.CompilerParams(vmem_limit_bytes=VMEM_LIMIT),
        name="mid",
    )(x, o_a, o_b, w_out.astype(bf16), g_cross.reshape(1, d), wq.reshape(d, hx).astype(bf16),
      k, v, wo.reshape(hx, d).astype(bf16))


def kernel(x, mem, g_mix, w_in, g_cq, g_ckv, g_kidx, w_uq, w_iq, w_uk, w_uv, rel_bias, conv_w, A_log, dt_bias, g_onorm, w_out, g_cross, g_mem, wq_x, wk_x, wv_x, wo_x, g_ffn, w_pq, sub_keys, u_emb, v_emb, g_final):
    bsz, seq, d = x.shape
    for l in range(DEPTH):
        cq, ckv, kw, qkv, z, ab, abt = in_proj_pallas(x, g_mix[l], w_in[l])
        o_a = dsa_pallas(cq, ckv, kw, g_cq[l], g_ckv[l], g_kidx[l],
                         w_uq[l], w_iq[l], w_uk[l], w_uv[l], rel_bias)
        o_b = gdn_pallas(qkv, z, ab, abt, conv_w[l], A_log[l], dt_bias[l], g_onorm[l])
        x = mid_pallas(x, o_a, o_b, w_out[l], g_cross[l], mem, g_mem[l],
                       wq_x[l], wk_x[l], wv_x[l], wo_x[l])
        xf = x.reshape(bsz * seq, d)
        y = peer_pallas(xf, g_ffn[l], w_pq[l], sub_keys[l], u_emb[l], v_emb[l])
        if l + 1 < DEPTH:
            x = (xf + y).reshape(bsz, seq, d)
    return final_residual_rmsnorm(xf, y, g_final).reshape(bsz, seq, d)
```

```python
import math
from functools import partial
import jax
import jax.numpy as jnp
from jax import lax
import numpy as np
from jax.experimental import pallas as pl
from jax.experimental.pallas import tpu as pltpu

D_MODEL = 1024
BATCH = 4
SEQ = 8192
DEPTH = 1

CHUNK = 64
Q_BLOCK = 128
EPS = 1e-6

A_HEADS = 8
A_HEAD_DIM = 64
A_Q_LORA = 256
A_KV_LORA = 256
IDX_HEADS = 8
IDX_DIM = 64
IDX_TOPK_MAX = 256
ATTN_SCALE = A_HEAD_DIM ** -0.5
IDX_SCALE = (IDX_HEADS * IDX_DIM) ** -0.5

B_HEADS = 4
B_HEAD_DIM = 128
B_QK = B_HEADS * B_HEAD_DIM
CONV_WIDTH = 4

REL_BUCKETS = 32
REL_MAX_DIST = 128

MEM_LEN = 256
X_HEADS = 4
X_HEAD_DIM = 128

P_HEADS = 8
N_KEYS = 128
N_EXPERTS = N_KEYS * N_KEYS
P_TOPK = 16
P_QDIM = 256
P_TOKEN_BLOCK = 128

COL_WIDTHS = (A_Q_LORA, A_KV_LORA, IDX_DIM, IDX_HEADS, B_QK, B_QK, B_QK, B_QK, B_HEADS, B_HEADS)
IN_COLS = sum(COL_WIDTHS)
MIX_WIDTH = A_HEADS * A_HEAD_DIM + B_HEADS * B_HEAD_DIM


def rmsnorm(x, g):
    xf = x.astype(jnp.float32)
    y = xf * lax.rsqrt(jnp.mean(xf * xf, axis=-1, keepdims=True) + EPS)
    return (y * g.astype(jnp.float32)).astype(x.dtype)


def l2norm(x):
    return x * lax.rsqrt(jnp.sum(x * x, axis=-1, keepdims=True) + EPS)


def split_columns(p, widths):
    out, start = [], 0
    for w in widths:
        out.append(p[..., start:start + w])
        start += w
    return out


def t5_bucket(rel):
    half = REL_BUCKETS // 2
    max_exact = half // 2
    n = jnp.abs(rel)
    nf = jnp.maximum(n, max_exact).astype(jnp.float32)
    large = max_exact + (jnp.log(nf / max_exact) / math.log(REL_MAX_DIST / max_exact)
                         * (half - max_exact)).astype(jnp.int32)
    large = jnp.minimum(large, half - 1)
    return jnp.where(rel > 0, half, 0) + jnp.where(n < max_exact, n, large)


def dsa_attention(cq, ckv, kidx, widx, g_cq, g_ckv, g_kidx, w_uq, w_iq, w_uk, w_uv, rel_bias):
    bsz, seq, _ = cq.shape
    topk = min(IDX_TOPK_MAX, seq // 4)
    n_blocks = seq // Q_BLOCK
    cq = rmsnorm(cq, g_cq)
    ckv = rmsnorm(ckv, g_ckv)
    kidx = rmsnorm(kidx, g_kidx).astype(jnp.float32)
    key_pos = jnp.arange(seq, dtype=jnp.int32)

    def blockify(a):
        return jnp.moveaxis(a.reshape(bsz, n_blocks, Q_BLOCK, a.shape[-1]), 1, 0)

    def one_block(args):
        cq_b, w_b, start = args
        qpos = start + jnp.arange(Q_BLOCK, dtype=jnp.int32)
        limit = (qpos // CHUNK + 1) * CHUNK
        admissible = key_pos[None, :] < limit[:, None]
        qi = jnp.einsum('bqc,chd->bqhd', cq_b, w_iq).astype(jnp.float32)
        dots = jax.nn.relu(jnp.einsum('bqhd,bsd->bqhs', qi, kidx))
        score = jnp.einsum('bqh,bqhs->bqs', w_b.astype(jnp.float32) * IDX_SCALE, dots)
        score = jnp.where(admissible[None], score, -jnp.inf)
        _, sel = lax.top_k(score, topk)
        valid = sel < limit[None, :, None]
        c_sel = jax.vmap(lambda c, i: c[i])(ckv, sel)
        q = jnp.einsum('bqc,chd->bqhd', cq_b, w_uq)
        q_lat = jnp.einsum('bqhd,chd->bqhc', q, w_uk)
        logits = jnp.einsum('bqhc,bqkc->bhqk', q_lat, c_sel).astype(jnp.float32) * ATTN_SCALE
        bias = rel_bias[t5_bucket(sel - qpos[None, :, None])].astype(jnp.float32)
        logits = logits + jnp.transpose(bias, (0, 3, 1, 2))
        logits = jnp.where(valid[:, None], logits, -jnp.inf)
        p = jax.nn.softmax(logits, axis=-1).astype(c_sel.dtype)
        o_lat = jnp.einsum('bhqk,bqkc->bqhc', p, c_sel)
        o = jnp.einsum('bqhc,chd->bqhd', o_lat, w_uv)
        return o.reshape(bsz, Q_BLOCK, A_HEADS * A_HEAD_DIM)

    starts = jnp.arange(n_blocks, dtype=jnp.int32) * Q_BLOCK
    o = lax.map(one_block, (blockify(cq), blockify(widx), starts))
    return jnp.moveaxis(o, 0, 1).reshape(bsz, seq, A_HEADS * A_HEAD_DIM)


INT_MIN = -2147483648
NEG_BIG = -1e30
KEY_TILE = 512
KEY_PAD = KEY_TILE - Q_BLOCK
VMEM_LIMIT = 56 * 1024 * 1024


def _rms(x, g):
    return x * lax.rsqrt(jnp.mean(x * x, axis=-1, keepdims=True) + EPS) * g


def _nt_dot(a, b):
    return lax.dot_general(a, b, (((1,), (1,)), ((), ())), preferred_element_type=jnp.float32)


def _dsa_prep_kernel(cq_ref, ckv_ref, kw_ref, gcq_ref, gckv_ref, gk_ref,
                     wiq_ref, wuq_ref, wukt_ref,
                     a_ref, qlat_ref, ckvn_ref, kidxn_ref, w_ref):
    bf16 = jnp.bfloat16
    kw = kw_ref[0]
    cqn = _rms(cq_ref[0], gcq_ref[...]).astype(bf16)
    qi = jnp.dot(cqn, wiq_ref[...], preferred_element_type=jnp.float32)
    q = jnp.dot(cqn, wuq_ref[...], preferred_element_type=jnp.float32)
    for h in range(A_HEADS):
        rows = slice(h * Q_BLOCK, (h + 1) * Q_BLOCK)
        a_ref[0, 0, rows, :] = qi[:, h * IDX_DIM:(h + 1) * IDX_DIM].astype(bf16)
        qh = q[:, h * A_HEAD_DIM:(h + 1) * A_HEAD_DIM].astype(bf16)
        ql = jnp.dot(qh, wukt_ref[h], preferred_element_type=jnp.float32) * ATTN_SCALE
        qlat_ref[0, 0, rows, :] = ql.astype(bf16)
    ckvn_ref[0] = _rms(ckv_ref[0], gckv_ref[...]).astype(bf16)
    kidxn_ref[0] = _rms(kw[:, :IDX_DIM], gk_ref[...]).astype(bf16)
    w_ref[0, 0] = kw[:, IDX_DIM:IDX_DIM + IDX_HEADS] * IDX_SCALE


def _dsa_main_kernel(a_ref, qlat_ref, w_ref, kidx_ref, ckv_ref, bias0_ref, wuv_ref, o_ref,
                     sc_ref, big_ref, p_ref, acc_ref, m_ref, l_ref, alpha_ref, wb_ref, *, topk):
    f32, i32, bf16 = jnp.float32, jnp.int32, jnp.bfloat16
    Q, T = Q_BLOCK, KEY_TILE
    NC = T // 128
    i = pl.program_id(1)
    e = (i + 1) * Q
    nt = (i + T // Q) // (T // Q)
    kf = jnp.float32(topk)

    w_blk = w_ref[0, 0]
    for h in range(IDX_HEADS):
        wb_ref[h] = jnp.broadcast_to(w_blk[:, h:h + 1], (Q, 128))
    row = lax.broadcasted_iota(i32, (Q, 128), 0)
    lane = lax.broadcasted_iota(i32, (Q, 128), 1)
    limit = ((i * Q + row) // CHUNK + 1) * CHUNK
    a_mat = a_ref[0, 0]

    def tile_start(j):
        return pl.multiple_of(e + KEY_PAD - (j + 1) * T, 128)

    def score_tile(j, carry):
        start = tile_start(j)
        big_ref[...] = _nt_dot(a_mat, kidx_ref[0, pl.ds(start, T), :])
        for c in range(NC):
            cols = slice(c * 128, (c + 1) * 128)
            s = jnp.zeros((Q, 128), f32)
            for h in range(IDX_HEADS):
                s = s + wb_ref[h] * jnp.maximum(big_ref[h * Q:(h + 1) * Q, cols], 0.0)
            bits = lax.bitcast_convert_type(s, i32)
            key = jnp.where(bits < 0, bits ^ jnp.int32(0x7FFFFFFF), bits)
            key = jnp.where(s == 0.0, 0, key)
            kpos = start - KEY_PAD + c * 128 + lane
            key = jnp.where(kpos >= 0, key, INT_MIN)
            key = jnp.where(kpos < limit, key, INT_MIN)
            sc_ref[j, :, cols] = key
        return carry

    lax.fori_loop(0, nt, score_tile, 0)

    def count_where(pred):
        def body(j, cnt):
            for c in range(NC):
                cols = slice(c * 128, (c + 1) * 128)
                kpos = tile_start(j) - KEY_PAD + c * 128 + lane
                cnt = cnt + jnp.where(pred(sc_ref[j, :, cols], kpos), 1.0, 0.0)
            return cnt
        cnt = lax.fori_loop(0, nt, body, jnp.zeros((Q, 128), f32))
        return jnp.broadcast_to(jnp.sum(cnt, axis=1, keepdims=True), (Q, 128))

    def bit_body(b, carry):
        u, cacc = carry
        cand = u | lax.shift_left(jnp.int32(1), 31 - b)
        tvec = cand ^ jnp.int32(INT_MIN)
        tot = count_where(lambda k, kpos: k >= tvec)
        ok = tot >= kf
        return jnp.where(ok, cand, u), jnp.where(ok, tot, cacc)

    u, cacc = lax.fori_loop(0, 32, bit_body,
                            (jnp.zeros((Q, 128), i32), jnp.zeros((Q, 128), f32)))
    thr = jnp.maximum(u ^ jnp.int32(INT_MIN), INT_MIN + 1)
    overflow = jnp.where(u != 0, cacc, 0.0) > kf
    n_over = jnp.max(jnp.max(jnp.where(overflow, 1.0, 0.0), axis=1, keepdims=True),
                     axis=0, keepdims=True)[0, 0]

    @pl.when(n_over > 0.0)
    def _():
        need = kf - count_where(lambda k, kpos: k > thr)

        def cut_body(b, cut):
            cand = cut | lax.shift_left(jnp.int32(1), 14 - b)
            cnt = count_where(lambda k, kpos: jnp.where(k == thr, kpos, cand) < cand)
            return jnp.where(cnt <= need, cand, cut)

        cut = lax.fori_loop(0, 15, cut_body, jnp.zeros((Q, 128), i32))

        def drop_tile(j, carry):
            for c in range(NC):
                cols = slice(c * 128, (c + 1) * 128)
                kpos = tile_start(j) - KEY_PAD + c * 128 + lane
                k = sc_ref[j, :, cols]
                drop = jnp.where(k == thr, kpos, -1) >= cut
                sc_ref[j, :, cols] = jnp.where(drop, INT_MIN, k)
            return carry

        lax.fori_loop(0, nt, drop_tile, 0)

    m_ref[...] = jnp.full(m_ref.shape, NEG_BIG, f32)
    l_ref[...] = jnp.zeros(l_ref.shape, f32)
    acc_ref[...] = jnp.zeros(acc_ref.shape, f32)
    qlat = qlat_ref[0, 0]

    def attn_tile(j, with_bias):
        start = tile_start(j)
        kv = ckv_ref[0, pl.ds(start, T), :]
        big_ref[...] = _nt_dot(qlat, kv)
        mask_add = [jnp.where(sc_ref[j, :, c * 128:(c + 1) * 128] >= thr, 0.0, NEG_BIG)
                    for c in range(NC)]
        for h in range(A_HEADS):
            rows = slice(h * Q, (h + 1) * Q)
            lg = []
            for c in range(NC):
                x = big_ref[rows, c * 128:(c + 1) * 128] + mask_add[c]
                if with_bias:
                    x = x + bias0_ref[h, :, c * 128:(c + 1) * 128]
                lg.append(x)
            mx = lg[0]
            for c in range(1, NC):
                mx = jnp.maximum(mx, lg[c])
            m_prev = m_ref[rows, :]
            m_new = jnp.maximum(m_prev, jnp.max(mx, axis=1, keepdims=True))
            psum = jnp.zeros((Q, 128), f32)
            for c in range(NC):
                p = jnp.exp(lg[c] - m_new)
                psum = psum + p
                p_ref[rows, c * 128:(c + 1) * 128] = p.astype(bf16)
            alpha = jnp.exp(m_prev - m_new)
            l_ref[rows, :] = alpha * l_ref[rows, :] + jnp.sum(psum, axis=1, keepdims=True)
            m_ref[rows, :] = m_new
            alpha_ref[rows, :] = alpha
        acc_ref[...] = alpha_ref[...] * acc_ref[...] + jnp.dot(
            p_ref[...], kv, preferred_element_type=f32)

    attn_tile(0, True)

    def attn_body(j, carry):
        attn_tile(j, False)
        return carry

    lax.fori_loop(1, nt, attn_body, 0)

    inv_l = 1.0 / l_ref[...]
    for h in range(A_HEADS):
        rows = slice(h * Q, (h + 1) * Q)
        o_lat = (acc_ref[rows, :] * inv_l[rows, :]).astype(bf16)
        o_ref[0, :, h * A_HEAD_DIM:(h + 1) * A_HEAD_DIM] = jnp.dot(
            o_lat, wuv_ref[h], preferred_element_type=f32)


def dsa_pallas(cq, ckv, kw, g_cq, g_ckv, g_kidx, w_uq, w_iq, w_uk, w_uv, rel_bias):
    bsz, seq, _ = cq.shape
    f32, bf16 = jnp.float32, jnp.bfloat16
    Q, T, H = Q_BLOCK, KEY_TILE, A_HEADS
    nblk = seq // Q
    topk = min(IDX_TOPK_MAX, seq // 4)
    wiq2 = w_iq.reshape(A_Q_LORA, IDX_HEADS * IDX_DIM).astype(bf16)
    wuq2 = w_uq.reshape(A_Q_LORA, H * A_HEAD_DIM).astype(bf16)
    wukt = jnp.transpose(w_uk, (1, 2, 0)).astype(bf16)
    wuv = jnp.transpose(w_uv, (1, 0, 2)).astype(bf16)
    tok = lambda w: pl.BlockSpec((1, Q, w), lambda b, t: (b, t, 0))
    full = lambda shape: pl.BlockSpec(shape, lambda b, t: (0,) * len(shape))
    a_mat, qlat, ckvn, kidxn, wsc = pl.pallas_call(
        _dsa_prep_kernel,
        grid=(bsz, nblk),
        in_specs=[tok(A_Q_LORA), tok(A_KV_LORA), tok(128),
                  full((1, A_Q_LORA)), full((1, A_KV_LORA)), full((1, IDX_DIM)),
                  full(wiq2.shape), full(wuq2.shape), full(wukt.shape)],
        out_specs=[pl.BlockSpec((1, 1, H * Q, IDX_DIM), lambda b, t: (b, t, 0, 0)),
                   pl.BlockSpec((1, 1, H * Q, A_KV_LORA), lambda b, t: (b, t, 0, 0)),
                   tok(A_KV_LORA), tok(IDX_DIM),
                   pl.BlockSpec((1, 1, Q, IDX_HEADS), lambda b, t: (b, t, 0, 0))],
        out_shape=[jax.ShapeDtypeStruct((bsz, nblk, H * Q, IDX_DIM), bf16),
                   jax.ShapeDtypeStruct((bsz, nblk, H * Q, A_KV_LORA), bf16),
                   jax.ShapeDtypeStruct((bsz, seq, A_KV_LORA), bf16),
                   jax.ShapeDtypeStruct((bsz, seq, IDX_DIM), bf16),
                   jax.ShapeDtypeStruct((bsz, nblk, Q, IDX_HEADS), f32)],
        name="dsa_prep",
    )(cq, ckv, kw, g_cq.reshape(1, -1), g_ckv.reshape(1, -1), g_kidx.reshape(1, -1),
      wiq2, wuq2, wukt)
    ckvp = jnp.pad(ckvn, ((0, 0), (KEY_PAD, 0), (0, 0)))
    kidxp = jnp.pad(kidxn, ((0, 0), (KEY_PAD, 0), (0, 0)))
    rel = (jnp.arange(T, dtype=jnp.int32)[None, :] - jnp.arange(Q, dtype=jnp.int32)[:, None]
           - KEY_PAD)
    far = rel_bias[t5_bucket(jnp.int32(-KEY_PAD - 1))]
    bias0 = jnp.transpose(rel_bias[t5_bucket(rel)] - far, (2, 0, 1)).astype(f32)
    nt_max = (nblk - 1 + T // Q) // (T // Q)
    skey = seq + KEY_PAD
    out = pl.pallas_call(
        partial(_dsa_main_kernel, topk=topk),
        grid=(bsz, nblk),
        in_specs=[pl.BlockSpec((1, 1, H * Q, IDX_DIM), lambda b, t: (b, t, 0, 0)),
                  pl.BlockSpec((1, 1, H * Q, A_KV_LORA), lambda b, t: (b, t, 0, 0)),
                  pl.BlockSpec((1, 1, Q, IDX_HEADS), lambda b, t: (b, t, 0, 0)),
                  pl.BlockSpec((1, skey, IDX_DIM), lambda b, t: (b, 0, 0)),
                  pl.BlockSpec((1, skey, A_KV_LORA), lambda b, t: (b, 0, 0)),
                  full(bias0.shape), full(wuv.shape)],
        out_specs=tok(H * A_HEAD_DIM),
        out_shape=jax.ShapeDtypeStruct((bsz, seq, H * A_HEAD_DIM), f32),
        scratch_shapes=[pltpu.VMEM((nt_max, Q, T), jnp.int32),
                        pltpu.VMEM((H * Q, T), f32),
                        pltpu.VMEM((H * Q, T), bf16),
                        pltpu.VMEM((H * Q, A_KV_LORA), f32),
                        pltpu.VMEM((H * Q, 1), f32),
                        pltpu.VMEM((H * Q, 1), f32),
                        pltpu.VMEM((H * Q, 1), f32),
                        pltpu.VMEM((IDX_HEADS, Q, 128), f32)],
        compiler_params=pltpu.CompilerParams(
            dimension_semantics=("arbitrary", "arbitrary"), vmem_limit_bytes=VMEM_LIMIT),
        name="dsa_main",
    )(a_mat, qlat, wsc, kidxp, ckvp, bias0, wuv)
    return out


def _dsa_prep_t_kernel(cq_ref, ckv_ref, kw_ref, gcq_ref, gckv_ref, gk_ref,
                       wiqt_ref, wuqt_ref, wuk_ref,
                       at_ref, qlatt_ref, ckvn_ref, kidxn_ref):
    f32, bf16 = jnp.float32, jnp.bfloat16
    Q = Q_BLOCK
    kw = kw_ref[0]
    cqn = _rms(cq_ref[0], gcq_ref[...]).astype(bf16)
    qit = _nt_dot(wiqt_ref[...], cqn)
    qt = _nt_dot(wuqt_ref[...], cqn)
    for h in range(A_HEADS):
        cols = slice(h * Q, (h + 1) * Q)
        at_ref[0, 0, :, cols] = qit[h * IDX_DIM:(h + 1) * IDX_DIM, :].astype(bf16)
        qh = qt[h * A_HEAD_DIM:(h + 1) * A_HEAD_DIM, :].astype(bf16)
        ql = jnp.dot(wuk_ref[h], qh, preferred_element_type=f32) * ATTN_SCALE
        qlatt_ref[0, 0, :, cols] = ql.astype(bf16)
    ckvn_ref[0] = _rms(ckv_ref[0], gckv_ref[...]).astype(bf16)
    kidxn_ref[0] = _rms(kw[:, :IDX_DIM], gk_ref[...]).astype(bf16)


def _dsa_main_t_kernel(at_ref, qlatt_ref, wt_ref, kidx_ref, ckv_ref, bias0_ref, wuvt_ref, o_ref,
                       sc_ref, big_ref, p_ref, acc_ref, m_ref, l_ref, *, topk):
    f32, i32, bf16 = jnp.float32, jnp.int32, jnp.bfloat16
    Q, T = Q_BLOCK, KEY_TILE
    i = pl.program_id(1)
    e = (i + 1) * Q
    nt = (i + T // Q) // (T // Q)
    kf = jnp.float32(topk)
    wsc = wt_ref[0] * IDX_SCALE
    sub = lax.broadcasted_iota(i32, (T, Q), 0)
    qlane = lax.broadcasted_iota(i32, (T, Q), 1)
    limit = ((i * Q + qlane) // CHUNK + 1) * CHUNK
    at = at_ref[0, 0]

    def tile_start(j):
        return pl.multiple_of(e + KEY_PAD - (j + 1) * T, 128)

    def key_pos(j):
        return tile_start(j) - KEY_PAD + sub

    def score_tile(j, carry):
        kid = kidx_ref[0, pl.ds(tile_start(j), T), :]
        big_ref[...] = jnp.dot(kid, at, preferred_element_type=f32)
        s = jnp.zeros((T, Q), f32)
        for h in range(IDX_HEADS):
            s = s + wsc[h:h + 1, :] * jnp.maximum(big_ref[:, h * Q:(h + 1) * Q], 0.0)
        bits = lax.bitcast_convert_type(s, i32)
        key = jnp.where(bits < 0, bits ^ jnp.int32(0x7FFFFFFF), bits)
        key = jnp.where(s == 0.0, 0, key)
        kpos = key_pos(j)
        key = jnp.where(kpos >= 0, key, INT_MIN)
        sc_ref[j] = jnp.where(kpos < limit, key, INT_MIN)
        return carry

    lax.fori_loop(0, nt, score_tile, 0)

    def count_where(pred):
        def body(j, cnt):
            hit = jnp.where(pred(sc_ref[j], key_pos(j)), 1.0, 0.0)
            return cnt + jnp.sum(hit.reshape(T // 64, 64, Q), axis=0)
        cnt = lax.fori_loop(0, nt, body, jnp.zeros((64, Q), f32))
        return jnp.sum(cnt, axis=0, keepdims=True)

    def bit_body(b, carry):
        u, cacc = carry
        cand = u | lax.shift_left(jnp.int32(1), 31 - b)
        tvec = cand ^ jnp.int32(INT_MIN)
        tot = count_where(lambda k, kpos: k >= tvec)
        ok = tot >= kf
        return jnp.where(ok, cand, u), jnp.where(ok, tot, cacc)

    u, cacc = lax.fori_loop(0, 32, bit_body, (jnp.zeros((1, Q), i32), jnp.zeros((1, Q), f32)))
    thr = jnp.maximum(u ^ jnp.int32(INT_MIN), INT_MIN + 1)
    overflow = jnp.where(u != 0, cacc, 0.0) > kf
    n_over = jnp.max(jnp.where(overflow, 1.0, 0.0), axis=1, keepdims=True)[0, 0]

    @pl.when(n_over > 0.0)
    def _():
        need = kf - count_where(lambda k, kpos: k > thr)

        def cut_body(b, cut):
            cand = cut | lax.shift_left(jnp.int32(1), 14 - b)
            cnt = count_where(lambda k, kpos: jnp.where(k == thr, kpos, cand) < cand)
            return jnp.where(cnt <= need, cand, cut)

        cut = lax.fori_loop(0, 15, cut_body, jnp.zeros((1, Q), i32))

        def drop_tile(j, carry):
            k = sc_ref[j]
            drop = jnp.where(k == thr, key_pos(j), -1) >= cut
            sc_ref[j] = jnp.where(drop, INT_MIN, k)
            return carry

        lax.fori_loop(0, nt, drop_tile, 0)

    m_ref[...] = jnp.full(m_ref.shape, NEG_BIG, f32)
    l_ref[...] = jnp.zeros(l_ref.shape, f32)
    acc_ref[...] = jnp.zeros(acc_ref.shape, f32)
    qlatt = qlatt_ref[0, 0]

    def attn_tile(j, with_bias):
        kv = ckv_ref[0, pl.ds(tile_start(j), T), :]
        big_ref[...] = jnp.dot(kv, qlatt, preferred_element_type=f32)
        mask_add = jnp.where(sc_ref[j] >= thr, 0.0, NEG_BIG)
        alphas = []
        for h in range(A_HEADS):
            cols = slice(h * Q, (h + 1) * Q)
            x = big_ref[:, cols] + mask_add
            if with_bias:
                x = x + bias0_ref[h]
            m_prev = m_ref[:, cols]
            m_new = jnp.maximum(m_prev, jnp.max(x, axis=0, keepdims=True))
            p = jnp.exp(x - m_new)
            alpha = jnp.exp(m_prev - m_new)
            l_ref[:, cols] = alpha * l_ref[:, cols] + jnp.sum(p, axis=0, keepdims=True)
            m_ref[:, cols] = m_new
            p_ref[:, cols] = p.astype(bf16)
            alphas.append(alpha)
        pv = lax.dot_general(kv, p_ref[...], (((0,), (0,)), ((), ())),
                             preferred_element_type=f32)
        acc_ref[...] = jnp.concatenate(alphas, axis=1) * acc_ref[...] + pv

    attn_tile(0, True)

    def attn_body(j, carry):
        attn_tile(j, False)
        return carry

    lax.fori_loop(1, nt, attn_body, 0)

    inv_l = 1.0 / l_ref[...]
    outs = []
    for h in range(A_HEADS):
        cols = slice(h * Q, (h + 1) * Q)
        o_lat_t = (acc_ref[:, cols] * inv_l[:, cols]).astype(bf16)
        outs.append(jnp.dot(wuvt_ref[h], o_lat_t, preferred_element_type=f32))
    o_ref[0] = jnp.concatenate(outs, axis=0).T


def dsa_pallas_t(cq, ckv, kw, widx_t, g_cq, g_ckv, g_kidx, w_uq, w_iq, w_uk, w_uv, rel_bias):
    bsz, seq, _ = cq.shape
    f32, bf16 = jnp.float32, jnp.bfloat16
    Q, T, H = Q_BLOCK, KEY_TILE, A_HEADS
    nblk = seq // Q
    topk = min(IDX_TOPK_MAX, seq // 4)
    wiqt = w_iq.reshape(A_Q_LORA, IDX_HEADS * IDX_DIM).T.astype(bf16)
    wuqt = w_uq.reshape(A_Q_LORA, H * A_HEAD_DIM).T.astype(bf16)
    wuk = jnp.transpose(w_uk, (1, 0, 2)).astype(bf16)
    wuvt = jnp.transpose(w_uv, (1, 2, 0)).astype(bf16)
    tok = lambda w: pl.BlockSpec((1, Q, w), lambda b, t: (b, t, 0))
    full = lambda shape: pl.BlockSpec(shape, lambda b, t: (0,) * len(shape))
    blk = lambda r: pl.BlockSpec((1, 1, r, H * Q), lambda b, t: (b, t, 0, 0))
    a_t, qlat_t, ckvn, kidxn = pl.pallas_call(
        _dsa_prep_t_kernel,
        grid=(bsz, nblk),
        in_specs=[tok(A_Q_LORA), tok(A_KV_LORA), tok(128),
                  full((1, A_Q_LORA)), full((1, A_KV_LORA)), full((1, IDX_DIM)),
                  full(wiqt.shape), full(wuqt.shape), full(wuk.shape)],
        out_specs=[blk(IDX_DIM), blk(A_KV_LORA), tok(A_KV_LORA), tok(IDX_DIM)],
        out_shape=[jax.ShapeDtypeStruct((bsz, nblk, IDX_DIM, H * Q), bf16),
                   jax.ShapeDtypeStruct((bsz, nblk, A_KV_LORA, H * Q), bf16),
                   jax.ShapeDtypeStruct((bsz, seq, A_KV_LORA), bf16),
                   jax.ShapeDtypeStruct((bsz, seq, IDX_DIM), bf16)],
        name="dsa_prep",
    )(cq, ckv, kw, g_cq.reshape(1, -1), g_ckv.reshape(1, -1), g_kidx.reshape(1, -1),
      wiqt, wuqt, wuk)
    ckvp = jnp.pad(ckvn, ((0, 0), (KEY_PAD, 0), (0, 0)))
    kidxp = jnp.pad(kidxn, ((0, 0), (KEY_PAD, 0), (0, 0)))
    rel = (jnp.arange(T, dtype=jnp.int32)[:, None] - jnp.arange(Q, dtype=jnp.int32)[None, :]
           - KEY_PAD)
    far = rel_bias[t5_bucket(jnp.int32(-KEY_PAD - 1))]
    bias0 = jnp.transpose(rel_bias[t5_bucket(rel)] - far, (2, 0, 1)).astype(f32)
    nt_max = (nblk - 1 + T // Q) // (T // Q)
    skey = seq + KEY_PAD
    return pl.pallas_call(
        partial(_dsa_main_t_kernel, topk=topk),
        grid=(bsz, nblk),
        in_specs=[blk(IDX_DIM), blk(A_KV_LORA),
                  pl.BlockSpec((1, IDX_HEADS, Q), lambda b, t: (b, 1, t)),
                  pl.BlockSpec((1, skey, IDX_DIM), lambda b, t: (b, 0, 0)),
                  pl.BlockSpec((1, skey, A_KV_LORA), lambda b, t: (b, 0, 0)),
                  full(bias0.shape), full(wuvt.shape)],
        out_specs=tok(H * A_HEAD_DIM),
        out_shape=jax.ShapeDtypeStruct((bsz, seq, H * A_HEAD_DIM), f32),
        scratch_shapes=[pltpu.VMEM((nt_max, T, Q), jnp.int32),
                        pltpu.VMEM((T, H * Q), f32),
                        pltpu.VMEM((T, H * Q), bf16),
                        pltpu.VMEM((A_KV_LORA, H * Q), f32),
                        pltpu.VMEM((1, H * Q), f32),
                        pltpu.VMEM((1, H * Q), f32)],
        compiler_params=pltpu.CompilerParams(
            dimension_semantics=("arbitrary", "arbitrary"), vmem_limit_bytes=VMEM_LIMIT),
        name="dsa_main",
    )(a_t, qlat_t, widx_t, kidxp, ckvp, bias0, wuvt)


def causal_dwconv(x, w):
    ch = x.shape[-1]
    return lax.conv_general_dilated(x, w[:, None, :].astype(x.dtype), window_strides=(1,),
                                    padding=[(CONV_WIDTH - 1, 0)],
                                    dimension_numbers=('NWC', 'WIO', 'NWC'),
                                    feature_group_count=ch)


def gated_deltanet(q, k, v, z, a, b, conv_w, A_log, dt_bias, g_onorm):
    bsz, seq, _ = q.shape
    f32 = jnp.float32
    H, Dh = B_HEADS, B_HEAD_DIM
    nc = seq // CHUNK
    qkv = jax.nn.silu(causal_dwconv(jnp.concatenate([q, k, v], axis=-1), conv_w)).astype(f32)
    q, k, v = [t.reshape(bsz, seq, H, Dh) for t in jnp.split(qkv, 3, axis=-1)]
    q = l2norm(q) * (Dh ** -0.5)
    k = l2norm(k)
    beta = jax.nn.sigmoid(b.astype(f32))
    g = -jnp.exp(A_log.astype(f32)) * jax.nn.softplus(a.astype(f32) + dt_bias.astype(f32))

    def to_chunks(t):
        t = t.reshape(bsz, nc, CHUNK, *t.shape[2:])
        return jnp.moveaxis(t, 3, 1)

    q, k, v, beta = to_chunks(q), to_chunks(k), to_chunks(v), to_chunks(beta)
    g = jnp.cumsum(to_chunks(g), axis=-1)
    pos = jnp.arange(CHUNK)
    causal = pos[:, None] >= pos[None, :]
    strict = pos[:, None] > pos[None, :]
    decay = jnp.exp(jnp.where(causal, g[..., :, None] - g[..., None, :], -jnp.inf))
    m = jnp.where(strict, beta[..., :, None] * jnp.einsum('bhnid,bhnjd->bhnij', k, k) * decay, 0.0)
    rhs = jnp.concatenate([v * beta[..., None], k * (beta * jnp.exp(g))[..., None]], axis=-1)
    sol = lax.linalg.triangular_solve(m, rhs, left_side=True, lower=True, unit_diagonal=True)
    u, w = sol[..., :Dh], sol[..., Dh:]
    intra = jnp.einsum('bhnid,bhnjd->bhnij', q, k) * decay
    q_dec = q * jnp.exp(g)[..., None]
    k_tail = k * jnp.exp(g[..., -1:] - g)[..., None]
    chunk_decay = jnp.exp(g[..., -1])

    def step(state, inp):
        q_c, u_c, w_c, a_c, kt_c, d_c = inp
        v_new = u_c - jnp.einsum('bhcd,bhde->bhce', w_c, state)
        o_c = jnp.einsum('bhcd,bhde->bhce', q_c, state) + jnp.einsum('bhij,bhje->bhie', a_c, v_new)
        state = state * d_c[..., None, None] + jnp.einsum('bhcd,bhce->bhde', kt_c, v_new)
        return state, o_c

    xs = tuple(jnp.moveaxis(t, 2, 0) for t in (q_dec, u, w, intra, k_tail, chunk_decay))
    _, o = lax.scan(step, jnp.zeros((bsz, H, Dh, Dh), f32), xs)
    o = jnp.transpose(o, (1, 0, 3, 2, 4)).reshape(bsz, seq, H, Dh)
    o = o * lax.rsqrt(jnp.mean(o * o, axis=-1, keepdims=True) + EPS) * g_onorm.astype(f32)
    o = o * jax.nn.silu(z.astype(f32).reshape(bsz, seq, H, Dh))
    return o.reshape(bsz, seq, H * Dh).astype(z.dtype)


def memory_cross_attention(h, mem_n, wq, wk, wv, wo):
    q = jnp.einsum('bsd,dhe->bhse', h, wq)
    k = jnp.einsum('bmd,dhe->bhme', mem_n, wk)
    v = jnp.einsum('bmd,dhe->bhme', mem_n, wv)
    logits = jnp.einsum('bhse,bhme->bhsm', q, k).astype(jnp.float32) * (X_HEAD_DIM ** -0.5)
    p = jax.nn.softmax(logits, axis=-1).astype(v.dtype)
    o = jnp.einsum('bhsm,bhme->bshe', p, v)
    return jnp.einsum('bshe,hed->bsd', o, wo)


def peer_ffn(h, w_pq, sub_keys, u_emb, v_emb):
    bsz, seq, d = h.shape
    n_tok = bsz * seq
    hb = h.reshape(n_tok // P_TOKEN_BLOCK, P_TOKEN_BLOCK, d)

    def one_block(xb):
        n = xb.shape[0]
        qr = jnp.einsum('nd,dhe->nhe', xb, w_pq).reshape(n, P_HEADS, 2, P_QDIM // 2)
        s = jnp.einsum('nhpe,hpke->nhpk', qr, sub_keys).astype(jnp.float32)
        s1, i1 = lax.top_k(s[:, :, 0], P_TOPK)
        s2, i2 = lax.top_k(s[:, :, 1], P_TOPK)
        cand = (s1[..., :, None] + s2[..., None, :]).reshape(n, P_HEADS, P_TOPK * P_TOPK)
        cidx = (i1[..., :, None] * N_KEYS + i2[..., None, :]).reshape(n, P_HEADS, P_TOPK * P_TOPK)
        best, pos = lax.top_k(cand, P_TOPK)
        eidx = jnp.take_along_axis(cidx, pos, axis=-1)
        gate = jax.nn.softmax(best, axis=-1).astype(xb.dtype)
        act = jax.nn.gelu(jnp.einsum('nhkd,nd->nhk', u_emb[eidx], xb), approximate=False)
        return jnp.einsum('nhk,nhkd->nd', gate * act, v_emb[eidx])

    return lax.map(one_block, hb).reshape(bsz, seq, d)


PEER_SCORE_TOKENS = 256
PEER_GATHER_TOKENS = 64
PEER_SLOTS = P_HEADS * P_TOPK
ROW_VREG = (8, 128)
WORDS_PER_ROW = 4


def _top16(s, payload=None):
    n = s.shape[0]
    pos_f = lax.broadcasted_iota(jnp.int32, s.shape, 0).astype(jnp.float32)
    vals, picks = [], []
    for _ in range(P_TOPK):
        m = jnp.max(s, axis=0, keepdims=True)
        pos = jnp.min(jnp.where(s == m, pos_f, float(n)), axis=0, keepdims=True)
        hit = pos_f == pos
        vals.append(m)
        if payload is None:
            picks.append(pos)
        else:
            picks.append(jnp.max(jnp.where(hit, payload, -1.0), axis=0, keepdims=True))
        s = jnp.where(hit, -jnp.inf, s)
    return vals, picks


def _peer_score_kernel(x_ref, g_ref, wpqt_ref, sk_ref, hn_ref, eidx_ref, gate_ref):
    f32, bf16 = jnp.float32, jnp.bfloat16
    hb = _rms(x_ref[...], g_ref[...]).astype(bf16)
    hn_ref[...] = hb
    qrt = _nt_dot(wpqt_ref[...], hb)
    half = P_QDIM // 2
    e_rows, g_rows = [], []
    for hd in range(P_HEADS):
        tops = []
        for p in range(2):
            qhp = qrt[(hd * 2 + p) * half:(hd * 2 + p + 1) * half, :].astype(bf16)
            s = jnp.dot(sk_ref[hd * 2 + p], qhp, preferred_element_type=f32)
            tops.append(_top16(s))
        (v1, i1), (v2, i2) = tops
        v2m = jnp.concatenate(v2, axis=0)
        i2m = jnp.concatenate(i2, axis=0)
        cand = jnp.concatenate([v1[a] + v2m for a in range(P_TOPK)], axis=0)
        cidx = jnp.concatenate([i1[a] * float(N_KEYS) + i2m for a in range(P_TOPK)], axis=0)
        best, be = _top16(cand, cidx)
        ex = [jnp.exp(b - best[0]) for b in best]
        den = ex[0]
        for k in range(1, P_TOPK):
            den = den + ex[k]
        inv = 1.0 / den
        e_rows += be
        g_rows += [x * inv for x in ex]
    eidx_ref[...] = (jnp.concatenate(e_rows, axis=0).T * float(WORDS_PER_ROW)).astype(jnp.int32)
    gate_ref[...] = jnp.concatenate(g_rows, axis=0).T


def _diag_mask():
    r = lax.broadcasted_iota(jnp.int32, (8, PEER_SLOTS * 8), 0)
    c = lax.broadcasted_iota(jnp.int32, (8, PEER_SLOTS * 8), 1)
    return (c & 7) == r


def _gather_rows(idx_ref, tab_ref, g_ref, t):
    for r in range(PEER_SLOTS):
        row0 = pl.multiple_of(idx_ref[t, r], WORDS_PER_ROW)
        g_ref[r * WORDS_PER_ROW:(r + 1) * WORDS_PER_ROW, :] = tab_ref[
            pl.ds(row0, WORDS_PER_ROW), :]


def _peer_act_kernel(idx_ref, hn_ref, tab_ref, a_ref, ga_ref, gb_ref, m_ref):
    f32, bf16 = jnp.float32, jnp.bfloat16
    tb = a_ref.shape[0]

    def pair_body(tp, carry):
        hp = hn_ref[pl.ds(pl.multiple_of(tp * 16, 16), 16), :]
        for u, g_ref in enumerate((ga_ref, gb_ref)):
            t = tp * 2 + u
            _gather_rows(idx_ref, tab_ref, g_ref, t)
            g = pltpu.bitcast(g_ref[...], bf16)
            m = _nt_dot(hp, g)
            m_ref[pl.ds(pl.multiple_of(t * 8, 8), 8), :] = m[u * 8:(u + 1) * 8, :]
        return carry

    lax.fori_loop(0, tb // 2, pair_body, 0)
    m3 = m_ref[...].reshape(tb, 8, PEER_SLOTS * 8)
    z = jnp.sum(jnp.where(_diag_mask()[None], m3, 0.0), axis=1)
    rr = lax.broadcasted_iota(jnp.int32, (PEER_SLOTS * 8, PEER_SLOTS), 0)
    cc = lax.broadcasted_iota(jnp.int32, (PEER_SLOTS * 8, PEER_SLOTS), 1)
    pool = jnp.where((rr >> 3) == cc, 1.0, 0.0).astype(bf16)
    z_hi = z.astype(bf16)
    z_lo = (z - z_hi.astype(f32)).astype(bf16)
    a_ref[...] = (jnp.dot(z_hi, pool, preferred_element_type=f32)
                  + jnp.dot(z_lo, pool, preferred_element_type=f32))


def _peer_out_kernel(idx_ref, a_ref, gate_ref, tab_ref, o_ref, ga_ref, gb_ref, w_ref):
    f32, bf16 = jnp.float32, jnp.bfloat16
    tb = a_ref.shape[0]
    a = a_ref[...]
    act = 0.5 * a * (1.0 + lax.erf(a * (2.0 ** -0.5)))
    wgt = (gate_ref[...] * act).astype(bf16)
    rr = lax.broadcasted_iota(jnp.int32, (PEER_SLOTS, PEER_SLOTS * 8), 0)
    cc = lax.broadcasted_iota(jnp.int32, (PEER_SLOTS, PEER_SLOTS * 8), 1)
    expand = jnp.where((cc >> 3) == rr, 1.0, 0.0).astype(bf16)
    w_ref[...] = jnp.dot(wgt, expand, preferred_element_type=f32)
    diag = _diag_mask()

    def pair_body(tp, carry):
        for u, g_ref in enumerate((ga_ref, gb_ref)):
            t = tp * 2 + u
            _gather_rows(idx_ref, tab_ref, g_ref, t)
            g = pltpu.bitcast(g_ref[...], bf16)
            wrow = jnp.broadcast_to(w_ref[pl.ds(t, 1), :], (8, PEER_SLOTS * 8))
            wsel = jnp.where(diag, wrow, 0.0).astype(bf16)
            o_ref[pl.ds(pl.multiple_of(t * 8, 8), 8), :] = jnp.dot(
                wsel, g, preferred_element_type=f32)
        return carry

    lax.fori_loop(0, tb // 2, pair_body, 0)


def _pack_table(tab):
    n_e = tab.shape[0]
    tb = lax.bitcast_convert_type(
        tab.astype(jnp.bfloat16).reshape(n_e, WORDS_PER_ROW, 2, 128), jnp.uint16).astype(jnp.uint32)
    word = tb[:, :, 0, :] | (tb[:, :, 1, :] << 16)
    return lax.bitcast_convert_type(word, jnp.int32).reshape(n_e * WORDS_PER_ROW, 128)


def peer_pallas(x, g_ffn, w_pq, sub_keys, u_emb, v_emb):
    n_tok, d = x.shape
    f32, bf16 = jnp.float32, jnp.bfloat16
    ts, tg = PEER_SCORE_TOKENS, PEER_GATHER_TOKENS
    wpqt = w_pq.reshape(d, P_HEADS * P_QDIM).T.astype(bf16)
    sk = sub_keys.reshape(P_HEADS * 2, N_KEYS, P_QDIM // 2).astype(bf16)
    hn, eidx, gate = pl.pallas_call(
        _peer_score_kernel,
        grid=(n_tok // ts,),
        in_specs=[pl.BlockSpec((ts, d), lambda i: (i, 0)),
                  pl.BlockSpec((1, d), lambda i: (0, 0)),
                  pl.BlockSpec(wpqt.shape, lambda i: (0, 0)),
                  pl.BlockSpec(sk.shape, lambda i: (0, 0, 0))],
        out_specs=[pl.BlockSpec((ts, d), lambda i: (i, 0)),
                   pl.BlockSpec((ts, PEER_SLOTS), lambda i: (i, 0)),
                   pl.BlockSpec((ts, PEER_SLOTS), lambda i: (i, 0))],
        out_shape=[jax.ShapeDtypeStruct((n_tok, d), bf16),
                   jax.ShapeDtypeStruct((n_tok, PEER_SLOTS), jnp.int32),
                   jax.ShapeDtypeStruct((n_tok, PEER_SLOTS), f32)],
        compiler_params=pltpu.CompilerParams(vmem_limit_bytes=VMEM_LIMIT),
        name="peer_score",
    )(x, g_ffn.reshape(1, d), wpqt, sk)
    utab, vtab = _pack_table(u_emb), _pack_table(v_emb)
    idx_spec = pl.BlockSpec((tg, PEER_SLOTS), lambda i: (i, 0), memory_space=pltpu.SMEM)
    tab_spec = pl.BlockSpec(memory_space=pltpu.VMEM)
    slot_spec = pl.BlockSpec((tg, PEER_SLOTS), lambda i: (i, 0))
    row_spec = pl.BlockSpec((tg * 8, 128), lambda i: (i, 0))
    gbuf = pltpu.VMEM((PEER_SLOTS * WORDS_PER_ROW, 128), jnp.int32)
    cparams = pltpu.CompilerParams(vmem_limit_bytes=VMEM_LIMIT)
    act = pl.pallas_call(
        _peer_act_kernel,
        grid=(n_tok // tg,),
        in_specs=[idx_spec, row_spec, tab_spec],
        out_specs=slot_spec,
        out_shape=jax.ShapeDtypeStruct((n_tok, PEER_SLOTS), f32),
        scratch_shapes=[gbuf, gbuf, pltpu.VMEM((tg * 8, PEER_SLOTS * 8), f32)],
        compiler_params=cparams,
        name="peer_act",
    )(eidx, hn.reshape(n_tok * 8, 128), utab)
    out = pl.pallas_call(
        _peer_out_kernel,
        grid=(n_tok // tg,),
        in_specs=[idx_spec, slot_spec, slot_spec, tab_spec],
        out_specs=row_spec,
        out_shape=jax.ShapeDtypeStruct((n_tok * 8, 128), f32),
        scratch_shapes=[gbuf, gbuf, pltpu.VMEM((tg, PEER_SLOTS * 8), f32)],
        compiler_params=cparams,
        name="peer_out",
    )(eidx, act, gate, vtab)
    return out.reshape(n_tok, d)


def _final_kernel(x_ref, y_ref, g_ref, o_ref):
    o_ref[...] = _rms(x_ref[...] + y_ref[...], g_ref[...])


def final_residual_rmsnorm(x, y, g):
    n, d = x.shape
    tm = 512
    row = pl.BlockSpec((tm, d), lambda i: (i, 0))
    return pl.pallas_call(
        _final_kernel,
        grid=(n // tm,),
        in_specs=[row, row, pl.BlockSpec((1, d), lambda i: (0, 0))],
        out_specs=row,
        out_shape=jax.ShapeDtypeStruct((n, d), x.dtype),
        name="final_rmsnorm",
    )(x, y, g.reshape(1, d))


IN_PROJ_TOKENS = 512
GDN_COLS = 3 * B_QK
GATE_ROWS = 8


def _in_proj_kernel(x_ref, g_ref, wa_ref, wkw_ref, wqkv_ref, wz_ref, wab_ref, wabt_ref,
                    cq_ref, ckv_ref, kw_ref, qkv_ref, z_ref, ab_ref, abt_ref):
    f32 = jnp.float32
    hb = _rms(x_ref[0], g_ref[...]).astype(jnp.bfloat16)
    a = jnp.dot(hb, wa_ref[...], preferred_element_type=f32)
    cq_ref[0] = a[:, :A_Q_LORA]
    ckv_ref[0] = a[:, A_Q_LORA:]
    kw_ref[0] = jnp.dot(hb, wkw_ref[...], preferred_element_type=f32)
    qkv_ref[0] = jnp.dot(hb, wqkv_ref[...], preferred_element_type=f32)
    z_ref[0] = jnp.dot(hb, wz_ref[...], preferred_element_type=f32)
    ab_ref[0] = jnp.dot(hb, wab_ref[...], preferred_element_type=f32)
    abt_ref[0] = _nt_dot(wabt_ref[...], hb)


def in_proj_pallas(x, g_mix, w_in):
    bsz, seq, d = x.shape
    f32, bf16 = jnp.float32, jnp.bfloat16
    tm = min(IN_PROJ_TOKENS, seq)
    o = np.cumsum((0,) + COL_WIDTHS)
    wb = w_in.astype(bf16)
    pad_cols = lambda w: jnp.pad(w, ((0, 0), (0, 128 - w.shape[1])))
    wa = wb[:, o[0]:o[2]]
    wkw = pad_cols(wb[:, o[2]:o[4]])
    wqkv = wb[:, o[4]:o[7]]
    wz = wb[:, o[7]:o[8]]
    wab = pad_cols(wb[:, o[8]:o[10]])
    wabt = jnp.concatenate([wb[:, o[8]:o[10]], wb[:, o[3]:o[4]]], axis=1).T
    full = lambda w: pl.BlockSpec(w.shape, lambda b, t: (0, 0))
    tok = lambda w: pl.BlockSpec((1, tm, w), lambda b, t: (b, t, 0))
    shp = lambda w: jax.ShapeDtypeStruct((bsz, seq, w), f32)
    return pl.pallas_call(
        _in_proj_kernel,
        grid=(bsz, seq // tm),
        in_specs=[tok(d), pl.BlockSpec((1, d), lambda b, t: (0, 0)),
                  full(wa), full(wkw), full(wqkv), full(wz), full(wab), full(wabt)],
        out_specs=[tok(A_Q_LORA), tok(A_KV_LORA), tok(128), tok(GDN_COLS), tok(B_QK), tok(128),
                   pl.BlockSpec((1, GATE_ROWS + IDX_HEADS, tm), lambda b, t: (b, 0, t))],
        out_shape=[shp(A_Q_LORA), shp(A_KV_LORA), shp(128), shp(GDN_COLS), shp(B_QK), shp(128),
                   jax.ShapeDtypeStruct((bsz, GATE_ROWS + IDX_HEADS, seq), f32)],
        compiler_params=pltpu.CompilerParams(vmem_limit_bytes=VMEM_LIMIT),
        name="in_proj",
    )(x, g_mix.reshape(1, d), wa, wkw, wqkv, wz, wab, wabt)


def _softplus(x):
    return jnp.maximum(x, 0.0) + jnp.log1p(jnp.exp(-jnp.abs(x)))


def _sigmoid(x):
    return 1.0 / (1.0 + jnp.exp(-x))


def _gdn_gates(pre, a_log, dt_bias, is_decay):
    g = -jnp.exp(a_log) * _softplus(pre + dt_bias)
    return jnp.where(is_decay, g, _sigmoid(pre))


def _gdn_prep_kernel(qkv_ref, halo_ref, cw_ref, ab_ref, abt_ref, alc_ref, dtc_ref, alr_ref, dtr_ref,
                     q_ref, k_ref, v_ref, gc_ref, gr_ref):
    tm = qkv_ref.shape[1]
    x = qkv_ref[0]
    halo = jnp.where(pl.program_id(1) > 0, halo_ref[0], 0.0)
    full = jnp.concatenate([halo, x], axis=0)
    y = x * cw_ref[CONV_WIDTH - 1:CONV_WIDTH, :]
    for back in range(1, CONV_WIDTH):
        shifted = pltpu.roll(full, back, axis=0)[8:, :]
        y = y + shifted * cw_ref[CONV_WIDTH - 1 - back:CONV_WIDTH - back, :]
    y = y * _sigmoid(y)
    for h in range(B_HEADS):
        cols = slice(h * B_HEAD_DIM, (h + 1) * B_HEAD_DIM)
        qh = y[:, h * B_HEAD_DIM:(h + 1) * B_HEAD_DIM]
        kh = y[:, B_QK + h * B_HEAD_DIM:B_QK + (h + 1) * B_HEAD_DIM]
        q_ref[0, :, cols] = qh * lax.rsqrt(
            jnp.sum(qh * qh, axis=-1, keepdims=True) + EPS) * (B_HEAD_DIM ** -0.5)
        k_ref[0, :, cols] = kh * lax.rsqrt(jnp.sum(kh * kh, axis=-1, keepdims=True) + EPS)
    v_ref[0] = y[:, 2 * B_QK:]
    lane = lax.broadcasted_iota(jnp.int32, (tm, 128), 1)
    gates_c = _gdn_gates(ab_ref[0], alc_ref[...], dtc_ref[...], lane < B_HEADS)
    row = lax.broadcasted_iota(jnp.int32, (GATE_ROWS, tm), 0)
    gates_r = _gdn_gates(abt_ref[0], alr_ref[...], dtr_ref[...], row < B_HEADS)
    ti = lax.broadcasted_iota(jnp.int32, (tm, tm), 0)
    tj = lax.broadcasted_iota(jnp.int32, (tm, tm), 1)
    same_chunk = (ti // CHUNK) == (tj // CHUNK)
    hi = lax.Precision.HIGHEST
    lower = jnp.where(same_chunk & (tj <= ti), 1.0, 0.0)
    upper = jnp.where(same_chunk & (ti <= tj), 1.0, 0.0)
    cum_c = jnp.dot(lower, gates_c, preferred_element_type=jnp.float32, precision=hi)
    cum_r = jnp.dot(gates_r, upper, preferred_element_type=jnp.float32, precision=hi)
    gc_ref[0] = jnp.where(lane < B_HEADS, cum_c, gates_c)
    gr_ref[0] = jnp.where(row < B_HEADS, cum_r, gates_r)


def _gdn_main_kernel(q_ref, k_ref, v_ref, z_ref, gc_ref, gr_ref, gon_ref, o_ref, s_ref):
    f32, bf16 = jnp.float32, jnp.bfloat16
    C, Dh = CHUNK, B_HEAD_DIM
    hi = lax.Precision.HIGHEST

    @pl.when(pl.program_id(1) == 0)
    def _():
        s_ref[...] = jnp.zeros(s_ref.shape, f32)

    ii = lax.broadcasted_iota(jnp.int32, (C, C), 0)
    jj = lax.broadcasted_iota(jnp.int32, (C, C), 1)
    causal = ii >= jj
    strict = ii > jj
    eye = jnp.where(ii == jj, 1.0, 0.0)
    mm = lambda a, b: jnp.dot(a, b, preferred_element_type=f32)
    mmh = lambda a, b: jnp.dot(a, b, preferred_element_type=f32, precision=hi)
    n_chunks = q_ref.shape[1] // C
    units = []
    for c in range(n_chunks):
        rows = slice(c * C, (c + 1) * C)
        gates_c = gc_ref[0, rows, :]
        gates_r = gr_ref[0, :, c * C:(c + 1) * C]
        for h in range(B_HEADS):
            cols = slice(h * Dh, (h + 1) * Dh)
            gcum = jnp.broadcast_to(gates_c[:, h:h + 1], (C, Dh))
            beta = jnp.broadcast_to(gates_c[:, B_HEADS + h:B_HEADS + h + 1], (C, Dh))
            gcum_r = jnp.broadcast_to(gates_r[h:h + 1, :], (C, C))
            decay = jnp.where(causal, jnp.exp(jnp.minimum(gcum[:, :C] - gcum_r, 0.0)), 0.0)
            q, k, v = q_ref[0, rows, cols], k_ref[0, rows, cols], v_ref[0, rows, cols]
            qb, kb = q.astype(bf16), k.astype(bf16)
            kk = _nt_dot(kb, kb)
            qk = _nt_dot(qb, kb)
            egc = jnp.exp(gcum)
            g_last = gcum[C - 1:C, :]
            units.append(dict(
                c=c, h=h, rows=rows, cols=cols,
                neg_m=jnp.where(strict, -(beta[:, :C] * kk * decay), 0.0),
                rhs=jnp.concatenate([v * beta, k * (beta * egc)], axis=1),
                q_dec=(q * egc).astype(bf16), intra=(qk * decay).astype(bf16),
                k_tail=(k * jnp.exp(g_last - gcum)).astype(bf16), chunk_decay=jnp.exp(g_last)))
    powers = [un["neg_m"] for un in units]
    t_inv = [eye + p for p in powers]
    for _ in range(5):
        powers = [mmh(p, p) for p in powers]
        t_inv = [t + mmh(p, t) for p, t in zip(powers, t_inv)]
    sols = [mmh(t, un["rhs"]) for t, un in zip(t_inv, units)]
    states = [s_ref[h] for h in range(B_HEADS)]
    for c in range(n_chunks):
        group = [(un, sol) for un, sol in zip(units, sols) if un["c"] == c]
        sbs = [states[un["h"]].astype(bf16) for un, _ in group]
        vbs = [(sol[:, :Dh] - mm(sol[:, Dh:].astype(bf16), sb)).astype(bf16)
               for (un, sol), sb in zip(group, sbs)]
        outs = [mm(un["q_dec"], sb) + mm(un["intra"], vb)
                for (un, _), sb, vb in zip(group, sbs, vbs)]
        for (un, _), vb in zip(group, vbs):
            states[un["h"]] = states[un["h"]] * un["chunk_decay"] + lax.dot_general(
                un["k_tail"], vb, (((0,), (0,)), ((), ())), preferred_element_type=f32)
        for (un, _), o in zip(group, outs):
            o = o * lax.rsqrt(jnp.mean(o * o, axis=-1, keepdims=True) + EPS) * gon_ref[...]
            zz = z_ref[0, un["rows"], un["cols"]]
            o_ref[0, un["rows"], un["cols"]] = o * (zz * _sigmoid(zz))
    for h in range(B_HEADS):
        s_ref[h] = states[h]


def gdn_pallas(qkv, z, ab, abt, conv_w, a_log, dt_bias, g_onorm):
    bsz, seq, _ = qkv.shape
    f32 = jnp.float32
    tm = min(256, seq)
    zero4 = jnp.zeros((B_HEADS,), f32)
    lane_row = lambda v: jnp.pad(jnp.concatenate([v.astype(f32), zero4]), (0, 120)).reshape(1, 128)
    sub_col = lambda v: jnp.concatenate([v.astype(f32), zero4]).reshape(GATE_ROWS, 1)
    tok = lambda w, t=tm: pl.BlockSpec((1, t, w), lambda b, i: (b, i, 0))
    const = lambda shape: pl.BlockSpec(shape, lambda b, i: (0,) * len(shape))
    shp = lambda w: jax.ShapeDtypeStruct((bsz, seq, w), f32)
    q, k, v, gc, gr = pl.pallas_call(
        _gdn_prep_kernel,
        grid=(bsz, seq // tm),
        in_specs=[tok(GDN_COLS),
                  pl.BlockSpec((1, 8, GDN_COLS), lambda b, i: (b, jnp.maximum(i * (tm // 8) - 1, 0), 0)),
                  const((CONV_WIDTH, GDN_COLS)), tok(128),
                  pl.BlockSpec((1, GATE_ROWS, tm), lambda b, i: (b, 0, i)),
                  const((1, 128)), const((1, 128)), const((GATE_ROWS, 1)), const((GATE_ROWS, 1))],
        out_specs=[tok(B_QK), tok(B_QK), tok(B_QK), tok(128),
                   pl.BlockSpec((1, GATE_ROWS, tm), lambda b, i: (b, 0, i))],
        out_shape=[shp(B_QK), shp(B_QK), shp(B_QK), shp(128),
                   jax.ShapeDtypeStruct((bsz, GATE_ROWS, seq), f32)],
        compiler_params=pltpu.CompilerParams(vmem_limit_bytes=VMEM_LIMIT),
        name="gdn_prep",
    )(qkv, qkv, conv_w.astype(f32), ab, abt, lane_row(a_log), lane_row(dt_bias),
      sub_col(a_log), sub_col(dt_bias))
    ts = 2 * CHUNK
    return pl.pallas_call(
        _gdn_main_kernel,
        grid=(bsz, seq // ts),
        in_specs=[tok(B_QK, ts), tok(B_QK, ts), tok(B_QK, ts), tok(B_QK, ts), tok(128, ts),
                  pl.BlockSpec((1, GATE_ROWS, ts), lambda b, i: (b, 0, i)),
                  const((1, B_HEAD_DIM))],
        out_specs=tok(B_QK, ts),
        out_shape=shp(B_QK),
        scratch_shapes=[pltpu.VMEM((B_HEADS, B_HEAD_DIM, B_HEAD_DIM), f32)],
        compiler_params=pltpu.CompilerParams(dimension_semantics=("arbitrary", "arbitrary")),
        name="gdn_main",
    )(q, k, v, z, gc, gr, g_onorm.astype(f32).reshape(1, B_HEAD_DIM))


def _mem_kv_kernel(mem_ref, g_ref, wk_ref, wv_ref, k_ref, v_ref):
    f32, bf16 = jnp.float32, jnp.bfloat16
    mn = _rms(mem_ref[0], g_ref[...]).astype(bf16)
    k = jnp.dot(mn, wk_ref[...], preferred_element_type=f32)
    v = jnp.dot(mn, wv_ref[...], preferred_element_type=f32)
    for h in range(X_HEADS):
        cols = slice(h * X_HEAD_DIM, (h + 1) * X_HEAD_DIM)
        k_ref[0, h] = k[:, cols].astype(bf16)
        v_ref[0, h] = v[:, cols].astype(bf16)


def _mid_kernel(x_ref, oa_ref, ob_ref, wo_ref, gx_ref, wq_ref, k_ref, v_ref, wox_ref, o_ref):
    f32, bf16 = jnp.float32, jnp.bfloat16
    na = oa_ref.shape[2]
    x1 = (x_ref[0]
          + jnp.dot(oa_ref[0].astype(bf16), wo_ref[:na, :], preferred_element_type=f32)
          + jnp.dot(ob_ref[0].astype(bf16), wo_ref[na:, :], preferred_element_type=f32))
    hq = _rms(x1, gx_ref[...]).astype(bf16)
    q = jnp.dot(hq, wq_ref[...], preferred_element_type=f32)
    heads = []
    for h in range(X_HEADS):
        qh = q[:, h * X_HEAD_DIM:(h + 1) * X_HEAD_DIM].astype(bf16)
        lg = _nt_dot(qh, k_ref[0, h]) * (X_HEAD_DIM ** -0.5)
        p = jnp.exp(lg - jnp.max(lg, axis=-1, keepdims=True))
        p = (p / jnp.sum(p, axis=-1, keepdims=True)).astype(bf16)
        heads.append(jnp.dot(p, v_ref[0, h], preferred_element_type=f32).astype(bf16))
    o = jnp.concatenate(heads, axis=1)
    o_ref[0] = x1 + jnp.dot(o, wox_ref[...], preferred_element_type=f32)


def mid_pallas(x, o_a, o_b, w_out, g_cross, mem, g_mem, wq, wk, wv, wo):
    bsz, seq, d = x.shape
    f32, bf16 = jnp.float32, jnp.bfloat16
    hx = X_HEADS * X_HEAD_DIM
    m_len = mem.shape[1]
    const2 = lambda shape: pl.BlockSpec(shape, lambda b: (0,) * len(shape))
    kv_spec = pl.BlockSpec((1, X_HEADS, m_len, X_HEAD_DIM), lambda b: (b, 0, 0, 0))
    kv_shape = jax.ShapeDtypeStruct((bsz, X_HEADS, m_len, X_HEAD_DIM), bf16)
    k, v = pl.pallas_call(
        _mem_kv_kernel,
        grid=(bsz,),
        in_specs=[pl.BlockSpec((1, m_len, d), lambda b: (b, 0, 0)), const2((1, d)),
                  const2((d, hx)), const2((d, hx))],
        out_specs=[kv_spec, kv_spec],
        out_shape=[kv_shape, kv_shape],
        name="mem_kv",
    )(mem, g_mem.reshape(1, d), wk.reshape(d, hx).astype(bf16), wv.reshape(d, hx).astype(bf16))
    tm = min(256, seq)
    tok = lambda w: pl.BlockSpec((1, tm, w), lambda b, t: (b, t, 0))
    const = lambda shape: pl.BlockSpec(shape, lambda b, t: (0,) * len(shape))
    kv_spec2 = pl.BlockSpec((1, X_HEADS, m_len, X_HEAD_DIM), lambda b, t: (b, 0, 0, 0))
    return pl.pallas_call(
        _mid_kernel,
        grid=(bsz, seq // tm),
        in_specs=[tok(d), tok(o_a.shape[2]), tok(o_b.shape[2]), const((MIX_WIDTH, d)),
                  const((1, d)), const((d, hx)), kv_spec2, kv_spec2, const((hx, d))],
        out_specs=tok(d),
        out_shape=jax.ShapeDtypeStruct((bsz, seq, d), f32),
        compiler_params=pltpu.CompilerParams(vmem_limit_bytes=VMEM_LIMIT),
        name="mid",
    )(x, o_a, o_b, w_out.astype(bf16), g_cross.reshape(1, d), wq.reshape(d, hx).astype(bf16),
      k, v, wo.reshape(hx, d).astype(bf16))


def kernel(x, mem, g_mix, w_in, g_cq, g_ckv, g_kidx, w_uq, w_iq, w_uk, w_uv, rel_bias, conv_w, A_log, dt_bias, g_onorm, w_out, g_cross, g_mem, wq_x, wk_x, wv_x, wo_x, g_ffn, w_pq, sub_keys, u_emb, v_emb, g_final):
    bsz, seq, d = x.shape
    for l in range(DEPTH):
        cq, ckv, kw, qkv, z, ab, abt = in_proj_pallas(x, g_mix[l], w_in[l])
        o_a = dsa_pallas_t(cq, ckv, kw, abt, g_cq[l], g_ckv[l], g_kidx[l],
                           w_uq[l], w_iq[l], w_uk[l], w_uv[l], rel_bias)
        o_b = gdn_pallas(qkv, z, ab, abt, conv_w[l], A_log[l], dt_bias[l], g_onorm[l])
        x = mid_pallas(x, o_a, o_b, w_out[l], g_cross[l], mem, g_mem[l],
                       wq_x[l], wk_x[l], wv_x[l], wo_x[l])
        xf = x.reshape(bsz * seq, d)
        y = peer_pallas(xf, g_ffn[l], w_pq[l], sub_keys[l], u_emb[l], v_emb[l])
        if l + 1 < DEPTH:
            x = (xf + y).reshape(bsz, seq, d)
    return final_residual_rmsnorm(xf, y, g_final).reshape(bsz, seq, d)
```

```python
import math
from functools import partial
import jax
import jax.numpy as jnp
from jax import lax
import numpy as np
from jax.experimental import pallas as pl
from jax.experimental.pallas import tpu as pltpu

D_MODEL = 1024
BATCH = 4
SEQ = 8192
DEPTH = 1

CHUNK = 64
Q_BLOCK = 128
EPS = 1e-6

A_HEADS = 8
A_HEAD_DIM = 64
A_Q_LORA = 256
A_KV_LORA = 256
IDX_HEADS = 8
IDX_DIM = 64
IDX_TOPK_MAX = 256
ATTN_SCALE = A_HEAD_DIM ** -0.5
IDX_SCALE = (IDX_HEADS * IDX_DIM) ** -0.5

B_HEADS = 4
B_HEAD_DIM = 128
B_QK = B_HEADS * B_HEAD_DIM
CONV_WIDTH = 4

REL_BUCKETS = 32
REL_MAX_DIST = 128

MEM_LEN = 256
X_HEADS = 4
X_HEAD_DIM = 128

P_HEADS = 8
N_KEYS = 128
N_EXPERTS = N_KEYS * N_KEYS
P_TOPK = 16
P_QDIM = 256
P_TOKEN_BLOCK = 128

COL_WIDTHS = (A_Q_LORA, A_KV_LORA, IDX_DIM, IDX_HEADS, B_QK, B_QK, B_QK, B_QK, B_HEADS, B_HEADS)
IN_COLS = sum(COL_WIDTHS)
MIX_WIDTH = A_HEADS * A_HEAD_DIM + B_HEADS * B_HEAD_DIM


def rmsnorm(x, g):
    xf = x.astype(jnp.float32)
    y = xf * lax.rsqrt(jnp.mean(xf * xf, axis=-1, keepdims=True) + EPS)
    return (y * g.astype(jnp.float32)).astype(x.dtype)


def l2norm(x):
    return x * lax.rsqrt(jnp.sum(x * x, axis=-1, keepdims=True) + EPS)


def split_columns(p, widths):
    out, start = [], 0
    for w in widths:
        out.append(p[..., start:start + w])
        start += w
    return out


def t5_bucket(rel):
    half = REL_BUCKETS // 2
    max_exact = half // 2
    n = jnp.abs(rel)
    nf = jnp.maximum(n, max_exact).astype(jnp.float32)
    large = max_exact + (jnp.log(nf / max_exact) / math.log(REL_MAX_DIST / max_exact)
                         * (half - max_exact)).astype(jnp.int32)
    large = jnp.minimum(large, half - 1)
    return jnp.where(rel > 0, half, 0) + jnp.where(n < max_exact, n, large)


def dsa_attention(cq, ckv, kidx, widx, g_cq, g_ckv, g_kidx, w_uq, w_iq, w_uk, w_uv, rel_bias):
    bsz, seq, _ = cq.shape
    topk = min(IDX_TOPK_MAX, seq // 4)
    n_blocks = seq // Q_BLOCK
    cq = rmsnorm(cq, g_cq)
    ckv = rmsnorm(ckv, g_ckv)
    kidx = rmsnorm(kidx, g_kidx).astype(jnp.float32)
    key_pos = jnp.arange(seq, dtype=jnp.int32)

    def blockify(a):
        return jnp.moveaxis(a.reshape(bsz, n_blocks, Q_BLOCK, a.shape[-1]), 1, 0)

    def one_block(args):
        cq_b, w_b, start = args
        qpos = start + jnp.arange(Q_BLOCK, dtype=jnp.int32)
        limit = (qpos // CHUNK + 1) * CHUNK
        admissible = key_pos[None, :] < limit[:, None]
        qi = jnp.einsum('bqc,chd->bqhd', cq_b, w_iq).astype(jnp.float32)
        dots = jax.nn.relu(jnp.einsum('bqhd,bsd->bqhs', qi, kidx))
        score = jnp.einsum('bqh,bqhs->bqs', w_b.astype(jnp.float32) * IDX_SCALE, dots)
        score = jnp.where(admissible[None], score, -jnp.inf)
        _, sel = lax.top_k(score, topk)
        valid = sel < limit[None, :, None]
        c_sel = jax.vmap(lambda c, i: c[i])(ckv, sel)
        q = jnp.einsum('bqc,chd->bqhd', cq_b, w_uq)
        q_lat = jnp.einsum('bqhd,chd->bqhc', q, w_uk)
        logits = jnp.einsum('bqhc,bqkc->bhqk', q_lat, c_sel).astype(jnp.float32) * ATTN_SCALE
        bias = rel_bias[t5_bucket(sel - qpos[None, :, None])].astype(jnp.float32)
        logits = logits + jnp.transpose(bias, (0, 3, 1, 2))
        logits = jnp.where(valid[:, None], logits, -jnp.inf)
        p = jax.nn.softmax(logits, axis=-1).astype(c_sel.dtype)
        o_lat = jnp.einsum('bhqk,bqkc->bqhc', p, c_sel)
        o = jnp.einsum('bqhc,chd->bqhd', o_lat, w_uv)
        return o.reshape(bsz, Q_BLOCK, A_HEADS * A_HEAD_DIM)

    starts = jnp.arange(n_blocks, dtype=jnp.int32) * Q_BLOCK
    o = lax.map(one_block, (blockify(cq), blockify(widx), starts))
    return jnp.moveaxis(o, 0, 1).reshape(bsz, seq, A_HEADS * A_HEAD_DIM)


INT_MIN = -2147483648
NEG_BIG = -1e30
KEY_TILE = 512
KEY_PAD = KEY_TILE - Q_BLOCK
VMEM_LIMIT = 56 * 1024 * 1024


def _rms(x, g):
    return x * lax.rsqrt(jnp.mean(x * x, axis=-1, keepdims=True) + EPS) * g


def _nt_dot(a, b):
    return lax.dot_general(a, b, (((1,), (1,)), ((), ())), preferred_element_type=jnp.float32)


def _dsa_prep_kernel(cq_ref, ckv_ref, kw_ref, gcq_ref, gckv_ref, gk_ref,
                     wiq_ref, wuq_ref, wukt_ref,
                     a_ref, qlat_ref, ckvn_ref, kidxn_ref, w_ref):
    bf16 = jnp.bfloat16
    kw = kw_ref[0]
    cqn = _rms(cq_ref[0], gcq_ref[...]).astype(bf16)
    qi = jnp.dot(cqn, wiq_ref[...], preferred_element_type=jnp.float32)
    q = jnp.dot(cqn, wuq_ref[...], preferred_element_type=jnp.float32)
    for h in range(A_HEADS):
        rows = slice(h * Q_BLOCK, (h + 1) * Q_BLOCK)
        a_ref[0, 0, rows, :] = qi[:, h * IDX_DIM:(h + 1) * IDX_DIM].astype(bf16)
        qh = q[:, h * A_HEAD_DIM:(h + 1) * A_HEAD_DIM].astype(bf16)
        ql = jnp.dot(qh, wukt_ref[h], preferred_element_type=jnp.float32) * ATTN_SCALE
        qlat_ref[0, 0, rows, :] = ql.astype(bf16)
    ckvn_ref[0] = _rms(ckv_ref[0], gckv_ref[...]).astype(bf16)
    kidxn_ref[0] = _rms(kw[:, :IDX_DIM], gk_ref[...]).astype(bf16)
    w_ref[0, 0] = kw[:, IDX_DIM:IDX_DIM + IDX_HEADS] * IDX_SCALE


def _dsa_main_kernel(a_ref, qlat_ref, w_ref, kidx_ref, ckv_ref, bias0_ref, wuv_ref, o_ref,
                     sc_ref, big_ref, p_ref, acc_ref, m_ref, l_ref, alpha_ref, wb_ref, *, topk):
    f32, i32, bf16 = jnp.float32, jnp.int32, jnp.bfloat16
    Q, T = Q_BLOCK, KEY_TILE
    NC = T // 128
    i = pl.program_id(1)
    e = (i + 1) * Q
    nt = (i + T // Q) // (T // Q)
    kf = jnp.float32(topk)

    w_blk = w_ref[0, 0]
    for h in range(IDX_HEADS):
        wb_ref[h] = jnp.broadcast_to(w_blk[:, h:h + 1], (Q, 128))
    row = lax.broadcasted_iota(i32, (Q, 128), 0)
    lane = lax.broadcasted_iota(i32, (Q, 128), 1)
    limit = ((i * Q + row) // CHUNK + 1) * CHUNK
    a_mat = a_ref[0, 0]

    def tile_start(j):
        return pl.multiple_of(e + KEY_PAD - (j + 1) * T, 128)

    def score_tile(j, carry):
        start = tile_start(j)
        big_ref[...] = _nt_dot(a_mat, kidx_ref[0, pl.ds(start, T), :])
        for c in range(NC):
            cols = slice(c * 128, (c + 1) * 128)
            s = jnp.zeros((Q, 128), f32)
            for h in range(IDX_HEADS):
                s = s + wb_ref[h] * jnp.maximum(big_ref[h * Q:(h + 1) * Q, cols], 0.0)
            bits = lax.bitcast_convert_type(s, i32)
            key = jnp.where(bits < 0, bits ^ jnp.int32(0x7FFFFFFF), bits)
            key = jnp.where(s == 0.0, 0, key)
            kpos = start - KEY_PAD + c * 128 + lane
            key = jnp.where(kpos >= 0, key, INT_MIN)
            key = jnp.where(kpos < limit, key, INT_MIN)
            sc_ref[j, :, cols] = key
        return carry

    lax.fori_loop(0, nt, score_tile, 0)

    def count_where(pred):
        def body(j, cnt):
            for c in range(NC):
                cols = slice(c * 128, (c + 1) * 128)
                kpos = tile_start(j) - KEY_PAD + c * 128 + lane
                cnt = cnt + jnp.where(pred(sc_ref[j, :, cols], kpos), 1.0, 0.0)
            return cnt
        cnt = lax.fori_loop(0, nt, body, jnp.zeros((Q, 128), f32))
        return jnp.broadcast_to(jnp.sum(cnt, axis=1, keepdims=True), (Q, 128))

    def bit_body(b, carry):
        u, cacc = carry
        cand = u | lax.shift_left(jnp.int32(1), 31 - b)
        tvec = cand ^ jnp.int32(INT_MIN)
        tot = count_where(lambda k, kpos: k >= tvec)
        ok = tot >= kf
        return jnp.where(ok, cand, u), jnp.where(ok, tot, cacc)

    u, cacc = lax.fori_loop(0, 32, bit_body,
                            (jnp.zeros((Q, 128), i32), jnp.zeros((Q, 128), f32)))
    thr = jnp.maximum(u ^ jnp.int32(INT_MIN), INT_MIN + 1)
    overflow = jnp.where(u != 0, cacc, 0.0) > kf
    n_over = jnp.max(jnp.max(jnp.where(overflow, 1.0, 0.0), axis=1, keepdims=True),
                     axis=0, keepdims=True)[0, 0]

    @pl.when(n_over > 0.0)
    def _():
        need = kf - count_where(lambda k, kpos: k > thr)

        def cut_body(b, cut):
            cand = cut | lax.shift_left(jnp.int32(1), 14 - b)
            cnt = count_where(lambda k, kpos: jnp.where(k == thr, kpos, cand) < cand)
            return jnp.where(cnt <= need, cand, cut)

        cut = lax.fori_loop(0, 15, cut_body, jnp.zeros((Q, 128), i32))

        def drop_tile(j, carry):
            for c in range(NC):
                cols = slice(c * 128, (c + 1) * 128)
                kpos = tile_start(j) - KEY_PAD + c * 128 + lane
                k = sc_ref[j, :, cols]
                drop = jnp.where(k == thr, kpos, -1) >= cut
                sc_ref[j, :, cols] = jnp.where(drop, INT_MIN, k)
            return carry

        lax.fori_loop(0, nt, drop_tile, 0)

    m_ref[...] = jnp.full(m_ref.shape, NEG_BIG, f32)
    l_ref[...] = jnp.zeros(l_ref.shape, f32)
    acc_ref[...] = jnp.zeros(acc_ref.shape, f32)
    qlat = qlat_ref[0, 0]

    def attn_tile(j, with_bias):
        start = tile_start(j)
        kv = ckv_ref[0, pl.ds(start, T), :]
        big_ref[...] = _nt_dot(qlat, kv)
        mask_add = [jnp.where(sc_ref[j, :, c * 128:(c + 1) * 128] >= thr, 0.0, NEG_BIG)
                    for c in range(NC)]
        for h in range(A_HEADS):
            rows = slice(h * Q, (h + 1) * Q)
            lg = []
            for c in range(NC):
                x = big_ref[rows, c * 128:(c + 1) * 128] + mask_add[c]
                if with_bias:
                    x = x + bias0_ref[h, :, c * 128:(c + 1) * 128]
                lg.append(x)
            mx = lg[0]
            for c in range(1, NC):
                mx = jnp.maximum(mx, lg[c])
            m_prev = m_ref[rows, :]
            m_new = jnp.maximum(m_prev, jnp.max(mx, axis=1, keepdims=True))
            psum = jnp.zeros((Q, 128), f32)
            for c in range(NC):
                p = jnp.exp(lg[c] - m_new)
                psum = psum + p
                p_ref[rows, c * 128:(c + 1) * 128] = p.astype(bf16)
            alpha = jnp.exp(m_prev - m_new)
            l_ref[rows, :] = alpha * l_ref[rows, :] + jnp.sum(psum, axis=1, keepdims=True)
            m_ref[rows, :] = m_new
            alpha_ref[rows, :] = alpha
        acc_ref[...] = alpha_ref[...] * acc_ref[...] + jnp.dot(
            p_ref[...], kv, preferred_element_type=f32)

    attn_tile(0, True)

    def attn_body(j, carry):
        attn_tile(j, False)
        return carry

    lax.fori_loop(1, nt, attn_body, 0)

    inv_l = 1.0 / l_ref[...]
    for h in range(A_HEADS):
        rows = slice(h * Q, (h + 1) * Q)
        o_lat = (acc_ref[rows, :] * inv_l[rows, :]).astype(bf16)
        o_ref[0, :, h * A_HEAD_DIM:(h + 1) * A_HEAD_DIM] = jnp.dot(
            o_lat, wuv_ref[h], preferred_element_type=f32)


def dsa_pallas(cq, ckv, kw, g_cq, g_ckv, g_kidx, w_uq, w_iq, w_uk, w_uv, rel_bias):
    bsz, seq, _ = cq.shape
    f32, bf16 = jnp.float32, jnp.bfloat16
    Q, T, H = Q_BLOCK, KEY_TILE, A_HEADS
    nblk = seq // Q
    topk = min(IDX_TOPK_MAX, seq // 4)
    wiq2 = w_iq.reshape(A_Q_LORA, IDX_HEADS * IDX_DIM).astype(bf16)
    wuq2 = w_uq.reshape(A_Q_LORA, H * A_HEAD_DIM).astype(bf16)
    wukt = jnp.transpose(w_uk, (1, 2, 0)).astype(bf16)
    wuv = jnp.transpose(w_uv, (1, 0, 2)).astype(bf16)
    tok = lambda w: pl.BlockSpec((1, Q, w), lambda b, t: (b, t, 0))
    full = lambda shape: pl.BlockSpec(shape, lambda b, t: (0,) * len(shape))
    a_mat, qlat, ckvn, kidxn, wsc = pl.pallas_call(
        _dsa_prep_kernel,
        grid=(bsz, nblk),
        in_specs=[tok(A_Q_LORA), tok(A_KV_LORA), tok(128),
                  full((1, A_Q_LORA)), full((1, A_KV_LORA)), full((1, IDX_DIM)),
                  full(wiq2.shape), full(wuq2.shape), full(wukt.shape)],
        out_specs=[pl.BlockSpec((1, 1, H * Q, IDX_DIM), lambda b, t: (b, t, 0, 0)),
                   pl.BlockSpec((1, 1, H * Q, A_KV_LORA), lambda b, t: (b, t, 0, 0)),
                   tok(A_KV_LORA), tok(IDX_DIM),
                   pl.BlockSpec((1, 1, Q, IDX_HEADS), lambda b, t: (b, t, 0, 0))],
        out_shape=[jax.ShapeDtypeStruct((bsz, nblk, H * Q, IDX_DIM), bf16),
                   jax.ShapeDtypeStruct((bsz, nblk, H * Q, A_KV_LORA), bf16),
                   jax.ShapeDtypeStruct((bsz, seq, A_KV_LORA), bf16),
                   jax.ShapeDtypeStruct((bsz, seq, IDX_DIM), bf16),
                   jax.ShapeDtypeStruct((bsz, nblk, Q, IDX_HEADS), f32)],
        name="dsa_prep",
    )(cq, ckv, kw, g_cq.reshape(1, -1), g_ckv.reshape(1, -1), g_kidx.reshape(1, -1),
      wiq2, wuq2, wukt)
    ckvp = jnp.pad(ckvn, ((0, 0), (KEY_PAD, 0), (0, 0)))
    kidxp = jnp.pad(kidxn, ((0, 0), (KEY_PAD, 0), (0, 0)))
    rel = (jnp.arange(T, dtype=jnp.int32)[None, :] - jnp.arange(Q, dtype=jnp.int32)[:, None]
           - KEY_PAD)
    far = rel_bias[t5_bucket(jnp.int32(-KEY_PAD - 1))]
    bias0 = jnp.transpose(rel_bias[t5_bucket(rel)] - far, (2, 0, 1)).astype(f32)
    nt_max = (nblk - 1 + T // Q) // (T // Q)
    skey = seq + KEY_PAD
    out = pl.pallas_call(
        partial(_dsa_main_kernel, topk=topk),
        grid=(bsz, nblk),
        in_specs=[pl.BlockSpec((1, 1, H * Q, IDX_DIM), lambda b, t: (b, t, 0, 0)),
                  pl.BlockSpec((1, 1, H * Q, A_KV_LORA), lambda b, t: (b, t, 0, 0)),
                  pl.BlockSpec((1, 1, Q, IDX_HEADS), lambda b, t: (b, t, 0, 0)),
                  pl.BlockSpec((1, skey, IDX_DIM), lambda b, t: (b, 0, 0)),
                  pl.BlockSpec((1, skey, A_KV_LORA), lambda b, t: (b, 0, 0)),
                  full(bias0.shape), full(wuv.shape)],
        out_specs=tok(H * A_HEAD_DIM),
        out_shape=jax.ShapeDtypeStruct((bsz, seq, H * A_HEAD_DIM), f32),
        scratch_shapes=[pltpu.VMEM((nt_max, Q, T), jnp.int32),
                        pltpu.VMEM((H * Q, T), f32),
                        pltpu.VMEM((H * Q, T), bf16),
                        pltpu.VMEM((H * Q, A_KV_LORA), f32),
                        pltpu.VMEM((H * Q, 1), f32),
                        pltpu.VMEM((H * Q, 1), f32),
                        pltpu.VMEM((H * Q, 1), f32),
                        pltpu.VMEM((IDX_HEADS, Q, 128), f32)],
        compiler_params=pltpu.CompilerParams(
            dimension_semantics=("arbitrary", "arbitrary"), vmem_limit_bytes=VMEM_LIMIT),
        name="dsa_main",
    )(a_mat, qlat, wsc, kidxp, ckvp, bias0, wuv)
    return out


def _dsa_prep_t_kernel(cq_ref, ckv_ref, kw_ref, gcq_ref, gckv_ref, gk_ref,
                       wiqt_ref, wuqt_ref, wuk_ref,
                       at_ref, qlatt_ref, ckvn_ref, kidxn_ref):
    f32, bf16 = jnp.float32, jnp.bfloat16
    Q = Q_BLOCK
    kw = kw_ref[0]
    cqn = _rms(cq_ref[0], gcq_ref[...]).astype(bf16)
    qit = _nt_dot(wiqt_ref[...], cqn)
    qt = _nt_dot(wuqt_ref[...], cqn)
    for h in range(A_HEADS):
        cols = slice(h * Q, (h + 1) * Q)
        at_ref[0, 0, :, cols] = qit[h * IDX_DIM:(h + 1) * IDX_DIM, :].astype(bf16)
        qh = qt[h * A_HEAD_DIM:(h + 1) * A_HEAD_DIM, :].astype(bf16)
        ql = jnp.dot(wuk_ref[h], qh, preferred_element_type=f32) * ATTN_SCALE
        qlatt_ref[0, 0, :, cols] = ql.astype(bf16)
    ckvn_ref[0] = _rms(ckv_ref[0], gckv_ref[...]).astype(bf16)
    kidxn_ref[0] = _rms(kw[:, :IDX_DIM], gk_ref[...]).astype(bf16)


def _dsa_main_t_kernel(at_ref, qlatt_ref, wt_ref, kidx_ref, ckv_ref, bias0_ref, wuvt_ref, o_ref,
                       sc_ref, big_ref, p_ref, acc_ref, m_ref, l_ref, *, topk):
    f32, i32, bf16 = jnp.float32, jnp.int32, jnp.bfloat16
    Q, T = Q_BLOCK, KEY_TILE
    i = pl.program_id(1)
    e = (i + 1) * Q
    nt = (i + T // Q) // (T // Q)
    kf = jnp.float32(topk)
    wsc = wt_ref[0] * IDX_SCALE
    sub = lax.broadcasted_iota(i32, (T, Q), 0)
    qlane = lax.broadcasted_iota(i32, (T, Q), 1)
    limit = ((i * Q + qlane) // CHUNK + 1) * CHUNK
    at = at_ref[0, 0]

    def tile_start(j):
        return pl.multiple_of(e + KEY_PAD - (j + 1) * T, 128)

    def key_pos(j):
        return tile_start(j) - KEY_PAD + sub

    def score_tile(j, carry):
        kid = kidx_ref[0, pl.ds(tile_start(j), T), :]
        big_ref[...] = jnp.dot(kid, at, preferred_element_type=f32)
        s = jnp.zeros((T, Q), f32)
        for h in range(IDX_HEADS):
            s = s + wsc[h:h + 1, :] * jnp.maximum(big_ref[:, h * Q:(h + 1) * Q], 0.0)
        bits = lax.bitcast_convert_type(s, i32)
        key = jnp.where(bits < 0, bits ^ jnp.int32(0x7FFFFFFF), bits)
        key = jnp.where(s == 0.0, 0, key)
        kpos = key_pos(j)
        key = jnp.where(kpos >= 0, key, INT_MIN)
        sc_ref[j] = jnp.where(kpos < limit, key, INT_MIN)
        return carry

    lax.fori_loop(0, nt, score_tile, 0)

    def count_where(pred):
        def body(j, cnt):
            hit = jnp.where(pred(sc_ref[j], key_pos(j)), 1.0, 0.0)
            return cnt + jnp.sum(hit.reshape(T // 64, 64, Q), axis=0)
        cnt = lax.fori_loop(0, nt, body, jnp.zeros((64, Q), f32))
        return jnp.sum(cnt, axis=0, keepdims=True)

    def bit_body(b, carry):
        u, cacc = carry
        cand = u | lax.shift_left(jnp.int32(1), 31 - b)
        tvec = cand ^ jnp.int32(INT_MIN)
        tot = count_where(lambda k, kpos: k >= tvec)
        ok = tot >= kf
        return jnp.where(ok, cand, u), jnp.where(ok, tot, cacc)

    u, cacc = lax.fori_loop(0, 32, bit_body, (jnp.zeros((1, Q), i32), jnp.zeros((1, Q), f32)))
    thr = jnp.maximum(u ^ jnp.int32(INT_MIN), INT_MIN + 1)
    overflow = jnp.where(u != 0, cacc, 0.0) > kf
    n_over = jnp.max(jnp.where(overflow, 1.0, 0.0), axis=1, keepdims=True)[0, 0]

    @pl.when(n_over > 0.0)
    def _():
        need = kf - count_where(lambda k, kpos: k > thr)

        def cut_body(b, cut):
            cand = cut | lax.shift_left(jnp.int32(1), 14 - b)
            cnt = count_where(lambda k, kpos: jnp.where(k == thr, kpos, cand) < cand)
            return jnp.where(cnt <= need, cand, cut)

        cut = lax.fori_loop(0, 15, cut_body, jnp.zeros((1, Q), i32))

        def drop_tile(j, carry):
            k = sc_ref[j]
            drop = jnp.where(k == thr, key_pos(j), -1) >= cut
            sc_ref[j] = jnp.where(drop, INT_MIN, k)
            return carry

        lax.fori_loop(0, nt, drop_tile, 0)

    m_ref[...] = jnp.full(m_ref.shape, NEG_BIG, f32)
    l_ref[...] = jnp.zeros(l_ref.shape, f32)
    acc_ref[...] = jnp.zeros(acc_ref.shape, f32)
    qlatt = qlatt_ref[0, 0]

    def attn_tile(j, with_bias):
        kv = ckv_ref[0, pl.ds(tile_start(j), T), :]
        big_ref[...] = jnp.dot(kv, qlatt, preferred_element_type=f32)
        mask_add = jnp.where(sc_ref[j] >= thr, 0.0, NEG_BIG)
        alphas = []
        for h in range(A_HEADS):
            cols = slice(h * Q, (h + 1) * Q)
            x = big_ref[:, cols] + mask_add
            if with_bias:
                x = x + bias0_ref[h]
            m_prev = m_ref[:, cols]
            m_new = jnp.maximum(m_prev, jnp.max(x, axis=0, keepdims=True))
            p = jnp.exp(x - m_new)
            alpha = jnp.exp(m_prev - m_new)
            l_ref[:, cols] = alpha * l_ref[:, cols] + jnp.sum(p, axis=0, keepdims=True)
            m_ref[:, cols] = m_new
            p_ref[:, cols] = p.astype(bf16)
            alphas.append(alpha)
        pv = lax.dot_general(kv, p_ref[...], (((0,), (0,)), ((), ())),
                             preferred_element_type=f32)
        acc_ref[...] = jnp.concatenate(alphas, axis=1) * acc_ref[...] + pv

    attn_tile(0, True)

    def attn_body(j, carry):
        attn_tile(j, False)
        return carry

    lax.fori_loop(1, nt, attn_body, 0)

    inv_l = 1.0 / l_ref[...]
    outs = []
    for h in range(A_HEADS):
        cols = slice(h * Q, (h + 1) * Q)
        o_lat_t = (acc_ref[:, cols] * inv_l[:, cols]).astype(bf16)
        outs.append(jnp.dot(wuvt_ref[h], o_lat_t, preferred_element_type=f32))
    o_ref[0] = jnp.concatenate(outs, axis=0).T


def dsa_pallas_t(cq, ckv, kw, widx_t, g_cq, g_ckv, g_kidx, w_uq, w_iq, w_uk, w_uv, rel_bias):
    bsz, seq, _ = cq.shape
    f32, bf16 = jnp.float32, jnp.bfloat16
    Q, T, H = Q_BLOCK, KEY_TILE, A_HEADS
    nblk = seq // Q
    topk = min(IDX_TOPK_MAX, seq // 4)
    wiqt = w_iq.reshape(A_Q_LORA, IDX_HEADS * IDX_DIM).T.astype(bf16)
    wuqt = w_uq.reshape(A_Q_LORA, H * A_HEAD_DIM).T.astype(bf16)
    wuk = jnp.transpose(w_uk, (1, 0, 2)).astype(bf16)
    wuvt = jnp.transpose(w_uv, (1, 2, 0)).astype(bf16)
    tok = lambda w: pl.BlockSpec((1, Q, w), lambda b, t: (b, t, 0))
    full = lambda shape: pl.BlockSpec(shape, lambda b, t: (0,) * len(shape))
    blk = lambda r: pl.BlockSpec((1, 1, r, H * Q), lambda b, t: (b, t, 0, 0))
    a_t, qlat_t, ckvn, kidxn = pl.pallas_call(
        _dsa_prep_t_kernel,
        grid=(bsz, nblk),
        in_specs=[tok(A_Q_LORA), tok(A_KV_LORA), tok(128),
                  full((1, A_Q_LORA)), full((1, A_KV_LORA)), full((1, IDX_DIM)),
                  full(wiqt.shape), full(wuqt.shape), full(wuk.shape)],
        out_specs=[blk(IDX_DIM), blk(A_KV_LORA), tok(A_KV_LORA), tok(IDX_DIM)],
        out_shape=[jax.ShapeDtypeStruct((bsz, nblk, IDX_DIM, H * Q), bf16),
                   jax.ShapeDtypeStruct((bsz, nblk, A_KV_LORA, H * Q), bf16),
                   jax.ShapeDtypeStruct((bsz, seq, A_KV_LORA), bf16),
                   jax.ShapeDtypeStruct((bsz, seq, IDX_DIM), bf16)],
        name="dsa_prep",
    )(cq, ckv, kw, g_cq.reshape(1, -1), g_ckv.reshape(1, -1), g_kidx.reshape(1, -1),
      wiqt, wuqt, wuk)
    ckvp = jnp.pad(ckvn, ((0, 0), (KEY_PAD, 0), (0, 0)))
    kidxp = jnp.pad(kidxn, ((0, 0), (KEY_PAD, 0), (0, 0)))
    rel = (jnp.arange(T, dtype=jnp.int32)[:, None] - jnp.arange(Q, dtype=jnp.int32)[None, :]
           - KEY_PAD)
    far = rel_bias[t5_bucket(jnp.int32(-KEY_PAD - 1))]
    bias0 = jnp.transpose(rel_bias[t5_bucket(rel)] - far, (2, 0, 1)).astype(f32)
    nt_max = (nblk - 1 + T // Q) // (T // Q)
    skey = seq + KEY_PAD
    return pl.pallas_call(
        partial(_dsa_main_t_kernel, topk=topk),
        grid=(bsz, nblk),
        in_specs=[blk(IDX_DIM), blk(A_KV_LORA),
                  pl.BlockSpec((1, IDX_HEADS, Q), lambda b, t: (b, 1, t)),
                  pl.BlockSpec((1, skey, IDX_DIM), lambda b, t: (b, 0, 0)),
                  pl.BlockSpec((1, skey, A_KV_LORA), lambda b, t: (b, 0, 0)),
                  full(bias0.shape), full(wuvt.shape)],
        out_specs=tok(H * A_HEAD_DIM),
        out_shape=jax.ShapeDtypeStruct((bsz, seq, H * A_HEAD_DIM), f32),
        scratch_shapes=[pltpu.VMEM((nt_max, T, Q), jnp.int32),
                        pltpu.VMEM((T, H * Q), f32),
                        pltpu.VMEM((T, H * Q), bf16),
                        pltpu.VMEM((A_KV_LORA, H * Q), f32),
                        pltpu.VMEM((1, H * Q), f32),
                        pltpu.VMEM((1, H * Q), f32)],
        compiler_params=pltpu.CompilerParams(
            dimension_semantics=("arbitrary", "arbitrary"), vmem_limit_bytes=VMEM_LIMIT),
        name="dsa_main",
    )(a_t, qlat_t, widx_t, kidxp, ckvp, bias0, wuvt)


def causal_dwconv(x, w):
    ch = x.shape[-1]
    return lax.conv_general_dilated(x, w[:, None, :].astype(x.dtype), window_strides=(1,),
                                    padding=[(CONV_WIDTH - 1, 0)],
                                    dimension_numbers=('NWC', 'WIO', 'NWC'),
                                    feature_group_count=ch)


def gated_deltanet(q, k, v, z, a, b, conv_w, A_log, dt_bias, g_onorm):
    bsz, seq, _ = q.shape
    f32 = jnp.float32
    H, Dh = B_HEADS, B_HEAD_DIM
    nc = seq // CHUNK
    qkv = jax.nn.silu(causal_dwconv(jnp.concatenate([q, k, v], axis=-1), conv_w)).astype(f32)
    q, k, v = [t.reshape(bsz, seq, H, Dh) for t in jnp.split(qkv, 3, axis=-1)]
    q = l2norm(q) * (Dh ** -0.5)
    k = l2norm(k)
    beta = jax.nn.sigmoid(b.astype(f32))
    g = -jnp.exp(A_log.astype(f32)) * jax.nn.softplus(a.astype(f32) + dt_bias.astype(f32))

    def to_chunks(t):
        t = t.reshape(bsz, nc, CHUNK, *t.shape[2:])
        return jnp.moveaxis(t, 3, 1)

    q, k, v, beta = to_chunks(q), to_chunks(k), to_chunks(v), to_chunks(beta)
    g = jnp.cumsum(to_chunks(g), axis=-1)
    pos = jnp.arange(CHUNK)
    causal = pos[:, None] >= pos[None, :]
    strict = pos[:, None] > pos[None, :]
    decay = jnp.exp(jnp.where(causal, g[..., :, None] - g[..., None, :], -jnp.inf))
    m = jnp.where(strict, beta[..., :, None] * jnp.einsum('bhnid,bhnjd->bhnij', k, k) * decay, 0.0)
    rhs = jnp.concatenate([v * beta[..., None], k * (beta * jnp.exp(g))[..., None]], axis=-1)
    sol = lax.linalg.triangular_solve(m, rhs, left_side=True, lower=True, unit_diagonal=True)
    u, w = sol[..., :Dh], sol[..., Dh:]
    intra = jnp.einsum('bhnid,bhnjd->bhnij', q, k) * decay
    q_dec = q * jnp.exp(g)[..., None]
    k_tail = k * jnp.exp(g[..., -1:] - g)[..., None]
    chunk_decay = jnp.exp(g[..., -1])

    def step(state, inp):
        q_c, u_c, w_c, a_c, kt_c, d_c = inp
        v_new = u_c - jnp.einsum('bhcd,bhde->bhce', w_c, state)
        o_c = jnp.einsum('bhcd,bhde->bhce', q_c, state) + jnp.einsum('bhij,bhje->bhie', a_c, v_new)
        state = state * d_c[..., None, None] + jnp.einsum('bhcd,bhce->bhde', kt_c, v_new)
        return state, o_c

    xs = tuple(jnp.moveaxis(t, 2, 0) for t in (q_dec, u, w, intra, k_tail, chunk_decay))
    _, o = lax.scan(step, jnp.zeros((bsz, H, Dh, Dh), f32), xs)
    o = jnp.transpose(o, (1, 0, 3, 2, 4)).reshape(bsz, seq, H, Dh)
    o = o * lax.rsqrt(jnp.mean(o * o, axis=-1, keepdims=True) + EPS) * g_onorm.astype(f32)
    o = o * jax.nn.silu(z.astype(f32).reshape(bsz, seq, H, Dh))
    return o.reshape(bsz, seq, H * Dh).astype(z.dtype)


def memory_cross_attention(h, mem_n, wq, wk, wv, wo):
    q = jnp.einsum('bsd,dhe->bhse', h, wq)
    k = jnp.einsum('bmd,dhe->bhme', mem_n, wk)
    v = jnp.einsum('bmd,dhe->bhme', mem_n, wv)
    logits = jnp.einsum('bhse,bhme->bhsm', q, k).astype(jnp.float32) * (X_HEAD_DIM ** -0.5)
    p = jax.nn.softmax(logits, axis=-1).astype(v.dtype)
    o = jnp.einsum('bhsm,bhme->bshe', p, v)
    return jnp.einsum('bshe,hed->bsd', o, wo)


def peer_ffn(h, w_pq, sub_keys, u_emb, v_emb):
    bsz, seq, d = h.shape
    n_tok = bsz * seq
    hb = h.reshape(n_tok // P_TOKEN_BLOCK, P_TOKEN_BLOCK, d)

    def one_block(xb):
        n = xb.shape[0]
        qr = jnp.einsum('nd,dhe->nhe', xb, w_pq).reshape(n, P_HEADS, 2, P_QDIM // 2)
        s = jnp.einsum('nhpe,hpke->nhpk', qr, sub_keys).astype(jnp.float32)
        s1, i1 = lax.top_k(s[:, :, 0], P_TOPK)
        s2, i2 = lax.top_k(s[:, :, 1], P_TOPK)
        cand = (s1[..., :, None] + s2[..., None, :]).reshape(n, P_HEADS, P_TOPK * P_TOPK)
        cidx = (i1[..., :, None] * N_KEYS + i2[..., None, :]).reshape(n, P_HEADS, P_TOPK * P_TOPK)
        best, pos = lax.top_k(cand, P_TOPK)
        eidx = jnp.take_along_axis(cidx, pos, axis=-1)
        gate = jax.nn.softmax(best, axis=-1).astype(xb.dtype)
        act = jax.nn.gelu(jnp.einsum('nhkd,nd->nhk', u_emb[eidx], xb), approximate=False)
        return jnp.einsum('nhk,nhkd->nd', gate * act, v_emb[eidx])

    return lax.map(one_block, hb).reshape(bsz, seq, d)


PEER_SCORE_TOKENS = 256
PEER_GATHER_TOKENS = 64
PEER_SLOTS = P_HEADS * P_TOPK
ROW_VREG = (8, 128)
WORDS_PER_ROW = 4


def _top16(s, payload=None):
    n = s.shape[0]
    pos_f = lax.broadcasted_iota(jnp.int32, s.shape, 0).astype(jnp.float32)
    vals, picks = [], []
    for _ in range(P_TOPK):
        m = jnp.max(s, axis=0, keepdims=True)
        pos = jnp.min(jnp.where(s == m, pos_f, float(n)), axis=0, keepdims=True)
        hit = pos_f == pos
        vals.append(m)
        if payload is None:
            picks.append(pos)
        else:
            picks.append(jnp.max(jnp.where(hit, payload, -1.0), axis=0, keepdims=True))
        s = jnp.where(hit, -jnp.inf, s)
    return vals, picks


def _peer_score_kernel(x_ref, g_ref, wpqt_ref, sk_ref, hn_ref, eidx_ref, gate_ref):
    f32, bf16 = jnp.float32, jnp.bfloat16
    hb = _rms(x_ref[...], g_ref[...]).astype(bf16)
    hn_ref[...] = hb
    qrt = _nt_dot(wpqt_ref[...], hb)
    half = P_QDIM // 2
    e_rows, g_rows = [], []
    for hd in range(P_HEADS):
        tops = []
        for p in range(2):
            qhp = qrt[(hd * 2 + p) * half:(hd * 2 + p + 1) * half, :].astype(bf16)
            s = jnp.dot(sk_ref[hd * 2 + p], qhp, preferred_element_type=f32)
            tops.append(_top16(s))
        (v1, i1), (v2, i2) = tops
        v2m = jnp.concatenate(v2, axis=0)
        i2m = jnp.concatenate(i2, axis=0)
        cand = jnp.concatenate([v1[a] + v2m for a in range(P_TOPK)], axis=0)
        cidx = jnp.concatenate([i1[a] * float(N_KEYS) + i2m for a in range(P_TOPK)], axis=0)
        best, be = _top16(cand, cidx)
        ex = [jnp.exp(b - best[0]) for b in best]
        den = ex[0]
        for k in range(1, P_TOPK):
            den = den + ex[k]
        inv = 1.0 / den
        e_rows += be
        g_rows += [x * inv for x in ex]
    eidx_ref[...] = (jnp.concatenate(e_rows, axis=0).T * float(WORDS_PER_ROW)).astype(jnp.int32)
    gate_ref[...] = jnp.concatenate(g_rows, axis=0).T


def _diag_mask():
    r = lax.broadcasted_iota(jnp.int32, (8, PEER_SLOTS * 8), 0)
    c = lax.broadcasted_iota(jnp.int32, (8, PEER_SLOTS * 8), 1)
    return (c & 7) == r


def _gather_rows(idx_ref, tab_ref, g_ref, t):
    for r in range(PEER_SLOTS):
        row0 = pl.multiple_of(idx_ref[t, r], WORDS_PER_ROW)
        g_ref[r * WORDS_PER_ROW:(r + 1) * WORDS_PER_ROW, :] = tab_ref[
            pl.ds(row0, WORDS_PER_ROW), :]


def _gather_pipeline(idx_ref, tab_ref, tiles, n_tokens, consume):
    for g_ref in tiles[2:]:
        g_ref[...] = jnp.zeros(g_ref.shape, g_ref.dtype)

    def trip(i, carry):
        first, second = 2 * i, 2 * i + 1
        for u in range(2):
            _gather_rows(idx_ref, tab_ref, tiles[u], first * 2 + u)
        for u in range(2):
            consume(tiles[2 + u], jnp.maximum(first - 1, 0), u)
        for u in range(2):
            _gather_rows(idx_ref, tab_ref, tiles[2 + u], second * 2 + u)
        for u in range(2):
            consume(tiles[u], first, u)
        return carry

    lax.fori_loop(0, n_tokens // 4, trip, 0)
    for u in range(2):
        consume(tiles[2 + u], n_tokens // 2 - 1, u)


def _peer_act_kernel(idx_ref, hn_ref, tab_ref, a_ref, g0_ref, g1_ref, g2_ref, g3_ref, m_ref):
    f32, bf16 = jnp.float32, jnp.bfloat16
    tb = a_ref.shape[0]

    def dots(g_ref, tp, u):
        hp = hn_ref[pl.ds(pl.multiple_of(tp * 16, 16), 16), :]
        m = _nt_dot(hp, pltpu.bitcast(g_ref[...], bf16))
        m_ref[pl.ds(pl.multiple_of((tp * 2 + u) * 8, 8), 8), :] = m[u * 8:(u + 1) * 8, :]

    _gather_pipeline(idx_ref, tab_ref, (g0_ref, g1_ref, g2_ref, g3_ref), tb, dots)
    m3 = m_ref[...].reshape(tb, 8, PEER_SLOTS * 8)
    z = jnp.sum(jnp.where(_diag_mask()[None], m3, 0.0), axis=1)
    rr = lax.broadcasted_iota(jnp.int32, (PEER_SLOTS * 8, PEER_SLOTS), 0)
    cc = lax.broadcasted_iota(jnp.int32, (PEER_SLOTS * 8, PEER_SLOTS), 1)
    pool = jnp.where((rr >> 3) == cc, 1.0, 0.0).astype(bf16)
    z_hi = z.astype(bf16)
    z_lo = (z - z_hi.astype(f32)).astype(bf16)
    a_ref[...] = (jnp.dot(z_hi, pool, preferred_element_type=f32)
                  + jnp.dot(z_lo, pool, preferred_element_type=f32))


def _peer_out_kernel(idx_ref, a_ref, gate_ref, tab_ref, o_ref, g0_ref, g1_ref, g2_ref, g3_ref,
                     w_ref):
    f32, bf16 = jnp.float32, jnp.bfloat16
    tb = a_ref.shape[0]
    a = a_ref[...]
    act = 0.5 * a * (1.0 + lax.erf(a * (2.0 ** -0.5)))
    wgt = (gate_ref[...] * act).astype(bf16)
    rr = lax.broadcasted_iota(jnp.int32, (PEER_SLOTS, PEER_SLOTS * 8), 0)
    cc = lax.broadcasted_iota(jnp.int32, (PEER_SLOTS, PEER_SLOTS * 8), 1)
    expand = jnp.where((cc >> 3) == rr, 1.0, 0.0).astype(bf16)
    w_ref[...] = jnp.dot(wgt, expand, preferred_element_type=f32)
    diag = _diag_mask()

    def combine(g_ref, tp, u):
        t = tp * 2 + u
        wrow = jnp.broadcast_to(w_ref[pl.ds(t, 1), :], (8, PEER_SLOTS * 8))
        wsel = jnp.where(diag, wrow, 0.0).astype(bf16)
        o_ref[pl.ds(pl.multiple_of(t * 8, 8), 8), :] = jnp.dot(
            wsel, pltpu.bitcast(g_ref[...], bf16), preferred_element_type=f32)

    _gather_pipeline(idx_ref, tab_ref, (g0_ref, g1_ref, g2_ref, g3_ref), tb, combine)


def _pack_table(tab):
    n_e = tab.shape[0]
    tb = lax.bitcast_convert_type(
        tab.astype(jnp.bfloat16).reshape(n_e, WORDS_PER_ROW, 2, 128), jnp.uint16).astype(jnp.uint32)
    word = tb[:, :, 0, :] | (tb[:, :, 1, :] << 16)
    return lax.bitcast_convert_type(word, jnp.int32).reshape(n_e * WORDS_PER_ROW, 128)


def peer_pallas(x, g_ffn, w_pq, sub_keys, u_emb, v_emb):
    n_tok, d = x.shape
    f32, bf16 = jnp.float32, jnp.bfloat16
    ts, tg = PEER_SCORE_TOKENS, PEER_GATHER_TOKENS
    wpqt = w_pq.reshape(d, P_HEADS * P_QDIM).T.astype(bf16)
    sk = sub_keys.reshape(P_HEADS * 2, N_KEYS, P_QDIM // 2).astype(bf16)
    hn, eidx, gate = pl.pallas_call(
        _peer_score_kernel,
        grid=(n_tok // ts,),
        in_specs=[pl.BlockSpec((ts, d), lambda i: (i, 0)),
                  pl.BlockSpec((1, d), lambda i: (0, 0)),
                  pl.BlockSpec(wpqt.shape, lambda i: (0, 0)),
                  pl.BlockSpec(sk.shape, lambda i: (0, 0, 0))],
        out_specs=[pl.BlockSpec((ts, d), lambda i: (i, 0)),
                   pl.BlockSpec((ts, PEER_SLOTS), lambda i: (i, 0)),
                   pl.BlockSpec((ts, PEER_SLOTS), lambda i: (i, 0))],
        out_shape=[jax.ShapeDtypeStruct((n_tok, d), bf16),
                   jax.ShapeDtypeStruct((n_tok, PEER_SLOTS), jnp.int32),
                   jax.ShapeDtypeStruct((n_tok, PEER_SLOTS), f32)],
        compiler_params=pltpu.CompilerParams(vmem_limit_bytes=VMEM_LIMIT),
        name="peer_score",
    )(x, g_ffn.reshape(1, d), wpqt, sk)
    utab, vtab = _pack_table(u_emb), _pack_table(v_emb)
    idx_spec = pl.BlockSpec((tg, PEER_SLOTS), lambda i: (i, 0), memory_space=pltpu.SMEM)
    tab_spec = pl.BlockSpec(memory_space=pltpu.VMEM)
    slot_spec = pl.BlockSpec((tg, PEER_SLOTS), lambda i: (i, 0))
    row_spec = pl.BlockSpec((tg * 8, 128), lambda i: (i, 0))
    gbuf = pltpu.VMEM((PEER_SLOTS * WORDS_PER_ROW, 128), jnp.int32)
    cparams = pltpu.CompilerParams(vmem_limit_bytes=VMEM_LIMIT)
    act = pl.pallas_call(
        _peer_act_kernel,
        grid=(n_tok // tg,),
        in_specs=[idx_spec, row_spec, tab_spec],
        out_specs=slot_spec,
        out_shape=jax.ShapeDtypeStruct((n_tok, PEER_SLOTS), f32),
        scratch_shapes=[gbuf, gbuf, gbuf, gbuf, pltpu.VMEM((tg * 8, PEER_SLOTS * 8), f32)],
        compiler_params=cparams,
        name="peer_act",
    )(eidx, hn.reshape(n_tok * 8, 128), utab)
    out = pl.pallas_call(
        _peer_out_kernel,
        grid=(n_tok // tg,),
        in_specs=[idx_spec, slot_spec, slot_spec, tab_spec],
        out_specs=row_spec,
        out_shape=jax.ShapeDtypeStruct((n_tok * 8, 128), f32),
        scratch_shapes=[gbuf, gbuf, gbuf, gbuf, pltpu.VMEM((tg, PEER_SLOTS * 8), f32)],
        compiler_params=cparams,
        name="peer_out",
    )(eidx, act, gate, vtab)
    return out.reshape(n_tok, d)


def _final_kernel(x_ref, y_ref, g_ref, o_ref):
    o_ref[...] = _rms(x_ref[...] + y_ref[...], g_ref[...])


def final_residual_rmsnorm(x, y, g):
    n, d = x.shape
    tm = 512
    row = pl.BlockSpec((tm, d), lambda i: (i, 0))
    return pl.pallas_call(
        _final_kernel,
        grid=(n // tm,),
        in_specs=[row, row, pl.BlockSpec((1, d), lambda i: (0, 0))],
        out_specs=row,
        out_shape=jax.ShapeDtypeStruct((n, d), x.dtype),
        name="final_rmsnorm",
    )(x, y, g.reshape(1, d))


IN_PROJ_TOKENS = 512
GDN_COLS = 3 * B_QK
GATE_ROWS = 8


def _in_proj_kernel(x_ref, g_ref, wa_ref, wkw_ref, wqkv_ref, wz_ref, wab_ref, wabt_ref,
                    cq_ref, ckv_ref, kw_ref, qkv_ref, z_ref, ab_ref, abt_ref):
    f32 = jnp.float32
    hb = _rms(x_ref[0], g_ref[...]).astype(jnp.bfloat16)
    a = jnp.dot(hb, wa_ref[...], preferred_element_type=f32)
    cq_ref[0] = a[:, :A_Q_LORA]
    ckv_ref[0] = a[:, A_Q_LORA:]
    kw_ref[0] = jnp.dot(hb, wkw_ref[...], preferred_element_type=f32)
    qkv_ref[0] = jnp.dot(hb, wqkv_ref[...], preferred_element_type=f32)
    z_ref[0] = jnp.dot(hb, wz_ref[...], preferred_element_type=f32)
    ab_ref[0] = jnp.dot(hb, wab_ref[...], preferred_element_type=f32)
    abt_ref[0] = _nt_dot(wabt_ref[...], hb)


def in_proj_pallas(x, g_mix, w_in):
    bsz, seq, d = x.shape
    f32, bf16 = jnp.float32, jnp.bfloat16
    tm = min(IN_PROJ_TOKENS, seq)
    o = np.cumsum((0,) + COL_WIDTHS)
    wb = w_in.astype(bf16)
    pad_cols = lambda w: jnp.pad(w, ((0, 0), (0, 128 - w.shape[1])))
    wa = wb[:, o[0]:o[2]]
    wkw = pad_cols(wb[:, o[2]:o[4]])
    wqkv = wb[:, o[4]:o[7]]
    wz = wb[:, o[7]:o[8]]
    wab = pad_cols(wb[:, o[8]:o[10]])
    wabt = jnp.concatenate([wb[:, o[8]:o[10]], wb[:, o[3]:o[4]]], axis=1).T
    full = lambda w: pl.BlockSpec(w.shape, lambda b, t: (0, 0))
    tok = lambda w: pl.BlockSpec((1, tm, w), lambda b, t: (b, t, 0))
    shp = lambda w: jax.ShapeDtypeStruct((bsz, seq, w), f32)
    return pl.pallas_call(
        _in_proj_kernel,
        grid=(bsz, seq // tm),
        in_specs=[tok(d), pl.BlockSpec((1, d), lambda b, t: (0, 0)),
                  full(wa), full(wkw), full(wqkv), full(wz), full(wab), full(wabt)],
        out_specs=[tok(A_Q_LORA), tok(A_KV_LORA), tok(128), tok(GDN_COLS), tok(B_QK), tok(128),
                   pl.BlockSpec((1, GATE_ROWS + IDX_HEADS, tm), lambda b, t: (b, 0, t))],
        out_shape=[shp(A_Q_LORA), shp(A_KV_LORA), shp(128), shp(GDN_COLS), shp(B_QK), shp(128),
                   jax.ShapeDtypeStruct((bsz, GATE_ROWS + IDX_HEADS, seq), f32)],
        compiler_params=pltpu.CompilerParams(vmem_limit_bytes=VMEM_LIMIT),
        name="in_proj",
    )(x, g_mix.reshape(1, d), wa, wkw, wqkv, wz, wab, wabt)


def _softplus(x):
    return jnp.maximum(x, 0.0) + jnp.log1p(jnp.exp(-jnp.abs(x)))


def _sigmoid(x):
    return 1.0 / (1.0 + jnp.exp(-x))


def _gdn_gates(pre, a_log, dt_bias, is_decay):
    g = -jnp.exp(a_log) * _softplus(pre + dt_bias)
    return jnp.where(is_decay, g, _sigmoid(pre))


def _gdn_prep_kernel(qkv_ref, halo_ref, cw_ref, ab_ref, abt_ref, alc_ref, dtc_ref, alr_ref, dtr_ref,
                     q_ref, k_ref, v_ref, gc_ref, gr_ref):
    tm = qkv_ref.shape[1]
    x = qkv_ref[0]
    halo = jnp.where(pl.program_id(1) > 0, halo_ref[0], 0.0)
    full = jnp.concatenate([halo, x], axis=0)
    y = x * cw_ref[CONV_WIDTH - 1:CONV_WIDTH, :]
    for back in range(1, CONV_WIDTH):
        shifted = pltpu.roll(full, back, axis=0)[8:, :]
        y = y + shifted * cw_ref[CONV_WIDTH - 1 - back:CONV_WIDTH - back, :]
    y = y * _sigmoid(y)
    for h in range(B_HEADS):
        cols = slice(h * B_HEAD_DIM, (h + 1) * B_HEAD_DIM)
        qh = y[:, h * B_HEAD_DIM:(h + 1) * B_HEAD_DIM]
        kh = y[:, B_QK + h * B_HEAD_DIM:B_QK + (h + 1) * B_HEAD_DIM]
        q_ref[0, :, cols] = qh * lax.rsqrt(
            jnp.sum(qh * qh, axis=-1, keepdims=True) + EPS) * (B_HEAD_DIM ** -0.5)
        k_ref[0, :, cols] = kh * lax.rsqrt(jnp.sum(kh * kh, axis=-1, keepdims=True) + EPS)
    v_ref[0] = y[:, 2 * B_QK:]
    lane = lax.broadcasted_iota(jnp.int32, (tm, 128), 1)
    gates_c = _gdn_gates(ab_ref[0], alc_ref[...], dtc_ref[...], lane < B_HEADS)
    row = lax.broadcasted_iota(jnp.int32, (GATE_ROWS, tm), 0)
    gates_r = _gdn_gates(abt_ref[0], alr_ref[...], dtr_ref[...], row < B_HEADS)
    ti = lax.broadcasted_iota(jnp.int32, (tm, tm), 0)
    tj = lax.broadcasted_iota(jnp.int32, (tm, tm), 1)
    same_chunk = (ti // CHUNK) == (tj // CHUNK)
    hi = lax.Precision.HIGHEST
    lower = jnp.where(same_chunk & (tj <= ti), 1.0, 0.0)
    upper = jnp.where(same_chunk & (ti <= tj), 1.0, 0.0)
    cum_c = jnp.dot(lower, gates_c, preferred_element_type=jnp.float32, precision=hi)
    cum_r = jnp.dot(gates_r, upper, preferred_element_type=jnp.float32, precision=hi)
    gc_ref[0] = jnp.where(lane < B_HEADS, cum_c, gates_c)
    gr_ref[0] = jnp.where(row < B_HEADS, cum_r, gates_r)


def _gdn_main_kernel(q_ref, k_ref, v_ref, z_ref, gc_ref, gr_ref, gon_ref, o_ref, s_ref):
    f32, bf16 = jnp.float32, jnp.bfloat16
    C, Dh = CHUNK, B_HEAD_DIM
    hi = lax.Precision.HIGHEST

    @pl.when(pl.program_id(1) == 0)
    def _():
        s_ref[...] = jnp.zeros(s_ref.shape, f32)

    ii = lax.broadcasted_iota(jnp.int32, (C, C), 0)
    jj = lax.broadcasted_iota(jnp.int32, (C, C), 1)
    causal = ii >= jj
    strict = ii > jj
    eye = jnp.where(ii == jj, 1.0, 0.0)
    mm = lambda a, b: jnp.dot(a, b, preferred_element_type=f32)
    mmh = lambda a, b: jnp.dot(a, b, preferred_element_type=f32, precision=hi)
    n_chunks = q_ref.shape[1] // C
    units = []
    for c in range(n_chunks):
        rows = slice(c * C, (c + 1) * C)
        gates_c = gc_ref[0, rows, :]
        gates_r = gr_ref[0, :, c * C:(c + 1) * C]
        for h in range(B_HEADS):
            cols = slice(h * Dh, (h + 1) * Dh)
            gcum = jnp.broadcast_to(gates_c[:, h:h + 1], (C, Dh))
            beta = jnp.broadcast_to(gates_c[:, B_HEADS + h:B_HEADS + h + 1], (C, Dh))
            gcum_r = jnp.broadcast_to(gates_r[h:h + 1, :], (C, C))
            decay = jnp.where(causal, jnp.exp(jnp.minimum(gcum[:, :C] - gcum_r, 0.0)), 0.0)
            q, k, v = q_ref[0, rows, cols], k_ref[0, rows, cols], v_ref[0, rows, cols]
            qb, kb = q.astype(bf16), k.astype(bf16)
            kk = _nt_dot(kb, kb)
            qk = _nt_dot(qb, kb)
            egc = jnp.exp(gcum)
            g_last = gcum[C - 1:C, :]
            units.append(dict(
                c=c, h=h, rows=rows, cols=cols,
                neg_m=jnp.where(strict, -(beta[:, :C] * kk * decay), 0.0),
                rhs=jnp.concatenate([v * beta, k * (beta * egc)], axis=1),
                q_dec=(q * egc).astype(bf16), intra=(qk * decay).astype(bf16),
                k_tail=(k * jnp.exp(g_last - gcum)).astype(bf16), chunk_decay=jnp.exp(g_last)))
    powers = [un["neg_m"] for un in units]
    t_inv = [eye + p for p in powers]
    for _ in range(5):
        powers = [mmh(p, p) for p in powers]
        t_inv = [t + mmh(p, t) for p, t in zip(powers, t_inv)]
    sols = [mmh(t, un["rhs"]) for t, un in zip(t_inv, units)]
    states = [s_ref[h] for h in range(B_HEADS)]
    for c in range(n_chunks):
        group = [(un, sol) for un, sol in zip(units, sols) if un["c"] == c]
        sbs = [states[un["h"]].astype(bf16) for un, _ in group]
        vbs = [(sol[:, :Dh] - mm(sol[:, Dh:].astype(bf16), sb)).astype(bf16)
               for (un, sol), sb in zip(group, sbs)]
        outs = [mm(un["q_dec"], sb) + mm(un["intra"], vb)
                for (un, _), sb, vb in zip(group, sbs, vbs)]
        for (un, _), vb in zip(group, vbs):
            states[un["h"]] = states[un["h"]] * un["chunk_decay"] + lax.dot_general(
                un["k_tail"], vb, (((0,), (0,)), ((), ())), preferred_element_type=f32)
        for (un, _), o in zip(group, outs):
            o = o * lax.rsqrt(jnp.mean(o * o, axis=-1, keepdims=True) + EPS) * gon_ref[...]
            zz = z_ref[0, un["rows"], un["cols"]]
            o_ref[0, un["rows"], un["cols"]] = o * (zz * _sigmoid(zz))
    for h in range(B_HEADS):
        s_ref[h] = states[h]


def gdn_pallas(qkv, z, ab, abt, conv_w, a_log, dt_bias, g_onorm):
    bsz, seq, _ = qkv.shape
    f32 = jnp.float32
    tm = min(256, seq)
    zero4 = jnp.zeros((B_HEADS,), f32)
    lane_row = lambda v: jnp.pad(jnp.concatenate([v.astype(f32), zero4]), (0, 120)).reshape(1, 128)
    sub_col = lambda v: jnp.concatenate([v.astype(f32), zero4]).reshape(GATE_ROWS, 1)
    tok = lambda w, t=tm: pl.BlockSpec((1, t, w), lambda b, i: (b, i, 0))
    const = lambda shape: pl.BlockSpec(shape, lambda b, i: (0,) * len(shape))
    shp = lambda w: jax.ShapeDtypeStruct((bsz, seq, w), f32)
    q, k, v, gc, gr = pl.pallas_call(
        _gdn_prep_kernel,
        grid=(bsz, seq // tm),
        in_specs=[tok(GDN_COLS),
                  pl.BlockSpec((1, 8, GDN_COLS), lambda b, i: (b, jnp.maximum(i * (tm // 8) - 1, 0), 0)),
                  const((CONV_WIDTH, GDN_COLS)), tok(128),
                  pl.BlockSpec((1, GATE_ROWS, tm), lambda b, i: (b, 0, i)),
                  const((1, 128)), const((1, 128)), const((GATE_ROWS, 1)), const((GATE_ROWS, 1))],
        out_specs=[tok(B_QK), tok(B_QK), tok(B_QK), tok(128),
                   pl.BlockSpec((1, GATE_ROWS, tm), lambda b, i: (b, 0, i))],
        out_shape=[shp(B_QK), shp(B_QK), shp(B_QK), shp(128),
                   jax.ShapeDtypeStruct((bsz, GATE_ROWS, seq), f32)],
        compiler_params=pltpu.CompilerParams(vmem_limit_bytes=VMEM_LIMIT),
        name="gdn_prep",
    )(qkv, qkv, conv_w.astype(f32), ab, abt, lane_row(a_log), lane_row(dt_bias),
      sub_col(a_log), sub_col(dt_bias))
    ts = 2 * CHUNK
    return pl.pallas_call(
        _gdn_main_kernel,
        grid=(bsz, seq // ts),
        in_specs=[tok(B_QK, ts), tok(B_QK, ts), tok(B_QK, ts), tok(B_QK, ts), tok(128, ts),
                  pl.BlockSpec((1, GATE_ROWS, ts), lambda b, i: (b, 0, i)),
                  const((1, B_HEAD_DIM))],
        out_specs=tok(B_QK, ts),
        out_shape=shp(B_QK),
        scratch_shapes=[pltpu.VMEM((B_HEADS, B_HEAD_DIM, B_HEAD_DIM), f32)],
        compiler_params=pltpu.CompilerParams(dimension_semantics=("arbitrary", "arbitrary")),
        name="gdn_main",
    )(q, k, v, z, gc, gr, g_onorm.astype(f32).reshape(1, B_HEAD_DIM))


def _mem_kv_kernel(mem_ref, g_ref, wk_ref, wv_ref, k_ref, v_ref):
    f32, bf16 = jnp.float32, jnp.bfloat16
    mn = _rms(mem_ref[0], g_ref[...]).astype(bf16)
    k = jnp.dot(mn, wk_ref[...], preferred_element_type=f32)
    v = jnp.dot(mn, wv_ref[...], preferred_element_type=f32)
    for h in range(X_HEADS):
        cols = slice(h * X_HEAD_DIM, (h + 1) * X_HEAD_DIM)
        k_ref[0, h] = k[:, cols].astype(bf16)
        v_ref[0, h] = v[:, cols].astype(bf16)


def _mid_kernel(x_ref, oa_ref, ob_ref, wo_ref, gx_ref, wq_ref, k_ref, v_ref, wox_ref, o_ref):
    f32, bf16 = jnp.float32, jnp.bfloat16
    na = oa_ref.shape[2]
    x1 = (x_ref[0]
          + jnp.dot(oa_ref[0].astype(bf16), wo_ref[:na, :], preferred_element_type=f32)
          + jnp.dot(ob_ref[0].astype(bf16), wo_ref[na:, :], preferred_element_type=f32))
    hq = _rms(x1, gx_ref[...]).astype(bf16)
    q = jnp.dot(hq, wq_ref[...], preferred_element_type=f32)
    heads = []
    for h in range(X_HEADS):
        qh = q[:, h * X_HEAD_DIM:(h + 1) * X_HEAD_DIM].astype(bf16)
        lg = _nt_dot(qh, k_ref[0, h]) * (X_HEAD_DIM ** -0.5)
        p = jnp.exp(lg - jnp.max(lg, axis=-1, keepdims=True))
        p = (p / jnp.sum(p, axis=-1, keepdims=True)).astype(bf16)
        heads.append(jnp.dot(p, v_ref[0, h], preferred_element_type=f32).astype(bf16))
    o = jnp.concatenate(heads, axis=1)
    o_ref[0] = x1 + jnp.dot(o, wox_ref[...], preferred_element_type=f32)


def mid_pallas(x, o_a, o_b, w_out, g_cross, mem, g_mem, wq, wk, wv, wo):
    bsz, seq, d = x.shape
    f32, bf16 = jnp.float32, jnp.bfloat16
    hx = X_HEADS * X_HEAD_DIM
    m_len = mem.shape[1]
    const2 = lambda shape: pl.BlockSpec(shape, lambda b: (0,) * len(shape))
    kv_spec = pl.BlockSpec((1, X_HEADS, m_len, X_HEAD_DIM), lambda b: (b, 0, 0, 0))
    kv_shape = jax.ShapeDtypeStruct((bsz, X_HEADS, m_len, X_HEAD_DIM), bf16)
    k, v = pl.pallas_call(
        _mem_kv_kernel,
        grid=(bsz,),
        in_specs=[pl.BlockSpec((1, m_len, d), lambda b: (b, 0, 0)), const2((1, d)),
                  const2((d, hx)), const2((d, hx))],
        out_specs=[kv_spec, kv_spec],
        out_shape=[kv_shape, kv_shape],
        name="mem_kv",
    )(mem, g_mem.reshape(1, d), wk.reshape(d, hx).astype(bf16), wv.reshape(d, hx).astype(bf16))
    tm = min(256, seq)
    tok = lambda w: pl.BlockSpec((1, tm, w), lambda b, t: (b, t, 0))
    const = lambda shape: pl.BlockSpec(shape, lambda b, t: (0,) * len(shape))
    kv_spec2 = pl.BlockSpec((1, X_HEADS, m_len, X_HEAD_DIM), lambda b, t: (b, 0, 0, 0))
    return pl.pallas_call(
        _mid_kernel,
        grid=(bsz, seq // tm),
        in_specs=[tok(d), tok(o_a.shape[2]), tok(o_b.shape[2]), const((MIX_WIDTH, d)),
                  const((1, d)), const((d, hx)), kv_spec2, kv_spec2, const((hx, d))],
        out_specs=tok(d),
        out_shape=jax.ShapeDtypeStruct((bsz, seq, d), f32),
        compiler_params=pltpu.CompilerParams(vmem_limit_bytes=VMEM_LIMIT),
        name="mid",
    )(x, o_a, o_b, w_out.astype(bf16), g_cross.reshape(1, d), wq.reshape(d, hx).astype(bf16),
      k, v, wo.reshape(hx, d).astype(bf16))


def kernel(x, mem, g_mix, w_in, g_cq, g_ckv, g_kidx, w_uq, w_iq, w_uk, w_uv, rel_bias, conv_w, A_log, dt_bias, g_onorm, w_out, g_cross, g_mem, wq_x, wk_x, wv_x, wo_x, g_ffn, w_pq, sub_keys, u_emb, v_emb, g_final):
    bsz, seq, d = x.shape
    for l in range(DEPTH):
        cq, ckv, kw, qkv, z, ab, abt = in_proj_pallas(x, g_mix[l], w_in[l])
        o_a = dsa_pallas_t(cq, ckv, kw, abt, g_cq[l], g_ckv[l], g_kidx[l],
                           w_uq[l], w_iq[l], w_uk[l], w_uv[l], rel_bias)
        o_b = gdn_pallas(qkv, z, ab, abt, conv_w[l], A_log[l], dt_bias[l], g_onorm[l])
        x = mid_pallas(x, o_a, o_b, w_out[l], g_cross[l], mem, g_mem[l],
                       wq_x[l], wk_x[l], wv_x[l], wo_x[l])
        xf = x.reshape(bsz * seq, d)
        y = peer_pallas(xf, g_ffn[l], w_pq[l], sub_keys[l], u_emb[l], v_emb[l])
        if l + 1 < DEPTH:
            x = (xf + y).reshape(bsz, seq, d)
    return final_residual_rmsnorm(xf, y, g_final).reshape(bsz, seq, d)
```

```python
import math
from functools import partial
import jax
import jax.numpy as jnp
from jax import lax
import numpy as np
from jax.experimental import pallas as pl
from jax.experimental.pallas import tpu as pltpu

D_MODEL = 1024
BATCH = 4
SEQ = 8192
DEPTH = 1

CHUNK = 64
Q_BLOCK = 128
EPS = 1e-6

A_HEADS = 8
A_HEAD_DIM = 64
A_Q_LORA = 256
A_KV_LORA = 256
IDX_HEADS = 8
IDX_DIM = 64
IDX_TOPK_MAX = 256
ATTN_SCALE = A_HEAD_DIM ** -0.5
IDX_SCALE = (IDX_HEADS * IDX_DIM) ** -0.5

B_HEADS = 4
B_HEAD_DIM = 128
B_QK = B_HEADS * B_HEAD_DIM
CONV_WIDTH = 4

REL_BUCKETS = 32
REL_MAX_DIST = 128

MEM_LEN = 256
X_HEADS = 4
X_HEAD_DIM = 128

P_HEADS = 8
N_KEYS = 128
N_EXPERTS = N_KEYS * N_KEYS
P_TOPK = 16
P_QDIM = 256
P_TOKEN_BLOCK = 128

COL_WIDTHS = (A_Q_LORA, A_KV_LORA, IDX_DIM, IDX_HEADS, B_QK, B_QK, B_QK, B_QK, B_HEADS, B_HEADS)
IN_COLS = sum(COL_WIDTHS)
MIX_WIDTH = A_HEADS * A_HEAD_DIM + B_HEADS * B_HEAD_DIM


def rmsnorm(x, g):
    xf = x.astype(jnp.float32)
    y = xf * lax.rsqrt(jnp.mean(xf * xf, axis=-1, keepdims=True) + EPS)
    return (y * g.astype(jnp.float32)).astype(x.dtype)


def l2norm(x):
    return x * lax.rsqrt(jnp.sum(x * x, axis=-1, keepdims=True) + EPS)


def split_columns(p, widths):
    out, start = [], 0
    for w in widths:
        out.append(p[..., start:start + w])
        start += w
    return out


def t5_bucket(rel):
    half = REL_BUCKETS // 2
    max_exact = half // 2
    n = jnp.abs(rel)
    nf = jnp.maximum(n, max_exact).astype(jnp.float32)
    large = max_exact + (jnp.log(nf / max_exact) / math.log(REL_MAX_DIST / max_exact)
                         * (half - max_exact)).astype(jnp.int32)
    large = jnp.minimum(large, half - 1)
    return jnp.where(rel > 0, half, 0) + jnp.where(n < max_exact, n, large)


def dsa_attention(cq, ckv, kidx, widx, g_cq, g_ckv, g_kidx, w_uq, w_iq, w_uk, w_uv, rel_bias):
    bsz, seq, _ = cq.shape
    topk = min(IDX_TOPK_MAX, seq // 4)
    n_blocks = seq // Q_BLOCK
    cq = rmsnorm(cq, g_cq)
    ckv = rmsnorm(ckv, g_ckv)
    kidx = rmsnorm(kidx, g_kidx).astype(jnp.float32)
    key_pos = jnp.arange(seq, dtype=jnp.int32)

    def blockify(a):
        return jnp.moveaxis(a.reshape(bsz, n_blocks, Q_BLOCK, a.shape[-1]), 1, 0)

    def one_block(args):
        cq_b, w_b, start = args
        qpos = start + jnp.arange(Q_BLOCK, dtype=jnp.int32)
        limit = (qpos // CHUNK + 1) * CHUNK
        admissible = key_pos[None, :] < limit[:, None]
        qi = jnp.einsum('bqc,chd->bqhd', cq_b, w_iq).astype(jnp.float32)
        dots = jax.nn.relu(jnp.einsum('bqhd,bsd->bqhs', qi, kidx))
        score = jnp.einsum('bqh,bqhs->bqs', w_b.astype(jnp.float32) * IDX_SCALE, dots)
        score = jnp.where(admissible[None], score, -jnp.inf)
        _, sel = lax.top_k(score, topk)
        valid = sel < limit[None, :, None]
        c_sel = jax.vmap(lambda c, i: c[i])(ckv, sel)
        q = jnp.einsum('bqc,chd->bqhd', cq_b, w_uq)
        q_lat = jnp.einsum('bqhd,chd->bqhc', q, w_uk)
        logits = jnp.einsum('bqhc,bqkc->bhqk', q_lat, c_sel).astype(jnp.float32) * ATTN_SCALE
        bias = rel_bias[t5_bucket(sel - qpos[None, :, None])].astype(jnp.float32)
        logits = logits + jnp.transpose(bias, (0, 3, 1, 2))
        logits = jnp.where(valid[:, None], logits, -jnp.inf)
        p = jax.nn.softmax(logits, axis=-1).astype(c_sel.dtype)
        o_lat = jnp.einsum('bhqk,bqkc->bqhc', p, c_sel)
        o = jnp.einsum('bqhc,chd->bqhd', o_lat, w_uv)
        return o.reshape(bsz, Q_BLOCK, A_HEADS * A_HEAD_DIM)

    starts = jnp.arange(n_blocks, dtype=jnp.int32) * Q_BLOCK
    o = lax.map(one_block, (blockify(cq), blockify(widx), starts))
    return jnp.moveaxis(o, 0, 1).reshape(bsz, seq, A_HEADS * A_HEAD_DIM)


INT_MIN = -2147483648
NEG_BIG = -1e30
KEY_TILE = 512
KEY_PAD = KEY_TILE - Q_BLOCK
VMEM_LIMIT = 56 * 1024 * 1024


def _rms(x, g):
    return x * lax.rsqrt(jnp.mean(x * x, axis=-1, keepdims=True) + EPS) * g


def _nt_dot(a, b):
    return lax.dot_general(a, b, (((1,), (1,)), ((), ())), preferred_element_type=jnp.float32)


def _dsa_prep_kernel(cq_ref, ckv_ref, kw_ref, gcq_ref, gckv_ref, gk_ref,
                     wiq_ref, wuq_ref, wukt_ref,
                     a_ref, qlat_ref, ckvn_ref, kidxn_ref, w_ref):
    bf16 = jnp.bfloat16
    kw = kw_ref[0]
    cqn = _rms(cq_ref[0], gcq_ref[...]).astype(bf16)
    qi = jnp.dot(cqn, wiq_ref[...], preferred_element_type=jnp.float32)
    q = jnp.dot(cqn, wuq_ref[...], preferred_element_type=jnp.float32)
    for h in range(A_HEADS):
        rows = slice(h * Q_BLOCK, (h + 1) * Q_BLOCK)
        a_ref[0, 0, rows, :] = qi[:, h * IDX_DIM:(h + 1) * IDX_DIM].astype(bf16)
        qh = q[:, h * A_HEAD_DIM:(h + 1) * A_HEAD_DIM].astype(bf16)
        ql = jnp.dot(qh, wukt_ref[h], preferred_element_type=jnp.float32) * ATTN_SCALE
        qlat_ref[0, 0, rows, :] = ql.astype(bf16)
    ckvn_ref[0] = _rms(ckv_ref[0], gckv_ref[...]).astype(bf16)
    kidxn_ref[0] = _rms(kw[:, :IDX_DIM], gk_ref[...]).astype(bf16)
    w_ref[0, 0] = kw[:, IDX_DIM:IDX_DIM + IDX_HEADS] * IDX_SCALE


def _dsa_main_kernel(a_ref, qlat_ref, w_ref, kidx_ref, ckv_ref, bias0_ref, wuv_ref, o_ref,
                     sc_ref, big_ref, p_ref, acc_ref, m_ref, l_ref, alpha_ref, wb_ref, *, topk):
    f32, i32, bf16 = jnp.float32, jnp.int32, jnp.bfloat16
    Q, T = Q_BLOCK, KEY_TILE
    NC = T // 128
    i = pl.program_id(1)
    e = (i + 1) * Q
    nt = (i + T // Q) // (T // Q)
    kf = jnp.float32(topk)

    w_blk = w_ref[0, 0]
    for h in range(IDX_HEADS):
        wb_ref[h] = jnp.broadcast_to(w_blk[:, h:h + 1], (Q, 128))
    row = lax.broadcasted_iota(i32, (Q, 128), 0)
    lane = lax.broadcasted_iota(i32, (Q, 128), 1)
    limit = ((i * Q + row) // CHUNK + 1) * CHUNK
    a_mat = a_ref[0, 0]

    def tile_start(j):
        return pl.multiple_of(e + KEY_PAD - (j + 1) * T, 128)

    def score_tile(j, carry):
        start = tile_start(j)
        big_ref[...] = _nt_dot(a_mat, kidx_ref[0, pl.ds(start, T), :])
        for c in range(NC):
            cols = slice(c * 128, (c + 1) * 128)
            s = jnp.zeros((Q, 128), f32)
            for h in range(IDX_HEADS):
                s = s + wb_ref[h] * jnp.maximum(big_ref[h * Q:(h + 1) * Q, cols], 0.0)
            bits = lax.bitcast_convert_type(s, i32)
            key = jnp.where(bits < 0, bits ^ jnp.int32(0x7FFFFFFF), bits)
            key = jnp.where(s == 0.0, 0, key)
            kpos = start - KEY_PAD + c * 128 + lane
            key = jnp.where(kpos >= 0, key, INT_MIN)
            key = jnp.where(kpos < limit, key, INT_MIN)
            sc_ref[j, :, cols] = key
        return carry

    lax.fori_loop(0, nt, score_tile, 0)

    def count_where(pred):
        def body(j, cnt):
            for c in range(NC):
                cols = slice(c * 128, (c + 1) * 128)
                kpos = tile_start(j) - KEY_PAD + c * 128 + lane
                cnt = cnt + jnp.where(pred(sc_ref[j, :, cols], kpos), 1.0, 0.0)
            return cnt
        cnt = lax.fori_loop(0, nt, body, jnp.zeros((Q, 128), f32))
        return jnp.broadcast_to(jnp.sum(cnt, axis=1, keepdims=True), (Q, 128))

    def bit_body(b, carry):
        u, cacc = carry
        cand = u | lax.shift_left(jnp.int32(1), 31 - b)
        tvec = cand ^ jnp.int32(INT_MIN)
        tot = count_where(lambda k, kpos: k >= tvec)
        ok = tot >= kf
        return jnp.where(ok, cand, u), jnp.where(ok, tot, cacc)

    u, cacc = lax.fori_loop(0, 32, bit_body,
                            (jnp.zeros((Q, 128), i32), jnp.zeros((Q, 128), f32)))
    thr = jnp.maximum(u ^ jnp.int32(INT_MIN), INT_MIN + 1)
    overflow = jnp.where(u != 0, cacc, 0.0) > kf
    n_over = jnp.max(jnp.max(jnp.where(overflow, 1.0, 0.0), axis=1, keepdims=True),
                     axis=0, keepdims=True)[0, 0]

    @pl.when(n_over > 0.0)
    def _():
        need = kf - count_where(lambda k, kpos: k > thr)

        def cut_body(b, cut):
            cand = cut | lax.shift_left(jnp.int32(1), 14 - b)
            cnt = count_where(lambda k, kpos: jnp.where(k == thr, kpos, cand) < cand)
            return jnp.where(cnt <= need, cand, cut)

        cut = lax.fori_loop(0, 15, cut_body, jnp.zeros((Q, 128), i32))

        def drop_tile(j, carry):
            for c in range(NC):
                cols = slice(c * 128, (c + 1) * 128)
                kpos = tile_start(j) - KEY_PAD + c * 128 + lane
                k = sc_ref[j, :, cols]
                drop = jnp.where(k == thr, kpos, -1) >= cut
                sc_ref[j, :, cols] = jnp.where(drop, INT_MIN, k)
            return carry

        lax.fori_loop(0, nt, drop_tile, 0)

    m_ref[...] = jnp.full(m_ref.shape, NEG_BIG, f32)
    l_ref[...] = jnp.zeros(l_ref.shape, f32)
    acc_ref[...] = jnp.zeros(acc_ref.shape, f32)
    qlat = qlat_ref[0, 0]

    def attn_tile(j, with_bias):
        start = tile_start(j)
        kv = ckv_ref[0, pl.ds(start, T), :]
        big_ref[...] = _nt_dot(qlat, kv)
        mask_add = [jnp.where(sc_ref[j, :, c * 128:(c + 1) * 128] >= thr, 0.0, NEG_BIG)
                    for c in range(NC)]
        for h in range(A_HEADS):
            rows = slice(h * Q, (h + 1) * Q)
            lg = []
            for c in range(NC):
                x = big_ref[rows, c * 128:(c + 1) * 128] + mask_add[c]
                if with_bias:
                    x = x + bias0_ref[h, :, c * 128:(c + 1) * 128]
                lg.append(x)
            mx = lg[0]
            for c in range(1, NC):
                mx = jnp.maximum(mx, lg[c])
            m_prev = m_ref[rows, :]
            m_new = jnp.maximum(m_prev, jnp.max(mx, axis=1, keepdims=True))
            psum = jnp.zeros((Q, 128), f32)
            for c in range(NC):
                p = jnp.exp(lg[c] - m_new)
                psum = psum + p
                p_ref[rows, c * 128:(c + 1) * 128] = p.astype(bf16)
            alpha = jnp.exp(m_prev - m_new)
            l_ref[rows, :] = alpha * l_ref[rows, :] + jnp.sum(psum, axis=1, keepdims=True)
            m_ref[rows, :] = m_new
            alpha_ref[rows, :] = alpha
        acc_ref[...] = alpha_ref[...] * acc_ref[...] + jnp.dot(
            p_ref[...], kv, preferred_element_type=f32)

    attn_tile(0, True)

    def attn_body(j, carry):
        attn_tile(j, False)
        return carry

    lax.fori_loop(1, nt, attn_body, 0)

    inv_l = 1.0 / l_ref[...]
    for h in range(A_HEADS):
        rows = slice(h * Q, (h + 1) * Q)
        o_lat = (acc_ref[rows, :] * inv_l[rows, :]).astype(bf16)
        o_ref[0, :, h * A_HEAD_DIM:(h + 1) * A_HEAD_DIM] = jnp.dot(
            o_lat, wuv_ref[h], preferred_element_type=f32)


def dsa_pallas(cq, ckv, kw, g_cq, g_ckv, g_kidx, w_uq, w_iq, w_uk, w_uv, rel_bias):
    bsz, seq, _ = cq.shape
    f32, bf16 = jnp.float32, jnp.bfloat16
    Q, T, H = Q_BLOCK, KEY_TILE, A_HEADS
    nblk = seq // Q
    topk = min(IDX_TOPK_MAX, seq // 4)
    wiq2 = w_iq.reshape(A_Q_LORA, IDX_HEADS * IDX_DIM).astype(bf16)
    wuq2 = w_uq.reshape(A_Q_LORA, H * A_HEAD_DIM).astype(bf16)
    wukt = jnp.transpose(w_uk, (1, 2, 0)).astype(bf16)
    wuv = jnp.transpose(w_uv, (1, 0, 2)).astype(bf16)
    tok = lambda w: pl.BlockSpec((1, Q, w), lambda b, t: (b, t, 0))
    full = lambda shape: pl.BlockSpec(shape, lambda b, t: (0,) * len(shape))
    a_mat, qlat, ckvn, kidxn, wsc = pl.pallas_call(
        _dsa_prep_kernel,
        grid=(bsz, nblk),
        in_specs=[tok(A_Q_LORA), tok(A_KV_LORA), tok(128),
                  full((1, A_Q_LORA)), full((1, A_KV_LORA)), full((1, IDX_DIM)),
                  full(wiq2.shape), full(wuq2.shape), full(wukt.shape)],
        out_specs=[pl.BlockSpec((1, 1, H * Q, IDX_DIM), lambda b, t: (b, t, 0, 0)),
                   pl.BlockSpec((1, 1, H * Q, A_KV_LORA), lambda b, t: (b, t, 0, 0)),
                   tok(A_KV_LORA), tok(IDX_DIM),
                   pl.BlockSpec((1, 1, Q, IDX_HEADS), lambda b, t: (b, t, 0, 0))],
        out_shape=[jax.ShapeDtypeStruct((bsz, nblk, H * Q, IDX_DIM), bf16),
                   jax.ShapeDtypeStruct((bsz, nblk, H * Q, A_KV_LORA), bf16),
                   jax.ShapeDtypeStruct((bsz, seq, A_KV_LORA), bf16),
                   jax.ShapeDtypeStruct((bsz, seq, IDX_DIM), bf16),
                   jax.ShapeDtypeStruct((bsz, nblk, Q, IDX_HEADS), f32)],
        name="dsa_prep",
    )(cq, ckv, kw, g_cq.reshape(1, -1), g_ckv.reshape(1, -1), g_kidx.reshape(1, -1),
      wiq2, wuq2, wukt)
    ckvp = jnp.pad(ckvn, ((0, 0), (KEY_PAD, 0), (0, 0)))
    kidxp = jnp.pad(kidxn, ((0, 0), (KEY_PAD, 0), (0, 0)))
    rel = (jnp.arange(T, dtype=jnp.int32)[None, :] - jnp.arange(Q, dtype=jnp.int32)[:, None]
           - KEY_PAD)
    far = rel_bias[t5_bucket(jnp.int32(-KEY_PAD - 1))]
    bias0 = jnp.transpose(rel_bias[t5_bucket(rel)] - far, (2, 0, 1)).astype(f32)
    nt_max = (nblk - 1 + T // Q) // (T // Q)
    skey = seq + KEY_PAD
    out = pl.pallas_call(
        partial(_dsa_main_kernel, topk=topk),
        grid=(bsz, nblk),
        in_specs=[pl.BlockSpec((1, 1, H * Q, IDX_DIM), lambda b, t: (b, t, 0, 0)),
                  pl.BlockSpec((1, 1, H * Q, A_KV_LORA), lambda b, t: (b, t, 0, 0)),
                  pl.BlockSpec((1, 1, Q, IDX_HEADS), lambda b, t: (b, t, 0, 0)),
                  pl.BlockSpec((1, skey, IDX_DIM), lambda b, t: (b, 0, 0)),
                  pl.BlockSpec((1, skey, A_KV_LORA), lambda b, t: (b, 0, 0)),
                  full(bias0.shape), full(wuv.shape)],
        out_specs=tok(H * A_HEAD_DIM),
        out_shape=jax.ShapeDtypeStruct((bsz, seq, H * A_HEAD_DIM), f32),
        scratch_shapes=[pltpu.VMEM((nt_max, Q, T), jnp.int32),
                        pltpu.VMEM((H * Q, T), f32),
                        pltpu.VMEM((H * Q, T), bf16),
                        pltpu.VMEM((H * Q, A_KV_LORA), f32),
                        pltpu.VMEM((H * Q, 1), f32),
                        pltpu.VMEM((H * Q, 1), f32),
                        pltpu.VMEM((H * Q, 1), f32),
                        pltpu.VMEM((IDX_HEADS, Q, 128), f32)],
        compiler_params=pltpu.CompilerParams(
            dimension_semantics=("arbitrary", "arbitrary"), vmem_limit_bytes=VMEM_LIMIT),
        name="dsa_main",
    )(a_mat, qlat, wsc, kidxp, ckvp, bias0, wuv)
    return out


def _dsa_prep_t_kernel(cq_ref, ckv_ref, kw_ref, gcq_ref, gckv_ref, gk_ref,
                       wiqt_ref, wuqt_ref, wuk_ref,
                       at_ref, qlatt_ref, ckvn_ref, kidxn_ref):
    f32, bf16 = jnp.float32, jnp.bfloat16
    Q = Q_BLOCK
    kw = kw_ref[0]
    cqn = _rms(cq_ref[0], gcq_ref[...]).astype(bf16)
    qit = _nt_dot(wiqt_ref[...], cqn)
    qt = _nt_dot(wuqt_ref[...], cqn)
    for h in range(A_HEADS):
        cols = slice(h * Q, (h + 1) * Q)
        at_ref[0, 0, :, cols] = qit[h * IDX_DIM:(h + 1) * IDX_DIM, :].astype(bf16)
        qh = qt[h * A_HEAD_DIM:(h + 1) * A_HEAD_DIM, :].astype(bf16)
        ql = jnp.dot(wuk_ref[h], qh, preferred_element_type=f32) * ATTN_SCALE
        qlatt_ref[0, 0, :, cols] = ql.astype(bf16)
    ckvn_ref[0] = _rms(ckv_ref[0], gckv_ref[...]).astype(bf16)
    kidxn_ref[0] = _rms(kw[:, :IDX_DIM], gk_ref[...]).astype(bf16)


def _dsa_main_t_kernel(at_ref, qlatt_ref, wt_ref, kidx_ref, ckv_ref, bias0_ref, wuvt_ref, o_ref,
                       sc_ref, big_ref, p_ref, acc_ref, m_ref, l_ref, *, topk):
    f32, i32, bf16 = jnp.float32, jnp.int32, jnp.bfloat16
    Q, T = Q_BLOCK, KEY_TILE
    i = pl.program_id(1)
    e = (i + 1) * Q
    nt = (i + T // Q) // (T // Q)
    kf = jnp.float32(topk)
    wsc = wt_ref[0] * IDX_SCALE
    sub = lax.broadcasted_iota(i32, (T, Q), 0)
    qlane = lax.broadcasted_iota(i32, (T, Q), 1)
    limit = ((i * Q + qlane) // CHUNK + 1) * CHUNK
    at = at_ref[0, 0]

    def tile_start(j):
        return pl.multiple_of(e + KEY_PAD - (j + 1) * T, 128)

    def key_pos(j):
        return tile_start(j) - KEY_PAD + sub

    def score_tile(j, carry):
        kid = kidx_ref[0, pl.ds(tile_start(j), T), :]
        big_ref[...] = jnp.dot(kid, at, preferred_element_type=f32)
        s = jnp.zeros((T, Q), f32)
        for h in range(IDX_HEADS):
            s = s + wsc[h:h + 1, :] * jnp.maximum(big_ref[:, h * Q:(h + 1) * Q], 0.0)
        bits = lax.bitcast_convert_type(s, i32)
        key = jnp.where(bits < 0, bits ^ jnp.int32(0x7FFFFFFF), bits)
        key = jnp.where(s == 0.0, 0, key)
        kpos = key_pos(j)
        key = jnp.where(kpos >= 0, key, INT_MIN)
        sc_ref[j] = jnp.where(kpos < limit, key, INT_MIN)
        return carry

    lax.fori_loop(0, nt, score_tile, 0)

    def count_where(pred):
        def body(j, cnt):
            hit = jnp.where(pred(sc_ref[j], key_pos(j)), 1.0, 0.0)
            return cnt + jnp.sum(hit.reshape(T // 64, 64, Q), axis=0)
        cnt = lax.fori_loop(0, nt, body, jnp.zeros((64, Q), f32))
        return jnp.sum(cnt, axis=0, keepdims=True)

    def bit_body(b, carry):
        u, cacc = carry
        cand = u | lax.shift_left(jnp.int32(1), 31 - b)
        tvec = cand ^ jnp.int32(INT_MIN)
        tot = count_where(lambda k, kpos: k >= tvec)
        ok = tot >= kf
        return jnp.where(ok, cand, u), jnp.where(ok, tot, cacc)

    u, cacc = lax.fori_loop(0, 32, bit_body, (jnp.zeros((1, Q), i32), jnp.zeros((1, Q), f32)))
    thr = jnp.maximum(u ^ jnp.int32(INT_MIN), INT_MIN + 1)
    overflow = jnp.where(u != 0, cacc, 0.0) > kf
    n_over = jnp.max(jnp.where(overflow, 1.0, 0.0), axis=1, keepdims=True)[0, 0]

    @pl.when(n_over > 0.0)
    def _():
        need = kf - count_where(lambda k, kpos: k > thr)

        def cut_body(b, cut):
            cand = cut | lax.shift_left(jnp.int32(1), 14 - b)
            cnt = count_where(lambda k, kpos: jnp.where(k == thr, kpos, cand) < cand)
            return jnp.where(cnt <= need, cand, cut)

        cut = lax.fori_loop(0, 15, cut_body, jnp.zeros((1, Q), i32))

        def drop_tile(j, carry):
            k = sc_ref[j]
            drop = jnp.where(k == thr, key_pos(j), -1) >= cut
            sc_ref[j] = jnp.where(drop, INT_MIN, k)
            return carry

        lax.fori_loop(0, nt, drop_tile, 0)

    m_ref[...] = jnp.full(m_ref.shape, NEG_BIG, f32)
    l_ref[...] = jnp.zeros(l_ref.shape, f32)
    acc_ref[...] = jnp.zeros(acc_ref.shape, f32)
    qlatt = qlatt_ref[0, 0]

    def attn_tile(j, with_bias):
        kv = ckv_ref[0, pl.ds(tile_start(j), T), :]
        big_ref[...] = jnp.dot(kv, qlatt, preferred_element_type=f32)
        mask_add = jnp.where(sc_ref[j] >= thr, 0.0, NEG_BIG)
        alphas = []
        for h in range(A_HEADS):
            cols = slice(h * Q, (h + 1) * Q)
            x = big_ref[:, cols] + mask_add
            if with_bias:
                x = x + bias0_ref[h]
            m_prev = m_ref[:, cols]
            m_new = jnp.maximum(m_prev, jnp.max(x, axis=0, keepdims=True))
            p = jnp.exp(x - m_new)
            alpha = jnp.exp(m_prev - m_new)
            l_ref[:, cols] = alpha * l_ref[:, cols] + jnp.sum(p, axis=0, keepdims=True)
            m_ref[:, cols] = m_new
            p_ref[:, cols] = p.astype(bf16)
            alphas.append(alpha)
        pv = lax.dot_general(kv, p_ref[...], (((0,), (0,)), ((), ())),
                             preferred_element_type=f32)
        acc_ref[...] = jnp.concatenate(alphas, axis=1) * acc_ref[...] + pv

    attn_tile(0, True)

    def attn_body(j, carry):
        attn_tile(j, False)
        return carry

    lax.fori_loop(1, nt, attn_body, 0)

    inv_l = 1.0 / l_ref[...]
    outs = []
    for h in range(A_HEADS):
        cols = slice(h * Q, (h + 1) * Q)
        o_lat_t = (acc_ref[:, cols] * inv_l[:, cols]).astype(bf16)
        outs.append(jnp.dot(wuvt_ref[h], o_lat_t, preferred_element_type=f32))
    o_ref[0] = jnp.concatenate(outs, axis=0).T


def dsa_pallas_t(cq, ckv, kw, widx_t, g_cq, g_ckv, g_kidx, w_uq, w_iq, w_uk, w_uv, rel_bias):
    bsz, seq, _ = cq.shape
    f32, bf16 = jnp.float32, jnp.bfloat16
    Q, T, H = Q_BLOCK, KEY_TILE, A_HEADS
    nblk = seq // Q
    topk = min(IDX_TOPK_MAX, seq // 4)
    wiqt = w_iq.reshape(A_Q_LORA, IDX_HEADS * IDX_DIM).T.astype(bf16)
    wuqt = w_uq.reshape(A_Q_LORA, H * A_HEAD_DIM).T.astype(bf16)
    wuk = jnp.transpose(w_uk, (1, 0, 2)).astype(bf16)
    wuvt = jnp.transpose(w_uv, (1, 2, 0)).astype(bf16)
    tok = lambda w: pl.BlockSpec((1, Q, w), lambda b, t: (b, t, 0))
    full = lambda shape: pl.BlockSpec(shape, lambda b, t: (0,) * len(shape))
    blk = lambda r: pl.BlockSpec((1, 1, r, H * Q), lambda b, t: (b, t, 0, 0))
    a_t, qlat_t, ckvn, kidxn = pl.pallas_call(
        _dsa_prep_t_kernel,
        grid=(bsz, nblk),
        in_specs=[tok(A_Q_LORA), tok(A_KV_LORA), tok(128),
                  full((1, A_Q_LORA)), full((1, A_KV_LORA)), full((1, IDX_DIM)),
                  full(wiqt.shape), full(wuqt.shape), full(wuk.shape)],
        out_specs=[blk(IDX_DIM), blk(A_KV_LORA), tok(A_KV_LORA), tok(IDX_DIM)],
        out_shape=[jax.ShapeDtypeStruct((bsz, nblk, IDX_DIM, H * Q), bf16),
                   jax.ShapeDtypeStruct((bsz, nblk, A_KV_LORA, H * Q), bf16),
                   jax.ShapeDtypeStruct((bsz, seq, A_KV_LORA), bf16),
                   jax.ShapeDtypeStruct((bsz, seq, IDX_DIM), bf16)],
        name="dsa_prep",
    )(cq, ckv, kw, g_cq.reshape(1, -1), g_ckv.reshape(1, -1), g_kidx.reshape(1, -1),
      wiqt, wuqt, wuk)
    ckvp = jnp.pad(ckvn, ((0, 0), (KEY_PAD, 0), (0, 0)))
    kidxp = jnp.pad(kidxn, ((0, 0), (KEY_PAD, 0), (0, 0)))
    rel = (jnp.arange(T, dtype=jnp.int32)[:, None] - jnp.arange(Q, dtype=jnp.int32)[None, :]
           - KEY_PAD)
    far = rel_bias[t5_bucket(jnp.int32(-KEY_PAD - 1))]
    bias0 = jnp.transpose(rel_bias[t5_bucket(rel)] - far, (2, 0, 1)).astype(f32)
    nt_max = (nblk - 1 + T // Q) // (T // Q)
    skey = seq + KEY_PAD
    return pl.pallas_call(
        partial(_dsa_main_t_kernel, topk=topk),
        grid=(bsz, nblk),
        in_specs=[blk(IDX_DIM), blk(A_KV_LORA),
                  pl.BlockSpec((1, IDX_HEADS, Q), lambda b, t: (b, 1, t)),
                  pl.BlockSpec((1, skey, IDX_DIM), lambda b, t: (b, 0, 0)),
                  pl.BlockSpec((1, skey, A_KV_LORA), lambda b, t: (b, 0, 0)),
                  full(bias0.shape), full(wuvt.shape)],
        out_specs=tok(H * A_HEAD_DIM),
        out_shape=jax.ShapeDtypeStruct((bsz, seq, H * A_HEAD_DIM), f32),
        scratch_shapes=[pltpu.VMEM((nt_max, T, Q), jnp.int32),
                        pltpu.VMEM((T, H * Q), f32),
                        pltpu.VMEM((T, H * Q), bf16),
                        pltpu.VMEM((A_KV_LORA, H * Q), f32),
                        pltpu.VMEM((1, H * Q), f32),
                        pltpu.VMEM((1, H * Q), f32)],
        compiler_params=pltpu.CompilerParams(
            dimension_semantics=("arbitrary", "arbitrary"), vmem_limit_bytes=VMEM_LIMIT),
        name="dsa_main",
    )(a_t, qlat_t, widx_t, kidxp, ckvp, bias0, wuvt)


def causal_dwconv(x, w):
    ch = x.shape[-1]
    return lax.conv_general_dilated(x, w[:, None, :].astype(x.dtype), window_strides=(1,),
                                    padding=[(CONV_WIDTH - 1, 0)],
                                    dimension_numbers=('NWC', 'WIO', 'NWC'),
                                    feature_group_count=ch)


def gated_deltanet(q, k, v, z, a, b, conv_w, A_log, dt_bias, g_onorm):
    bsz, seq, _ = q.shape
    f32 = jnp.float32
    H, Dh = B_HEADS, B_HEAD_DIM
    nc = seq // CHUNK
    qkv = jax.nn.silu(causal_dwconv(jnp.concatenate([q, k, v], axis=-1), conv_w)).astype(f32)
    q, k, v = [t.reshape(bsz, seq, H, Dh) for t in jnp.split(qkv, 3, axis=-1)]
    q = l2norm(q) * (Dh ** -0.5)
    k = l2norm(k)
    beta = jax.nn.sigmoid(b.astype(f32))
    g = -jnp.exp(A_log.astype(f32)) * jax.nn.softplus(a.astype(f32) + dt_bias.astype(f32))

    def to_chunks(t):
        t = t.reshape(bsz, nc, CHUNK, *t.shape[2:])
        return jnp.moveaxis(t, 3, 1)

    q, k, v, beta = to_chunks(q), to_chunks(k), to_chunks(v), to_chunks(beta)
    g = jnp.cumsum(to_chunks(g), axis=-1)
    pos = jnp.arange(CHUNK)
    causal = pos[:, None] >= pos[None, :]
    strict = pos[:, None] > pos[None, :]
    decay = jnp.exp(jnp.where(causal, g[..., :, None] - g[..., None, :], -jnp.inf))
    m = jnp.where(strict, beta[..., :, None] * jnp.einsum('bhnid,bhnjd->bhnij', k, k) * decay, 0.0)
    rhs = jnp.concatenate([v * beta[..., None], k * (beta * jnp.exp(g))[..., None]], axis=-1)
    sol = lax.linalg.triangular_solve(m, rhs, left_side=True, lower=True, unit_diagonal=True)
    u, w = sol[..., :Dh], sol[..., Dh:]
    intra = jnp.einsum('bhnid,bhnjd->bhnij', q, k) * decay
    q_dec = q * jnp.exp(g)[..., None]
    k_tail = k * jnp.exp(g[..., -1:] - g)[..., None]
    chunk_decay = jnp.exp(g[..., -1])

    def step(state, inp):
        q_c, u_c, w_c, a_c, kt_c, d_c = inp
        v_new = u_c - jnp.einsum('bhcd,bhde->bhce', w_c, state)
        o_c = jnp.einsum('bhcd,bhde->bhce', q_c, state) + jnp.einsum('bhij,bhje->bhie', a_c, v_new)
        state = state * d_c[..., None, None] + jnp.einsum('bhcd,bhce->bhde', kt_c, v_new)
        return state, o_c

    xs = tuple(jnp.moveaxis(t, 2, 0) for t in (q_dec, u, w, intra, k_tail, chunk_decay))
    _, o = lax.scan(step, jnp.zeros((bsz, H, Dh, Dh), f32), xs)
    o = jnp.transpose(o, (1, 0, 3, 2, 4)).reshape(bsz, seq, H, Dh)
    o = o * lax.rsqrt(jnp.mean(o * o, axis=-1, keepdims=True) + EPS) * g_onorm.astype(f32)
    o = o * jax.nn.silu(z.astype(f32).reshape(bsz, seq, H, Dh))
    return o.reshape(bsz, seq, H * Dh).astype(z.dtype)


def memory_cross_attention(h, mem_n, wq, wk, wv, wo):
    q = jnp.einsum('bsd,dhe->bhse', h, wq)
    k = jnp.einsum('bmd,dhe->bhme', mem_n, wk)
    v = jnp.einsum('bmd,dhe->bhme', mem_n, wv)
    logits = jnp.einsum('bhse,bhme->bhsm', q, k).astype(jnp.float32) * (X_HEAD_DIM ** -0.5)
    p = jax.nn.softmax(logits, axis=-1).astype(v.dtype)
    o = jnp.einsum('bhsm,bhme->bshe', p, v)
    return jnp.einsum('bshe,hed->bsd', o, wo)


def peer_ffn(h, w_pq, sub_keys, u_emb, v_emb):
    bsz, seq, d = h.shape
    n_tok = bsz * seq
    hb = h.reshape(n_tok // P_TOKEN_BLOCK, P_TOKEN_BLOCK, d)

    def one_block(xb):
        n = xb.shape[0]
        qr = jnp.einsum('nd,dhe->nhe', xb, w_pq).reshape(n, P_HEADS, 2, P_QDIM // 2)
        s = jnp.einsum('nhpe,hpke->nhpk', qr, sub_keys).astype(jnp.float32)
        s1, i1 = lax.top_k(s[:, :, 0], P_TOPK)
        s2, i2 = lax.top_k(s[:, :, 1], P_TOPK)
        cand = (s1[..., :, None] + s2[..., None, :]).reshape(n, P_HEADS, P_TOPK * P_TOPK)
        cidx = (i1[..., :, None] * N_KEYS + i2[..., None, :]).reshape(n, P_HEADS, P_TOPK * P_TOPK)
        best, pos = lax.top_k(cand, P_TOPK)
        eidx = jnp.take_along_axis(cidx, pos, axis=-1)
        gate = jax.nn.softmax(best, axis=-1).astype(xb.dtype)
        act = jax.nn.gelu(jnp.einsum('nhkd,nd->nhk', u_emb[eidx], xb), approximate=False)
        return jnp.einsum('nhk,nhkd->nd', gate * act, v_emb[eidx])

    return lax.map(one_block, hb).reshape(bsz, seq, d)


PEER_SCORE_TOKENS = 256
PEER_GATHER_TOKENS = 64
PEER_SLOTS = P_HEADS * P_TOPK
ROW_VREG = (8, 128)
WORDS_PER_ROW = 4
PEER_TILES = 8


def _top16(s, payload=None):
    n = s.shape[0]
    pos_f = lax.broadcasted_iota(jnp.int32, s.shape, 0).astype(jnp.float32)
    vals, picks = [], []
    for _ in range(P_TOPK):
        m = jnp.max(s, axis=0, keepdims=True)
        pos = jnp.min(jnp.where(s == m, pos_f, float(n)), axis=0, keepdims=True)
        hit = pos_f == pos
        vals.append(m)
        if payload is None:
            picks.append(pos)
        else:
            picks.append(jnp.max(jnp.where(hit, payload, -1.0), axis=0, keepdims=True))
        s = jnp.where(hit, -jnp.inf, s)
    return vals, picks


def _peer_score_kernel(x_ref, g_ref, wpqt_ref, sk_ref, hn_ref, eidx_ref, gate_ref):
    f32, bf16 = jnp.float32, jnp.bfloat16
    hb = _rms(x_ref[...], g_ref[...]).astype(bf16)
    hn_ref[...] = hb
    qrt = _nt_dot(wpqt_ref[...], hb)
    half = P_QDIM // 2
    e_rows, g_rows = [], []
    for hd in range(P_HEADS):
        tops = []
        for p in range(2):
            qhp = qrt[(hd * 2 + p) * half:(hd * 2 + p + 1) * half, :].astype(bf16)
            s = jnp.dot(sk_ref[hd * 2 + p], qhp, preferred_element_type=f32)
            tops.append(_top16(s))
        (v1, i1), (v2, i2) = tops
        v2m = jnp.concatenate(v2, axis=0)
        i2m = jnp.concatenate(i2, axis=0)
        cand = jnp.concatenate([v1[a] + v2m for a in range(P_TOPK)], axis=0)
        cidx = jnp.concatenate([i1[a] * float(N_KEYS) + i2m for a in range(P_TOPK)], axis=0)
        best, be = _top16(cand, cidx)
        ex = [jnp.exp(b - best[0]) for b in best]
        den = ex[0]
        for k in range(1, P_TOPK):
            den = den + ex[k]
        inv = 1.0 / den
        e_rows += be
        g_rows += [x * inv for x in ex]
    eidx_ref[...] = (jnp.concatenate(e_rows, axis=0).T * float(WORDS_PER_ROW)).astype(jnp.int32)
    gate_ref[...] = jnp.concatenate(g_rows, axis=0).T


def _diag_mask():
    r = lax.broadcasted_iota(jnp.int32, (8, PEER_SLOTS * 8), 0)
    c = lax.broadcasted_iota(jnp.int32, (8, PEER_SLOTS * 8), 1)
    return (c & 7) == r


def _gather_pair(idx_ref, tab_ref, g_refs, t0):
    views = [idx_ref.at[t0 + u] for u in range(2)]
    for r in range(PEER_SLOTS):
        for view, g_ref in zip(views, g_refs):
            row0 = pl.multiple_of(view[r], WORDS_PER_ROW)
            g_ref[r * WORDS_PER_ROW:(r + 1) * WORDS_PER_ROW, :] = tab_ref[
                pl.ds(row0, WORDS_PER_ROW), :]


def _gather_pipeline(idx_ref, tab_ref, tiles, n_tokens, consume):
    pairs = len(tiles) // 2
    for g_ref in tiles[-2:]:
        g_ref[...] = jnp.zeros(g_ref.shape, g_ref.dtype)

    def trip(i, carry):
        for k in range(pairs):
            pair = pairs * i + k
            prev_tiles = tiles[2 * k - 2:2 * k] if k else tiles[-2:]
            _gather_pair(idx_ref, tab_ref, tiles[2 * k:2 * k + 2], pair * 2)
            for u in range(2):
                consume(prev_tiles[u], jnp.maximum(pair - 1, 0), u)
        return carry

    lax.fori_loop(0, n_tokens // (2 * pairs), trip, 0)
    for u in range(2):
        consume(tiles[-2 + u], n_tokens // 2 - 1, u)


def _peer_act_kernel(idx_ref, hn_ref, tab_ref, a_ref, m_ref, *tiles):
    f32, bf16 = jnp.float32, jnp.bfloat16
    tb = a_ref.shape[0]

    def dots(g_ref, tp, u):
        hp = hn_ref[pl.ds(pl.multiple_of(tp * 16, 16), 16), :]
        m = _nt_dot(hp, pltpu.bitcast(g_ref[...], bf16))
        m_ref[pl.ds(pl.multiple_of((tp * 2 + u) * 8, 8), 8), :] = m[u * 8:(u + 1) * 8, :]

    _gather_pipeline(idx_ref, tab_ref, tiles, tb, dots)
    m3 = m_ref[...].reshape(tb, 8, PEER_SLOTS * 8)
    z = jnp.sum(jnp.where(_diag_mask()[None], m3, 0.0), axis=1)
    rr = lax.broadcasted_iota(jnp.int32, (PEER_SLOTS * 8, PEER_SLOTS), 0)
    cc = lax.broadcasted_iota(jnp.int32, (PEER_SLOTS * 8, PEER_SLOTS), 1)
    pool = jnp.where((rr >> 3) == cc, 1.0, 0.0).astype(bf16)
    z_hi = z.astype(bf16)
    z_lo = (z - z_hi.astype(f32)).astype(bf16)
    a_ref[...] = (jnp.dot(z_hi, pool, preferred_element_type=f32)
                  + jnp.dot(z_lo, pool, preferred_element_type=f32))


def _peer_out_kernel(idx_ref, a_ref, gate_ref, tab_ref, o_ref, w_ref, *tiles):
    f32, bf16 = jnp.float32, jnp.bfloat16
    tb = a_ref.shape[0]
    a = a_ref[...]
    act = 0.5 * a * (1.0 + lax.erf(a * (2.0 ** -0.5)))
    wgt = (gate_ref[...] * act).astype(bf16)
    rr = lax.broadcasted_iota(jnp.int32, (PEER_SLOTS, PEER_SLOTS * 8), 0)
    cc = lax.broadcasted_iota(jnp.int32, (PEER_SLOTS, PEER_SLOTS * 8), 1)
    expand = jnp.where((cc >> 3) == rr, 1.0, 0.0).astype(bf16)
    w_ref[...] = jnp.dot(wgt, expand, preferred_element_type=f32)
    diag = _diag_mask()

    def combine(g_ref, tp, u):
        t = tp * 2 + u
        wrow = jnp.broadcast_to(w_ref[pl.ds(t, 1), :], (8, PEER_SLOTS * 8))
        wsel = jnp.where(diag, wrow, 0.0).astype(bf16)
        o_ref[pl.ds(pl.multiple_of(t * 8, 8), 8), :] = jnp.dot(
            wsel, pltpu.bitcast(g_ref[...], bf16), preferred_element_type=f32)

    _gather_pipeline(idx_ref, tab_ref, tiles, tb, combine)


def _pack_table(tab):
    n_e = tab.shape[0]
    tb = lax.bitcast_convert_type(
        tab.astype(jnp.bfloat16).reshape(n_e, WORDS_PER_ROW, 2, 128), jnp.uint16).astype(jnp.uint32)
    word = tb[:, :, 0, :] | (tb[:, :, 1, :] << 16)
    return lax.bitcast_convert_type(word, jnp.int32).reshape(n_e * WORDS_PER_ROW, 128)


def peer_pallas(x, g_ffn, w_pq, sub_keys, u_emb, v_emb):
    n_tok, d = x.shape
    f32, bf16 = jnp.float32, jnp.bfloat16
    ts, tg = PEER_SCORE_TOKENS, PEER_GATHER_TOKENS
    wpqt = w_pq.reshape(d, P_HEADS * P_QDIM).T.astype(bf16)
    sk = sub_keys.reshape(P_HEADS * 2, N_KEYS, P_QDIM // 2).astype(bf16)
    hn, eidx, gate = pl.pallas_call(
        _peer_score_kernel,
        grid=(n_tok // ts,),
        in_specs=[pl.BlockSpec((ts, d), lambda i: (i, 0)),
                  pl.BlockSpec((1, d), lambda i: (0, 0)),
                  pl.BlockSpec(wpqt.shape, lambda i: (0, 0)),
                  pl.BlockSpec(sk.shape, lambda i: (0, 0, 0))],
        out_specs=[pl.BlockSpec((ts, d), lambda i: (i, 0)),
                   pl.BlockSpec((ts, PEER_SLOTS), lambda i: (i, 0)),
                   pl.BlockSpec((ts, PEER_SLOTS), lambda i: (i, 0))],
        out_shape=[jax.ShapeDtypeStruct((n_tok, d), bf16),
                   jax.ShapeDtypeStruct((n_tok, PEER_SLOTS), jnp.int32),
                   jax.ShapeDtypeStruct((n_tok, PEER_SLOTS), f32)],
        compiler_params=pltpu.CompilerParams(vmem_limit_bytes=VMEM_LIMIT),
        name="peer_score",
    )(x, g_ffn.reshape(1, d), wpqt, sk)
    utab, vtab = _pack_table(u_emb), _pack_table(v_emb)
    idx_spec = pl.BlockSpec((tg, PEER_SLOTS), lambda i: (i, 0), memory_space=pltpu.SMEM)
    tab_spec = pl.BlockSpec(memory_space=pltpu.VMEM)
    slot_spec = pl.BlockSpec((tg, PEER_SLOTS), lambda i: (i, 0))
    row_spec = pl.BlockSpec((tg * 8, 128), lambda i: (i, 0))
    gbuf = pltpu.VMEM((PEER_SLOTS * WORDS_PER_ROW, 128), jnp.int32)
    cparams = pltpu.CompilerParams(vmem_limit_bytes=VMEM_LIMIT)
    act = pl.pallas_call(
        _peer_act_kernel,
        grid=(n_tok // tg,),
        in_specs=[idx_spec, row_spec, tab_spec],
        out_specs=slot_spec,
        out_shape=jax.ShapeDtypeStruct((n_tok, PEER_SLOTS), f32),
        scratch_shapes=[pltpu.VMEM((tg * 8, PEER_SLOTS * 8), f32)] + [gbuf] * PEER_TILES,
        compiler_params=cparams,
        name="peer_act",
    )(eidx, hn.reshape(n_tok * 8, 128), utab)
    out = pl.pallas_call(
        _peer_out_kernel,
        grid=(n_tok // tg,),
        in_specs=[idx_spec, slot_spec, slot_spec, tab_spec],
        out_specs=row_spec,
        out_shape=jax.ShapeDtypeStruct((n_tok * 8, 128), f32),
        scratch_shapes=[pltpu.VMEM((tg, PEER_SLOTS * 8), f32)] + [gbuf] * PEER_TILES,
        compiler_params=cparams,
        name="peer_out",
    )(eidx, act, gate, vtab)
    return out.reshape(n_tok, d)


def _final_kernel(x_ref, y_ref, g_ref, o_ref):
    o_ref[...] = _rms(x_ref[...] + y_ref[...], g_ref[...])


def final_residual_rmsnorm(x, y, g):
    n, d = x.shape
    tm = 512
    row = pl.BlockSpec((tm, d), lambda i: (i, 0))
    return pl.pallas_call(
        _final_kernel,
        grid=(n // tm,),
        in_specs=[row, row, pl.BlockSpec((1, d), lambda i: (0, 0))],
        out_specs=row,
        out_shape=jax.ShapeDtypeStruct((n, d), x.dtype),
        name="final_rmsnorm",
    )(x, y, g.reshape(1, d))


IN_PROJ_TOKENS = 512
GDN_COLS = 3 * B_QK
GATE_ROWS = 8


def _in_proj_kernel(x_ref, g_ref, wa_ref, wkw_ref, wqkv_ref, wz_ref, wab_ref, wabt_ref,
                    cq_ref, ckv_ref, kw_ref, qkv_ref, z_ref, ab_ref, abt_ref):
    f32 = jnp.float32
    hb = _rms(x_ref[0], g_ref[...]).astype(jnp.bfloat16)
    a = jnp.dot(hb, wa_ref[...], preferred_element_type=f32)
    cq_ref[0] = a[:, :A_Q_LORA]
    ckv_ref[0] = a[:, A_Q_LORA:]
    kw_ref[0] = jnp.dot(hb, wkw_ref[...], preferred_element_type=f32)
    qkv_ref[0] = jnp.dot(hb, wqkv_ref[...], preferred_element_type=f32)
    z_ref[0] = jnp.dot(hb, wz_ref[...], preferred_element_type=f32)
    ab_ref[0] = jnp.dot(hb, wab_ref[...], preferred_element_type=f32)
    abt_ref[0] = _nt_dot(wabt_ref[...], hb)


def in_proj_pallas(x, g_mix, w_in):
    bsz, seq, d = x.shape
    f32, bf16 = jnp.float32, jnp.bfloat16
    tm = min(IN_PROJ_TOKENS, seq)
    o = np.cumsum((0,) + COL_WIDTHS)
    wb = w_in.astype(bf16)
    pad_cols = lambda w: jnp.pad(w, ((0, 0), (0, 128 - w.shape[1])))
    wa = wb[:, o[0]:o[2]]
    wkw = pad_cols(wb[:, o[2]:o[4]])
    wqkv = wb[:, o[4]:o[7]]
    wz = wb[:, o[7]:o[8]]
    wab = pad_cols(wb[:, o[8]:o[10]])
    wabt = jnp.concatenate([wb[:, o[8]:o[10]], wb[:, o[3]:o[4]]], axis=1).T
    full = lambda w: pl.BlockSpec(w.shape, lambda b, t: (0, 0))
    tok = lambda w: pl.BlockSpec((1, tm, w), lambda b, t: (b, t, 0))
    shp = lambda w: jax.ShapeDtypeStruct((bsz, seq, w), f32)
    return pl.pallas_call(
        _in_proj_kernel,
        grid=(bsz, seq // tm),
        in_specs=[tok(d), pl.BlockSpec((1, d), lambda b, t: (0, 0)),
                  full(wa), full(wkw), full(wqkv), full(wz), full(wab), full(wabt)],
        out_specs=[tok(A_Q_LORA), tok(A_KV_LORA), tok(128), tok(GDN_COLS), tok(B_QK), tok(128),
                   pl.BlockSpec((1, GATE_ROWS + IDX_HEADS, tm), lambda b, t: (b, 0, t))],
        out_shape=[shp(A_Q_LORA), shp(A_KV_LORA), shp(128), shp(GDN_COLS), shp(B_QK), shp(128),
                   jax.ShapeDtypeStruct((bsz, GATE_ROWS + IDX_HEADS, seq), f32)],
        compiler_params=pltpu.CompilerParams(vmem_limit_bytes=VMEM_LIMIT),
        name="in_proj",
    )(x, g_mix.reshape(1, d), wa, wkw, wqkv, wz, wab, wabt)


def _softplus(x):
    return jnp.maximum(x, 0.0) + jnp.log1p(jnp.exp(-jnp.abs(x)))


def _sigmoid(x):
    return 1.0 / (1.0 + jnp.exp(-x))


def _gdn_gates(pre, a_log, dt_bias, is_decay):
    g = -jnp.exp(a_log) * _softplus(pre + dt_bias)
    return jnp.where(is_decay, g, _sigmoid(pre))


def _gdn_prep_kernel(qkv_ref, halo_ref, cw_ref, ab_ref, abt_ref, alc_ref, dtc_ref, alr_ref, dtr_ref,
                     q_ref, k_ref, v_ref, gc_ref, gr_ref):
    tm = qkv_ref.shape[1]
    x = qkv_ref[0]
    halo = jnp.where(pl.program_id(1) > 0, halo_ref[0], 0.0)
    full = jnp.concatenate([halo, x], axis=0)
    y = x * cw_ref[CONV_WIDTH - 1:CONV_WIDTH, :]
    for back in range(1, CONV_WIDTH):
        shifted = pltpu.roll(full, back, axis=0)[8:, :]
        y = y + shifted * cw_ref[CONV_WIDTH - 1 - back:CONV_WIDTH - back, :]
    y = y * _sigmoid(y)
    for h in range(B_HEADS):
        cols = slice(h * B_HEAD_DIM, (h + 1) * B_HEAD_DIM)
        qh = y[:, h * B_HEAD_DIM:(h + 1) * B_HEAD_DIM]
        kh = y[:, B_QK + h * B_HEAD_DIM:B_QK + (h + 1) * B_HEAD_DIM]
        q_ref[0, :, cols] = qh * lax.rsqrt(
            jnp.sum(qh * qh, axis=-1, keepdims=True) + EPS) * (B_HEAD_DIM ** -0.5)
        k_ref[0, :, cols] = kh * lax.rsqrt(jnp.sum(kh * kh, axis=-1, keepdims=True) + EPS)
    v_ref[0] = y[:, 2 * B_QK:]
    lane = lax.broadcasted_iota(jnp.int32, (tm, 128), 1)
    gates_c = _gdn_gates(ab_ref[0], alc_ref[...], dtc_ref[...], lane < B_HEADS)
    row = lax.broadcasted_iota(jnp.int32, (GATE_ROWS, tm), 0)
    gates_r = _gdn_gates(abt_ref[0], alr_ref[...], dtr_ref[...], row < B_HEADS)
    ti = lax.broadcasted_iota(jnp.int32, (tm, tm), 0)
    tj = lax.broadcasted_iota(jnp.int32, (tm, tm), 1)
    same_chunk = (ti // CHUNK) == (tj // CHUNK)
    hi = lax.Precision.HIGHEST
    lower = jnp.where(same_chunk & (tj <= ti), 1.0, 0.0)
    upper = jnp.where(same_chunk & (ti <= tj), 1.0, 0.0)
    cum_c = jnp.dot(lower, gates_c, preferred_element_type=jnp.float32, precision=hi)
    cum_r = jnp.dot(gates_r, upper, preferred_element_type=jnp.float32, precision=hi)
    gc_ref[0] = jnp.where(lane < B_HEADS, cum_c, gates_c)
    gr_ref[0] = jnp.where(row < B_HEADS, cum_r, gates_r)


def _gdn_main_kernel(q_ref, k_ref, v_ref, z_ref, gc_ref, gr_ref, gon_ref, o_ref, s_ref):
    f32, bf16 = jnp.float32, jnp.bfloat16
    C, Dh = CHUNK, B_HEAD_DIM
    hi = lax.Precision.HIGHEST

    @pl.when(pl.program_id(1) == 0)
    def _():
        s_ref[...] = jnp.zeros(s_ref.shape, f32)

    ii = lax.broadcasted_iota(jnp.int32, (C, C), 0)
    jj = lax.broadcasted_iota(jnp.int32, (C, C), 1)
    causal = ii >= jj
    strict = ii > jj
    eye = jnp.where(ii == jj, 1.0, 0.0)
    mm = lambda a, b: jnp.dot(a, b, preferred_element_type=f32)
    mmh = lambda a, b: jnp.dot(a, b, preferred_element_type=f32, precision=hi)
    n_chunks = q_ref.shape[1] // C
    units = []
    for c in range(n_chunks):
        rows = slice(c * C, (c + 1) * C)
        gates_c = gc_ref[0, rows, :]
        gates_r = gr_ref[0, :, c * C:(c + 1) * C]
        for h in range(B_HEADS):
            cols = slice(h * Dh, (h + 1) * Dh)
            gcum = jnp.broadcast_to(gates_c[:, h:h + 1], (C, Dh))
            beta = jnp.broadcast_to(gates_c[:, B_HEADS + h:B_HEADS + h + 1], (C, Dh))
            gcum_r = jnp.broadcast_to(gates_r[h:h + 1, :], (C, C))
            decay = jnp.where(causal, jnp.exp(jnp.minimum(gcum[:, :C] - gcum_r, 0.0)), 0.0)
            q, k, v = q_ref[0, rows, cols], k_ref[0, rows, cols], v_ref[0, rows, cols]
            qb, kb = q.astype(bf16), k.astype(bf16)
            kk = _nt_dot(kb, kb)
            qk = _nt_dot(qb, kb)
            egc = jnp.exp(gcum)
            g_last = gcum[C - 1:C, :]
            units.append(dict(
                c=c, h=h, rows=rows, cols=cols,
                neg_m=jnp.where(strict, -(beta[:, :C] * kk * decay), 0.0),
                rhs=jnp.concatenate([v * beta, k * (beta * egc)], axis=1),
                q_dec=(q * egc).astype(bf16), intra=(qk * decay).astype(bf16),
                k_tail=(k * jnp.exp(g_last - gcum)).astype(bf16), chunk_decay=jnp.exp(g_last)))
    powers = [un["neg_m"] for un in units]
    t_inv = [eye + p for p in powers]
    for _ in range(5):
        powers = [mmh(p, p) for p in powers]
        t_inv = [t + mmh(p, t) for p, t in zip(powers, t_inv)]
    sols = [mmh(t, un["rhs"]) for t, un in zip(t_inv, units)]
    states = [s_ref[h] for h in range(B_HEADS)]
    for c in range(n_chunks):
        group = [(un, sol) for un, sol in zip(units, sols) if un["c"] == c]
        sbs = [states[un["h"]].astype(bf16) for un, _ in group]
        vbs = [(sol[:, :Dh] - mm(sol[:, Dh:].astype(bf16), sb)).astype(bf16)
               for (un, sol), sb in zip(group, sbs)]
        outs = [mm(un["q_dec"], sb) + mm(un["intra"], vb)
                for (un, _), sb, vb in zip(group, sbs, vbs)]
        for (un, _), vb in zip(group, vbs):
            states[un["h"]] = states[un["h"]] * un["chunk_decay"] + lax.dot_general(
                un["k_tail"], vb, (((0,), (0,)), ((), ())), preferred_element_type=f32)
        for (un, _), o in zip(group, outs):
            o = o * lax.rsqrt(jnp.mean(o * o, axis=-1, keepdims=True) + EPS) * gon_ref[...]
            zz = z_ref[0, un["rows"], un["cols"]]
            o_ref[0, un["rows"], un["cols"]] = o * (zz * _sigmoid(zz))
    for h in range(B_HEADS):
        s_ref[h] = states[h]


def gdn_pallas(qkv, z, ab, abt, conv_w, a_log, dt_bias, g_onorm):
    bsz, seq, _ = qkv.shape
    f32 = jnp.float32
    tm = min(256, seq)
    zero4 = jnp.zeros((B_HEADS,), f32)
    lane_row = lambda v: jnp.pad(jnp.concatenate([v.astype(f32), zero4]), (0, 120)).reshape(1, 128)
    sub_col = lambda v: jnp.concatenate([v.astype(f32), zero4]).reshape(GATE_ROWS, 1)
    tok = lambda w, t=tm: pl.BlockSpec((1, t, w), lambda b, i: (b, i, 0))
    const = lambda shape: pl.BlockSpec(shape, lambda b, i: (0,) * len(shape))
    shp = lambda w: jax.ShapeDtypeStruct((bsz, seq, w), f32)
    q, k, v, gc, gr = pl.pallas_call(
        _gdn_prep_kernel,
        grid=(bsz, seq // tm),
        in_specs=[tok(GDN_COLS),
                  pl.BlockSpec((1, 8, GDN_COLS), lambda b, i: (b, jnp.maximum(i * (tm // 8) - 1, 0), 0)),
                  const((CONV_WIDTH, GDN_COLS)), tok(128),
                  pl.BlockSpec((1, GATE_ROWS, tm), lambda b, i: (b, 0, i)),
                  const((1, 128)), const((1, 128)), const((GATE_ROWS, 1)), const((GATE_ROWS, 1))],
        out_specs=[tok(B_QK), tok(B_QK), tok(B_QK), tok(128),
                   pl.BlockSpec((1, GATE_ROWS, tm), lambda b, i: (b, 0, i))],
        out_shape=[shp(B_QK), shp(B_QK), shp(B_QK), shp(128),
                   jax.ShapeDtypeStruct((bsz, GATE_ROWS, seq), f32)],
        compiler_params=pltpu.CompilerParams(vmem_limit_bytes=VMEM_LIMIT),
        name="gdn_prep",
    )(qkv, qkv, conv_w.astype(f32), ab, abt, lane_row(a_log), lane_row(dt_bias),
      sub_col(a_log), sub_col(dt_bias))
    ts = 2 * CHUNK
    return pl.pallas_call(
        _gdn_main_kernel,
        grid=(bsz, seq // ts),
        in_specs=[tok(B_QK, ts), tok(B_QK, ts), tok(B_QK, ts), tok(B_QK, ts), tok(128, ts),
                  pl.BlockSpec((1, GATE_ROWS, ts), lambda b, i: (b, 0, i)),
                  const((1, B_HEAD_DIM))],
        out_specs=tok(B_QK, ts),
        out_shape=shp(B_QK),
        scratch_shapes=[pltpu.VMEM((B_HEADS, B_HEAD_DIM, B_HEAD_DIM), f32)],
        compiler_params=pltpu.CompilerParams(dimension_semantics=("arbitrary", "arbitrary")),
        name="gdn_main",
    )(q, k, v, z, gc, gr, g_onorm.astype(f32).reshape(1, B_HEAD_DIM))


def _mem_kv_kernel(mem_ref, g_ref, wk_ref, wv_ref, k_ref, v_ref):
    f32, bf16 = jnp.float32, jnp.bfloat16
    mn = _rms(mem_ref[0], g_ref[...]).astype(bf16)
    k = jnp.dot(mn, wk_ref[...], preferred_element_type=f32)
    v = jnp.dot(mn, wv_ref[...], preferred_element_type=f32)
    for h in range(X_HEADS):
        cols = slice(h * X_HEAD_DIM, (h + 1) * X_HEAD_DIM)
        k_ref[0, h] = k[:, cols].astype(bf16)
        v_ref[0, h] = v[:, cols].astype(bf16)


def _mid_kernel(x_ref, oa_ref, ob_ref, wo_ref, gx_ref, wq_ref, k_ref, v_ref, wox_ref, o_ref):
    f32, bf16 = jnp.float32, jnp.bfloat16
    na = oa_ref.shape[2]
    x1 = (x_ref[0]
          + jnp.dot(oa_ref[0].astype(bf16), wo_ref[:na, :], preferred_element_type=f32)
          + jnp.dot(ob_ref[0].astype(bf16), wo_ref[na:, :], preferred_element_type=f32))
    hq = _rms(x1, gx_ref[...]).astype(bf16)
    q = jnp.dot(hq, wq_ref[...], preferred_element_type=f32)
    heads = []
    for h in range(X_HEADS):
        qh = q[:, h * X_HEAD_DIM:(h + 1) * X_HEAD_DIM].astype(bf16)
        lg = _nt_dot(qh, k_ref[0, h]) * (X_HEAD_DIM ** -0.5)
        p = jnp.exp(lg - jnp.max(lg, axis=-1, keepdims=True))
        p = (p / jnp.sum(p, axis=-1, keepdims=True)).astype(bf16)
        heads.append(jnp.dot(p, v_ref[0, h], preferred_element_type=f32).astype(bf16))
    o = jnp.concatenate(heads, axis=1)
    o_ref[0] = x1 + jnp.dot(o, wox_ref[...], preferred_element_type=f32)


def mid_pallas(x, o_a, o_b, w_out, g_cross, mem, g_mem, wq, wk, wv, wo):
    bsz, seq, d = x.shape
    f32, bf16 = jnp.float32, jnp.bfloat16
    hx = X_HEADS * X_HEAD_DIM
    m_len = mem.shape[1]
    const2 = lambda shape: pl.BlockSpec(shape, lambda b: (0,) * len(shape))
    kv_spec = pl.BlockSpec((1, X_HEADS, m_len, X_HEAD_DIM), lambda b: (b, 0, 0, 0))
    kv_shape = jax.ShapeDtypeStruct((bsz, X_HEADS, m_len, X_HEAD_DIM), bf16)
    k, v = pl.pallas_call(
        _mem_kv_kernel,
        grid=(bsz,),
        in_specs=[pl.BlockSpec((1, m_len, d), lambda b: (b, 0, 0)), const2((1, d)),
                  const2((d, hx)), const2((d, hx))],
        out_specs=[kv_spec, kv_spec],
        out_shape=[kv_shape, kv_shape],
        name="mem_kv",
    )(mem, g_mem.reshape(1, d), wk.reshape(d, hx).astype(bf16), wv.reshape(d, hx).astype(bf16))
    tm = min(256, seq)
    tok = lambda w: pl.BlockSpec((1, tm, w), lambda b, t: (b, t, 0))
    const = lambda shape: pl.BlockSpec(shape, lambda b, t: (0,) * len(shape))
    kv_spec2 = pl.BlockSpec((1, X_HEADS, m_len, X_HEAD_DIM), lambda b, t: (b, 0, 0, 0))
    return pl.pallas_call(
        _mid_kernel,
        grid=(bsz, seq // tm),
        in_specs=[tok(d), tok(o_a.shape[2]), tok(o_b.shape[2]), const((MIX_WIDTH, d)),
                  const((1, d)), const((d, hx)), kv_spec2, kv_spec2, const((hx, d))],
        out_specs=tok(d),
        out_shape=jax.ShapeDtypeStruct((bsz, seq, d), f32),
        compiler_params=pltpu.CompilerParams(vmem_limit_bytes=VMEM_LIMIT),
        name="mid",
    )(x, o_a, o_b, w_out.astype(bf16), g_cross.reshape(1, d), wq.reshape(d, hx).astype(bf16),
      k, v, wo.reshape(hx, d).astype(bf16))


def kernel(x, mem, g_mix, w_in, g_cq, g_ckv, g_kidx, w_uq, w_iq, w_uk, w_uv, rel_bias, conv_w, A_log, dt_bias, g_onorm, w_out, g_cross, g_mem, wq_x, wk_x, wv_x, wo_x, g_ffn, w_pq, sub_keys, u_emb, v_emb, g_final):
    bsz, seq, d = x.shape
    for l in range(DEPTH):
        cq, ckv, kw, qkv, z, ab, abt = in_proj_pallas(x, g_mix[l], w_in[l])
        o_a = dsa_pallas_t(cq, ckv, kw, abt, g_cq[l], g_ckv[l], g_kidx[l],
                           w_uq[l], w_iq[l], w_uk[l], w_uv[l], rel_bias)
        o_b = gdn_pallas(qkv, z, ab, abt, conv_w[l], A_log[l], dt_bias[l], g_onorm[l])
        x = mid_pallas(x, o_a, o_b, w_out[l], g_cross[l], mem, g_mem[l],
                       wq_x[l], wk_x[l], wv_x[l], wo_x[l])
        xf = x.reshape(bsz * seq, d)
        y = peer_pallas(xf, g_ffn[l], w_pq[l], sub_keys[l], u_emb[l], v_emb[l])
        if l + 1 < DEPTH:
            x = (xf + y).reshape(bsz, seq, d)
    return final_residual_rmsnorm(xf, y, g_final).reshape(bsz, seq, d)
```

```python
import math
from functools import partial
import jax
import jax.numpy as jnp
from jax import lax
import numpy as np
from jax.experimental import pallas as pl
from jax.experimental.pallas import tpu as pltpu

D_MODEL = 1024
BATCH = 4
SEQ = 8192
DEPTH = 1

CHUNK = 64
Q_BLOCK = 128
EPS = 1e-6

A_HEADS = 8
A_HEAD_DIM = 64
A_Q_LORA = 256
A_KV_LORA = 256
IDX_HEADS = 8
IDX_DIM = 64
IDX_TOPK_MAX = 256
ATTN_SCALE = A_HEAD_DIM ** -0.5
IDX_SCALE = (IDX_HEADS * IDX_DIM) ** -0.5

B_HEADS = 4
B_HEAD_DIM = 128
B_QK = B_HEADS * B_HEAD_DIM
CONV_WIDTH = 4

REL_BUCKETS = 32
REL_MAX_DIST = 128

MEM_LEN = 256
X_HEADS = 4
X_HEAD_DIM = 128

P_HEADS = 8
N_KEYS = 128
N_EXPERTS = N_KEYS * N_KEYS
P_TOPK = 16
P_QDIM = 256
P_TOKEN_BLOCK = 128

COL_WIDTHS = (A_Q_LORA, A_KV_LORA, IDX_DIM, IDX_HEADS, B_QK, B_QK, B_QK, B_QK, B_HEADS, B_HEADS)
IN_COLS = sum(COL_WIDTHS)
MIX_WIDTH = A_HEADS * A_HEAD_DIM + B_HEADS * B_HEAD_DIM


def rmsnorm(x, g):
    xf = x.astype(jnp.float32)
    y = xf * lax.rsqrt(jnp.mean(xf * xf, axis=-1, keepdims=True) + EPS)
    return (y * g.astype(jnp.float32)).astype(x.dtype)


def l2norm(x):
    return x * lax.rsqrt(jnp.sum(x * x, axis=-1, keepdims=True) + EPS)


def split_columns(p, widths):
    out, start = [], 0
    for w in widths:
        out.append(p[..., start:start + w])
        start += w
    return out


def t5_bucket(rel):
    half = REL_BUCKETS // 2
    max_exact = half // 2
    n = jnp.abs(rel)
    nf = jnp.maximum(n, max_exact).astype(jnp.float32)
    large = max_exact + (jnp.log(nf / max_exact) / math.log(REL_MAX_DIST / max_exact)
                         * (half - max_exact)).astype(jnp.int32)
    large = jnp.minimum(large, half - 1)
    return jnp.where(rel > 0, half, 0) + jnp.where(n < max_exact, n, large)


def dsa_attention(cq, ckv, kidx, widx, g_cq, g_ckv, g_kidx, w_uq, w_iq, w_uk, w_uv, rel_bias):
    bsz, seq, _ = cq.shape
    topk = min(IDX_TOPK_MAX, seq // 4)
    n_blocks = seq // Q_BLOCK
    cq = rmsnorm(cq, g_cq)
    ckv = rmsnorm(ckv, g_ckv)
    kidx = rmsnorm(kidx, g_kidx).astype(jnp.float32)
    key_pos = jnp.arange(seq, dtype=jnp.int32)

    def blockify(a):
        return jnp.moveaxis(a.reshape(bsz, n_blocks, Q_BLOCK, a.shape[-1]), 1, 0)

    def one_block(args):
        cq_b, w_b, start = args
        qpos = start + jnp.arange(Q_BLOCK, dtype=jnp.int32)
        limit = (qpos // CHUNK + 1) * CHUNK
        admissible = key_pos[None, :] < limit[:, None]
        qi = jnp.einsum('bqc,chd->bqhd', cq_b, w_iq).astype(jnp.float32)
        dots = jax.nn.relu(jnp.einsum('bqhd,bsd->bqhs', qi, kidx))
        score = jnp.einsum('bqh,bqhs->bqs', w_b.astype(jnp.float32) * IDX_SCALE, dots)
        score = jnp.where(admissible[None], score, -jnp.inf)
        _, sel = lax.top_k(score, topk)
        valid = sel < limit[None, :, None]
        c_sel = jax.vmap(lambda c, i: c[i])(ckv, sel)
        q = jnp.einsum('bqc,chd->bqhd', cq_b, w_uq)
        q_lat = jnp.einsum('bqhd,chd->bqhc', q, w_uk)
        logits = jnp.einsum('bqhc,bqkc->bhqk', q_lat, c_sel).astype(jnp.float32) * ATTN_SCALE
        bias = rel_bias[t5_bucket(sel - qpos[None, :, None])].astype(jnp.float32)
        logits = logits + jnp.transpose(bias, (0, 3, 1, 2))
        logits = jnp.where(valid[:, None], logits, -jnp.inf)
        p = jax.nn.softmax(logits, axis=-1).astype(c_sel.dtype)
        o_lat = jnp.einsum('bhqk,bqkc->bqhc', p, c_sel)
        o = jnp.einsum('bqhc,chd->bqhd', o_lat, w_uv)
        return o.reshape(bsz, Q_BLOCK, A_HEADS * A_HEAD_DIM)

    starts = jnp.arange(n_blocks, dtype=jnp.int32) * Q_BLOCK
    o = lax.map(one_block, (blockify(cq), blockify(widx), starts))
    return jnp.moveaxis(o, 0, 1).reshape(bsz, seq, A_HEADS * A_HEAD_DIM)


INT_MIN = -2147483648
NEG_BIG = -1e30
KEY_TILE = 512
KEY_PAD = KEY_TILE - Q_BLOCK
VMEM_LIMIT = 56 * 1024 * 1024


def _rms(x, g):
    return x * lax.rsqrt(jnp.mean(x * x, axis=-1, keepdims=True) + EPS) * g


def _nt_dot(a, b):
    return lax.dot_general(a, b, (((1,), (1,)), ((), ())), preferred_element_type=jnp.float32)


def _dsa_prep_kernel(cq_ref, ckv_ref, kw_ref, gcq_ref, gckv_ref, gk_ref,
                     wiq_ref, wuq_ref, wukt_ref,
                     a_ref, qlat_ref, ckvn_ref, kidxn_ref, w_ref):
    bf16 = jnp.bfloat16
    kw = kw_ref[0]
    cqn = _rms(cq_ref[0], gcq_ref[...]).astype(bf16)
    qi = jnp.dot(cqn, wiq_ref[...], preferred_element_type=jnp.float32)
    q = jnp.dot(cqn, wuq_ref[...], preferred_element_type=jnp.float32)
    for h in range(A_HEADS):
        rows = slice(h * Q_BLOCK, (h + 1) * Q_BLOCK)
        a_ref[0, 0, rows, :] = qi[:, h * IDX_DIM:(h + 1) * IDX_DIM].astype(bf16)
        qh = q[:, h * A_HEAD_DIM:(h + 1) * A_HEAD_DIM].astype(bf16)
        ql = jnp.dot(qh, wukt_ref[h], preferred_element_type=jnp.float32) * ATTN_SCALE
        qlat_ref[0, 0, rows, :] = ql.astype(bf16)
    ckvn_ref[0] = _rms(ckv_ref[0], gckv_ref[...]).astype(bf16)
    kidxn_ref[0] = _rms(kw[:, :IDX_DIM], gk_ref[...]).astype(bf16)
    w_ref[0, 0] = kw[:, IDX_DIM:IDX_DIM + IDX_HEADS] * IDX_SCALE


def _dsa_main_kernel(a_ref, qlat_ref, w_ref, kidx_ref, ckv_ref, bias0_ref, wuv_ref, o_ref,
                     sc_ref, big_ref, p_ref, acc_ref, m_ref, l_ref, alpha_ref, wb_ref, *, topk):
    f32, i32, bf16 = jnp.float32, jnp.int32, jnp.bfloat16
    Q, T = Q_BLOCK, KEY_TILE
    NC = T // 128
    i = pl.program_id(1)
    e = (i + 1) * Q
    nt = (i + T // Q) // (T // Q)
    kf = jnp.float32(topk)

    w_blk = w_ref[0, 0]
    for h in range(IDX_HEADS):
        wb_ref[h] = jnp.broadcast_to(w_blk[:, h:h + 1], (Q, 128))
    row = lax.broadcasted_iota(i32, (Q, 128), 0)
    lane = lax.broadcasted_iota(i32, (Q, 128), 1)
    limit = ((i * Q + row) // CHUNK + 1) * CHUNK
    a_mat = a_ref[0, 0]

    def tile_start(j):
        return pl.multiple_of(e + KEY_PAD - (j + 1) * T, 128)

    def score_tile(j, carry):
        start = tile_start(j)
        big_ref[...] = _nt_dot(a_mat, kidx_ref[0, pl.ds(start, T), :])
        for c in range(NC):
            cols = slice(c * 128, (c + 1) * 128)
            s = jnp.zeros((Q, 128), f32)
            for h in range(IDX_HEADS):
                s = s + wb_ref[h] * jnp.maximum(big_ref[h * Q:(h + 1) * Q, cols], 0.0)
            bits = lax.bitcast_convert_type(s, i32)
            key = jnp.where(bits < 0, bits ^ jnp.int32(0x7FFFFFFF), bits)
            key = jnp.where(s == 0.0, 0, key)
            kpos = start - KEY_PAD + c * 128 + lane
            key = jnp.where(kpos >= 0, key, INT_MIN)
            key = jnp.where(kpos < limit, key, INT_MIN)
            sc_ref[j, :, cols] = key
        return carry

    lax.fori_loop(0, nt, score_tile, 0)

    def count_where(pred):
        def body(j, cnt):
            for c in range(NC):
                cols = slice(c * 128, (c + 1) * 128)
                kpos = tile_start(j) - KEY_PAD + c * 128 + lane
                cnt = cnt + jnp.where(pred(sc_ref[j, :, cols], kpos), 1.0, 0.0)
            return cnt
        cnt = lax.fori_loop(0, nt, body, jnp.zeros((Q, 128), f32))
        return jnp.broadcast_to(jnp.sum(cnt, axis=1, keepdims=True), (Q, 128))

    def bit_body(b, carry):
        u, cacc = carry
        cand = u | lax.shift_left(jnp.int32(1), 31 - b)
        tvec = cand ^ jnp.int32(INT_MIN)
        tot = count_where(lambda k, kpos: k >= tvec)
        ok = tot >= kf
        return jnp.where(ok, cand, u), jnp.where(ok, tot, cacc)

    u, cacc = lax.fori_loop(0, 32, bit_body,
                            (jnp.zeros((Q, 128), i32), jnp.zeros((Q, 128), f32)))
    thr = jnp.maximum(u ^ jnp.int32(INT_MIN), INT_MIN + 1)
    overflow = jnp.where(u != 0, cacc, 0.0) > kf
    n_over = jnp.max(jnp.max(jnp.where(overflow, 1.0, 0.0), axis=1, keepdims=True),
                     axis=0, keepdims=True)[0, 0]

    @pl.when(n_over > 0.0)
    def _():
        need = kf - count_where(lambda k, kpos: k > thr)

        def cut_body(b, cut):
            cand = cut | lax.shift_left(jnp.int32(1), 14 - b)
            cnt = count_where(lambda k, kpos: jnp.where(k == thr, kpos, cand) < cand)
            return jnp.where(cnt <= need, cand, cut)

        cut = lax.fori_loop(0, 15, cut_body, jnp.zeros((Q, 128), i32))

        def drop_tile(j, carry):
            for c in range(NC):
                cols = slice(c * 128, (c + 1) * 128)
                kpos = tile_start(j) - KEY_PAD + c * 128 + lane
                k = sc_ref[j, :, cols]
                drop = jnp.where(k == thr, kpos, -1) >= cut
                sc_ref[j, :, cols] = jnp.where(drop, INT_MIN, k)
            return carry

        lax.fori_loop(0, nt, drop_tile, 0)

    m_ref[...] = jnp.full(m_ref.shape, NEG_BIG, f32)
    l_ref[...] = jnp.zeros(l_ref.shape, f32)
    acc_ref[...] = jnp.zeros(acc_ref.shape, f32)
    qlat = qlat_ref[0, 0]

    def attn_tile(j, with_bias):
        start = tile_start(j)
        kv = ckv_ref[0, pl.ds(start, T), :]
        big_ref[...] = _nt_dot(qlat, kv)
        mask_add = [jnp.where(sc_ref[j, :, c * 128:(c + 1) * 128] >= thr, 0.0, NEG_BIG)
                    for c in range(NC)]
        for h in range(A_HEADS):
            rows = slice(h * Q, (h + 1) * Q)
            lg = []
            for c in range(NC):
                x = big_ref[rows, c * 128:(c + 1) * 128] + mask_add[c]
                if with_bias:
                    x = x + bias0_ref[h, :, c * 128:(c + 1) * 128]
                lg.append(x)
            mx = lg[0]
            for c in range(1, NC):
                mx = jnp.maximum(mx, lg[c])
            m_prev = m_ref[rows, :]
            m_new = jnp.maximum(m_prev, jnp.max(mx, axis=1, keepdims=True))
            psum = jnp.zeros((Q, 128), f32)
            for c in range(NC):
                p = jnp.exp(lg[c] - m_new)
                psum = psum + p
                p_ref[rows, c * 128:(c + 1) * 128] = p.astype(bf16)
            alpha = jnp.exp(m_prev - m_new)
            l_ref[rows, :] = alpha * l_ref[rows, :] + jnp.sum(psum, axis=1, keepdims=True)
            m_ref[rows, :] = m_new
            alpha_ref[rows, :] = alpha
        acc_ref[...] = alpha_ref[...] * acc_ref[...] + jnp.dot(
            p_ref[...], kv, preferred_element_type=f32)

    attn_tile(0, True)

    def attn_body(j, carry):
        attn_tile(j, False)
        return carry

    lax.fori_loop(1, nt, attn_body, 0)

    inv_l = 1.0 / l_ref[...]
    for h in range(A_HEADS):
        rows = slice(h * Q, (h + 1) * Q)
        o_lat = (acc_ref[rows, :] * inv_l[rows, :]).astype(bf16)
        o_ref[0, :, h * A_HEAD_DIM:(h + 1) * A_HEAD_DIM] = jnp.dot(
            o_lat, wuv_ref[h], preferred_element_type=f32)


def dsa_pallas(cq, ckv, kw, g_cq, g_ckv, g_kidx, w_uq, w_iq, w_uk, w_uv, rel_bias):
    bsz, seq, _ = cq.shape
    f32, bf16 = jnp.float32, jnp.bfloat16
    Q, T, H = Q_BLOCK, KEY_TILE, A_HEADS
    nblk = seq // Q
    topk = min(IDX_TOPK_MAX, seq // 4)
    wiq2 = w_iq.reshape(A_Q_LORA, IDX_HEADS * IDX_DIM).astype(bf16)
    wuq2 = w_uq.reshape(A_Q_LORA, H * A_HEAD_DIM).astype(bf16)
    wukt = jnp.transpose(w_uk, (1, 2, 0)).astype(bf16)
    wuv = jnp.transpose(w_uv, (1, 0, 2)).astype(bf16)
    tok = lambda w: pl.BlockSpec((1, Q, w), lambda b, t: (b, t, 0))
    full = lambda shape: pl.BlockSpec(shape, lambda b, t: (0,) * len(shape))
    a_mat, qlat, ckvn, kidxn, wsc = pl.pallas_call(
        _dsa_prep_kernel,
        grid=(bsz, nblk),
        in_specs=[tok(A_Q_LORA), tok(A_KV_LORA), tok(128),
                  full((1, A_Q_LORA)), full((1, A_KV_LORA)), full((1, IDX_DIM)),
                  full(wiq2.shape), full(wuq2.shape), full(wukt.shape)],
        out_specs=[pl.BlockSpec((1, 1, H * Q, IDX_DIM), lambda b, t: (b, t, 0, 0)),
                   pl.BlockSpec((1, 1, H * Q, A_KV_LORA), lambda b, t: (b, t, 0, 0)),
                   tok(A_KV_LORA), tok(IDX_DIM),
                   pl.BlockSpec((1, 1, Q, IDX_HEADS), lambda b, t: (b, t, 0, 0))],
        out_shape=[jax.ShapeDtypeStruct((bsz, nblk, H * Q, IDX_DIM), bf16),
                   jax.ShapeDtypeStruct((bsz, nblk, H * Q, A_KV_LORA), bf16),
                   jax.ShapeDtypeStruct((bsz, seq, A_KV_LORA), bf16),
                   jax.ShapeDtypeStruct((bsz, seq, IDX_DIM), bf16),
                   jax.ShapeDtypeStruct((bsz, nblk, Q, IDX_HEADS), f32)],
        name="dsa_prep",
    )(cq, ckv, kw, g_cq.reshape(1, -1), g_ckv.reshape(1, -1), g_kidx.reshape(1, -1),
      wiq2, wuq2, wukt)
    ckvp = jnp.pad(ckvn, ((0, 0), (KEY_PAD, 0), (0, 0)))
    kidxp = jnp.pad(kidxn, ((0, 0), (KEY_PAD, 0), (0, 0)))
    rel = (jnp.arange(T, dtype=jnp.int32)[None, :] - jnp.arange(Q, dtype=jnp.int32)[:, None]
           - KEY_PAD)
    far = rel_bias[t5_bucket(jnp.int32(-KEY_PAD - 1))]
    bias0 = jnp.transpose(rel_bias[t5_bucket(rel)] - far, (2, 0, 1)).astype(f32)
    nt_max = (nblk - 1 + T // Q) // (T // Q)
    skey = seq + KEY_PAD
    out = pl.pallas_call(
        partial(_dsa_main_kernel, topk=topk),
        grid=(bsz, nblk),
        in_specs=[pl.BlockSpec((1, 1, H * Q, IDX_DIM), lambda b, t: (b, t, 0, 0)),
                  pl.BlockSpec((1, 1, H * Q, A_KV_LORA), lambda b, t: (b, t, 0, 0)),
                  pl.BlockSpec((1, 1, Q, IDX_HEADS), lambda b, t: (b, t, 0, 0)),
                  pl.BlockSpec((1, skey, IDX_DIM), lambda b, t: (b, 0, 0)),
                  pl.BlockSpec((1, skey, A_KV_LORA), lambda b, t: (b, 0, 0)),
                  full(bias0.shape), full(wuv.shape)],
        out_specs=tok(H * A_HEAD_DIM),
        out_shape=jax.ShapeDtypeStruct((bsz, seq, H * A_HEAD_DIM), f32),
        scratch_shapes=[pltpu.VMEM((nt_max, Q, T), jnp.int32),
                        pltpu.VMEM((H * Q, T), f32),
                        pltpu.VMEM((H * Q, T), bf16),
                        pltpu.VMEM((H * Q, A_KV_LORA), f32),
                        pltpu.VMEM((H * Q, 1), f32),
                        pltpu.VMEM((H * Q, 1), f32),
                        pltpu.VMEM((H * Q, 1), f32),
                        pltpu.VMEM((IDX_HEADS, Q, 128), f32)],
        compiler_params=pltpu.CompilerParams(
            dimension_semantics=("arbitrary", "arbitrary"), vmem_limit_bytes=VMEM_LIMIT),
        name="dsa_main",
    )(a_mat, qlat, wsc, kidxp, ckvp, bias0, wuv)
    return out


def _dsa_prep_t_kernel(cq_ref, ckv_ref, kw_ref, gcq_ref, gckv_ref, gk_ref,
                       wiqt_ref, wuqt_ref, wuk_ref,
                       at_ref, qlatt_ref, ckvn_ref, kidxn_ref):
    f32, bf16 = jnp.float32, jnp.bfloat16
    Q = Q_BLOCK
    kw = kw_ref[0]
    cqn = _rms(cq_ref[0], gcq_ref[...]).astype(bf16)
    qit = _nt_dot(wiqt_ref[...], cqn)
    qt = _nt_dot(wuqt_ref[...], cqn)
    for h in range(A_HEADS):
        cols = slice(h * Q, (h + 1) * Q)
        at_ref[0, 0, :, cols] = qit[h * IDX_DIM:(h + 1) * IDX_DIM, :].astype(bf16)
        qh = qt[h * A_HEAD_DIM:(h + 1) * A_HEAD_DIM, :].astype(bf16)
        ql = jnp.dot(wuk_ref[h], qh, preferred_element_type=f32) * ATTN_SCALE
        qlatt_ref[0, 0, :, cols] = ql.astype(bf16)
    ckvn_ref[0] = _rms(ckv_ref[0], gckv_ref[...]).astype(bf16)
    kidxn_ref[0] = _rms(kw[:, :IDX_DIM], gk_ref[...]).astype(bf16)


def _dsa_main_t_kernel(at_ref, qlatt_ref, wt_ref, kidx_ref, ckv_ref, bias0_ref, wuvt_ref, o_ref,
                       sc_ref, big_ref, p_ref, acc_ref, m_ref, l_ref, *, topk):
    f32, i32, bf16 = jnp.float32, jnp.int32, jnp.bfloat16
    Q, T = Q_BLOCK, KEY_TILE
    i = pl.program_id(1)
    e = (i + 1) * Q
    nt = (i + T // Q) // (T // Q)
    kf = jnp.float32(topk)
    wsc = wt_ref[0] * IDX_SCALE
    sub = lax.broadcasted_iota(i32, (T, Q), 0)
    qlane = lax.broadcasted_iota(i32, (T, Q), 1)
    limit = ((i * Q + qlane) // CHUNK + 1) * CHUNK
    at = at_ref[0, 0]

    def tile_start(j):
        return pl.multiple_of(e + KEY_PAD - (j + 1) * T, 128)

    def key_pos(j):
        return tile_start(j) - KEY_PAD + sub

    def score_tile(j, carry):
        kid = kidx_ref[0, pl.ds(tile_start(j), T), :]
        big_ref[...] = jnp.dot(kid, at, preferred_element_type=f32)
        s = jnp.zeros((T, Q), f32)
        for h in range(IDX_HEADS):
            s = s + wsc[h:h + 1, :] * jnp.maximum(big_ref[:, h * Q:(h + 1) * Q], 0.0)
        bits = lax.bitcast_convert_type(s, i32)
        key = jnp.where(bits < 0, bits ^ jnp.int32(0x7FFFFFFF), bits)
        key = jnp.where(s == 0.0, 0, key)
        kpos = key_pos(j)
        key = jnp.where(kpos >= 0, key, INT_MIN)
        sc_ref[j] = jnp.where(kpos < limit, key, INT_MIN)
        return carry

    lax.fori_loop(0, nt, score_tile, 0)

    def count_where(pred):
        def body(j, cnt):
            hit = jnp.where(pred(sc_ref[j], key_pos(j)), 1.0, 0.0)
            return cnt + jnp.sum(hit.reshape(T // 64, 64, Q), axis=0)
        cnt = lax.fori_loop(0, nt, body, jnp.zeros((64, Q), f32))
        return jnp.sum(cnt, axis=0, keepdims=True)

    def bit_body(b, carry):
        u, cacc = carry
        cand = u | lax.shift_left(jnp.int32(1), 31 - b)
        tvec = cand ^ jnp.int32(INT_MIN)
        tot = count_where(lambda k, kpos: k >= tvec)
        ok = tot >= kf
        return jnp.where(ok, cand, u), jnp.where(ok, tot, cacc)

    u, cacc = lax.fori_loop(0, 32, bit_body, (jnp.zeros((1, Q), i32), jnp.zeros((1, Q), f32)))
    thr = jnp.maximum(u ^ jnp.int32(INT_MIN), INT_MIN + 1)
    overflow = jnp.where(u != 0, cacc, 0.0) > kf
    n_over = jnp.max(jnp.where(overflow, 1.0, 0.0), axis=1, keepdims=True)[0, 0]

    @pl.when(n_over > 0.0)
    def _():
        need = kf - count_where(lambda k, kpos: k > thr)

        def cut_body(b, cut):
            cand = cut | lax.shift_left(jnp.int32(1), 14 - b)
            cnt = count_where(lambda k, kpos: jnp.where(k == thr, kpos, cand) < cand)
            return jnp.where(cnt <= need, cand, cut)

        cut = lax.fori_loop(0, 15, cut_body, jnp.zeros((1, Q), i32))

        def drop_tile(j, carry):
            k = sc_ref[j]
            drop = jnp.where(k == thr, key_pos(j), -1) >= cut
            sc_ref[j] = jnp.where(drop, INT_MIN, k)
            return carry

        lax.fori_loop(0, nt, drop_tile, 0)

    m_ref[...] = jnp.full(m_ref.shape, NEG_BIG, f32)
    l_ref[...] = jnp.zeros(l_ref.shape, f32)
    acc_ref[...] = jnp.zeros(acc_ref.shape, f32)
    qlatt = qlatt_ref[0, 0]

    def attn_tile(j, with_bias):
        kv = ckv_ref[0, pl.ds(tile_start(j), T), :]
        big_ref[...] = jnp.dot(kv, qlatt, preferred_element_type=f32)
        mask_add = jnp.where(sc_ref[j] >= thr, 0.0, NEG_BIG)
        alphas = []
        for h in range(A_HEADS):
            cols = slice(h * Q, (h + 1) * Q)
            x = big_ref[:, cols] + mask_add
            if with_bias:
                x = x + bias0_ref[h]
            m_prev = m_ref[:, cols]
            m_new = jnp.maximum(m_prev, jnp.max(x, axis=0, keepdims=True))
            p = jnp.exp(x - m_new)
            alpha = jnp.exp(m_prev - m_new)
            l_ref[:, cols] = alpha * l_ref[:, cols] + jnp.sum(p, axis=0, keepdims=True)
            m_ref[:, cols] = m_new
            p_ref[:, cols] = p.astype(bf16)
            alphas.append(alpha)
        pv = lax.dot_general(kv, p_ref[...], (((0,), (0,)), ((), ())),
                             preferred_element_type=f32)
        acc_ref[...] = jnp.concatenate(alphas, axis=1) * acc_ref[...] + pv

    attn_tile(0, True)

    def attn_body(j, carry):
        attn_tile(j, False)
        return carry

    lax.fori_loop(1, nt, attn_body, 0)

    inv_l = 1.0 / l_ref[...]
    outs = []
    for h in range(A_HEADS):
        cols = slice(h * Q, (h + 1) * Q)
        o_lat_t = (acc_ref[:, cols] * inv_l[:, cols]).astype(bf16)
        outs.append(jnp.dot(wuvt_ref[h], o_lat_t, preferred_element_type=f32))
    o_ref[0] = jnp.concatenate(outs, axis=0).T


def dsa_pallas_t(cq, ckv, kw, widx_t, g_cq, g_ckv, g_kidx, w_uq, w_iq, w_uk, w_uv, rel_bias):
    bsz, seq, _ = cq.shape
    f32, bf16 = jnp.float32, jnp.bfloat16
    Q, T, H = Q_BLOCK, KEY_TILE, A_HEADS
    nblk = seq // Q
    topk = min(IDX_TOPK_MAX, seq // 4)
    wiqt = w_iq.reshape(A_Q_LORA, IDX_HEADS * IDX_DIM).T.astype(bf16)
    wuqt = w_uq.reshape(A_Q_LORA, H * A_HEAD_DIM).T.astype(bf16)
    wuk = jnp.transpose(w_uk, (1, 0, 2)).astype(bf16)
    wuvt = jnp.transpose(w_uv, (1, 2, 0)).astype(bf16)
    tok = lambda w: pl.BlockSpec((1, Q, w), lambda b, t: (b, t, 0))
    full = lambda shape: pl.BlockSpec(shape, lambda b, t: (0,) * len(shape))
    blk = lambda r: pl.BlockSpec((1, 1, r, H * Q), lambda b, t: (b, t, 0, 0))
    a_t, qlat_t, ckvn, kidxn = pl.pallas_call(
        _dsa_prep_t_kernel,
        grid=(bsz, nblk),
        in_specs=[tok(A_Q_LORA), tok(A_KV_LORA), tok(128),
                  full((1, A_Q_LORA)), full((1, A_KV_LORA)), full((1, IDX_DIM)),
                  full(wiqt.shape), full(wuqt.shape), full(wuk.shape)],
        out_specs=[blk(IDX_DIM), blk(A_KV_LORA), tok(A_KV_LORA), tok(IDX_DIM)],
        out_shape=[jax.ShapeDtypeStruct((bsz, nblk, IDX_DIM, H * Q), bf16),
                   jax.ShapeDtypeStruct((bsz, nblk, A_KV_LORA, H * Q), bf16),
                   jax.ShapeDtypeStruct((bsz, seq, A_KV_LORA), bf16),
                   jax.ShapeDtypeStruct((bsz, seq, IDX_DIM), bf16)],
        name="dsa_prep",
    )(cq, ckv, kw, g_cq.reshape(1, -1), g_ckv.reshape(1, -1), g_kidx.reshape(1, -1),
      wiqt, wuqt, wuk)
    ckvp = jnp.pad(ckvn, ((0, 0), (KEY_PAD, 0), (0, 0)))
    kidxp = jnp.pad(kidxn, ((0, 0), (KEY_PAD, 0), (0, 0)))
    rel = (jnp.arange(T, dtype=jnp.int32)[:, None] - jnp.arange(Q, dtype=jnp.int32)[None, :]
           - KEY_PAD)
    far = rel_bias[t5_bucket(jnp.int32(-KEY_PAD - 1))]
    bias0 = jnp.transpose(rel_bias[t5_bucket(rel)] - far, (2, 0, 1)).astype(f32)
    nt_max = (nblk - 1 + T // Q) // (T // Q)
    skey = seq + KEY_PAD
    return pl.pallas_call(
        partial(_dsa_main_t_kernel, topk=topk),
        grid=(bsz, nblk),
        in_specs=[blk(IDX_DIM), blk(A_KV_LORA),
                  pl.BlockSpec((1, IDX_HEADS, Q), lambda b, t: (b, 1, t)),
                  pl.BlockSpec((1, skey, IDX_DIM), lambda b, t: (b, 0, 0)),
                  pl.BlockSpec((1, skey, A_KV_LORA), lambda b, t: (b, 0, 0)),
                  full(bias0.shape), full(wuvt.shape)],
        out_specs=tok(H * A_HEAD_DIM),
        out_shape=jax.ShapeDtypeStruct((bsz, seq, H * A_HEAD_DIM), f32),
        scratch_shapes=[pltpu.VMEM((nt_max, T, Q), jnp.int32),
                        pltpu.VMEM((T, H * Q), f32),
                        pltpu.VMEM((T, H * Q), bf16),
                        pltpu.VMEM((A_KV_LORA, H * Q), f32),
                        pltpu.VMEM((1, H * Q), f32),
                        pltpu.VMEM((1, H * Q), f32)],
        compiler_params=pltpu.CompilerParams(
            dimension_semantics=("arbitrary", "arbitrary"), vmem_limit_bytes=VMEM_LIMIT),
        name="dsa_main",
    )(a_t, qlat_t, widx_t, kidxp, ckvp, bias0, wuvt)


def causal_dwconv(x, w):
    ch = x.shape[-1]
    return lax.conv_general_dilated(x, w[:, None, :].astype(x.dtype), window_strides=(1,),
                                    padding=[(CONV_WIDTH - 1, 0)],
                                    dimension_numbers=('NWC', 'WIO', 'NWC'),
                                    feature_group_count=ch)


def gated_deltanet(q, k, v, z, a, b, conv_w, A_log, dt_bias, g_onorm):
    bsz, seq, _ = q.shape
    f32 = jnp.float32
    H, Dh = B_HEADS, B_HEAD_DIM
    nc = seq // CHUNK
    qkv = jax.nn.silu(causal_dwconv(jnp.concatenate([q, k, v], axis=-1), conv_w)).astype(f32)
    q, k, v = [t.reshape(bsz, seq, H, Dh) for t in jnp.split(qkv, 3, axis=-1)]
    q = l2norm(q) * (Dh ** -0.5)
    k = l2norm(k)
    beta = jax.nn.sigmoid(b.astype(f32))
    g = -jnp.exp(A_log.astype(f32)) * jax.nn.softplus(a.astype(f32) + dt_bias.astype(f32))

    def to_chunks(t):
        t = t.reshape(bsz, nc, CHUNK, *t.shape[2:])
        return jnp.moveaxis(t, 3, 1)

    q, k, v, beta = to_chunks(q), to_chunks(k), to_chunks(v), to_chunks(beta)
    g = jnp.cumsum(to_chunks(g), axis=-1)
    pos = jnp.arange(CHUNK)
    causal = pos[:, None] >= pos[None, :]
    strict = pos[:, None] > pos[None, :]
    decay = jnp.exp(jnp.where(causal, g[..., :, None] - g[..., None, :], -jnp.inf))
    m = jnp.where(strict, beta[..., :, None] * jnp.einsum('bhnid,bhnjd->bhnij', k, k) * decay, 0.0)
    rhs = jnp.concatenate([v * beta[..., None], k * (beta * jnp.exp(g))[..., None]], axis=-1)
    sol = lax.linalg.triangular_solve(m, rhs, left_side=True, lower=True, unit_diagonal=True)
    u, w = sol[..., :Dh], sol[..., Dh:]
    intra = jnp.einsum('bhnid,bhnjd->bhnij', q, k) * decay
    q_dec = q * jnp.exp(g)[..., None]
    k_tail = k * jnp.exp(g[..., -1:] - g)[..., None]
    chunk_decay = jnp.exp(g[..., -1])

    def step(state, inp):
        q_c, u_c, w_c, a_c, kt_c, d_c = inp
        v_new = u_c - jnp.einsum('bhcd,bhde->bhce', w_c, state)
        o_c = jnp.einsum('bhcd,bhde->bhce', q_c, state) + jnp.einsum('bhij,bhje->bhie', a_c, v_new)
        state = state * d_c[..., None, None] + jnp.einsum('bhcd,bhce->bhde', kt_c, v_new)
        return state, o_c

    xs = tuple(jnp.moveaxis(t, 2, 0) for t in (q_dec, u, w, intra, k_tail, chunk_decay))
    _, o = lax.scan(step, jnp.zeros((bsz, H, Dh, Dh), f32), xs)
    o = jnp.transpose(o, (1, 0, 3, 2, 4)).reshape(bsz, seq, H, Dh)
    o = o * lax.rsqrt(jnp.mean(o * o, axis=-1, keepdims=True) + EPS) * g_onorm.astype(f32)
    o = o * jax.nn.silu(z.astype(f32).reshape(bsz, seq, H, Dh))
    return o.reshape(bsz, seq, H * Dh).astype(z.dtype)


def memory_cross_attention(h, mem_n, wq, wk, wv, wo):
    q = jnp.einsum('bsd,dhe->bhse', h, wq)
    k = jnp.einsum('bmd,dhe->bhme', mem_n, wk)
    v = jnp.einsum('bmd,dhe->bhme', mem_n, wv)
    logits = jnp.einsum('bhse,bhme->bhsm', q, k).astype(jnp.float32) * (X_HEAD_DIM ** -0.5)
    p = jax.nn.softmax(logits, axis=-1).astype(v.dtype)
    o = jnp.einsum('bhsm,bhme->bshe', p, v)
    return jnp.einsum('bshe,hed->bsd', o, wo)


def peer_ffn(h, w_pq, sub_keys, u_emb, v_emb):
    bsz, seq, d = h.shape
    n_tok = bsz * seq
    hb = h.reshape(n_tok // P_TOKEN_BLOCK, P_TOKEN_BLOCK, d)

    def one_block(xb):
        n = xb.shape[0]
        qr = jnp.einsum('nd,dhe->nhe', xb, w_pq).reshape(n, P_HEADS, 2, P_QDIM // 2)
        s = jnp.einsum('nhpe,hpke->nhpk', qr, sub_keys).astype(jnp.float32)
        s1, i1 = lax.top_k(s[:, :, 0], P_TOPK)
        s2, i2 = lax.top_k(s[:, :, 1], P_TOPK)
        cand = (s1[..., :, None] + s2[..., None, :]).reshape(n, P_HEADS, P_TOPK * P_TOPK)
        cidx = (i1[..., :, None] * N_KEYS + i2[..., None, :]).reshape(n, P_HEADS, P_TOPK * P_TOPK)
        best, pos = lax.top_k(cand, P_TOPK)
        eidx = jnp.take_along_axis(cidx, pos, axis=-1)
        gate = jax.nn.softmax(best, axis=-1).astype(xb.dtype)
        act = jax.nn.gelu(jnp.einsum('nhkd,nd->nhk', u_emb[eidx], xb), approximate=False)
        return jnp.einsum('nhk,nhkd->nd', gate * act, v_emb[eidx])

    return lax.map(one_block, hb).reshape(bsz, seq, d)


PEER_SCORE_TOKENS = 256
PEER_GATHER_TOKENS = 64
PEER_SLOTS = P_HEADS * P_TOPK
ROW_VREG = (8, 128)
WORDS_PER_ROW = 4
PEER_TILES = 8


def _top16(s, order=None, payload=None):
    if order is None:
        order = lax.broadcasted_iota(jnp.int32, s.shape, 0).astype(jnp.float32)
    vals, picks = [], []
    for _ in range(P_TOPK):
        m = jnp.max(s, axis=0, keepdims=True)
        pos = jnp.min(jnp.where(s == m, order, float(N_EXPERTS)), axis=0, keepdims=True)
        hit = order == pos
        vals.append(m)
        if payload is None:
            picks.append(pos)
        else:
            picks.append(jnp.max(jnp.where(hit, payload, -1.0), axis=0, keepdims=True))
        s = jnp.where(hit, -jnp.inf, s)
    return vals, picks


def _staircase_candidates(v1, i1, v2, i2):
    v1m, i1m = jnp.concatenate(v1, axis=0), jnp.concatenate(i1, axis=0)
    v2m, i2m = jnp.concatenate(v2, axis=0), jnp.concatenate(i2, axis=0)
    t = v1m.shape[1]
    sub8 = lax.broadcasted_iota(jnp.int32, (8, t), 0).astype(jnp.float32)
    cand, cidx, rank = [], [], []
    for a in range(8):
        cand.append(v1[a] + v2m[:8])
        cidx.append(i1[a] * float(N_KEYS) + i2m[:8])
        rank.append(sub8 + float(a * P_TOPK))
    cand.append(v1[0] + v2m[8:])
    cidx.append(i1[0] * float(N_KEYS) + i2m[8:])
    rank.append(sub8 + 8.0)
    cand.append(v1m[8:] + v2[0])
    cidx.append(i1m[8:] * float(N_KEYS) + i2[0])
    rank.append((sub8 + 8.0) * float(P_TOPK))
    cat = lambda xs: jnp.concatenate(xs, axis=0)
    return cat(cand), cat(rank), cat(cidx)


def _peer_score_kernel(x_ref, g_ref, wpqt_ref, sk_ref, hn_ref, eidx_ref, gate_ref):
    f32, bf16 = jnp.float32, jnp.bfloat16
    hb = _rms(x_ref[...], g_ref[...]).astype(bf16)
    hn_ref[...] = hb
    qrt = _nt_dot(wpqt_ref[...], hb)
    half = P_QDIM // 2
    e_rows, g_rows = [], []
    for hd in range(P_HEADS):
        tops = []
        for p in range(2):
            qhp = qrt[(hd * 2 + p) * half:(hd * 2 + p + 1) * half, :].astype(bf16)
            s = jnp.dot(sk_ref[hd * 2 + p], qhp, preferred_element_type=f32)
            tops.append(_top16(s))
        (v1, i1), (v2, i2) = tops
        best, be = _top16(*_staircase_candidates(v1, i1, v2, i2))
        ex = [jnp.exp(b - best[0]) for b in best]
        den = ex[0]
        for k in range(1, P_TOPK):
            den = den + ex[k]
        inv = 1.0 / den
        e_rows += be
        g_rows += [x * inv for x in ex]
    eidx_ref[...] = (jnp.concatenate(e_rows, axis=0).T * float(WORDS_PER_ROW)).astype(jnp.int32)
    gate_ref[...] = jnp.concatenate(g_rows, axis=0).T


def _diag_mask():
    r = lax.broadcasted_iota(jnp.int32, (8, PEER_SLOTS * 8), 0)
    c = lax.broadcasted_iota(jnp.int32, (8, PEER_SLOTS * 8), 1)
    return (c & 7) == r


def _gather_pair(idx_ref, tab_ref, g_refs, t0):
    views = [idx_ref.at[t0 + u] for u in range(2)]
    for r in range(PEER_SLOTS):
        for view, g_ref in zip(views, g_refs):
            row0 = pl.multiple_of(view[r], WORDS_PER_ROW)
            g_ref[r * WORDS_PER_ROW:(r + 1) * WORDS_PER_ROW, :] = tab_ref[
                pl.ds(row0, WORDS_PER_ROW), :]


def _gather_pipeline(idx_ref, tab_ref, tiles, n_tokens, consume):
    pairs = len(tiles) // 2
    for g_ref in tiles[-2:]:
        g_ref[...] = jnp.zeros(g_ref.shape, g_ref.dtype)

    def trip(i, carry):
        for k in range(pairs):
            pair = pairs * i + k
            prev_tiles = tiles[2 * k - 2:2 * k] if k else tiles[-2:]
            _gather_pair(idx_ref, tab_ref, tiles[2 * k:2 * k + 2], pair * 2)
            for u in range(2):
                consume(prev_tiles[u], jnp.maximum(pair - 1, 0), u)
        return carry

    lax.fori_loop(0, n_tokens // (2 * pairs), trip, 0)
    for u in range(2):
        consume(tiles[-2 + u], n_tokens // 2 - 1, u)


def _peer_act_kernel(idx_ref, hn_ref, tab_ref, a_ref, m_ref, *tiles):
    f32, bf16 = jnp.float32, jnp.bfloat16
    tb = a_ref.shape[0]

    def dots(g_ref, tp, u):
        hp = hn_ref[pl.ds(pl.multiple_of(tp * 16, 16), 16), :]
        m = _nt_dot(hp, pltpu.bitcast(g_ref[...], bf16))
        m_ref[pl.ds(pl.multiple_of((tp * 2 + u) * 8, 8), 8), :] = m[u * 8:(u + 1) * 8, :]

    _gather_pipeline(idx_ref, tab_ref, tiles, tb, dots)
    m3 = m_ref[...].reshape(tb, 8, PEER_SLOTS * 8)
    z = jnp.sum(jnp.where(_diag_mask()[None], m3, 0.0), axis=1)
    rr = lax.broadcasted_iota(jnp.int32, (PEER_SLOTS * 8, PEER_SLOTS), 0)
    cc = lax.broadcasted_iota(jnp.int32, (PEER_SLOTS * 8, PEER_SLOTS), 1)
    pool = jnp.where((rr >> 3) == cc, 1.0, 0.0).astype(bf16)
    z_hi = z.astype(bf16)
    z_lo = (z - z_hi.astype(f32)).astype(bf16)
    a_ref[...] = (jnp.dot(z_hi, pool, preferred_element_type=f32)
                  + jnp.dot(z_lo, pool, preferred_element_type=f32))


def _peer_out_kernel(idx_ref, a_ref, gate_ref, tab_ref, o_ref, w_ref, *tiles):
    f32, bf16 = jnp.float32, jnp.bfloat16
    tb = a_ref.shape[0]
    a = a_ref[...]
    act = 0.5 * a * (1.0 + lax.erf(a * (2.0 ** -0.5)))
    wgt = (gate_ref[...] * act).astype(bf16)
    rr = lax.broadcasted_iota(jnp.int32, (PEER_SLOTS, PEER_SLOTS * 8), 0)
    cc = lax.broadcasted_iota(jnp.int32, (PEER_SLOTS, PEER_SLOTS * 8), 1)
    expand = jnp.where((cc >> 3) == rr, 1.0, 0.0).astype(bf16)
    w_ref[...] = jnp.dot(wgt, expand, preferred_element_type=f32)
    diag = _diag_mask()

    def combine(g_ref, tp, u):
        t = tp * 2 + u
        wrow = jnp.broadcast_to(w_ref[pl.ds(t, 1), :], (8, PEER_SLOTS * 8))
        wsel = jnp.where(diag, wrow, 0.0).astype(bf16)
        o_ref[pl.ds(pl.multiple_of(t * 8, 8), 8), :] = jnp.dot(
            wsel, pltpu.bitcast(g_ref[...], bf16), preferred_element_type=f32)

    _gather_pipeline(idx_ref, tab_ref, tiles, tb, combine)


def _pack_table(tab):
    n_e = tab.shape[0]
    tb = lax.bitcast_convert_type(
        tab.astype(jnp.bfloat16).reshape(n_e, WORDS_PER_ROW, 2, 128), jnp.uint16).astype(jnp.uint32)
    word = tb[:, :, 0, :] | (tb[:, :, 1, :] << 16)
    return lax.bitcast_convert_type(word, jnp.int32).reshape(n_e * WORDS_PER_ROW, 128)


def peer_pallas(x, g_ffn, w_pq, sub_keys, u_emb, v_emb):
    n_tok, d = x.shape
    f32, bf16 = jnp.float32, jnp.bfloat16
    ts, tg = PEER_SCORE_TOKENS, PEER_GATHER_TOKENS
    wpqt = w_pq.reshape(d, P_HEADS * P_QDIM).T.astype(bf16)
    sk = sub_keys.reshape(P_HEADS * 2, N_KEYS, P_QDIM // 2).astype(bf16)
    hn, eidx, gate = pl.pallas_call(
        _peer_score_kernel,
        grid=(n_tok // ts,),
        in_specs=[pl.BlockSpec((ts, d), lambda i: (i, 0)),
                  pl.BlockSpec((1, d), lambda i: (0, 0)),
                  pl.BlockSpec(wpqt.shape, lambda i: (0, 0)),
                  pl.BlockSpec(sk.shape, lambda i: (0, 0, 0))],
        out_specs=[pl.BlockSpec((ts, d), lambda i: (i, 0)),
                   pl.BlockSpec((ts, PEER_SLOTS), lambda i: (i, 0)),
                   pl.BlockSpec((ts, PEER_SLOTS), lambda i: (i, 0))],
        out_shape=[jax.ShapeDtypeStruct((n_tok, d), bf16),
                   jax.ShapeDtypeStruct((n_tok, PEER_SLOTS), jnp.int32),
                   jax.ShapeDtypeStruct((n_tok, PEER_SLOTS), f32)],
        compiler_params=pltpu.CompilerParams(vmem_limit_bytes=VMEM_LIMIT),
        name="peer_score",
    )(x, g_ffn.reshape(1, d), wpqt, sk)
    utab, vtab = _pack_table(u_emb), _pack_table(v_emb)
    idx_spec = pl.BlockSpec((tg, PEER_SLOTS), lambda i: (i, 0), memory_space=pltpu.SMEM)
    tab_spec = pl.BlockSpec(memory_space=pltpu.VMEM)
    slot_spec = pl.BlockSpec((tg, PEER_SLOTS), lambda i: (i, 0))
    row_spec = pl.BlockSpec((tg * 8, 128), lambda i: (i, 0))
    gbuf = pltpu.VMEM((PEER_SLOTS * WORDS_PER_ROW, 128), jnp.int32)
    cparams = pltpu.CompilerParams(vmem_limit_bytes=VMEM_LIMIT)
    act = pl.pallas_call(
        _peer_act_kernel,
        grid=(n_tok // tg,),
        in_specs=[idx_spec, row_spec, tab_spec],
        out_specs=slot_spec,
        out_shape=jax.ShapeDtypeStruct((n_tok, PEER_SLOTS), f32),
        scratch_shapes=[pltpu.VMEM((tg * 8, PEER_SLOTS * 8), f32)] + [gbuf] * PEER_TILES,
        compiler_params=cparams,
        name="peer_act",
    )(eidx, hn.reshape(n_tok * 8, 128), utab)
    out = pl.pallas_call(
        _peer_out_kernel,
        grid=(n_tok // tg,),
        in_specs=[idx_spec, slot_spec, slot_spec, tab_spec],
        out_specs=row_spec,
        out_shape=jax.ShapeDtypeStruct((n_tok * 8, 128), f32),
        scratch_shapes=[pltpu.VMEM((tg, PEER_SLOTS * 8), f32)] + [gbuf] * PEER_TILES,
        compiler_params=cparams,
        name="peer_out",
    )(eidx, act, gate, vtab)
    return out.reshape(n_tok, d)


def _final_kernel(x_ref, y_ref, g_ref, o_ref):
    o_ref[...] = _rms(x_ref[...] + y_ref[...], g_ref[...])


def final_residual_rmsnorm(x, y, g):
    n, d = x.shape
    tm = 512
    row = pl.BlockSpec((tm, d), lambda i: (i, 0))
    return pl.pallas_call(
        _final_kernel,
        grid=(n // tm,),
        in_specs=[row, row, pl.BlockSpec((1, d), lambda i: (0, 0))],
        out_specs=row,
        out_shape=jax.ShapeDtypeStruct((n, d), x.dtype),
        name="final_rmsnorm",
    )(x, y, g.reshape(1, d))


IN_PROJ_TOKENS = 512
GDN_COLS = 3 * B_QK
GATE_ROWS = 8


def _in_proj_kernel(x_ref, g_ref, wa_ref, wkw_ref, wqkv_ref, wz_ref, wab_ref, wabt_ref,
                    cq_ref, ckv_ref, kw_ref, qkv_ref, z_ref, ab_ref, abt_ref):
    f32 = jnp.float32
    hb = _rms(x_ref[0], g_ref[...]).astype(jnp.bfloat16)
    a = jnp.dot(hb, wa_ref[...], preferred_element_type=f32)
    cq_ref[0] = a[:, :A_Q_LORA]
    ckv_ref[0] = a[:, A_Q_LORA:]
    kw_ref[0] = jnp.dot(hb, wkw_ref[...], preferred_element_type=f32)
    qkv_ref[0] = jnp.dot(hb, wqkv_ref[...], preferred_element_type=f32)
    z_ref[0] = jnp.dot(hb, wz_ref[...], preferred_element_type=f32)
    ab_ref[0] = jnp.dot(hb, wab_ref[...], preferred_element_type=f32)
    abt_ref[0] = _nt_dot(wabt_ref[...], hb)


def in_proj_pallas(x, g_mix, w_in):
    bsz, seq, d = x.shape
    f32, bf16 = jnp.float32, jnp.bfloat16
    tm = min(IN_PROJ_TOKENS, seq)
    o = np.cumsum((0,) + COL_WIDTHS)
    wb = w_in.astype(bf16)
    pad_cols = lambda w: jnp.pad(w, ((0, 0), (0, 128 - w.shape[1])))
    wa = wb[:, o[0]:o[2]]
    wkw = pad_cols(wb[:, o[2]:o[4]])
    wqkv = wb[:, o[4]:o[7]]
    wz = wb[:, o[7]:o[8]]
    wab = pad_cols(wb[:, o[8]:o[10]])
    wabt = jnp.concatenate([wb[:, o[8]:o[10]], wb[:, o[3]:o[4]]], axis=1).T
    full = lambda w: pl.BlockSpec(w.shape, lambda b, t: (0, 0))
    tok = lambda w: pl.BlockSpec((1, tm, w), lambda b, t: (b, t, 0))
    shp = lambda w: jax.ShapeDtypeStruct((bsz, seq, w), f32)
    return pl.pallas_call(
        _in_proj_kernel,
        grid=(bsz, seq // tm),
        in_specs=[tok(d), pl.BlockSpec((1, d), lambda b, t: (0, 0)),
                  full(wa), full(wkw), full(wqkv), full(wz), full(wab), full(wabt)],
        out_specs=[tok(A_Q_LORA), tok(A_KV_LORA), tok(128), tok(GDN_COLS), tok(B_QK), tok(128),
                   pl.BlockSpec((1, GATE_ROWS + IDX_HEADS, tm), lambda b, t: (b, 0, t))],
        out_shape=[shp(A_Q_LORA), shp(A_KV_LORA), shp(128), shp(GDN_COLS), shp(B_QK), shp(128),
                   jax.ShapeDtypeStruct((bsz, GATE_ROWS + IDX_HEADS, seq), f32)],
        compiler_params=pltpu.CompilerParams(vmem_limit_bytes=VMEM_LIMIT),
        name="in_proj",
    )(x, g_mix.reshape(1, d), wa, wkw, wqkv, wz, wab, wabt)


def _softplus(x):
    return jnp.maximum(x, 0.0) + jnp.log1p(jnp.exp(-jnp.abs(x)))


def _sigmoid(x):
    return 1.0 / (1.0 + jnp.exp(-x))


def _gdn_gates(pre, a_log, dt_bias, is_decay):
    g = -jnp.exp(a_log) * _softplus(pre + dt_bias)
    return jnp.where(is_decay, g, _sigmoid(pre))


def _gdn_prep_kernel(qkv_ref, halo_ref, cw_ref, ab_ref, abt_ref, alc_ref, dtc_ref, alr_ref, dtr_ref,
                     q_ref, k_ref, v_ref, gc_ref, gr_ref):
    tm = qkv_ref.shape[1]
    x = qkv_ref[0]
    halo = jnp.where(pl.program_id(1) > 0, halo_ref[0], 0.0)
    full = jnp.concatenate([halo, x], axis=0)
    y = x * cw_ref[CONV_WIDTH - 1:CONV_WIDTH, :]
    for back in range(1, CONV_WIDTH):
        shifted = pltpu.roll(full, back, axis=0)[8:, :]
        y = y + shifted * cw_ref[CONV_WIDTH - 1 - back:CONV_WIDTH - back, :]
    y = y * _sigmoid(y)
    for h in range(B_HEADS):
        cols = slice(h * B_HEAD_DIM, (h + 1) * B_HEAD_DIM)
        qh = y[:, h * B_HEAD_DIM:(h + 1) * B_HEAD_DIM]
        kh = y[:, B_QK + h * B_HEAD_DIM:B_QK + (h + 1) * B_HEAD_DIM]
        q_ref[0, :, cols] = qh * lax.rsqrt(
            jnp.sum(qh * qh, axis=-1, keepdims=True) + EPS) * (B_HEAD_DIM ** -0.5)
        k_ref[0, :, cols] = kh * lax.rsqrt(jnp.sum(kh * kh, axis=-1, keepdims=True) + EPS)
    v_ref[0] = y[:, 2 * B_QK:]
    lane = lax.broadcasted_iota(jnp.int32, (tm, 128), 1)
    gates_c = _gdn_gates(ab_ref[0], alc_ref[...], dtc_ref[...], lane < B_HEADS)
    row = lax.broadcasted_iota(jnp.int32, (GATE_ROWS, tm), 0)
    gates_r = _gdn_gates(abt_ref[0], alr_ref[...], dtr_ref[...], row < B_HEADS)
    ti = lax.broadcasted_iota(jnp.int32, (tm, tm), 0)
    tj = lax.broadcasted_iota(jnp.int32, (tm, tm), 1)
    same_chunk = (ti // CHUNK) == (tj // CHUNK)
    hi = lax.Precision.HIGHEST
    lower = jnp.where(same_chunk & (tj <= ti), 1.0, 0.0)
    upper = jnp.where(same_chunk & (ti <= tj), 1.0, 0.0)
    cum_c = jnp.dot(lower, gates_c, preferred_element_type=jnp.float32, precision=hi)
    cum_r = jnp.dot(gates_r, upper, preferred_element_type=jnp.float32, precision=hi)
    gc_ref[0] = jnp.where(lane < B_HEADS, cum_c, gates_c)
    gr_ref[0] = jnp.where(row < B_HEADS, cum_r, gates_r)


def _gdn_main_kernel(q_ref, k_ref, v_ref, z_ref, gc_ref, gr_ref, gon_ref, o_ref, s_ref):
    f32, bf16 = jnp.float32, jnp.bfloat16
    C, Dh = CHUNK, B_HEAD_DIM
    hi = lax.Precision.HIGHEST

    @pl.when(pl.program_id(1) == 0)
    def _():
        s_ref[...] = jnp.zeros(s_ref.shape, f32)

    ii = lax.broadcasted_iota(jnp.int32, (C, C), 0)
    jj = lax.broadcasted_iota(jnp.int32, (C, C), 1)
    causal = ii >= jj
    strict = ii > jj
    eye = jnp.where(ii == jj, 1.0, 0.0)
    mm = lambda a, b: jnp.dot(a, b, preferred_element_type=f32)
    mmh = lambda a, b: jnp.dot(a, b, preferred_element_type=f32, precision=hi)
    n_chunks = q_ref.shape[1] // C
    units = []
    for c in range(n_chunks):
        rows = slice(c * C, (c + 1) * C)
        gates_c = gc_ref[0, rows, :]
        gates_r = gr_ref[0, :, c * C:(c + 1) * C]
        for h in range(B_HEADS):
            cols = slice(h * Dh, (h + 1) * Dh)
            gcum = jnp.broadcast_to(gates_c[:, h:h + 1], (C, Dh))
            beta = jnp.broadcast_to(gates_c[:, B_HEADS + h:B_HEADS + h + 1], (C, Dh))
            gcum_r = jnp.broadcast_to(gates_r[h:h + 1, :], (C, C))
            decay = jnp.where(causal, jnp.exp(jnp.minimum(gcum[:, :C] - gcum_r, 0.0)), 0.0)
            q, k, v = q_ref[0, rows, cols], k_ref[0, rows, cols], v_ref[0, rows, cols]
            qb, kb = q.astype(bf16), k.astype(bf16)
            kk = _nt_dot(kb, kb)
            qk = _nt_dot(qb, kb)
            egc = jnp.exp(gcum)
            g_last = gcum[C - 1:C, :]
            units.append(dict(
                c=c, h=h, rows=rows, cols=cols,
                neg_m=jnp.where(strict, -(beta[:, :C] * kk * decay), 0.0),
                rhs=jnp.concatenate([v * beta, k * (beta * egc)], axis=1),
                q_dec=(q * egc).astype(bf16), intra=(qk * decay).astype(bf16),
                k_tail=(k * jnp.exp(g_last - gcum)).astype(bf16), chunk_decay=jnp.exp(g_last)))
    powers = [un["neg_m"] for un in units]
    t_inv = [eye + p for p in powers]
    for _ in range(5):
        powers = [mmh(p, p) for p in powers]
        t_inv = [t + mmh(p, t) for p, t in zip(powers, t_inv)]
    sols = [mmh(t, un["rhs"]) for t, un in zip(t_inv, units)]
    states = [s_ref[h] for h in range(B_HEADS)]
    for c in range(n_chunks):
        group = [(un, sol) for un, sol in zip(units, sols) if un["c"] == c]
        sbs = [states[un["h"]].astype(bf16) for un, _ in group]
        vbs = [(sol[:, :Dh] - mm(sol[:, Dh:].astype(bf16), sb)).astype(bf16)
               for (un, sol), sb in zip(group, sbs)]
        outs = [mm(un["q_dec"], sb) + mm(un["intra"], vb)
                for (un, _), sb, vb in zip(group, sbs, vbs)]
        for (un, _), vb in zip(group, vbs):
            states[un["h"]] = states[un["h"]] * un["chunk_decay"] + lax.dot_general(
                un["k_tail"], vb, (((0,), (0,)), ((), ())), preferred_element_type=f32)
        for (un, _), o in zip(group, outs):
            o = o * lax.rsqrt(jnp.mean(o * o, axis=-1, keepdims=True) + EPS) * gon_ref[...]
            zz = z_ref[0, un["rows"], un["cols"]]
            o_ref[0, un["rows"], un["cols"]] = o * (zz * _sigmoid(zz))
    for h in range(B_HEADS):
        s_ref[h] = states[h]


def gdn_pallas(qkv, z, ab, abt, conv_w, a_log, dt_bias, g_onorm):
    bsz, seq, _ = qkv.shape
    f32 = jnp.float32
    tm = min(256, seq)
    zero4 = jnp.zeros((B_HEADS,), f32)
    lane_row = lambda v: jnp.pad(jnp.concatenate([v.astype(f32), zero4]), (0, 120)).reshape(1, 128)
    sub_col = lambda v: jnp.concatenate([v.astype(f32), zero4]).reshape(GATE_ROWS, 1)
    tok = lambda w, t=tm: pl.BlockSpec((1, t, w), lambda b, i: (b, i, 0))
    const = lambda shape: pl.BlockSpec(shape, lambda b, i: (0,) * len(shape))
    shp = lambda w: jax.ShapeDtypeStruct((bsz, seq, w), f32)
    q, k, v, gc, gr = pl.pallas_call(
        _gdn_prep_kernel,
        grid=(bsz, seq // tm),
        in_specs=[tok(GDN_COLS),
                  pl.BlockSpec((1, 8, GDN_COLS), lambda b, i: (b, jnp.maximum(i * (tm // 8) - 1, 0), 0)),
                  const((CONV_WIDTH, GDN_COLS)), tok(128),
                  pl.BlockSpec((1, GATE_ROWS, tm), lambda b, i: (b, 0, i)),
                  const((1, 128)), const((1, 128)), const((GATE_ROWS, 1)), const((GATE_ROWS, 1))],
        out_specs=[tok(B_QK), tok(B_QK), tok(B_QK), tok(128),
                   pl.BlockSpec((1, GATE_ROWS, tm), lambda b, i: (b, 0, i))],
        out_shape=[shp(B_QK), shp(B_QK), shp(B_QK), shp(128),
                   jax.ShapeDtypeStruct((bsz, GATE_ROWS, seq), f32)],
        compiler_params=pltpu.CompilerParams(vmem_limit_bytes=VMEM_LIMIT),
        name="gdn_prep",
    )(qkv, qkv, conv_w.astype(f32), ab, abt, lane_row(a_log), lane_row(dt_bias),
      sub_col(a_log), sub_col(dt_bias))
    ts = 2 * CHUNK
    return pl.pallas_call(
        _gdn_main_kernel,
        grid=(bsz, seq // ts),
        in_specs=[tok(B_QK, ts), tok(B_QK, ts), tok(B_QK, ts), tok(B_QK, ts), tok(128, ts),
                  pl.BlockSpec((1, GATE_ROWS, ts), lambda b, i: (b, 0, i)),
                  const((1, B_HEAD_DIM))],
        out_specs=tok(B_QK, ts),
        out_shape=shp(B_QK),
        scratch_shapes=[pltpu.VMEM((B_HEADS, B_HEAD_DIM, B_HEAD_DIM), f32)],
        compiler_params=pltpu.CompilerParams(dimension_semantics=("arbitrary", "arbitrary")),
        name="gdn_main",
    )(q, k, v, z, gc, gr, g_onorm.astype(f32).reshape(1, B_HEAD_DIM))


def _mem_kv_kernel(mem_ref, g_ref, wk_ref, wv_ref, k_ref, v_ref):
    f32, bf16 = jnp.float32, jnp.bfloat16
    mn = _rms(mem_ref[0], g_ref[...]).astype(bf16)
    k = jnp.dot(mn, wk_ref[...], preferred_element_type=f32)
    v = jnp.dot(mn, wv_ref[...], preferred_element_type=f32)
    for h in range(X_HEADS):
        cols = slice(h * X_HEAD_DIM, (h + 1) * X_HEAD_DIM)
        k_ref[0, h] = k[:, cols].astype(bf16)
        v_ref[0, h] = v[:, cols].astype(bf16)


def _mid_kernel(x_ref, oa_ref, ob_ref, wo_ref, gx_ref, wq_ref, k_ref, v_ref, wox_ref, o_ref):
    f32, bf16 = jnp.float32, jnp.bfloat16
    na = oa_ref.shape[2]
    x1 = (x_ref[0]
          + jnp.dot(oa_ref[0].astype(bf16), wo_ref[:na, :], preferred_element_type=f32)
          + jnp.dot(ob_ref[0].astype(bf16), wo_ref[na:, :], preferred_element_type=f32))
    hq = _rms(x1, gx_ref[...]).astype(bf16)
    q = jnp.dot(hq, wq_ref[...], preferred_element_type=f32)
    heads = []
    for h in range(X_HEADS):
        qh = q[:, h * X_HEAD_DIM:(h + 1) * X_HEAD_DIM].astype(bf16)
        lg = _nt_dot(qh, k_ref[0, h]) * (X_HEAD_DIM ** -0.5)
        p = jnp.exp(lg - jnp.max(lg, axis=-1, keepdims=True))
        p = (p / jnp.sum(p, axis=-1, keepdims=True)).astype(bf16)
        heads.append(jnp.dot(p, v_ref[0, h], preferred_element_type=f32).astype(bf16))
    o = jnp.concatenate(heads, axis=1)
    o_ref[0] = x1 + jnp.dot(o, wox_ref[...], preferred_element_type=f32)


def mid_pallas(x, o_a, o_b, w_out, g_cross, mem, g_mem, wq, wk, wv, wo):
    bsz, seq, d = x.shape
    f32, bf16 = jnp.float32, jnp.bfloat16
    hx = X_HEADS * X_HEAD_DIM
    m_len = mem.shape[1]
    const2 = lambda shape: pl.BlockSpec(shape, lambda b: (0,) * len(shape))
    kv_spec = pl.BlockSpec((1, X_HEADS, m_len, X_HEAD_DIM), lambda b: (b, 0, 0, 0))
    kv_shape = jax.ShapeDtypeStruct((bsz, X_HEADS, m_len, X_HEAD_DIM), bf16)
    k, v = pl.pallas_call(
        _mem_kv_kernel,
        grid=(bsz,),
        in_specs=[pl.BlockSpec((1, m_len, d), lambda b: (b, 0, 0)), const2((1, d)),
                  const2((d, hx)), const2((d, hx))],
        out_specs=[kv_spec, kv_spec],
        out_shape=[kv_shape, kv_shape],
        name="mem_kv",
    )(mem, g_mem.reshape(1, d), wk.reshape(d, hx).astype(bf16), wv.reshape(d, hx).astype(bf16))
    tm = min(256, seq)
    tok = lambda w: pl.BlockSpec((1, tm, w), lambda b, t: (b, t, 0))
    const = lambda shape: pl.BlockSpec(shape, lambda b, t: (0,) * len(shape))
    kv_spec2 = pl.BlockSpec((1, X_HEADS, m_len, X_HEAD_DIM), lambda b, t: (b, 0, 0, 0))
    return pl.pallas_call(
        _mid_kernel,
        grid=(bsz, seq // tm),
        in_specs=[tok(d), tok(o_a.shape[2]), tok(o_b.shape[2]), const((MIX_WIDTH, d)),
                  const((1, d)), const((d, hx)), kv_spec2, kv_spec2, const((hx, d))],
        out_specs=tok(d),
        out_shape=jax.ShapeDtypeStruct((bsz, seq, d), f32),
        compiler_params=pltpu.CompilerParams(vmem_limit_bytes=VMEM_LIMIT),
        name="mid",
    )(x, o_a, o_b, w_out.astype(bf16), g_cross.reshape(1, d), wq.reshape(d, hx).astype(bf16),
      k, v, wo.reshape(hx, d).astype(bf16))


def kernel(x, mem, g_mix, w_in, g_cq, g_ckv, g_kidx, w_uq, w_iq, w_uk, w_uv, rel_bias, conv_w, A_log, dt_bias, g_onorm, w_out, g_cross, g_mem, wq_x, wk_x, wv_x, wo_x, g_ffn, w_pq, sub_keys, u_emb, v_emb, g_final):
    bsz, seq, d = x.shape
    for l in range(DEPTH):
        cq, ckv, kw, qkv, z, ab, abt = in_proj_pallas(x, g_mix[l], w_in[l])
        o_a = dsa_pallas_t(cq, ckv, kw, abt, g_cq[l], g_ckv[l], g_kidx[l],
                           w_uq[l], w_iq[l], w_uk[l], w_uv[l], rel_bias)
        o_b = gdn_pallas(qkv, z, ab, abt, conv_w[l], A_log[l], dt_bias[l], g_onorm[l])
        x = mid_pallas(x, o_a, o_b, w_out[l], g_cross[l], mem, g_mem[l],
                       wq_x[l], wk_x[l], wv_x[l], wo_x[l])
        xf = x.reshape(bsz * seq, d)
        y = peer_pallas(xf, g_ffn[l], w_pq[l], sub_keys[l], u_emb[l], v_emb[l])
        if l + 1 < DEPTH:
            x = (xf + y).reshape(bsz, seq, d)
    return final_residual_rmsnorm(xf, y, g_final).reshape(bsz, seq, d)
```

```python
import math
from functools import partial
import jax
import jax.numpy as jnp
from jax import lax
import numpy as np
from jax.experimental import pallas as pl
from jax.experimental.pallas import tpu as pltpu

D_MODEL = 1024
BATCH = 4
SEQ = 8192
DEPTH = 1

CHUNK = 64
Q_BLOCK = 128
EPS = 1e-6

A_HEADS = 8
A_HEAD_DIM = 64
A_Q_LORA = 256
A_KV_LORA = 256
IDX_HEADS = 8
IDX_DIM = 64
IDX_TOPK_MAX = 256
ATTN_SCALE = A_HEAD_DIM ** -0.5
IDX_SCALE = (IDX_HEADS * IDX_DIM) ** -0.5

B_HEADS = 4
B_HEAD_DIM = 128
B_QK = B_HEADS * B_HEAD_DIM
CONV_WIDTH = 4

REL_BUCKETS = 32
REL_MAX_DIST = 128

MEM_LEN = 256
X_HEADS = 4
X_HEAD_DIM = 128

P_HEADS = 8
N_KEYS = 128
N_EXPERTS = N_KEYS * N_KEYS
P_TOPK = 16
P_QDIM = 256
P_TOKEN_BLOCK = 128

COL_WIDTHS = (A_Q_LORA, A_KV_LORA, IDX_DIM, IDX_HEADS, B_QK, B_QK, B_QK, B_QK, B_HEADS, B_HEADS)
IN_COLS = sum(COL_WIDTHS)
MIX_WIDTH = A_HEADS * A_HEAD_DIM + B_HEADS * B_HEAD_DIM


def rmsnorm(x, g):
    xf = x.astype(jnp.float32)
    y = xf * lax.rsqrt(jnp.mean(xf * xf, axis=-1, keepdims=True) + EPS)
    return (y * g.astype(jnp.float32)).astype(x.dtype)


def l2norm(x):
    return x * lax.rsqrt(jnp.sum(x * x, axis=-1, keepdims=True) + EPS)


def split_columns(p, widths):
    out, start = [], 0
    for w in widths:
        out.append(p[..., start:start + w])
        start += w
    return out


def t5_bucket(rel):
    half = REL_BUCKETS // 2
    max_exact = half // 2
    n = jnp.abs(rel)
    nf = jnp.maximum(n, max_exact).astype(jnp.float32)
    large = max_exact + (jnp.log(nf / max_exact) / math.log(REL_MAX_DIST / max_exact)
                         * (half - max_exact)).astype(jnp.int32)
    large = jnp.minimum(large, half - 1)
    return jnp.where(rel > 0, half, 0) + jnp.where(n < max_exact, n, large)


def dsa_attention(cq, ckv, kidx, widx, g_cq, g_ckv, g_kidx, w_uq, w_iq, w_uk, w_uv, rel_bias):
    bsz, seq, _ = cq.shape
    topk = min(IDX_TOPK_MAX, seq // 4)
    n_blocks = seq // Q_BLOCK
    cq = rmsnorm(cq, g_cq)
    ckv = rmsnorm(ckv, g_ckv)
    kidx = rmsnorm(kidx, g_kidx).astype(jnp.float32)
    key_pos = jnp.arange(seq, dtype=jnp.int32)

    def blockify(a):
        return jnp.moveaxis(a.reshape(bsz, n_blocks, Q_BLOCK, a.shape[-1]), 1, 0)

    def one_block(args):
        cq_b, w_b, start = args
        qpos = start + jnp.arange(Q_BLOCK, dtype=jnp.int32)
        limit = (qpos // CHUNK + 1) * CHUNK
        admissible = key_pos[None, :] < limit[:, None]
        qi = jnp.einsum('bqc,chd->bqhd', cq_b, w_iq).astype(jnp.float32)
        dots = jax.nn.relu(jnp.einsum('bqhd,bsd->bqhs', qi, kidx))
        score = jnp.einsum('bqh,bqhs->bqs', w_b.astype(jnp.float32) * IDX_SCALE, dots)
        score = jnp.where(admissible[None], score, -jnp.inf)
        _, sel = lax.top_k(score, topk)
        valid = sel < limit[None, :, None]
        c_sel = jax.vmap(lambda c, i: c[i])(ckv, sel)
        q = jnp.einsum('bqc,chd->bqhd', cq_b, w_uq)
        q_lat = jnp.einsum('bqhd,chd->bqhc', q, w_uk)
        logits = jnp.einsum('bqhc,bqkc->bhqk', q_lat, c_sel).astype(jnp.float32) * ATTN_SCALE
        bias = rel_bias[t5_bucket(sel - qpos[None, :, None])].astype(jnp.float32)
        logits = logits + jnp.transpose(bias, (0, 3, 1, 2))
        logits = jnp.where(valid[:, None], logits, -jnp.inf)
        p = jax.nn.softmax(logits, axis=-1).astype(c_sel.dtype)
        o_lat = jnp.einsum('bhqk,bqkc->bqhc', p, c_sel)
        o = jnp.einsum('bqhc,chd->bqhd', o_lat, w_uv)
        return o.reshape(bsz, Q_BLOCK, A_HEADS * A_HEAD_DIM)

    starts = jnp.arange(n_blocks, dtype=jnp.int32) * Q_BLOCK
    o = lax.map(one_block, (blockify(cq), blockify(widx), starts))
    return jnp.moveaxis(o, 0, 1).reshape(bsz, seq, A_HEADS * A_HEAD_DIM)


INT_MIN = -2147483648
NEG_BIG = -1e30
KEY_TILE = 512
KEY_PAD = KEY_TILE - Q_BLOCK
VMEM_LIMIT = 56 * 1024 * 1024


def _rms(x, g):
    return x * lax.rsqrt(jnp.mean(x * x, axis=-1, keepdims=True) + EPS) * g


def _nt_dot(a, b):
    return lax.dot_general(a, b, (((1,), (1,)), ((), ())), preferred_element_type=jnp.float32)


def _dsa_prep_kernel(cq_ref, ckv_ref, kw_ref, gcq_ref, gckv_ref, gk_ref,
                     wiq_ref, wuq_ref, wukt_ref,
                     a_ref, qlat_ref, ckvn_ref, kidxn_ref, w_ref):
    bf16 = jnp.bfloat16
    kw = kw_ref[0]
    cqn = _rms(cq_ref[0], gcq_ref[...]).astype(bf16)
    qi = jnp.dot(cqn, wiq_ref[...], preferred_element_type=jnp.float32)
    q = jnp.dot(cqn, wuq_ref[...], preferred_element_type=jnp.float32)
    for h in range(A_HEADS):
        rows = slice(h * Q_BLOCK, (h + 1) * Q_BLOCK)
        a_ref[0, 0, rows, :] = qi[:, h * IDX_DIM:(h + 1) * IDX_DIM].astype(bf16)
        qh = q[:, h * A_HEAD_DIM:(h + 1) * A_HEAD_DIM].astype(bf16)
        ql = jnp.dot(qh, wukt_ref[h], preferred_element_type=jnp.float32) * ATTN_SCALE
        qlat_ref[0, 0, rows, :] = ql.astype(bf16)
    ckvn_ref[0] = _rms(ckv_ref[0], gckv_ref[...]).astype(bf16)
    kidxn_ref[0] = _rms(kw[:, :IDX_DIM], gk_ref[...]).astype(bf16)
    w_ref[0, 0] = kw[:, IDX_DIM:IDX_DIM + IDX_HEADS] * IDX_SCALE


def _dsa_main_kernel(a_ref, qlat_ref, w_ref, kidx_ref, ckv_ref, bias0_ref, wuv_ref, o_ref,
                     sc_ref, big_ref, p_ref, acc_ref, m_ref, l_ref, alpha_ref, wb_ref, *, topk):
    f32, i32, bf16 = jnp.float32, jnp.int32, jnp.bfloat16
    Q, T = Q_BLOCK, KEY_TILE
    NC = T // 128
    i = pl.program_id(1)
    e = (i + 1) * Q
    nt = (i + T // Q) // (T // Q)
    kf = jnp.float32(topk)

    w_blk = w_ref[0, 0]
    for h in range(IDX_HEADS):
        wb_ref[h] = jnp.broadcast_to(w_blk[:, h:h + 1], (Q, 128))
    row = lax.broadcasted_iota(i32, (Q, 128), 0)
    lane = lax.broadcasted_iota(i32, (Q, 128), 1)
    limit = ((i * Q + row) // CHUNK + 1) * CHUNK
    a_mat = a_ref[0, 0]

    def tile_start(j):
        return pl.multiple_of(e + KEY_PAD - (j + 1) * T, 128)

    def score_tile(j, carry):
        start = tile_start(j)
        big_ref[...] = _nt_dot(a_mat, kidx_ref[0, pl.ds(start, T), :])
        for c in range(NC):
            cols = slice(c * 128, (c + 1) * 128)
            s = jnp.zeros((Q, 128), f32)
            for h in range(IDX_HEADS):
                s = s + wb_ref[h] * jnp.maximum(big_ref[h * Q:(h + 1) * Q, cols], 0.0)
            bits = lax.bitcast_convert_type(s, i32)
            key = jnp.where(bits < 0, bits ^ jnp.int32(0x7FFFFFFF), bits)
            key = jnp.where(s == 0.0, 0, key)
            kpos = start - KEY_PAD + c * 128 + lane
            key = jnp.where(kpos >= 0, key, INT_MIN)
            key = jnp.where(kpos < limit, key, INT_MIN)
            sc_ref[j, :, cols] = key
        return carry

    lax.fori_loop(0, nt, score_tile, 0)

    def count_where(pred):
        def body(j, cnt):
            for c in range(NC):
                cols = slice(c * 128, (c + 1) * 128)
                kpos = tile_start(j) - KEY_PAD + c * 128 + lane
                cnt = cnt + jnp.where(pred(sc_ref[j, :, cols], kpos), 1.0, 0.0)
            return cnt
        cnt = lax.fori_loop(0, nt, body, jnp.zeros((Q, 128), f32))
        return jnp.broadcast_to(jnp.sum(cnt, axis=1, keepdims=True), (Q, 128))

    def bit_body(b, carry):
        u, cacc = carry
        cand = u | lax.shift_left(jnp.int32(1), 31 - b)
        tvec = cand ^ jnp.int32(INT_MIN)
        tot = count_where(lambda k, kpos: k >= tvec)
        ok = tot >= kf
        return jnp.where(ok, cand, u), jnp.where(ok, tot, cacc)

    u, cacc = lax.fori_loop(0, 32, bit_body,
                            (jnp.zeros((Q, 128), i32), jnp.zeros((Q, 128), f32)))
    thr = jnp.maximum(u ^ jnp.int32(INT_MIN), INT_MIN + 1)
    overflow = jnp.where(u != 0, cacc, 0.0) > kf
    n_over = jnp.max(jnp.max(jnp.where(overflow, 1.0, 0.0), axis=1, keepdims=True),
                     axis=0, keepdims=True)[0, 0]

    @pl.when(n_over > 0.0)
    def _():
        need = kf - count_where(lambda k, kpos: k > thr)

        def cut_body(b, cut):
            cand = cut | lax.shift_left(jnp.int32(1), 14 - b)
            cnt = count_where(lambda k, kpos: jnp.where(k == thr, kpos, cand) < cand)
            return jnp.where(cnt <= need, cand, cut)

        cut = lax.fori_loop(0, 15, cut_body, jnp.zeros((Q, 128), i32))

        def drop_tile(j, carry):
            for c in range(NC):
                cols = slice(c * 128, (c + 1) * 128)
                kpos = tile_start(j) - KEY_PAD + c * 128 + lane
                k = sc_ref[j, :, cols]
                drop = jnp.where(k == thr, kpos, -1) >= cut
                sc_ref[j, :, cols] = jnp.where(drop, INT_MIN, k)
            return carry

        lax.fori_loop(0, nt, drop_tile, 0)

    m_ref[...] = jnp.full(m_ref.shape, NEG_BIG, f32)
    l_ref[...] = jnp.zeros(l_ref.shape, f32)
    acc_ref[...] = jnp.zeros(acc_ref.shape, f32)
    qlat = qlat_ref[0, 0]

    def attn_tile(j, with_bias):
        start = tile_start(j)
        kv = ckv_ref[0, pl.ds(start, T), :]
        big_ref[...] = _nt_dot(qlat, kv)
        mask_add = [jnp.where(sc_ref[j, :, c * 128:(c + 1) * 128] >= thr, 0.0, NEG_BIG)
                    for c in range(NC)]
        for h in range(A_HEADS):
            rows = slice(h * Q, (h + 1) * Q)
            lg = []
            for c in range(NC):
                x = big_ref[rows, c * 128:(c + 1) * 128] + mask_add[c]
                if with_bias:
                    x = x + bias0_ref[h, :, c * 128:(c + 1) * 128]
                lg.append(x)
            mx = lg[0]
            for c in range(1, NC):
                mx = jnp.maximum(mx, lg[c])
            m_prev = m_ref[rows, :]
            m_new = jnp.maximum(m_prev, jnp.max(mx, axis=1, keepdims=True))
            psum = jnp.zeros((Q, 128), f32)
            for c in range(NC):
                p = jnp.exp(lg[c] - m_new)
                psum = psum + p
                p_ref[rows, c * 128:(c + 1) * 128] = p.astype(bf16)
            alpha = jnp.exp(m_prev - m_new)
            l_ref[rows, :] = alpha * l_ref[rows, :] + jnp.sum(psum, axis=1, keepdims=True)
            m_ref[rows, :] = m_new
            alpha_ref[rows, :] = alpha
        acc_ref[...] = alpha_ref[...] * acc_ref[...] + jnp.dot(
            p_ref[...], kv, preferred_element_type=f32)

    attn_tile(0, True)

    def attn_body(j, carry):
        attn_tile(j, False)
        return carry

    lax.fori_loop(1, nt, attn_body, 0)

    inv_l = 1.0 / l_ref[...]
    for h in range(A_HEADS):
        rows = slice(h * Q, (h + 1) * Q)
        o_lat = (acc_ref[rows, :] * inv_l[rows, :]).astype(bf16)
        o_ref[0, :, h * A_HEAD_DIM:(h + 1) * A_HEAD_DIM] = jnp.dot(
            o_lat, wuv_ref[h], preferred_element_type=f32)


def dsa_pallas(cq, ckv, kw, g_cq, g_ckv, g_kidx, w_uq, w_iq, w_uk, w_uv, rel_bias):
    bsz, seq, _ = cq.shape
    f32, bf16 = jnp.float32, jnp.bfloat16
    Q, T, H = Q_BLOCK, KEY_TILE, A_HEADS
    nblk = seq // Q
    topk = min(IDX_TOPK_MAX, seq // 4)
    wiq2 = w_iq.reshape(A_Q_LORA, IDX_HEADS * IDX_DIM).astype(bf16)
    wuq2 = w_uq.reshape(A_Q_LORA, H * A_HEAD_DIM).astype(bf16)
    wukt = jnp.transpose(w_uk, (1, 2, 0)).astype(bf16)
    wuv = jnp.transpose(w_uv, (1, 0, 2)).astype(bf16)
    tok = lambda w: pl.BlockSpec((1, Q, w), lambda b, t: (b, t, 0))
    full = lambda shape: pl.BlockSpec(shape, lambda b, t: (0,) * len(shape))
    a_mat, qlat, ckvn, kidxn, wsc = pl.pallas_call(
        _dsa_prep_kernel,
        grid=(bsz, nblk),
        in_specs=[tok(A_Q_LORA), tok(A_KV_LORA), tok(128),
                  full((1, A_Q_LORA)), full((1, A_KV_LORA)), full((1, IDX_DIM)),
                  full(wiq2.shape), full(wuq2.shape), full(wukt.shape)],
        out_specs=[pl.BlockSpec((1, 1, H * Q, IDX_DIM), lambda b, t: (b, t, 0, 0)),
                   pl.BlockSpec((1, 1, H * Q, A_KV_LORA), lambda b, t: (b, t, 0, 0)),
                   tok(A_KV_LORA), tok(IDX_DIM),
                   pl.BlockSpec((1, 1, Q, IDX_HEADS), lambda b, t: (b, t, 0, 0))],
        out_shape=[jax.ShapeDtypeStruct((bsz, nblk, H * Q, IDX_DIM), bf16),
                   jax.ShapeDtypeStruct((bsz, nblk, H * Q, A_KV_LORA), bf16),
                   jax.ShapeDtypeStruct((bsz, seq, A_KV_LORA), bf16),
                   jax.ShapeDtypeStruct((bsz, seq, IDX_DIM), bf16),
                   jax.ShapeDtypeStruct((bsz, nblk, Q, IDX_HEADS), f32)],
        name="dsa_prep",
    )(cq, ckv, kw, g_cq.reshape(1, -1), g_ckv.reshape(1, -1), g_kidx.reshape(1, -1),
      wiq2, wuq2, wukt)
    ckvp = jnp.pad(ckvn, ((0, 0), (KEY_PAD, 0), (0, 0)))
    kidxp = jnp.pad(kidxn, ((0, 0), (KEY_PAD, 0), (0, 0)))
    rel = (jnp.arange(T, dtype=jnp.int32)[None, :] - jnp.arange(Q, dtype=jnp.int32)[:, None]
           - KEY_PAD)
    far = rel_bias[t5_bucket(jnp.int32(-KEY_PAD - 1))]
    bias0 = jnp.transpose(rel_bias[t5_bucket(rel)] - far, (2, 0, 1)).astype(f32)
    nt_max = (nblk - 1 + T // Q) // (T // Q)
    skey = seq + KEY_PAD
    out = pl.pallas_call(
        partial(_dsa_main_kernel, topk=topk),
        grid=(bsz, nblk),
        in_specs=[pl.BlockSpec((1, 1, H * Q, IDX_DIM), lambda b, t: (b, t, 0, 0)),
                  pl.BlockSpec((1, 1, H * Q, A_KV_LORA), lambda b, t: (b, t, 0, 0)),
                  pl.BlockSpec((1, 1, Q, IDX_HEADS), lambda b, t: (b, t, 0, 0)),
                  pl.BlockSpec((1, skey, IDX_DIM), lambda b, t: (b, 0, 0)),
                  pl.BlockSpec((1, skey, A_KV_LORA), lambda b, t: (b, 0, 0)),
                  full(bias0.shape), full(wuv.shape)],
        out_specs=tok(H * A_HEAD_DIM),
        out_shape=jax.ShapeDtypeStruct((bsz, seq, H * A_HEAD_DIM), f32),
        scratch_shapes=[pltpu.VMEM((nt_max, Q, T), jnp.int32),
                        pltpu.VMEM((H * Q, T), f32),
                        pltpu.VMEM((H * Q, T), bf16),
                        pltpu.VMEM((H * Q, A_KV_LORA), f32),
                        pltpu.VMEM((H * Q, 1), f32),
                        pltpu.VMEM((H * Q, 1), f32),
                        pltpu.VMEM((H * Q, 1), f32),
                        pltpu.VMEM((IDX_HEADS, Q, 128), f32)],
        compiler_params=pltpu.CompilerParams(
            dimension_semantics=("arbitrary", "arbitrary"), vmem_limit_bytes=VMEM_LIMIT),
        name="dsa_main",
    )(a_mat, qlat, wsc, kidxp, ckvp, bias0, wuv)
    return out


def _dsa_prep_t_kernel(cq_ref, ckv_ref, kw_ref, gcq_ref, gckv_ref, gk_ref,
                       wiqt_ref, wuqt_ref, wuk_ref,
                       at_ref, qlatt_ref, ckvn_ref, kidxn_ref):
    f32, bf16 = jnp.float32, jnp.bfloat16
    Q = Q_BLOCK
    kw = kw_ref[0]
    cqn = _rms(cq_ref[0], gcq_ref[...]).astype(bf16)
    qit = _nt_dot(wiqt_ref[...], cqn)
    qt = _nt_dot(wuqt_ref[...], cqn)
    for h in range(A_HEADS):
        cols = slice(h * Q, (h + 1) * Q)
        at_ref[0, 0, :, cols] = qit[h * IDX_DIM:(h + 1) * IDX_DIM, :].astype(bf16)
        qh = qt[h * A_HEAD_DIM:(h + 1) * A_HEAD_DIM, :].astype(bf16)
        ql = jnp.dot(wuk_ref[h], qh, preferred_element_type=f32) * ATTN_SCALE
        qlatt_ref[0, 0, :, cols] = ql.astype(bf16)
    ckvn_ref[0] = _rms(ckv_ref[0], gckv_ref[...]).astype(bf16)
    kidxn_ref[0] = _rms(kw[:, :IDX_DIM], gk_ref[...]).astype(bf16)


def _dsa_main_t_kernel(at_ref, qlatt_ref, wt_ref, kidx_ref, ckv_ref, bias0_ref, wuvt_ref, o_ref,
                       sc_ref, big_ref, acc_ref, m_ref, l_ref, *, topk):
    f32, i32, bf16 = jnp.float32, jnp.int32, jnp.bfloat16
    Q, T = Q_BLOCK, KEY_TILE
    i = pl.program_id(1)
    e = (i + 1) * Q
    nt = (i + T // Q) // (T // Q)
    kf = jnp.float32(topk)
    wsc = wt_ref[0] * IDX_SCALE
    sub = lax.broadcasted_iota(i32, (T, Q), 0)
    qlane = lax.broadcasted_iota(i32, (T, Q), 1)
    limit = ((i * Q + qlane) // CHUNK + 1) * CHUNK
    at = at_ref[0, 0]

    def tile_start(j):
        return pl.multiple_of(e + KEY_PAD - (j + 1) * T, 128)

    def key_pos(j):
        return tile_start(j) - KEY_PAD + sub

    def score_tile(j, carry):
        kid = kidx_ref[0, pl.ds(tile_start(j), T), :]
        big_ref[...] = jnp.dot(kid, at, preferred_element_type=f32)
        s = jnp.zeros((T, Q), f32)
        for h in range(IDX_HEADS):
            s = s + wsc[h:h + 1, :] * jnp.maximum(big_ref[:, h * Q:(h + 1) * Q], 0.0)
        bits = lax.bitcast_convert_type(s, i32)
        key = jnp.where(bits < 0, bits ^ jnp.int32(0x7FFFFFFF), bits)
        key = jnp.where(s == 0.0, 0, key)
        kpos = key_pos(j)
        key = jnp.where(kpos >= 0, key, INT_MIN)
        sc_ref[j] = jnp.where(kpos < limit, key, INT_MIN)
        return carry

    lax.fori_loop(0, nt, score_tile, 0)

    def count_where(pred):
        def body(j, cnt):
            hit = jnp.where(pred(sc_ref[j], key_pos(j)), 1.0, 0.0)
            return cnt + jnp.sum(hit.reshape(T // 64, 64, Q), axis=0)
        cnt = lax.fori_loop(0, nt, body, jnp.zeros((64, Q), f32))
        return jnp.sum(cnt, axis=0, keepdims=True)

    def bit_body(b, carry):
        u, cacc = carry
        cand = u | lax.shift_left(jnp.int32(1), 31 - b)
        tvec = cand ^ jnp.int32(INT_MIN)
        tot = count_where(lambda k, kpos: k >= tvec)
        ok = tot >= kf
        return jnp.where(ok, cand, u), jnp.where(ok, tot, cacc)

    u, cacc = lax.fori_loop(0, 32, bit_body, (jnp.zeros((1, Q), i32), jnp.zeros((1, Q), f32)))
    thr = jnp.maximum(u ^ jnp.int32(INT_MIN), INT_MIN + 1)
    overflow = jnp.where(u != 0, cacc, 0.0) > kf
    n_over = jnp.max(jnp.where(overflow, 1.0, 0.0), axis=1, keepdims=True)[0, 0]

    @pl.when(n_over > 0.0)
    def _():
        need = kf - count_where(lambda k, kpos: k > thr)

        def cut_body(b, cut):
            cand = cut | lax.shift_left(jnp.int32(1), 14 - b)
            cnt = count_where(lambda k, kpos: jnp.where(k == thr, kpos, cand) < cand)
            return jnp.where(cnt <= need, cand, cut)

        cut = lax.fori_loop(0, 15, cut_body, jnp.zeros((1, Q), i32))

        def drop_tile(j, carry):
            k = sc_ref[j]
            drop = jnp.where(k == thr, key_pos(j), -1) >= cut
            sc_ref[j] = jnp.where(drop, INT_MIN, k)
            return carry

        lax.fori_loop(0, nt, drop_tile, 0)

    m_ref[...] = jnp.full(m_ref.shape, NEG_BIG, f32)
    l_ref[...] = jnp.zeros(l_ref.shape, f32)
    acc_ref[...] = jnp.zeros(acc_ref.shape, f32)
    qlatt = qlatt_ref[0, 0]

    def attn_tile(j, with_bias):
        kv = ckv_ref[0, pl.ds(tile_start(j), T), :]
        kvt = kv.T
        mask_add = jnp.where(sc_ref[j] >= thr, 0.0, NEG_BIG)
        pair_cols = [slice(g * 2 * Q, (g + 1) * 2 * Q) for g in range(A_HEADS // 2)]
        logits = lambda g: jnp.dot(kv, qlatt[:, pair_cols[g]], preferred_element_type=f32)
        x_next = logits(0)
        for g in range(A_HEADS // 2):
            x_pair = x_next
            if g + 1 < A_HEADS // 2:
                x_next = logits(g + 1)
            ps, alphas = [], []
            for u in range(2):
                h = 2 * g + u
                cols = slice(h * Q, (h + 1) * Q)
                x = x_pair[:, u * Q:(u + 1) * Q] + mask_add
                if with_bias:
                    x = x + bias0_ref[h]
                m_prev = m_ref[:, cols]
                m_new = jnp.maximum(m_prev, jnp.max(x, axis=0, keepdims=True))
                p = jnp.exp(x - m_new)
                alpha = jnp.exp(m_prev - m_new)
                l_ref[:, cols] = alpha * l_ref[:, cols] + jnp.sum(p, axis=0, keepdims=True)
                m_ref[:, cols] = m_new
                ps.append(p.astype(bf16))
                alphas.append(alpha)
            pv = jnp.dot(kvt, jnp.concatenate(ps, axis=1), preferred_element_type=f32)
            acc_ref[:, pair_cols[g]] = (
                jnp.concatenate(alphas, axis=1) * acc_ref[:, pair_cols[g]] + pv)

    attn_tile(0, True)

    def attn_body(j, carry):
        attn_tile(j, False)
        return carry

    lax.fori_loop(1, nt, attn_body, 0)

    inv_l = 1.0 / l_ref[...]
    outs = []
    for h in range(A_HEADS):
        cols = slice(h * Q, (h + 1) * Q)
        o_lat_t = (acc_ref[:, cols] * inv_l[:, cols]).astype(bf16)
        outs.append(jnp.dot(wuvt_ref[h], o_lat_t, preferred_element_type=f32))
    o_ref[0] = jnp.concatenate(outs, axis=0).T


def dsa_pallas_t(cq, ckv, kw, widx_t, g_cq, g_ckv, g_kidx, w_uq, w_iq, w_uk, w_uv, rel_bias):
    bsz, seq, _ = cq.shape
    f32, bf16 = jnp.float32, jnp.bfloat16
    Q, T, H = Q_BLOCK, KEY_TILE, A_HEADS
    nblk = seq // Q
    topk = min(IDX_TOPK_MAX, seq // 4)
    wiqt = w_iq.reshape(A_Q_LORA, IDX_HEADS * IDX_DIM).T.astype(bf16)
    wuqt = w_uq.reshape(A_Q_LORA, H * A_HEAD_DIM).T.astype(bf16)
    wuk = jnp.transpose(w_uk, (1, 0, 2)).astype(bf16)
    wuvt = jnp.transpose(w_uv, (1, 2, 0)).astype(bf16)
    tok = lambda w: pl.BlockSpec((1, Q, w), lambda b, t: (b, t, 0))
    full = lambda shape: pl.BlockSpec(shape, lambda b, t: (0,) * len(shape))
    blk = lambda r: pl.BlockSpec((1, 1, r, H * Q), lambda b, t: (b, t, 0, 0))
    a_t, qlat_t, ckvn, kidxn = pl.pallas_call(
        _dsa_prep_t_kernel,
        grid=(bsz, nblk),
        in_specs=[tok(A_Q_LORA), tok(A_KV_LORA), tok(128),
                  full((1, A_Q_LORA)), full((1, A_KV_LORA)), full((1, IDX_DIM)),
                  full(wiqt.shape), full(wuqt.shape), full(wuk.shape)],
        out_specs=[blk(IDX_DIM), blk(A_KV_LORA), tok(A_KV_LORA), tok(IDX_DIM)],
        out_shape=[jax.ShapeDtypeStruct((bsz, nblk, IDX_DIM, H * Q), bf16),
                   jax.ShapeDtypeStruct((bsz, nblk, A_KV_LORA, H * Q), bf16),
                   jax.ShapeDtypeStruct((bsz, seq, A_KV_LORA), bf16),
                   jax.ShapeDtypeStruct((bsz, seq, IDX_DIM), bf16)],
        name="dsa_prep",
    )(cq, ckv, kw, g_cq.reshape(1, -1), g_ckv.reshape(1, -1), g_kidx.reshape(1, -1),
      wiqt, wuqt, wuk)
    ckvp = jnp.pad(ckvn, ((0, 0), (KEY_PAD, 0), (0, 0)))
    kidxp = jnp.pad(kidxn, ((0, 0), (KEY_PAD, 0), (0, 0)))
    rel = (jnp.arange(T, dtype=jnp.int32)[:, None] - jnp.arange(Q, dtype=jnp.int32)[None, :]
           - KEY_PAD)
    far = rel_bias[t5_bucket(jnp.int32(-KEY_PAD - 1))]
    bias0 = jnp.transpose(rel_bias[t5_bucket(rel)] - far, (2, 0, 1)).astype(f32)
    nt_max = (nblk - 1 + T // Q) // (T // Q)
    skey = seq + KEY_PAD
    return pl.pallas_call(
        partial(_dsa_main_t_kernel, topk=topk),
        grid=(bsz, nblk),
        in_specs=[blk(IDX_DIM), blk(A_KV_LORA),
                  pl.BlockSpec((1, IDX_HEADS, Q), lambda b, t: (b, 1, t)),
                  pl.BlockSpec((1, skey, IDX_DIM), lambda b, t: (b, 0, 0)),
                  pl.BlockSpec((1, skey, A_KV_LORA), lambda b, t: (b, 0, 0)),
                  full(bias0.shape), full(wuvt.shape)],
        out_specs=tok(H * A_HEAD_DIM),
        out_shape=jax.ShapeDtypeStruct((bsz, seq, H * A_HEAD_DIM), f32),
        scratch_shapes=[pltpu.VMEM((nt_max, T, Q), jnp.int32),
                        pltpu.VMEM((T, H * Q), f32),
                        pltpu.VMEM((A_KV_LORA, H * Q), f32),
                        pltpu.VMEM((1, H * Q), f32),
                        pltpu.VMEM((1, H * Q), f32)],
        compiler_params=pltpu.CompilerParams(
            dimension_semantics=("arbitrary", "arbitrary"), vmem_limit_bytes=VMEM_LIMIT),
        name="dsa_main",
    )(a_t, qlat_t, widx_t, kidxp, ckvp, bias0, wuvt)


def causal_dwconv(x, w):
    ch = x.shape[-1]
    return lax.conv_general_dilated(x, w[:, None, :].astype(x.dtype), window_strides=(1,),
                                    padding=[(CONV_WIDTH - 1, 0)],
                                    dimension_numbers=('NWC', 'WIO', 'NWC'),
                                    feature_group_count=ch)


def gated_deltanet(q, k, v, z, a, b, conv_w, A_log, dt_bias, g_onorm):
    bsz, seq, _ = q.shape
    f32 = jnp.float32
    H, Dh = B_HEADS, B_HEAD_DIM
    nc = seq // CHUNK
    qkv = jax.nn.silu(causal_dwconv(jnp.concatenate([q, k, v], axis=-1), conv_w)).astype(f32)
    q, k, v = [t.reshape(bsz, seq, H, Dh) for t in jnp.split(qkv, 3, axis=-1)]
    q = l2norm(q) * (Dh ** -0.5)
    k = l2norm(k)
    beta = jax.nn.sigmoid(b.astype(f32))
    g = -jnp.exp(A_log.astype(f32)) * jax.nn.softplus(a.astype(f32) + dt_bias.astype(f32))

    def to_chunks(t):
        t = t.reshape(bsz, nc, CHUNK, *t.shape[2:])
        return jnp.moveaxis(t, 3, 1)

    q, k, v, beta = to_chunks(q), to_chunks(k), to_chunks(v), to_chunks(beta)
    g = jnp.cumsum(to_chunks(g), axis=-1)
    pos = jnp.arange(CHUNK)
    causal = pos[:, None] >= pos[None, :]
    strict = pos[:, None] > pos[None, :]
    decay = jnp.exp(jnp.where(causal, g[..., :, None] - g[..., None, :], -jnp.inf))
    m = jnp.where(strict, beta[..., :, None] * jnp.einsum('bhnid,bhnjd->bhnij', k, k) * decay, 0.0)
    rhs = jnp.concatenate([v * beta[..., None], k * (beta * jnp.exp(g))[..., None]], axis=-1)
    sol = lax.linalg.triangular_solve(m, rhs, left_side=True, lower=True, unit_diagonal=True)
    u, w = sol[..., :Dh], sol[..., Dh:]
    intra = jnp.einsum('bhnid,bhnjd->bhnij', q, k) * decay
    q_dec = q * jnp.exp(g)[..., None]
    k_tail = k * jnp.exp(g[..., -1:] - g)[..., None]
    chunk_decay = jnp.exp(g[..., -1])

    def step(state, inp):
        q_c, u_c, w_c, a_c, kt_c, d_c = inp
        v_new = u_c - jnp.einsum('bhcd,bhde->bhce', w_c, state)
        o_c = jnp.einsum('bhcd,bhde->bhce', q_c, state) + jnp.einsum('bhij,bhje->bhie', a_c, v_new)
        state = state * d_c[..., None, None] + jnp.einsum('bhcd,bhce->bhde', kt_c, v_new)
        return state, o_c

    xs = tuple(jnp.moveaxis(t, 2, 0) for t in (q_dec, u, w, intra, k_tail, chunk_decay))
    _, o = lax.scan(step, jnp.zeros((bsz, H, Dh, Dh), f32), xs)
    o = jnp.transpose(o, (1, 0, 3, 2, 4)).reshape(bsz, seq, H, Dh)
    o = o * lax.rsqrt(jnp.mean(o * o, axis=-1, keepdims=True) + EPS) * g_onorm.astype(f32)
    o = o * jax.nn.silu(z.astype(f32).reshape(bsz, seq, H, Dh))
    return o.reshape(bsz, seq, H * Dh).astype(z.dtype)


def memory_cross_attention(h, mem_n, wq, wk, wv, wo):
    q = jnp.einsum('bsd,dhe->bhse', h, wq)
    k = jnp.einsum('bmd,dhe->bhme', mem_n, wk)
    v = jnp.einsum('bmd,dhe->bhme', mem_n, wv)
    logits = jnp.einsum('bhse,bhme->bhsm', q, k).astype(jnp.float32) * (X_HEAD_DIM ** -0.5)
    p = jax.nn.softmax(logits, axis=-1).astype(v.dtype)
    o = jnp.einsum('bhsm,bhme->bshe', p, v)
    return jnp.einsum('bshe,hed->bsd', o, wo)


def peer_ffn(h, w_pq, sub_keys, u_emb, v_emb):
    bsz, seq, d = h.shape
    n_tok = bsz * seq
    hb = h.reshape(n_tok // P_TOKEN_BLOCK, P_TOKEN_BLOCK, d)

    def one_block(xb):
        n = xb.shape[0]
        qr = jnp.einsum('nd,dhe->nhe', xb, w_pq).reshape(n, P_HEADS, 2, P_QDIM // 2)
        s = jnp.einsum('nhpe,hpke->nhpk', qr, sub_keys).astype(jnp.float32)
        s1, i1 = lax.top_k(s[:, :, 0], P_TOPK)
        s2, i2 = lax.top_k(s[:, :, 1], P_TOPK)
        cand = (s1[..., :, None] + s2[..., None, :]).reshape(n, P_HEADS, P_TOPK * P_TOPK)
        cidx = (i1[..., :, None] * N_KEYS + i2[..., None, :]).reshape(n, P_HEADS, P_TOPK * P_TOPK)
        best, pos = lax.top_k(cand, P_TOPK)
        eidx = jnp.take_along_axis(cidx, pos, axis=-1)
        gate = jax.nn.softmax(best, axis=-1).astype(xb.dtype)
        act = jax.nn.gelu(jnp.einsum('nhkd,nd->nhk', u_emb[eidx], xb), approximate=False)
        return jnp.einsum('nhk,nhkd->nd', gate * act, v_emb[eidx])

    return lax.map(one_block, hb).reshape(bsz, seq, d)


PEER_SCORE_TOKENS = 256
PEER_GATHER_TOKENS = 64
PEER_SLOTS = P_HEADS * P_TOPK
ROW_VREG = (8, 128)
WORDS_PER_ROW = 4
PEER_TILES = 8


def _top16(s, order=None, payload=None):
    if order is None:
        order = lax.broadcasted_iota(jnp.int32, s.shape, 0).astype(jnp.float32)
    vals, picks = [], []
    for _ in range(P_TOPK):
        m = jnp.max(s, axis=0, keepdims=True)
        pos = jnp.min(jnp.where(s == m, order, float(N_EXPERTS)), axis=0, keepdims=True)
        hit = order == pos
        vals.append(m)
        if payload is None:
            picks.append(pos)
        else:
            picks.append(jnp.max(jnp.where(hit, payload, -1.0), axis=0, keepdims=True))
        s = jnp.where(hit, -jnp.inf, s)
    return vals, picks


def _staircase_candidates(v1, i1, v2, i2):
    v1m, i1m = jnp.concatenate(v1, axis=0), jnp.concatenate(i1, axis=0)
    v2m, i2m = jnp.concatenate(v2, axis=0), jnp.concatenate(i2, axis=0)
    t = v1m.shape[1]
    sub8 = lax.broadcasted_iota(jnp.int32, (8, t), 0).astype(jnp.float32)
    cand, cidx, rank = [], [], []
    for a in range(8):
        cand.append(v1[a] + v2m[:8])
        cidx.append(i1[a] * float(N_KEYS) + i2m[:8])
        rank.append(sub8 + float(a * P_TOPK))
    cand.append(v1[0] + v2m[8:])
    cidx.append(i1[0] * float(N_KEYS) + i2m[8:])
    rank.append(sub8 + 8.0)
    cand.append(v1m[8:] + v2[0])
    cidx.append(i1m[8:] * float(N_KEYS) + i2[0])
    rank.append((sub8 + 8.0) * float(P_TOPK))
    cat = lambda xs: jnp.concatenate(xs, axis=0)
    return cat(cand), cat(rank), cat(cidx)


def _peer_score_kernel(x_ref, g_ref, wpqt_ref, sk_ref, hn_ref, eidx_ref, gate_ref):
    f32, bf16 = jnp.float32, jnp.bfloat16
    hb = _rms(x_ref[...], g_ref[...]).astype(bf16)
    hn_ref[...] = hb
    qrt = _nt_dot(wpqt_ref[...], hb)
    half = P_QDIM // 2
    e_rows, g_rows = [], []
    for hd in range(P_HEADS):
        tops = []
        for p in range(2):
            qhp = qrt[(hd * 2 + p) * half:(hd * 2 + p + 1) * half, :].astype(bf16)
            s = jnp.dot(sk_ref[hd * 2 + p], qhp, preferred_element_type=f32)
            tops.append(_top16(s))
        (v1, i1), (v2, i2) = tops
        best, be = _top16(*_staircase_candidates(v1, i1, v2, i2))
        ex = [jnp.exp(b - best[0]) for b in best]
        den = ex[0]
        for k in range(1, P_TOPK):
            den = den + ex[k]
        inv = 1.0 / den
        e_rows += be
        g_rows += [x * inv for x in ex]
    eidx_ref[...] = (jnp.concatenate(e_rows, axis=0).T * float(WORDS_PER_ROW)).astype(jnp.int32)
    gate_ref[...] = jnp.concatenate(g_rows, axis=0).T


def _diag_mask():
    r = lax.broadcasted_iota(jnp.int32, (8, PEER_SLOTS * 8), 0)
    c = lax.broadcasted_iota(jnp.int32, (8, PEER_SLOTS * 8), 1)
    return (c & 7) == r


def _gather_pair(idx_ref, tab_ref, g_refs, t0):
    views = [idx_ref.at[t0 + u] for u in range(2)]
    for r in range(PEER_SLOTS):
        for view, g_ref in zip(views, g_refs):
            row0 = pl.multiple_of(view[r], WORDS_PER_ROW)
            g_ref[r * WORDS_PER_ROW:(r + 1) * WORDS_PER_ROW, :] = tab_ref[
                pl.ds(row0, WORDS_PER_ROW), :]


def _gather_pipeline(idx_ref, tab_ref, tiles, n_tokens, consume):
    pairs = len(tiles) // 2
    for g_ref in tiles[-2:]:
        g_ref[...] = jnp.zeros(g_ref.shape, g_ref.dtype)

    def trip(i, carry):
        for k in range(pairs):
            pair = pairs * i + k
            prev_tiles = tiles[2 * k - 2:2 * k] if k else tiles[-2:]
            _gather_pair(idx_ref, tab_ref, tiles[2 * k:2 * k + 2], pair * 2)
            for u in range(2):
                consume(prev_tiles[u], jnp.maximum(pair - 1, 0), u)
        return carry

    lax.fori_loop(0, n_tokens // (2 * pairs), trip, 0)
    for u in range(2):
        consume(tiles[-2 + u], n_tokens // 2 - 1, u)


def _peer_act_kernel(idx_ref, hn_ref, tab_ref, a_ref, m_ref, *tiles):
    f32, bf16 = jnp.float32, jnp.bfloat16
    tb = a_ref.shape[0]

    def dots(g_ref, tp, u):
        hp = hn_ref[pl.ds(pl.multiple_of(tp * 16, 16), 16), :]
        m = _nt_dot(hp, pltpu.bitcast(g_ref[...], bf16))
        m_ref[pl.ds(pl.multiple_of((tp * 2 + u) * 8, 8), 8), :] = m[u * 8:(u + 1) * 8, :]

    _gather_pipeline(idx_ref, tab_ref, tiles, tb, dots)
    m3 = m_ref[...].reshape(tb, 8, PEER_SLOTS * 8)
    z = jnp.sum(jnp.where(_diag_mask()[None], m3, 0.0), axis=1)
    rr = lax.broadcasted_iota(jnp.int32, (PEER_SLOTS * 8, PEER_SLOTS), 0)
    cc = lax.broadcasted_iota(jnp.int32, (PEER_SLOTS * 8, PEER_SLOTS), 1)
    pool = jnp.where((rr >> 3) == cc, 1.0, 0.0).astype(bf16)
    z_hi = z.astype(bf16)
    z_lo = (z - z_hi.astype(f32)).astype(bf16)
    a_ref[...] = (jnp.dot(z_hi, pool, preferred_element_type=f32)
                  + jnp.dot(z_lo, pool, preferred_element_type=f32))


def _peer_out_kernel(idx_ref, a_ref, gate_ref, tab_ref, o_ref, w_ref, *tiles):
    f32, bf16 = jnp.float32, jnp.bfloat16
    tb = a_ref.shape[0]
    a = a_ref[...]
    act = 0.5 * a * (1.0 + lax.erf(a * (2.0 ** -0.5)))
    wgt = (gate_ref[...] * act).astype(bf16)
    rr = lax.broadcasted_iota(jnp.int32, (PEER_SLOTS, PEER_SLOTS * 8), 0)
    cc = lax.broadcasted_iota(jnp.int32, (PEER_SLOTS, PEER_SLOTS * 8), 1)
    expand = jnp.where((cc >> 3) == rr, 1.0, 0.0).astype(bf16)
    w_ref[...] = jnp.dot(wgt, expand, preferred_element_type=f32)
    diag = _diag_mask()

    def combine(g_ref, tp, u):
        t = tp * 2 + u
        wrow = jnp.broadcast_to(w_ref[pl.ds(t, 1), :], (8, PEER_SLOTS * 8))
        wsel = jnp.where(diag, wrow, 0.0).astype(bf16)
        o_ref[pl.ds(pl.multiple_of(t * 8, 8), 8), :] = jnp.dot(
            wsel, pltpu.bitcast(g_ref[...], bf16), preferred_element_type=f32)

    _gather_pipeline(idx_ref, tab_ref, tiles, tb, combine)


def _pack_table(tab):
    n_e = tab.shape[0]
    tb = lax.bitcast_convert_type(
        tab.astype(jnp.bfloat16).reshape(n_e, WORDS_PER_ROW, 2, 128), jnp.uint16).astype(jnp.uint32)
    word = tb[:, :, 0, :] | (tb[:, :, 1, :] << 16)
    return lax.bitcast_convert_type(word, jnp.int32).reshape(n_e * WORDS_PER_ROW, 128)


def peer_pallas(x, g_ffn, w_pq, sub_keys, u_emb, v_emb):
    n_tok, d = x.shape
    f32, bf16 = jnp.float32, jnp.bfloat16
    ts, tg = PEER_SCORE_TOKENS, PEER_GATHER_TOKENS
    wpqt = w_pq.reshape(d, P_HEADS * P_QDIM).T.astype(bf16)
    sk = sub_keys.reshape(P_HEADS * 2, N_KEYS, P_QDIM // 2).astype(bf16)
    hn, eidx, gate = pl.pallas_call(
        _peer_score_kernel,
        grid=(n_tok // ts,),
        in_specs=[pl.BlockSpec((ts, d), lambda i: (i, 0)),
                  pl.BlockSpec((1, d), lambda i: (0, 0)),
                  pl.BlockSpec(wpqt.shape, lambda i: (0, 0)),
                  pl.BlockSpec(sk.shape, lambda i: (0, 0, 0))],
        out_specs=[pl.BlockSpec((ts, d), lambda i: (i, 0)),
                   pl.BlockSpec((ts, PEER_SLOTS), lambda i: (i, 0)),
                   pl.BlockSpec((ts, PEER_SLOTS), lambda i: (i, 0))],
        out_shape=[jax.ShapeDtypeStruct((n_tok, d), bf16),
                   jax.ShapeDtypeStruct((n_tok, PEER_SLOTS), jnp.int32),
                   jax.ShapeDtypeStruct((n_tok, PEER_SLOTS), f32)],
        compiler_params=pltpu.CompilerParams(vmem_limit_bytes=VMEM_LIMIT),
        name="peer_score",
    )(x, g_ffn.reshape(1, d), wpqt, sk)
    utab, vtab = _pack_table(u_emb), _pack_table(v_emb)
    idx_spec = pl.BlockSpec((tg, PEER_SLOTS), lambda i: (i, 0), memory_space=pltpu.SMEM)
    tab_spec = pl.BlockSpec(memory_space=pltpu.VMEM)
    slot_spec = pl.BlockSpec((tg, PEER_SLOTS), lambda i: (i, 0))
    row_spec = pl.BlockSpec((tg * 8, 128), lambda i: (i, 0))
    gbuf = pltpu.VMEM((PEER_SLOTS * WORDS_PER_ROW, 128), jnp.int32)
    cparams = pltpu.CompilerParams(vmem_limit_bytes=VMEM_LIMIT)
    act = pl.pallas_call(
        _peer_act_kernel,
        grid=(n_tok // tg,),
        in_specs=[idx_spec, row_spec, tab_spec],
        out_specs=slot_spec,
        out_shape=jax.ShapeDtypeStruct((n_tok, PEER_SLOTS), f32),
        scratch_shapes=[pltpu.VMEM((tg * 8, PEER_SLOTS * 8), f32)] + [gbuf] * PEER_TILES,
        compiler_params=cparams,
        name="peer_act",
    )(eidx, hn.reshape(n_tok * 8, 128), utab)
    out = pl.pallas_call(
        _peer_out_kernel,
        grid=(n_tok // tg,),
        in_specs=[idx_spec, slot_spec, slot_spec, tab_spec],
        out_specs=row_spec,
        out_shape=jax.ShapeDtypeStruct((n_tok * 8, 128), f32),
        scratch_shapes=[pltpu.VMEM((tg, PEER_SLOTS * 8), f32)] + [gbuf] * PEER_TILES,
        compiler_params=cparams,
        name="peer_out",
    )(eidx, act, gate, vtab)
    return out.reshape(n_tok, d)


def _final_kernel(x_ref, y_ref, g_ref, o_ref):
    o_ref[...] = _rms(x_ref[...] + y_ref[...], g_ref[...])


def final_residual_rmsnorm(x, y, g):
    n, d = x.shape
    tm = 512
    row = pl.BlockSpec((tm, d), lambda i: (i, 0))
    return pl.pallas_call(
        _final_kernel,
        grid=(n // tm,),
        in_specs=[row, row, pl.BlockSpec((1, d), lambda i: (0, 0))],
        out_specs=row,
        out_shape=jax.ShapeDtypeStruct((n, d), x.dtype),
        name="final_rmsnorm",
    )(x, y, g.reshape(1, d))


IN_PROJ_TOKENS = 512
GDN_COLS = 3 * B_QK
GATE_ROWS = 8


def _in_proj_kernel(x_ref, g_ref, wa_ref, wkw_ref, wqkv_ref, wz_ref, wab_ref, wabt_ref,
                    cq_ref, ckv_ref, kw_ref, qkv_ref, z_ref, ab_ref, abt_ref):
    f32 = jnp.float32
    hb = _rms(x_ref[0], g_ref[...]).astype(jnp.bfloat16)
    a = jnp.dot(hb, wa_ref[...], preferred_element_type=f32)
    cq_ref[0] = a[:, :A_Q_LORA]
    ckv_ref[0] = a[:, A_Q_LORA:]
    kw_ref[0] = jnp.dot(hb, wkw_ref[...], preferred_element_type=f32)
    qkv_ref[0] = jnp.dot(hb, wqkv_ref[...], preferred_element_type=f32)
    z_ref[0] = jnp.dot(hb, wz_ref[...], preferred_element_type=f32)
    ab_ref[0] = jnp.dot(hb, wab_ref[...], preferred_element_type=f32)
    abt_ref[0] = _nt_dot(wabt_ref[...], hb)


def in_proj_pallas(x, g_mix, w_in):
    bsz, seq, d = x.shape
    f32, bf16 = jnp.float32, jnp.bfloat16
    tm = min(IN_PROJ_TOKENS, seq)
    o = np.cumsum((0,) + COL_WIDTHS)
    wb = w_in.astype(bf16)
    pad_cols = lambda w: jnp.pad(w, ((0, 0), (0, 128 - w.shape[1])))
    wa = wb[:, o[0]:o[2]]
    wkw = pad_cols(wb[:, o[2]:o[4]])
    wqkv = wb[:, o[4]:o[7]]
    wz = wb[:, o[7]:o[8]]
    wab = pad_cols(wb[:, o[8]:o[10]])
    wabt = jnp.concatenate([wb[:, o[8]:o[10]], wb[:, o[3]:o[4]]], axis=1).T
    full = lambda w: pl.BlockSpec(w.shape, lambda b, t: (0, 0))
    tok = lambda w: pl.BlockSpec((1, tm, w), lambda b, t: (b, t, 0))
    shp = lambda w: jax.ShapeDtypeStruct((bsz, seq, w), f32)
    return pl.pallas_call(
        _in_proj_kernel,
        grid=(bsz, seq // tm),
        in_specs=[tok(d), pl.BlockSpec((1, d), lambda b, t: (0, 0)),
                  full(wa), full(wkw), full(wqkv), full(wz), full(wab), full(wabt)],
        out_specs=[tok(A_Q_LORA), tok(A_KV_LORA), tok(128), tok(GDN_COLS), tok(B_QK), tok(128),
                   pl.BlockSpec((1, GATE_ROWS + IDX_HEADS, tm), lambda b, t: (b, 0, t))],
        out_shape=[shp(A_Q_LORA), shp(A_KV_LORA), shp(128), shp(GDN_COLS), shp(B_QK), shp(128),
                   jax.ShapeDtypeStruct((bsz, GATE_ROWS + IDX_HEADS, seq), f32)],
        compiler_params=pltpu.CompilerParams(vmem_limit_bytes=VMEM_LIMIT),
        name="in_proj",
    )(x, g_mix.reshape(1, d), wa, wkw, wqkv, wz, wab, wabt)


def _softplus(x):
    return jnp.maximum(x, 0.0) + jnp.log1p(jnp.exp(-jnp.abs(x)))


def _sigmoid(x):
    return 1.0 / (1.0 + jnp.exp(-x))


def _gdn_gates(pre, a_log, dt_bias, is_decay):
    g = -jnp.exp(a_log) * _softplus(pre + dt_bias)
    return jnp.where(is_decay, g, _sigmoid(pre))


def _gdn_prep_kernel(qkv_ref, halo_ref, cw_ref, ab_ref, abt_ref, alc_ref, dtc_ref, alr_ref, dtr_ref,
                     q_ref, k_ref, v_ref, gc_ref, gr_ref):
    tm = qkv_ref.shape[1]
    x = qkv_ref[0]
    halo = jnp.where(pl.program_id(1) > 0, halo_ref[0], 0.0)
    full = jnp.concatenate([halo, x], axis=0)
    y = x * cw_ref[CONV_WIDTH - 1:CONV_WIDTH, :]
    for back in range(1, CONV_WIDTH):
        shifted = pltpu.roll(full, back, axis=0)[8:, :]
        y = y + shifted * cw_ref[CONV_WIDTH - 1 - back:CONV_WIDTH - back, :]
    y = y * _sigmoid(y)
    for h in range(B_HEADS):
        cols = slice(h * B_HEAD_DIM, (h + 1) * B_HEAD_DIM)
        qh = y[:, h * B_HEAD_DIM:(h + 1) * B_HEAD_DIM]
        kh = y[:, B_QK + h * B_HEAD_DIM:B_QK + (h + 1) * B_HEAD_DIM]
        q_ref[0, :, cols] = qh * lax.rsqrt(
            jnp.sum(qh * qh, axis=-1, keepdims=True) + EPS) * (B_HEAD_DIM ** -0.5)
        k_ref[0, :, cols] = kh * lax.rsqrt(jnp.sum(kh * kh, axis=-1, keepdims=True) + EPS)
    v_ref[0] = y[:, 2 * B_QK:]
    lane = lax.broadcasted_iota(jnp.int32, (tm, 128), 1)
    gates_c = _gdn_gates(ab_ref[0], alc_ref[...], dtc_ref[...], lane < B_HEADS)
    row = lax.broadcasted_iota(jnp.int32, (GATE_ROWS, tm), 0)
    gates_r = _gdn_gates(abt_ref[0], alr_ref[...], dtr_ref[...], row < B_HEADS)
    ti = lax.broadcasted_iota(jnp.int32, (tm, tm), 0)
    tj = lax.broadcasted_iota(jnp.int32, (tm, tm), 1)
    same_chunk = (ti // CHUNK) == (tj // CHUNK)
    hi = lax.Precision.HIGHEST
    lower = jnp.where(same_chunk & (tj <= ti), 1.0, 0.0)
    upper = jnp.where(same_chunk & (ti <= tj), 1.0, 0.0)
    cum_c = jnp.dot(lower, gates_c, preferred_element_type=jnp.float32, precision=hi)
    cum_r = jnp.dot(gates_r, upper, preferred_element_type=jnp.float32, precision=hi)
    gc_ref[0] = jnp.where(lane < B_HEADS, cum_c, gates_c)
    gr_ref[0] = jnp.where(row < B_HEADS, cum_r, gates_r)


def _gdn_main_kernel(q_ref, k_ref, v_ref, z_ref, gc_ref, gr_ref, gon_ref, o_ref, s_ref):
    f32, bf16 = jnp.float32, jnp.bfloat16
    C, Dh = CHUNK, B_HEAD_DIM
    hi = lax.Precision.HIGHEST

    @pl.when(pl.program_id(1) == 0)
    def _():
        s_ref[...] = jnp.zeros(s_ref.shape, f32)

    ii = lax.broadcasted_iota(jnp.int32, (C, C), 0)
    jj = lax.broadcasted_iota(jnp.int32, (C, C), 1)
    causal = ii >= jj
    strict = ii > jj
    eye = jnp.where(ii == jj, 1.0, 0.0)
    mm = lambda a, b: jnp.dot(a, b, preferred_element_type=f32)
    mmh = lambda a, b: jnp.dot(a, b, preferred_element_type=f32, precision=hi)
    n_chunks = q_ref.shape[1] // C
    units = []
    for c in range(n_chunks):
        rows = slice(c * C, (c + 1) * C)
        gates_c = gc_ref[0, rows, :]
        gates_r = gr_ref[0, :, c * C:(c + 1) * C]
        for h in range(B_HEADS):
            cols = slice(h * Dh, (h + 1) * Dh)
            gcum = jnp.broadcast_to(gates_c[:, h:h + 1], (C, Dh))
            beta = jnp.broadcast_to(gates_c[:, B_HEADS + h:B_HEADS + h + 1], (C, Dh))
            gcum_r = jnp.broadcast_to(gates_r[h:h + 1, :], (C, C))
            decay = jnp.where(causal, jnp.exp(jnp.minimum(gcum[:, :C] - gcum_r, 0.0)), 0.0)
            q, k, v = q_ref[0, rows, cols], k_ref[0, rows, cols], v_ref[0, rows, cols]
            qb, kb = q.astype(bf16), k.astype(bf16)
            kk = _nt_dot(kb, kb)
            qk = _nt_dot(qb, kb)
            egc = jnp.exp(gcum)
            g_last = gcum[C - 1:C, :]
            units.append(dict(
                c=c, h=h, rows=rows, cols=cols,
                neg_m=jnp.where(strict, -(beta[:, :C] * kk * decay), 0.0),
                rhs=jnp.concatenate([v * beta, k * (beta * egc)], axis=1),
                q_dec=(q * egc).astype(bf16), intra=(qk * decay).astype(bf16),
                k_tail=(k * jnp.exp(g_last - gcum)).astype(bf16), chunk_decay=jnp.exp(g_last)))
    powers = [un["neg_m"] for un in units]
    t_inv = [eye + p for p in powers]
    for _ in range(5):
        powers = [mmh(p, p) for p in powers]
        t_inv = [t + mmh(p, t) for p, t in zip(powers, t_inv)]
    sols = [mmh(t, un["rhs"]) for t, un in zip(t_inv, units)]
    states = [s_ref[h] for h in range(B_HEADS)]
    for c in range(n_chunks):
        group = [(un, sol) for un, sol in zip(units, sols) if un["c"] == c]
        sbs = [states[un["h"]].astype(bf16) for un, _ in group]
        vbs = [(sol[:, :Dh] - mm(sol[:, Dh:].astype(bf16), sb)).astype(bf16)
               for (un, sol), sb in zip(group, sbs)]
        outs = [mm(un["q_dec"], sb) + mm(un["intra"], vb)
                for (un, _), sb, vb in zip(group, sbs, vbs)]
        for (un, _), vb in zip(group, vbs):
            states[un["h"]] = states[un["h"]] * un["chunk_decay"] + lax.dot_general(
                un["k_tail"], vb, (((0,), (0,)), ((), ())), preferred_element_type=f32)
        for (un, _), o in zip(group, outs):
            o = o * lax.rsqrt(jnp.mean(o * o, axis=-1, keepdims=True) + EPS) * gon_ref[...]
            zz = z_ref[0, un["rows"], un["cols"]]
            o_ref[0, un["rows"], un["cols"]] = o * (zz * _sigmoid(zz))
    for h in range(B_HEADS):
        s_ref[h] = states[h]


def gdn_pallas(qkv, z, ab, abt, conv_w, a_log, dt_bias, g_onorm):
    bsz, seq, _ = qkv.shape
    f32 = jnp.float32
    tm = min(256, seq)
    zero4 = jnp.zeros((B_HEADS,), f32)
    lane_row = lambda v: jnp.pad(jnp.concatenate([v.astype(f32), zero4]), (0, 120)).reshape(1, 128)
    sub_col = lambda v: jnp.concatenate([v.astype(f32), zero4]).reshape(GATE_ROWS, 1)
    tok = lambda w, t=tm: pl.BlockSpec((1, t, w), lambda b, i: (b, i, 0))
    const = lambda shape: pl.BlockSpec(shape, lambda b, i: (0,) * len(shape))
    shp = lambda w: jax.ShapeDtypeStruct((bsz, seq, w), f32)
    q, k, v, gc, gr = pl.pallas_call(
        _gdn_prep_kernel,
        grid=(bsz, seq // tm),
        in_specs=[tok(GDN_COLS),
                  pl.BlockSpec((1, 8, GDN_COLS), lambda b, i: (b, jnp.maximum(i * (tm // 8) - 1, 0), 0)),
                  const((CONV_WIDTH, GDN_COLS)), tok(128),
                  pl.BlockSpec((1, GATE_ROWS, tm), lambda b, i: (b, 0, i)),
                  const((1, 128)), const((1, 128)), const((GATE_ROWS, 1)), const((GATE_ROWS, 1))],
        out_specs=[tok(B_QK), tok(B_QK), tok(B_QK), tok(128),
                   pl.BlockSpec((1, GATE_ROWS, tm), lambda b, i: (b, 0, i))],
        out_shape=[shp(B_QK), shp(B_QK), shp(B_QK), shp(128),
                   jax.ShapeDtypeStruct((bsz, GATE_ROWS, seq), f32)],
        compiler_params=pltpu.CompilerParams(vmem_limit_bytes=VMEM_LIMIT),
        name="gdn_prep",
    )(qkv, qkv, conv_w.astype(f32), ab, abt, lane_row(a_log), lane_row(dt_bias),
      sub_col(a_log), sub_col(dt_bias))
    ts = 2 * CHUNK
    return pl.pallas_call(
        _gdn_main_kernel,
        grid=(bsz, seq // ts),
        in_specs=[tok(B_QK, ts), tok(B_QK, ts), tok(B_QK, ts), tok(B_QK, ts), tok(128, ts),
                  pl.BlockSpec((1, GATE_ROWS, ts), lambda b, i: (b, 0, i)),
                  const((1, B_HEAD_DIM))],
        out_specs=tok(B_QK, ts),
        out_shape=shp(B_QK),
        scratch_shapes=[pltpu.VMEM((B_HEADS, B_HEAD_DIM, B_HEAD_DIM), f32)],
        compiler_params=pltpu.CompilerParams(dimension_semantics=("arbitrary", "arbitrary")),
        name="gdn_main",
    )(q, k, v, z, gc, gr, g_onorm.astype(f32).reshape(1, B_HEAD_DIM))


def _mem_kv_kernel(mem_ref, g_ref, wk_ref, wv_ref, k_ref, v_ref):
    f32, bf16 = jnp.float32, jnp.bfloat16
    mn = _rms(mem_ref[0], g_ref[...]).astype(bf16)
    k = jnp.dot(mn, wk_ref[...], preferred_element_type=f32)
    v = jnp.dot(mn, wv_ref[...], preferred_element_type=f32)
    for h in range(X_HEADS):
        cols = slice(h * X_HEAD_DIM, (h + 1) * X_HEAD_DIM)
        k_ref[0, h] = k[:, cols].astype(bf16)
        v_ref[0, h] = v[:, cols].astype(bf16)


def _mid_kernel(x_ref, oa_ref, ob_ref, wo_ref, gx_ref, wq_ref, k_ref, v_ref, wox_ref, o_ref):
    f32, bf16 = jnp.float32, jnp.bfloat16
    na = oa_ref.shape[2]
    x1 = (x_ref[0]
          + jnp.dot(oa_ref[0].astype(bf16), wo_ref[:na, :], preferred_element_type=f32)
          + jnp.dot(ob_ref[0].astype(bf16), wo_ref[na:, :], preferred_element_type=f32))
    hq = _rms(x1, gx_ref[...]).astype(bf16)
    q = jnp.dot(hq, wq_ref[...], preferred_element_type=f32)
    heads = []
    for h in range(X_HEADS):
        qh = q[:, h * X_HEAD_DIM:(h + 1) * X_HEAD_DIM].astype(bf16)
        lg = _nt_dot(qh, k_ref[0, h]) * (X_HEAD_DIM ** -0.5)
        p = jnp.exp(lg - jnp.max(lg, axis=-1, keepdims=True))
        p = (p / jnp.sum(p, axis=-1, keepdims=True)).astype(bf16)
        heads.append(jnp.dot(p, v_ref[0, h], preferred_element_type=f32).astype(bf16))
    o = jnp.concatenate(heads, axis=1)
    o_ref[0] = x1 + jnp.dot(o, wox_ref[...], preferred_element_type=f32)


def mid_pallas(x, o_a, o_b, w_out, g_cross, mem, g_mem, wq, wk, wv, wo):
    bsz, seq, d = x.shape
    f32, bf16 = jnp.float32, jnp.bfloat16
    hx = X_HEADS * X_HEAD_DIM
    m_len = mem.shape[1]
    const2 = lambda shape: pl.BlockSpec(shape, lambda b: (0,) * len(shape))
    kv_spec = pl.BlockSpec((1, X_HEADS, m_len, X_HEAD_DIM), lambda b: (b, 0, 0, 0))
    kv_shape = jax.ShapeDtypeStruct((bsz, X_HEADS, m_len, X_HEAD_DIM), bf16)
    k, v = pl.pallas_call(
        _mem_kv_kernel,
        grid=(bsz,),
        in_specs=[pl.BlockSpec((1, m_len, d), lambda b: (b, 0, 0)), const2((1, d)),
                  const2((d, hx)), const2((d, hx))],
        out_specs=[kv_spec, kv_spec],
        out_shape=[kv_shape, kv_shape],
        name="mem_kv",
    )(mem, g_mem.reshape(1, d), wk.reshape(d, hx).astype(bf16), wv.reshape(d, hx).astype(bf16))
    tm = min(256, seq)
    tok = lambda w: pl.BlockSpec((1, tm, w), lambda b, t: (b, t, 0))
    const = lambda shape: pl.BlockSpec(shape, lambda b, t: (0,) * len(shape))
    kv_spec2 = pl.BlockSpec((1, X_HEADS, m_len, X_HEAD_DIM), lambda b, t: (b, 0, 0, 0))
    return pl.pallas_call(
        _mid_kernel,
        grid=(bsz, seq // tm),
        in_specs=[tok(d), tok(o_a.shape[2]), tok(o_b.shape[2]), const((MIX_WIDTH, d)),
                  const((1, d)), const((d, hx)), kv_spec2, kv_spec2, const((hx, d))],
        out_specs=tok(d),
        out_shape=jax.ShapeDtypeStruct((bsz, seq, d), f32),
        compiler_params=pltpu.CompilerParams(vmem_limit_bytes=VMEM_LIMIT),
        name="mid",
    )(x, o_a, o_b, w_out.astype(bf16), g_cross.reshape(1, d), wq.reshape(d, hx).astype(bf16),
      k, v, wo.reshape(hx, d).astype(bf16))


def kernel(x, mem, g_mix, w_in, g_cq, g_ckv, g_kidx, w_uq, w_iq, w_uk, w_uv, rel_bias, conv_w, A_log, dt_bias, g_onorm, w_out, g_cross, g_mem, wq_x, wk_x, wv_x, wo_x, g_ffn, w_pq, sub_keys, u_emb, v_emb, g_final):
    bsz, seq, d = x.shape
    for l in range(DEPTH):
        cq, ckv, kw, qkv, z, ab, abt = in_proj_pallas(x, g_mix[l], w_in[l])
        o_a = dsa_pallas_t(cq, ckv, kw, abt, g_cq[l], g_ckv[l], g_kidx[l],
                           w_uq[l], w_iq[l], w_uk[l], w_uv[l], rel_bias)
        o_b = gdn_pallas(qkv, z, ab, abt, conv_w[l], A_log[l], dt_bias[l], g_onorm[l])
        x = mid_pallas(x, o_a, o_b, w_out[l], g_cross[l], mem, g_mem[l],
                       wq_x[l], wk_x[l], wv_x[l], wo_x[l])
        xf = x.reshape(bsz * seq, d)
        y = peer_pallas(xf, g_ffn[l], w_pq[l], sub_keys[l], u_emb[l], v_emb[l])
        if l + 1 < DEPTH:
            x = (xf + y).reshape(bsz, seq, d)
    return final_residual_rmsnorm(xf, y, g_final).reshape(bsz, seq, d)
```

```python
import math
from functools import partial
import jax
import jax.numpy as jnp
from jax import lax
import numpy as np
from jax.experimental import pallas as pl
from jax.experimental.pallas import tpu as pltpu

DEPTH = 1

CHUNK = 64
Q_BLOCK = 128
EPS = 1e-6

A_HEADS = 8
A_HEAD_DIM = 64
A_Q_LORA = 256
A_KV_LORA = 256
IDX_HEADS = 8
IDX_DIM = 64
IDX_TOPK_MAX = 256
ATTN_SCALE = A_HEAD_DIM ** -0.5
IDX_SCALE = (IDX_HEADS * IDX_DIM) ** -0.5

B_HEADS = 4
B_HEAD_DIM = 128
B_QK = B_HEADS * B_HEAD_DIM
CONV_WIDTH = 4

REL_BUCKETS = 32
REL_MAX_DIST = 128

X_HEADS = 4
X_HEAD_DIM = 128

P_HEADS = 8
N_KEYS = 128
N_EXPERTS = N_KEYS * N_KEYS
P_TOPK = 16
P_QDIM = 256

COL_WIDTHS = (A_Q_LORA, A_KV_LORA, IDX_DIM, IDX_HEADS, B_QK, B_QK, B_QK, B_QK, B_HEADS, B_HEADS)
MIX_WIDTH = A_HEADS * A_HEAD_DIM + B_HEADS * B_HEAD_DIM


def t5_bucket(rel):
    half = REL_BUCKETS // 2
    max_exact = half // 2
    n = jnp.abs(rel)
    nf = jnp.maximum(n, max_exact).astype(jnp.float32)
    large = max_exact + (jnp.log(nf / max_exact) / math.log(REL_MAX_DIST / max_exact)
                         * (half - max_exact)).astype(jnp.int32)
    large = jnp.minimum(large, half - 1)
    return jnp.where(rel > 0, half, 0) + jnp.where(n < max_exact, n, large)


INT_MIN = -2147483648
NEG_BIG = -1e30
KEY_TILE = 512
KEY_PAD = KEY_TILE - Q_BLOCK
VMEM_LIMIT = 56 * 1024 * 1024


def _rms(x, g):
    return x * lax.rsqrt(jnp.mean(x * x, axis=-1, keepdims=True) + EPS) * g


def _nt_dot(a, b):
    return lax.dot_general(a, b, (((1,), (1,)), ((), ())), preferred_element_type=jnp.float32)


def _dsa_prep_kernel(cq_ref, ckv_ref, kw_ref, gcq_ref, gckv_ref, gk_ref,
                       wiqt_ref, wuqt_ref, wuk_ref,
                       at_ref, qlatt_ref, ckvn_ref, kidxn_ref):
    f32, bf16 = jnp.float32, jnp.bfloat16
    Q = Q_BLOCK
    kw = kw_ref[0]
    cqn = _rms(cq_ref[0], gcq_ref[...]).astype(bf16)
    qit = _nt_dot(wiqt_ref[...], cqn)
    qt = _nt_dot(wuqt_ref[...], cqn)
    for h in range(A_HEADS):
        cols = slice(h * Q, (h + 1) * Q)
        at_ref[0, 0, :, cols] = qit[h * IDX_DIM:(h + 1) * IDX_DIM, :].astype(bf16)
        qh = qt[h * A_HEAD_DIM:(h + 1) * A_HEAD_DIM, :].astype(bf16)
        ql = jnp.dot(wuk_ref[h], qh, preferred_element_type=f32) * ATTN_SCALE
        qlatt_ref[0, 0, :, cols] = ql.astype(bf16)
    ckvn_ref[0] = _rms(ckv_ref[0], gckv_ref[...]).astype(bf16)
    kidxn_ref[0] = _rms(kw[:, :IDX_DIM], gk_ref[...]).astype(bf16)


def _dsa_main_kernel(at_ref, qlatt_ref, wt_ref, kidx_ref, ckv_ref, bias0_ref, wuvt_ref, o_ref,
                       sc_ref, big_ref, acc_ref, m_ref, l_ref, *, topk):
    f32, i32, bf16 = jnp.float32, jnp.int32, jnp.bfloat16
    Q, T = Q_BLOCK, KEY_TILE
    i = pl.program_id(1)
    e = (i + 1) * Q
    nt = (i + T // Q) // (T // Q)
    kf = jnp.float32(topk)
    wsc = wt_ref[0] * IDX_SCALE
    sub = lax.broadcasted_iota(i32, (T, Q), 0)
    qlane = lax.broadcasted_iota(i32, (T, Q), 1)
    limit = ((i * Q + qlane) // CHUNK + 1) * CHUNK
    at = at_ref[0, 0]

    def tile_start(j):
        return pl.multiple_of(e + KEY_PAD - (j + 1) * T, 128)

    def key_pos(j):
        return tile_start(j) - KEY_PAD + sub

    def score_tile(j, carry):
        kid = kidx_ref[0, pl.ds(tile_start(j), T), :]
        big_ref[...] = jnp.dot(kid, at, preferred_element_type=f32)
        s = jnp.zeros((T, Q), f32)
        for h in range(IDX_HEADS):
            s = s + wsc[h:h + 1, :] * jnp.maximum(big_ref[:, h * Q:(h + 1) * Q], 0.0)
        bits = lax.bitcast_convert_type(s, i32)
        key = jnp.where(bits < 0, bits ^ jnp.int32(0x7FFFFFFF), bits)
        key = jnp.where(s == 0.0, 0, key)
        kpos = key_pos(j)
        key = jnp.where(kpos >= 0, key, INT_MIN)
        sc_ref[j] = jnp.where(kpos < limit, key, INT_MIN)
        return carry

    lax.fori_loop(0, nt, score_tile, 0)

    def count_where(pred):
        def body(j, cnt):
            hit = jnp.where(pred(sc_ref[j], key_pos(j)), 1.0, 0.0)
            return cnt + jnp.sum(hit.reshape(T // 64, 64, Q), axis=0)
        cnt = lax.fori_loop(0, nt, body, jnp.zeros((64, Q), f32))
        return jnp.sum(cnt, axis=0, keepdims=True)

    def bit_body(b, carry):
        u, cacc = carry
        cand = u | lax.shift_left(jnp.int32(1), 31 - b)
        tvec = cand ^ jnp.int32(INT_MIN)
        tot = count_where(lambda k, kpos: k >= tvec)
        ok = tot >= kf
        return jnp.where(ok, cand, u), jnp.where(ok, tot, cacc)

    u, cacc = lax.fori_loop(0, 32, bit_body, (jnp.zeros((1, Q), i32), jnp.zeros((1, Q), f32)))
    thr = jnp.maximum(u ^ jnp.int32(INT_MIN), INT_MIN + 1)
    overflow = jnp.where(u != 0, cacc, 0.0) > kf
    n_over = jnp.max(jnp.where(overflow, 1.0, 0.0), axis=1, keepdims=True)[0, 0]

    @pl.when(n_over > 0.0)
    def _():
        need = kf - count_where(lambda k, kpos: k > thr)

        def cut_body(b, cut):
            cand = cut | lax.shift_left(jnp.int32(1), 14 - b)
            cnt = count_where(lambda k, kpos: jnp.where(k == thr, kpos, cand) < cand)
            return jnp.where(cnt <= need, cand, cut)

        cut = lax.fori_loop(0, 15, cut_body, jnp.zeros((1, Q), i32))

        def drop_tile(j, carry):
            k = sc_ref[j]
            drop = jnp.where(k == thr, key_pos(j), -1) >= cut
            sc_ref[j] = jnp.where(drop, INT_MIN, k)
            return carry

        lax.fori_loop(0, nt, drop_tile, 0)

    m_ref[...] = jnp.full(m_ref.shape, NEG_BIG, f32)
    l_ref[...] = jnp.zeros(l_ref.shape, f32)
    acc_ref[...] = jnp.zeros(acc_ref.shape, f32)
    qlatt = qlatt_ref[0, 0]

    def attn_tile(j, with_bias):
        kv = ckv_ref[0, pl.ds(tile_start(j), T), :]
        kvt = kv.T
        mask_add = jnp.where(sc_ref[j] >= thr, 0.0, NEG_BIG)
        pair_cols = [slice(g * 2 * Q, (g + 1) * 2 * Q) for g in range(A_HEADS // 2)]
        logits = lambda g: jnp.dot(kv, qlatt[:, pair_cols[g]], preferred_element_type=f32)
        x_next = logits(0)
        for g in range(A_HEADS // 2):
            x_pair = x_next
            if g + 1 < A_HEADS // 2:
                x_next = logits(g + 1)
            ps, alphas = [], []
            for u in range(2):
                h = 2 * g + u
                cols = slice(h * Q, (h + 1) * Q)
                x = x_pair[:, u * Q:(u + 1) * Q] + mask_add
                if with_bias:
                    x = x + bias0_ref[h]
                m_prev = m_ref[:, cols]
                m_new = jnp.maximum(m_prev, jnp.max(x, axis=0, keepdims=True))
                p = jnp.exp(x - m_new)
                alpha = jnp.exp(m_prev - m_new)
                l_ref[:, cols] = alpha * l_ref[:, cols] + jnp.sum(p, axis=0, keepdims=True)
                m_ref[:, cols] = m_new
                ps.append(p.astype(bf16))
                alphas.append(alpha)
            pv = jnp.dot(kvt, jnp.concatenate(ps, axis=1), preferred_element_type=f32)
            acc_ref[:, pair_cols[g]] = (
                jnp.concatenate(alphas, axis=1) * acc_ref[:, pair_cols[g]] + pv)

    attn_tile(0, True)

    def attn_body(j, carry):
        attn_tile(j, False)
        return carry

    lax.fori_loop(1, nt, attn_body, 0)

    inv_l = 1.0 / l_ref[...]
    outs = []
    for h in range(A_HEADS):
        cols = slice(h * Q, (h + 1) * Q)
        o_lat_t = (acc_ref[:, cols] * inv_l[:, cols]).astype(bf16)
        outs.append(jnp.dot(wuvt_ref[h], o_lat_t, preferred_element_type=f32))
    o_ref[0] = jnp.concatenate(outs, axis=0).T


def dsa_pallas(cq, ckv, kw, widx_t, g_cq, g_ckv, g_kidx, w_uq, w_iq, w_uk, w_uv, rel_bias):
    bsz, seq, _ = cq.shape
    f32, bf16 = jnp.float32, jnp.bfloat16
    Q, T, H = Q_BLOCK, KEY_TILE, A_HEADS
    nblk = seq // Q
    topk = min(IDX_TOPK_MAX, seq // 4)
    wiqt = w_iq.reshape(A_Q_LORA, IDX_HEADS * IDX_DIM).T.astype(bf16)
    wuqt = w_uq.reshape(A_Q_LORA, H * A_HEAD_DIM).T.astype(bf16)
    wuk = jnp.transpose(w_uk, (1, 0, 2)).astype(bf16)
    wuvt = jnp.transpose(w_uv, (1, 2, 0)).astype(bf16)
    tok = lambda w: pl.BlockSpec((1, Q, w), lambda b, t: (b, t, 0))
    full = lambda shape: pl.BlockSpec(shape, lambda b, t: (0,) * len(shape))
    blk = lambda r: pl.BlockSpec((1, 1, r, H * Q), lambda b, t: (b, t, 0, 0))
    a_t, qlat_t, ckvn, kidxn = pl.pallas_call(
        _dsa_prep_kernel,
        grid=(bsz, nblk),
        in_specs=[tok(A_Q_LORA), tok(A_KV_LORA), tok(128),
                  full((1, A_Q_LORA)), full((1, A_KV_LORA)), full((1, IDX_DIM)),
                  full(wiqt.shape), full(wuqt.shape), full(wuk.shape)],
        out_specs=[blk(IDX_DIM), blk(A_KV_LORA), tok(A_KV_LORA), tok(IDX_DIM)],
        out_shape=[jax.ShapeDtypeStruct((bsz, nblk, IDX_DIM, H * Q), bf16),
                   jax.ShapeDtypeStruct((bsz, nblk, A_KV_LORA, H * Q), bf16),
                   jax.ShapeDtypeStruct((bsz, seq, A_KV_LORA), bf16),
                   jax.ShapeDtypeStruct((bsz, seq, IDX_DIM), bf16)],
        name="dsa_prep",
    )(cq, ckv, kw, g_cq.reshape(1, -1), g_ckv.reshape(1, -1), g_kidx.reshape(1, -1),
      wiqt, wuqt, wuk)
    ckvp = jnp.pad(ckvn, ((0, 0), (KEY_PAD, 0), (0, 0)))
    kidxp = jnp.pad(kidxn, ((0, 0), (KEY_PAD, 0), (0, 0)))
    rel = (jnp.arange(T, dtype=jnp.int32)[:, None] - jnp.arange(Q, dtype=jnp.int32)[None, :]
           - KEY_PAD)
    far = rel_bias[t5_bucket(jnp.int32(-KEY_PAD - 1))]
    bias0 = jnp.transpose(rel_bias[t5_bucket(rel)] - far, (2, 0, 1)).astype(f32)
    nt_max = (nblk - 1 + T // Q) // (T // Q)
    skey = seq + KEY_PAD
    return pl.pallas_call(
        partial(_dsa_main_kernel, topk=topk),
        grid=(bsz, nblk),
        in_specs=[blk(IDX_DIM), blk(A_KV_LORA),
                  pl.BlockSpec((1, IDX_HEADS, Q), lambda b, t: (b, 1, t)),
                  pl.BlockSpec((1, skey, IDX_DIM), lambda b, t: (b, 0, 0)),
                  pl.BlockSpec((1, skey, A_KV_LORA), lambda b, t: (b, 0, 0)),
                  full(bias0.shape), full(wuvt.shape)],
        out_specs=tok(H * A_HEAD_DIM),
        out_shape=jax.ShapeDtypeStruct((bsz, seq, H * A_HEAD_DIM), f32),
        scratch_shapes=[pltpu.VMEM((nt_max, T, Q), jnp.int32),
                        pltpu.VMEM((T, H * Q), f32),
                        pltpu.VMEM((A_KV_LORA, H * Q), f32),
                        pltpu.VMEM((1, H * Q), f32),
                        pltpu.VMEM((1, H * Q), f32)],
        compiler_params=pltpu.CompilerParams(
            dimension_semantics=("arbitrary", "arbitrary"), vmem_limit_bytes=VMEM_LIMIT),
        name="dsa_main",
    )(a_t, qlat_t, widx_t, kidxp, ckvp, bias0, wuvt)


PEER_SCORE_TOKENS = 256
PEER_GATHER_TOKENS = 64
PEER_SLOTS = P_HEADS * P_TOPK
WORDS_PER_ROW = 4
PEER_TILES = 8


def _top16(s, order=None, payload=None):
    if order is None:
        order = lax.broadcasted_iota(jnp.int32, s.shape, 0).astype(jnp.float32)
    vals, picks = [], []
    for _ in range(P_TOPK):
        m = jnp.max(s, axis=0, keepdims=True)
        pos = jnp.min(jnp.where(s == m, order, float(N_EXPERTS)), axis=0, keepdims=True)
        hit = order == pos
        vals.append(m)
        if payload is None:
            picks.append(pos)
        else:
            picks.append(jnp.max(jnp.where(hit, payload, -1.0), axis=0, keepdims=True))
        s = jnp.where(hit, -jnp.inf, s)
    return vals, picks


def _staircase_candidates(v1, i1, v2, i2):
    v1m, i1m = jnp.concatenate(v1, axis=0), jnp.concatenate(i1, axis=0)
    v2m, i2m = jnp.concatenate(v2, axis=0), jnp.concatenate(i2, axis=0)
    t = v1m.shape[1]
    sub8 = lax.broadcasted_iota(jnp.int32, (8, t), 0).astype(jnp.float32)
    cand, cidx, rank = [], [], []
    for a in range(8):
        cand.append(v1[a] + v2m[:8])
        cidx.append(i1[a] * float(N_KEYS) + i2m[:8])
        rank.append(sub8 + float(a * P_TOPK))
    cand.append(v1[0] + v2m[8:])
    cidx.append(i1[0] * float(N_KEYS) + i2m[8:])
    rank.append(sub8 + 8.0)
    cand.append(v1m[8:] + v2[0])
    cidx.append(i1m[8:] * float(N_KEYS) + i2[0])
    rank.append((sub8 + 8.0) * float(P_TOPK))
    cat = lambda xs: jnp.concatenate(xs, axis=0)
    return cat(cand), cat(rank), cat(cidx)


def _peer_score_kernel(x_ref, g_ref, wpqt_ref, sk_ref, hn_ref, eidx_ref, gate_ref):
    f32, bf16 = jnp.float32, jnp.bfloat16
    hb = _rms(x_ref[...], g_ref[...]).astype(bf16)
    hn_ref[...] = hb
    qrt = _nt_dot(wpqt_ref[...], hb)
    half = P_QDIM // 2
    e_rows, g_rows = [], []
    for hd in range(P_HEADS):
        tops = []
        for p in range(2):
            qhp = qrt[(hd * 2 + p) * half:(hd * 2 + p + 1) * half, :].astype(bf16)
            s = jnp.dot(sk_ref[hd * 2 + p], qhp, preferred_element_type=f32)
            tops.append(_top16(s))
        (v1, i1), (v2, i2) = tops
        best, be = _top16(*_staircase_candidates(v1, i1, v2, i2))
        ex = [jnp.exp(b - best[0]) for b in best]
        den = ex[0]
        for k in range(1, P_TOPK):
            den = den + ex[k]
        inv = 1.0 / den
        e_rows += be
        g_rows += [x * inv for x in ex]
    eidx_ref[...] = (jnp.concatenate(e_rows, axis=0).T * float(WORDS_PER_ROW)).astype(jnp.int32)
    gate_ref[...] = jnp.concatenate(g_rows, axis=0).T


def _diag_mask():
    r = lax.broadcasted_iota(jnp.int32, (8, PEER_SLOTS * 8), 0)
    c = lax.broadcasted_iota(jnp.int32, (8, PEER_SLOTS * 8), 1)
    return (c & 7) == r


def _gather_pair(idx_ref, tab_ref, g_refs, t0):
    views = [idx_ref.at[t0 + u] for u in range(2)]
    for r in range(PEER_SLOTS):
        for view, g_ref in zip(views, g_refs):
            row0 = pl.multiple_of(view[r], WORDS_PER_ROW)
            g_ref[r * WORDS_PER_ROW:(r + 1) * WORDS_PER_ROW, :] = tab_ref[
                pl.ds(row0, WORDS_PER_ROW), :]


def _gather_pipeline(idx_ref, tab_ref, tiles, n_tokens, consume):
    pairs = len(tiles) // 2
    for g_ref in tiles[-2:]:
        g_ref[...] = jnp.zeros(g_ref.shape, g_ref.dtype)

    def trip(i, carry):
        for k in range(pairs):
            pair = pairs * i + k
            prev_tiles = tiles[2 * k - 2:2 * k] if k else tiles[-2:]
            _gather_pair(idx_ref, tab_ref, tiles[2 * k:2 * k + 2], pair * 2)
            for u in range(2):
                consume(prev_tiles[u], jnp.maximum(pair - 1, 0), u)
        return carry

    lax.fori_loop(0, n_tokens // (2 * pairs), trip, 0)
    for u in range(2):
        consume(tiles[-2 + u], n_tokens // 2 - 1, u)


def _peer_act_kernel(idx_ref, hn_ref, tab_ref, a_ref, m_ref, *tiles):
    f32, bf16 = jnp.float32, jnp.bfloat16
    tb = a_ref.shape[0]

    def dots(g_ref, tp, u):
        hp = hn_ref[pl.ds(pl.multiple_of(tp * 16, 16), 16), :]
        m = _nt_dot(hp, pltpu.bitcast(g_ref[...], bf16))
        m_ref[pl.ds(pl.multiple_of((tp * 2 + u) * 8, 8), 8), :] = m[u * 8:(u + 1) * 8, :]

    _gather_pipeline(idx_ref, tab_ref, tiles, tb, dots)
    m3 = m_ref[...].reshape(tb, 8, PEER_SLOTS * 8)
    z = jnp.sum(jnp.where(_diag_mask()[None], m3, 0.0), axis=1)
    rr = lax.broadcasted_iota(jnp.int32, (PEER_SLOTS * 8, PEER_SLOTS), 0)
    cc = lax.broadcasted_iota(jnp.int32, (PEER_SLOTS * 8, PEER_SLOTS), 1)
    pool = jnp.where((rr >> 3) == cc, 1.0, 0.0).astype(bf16)
    z_hi = z.astype(bf16)
    z_lo = (z - z_hi.astype(f32)).astype(bf16)
    a_ref[...] = (jnp.dot(z_hi, pool, preferred_element_type=f32)
                  + jnp.dot(z_lo, pool, preferred_element_type=f32))


def _peer_out_kernel(idx_ref, a_ref, gate_ref, tab_ref, o_ref, w_ref, *tiles):
    f32, bf16 = jnp.float32, jnp.bfloat16
    tb = a_ref.shape[0]
    a = a_ref[...]
    act = 0.5 * a * (1.0 + lax.erf(a * (2.0 ** -0.5)))
    wgt = (gate_ref[...] * act).astype(bf16)
    rr = lax.broadcasted_iota(jnp.int32, (PEER_SLOTS, PEER_SLOTS * 8), 0)
    cc = lax.broadcasted_iota(jnp.int32, (PEER_SLOTS, PEER_SLOTS * 8), 1)
    expand = jnp.where((cc >> 3) == rr, 1.0, 0.0).astype(bf16)
    w_ref[...] = jnp.dot(wgt, expand, preferred_element_type=f32)
    diag = _diag_mask()

    def combine(g_ref, tp, u):
        t = tp * 2 + u
        wrow = jnp.broadcast_to(w_ref[pl.ds(t, 1), :], (8, PEER_SLOTS * 8))
        wsel = jnp.where(diag, wrow, 0.0).astype(bf16)
        o_ref[pl.ds(pl.multiple_of(t * 8, 8), 8), :] = jnp.dot(
            wsel, pltpu.bitcast(g_ref[...], bf16), preferred_element_type=f32)

    _gather_pipeline(idx_ref, tab_ref, tiles, tb, combine)


def _pack_table(tab):
    n_e = tab.shape[0]
    tb = lax.bitcast_convert_type(
        tab.astype(jnp.bfloat16).reshape(n_e, WORDS_PER_ROW, 2, 128), jnp.uint16).astype(jnp.uint32)
    word = tb[:, :, 0, :] | (tb[:, :, 1, :] << 16)
    return lax.bitcast_convert_type(word, jnp.int32).reshape(n_e * WORDS_PER_ROW, 128)


def peer_pallas(x, g_ffn, w_pq, sub_keys, u_emb, v_emb):
    n_tok, d = x.shape
    f32, bf16 = jnp.float32, jnp.bfloat16
    ts, tg = PEER_SCORE_TOKENS, PEER_GATHER_TOKENS
    wpqt = w_pq.reshape(d, P_HEADS * P_QDIM).T.astype(bf16)
    sk = sub_keys.reshape(P_HEADS * 2, N_KEYS, P_QDIM // 2).astype(bf16)
    hn, eidx, gate = pl.pallas_call(
        _peer_score_kernel,
        grid=(n_tok // ts,),
        in_specs=[pl.BlockSpec((ts, d), lambda i: (i, 0)),
                  pl.BlockSpec((1, d), lambda i: (0, 0)),
                  pl.BlockSpec(wpqt.shape, lambda i: (0, 0)),
                  pl.BlockSpec(sk.shape, lambda i: (0, 0, 0))],
        out_specs=[pl.BlockSpec((ts, d), lambda i: (i, 0)),
                   pl.BlockSpec((ts, PEER_SLOTS), lambda i: (i, 0)),
                   pl.BlockSpec((ts, PEER_SLOTS), lambda i: (i, 0))],
        out_shape=[jax.ShapeDtypeStruct((n_tok, d), bf16),
                   jax.ShapeDtypeStruct((n_tok, PEER_SLOTS), jnp.int32),
                   jax.ShapeDtypeStruct((n_tok, PEER_SLOTS), f32)],
        compiler_params=pltpu.CompilerParams(vmem_limit_bytes=VMEM_LIMIT),
        name="peer_score",
    )(x, g_ffn.reshape(1, d), wpqt, sk)
    utab, vtab = _pack_table(u_emb), _pack_table(v_emb)
    idx_spec = pl.BlockSpec((tg, PEER_SLOTS), lambda i: (i, 0), memory_space=pltpu.SMEM)
    tab_spec = pl.BlockSpec(memory_space=pltpu.VMEM)
    slot_spec = pl.BlockSpec((tg, PEER_SLOTS), lambda i: (i, 0))
    row_spec = pl.BlockSpec((tg * 8, 128), lambda i: (i, 0))
    gbuf = pltpu.VMEM((PEER_SLOTS * WORDS_PER_ROW, 128), jnp.int32)
    cparams = pltpu.CompilerParams(vmem_limit_bytes=VMEM_LIMIT)
    act = pl.pallas_call(
        _peer_act_kernel,
        grid=(n_tok // tg,),
        in_specs=[idx_spec, row_spec, tab_spec],
        out_specs=slot_spec,
        out_shape=jax.ShapeDtypeStruct((n_tok, PEER_SLOTS), f32),
        scratch_shapes=[pltpu.VMEM((tg * 8, PEER_SLOTS * 8), f32)] + [gbuf] * PEER_TILES,
        compiler_params=cparams,
        name="peer_act",
    )(eidx, hn.reshape(n_tok * 8, 128), utab)
    out = pl.pallas_call(
        _peer_out_kernel,
        grid=(n_tok // tg,),
        in_specs=[idx_spec, slot_spec, slot_spec, tab_spec],
        out_specs=row_spec,
        out_shape=jax.ShapeDtypeStruct((n_tok * 8, 128), f32),
        scratch_shapes=[pltpu.VMEM((tg, PEER_SLOTS * 8), f32)] + [gbuf] * PEER_TILES,
        compiler_params=cparams,
        name="peer_out",
    )(eidx, act, gate, vtab)
    return out.reshape(n_tok, d)


def _final_kernel(x_ref, y_ref, g_ref, o_ref):
    o_ref[...] = _rms(x_ref[...] + y_ref[...], g_ref[...])


def final_residual_rmsnorm(x, y, g):
    n, d = x.shape
    tm = 512
    row = pl.BlockSpec((tm, d), lambda i: (i, 0))
    return pl.pallas_call(
        _final_kernel,
        grid=(n // tm,),
        in_specs=[row, row, pl.BlockSpec((1, d), lambda i: (0, 0))],
        out_specs=row,
        out_shape=jax.ShapeDtypeStruct((n, d), x.dtype),
        name="final_rmsnorm",
    )(x, y, g.reshape(1, d))


IN_PROJ_TOKENS = 512
GDN_COLS = 3 * B_QK
GATE_ROWS = 8


def _in_proj_kernel(x_ref, g_ref, wa_ref, wkw_ref, wqkv_ref, wz_ref, wab_ref, wabt_ref,
                    cq_ref, ckv_ref, kw_ref, qkv_ref, z_ref, ab_ref, abt_ref):
    f32 = jnp.float32
    hb = _rms(x_ref[0], g_ref[...]).astype(jnp.bfloat16)
    a = jnp.dot(hb, wa_ref[...], preferred_element_type=f32)
    cq_ref[0] = a[:, :A_Q_LORA]
    ckv_ref[0] = a[:, A_Q_LORA:]
    kw_ref[0] = jnp.dot(hb, wkw_ref[...], preferred_element_type=f32)
    qkv_ref[0] = jnp.dot(hb, wqkv_ref[...], preferred_element_type=f32)
    z_ref[0] = jnp.dot(hb, wz_ref[...], preferred_element_type=f32)
    ab_ref[0] = jnp.dot(hb, wab_ref[...], preferred_element_type=f32)
    abt_ref[0] = _nt_dot(wabt_ref[...], hb)


def in_proj_pallas(x, g_mix, w_in):
    bsz, seq, d = x.shape
    f32, bf16 = jnp.float32, jnp.bfloat16
    tm = min(IN_PROJ_TOKENS, seq)
    o = np.cumsum((0,) + COL_WIDTHS)
    wb = w_in.astype(bf16)
    pad_cols = lambda w: jnp.pad(w, ((0, 0), (0, 128 - w.shape[1])))
    wa = wb[:, o[0]:o[2]]
    wkw = pad_cols(wb[:, o[2]:o[4]])
    wqkv = wb[:, o[4]:o[7]]
    wz = wb[:, o[7]:o[8]]
    wab = pad_cols(wb[:, o[8]:o[10]])
    wabt = jnp.concatenate([wb[:, o[8]:o[10]], wb[:, o[3]:o[4]]], axis=1).T
    full = lambda w: pl.BlockSpec(w.shape, lambda b, t: (0, 0))
    tok = lambda w: pl.BlockSpec((1, tm, w), lambda b, t: (b, t, 0))
    shp = lambda w: jax.ShapeDtypeStruct((bsz, seq, w), f32)
    return pl.pallas_call(
        _in_proj_kernel,
        grid=(bsz, seq // tm),
        in_specs=[tok(d), pl.BlockSpec((1, d), lambda b, t: (0, 0)),
                  full(wa), full(wkw), full(wqkv), full(wz), full(wab), full(wabt)],
        out_specs=[tok(A_Q_LORA), tok(A_KV_LORA), tok(128), tok(GDN_COLS), tok(B_QK), tok(128),
                   pl.BlockSpec((1, GATE_ROWS + IDX_HEADS, tm), lambda b, t: (b, 0, t))],
        out_shape=[shp(A_Q_LORA), shp(A_KV_LORA), shp(128), shp(GDN_COLS), shp(B_QK), shp(128),
                   jax.ShapeDtypeStruct((bsz, GATE_ROWS + IDX_HEADS, seq), f32)],
        compiler_params=pltpu.CompilerParams(vmem_limit_bytes=VMEM_LIMIT),
        name="in_proj",
    )(x, g_mix.reshape(1, d), wa, wkw, wqkv, wz, wab, wabt)


def _softplus(x):
    return jnp.maximum(x, 0.0) + jnp.log1p(jnp.exp(-jnp.abs(x)))


def _sigmoid(x):
    return 1.0 / (1.0 + jnp.exp(-x))


def _gdn_gates(pre, a_log, dt_bias, is_decay):
    g = -jnp.exp(a_log) * _softplus(pre + dt_bias)
    return jnp.where(is_decay, g, _sigmoid(pre))


def _gdn_prep_kernel(qkv_ref, halo_ref, cw_ref, ab_ref, abt_ref, alc_ref, dtc_ref, alr_ref, dtr_ref,
                     q_ref, k_ref, v_ref, gc_ref, gr_ref):
    tm = qkv_ref.shape[1]
    x = qkv_ref[0]
    halo = jnp.where(pl.program_id(1) > 0, halo_ref[0], 0.0)
    full = jnp.concatenate([halo, x], axis=0)
    y = x * cw_ref[CONV_WIDTH - 1:CONV_WIDTH, :]
    for back in range(1, CONV_WIDTH):
        shifted = pltpu.roll(full, back, axis=0)[8:, :]
        y = y + shifted * cw_ref[CONV_WIDTH - 1 - back:CONV_WIDTH - back, :]
    y = y * _sigmoid(y)
    for h in range(B_HEADS):
        cols = slice(h * B_HEAD_DIM, (h + 1) * B_HEAD_DIM)
        qh = y[:, h * B_HEAD_DIM:(h + 1) * B_HEAD_DIM]
        kh = y[:, B_QK + h * B_HEAD_DIM:B_QK + (h + 1) * B_HEAD_DIM]
        q_ref[0, :, cols] = qh * lax.rsqrt(
            jnp.sum(qh * qh, axis=-1, keepdims=True) + EPS) * (B_HEAD_DIM ** -0.5)
        k_ref[0, :, cols] = kh * lax.rsqrt(jnp.sum(kh * kh, axis=-1, keepdims=True) + EPS)
    v_ref[0] = y[:, 2 * B_QK:]
    lane = lax.broadcasted_iota(jnp.int32, (tm, 128), 1)
    gates_c = _gdn_gates(ab_ref[0], alc_ref[...], dtc_ref[...], lane < B_HEADS)
    row = lax.broadcasted_iota(jnp.int32, (GATE_ROWS, tm), 0)
    gates_r = _gdn_gates(abt_ref[0], alr_ref[...], dtr_ref[...], row < B_HEADS)
    ti = lax.broadcasted_iota(jnp.int32, (tm, tm), 0)
    tj = lax.broadcasted_iota(jnp.int32, (tm, tm), 1)
    same_chunk = (ti // CHUNK) == (tj // CHUNK)
    hi = lax.Precision.HIGHEST
    lower = jnp.where(same_chunk & (tj <= ti), 1.0, 0.0)
    upper = jnp.where(same_chunk & (ti <= tj), 1.0, 0.0)
    cum_c = jnp.dot(lower, gates_c, preferred_element_type=jnp.float32, precision=hi)
    cum_r = jnp.dot(gates_r, upper, preferred_element_type=jnp.float32, precision=hi)
    gc_ref[0] = jnp.where(lane < B_HEADS, cum_c, gates_c)
    gr_ref[0] = jnp.where(row < B_HEADS, cum_r, gates_r)


def _gdn_main_kernel(q_ref, k_ref, v_ref, z_ref, gc_ref, gr_ref, gon_ref, o_ref, s_ref):
    f32, bf16 = jnp.float32, jnp.bfloat16
    C, Dh = CHUNK, B_HEAD_DIM
    hi = lax.Precision.HIGHEST

    @pl.when(pl.program_id(1) == 0)
    def _():
        s_ref[...] = jnp.zeros(s_ref.shape, f32)

    ii = lax.broadcasted_iota(jnp.int32, (C, C), 0)
    jj = lax.broadcasted_iota(jnp.int32, (C, C), 1)
    causal = ii >= jj
    strict = ii > jj
    eye = jnp.where(ii == jj, 1.0, 0.0)
    mm = lambda a, b: jnp.dot(a, b, preferred_element_type=f32)
    mmh = lambda a, b: jnp.dot(a, b, preferred_element_type=f32, precision=hi)
    n_chunks = q_ref.shape[1] // C
    units = []
    for c in range(n_chunks):
        rows = slice(c * C, (c + 1) * C)
        gates_c = gc_ref[0, rows, :]
        gates_r = gr_ref[0, :, c * C:(c + 1) * C]
        for h in range(B_HEADS):
            cols = slice(h * Dh, (h + 1) * Dh)
            gcum = jnp.broadcast_to(gates_c[:, h:h + 1], (C, Dh))
            beta = jnp.broadcast_to(gates_c[:, B_HEADS + h:B_HEADS + h + 1], (C, Dh))
            gcum_r = jnp.broadcast_to(gates_r[h:h + 1, :], (C, C))
            decay = jnp.where(causal, jnp.exp(jnp.minimum(gcum[:, :C] - gcum_r, 0.0)), 0.0)
            q, k, v = q_ref[0, rows, cols], k_ref[0, rows, cols], v_ref[0, rows, cols]
            qb, kb = q.astype(bf16), k.astype(bf16)
            kk = _nt_dot(kb, kb)
            qk = _nt_dot(qb, kb)
            egc = jnp.exp(gcum)
            g_last = gcum[C - 1:C, :]
            units.append(dict(
                c=c, h=h, rows=rows, cols=cols,
                neg_m=jnp.where(strict, -(beta[:, :C] * kk * decay), 0.0),
                rhs=jnp.concatenate([v * beta, k * (beta * egc)], axis=1),
                q_dec=(q * egc).astype(bf16), intra=(qk * decay).astype(bf16),
                k_tail=(k * jnp.exp(g_last - gcum)).astype(bf16), chunk_decay=jnp.exp(g_last)))
    powers = [un["neg_m"] for un in units]
    t_inv = [eye + p for p in powers]
    for _ in range(5):
        powers = [mmh(p, p) for p in powers]
        t_inv = [t + mmh(p, t) for p, t in zip(powers, t_inv)]
    sols = [mmh(t, un["rhs"]) for t, un in zip(t_inv, units)]
    states = [s_ref[h] for h in range(B_HEADS)]
    for c in range(n_chunks):
        group = [(un, sol) for un, sol in zip(units, sols) if un["c"] == c]
        sbs = [states[un["h"]].astype(bf16) for un, _ in group]
        vbs = [(sol[:, :Dh] - mm(sol[:, Dh:].astype(bf16), sb)).astype(bf16)
               for (un, sol), sb in zip(group, sbs)]
        outs = [mm(un["q_dec"], sb) + mm(un["intra"], vb)
                for (un, _), sb, vb in zip(group, sbs, vbs)]
        for (un, _), vb in zip(group, vbs):
            states[un["h"]] = states[un["h"]] * un["chunk_decay"] + lax.dot_general(
                un["k_tail"], vb, (((0,), (0,)), ((), ())), preferred_element_type=f32)
        for (un, _), o in zip(group, outs):
            o = o * lax.rsqrt(jnp.mean(o * o, axis=-1, keepdims=True) + EPS) * gon_ref[...]
            zz = z_ref[0, un["rows"], un["cols"]]
            o_ref[0, un["rows"], un["cols"]] = o * (zz * _sigmoid(zz))
    for h in range(B_HEADS):
        s_ref[h] = states[h]


def gdn_pallas(qkv, z, ab, abt, conv_w, a_log, dt_bias, g_onorm):
    bsz, seq, _ = qkv.shape
    f32 = jnp.float32
    tm = min(256, seq)
    zero4 = jnp.zeros((B_HEADS,), f32)
    lane_row = lambda v: jnp.pad(jnp.concatenate([v.astype(f32), zero4]), (0, 120)).reshape(1, 128)
    sub_col = lambda v: jnp.concatenate([v.astype(f32), zero4]).reshape(GATE_ROWS, 1)
    tok = lambda w, t=tm: pl.BlockSpec((1, t, w), lambda b, i: (b, i, 0))
    const = lambda shape: pl.BlockSpec(shape, lambda b, i: (0,) * len(shape))
    shp = lambda w: jax.ShapeDtypeStruct((bsz, seq, w), f32)
    q, k, v, gc, gr = pl.pallas_call(
        _gdn_prep_kernel,
        grid=(bsz, seq // tm),
        in_specs=[tok(GDN_COLS),
                  pl.BlockSpec((1, 8, GDN_COLS), lambda b, i: (b, jnp.maximum(i * (tm // 8) - 1, 0), 0)),
                  const((CONV_WIDTH, GDN_COLS)), tok(128),
                  pl.BlockSpec((1, GATE_ROWS, tm), lambda b, i: (b, 0, i)),
                  const((1, 128)), const((1, 128)), const((GATE_ROWS, 1)), const((GATE_ROWS, 1))],
        out_specs=[tok(B_QK), tok(B_QK), tok(B_QK), tok(128),
                   pl.BlockSpec((1, GATE_ROWS, tm), lambda b, i: (b, 0, i))],
        out_shape=[shp(B_QK), shp(B_QK), shp(B_QK), shp(128),
                   jax.ShapeDtypeStruct((bsz, GATE_ROWS, seq), f32)],
        compiler_params=pltpu.CompilerParams(vmem_limit_bytes=VMEM_LIMIT),
        name="gdn_prep",
    )(qkv, qkv, conv_w.astype(f32), ab, abt, lane_row(a_log), lane_row(dt_bias),
      sub_col(a_log), sub_col(dt_bias))
    ts = 2 * CHUNK
    return pl.pallas_call(
        _gdn_main_kernel,
        grid=(bsz, seq // ts),
        in_specs=[tok(B_QK, ts), tok(B_QK, ts), tok(B_QK, ts), tok(B_QK, ts), tok(128, ts),
                  pl.BlockSpec((1, GATE_ROWS, ts), lambda b, i: (b, 0, i)),
                  const((1, B_HEAD_DIM))],
        out_specs=tok(B_QK, ts),
        out_shape=shp(B_QK),
        scratch_shapes=[pltpu.VMEM((B_HEADS, B_HEAD_DIM, B_HEAD_DIM), f32)],
        compiler_params=pltpu.CompilerParams(dimension_semantics=("arbitrary", "arbitrary")),
        name="gdn_main",
    )(q, k, v, z, gc, gr, g_onorm.astype(f32).reshape(1, B_HEAD_DIM))


def _mem_kv_kernel(mem_ref, g_ref, wk_ref, wv_ref, k_ref, v_ref):
    f32, bf16 = jnp.float32, jnp.bfloat16
    mn = _rms(mem_ref[0], g_ref[...]).astype(bf16)
    k = jnp.dot(mn, wk_ref[...], preferred_element_type=f32)
    v = jnp.dot(mn, wv_ref[...], preferred_element_type=f32)
    for h in range(X_HEADS):
        cols = slice(h * X_HEAD_DIM, (h + 1) * X_HEAD_DIM)
        k_ref[0, h] = k[:, cols].astype(bf16)
        v_ref[0, h] = v[:, cols].astype(bf16)


def _mid_kernel(x_ref, oa_ref, ob_ref, wo_ref, gx_ref, wq_ref, k_ref, v_ref, wox_ref, o_ref):
    f32, bf16 = jnp.float32, jnp.bfloat16
    na = oa_ref.shape[2]
    x1 = (x_ref[0]
          + jnp.dot(oa_ref[0].astype(bf16), wo_ref[:na, :], preferred_element_type=f32)
          + jnp.dot(ob_ref[0].astype(bf16), wo_ref[na:, :], preferred_element_type=f32))
    hq = _rms(x1, gx_ref[...]).astype(bf16)
    q = jnp.dot(hq, wq_ref[...], preferred_element_type=f32)
    heads = []
    for h in range(X_HEADS):
        qh = q[:, h * X_HEAD_DIM:(h + 1) * X_HEAD_DIM].astype(bf16)
        lg = _nt_dot(qh, k_ref[0, h]) * (X_HEAD_DIM ** -0.5)
        p = jnp.exp(lg - jnp.max(lg, axis=-1, keepdims=True))
        p = (p / jnp.sum(p, axis=-1, keepdims=True)).astype(bf16)
        heads.append(jnp.dot(p, v_ref[0, h], preferred_element_type=f32).astype(bf16))
    o = jnp.concatenate(heads, axis=1)
    o_ref[0] = x1 + jnp.dot(o, wox_ref[...], preferred_element_type=f32)


def mid_pallas(x, o_a, o_b, w_out, g_cross, mem, g_mem, wq, wk, wv, wo):
    bsz, seq, d = x.shape
    f32, bf16 = jnp.float32, jnp.bfloat16
    hx = X_HEADS * X_HEAD_DIM
    m_len = mem.shape[1]
    const2 = lambda shape: pl.BlockSpec(shape, lambda b: (0,) * len(shape))
    kv_spec = pl.BlockSpec((1, X_HEADS, m_len, X_HEAD_DIM), lambda b: (b, 0, 0, 0))
    kv_shape = jax.ShapeDtypeStruct((bsz, X_HEADS, m_len, X_HEAD_DIM), bf16)
    k, v = pl.pallas_call(
        _mem_kv_kernel,
        grid=(bsz,),
        in_specs=[pl.BlockSpec((1, m_len, d), lambda b: (b, 0, 0)), const2((1, d)),
                  const2((d, hx)), const2((d, hx))],
        out_specs=[kv_spec, kv_spec],
        out_shape=[kv_shape, kv_shape],
        name="mem_kv",
    )(mem, g_mem.reshape(1, d), wk.reshape(d, hx).astype(bf16), wv.reshape(d, hx).astype(bf16))
    tm = min(256, seq)
    tok = lambda w: pl.BlockSpec((1, tm, w), lambda b, t: (b, t, 0))
    const = lambda shape: pl.BlockSpec(shape, lambda b, t: (0,) * len(shape))
    kv_spec2 = pl.BlockSpec((1, X_HEADS, m_len, X_HEAD_DIM), lambda b, t: (b, 0, 0, 0))
    return pl.pallas_call(
        _mid_kernel,
        grid=(bsz, seq // tm),
        in_specs=[tok(d), tok(o_a.shape[2]), tok(o_b.shape[2]), const((MIX_WIDTH, d)),
                  const((1, d)), const((d, hx)), kv_spec2, kv_spec2, const((hx, d))],
        out_specs=tok(d),
        out_shape=jax.ShapeDtypeStruct((bsz, seq, d), f32),
        compiler_params=pltpu.CompilerParams(vmem_limit_bytes=VMEM_LIMIT),
        name="mid",
    )(x, o_a, o_b, w_out.astype(bf16), g_cross.reshape(1, d), wq.reshape(d, hx).astype(bf16),
      k, v, wo.reshape(hx, d).astype(bf16))


def kernel(x, mem, g_mix, w_in, g_cq, g_ckv, g_kidx, w_uq, w_iq, w_uk, w_uv, rel_bias, conv_w, A_log, dt_bias, g_onorm, w_out, g_cross, g_mem, wq_x, wk_x, wv_x, wo_x, g_ffn, w_pq, sub_keys, u_emb, v_emb, g_final):
    bsz, seq, d = x.shape
    for l in range(DEPTH):
        cq, ckv, kw, qkv, z, ab, abt = in_proj_pallas(x, g_mix[l], w_in[l])
        o_a = dsa_pallas(cq, ckv, kw, abt, g_cq[l], g_ckv[l], g_kidx[l],
                           w_uq[l], w_iq[l], w_uk[l], w_uv[l], rel_bias)
        o_b = gdn_pallas(qkv, z, ab, abt, conv_w[l], A_log[l], dt_bias[l], g_onorm[l])
        x = mid_pallas(x, o_a, o_b, w_out[l], g_cross[l], mem, g_mem[l],
                       wq_x[l], wk_x[l], wv_x[l], wo_x[l])
        xf = x.reshape(bsz * seq, d)
        y = peer_pallas(xf, g_ffn[l], w_pq[l], sub_keys[l], u_emb[l], v_emb[l])
        if l + 1 < DEPTH:
            x = (xf + y).reshape(bsz, seq, d)
    return final_residual_rmsnorm(xf, y, g_final).reshape(bsz, seq, d)
```

```python
import math
from functools import partial
import jax
import jax.numpy as jnp
from jax import lax
import numpy as np
from jax.experimental import pallas as pl
from jax.experimental.pallas import tpu as pltpu

DEPTH = 1

CHUNK = 64
Q_BLOCK = 128
EPS = 1e-6

A_HEADS = 8
A_HEAD_DIM = 64
A_Q_LORA = 256
A_KV_LORA = 256
IDX_HEADS = 8
IDX_DIM = 64
IDX_TOPK_MAX = 256
ATTN_SCALE = A_HEAD_DIM ** -0.5
IDX_SCALE = (IDX_HEADS * IDX_DIM) ** -0.5

B_HEADS = 4
B_HEAD_DIM = 128
B_QK = B_HEADS * B_HEAD_DIM
CONV_WIDTH = 4

REL_BUCKETS = 32
REL_MAX_DIST = 128

X_HEADS = 4
X_HEAD_DIM = 128

P_HEADS = 8
N_KEYS = 128
N_EXPERTS = N_KEYS * N_KEYS
P_TOPK = 16
P_QDIM = 256

COL_WIDTHS = (A_Q_LORA, A_KV_LORA, IDX_DIM, IDX_HEADS, B_QK, B_QK, B_QK, B_QK, B_HEADS, B_HEADS)
MIX_WIDTH = A_HEADS * A_HEAD_DIM + B_HEADS * B_HEAD_DIM


def t5_bucket(rel):
    half = REL_BUCKETS // 2
    max_exact = half // 2
    n = jnp.abs(rel)
    nf = jnp.maximum(n, max_exact).astype(jnp.float32)
    large = max_exact + (jnp.log(nf / max_exact) / math.log(REL_MAX_DIST / max_exact)
                         * (half - max_exact)).astype(jnp.int32)
    large = jnp.minimum(large, half - 1)
    return jnp.where(rel > 0, half, 0) + jnp.where(n < max_exact, n, large)


INT_MIN = -2147483648
NEG_BIG = -1e30
KEY_TILE = 512
KEY_PAD = KEY_TILE - Q_BLOCK
VMEM_LIMIT = 56 * 1024 * 1024


def _rms(x, g):
    return x * lax.rsqrt(jnp.mean(x * x, axis=-1, keepdims=True) + EPS) * g


def _nt_dot(a, b):
    return lax.dot_general(a, b, (((1,), (1,)), ((), ())), preferred_element_type=jnp.float32)


def _dsa_prep_kernel(cq_ref, ckv_ref, kw_ref, gcq_ref, gckv_ref, gk_ref,
                       wiqt_ref, wuqt_ref, wuk_ref,
                       at_ref, qlatt_ref, ckvn_ref, kidxn_ref):
    f32, bf16 = jnp.float32, jnp.bfloat16
    Q = Q_BLOCK
    kw = kw_ref[0]
    cqn = _rms(cq_ref[0], gcq_ref[...]).astype(bf16)
    qit = _nt_dot(wiqt_ref[...], cqn)
    qt = _nt_dot(wuqt_ref[...], cqn)
    for h in range(A_HEADS):
        cols = slice(h * Q, (h + 1) * Q)
        at_ref[0, 0, :, cols] = qit[h * IDX_DIM:(h + 1) * IDX_DIM, :].astype(bf16)
        qh = qt[h * A_HEAD_DIM:(h + 1) * A_HEAD_DIM, :].astype(bf16)
        ql = jnp.dot(wuk_ref[h], qh, preferred_element_type=f32) * ATTN_SCALE
        qlatt_ref[0, 0, :, cols] = ql.astype(bf16)
    ckvn_ref[0] = _rms(ckv_ref[0], gckv_ref[...]).astype(bf16)
    kidxn_ref[0] = _rms(kw[:, :IDX_DIM], gk_ref[...]).astype(bf16)


def _dsa_main_kernel(at_ref, qlatt_ref, wt_ref, kidx_ref, ckv_ref, bias0_ref, wuvt_ref, o_ref,
                       sc_ref, big_ref, acc_ref, m_ref, l_ref, *, topk):
    f32, i32, bf16 = jnp.float32, jnp.int32, jnp.bfloat16
    Q, T = Q_BLOCK, KEY_TILE
    i = pl.program_id(1)
    e = (i + 1) * Q
    nt = (i + T // Q) // (T // Q)
    kf = jnp.float32(topk)
    wsc = wt_ref[0] * IDX_SCALE
    sub = lax.broadcasted_iota(i32, (T, Q), 0)
    qlane = lax.broadcasted_iota(i32, (T, Q), 1)
    limit = ((i * Q + qlane) // CHUNK + 1) * CHUNK
    at = at_ref[0, 0]

    def tile_start(j):
        return pl.multiple_of(e + KEY_PAD - (j + 1) * T, 128)

    def key_pos(j):
        return tile_start(j) - KEY_PAD + sub

    def score_tile(j, carry):
        kid = kidx_ref[0, pl.ds(tile_start(j), T), :]
        big_ref[...] = jnp.dot(kid, at, preferred_element_type=f32)
        s = jnp.zeros((T, Q), f32)
        for h in range(IDX_HEADS):
            s = s + wsc[h:h + 1, :] * jnp.maximum(big_ref[:, h * Q:(h + 1) * Q], 0.0)
        bits = lax.bitcast_convert_type(s, i32)
        key = jnp.where(bits < 0, bits ^ jnp.int32(0x7FFFFFFF), bits)
        key = jnp.where(s == 0.0, 0, key)
        kpos = key_pos(j)
        key = jnp.where(kpos >= 0, key, INT_MIN)
        sc_ref[j] = jnp.where(kpos < limit, key, INT_MIN)
        return carry

    lax.fori_loop(0, nt, score_tile, 0)

    def count_where(pred):
        def body(j, cnt):
            hit = jnp.where(pred(sc_ref[j], key_pos(j)), 1.0, 0.0)
            return cnt + jnp.sum(hit.reshape(T // 64, 64, Q), axis=0)
        cnt = lax.fori_loop(0, nt, body, jnp.zeros((64, Q), f32))
        return jnp.sum(cnt, axis=0, keepdims=True)

    def bit_body(b, carry):
        u, cacc = carry
        cand = u | lax.shift_left(jnp.int32(1), 31 - b)
        tvec = cand ^ jnp.int32(INT_MIN)
        tot = count_where(lambda k, kpos: k >= tvec)
        ok = tot >= kf
        return jnp.where(ok, cand, u), jnp.where(ok, tot, cacc)

    u, cacc = lax.fori_loop(0, 32, bit_body, (jnp.zeros((1, Q), i32), jnp.zeros((1, Q), f32)))
    thr = jnp.maximum(u ^ jnp.int32(INT_MIN), INT_MIN + 1)
    overflow = jnp.where(u != 0, cacc, 0.0) > kf
    n_over = jnp.max(jnp.where(overflow, 1.0, 0.0), axis=1, keepdims=True)[0, 0]

    @pl.when(n_over > 0.0)
    def _():
        need = kf - count_where(lambda k, kpos: k > thr)

        def cut_body(b, cut):
            cand = cut | lax.shift_left(jnp.int32(1), 14 - b)
            cnt = count_where(lambda k, kpos: jnp.where(k == thr, kpos, cand) < cand)
            return jnp.where(cnt <= need, cand, cut)

        cut = lax.fori_loop(0, 15, cut_body, jnp.zeros((1, Q), i32))

        def drop_tile(j, carry):
            k = sc_ref[j]
            drop = jnp.where(k == thr, key_pos(j), -1) >= cut
            sc_ref[j] = jnp.where(drop, INT_MIN, k)
            return carry

        lax.fori_loop(0, nt, drop_tile, 0)

    m_ref[...] = jnp.full(m_ref.shape, NEG_BIG, f32)
    l_ref[...] = jnp.zeros(l_ref.shape, f32)
    acc_ref[...] = jnp.zeros(acc_ref.shape, f32)
    qlatt = qlatt_ref[0, 0]

    def attn_tile(j, with_bias):
        kv = ckv_ref[0, pl.ds(tile_start(j), T), :]
        kvt = kv.T
        mask_add = jnp.where(sc_ref[j] >= thr, 0.0, NEG_BIG)
        pair_cols = [slice(g * 2 * Q, (g + 1) * 2 * Q) for g in range(A_HEADS // 2)]
        logits = lambda g: jnp.dot(kv, qlatt[:, pair_cols[g]], preferred_element_type=f32)
        x_next = logits(0)
        for g in range(A_HEADS // 2):
            x_pair = x_next
            if g + 1 < A_HEADS // 2:
                x_next = logits(g + 1)
            ps, alphas = [], []
            for u in range(2):
                h = 2 * g + u
                cols = slice(h * Q, (h + 1) * Q)
                x = x_pair[:, u * Q:(u + 1) * Q] + mask_add
                if with_bias:
                    x = x + bias0_ref[h]
                m_prev = m_ref[:, cols]
                m_new = jnp.maximum(m_prev, jnp.max(x, axis=0, keepdims=True))
                p = jnp.exp(x - m_new)
                alpha = jnp.exp(m_prev - m_new)
                l_ref[:, cols] = alpha * l_ref[:, cols] + jnp.sum(p, axis=0, keepdims=True)
                m_ref[:, cols] = m_new
                ps.append(p.astype(bf16))
                alphas.append(alpha)
            pv = jnp.dot(kvt, jnp.concatenate(ps, axis=1), preferred_element_type=f32)
            acc_ref[:, pair_cols[g]] = (
                jnp.concatenate(alphas, axis=1) * acc_ref[:, pair_cols[g]] + pv)

    attn_tile(0, True)

    def attn_body(j, carry):
        attn_tile(j, False)
        return carry

    lax.fori_loop(1, nt, attn_body, 0)

    inv_l = 1.0 / l_ref[...]
    outs = []
    for h in range(A_HEADS):
        cols = slice(h * Q, (h + 1) * Q)
        o_lat_t = (acc_ref[:, cols] * inv_l[:, cols]).astype(bf16)
        outs.append(jnp.dot(wuvt_ref[h], o_lat_t, preferred_element_type=f32))
    o_ref[0] = jnp.concatenate(outs, axis=0).T


def dsa_pallas(cq, ckv, kw, widx_t, g_cq, g_ckv, g_kidx, w_uq, w_iq, w_uk, w_uv, rel_bias):
    bsz, seq, _ = cq.shape
    f32, bf16 = jnp.float32, jnp.bfloat16
    Q, T, H = Q_BLOCK, KEY_TILE, A_HEADS
    nblk = seq // Q
    topk = min(IDX_TOPK_MAX, seq // 4)
    wiqt = w_iq.reshape(A_Q_LORA, IDX_HEADS * IDX_DIM).T.astype(bf16)
    wuqt = w_uq.reshape(A_Q_LORA, H * A_HEAD_DIM).T.astype(bf16)
    wuk = jnp.transpose(w_uk, (1, 0, 2)).astype(bf16)
    wuvt = jnp.transpose(w_uv, (1, 2, 0)).astype(bf16)
    tok = lambda w: pl.BlockSpec((1, Q, w), lambda b, t: (b, t, 0))
    full = lambda shape: pl.BlockSpec(shape, lambda b, t: (0,) * len(shape))
    blk = lambda r: pl.BlockSpec((1, 1, r, H * Q), lambda b, t: (b, t, 0, 0))
    a_t, qlat_t, ckvn, kidxn = pl.pallas_call(
        _dsa_prep_kernel,
        grid=(bsz, nblk),
        in_specs=[tok(A_Q_LORA), tok(A_KV_LORA), tok(128),
                  full((1, A_Q_LORA)), full((1, A_KV_LORA)), full((1, IDX_DIM)),
                  full(wiqt.shape), full(wuqt.shape), full(wuk.shape)],
        out_specs=[blk(IDX_DIM), blk(A_KV_LORA), tok(A_KV_LORA), tok(IDX_DIM)],
        out_shape=[jax.ShapeDtypeStruct((bsz, nblk, IDX_DIM, H * Q), bf16),
                   jax.ShapeDtypeStruct((bsz, nblk, A_KV_LORA, H * Q), bf16),
                   jax.ShapeDtypeStruct((bsz, seq, A_KV_LORA), bf16),
                   jax.ShapeDtypeStruct((bsz, seq, IDX_DIM), bf16)],
        name="dsa_prep",
    )(cq, ckv, kw, g_cq.reshape(1, -1), g_ckv.reshape(1, -1), g_kidx.reshape(1, -1),
      wiqt, wuqt, wuk)
    ckvp = jnp.pad(ckvn, ((0, 0), (KEY_PAD, 0), (0, 0)))
    kidxp = jnp.pad(kidxn, ((0, 0), (KEY_PAD, 0), (0, 0)))
    rel = (jnp.arange(T, dtype=jnp.int32)[:, None] - jnp.arange(Q, dtype=jnp.int32)[None, :]
           - KEY_PAD)
    far = rel_bias[t5_bucket(jnp.int32(-KEY_PAD - 1))]
    bias0 = jnp.transpose(rel_bias[t5_bucket(rel)] - far, (2, 0, 1)).astype(f32)
    nt_max = (nblk - 1 + T // Q) // (T // Q)
    skey = seq + KEY_PAD
    return pl.pallas_call(
        partial(_dsa_main_kernel, topk=topk),
        grid=(bsz, nblk),
        in_specs=[blk(IDX_DIM), blk(A_KV_LORA),
                  pl.BlockSpec((1, IDX_HEADS, Q), lambda b, t: (b, 1, t)),
                  pl.BlockSpec((1, skey, IDX_DIM), lambda b, t: (b, 0, 0)),
                  pl.BlockSpec((1, skey, A_KV_LORA), lambda b, t: (b, 0, 0)),
                  full(bias0.shape), full(wuvt.shape)],
        out_specs=tok(H * A_HEAD_DIM),
        out_shape=jax.ShapeDtypeStruct((bsz, seq, H * A_HEAD_DIM), f32),
        scratch_shapes=[pltpu.VMEM((nt_max, T, Q), jnp.int32),
                        pltpu.VMEM((T, H * Q), f32),
                        pltpu.VMEM((A_KV_LORA, H * Q), f32),
                        pltpu.VMEM((1, H * Q), f32),
                        pltpu.VMEM((1, H * Q), f32)],
        compiler_params=pltpu.CompilerParams(
            dimension_semantics=("arbitrary", "arbitrary"), vmem_limit_bytes=VMEM_LIMIT),
        name="dsa_main",
    )(a_t, qlat_t, widx_t, kidxp, ckvp, bias0, wuvt)


PEER_SCORE_TOKENS = 256
PEER_GATHER_TOKENS = 64
PEER_SLOTS = P_HEADS * P_TOPK
WORDS_PER_ROW = 4
PEER_TILES = 16


def _top16(s, order=None, payload=None):
    if order is None:
        order = lax.broadcasted_iota(jnp.int32, s.shape, 0).astype(jnp.float32)
    vals, picks = [], []
    for _ in range(P_TOPK):
        m = jnp.max(s, axis=0, keepdims=True)
        pos = jnp.min(jnp.where(s == m, order, float(N_EXPERTS)), axis=0, keepdims=True)
        hit = order == pos
        vals.append(m)
        if payload is None:
            picks.append(pos)
        else:
            picks.append(jnp.max(jnp.where(hit, payload, -1.0), axis=0, keepdims=True))
        s = jnp.where(hit, -jnp.inf, s)
    return vals, picks


def _staircase_candidates(v1, i1, v2, i2):
    v1m, i1m = jnp.concatenate(v1, axis=0), jnp.concatenate(i1, axis=0)
    v2m, i2m = jnp.concatenate(v2, axis=0), jnp.concatenate(i2, axis=0)
    t = v1m.shape[1]
    sub8 = lax.broadcasted_iota(jnp.int32, (8, t), 0).astype(jnp.float32)
    cand, cidx, rank = [], [], []
    for a in range(8):
        cand.append(v1[a] + v2m[:8])
        cidx.append(i1[a] * float(N_KEYS) + i2m[:8])
        rank.append(sub8 + float(a * P_TOPK))
    cand.append(v1[0] + v2m[8:])
    cidx.append(i1[0] * float(N_KEYS) + i2m[8:])
    rank.append(sub8 + 8.0)
    cand.append(v1m[8:] + v2[0])
    cidx.append(i1m[8:] * float(N_KEYS) + i2[0])
    rank.append((sub8 + 8.0) * float(P_TOPK))
    cat = lambda xs: jnp.concatenate(xs, axis=0)
    return cat(cand), cat(rank), cat(cidx)


def _peer_score_kernel(x_ref, g_ref, wpqt_ref, sk_ref, hn_ref, eidx_ref, gate_ref):
    f32, bf16 = jnp.float32, jnp.bfloat16
    hb = _rms(x_ref[...], g_ref[...]).astype(bf16)
    hn_ref[...] = hb
    qrt = _nt_dot(wpqt_ref[...], hb)
    half = P_QDIM // 2
    e_rows, g_rows = [], []
    for hd in range(P_HEADS):
        tops = []
        for p in range(2):
            qhp = qrt[(hd * 2 + p) * half:(hd * 2 + p + 1) * half, :].astype(bf16)
            s = jnp.dot(sk_ref[hd * 2 + p], qhp, preferred_element_type=f32)
            tops.append(_top16(s))
        (v1, i1), (v2, i2) = tops
        best, be = _top16(*_staircase_candidates(v1, i1, v2, i2))
        ex = [jnp.exp(b - best[0]) for b in best]
        den = ex[0]
        for k in range(1, P_TOPK):
            den = den + ex[k]
        inv = 1.0 / den
        e_rows += be
        g_rows += [x * inv for x in ex]
    eidx_ref[...] = (jnp.concatenate(e_rows, axis=0).T * float(WORDS_PER_ROW)).astype(jnp.int32)
    gate_ref[...] = jnp.concatenate(g_rows, axis=0).T


def _diag_mask():
    r = lax.broadcasted_iota(jnp.int32, (8, PEER_SLOTS * 8), 0)
    c = lax.broadcasted_iota(jnp.int32, (8, PEER_SLOTS * 8), 1)
    return (c & 7) == r


def _gather_pair(idx_ref, tab_ref, g_refs, t0):
    views = [idx_ref.at[t0 + u] for u in range(2)]
    for r in range(PEER_SLOTS):
        for view, g_ref in zip(views, g_refs):
            row0 = pl.multiple_of(view[r], WORDS_PER_ROW)
            g_ref[r * WORDS_PER_ROW:(r + 1) * WORDS_PER_ROW, :] = tab_ref[
                pl.ds(row0, WORDS_PER_ROW), :]


def _gather_pipeline(idx_ref, tab_ref, tiles, n_tokens, consume):
    pairs = len(tiles) // 2
    for g_ref in tiles[-2:]:
        g_ref[...] = jnp.zeros(g_ref.shape, g_ref.dtype)

    def trip(i, carry):
        for k in range(pairs):
            pair = pairs * i + k
            prev_tiles = tiles[2 * k - 2:2 * k] if k else tiles[-2:]
            _gather_pair(idx_ref, tab_ref, tiles[2 * k:2 * k + 2], pair * 2)
            for u in range(2):
                consume(prev_tiles[u], jnp.maximum(pair - 1, 0), u)
        return carry

    lax.fori_loop(0, n_tokens // (2 * pairs), trip, 0)
    for u in range(2):
        consume(tiles[-2 + u], n_tokens // 2 - 1, u)


def _peer_act_kernel(idx_ref, hn_ref, tab_ref, a_ref, m_ref, *tiles):
    f32, bf16 = jnp.float32, jnp.bfloat16
    tb = a_ref.shape[0]

    def dots(g_ref, tp, u):
        hp = hn_ref[pl.ds(pl.multiple_of(tp * 16, 16), 16), :]
        m = _nt_dot(hp, pltpu.bitcast(g_ref[...], bf16))
        m_ref[pl.ds(pl.multiple_of((tp * 2 + u) * 8, 8), 8), :] = m[u * 8:(u + 1) * 8, :]

    _gather_pipeline(idx_ref, tab_ref, tiles, tb, dots)
    m3 = m_ref[...].reshape(tb, 8, PEER_SLOTS * 8)
    z = jnp.sum(jnp.where(_diag_mask()[None], m3, 0.0), axis=1)
    rr = lax.broadcasted_iota(jnp.int32, (PEER_SLOTS * 8, PEER_SLOTS), 0)
    cc = lax.broadcasted_iota(jnp.int32, (PEER_SLOTS * 8, PEER_SLOTS), 1)
    pool = jnp.where((rr >> 3) == cc, 1.0, 0.0).astype(bf16)
    z_hi = z.astype(bf16)
    z_lo = (z - z_hi.astype(f32)).astype(bf16)
    a_ref[...] = (jnp.dot(z_hi, pool, preferred_element_type=f32)
                  + jnp.dot(z_lo, pool, preferred_element_type=f32))


def _peer_out_kernel(idx_ref, a_ref, gate_ref, tab_ref, o_ref, w_ref, *tiles):
    f32, bf16 = jnp.float32, jnp.bfloat16
    tb = a_ref.shape[0]
    a = a_ref[...]
    act = 0.5 * a * (1.0 + lax.erf(a * (2.0 ** -0.5)))
    wgt = (gate_ref[...] * act).astype(bf16)
    rr = lax.broadcasted_iota(jnp.int32, (PEER_SLOTS, PEER_SLOTS * 8), 0)
    cc = lax.broadcasted_iota(jnp.int32, (PEER_SLOTS, PEER_SLOTS * 8), 1)
    expand = jnp.where((cc >> 3) == rr, 1.0, 0.0).astype(bf16)
    w_ref[...] = jnp.dot(wgt, expand, preferred_element_type=f32)
    diag = _diag_mask()

    def combine(g_ref, tp, u):
        t = tp * 2 + u
        wrow = jnp.broadcast_to(w_ref[pl.ds(t, 1), :], (8, PEER_SLOTS * 8))
        wsel = jnp.where(diag, wrow, 0.0).astype(bf16)
        o_ref[pl.ds(pl.multiple_of(t * 8, 8), 8), :] = jnp.dot(
            wsel, pltpu.bitcast(g_ref[...], bf16), preferred_element_type=f32)

    _gather_pipeline(idx_ref, tab_ref, tiles, tb, combine)


def _pack_table(tab):
    n_e = tab.shape[0]
    tb = lax.bitcast_convert_type(
        tab.astype(jnp.bfloat16).reshape(n_e, WORDS_PER_ROW, 2, 128), jnp.uint16).astype(jnp.uint32)
    word = tb[:, :, 0, :] | (tb[:, :, 1, :] << 16)
    return lax.bitcast_convert_type(word, jnp.int32).reshape(n_e * WORDS_PER_ROW, 128)


def peer_pallas(x, g_ffn, w_pq, sub_keys, u_emb, v_emb):
    n_tok, d = x.shape
    f32, bf16 = jnp.float32, jnp.bfloat16
    ts, tg = PEER_SCORE_TOKENS, PEER_GATHER_TOKENS
    wpqt = w_pq.reshape(d, P_HEADS * P_QDIM).T.astype(bf16)
    sk = sub_keys.reshape(P_HEADS * 2, N_KEYS, P_QDIM // 2).astype(bf16)
    hn, eidx, gate = pl.pallas_call(
        _peer_score_kernel,
        grid=(n_tok // ts,),
        in_specs=[pl.BlockSpec((ts, d), lambda i: (i, 0)),
                  pl.BlockSpec((1, d), lambda i: (0, 0)),
                  pl.BlockSpec(wpqt.shape, lambda i: (0, 0)),
                  pl.BlockSpec(sk.shape, lambda i: (0, 0, 0))],
        out_specs=[pl.BlockSpec((ts, d), lambda i: (i, 0)),
                   pl.BlockSpec((ts, PEER_SLOTS), lambda i: (i, 0)),
                   pl.BlockSpec((ts, PEER_SLOTS), lambda i: (i, 0))],
        out_shape=[jax.ShapeDtypeStruct((n_tok, d), bf16),
                   jax.ShapeDtypeStruct((n_tok, PEER_SLOTS), jnp.int32),
                   jax.ShapeDtypeStruct((n_tok, PEER_SLOTS), f32)],
        compiler_params=pltpu.CompilerParams(vmem_limit_bytes=VMEM_LIMIT),
        name="peer_score",
    )(x, g_ffn.reshape(1, d), wpqt, sk)
    utab, vtab = _pack_table(u_emb), _pack_table(v_emb)
    idx_spec = pl.BlockSpec((tg, PEER_SLOTS), lambda i: (i, 0), memory_space=pltpu.SMEM)
    tab_spec = pl.BlockSpec(memory_space=pltpu.VMEM)
    slot_spec = pl.BlockSpec((tg, PEER_SLOTS), lambda i: (i, 0))
    row_spec = pl.BlockSpec((tg * 8, 128), lambda i: (i, 0))
    gbuf = pltpu.VMEM((PEER_SLOTS * WORDS_PER_ROW, 128), jnp.int32)
    cparams = pltpu.CompilerParams(vmem_limit_bytes=VMEM_LIMIT)
    act = pl.pallas_call(
        _peer_act_kernel,
        grid=(n_tok // tg,),
        in_specs=[idx_spec, row_spec, tab_spec],
        out_specs=slot_spec,
        out_shape=jax.ShapeDtypeStruct((n_tok, PEER_SLOTS), f32),
        scratch_shapes=[pltpu.VMEM((tg * 8, PEER_SLOTS * 8), f32)] + [gbuf] * PEER_TILES,
        compiler_params=cparams,
        name="peer_act",
    )(eidx, hn.reshape(n_tok * 8, 128), utab)
    out = pl.pallas_call(
        _peer_out_kernel,
        grid=(n_tok // tg,),
        in_specs=[idx_spec, slot_spec, slot_spec, tab_spec],
        out_specs=row_spec,
        out_shape=jax.ShapeDtypeStruct((n_tok * 8, 128), f32),
        scratch_shapes=[pltpu.VMEM((tg, PEER_SLOTS * 8), f32)] + [gbuf] * PEER_TILES,
        compiler_params=cparams,
        name="peer_out",
    )(eidx, act, gate, vtab)
    return out.reshape(n_tok, d)


def _final_kernel(x_ref, y_ref, g_ref, o_ref):
    o_ref[...] = _rms(x_ref[...] + y_ref[...], g_ref[...])


def final_residual_rmsnorm(x, y, g):
    n, d = x.shape
    tm = 512
    row = pl.BlockSpec((tm, d), lambda i: (i, 0))
    return pl.pallas_call(
        _final_kernel,
        grid=(n // tm,),
        in_specs=[row, row, pl.BlockSpec((1, d), lambda i: (0, 0))],
        out_specs=row,
        out_shape=jax.ShapeDtypeStruct((n, d), x.dtype),
        name="final_rmsnorm",
    )(x, y, g.reshape(1, d))


IN_PROJ_TOKENS = 512
GDN_COLS = 3 * B_QK
GATE_ROWS = 8


def _in_proj_kernel(x_ref, g_ref, wa_ref, wkw_ref, wqkv_ref, wz_ref, wab_ref, wabt_ref,
                    cq_ref, ckv_ref, kw_ref, qkv_ref, z_ref, ab_ref, abt_ref):
    f32 = jnp.float32
    hb = _rms(x_ref[0], g_ref[...]).astype(jnp.bfloat16)
    a = jnp.dot(hb, wa_ref[...], preferred_element_type=f32)
    cq_ref[0] = a[:, :A_Q_LORA]
    ckv_ref[0] = a[:, A_Q_LORA:]
    kw_ref[0] = jnp.dot(hb, wkw_ref[...], preferred_element_type=f32)
    qkv_ref[0] = jnp.dot(hb, wqkv_ref[...], preferred_element_type=f32)
    z_ref[0] = jnp.dot(hb, wz_ref[...], preferred_element_type=f32)
    ab_ref[0] = jnp.dot(hb, wab_ref[...], preferred_element_type=f32)
    abt_ref[0] = _nt_dot(wabt_ref[...], hb)


def in_proj_pallas(x, g_mix, w_in):
    bsz, seq, d = x.shape
    f32, bf16 = jnp.float32, jnp.bfloat16
    tm = min(IN_PROJ_TOKENS, seq)
    o = np.cumsum((0,) + COL_WIDTHS)
    wb = w_in.astype(bf16)
    pad_cols = lambda w: jnp.pad(w, ((0, 0), (0, 128 - w.shape[1])))
    wa = wb[:, o[0]:o[2]]
    wkw = pad_cols(wb[:, o[2]:o[4]])
    wqkv = wb[:, o[4]:o[7]]
    wz = wb[:, o[7]:o[8]]
    wab = pad_cols(wb[:, o[8]:o[10]])
    wabt = jnp.concatenate([wb[:, o[8]:o[10]], wb[:, o[3]:o[4]]], axis=1).T
    full = lambda w: pl.BlockSpec(w.shape, lambda b, t: (0, 0))
    tok = lambda w: pl.BlockSpec((1, tm, w), lambda b, t: (b, t, 0))
    shp = lambda w: jax.ShapeDtypeStruct((bsz, seq, w), f32)
    return pl.pallas_call(
        _in_proj_kernel,
        grid=(bsz, seq // tm),
        in_specs=[tok(d), pl.BlockSpec((1, d), lambda b, t: (0, 0)),
                  full(wa), full(wkw), full(wqkv), full(wz), full(wab), full(wabt)],
        out_specs=[tok(A_Q_LORA), tok(A_KV_LORA), tok(128), tok(GDN_COLS), tok(B_QK), tok(128),
                   pl.BlockSpec((1, GATE_ROWS + IDX_HEADS, tm), lambda b, t: (b, 0, t))],
        out_shape=[shp(A_Q_LORA), shp(A_KV_LORA), shp(128), shp(GDN_COLS), shp(B_QK), shp(128),
                   jax.ShapeDtypeStruct((bsz, GATE_ROWS + IDX_HEADS, seq), f32)],
        compiler_params=pltpu.CompilerParams(vmem_limit_bytes=VMEM_LIMIT),
        name="in_proj",
    )(x, g_mix.reshape(1, d), wa, wkw, wqkv, wz, wab, wabt)


def _softplus(x):
    return jnp.maximum(x, 0.0) + jnp.log1p(jnp.exp(-jnp.abs(x)))


def _sigmoid(x):
    return 1.0 / (1.0 + jnp.exp(-x))


def _gdn_gates(pre, a_log, dt_bias, is_decay):
    g = -jnp.exp(a_log) * _softplus(pre + dt_bias)
    return jnp.where(is_decay, g, _sigmoid(pre))


def _gdn_prep_kernel(qkv_ref, halo_ref, cw_ref, ab_ref, abt_ref, alc_ref, dtc_ref, alr_ref, dtr_ref,
                     q_ref, k_ref, v_ref, gc_ref, gr_ref):
    tm = qkv_ref.shape[1]
    x = qkv_ref[0]
    halo = jnp.where(pl.program_id(1) > 0, halo_ref[0], 0.0)
    full = jnp.concatenate([halo, x], axis=0)
    y = x * cw_ref[CONV_WIDTH - 1:CONV_WIDTH, :]
    for back in range(1, CONV_WIDTH):
        shifted = pltpu.roll(full, back, axis=0)[8:, :]
        y = y + shifted * cw_ref[CONV_WIDTH - 1 - back:CONV_WIDTH - back, :]
    y = y * _sigmoid(y)
    for h in range(B_HEADS):
        cols = slice(h * B_HEAD_DIM, (h + 1) * B_HEAD_DIM)
        qh = y[:, h * B_HEAD_DIM:(h + 1) * B_HEAD_DIM]
        kh = y[:, B_QK + h * B_HEAD_DIM:B_QK + (h + 1) * B_HEAD_DIM]
        q_ref[0, :, cols] = qh * lax.rsqrt(
            jnp.sum(qh * qh, axis=-1, keepdims=True) + EPS) * (B_HEAD_DIM ** -0.5)
        k_ref[0, :, cols] = kh * lax.rsqrt(jnp.sum(kh * kh, axis=-1, keepdims=True) + EPS)
    v_ref[0] = y[:, 2 * B_QK:]
    lane = lax.broadcasted_iota(jnp.int32, (tm, 128), 1)
    gates_c = _gdn_gates(ab_ref[0], alc_ref[...], dtc_ref[...], lane < B_HEADS)
    row = lax.broadcasted_iota(jnp.int32, (GATE_ROWS, tm), 0)
    gates_r = _gdn_gates(abt_ref[0], alr_ref[...], dtr_ref[...], row < B_HEADS)
    ti = lax.broadcasted_iota(jnp.int32, (tm, tm), 0)
    tj = lax.broadcasted_iota(jnp.int32, (tm, tm), 1)
    same_chunk = (ti // CHUNK) == (tj // CHUNK)
    hi = lax.Precision.HIGHEST
    lower = jnp.where(same_chunk & (tj <= ti), 1.0, 0.0)
    upper = jnp.where(same_chunk & (ti <= tj), 1.0, 0.0)
    cum_c = jnp.dot(lower, gates_c, preferred_element_type=jnp.float32, precision=hi)
    cum_r = jnp.dot(gates_r, upper, preferred_element_type=jnp.float32, precision=hi)
    gc_ref[0] = jnp.where(lane < B_HEADS, cum_c, gates_c)
    gr_ref[0] = jnp.where(row < B_HEADS, cum_r, gates_r)


def _gdn_main_kernel(q_ref, k_ref, v_ref, z_ref, gc_ref, gr_ref, gon_ref, o_ref, s_ref):
    f32, bf16 = jnp.float32, jnp.bfloat16
    C, Dh = CHUNK, B_HEAD_DIM
    hi = lax.Precision.HIGHEST

    @pl.when(pl.program_id(1) == 0)
    def _():
        s_ref[...] = jnp.zeros(s_ref.shape, f32)

    ii = lax.broadcasted_iota(jnp.int32, (C, C), 0)
    jj = lax.broadcasted_iota(jnp.int32, (C, C), 1)
    causal = ii >= jj
    strict = ii > jj
    eye = jnp.where(ii == jj, 1.0, 0.0)
    mm = lambda a, b: jnp.dot(a, b, preferred_element_type=f32)
    mmh = lambda a, b: jnp.dot(a, b, preferred_element_type=f32, precision=hi)
    n_chunks = q_ref.shape[1] // C
    units = []
    for c in range(n_chunks):
        rows = slice(c * C, (c + 1) * C)
        gates_c = gc_ref[0, rows, :]
        gates_r = gr_ref[0, :, c * C:(c + 1) * C]
        for h in range(B_HEADS):
            cols = slice(h * Dh, (h + 1) * Dh)
            gcum = jnp.broadcast_to(gates_c[:, h:h + 1], (C, Dh))
            beta = jnp.broadcast_to(gates_c[:, B_HEADS + h:B_HEADS + h + 1], (C, Dh))
            gcum_r = jnp.broadcast_to(gates_r[h:h + 1, :], (C, C))
            decay = jnp.where(causal, jnp.exp(jnp.minimum(gcum[:, :C] - gcum_r, 0.0)), 0.0)
            q, k, v = q_ref[0, rows, cols], k_ref[0, rows, cols], v_ref[0, rows, cols]
            qb, kb = q.astype(bf16), k.astype(bf16)
            kk = _nt_dot(kb, kb)
            qk = _nt_dot(qb, kb)
            egc = jnp.exp(gcum)
            g_last = gcum[C - 1:C, :]
            units.append(dict(
                c=c, h=h, rows=rows, cols=cols,
                neg_m=jnp.where(strict, -(beta[:, :C] * kk * decay), 0.0),
                rhs=jnp.concatenate([v * beta, k * (beta * egc)], axis=1),
                q_dec=(q * egc).astype(bf16), intra=(qk * decay).astype(bf16),
                k_tail=(k * jnp.exp(g_last - gcum)).astype(bf16), chunk_decay=jnp.exp(g_last)))
    powers = [un["neg_m"] for un in units]
    t_inv = [eye + p for p in powers]
    for _ in range(5):
        powers = [mmh(p, p) for p in powers]
        t_inv = [t + mmh(p, t) for p, t in zip(powers, t_inv)]
    sols = [mmh(t, un["rhs"]) for t, un in zip(t_inv, units)]
    states = [s_ref[h] for h in range(B_HEADS)]
    for c in range(n_chunks):
        group = [(un, sol) for un, sol in zip(units, sols) if un["c"] == c]
        sbs = [states[un["h"]].astype(bf16) for un, _ in group]
        vbs = [(sol[:, :Dh] - mm(sol[:, Dh:].astype(bf16), sb)).astype(bf16)
               for (un, sol), sb in zip(group, sbs)]
        outs = [mm(un["q_dec"], sb) + mm(un["intra"], vb)
                for (un, _), sb, vb in zip(group, sbs, vbs)]
        for (un, _), vb in zip(group, vbs):
            states[un["h"]] = states[un["h"]] * un["chunk_decay"] + lax.dot_general(
                un["k_tail"], vb, (((0,), (0,)), ((), ())), preferred_element_type=f32)
        for (un, _), o in zip(group, outs):
            o = o * lax.rsqrt(jnp.mean(o * o, axis=-1, keepdims=True) + EPS) * gon_ref[...]
            zz = z_ref[0, un["rows"], un["cols"]]
            o_ref[0, un["rows"], un["cols"]] = o * (zz * _sigmoid(zz))
    for h in range(B_HEADS):
        s_ref[h] = states[h]


def gdn_pallas(qkv, z, ab, abt, conv_w, a_log, dt_bias, g_onorm):
    bsz, seq, _ = qkv.shape
    f32 = jnp.float32
    tm = min(256, seq)
    zero4 = jnp.zeros((B_HEADS,), f32)
    lane_row = lambda v: jnp.pad(jnp.concatenate([v.astype(f32), zero4]), (0, 120)).reshape(1, 128)
    sub_col = lambda v: jnp.concatenate([v.astype(f32), zero4]).reshape(GATE_ROWS, 1)
    tok = lambda w, t=tm: pl.BlockSpec((1, t, w), lambda b, i: (b, i, 0))
    const = lambda shape: pl.BlockSpec(shape, lambda b, i: (0,) * len(shape))
    shp = lambda w: jax.ShapeDtypeStruct((bsz, seq, w), f32)
    q, k, v, gc, gr = pl.pallas_call(
        _gdn_prep_kernel,
        grid=(bsz, seq // tm),
        in_specs=[tok(GDN_COLS),
                  pl.BlockSpec((1, 8, GDN_COLS), lambda b, i: (b, jnp.maximum(i * (tm // 8) - 1, 0), 0)),
                  const((CONV_WIDTH, GDN_COLS)), tok(128),
                  pl.BlockSpec((1, GATE_ROWS, tm), lambda b, i: (b, 0, i)),
                  const((1, 128)), const((1, 128)), const((GATE_ROWS, 1)), const((GATE_ROWS, 1))],
        out_specs=[tok(B_QK), tok(B_QK), tok(B_QK), tok(128),
                   pl.BlockSpec((1, GATE_ROWS, tm), lambda b, i: (b, 0, i))],
        out_shape=[shp(B_QK), shp(B_QK), shp(B_QK), shp(128),
                   jax.ShapeDtypeStruct((bsz, GATE_ROWS, seq), f32)],
        compiler_params=pltpu.CompilerParams(vmem_limit_bytes=VMEM_LIMIT),
        name="gdn_prep",
    )(qkv, qkv, conv_w.astype(f32), ab, abt, lane_row(a_log), lane_row(dt_bias),
      sub_col(a_log), sub_col(dt_bias))
    ts = 2 * CHUNK
    return pl.pallas_call(
        _gdn_main_kernel,
        grid=(bsz, seq // ts),
        in_specs=[tok(B_QK, ts), tok(B_QK, ts), tok(B_QK, ts), tok(B_QK, ts), tok(128, ts),
                  pl.BlockSpec((1, GATE_ROWS, ts), lambda b, i: (b, 0, i)),
                  const((1, B_HEAD_DIM))],
        out_specs=tok(B_QK, ts),
        out_shape=shp(B_QK),
        scratch_shapes=[pltpu.VMEM((B_HEADS, B_HEAD_DIM, B_HEAD_DIM), f32)],
        compiler_params=pltpu.CompilerParams(dimension_semantics=("arbitrary", "arbitrary")),
        name="gdn_main",
    )(q, k, v, z, gc, gr, g_onorm.astype(f32).reshape(1, B_HEAD_DIM))


def _mem_kv_kernel(mem_ref, g_ref, wk_ref, wv_ref, k_ref, v_ref):
    f32, bf16 = jnp.float32, jnp.bfloat16
    mn = _rms(mem_ref[0], g_ref[...]).astype(bf16)
    k = jnp.dot(mn, wk_ref[...], preferred_element_type=f32)
    v = jnp.dot(mn, wv_ref[...], preferred_element_type=f32)
    for h in range(X_HEADS):
        cols = slice(h * X_HEAD_DIM, (h + 1) * X_HEAD_DIM)
        k_ref[0, h] = k[:, cols].astype(bf16)
        v_ref[0, h] = v[:, cols].astype(bf16)


def _mid_kernel(x_ref, oa_ref, ob_ref, wo_ref, gx_ref, wq_ref, k_ref, v_ref, wox_ref, o_ref):
    f32, bf16 = jnp.float32, jnp.bfloat16
    na = oa_ref.shape[2]
    x1 = (x_ref[0]
          + jnp.dot(oa_ref[0].astype(bf16), wo_ref[:na, :], preferred_element_type=f32)
          + jnp.dot(ob_ref[0].astype(bf16), wo_ref[na:, :], preferred_element_type=f32))
    hq = _rms(x1, gx_ref[...]).astype(bf16)
    q = jnp.dot(hq, wq_ref[...], preferred_element_type=f32)
    heads = []
    for h in range(X_HEADS):
        qh = q[:, h * X_HEAD_DIM:(h + 1) * X_HEAD_DIM].astype(bf16)
        lg = _nt_dot(qh, k_ref[0, h]) * (X_HEAD_DIM ** -0.5)
        p = jnp.exp(lg - jnp.max(lg, axis=-1, keepdims=True))
        p = (p / jnp.sum(p, axis=-1, keepdims=True)).astype(bf16)
        heads.append(jnp.dot(p, v_ref[0, h], preferred_element_type=f32).astype(bf16))
    o = jnp.concatenate(heads, axis=1)
    o_ref[0] = x1 + jnp.dot(o, wox_ref[...], preferred_element_type=f32)


def mid_pallas(x, o_a, o_b, w_out, g_cross, mem, g_mem, wq, wk, wv, wo):
    bsz, seq, d = x.shape
    f32, bf16 = jnp.float32, jnp.bfloat16
    hx = X_HEADS * X_HEAD_DIM
    m_len = mem.shape[1]
    const2 = lambda shape: pl.BlockSpec(shape, lambda b: (0,) * len(shape))
    kv_spec = pl.BlockSpec((1, X_HEADS, m_len, X_HEAD_DIM), lambda b: (b, 0, 0, 0))
    kv_shape = jax.ShapeDtypeStruct((bsz, X_HEADS, m_len, X_HEAD_DIM), bf16)
    k, v = pl.pallas_call(
        _mem_kv_kernel,
        grid=(bsz,),
        in_specs=[pl.BlockSpec((1, m_len, d), lambda b: (b, 0, 0)), const2((1, d)),
                  const2((d, hx)), const2((d, hx))],
        out_specs=[kv_spec, kv_spec],
        out_shape=[kv_shape, kv_shape],
        name="mem_kv",
    )(mem, g_mem.reshape(1, d), wk.reshape(d, hx).astype(bf16), wv.reshape(d, hx).astype(bf16))
    tm = min(256, seq)
    tok = lambda w: pl.BlockSpec((1, tm, w), lambda b, t: (b, t, 0))
    const = lambda shape: pl.BlockSpec(shape, lambda b, t: (0,) * len(shape))
    kv_spec2 = pl.BlockSpec((1, X_HEADS, m_len, X_HEAD_DIM), lambda b, t: (b, 0, 0, 0))
    return pl.pallas_call(
        _mid_kernel,
        grid=(bsz, seq // tm),
        in_specs=[tok(d), tok(o_a.shape[2]), tok(o_b.shape[2]), const((MIX_WIDTH, d)),
                  const((1, d)), const((d, hx)), kv_spec2, kv_spec2, const((hx, d))],
        out_specs=tok(d),
        out_shape=jax.ShapeDtypeStruct((bsz, seq, d), f32),
        compiler_params=pltpu.CompilerParams(vmem_limit_bytes=VMEM_LIMIT),
        name="mid",
    )(x, o_a, o_b, w_out.astype(bf16), g_cross.reshape(1, d), wq.reshape(d, hx).astype(bf16),
      k, v, wo.reshape(hx, d).astype(bf16))


def kernel(x, mem, g_mix, w_in, g_cq, g_ckv, g_kidx, w_uq, w_iq, w_uk, w_uv, rel_bias, conv_w, A_log, dt_bias, g_onorm, w_out, g_cross, g_mem, wq_x, wk_x, wv_x, wo_x, g_ffn, w_pq, sub_keys, u_emb, v_emb, g_final):
    bsz, seq, d = x.shape
    for l in range(DEPTH):
        cq, ckv, kw, qkv, z, ab, abt = in_proj_pallas(x, g_mix[l], w_in[l])
        o_a = dsa_pallas(cq, ckv, kw, abt, g_cq[l], g_ckv[l], g_kidx[l],
                           w_uq[l], w_iq[l], w_uk[l], w_uv[l], rel_bias)
        o_b = gdn_pallas(qkv, z, ab, abt, conv_w[l], A_log[l], dt_bias[l], g_onorm[l])
        x = mid_pallas(x, o_a, o_b, w_out[l], g_cross[l], mem, g_mem[l],
                       wq_x[l], wk_x[l], wv_x[l], wo_x[l])
        xf = x.reshape(bsz * seq, d)
        y = peer_pallas(xf, g_ffn[l], w_pq[l], sub_keys[l], u_emb[l], v_emb[l])
        if l + 1 < DEPTH:
            x = (xf + y).reshape(bsz, seq, d)
    return final_residual_rmsnorm(xf, y, g_final).reshape(bsz, seq, d)
```

```python
import math
from functools import partial
import jax
import jax.numpy as jnp
from jax import lax
import numpy as np
from jax.experimental import pallas as pl
from jax.experimental.pallas import tpu as pltpu

DEPTH = 1

CHUNK = 64
Q_BLOCK = 128
EPS = 1e-6

A_HEADS = 8
A_HEAD_DIM = 64
A_Q_LORA = 256
A_KV_LORA = 256
IDX_HEADS = 8
IDX_DIM = 64
IDX_TOPK_MAX = 256
ATTN_SCALE = A_HEAD_DIM ** -0.5
IDX_SCALE = (IDX_HEADS * IDX_DIM) ** -0.5

B_HEADS = 4
B_HEAD_DIM = 128
B_QK = B_HEADS * B_HEAD_DIM
CONV_WIDTH = 4

REL_BUCKETS = 32
REL_MAX_DIST = 128

X_HEADS = 4
X_HEAD_DIM = 128

P_HEADS = 8
N_KEYS = 128
N_EXPERTS = N_KEYS * N_KEYS
P_TOPK = 16
P_QDIM = 256

COL_WIDTHS = (A_Q_LORA, A_KV_LORA, IDX_DIM, IDX_HEADS, B_QK, B_QK, B_QK, B_QK, B_HEADS, B_HEADS)
MIX_WIDTH = A_HEADS * A_HEAD_DIM + B_HEADS * B_HEAD_DIM


def t5_bucket(rel):
    half = REL_BUCKETS // 2
    max_exact = half // 2
    n = jnp.abs(rel)
    nf = jnp.maximum(n, max_exact).astype(jnp.float32)
    large = max_exact + (jnp.log(nf / max_exact) / math.log(REL_MAX_DIST / max_exact)
                         * (half - max_exact)).astype(jnp.int32)
    large = jnp.minimum(large, half - 1)
    return jnp.where(rel > 0, half, 0) + jnp.where(n < max_exact, n, large)


INT_MIN = -2147483648
NEG_BIG = -1e30
KEY_TILE = 512
KEY_PAD = KEY_TILE - Q_BLOCK
VMEM_LIMIT = 56 * 1024 * 1024


def _rms(x, g):
    return x * lax.rsqrt(jnp.mean(x * x, axis=-1, keepdims=True) + EPS) * g


def _nt_dot(a, b):
    return lax.dot_general(a, b, (((1,), (1,)), ((), ())), preferred_element_type=jnp.float32)


def _dsa_prep_kernel(cq_ref, ckv_ref, kw_ref, gcq_ref, gckv_ref, gk_ref,
                       wiqt_ref, wuqt_ref, wuk_ref,
                       at_ref, qlatt_ref, ckvn_ref, kidxn_ref):
    f32, bf16 = jnp.float32, jnp.bfloat16
    Q = Q_BLOCK
    kw = kw_ref[0]
    cqn = _rms(cq_ref[0], gcq_ref[...]).astype(bf16)
    qit = _nt_dot(wiqt_ref[...], cqn)
    qt = _nt_dot(wuqt_ref[...], cqn)
    for h in range(A_HEADS):
        cols = slice(h * Q, (h + 1) * Q)
        at_ref[0, 0, :, cols] = qit[h * IDX_DIM:(h + 1) * IDX_DIM, :].astype(bf16)
        qh = qt[h * A_HEAD_DIM:(h + 1) * A_HEAD_DIM, :].astype(bf16)
        ql = jnp.dot(wuk_ref[h], qh, preferred_element_type=f32) * ATTN_SCALE
        qlatt_ref[0, 0, :, cols] = ql.astype(bf16)
    ckvn_ref[0] = _rms(ckv_ref[0], gckv_ref[...]).astype(bf16)
    kidxn_ref[0] = _rms(kw[:, :IDX_DIM], gk_ref[...]).astype(bf16)


def _dsa_main_kernel(at_ref, qlatt_ref, wt_ref, kidx_ref, ckv_ref, bucket_ref, far_ref, rb_ref,
                     wuvt_ref, o_ref, bias0_ref, sc_ref, big_ref, acc_ref, m_ref, l_ref, *, topk):
    f32, i32, bf16 = jnp.float32, jnp.int32, jnp.bfloat16
    Q, T = Q_BLOCK, KEY_TILE
    i = pl.program_id(1)

    @pl.when(jnp.logical_and(pl.program_id(0) == 0, i == 0))
    def _():
        bucket = bucket_ref[...]
        for h in range(A_HEADS):
            far_bias = rb_ref[far_ref[0], h]
            bias = jnp.zeros((T, Q), f32)
            for b in range(REL_BUCKETS):
                bias = jnp.where(bucket == b, rb_ref[b, h] - far_bias, bias)
            bias0_ref[h] = bias
    e = (i + 1) * Q
    nt = (i + T // Q) // (T // Q)
    kf = jnp.float32(topk)
    wsc = wt_ref[0] * IDX_SCALE
    sub = lax.broadcasted_iota(i32, (T, Q), 0)
    qlane = lax.broadcasted_iota(i32, (T, Q), 1)
    limit = ((i * Q + qlane) // CHUNK + 1) * CHUNK
    at = at_ref[0, 0]

    def tile_start(j):
        return pl.multiple_of(e + KEY_PAD - (j + 1) * T, 128)

    def key_pos(j):
        return tile_start(j) - KEY_PAD + sub

    def score_tile(j, carry):
        kid = kidx_ref[0, pl.ds(tile_start(j), T), :]
        big_ref[...] = jnp.dot(kid, at, preferred_element_type=f32)
        s = jnp.zeros((T, Q), f32)
        for h in range(IDX_HEADS):
            s = s + wsc[h:h + 1, :] * jnp.maximum(big_ref[:, h * Q:(h + 1) * Q], 0.0)
        bits = lax.bitcast_convert_type(s, i32)
        key = jnp.where(bits < 0, bits ^ jnp.int32(0x7FFFFFFF), bits)
        key = jnp.where(s == 0.0, 0, key)
        kpos = key_pos(j)
        key = jnp.where(kpos >= 0, key, INT_MIN)
        sc_ref[j] = jnp.where(kpos < limit, key, INT_MIN)
        return carry

    lax.fori_loop(0, nt, score_tile, 0)

    def count_where(pred):
        def body(j, cnt):
            hit = jnp.where(pred(sc_ref[j], key_pos(j)), 1.0, 0.0)
            return cnt + jnp.sum(hit.reshape(T // 64, 64, Q), axis=0)
        cnt = lax.fori_loop(0, nt, body, jnp.zeros((64, Q), f32))
        return jnp.sum(cnt, axis=0, keepdims=True)

    def bit_body(b, carry):
        u, cacc = carry
        cand = u | lax.shift_left(jnp.int32(1), 31 - b)
        tvec = cand ^ jnp.int32(INT_MIN)
        tot = count_where(lambda k, kpos: k >= tvec)
        ok = tot >= kf
        return jnp.where(ok, cand, u), jnp.where(ok, tot, cacc)

    u, cacc = lax.fori_loop(0, 32, bit_body, (jnp.zeros((1, Q), i32), jnp.zeros((1, Q), f32)))
    thr = jnp.maximum(u ^ jnp.int32(INT_MIN), INT_MIN + 1)
    overflow = jnp.where(u != 0, cacc, 0.0) > kf
    n_over = jnp.max(jnp.where(overflow, 1.0, 0.0), axis=1, keepdims=True)[0, 0]

    @pl.when(n_over > 0.0)
    def _():
        need = kf - count_where(lambda k, kpos: k > thr)

        def cut_body(b, cut):
            cand = cut | lax.shift_left(jnp.int32(1), 14 - b)
            cnt = count_where(lambda k, kpos: jnp.where(k == thr, kpos, cand) < cand)
            return jnp.where(cnt <= need, cand, cut)

        cut = lax.fori_loop(0, 15, cut_body, jnp.zeros((1, Q), i32))

        def drop_tile(j, carry):
            k = sc_ref[j]
            drop = jnp.where(k == thr, key_pos(j), -1) >= cut
            sc_ref[j] = jnp.where(drop, INT_MIN, k)
            return carry

        lax.fori_loop(0, nt, drop_tile, 0)

    m_ref[...] = jnp.full(m_ref.shape, NEG_BIG, f32)
    l_ref[...] = jnp.zeros(l_ref.shape, f32)
    acc_ref[...] = jnp.zeros(acc_ref.shape, f32)
    qlatt = qlatt_ref[0, 0]

    def attn_tile(j, with_bias):
        kv = ckv_ref[0, pl.ds(tile_start(j), T), :]
        kvt = kv.T
        mask_add = jnp.where(sc_ref[j] >= thr, 0.0, NEG_BIG)
        pair_cols = [slice(g * 2 * Q, (g + 1) * 2 * Q) for g in range(A_HEADS // 2)]
        logits = lambda g: jnp.dot(kv, qlatt[:, pair_cols[g]], preferred_element_type=f32)
        x_next = logits(0)
        for g in range(A_HEADS // 2):
            x_pair = x_next
            if g + 1 < A_HEADS // 2:
                x_next = logits(g + 1)
            ps, alphas = [], []
            for u in range(2):
                h = 2 * g + u
                cols = slice(h * Q, (h + 1) * Q)
                x = x_pair[:, u * Q:(u + 1) * Q] + mask_add
                if with_bias:
                    x = x + bias0_ref[h]
                m_prev = m_ref[:, cols]
                m_new = jnp.maximum(m_prev, jnp.max(x, axis=0, keepdims=True))
                p = jnp.exp(x - m_new)
                alpha = jnp.exp(m_prev - m_new)
                l_ref[:, cols] = alpha * l_ref[:, cols] + jnp.sum(p, axis=0, keepdims=True)
                m_ref[:, cols] = m_new
                ps.append(p.astype(bf16))
                alphas.append(alpha)
            pv = jnp.dot(kvt, jnp.concatenate(ps, axis=1), preferred_element_type=f32)
            acc_ref[:, pair_cols[g]] = (
                jnp.concatenate(alphas, axis=1) * acc_ref[:, pair_cols[g]] + pv)

    attn_tile(0, True)

    def attn_body(j, carry):
        attn_tile(j, False)
        return carry

    lax.fori_loop(1, nt, attn_body, 0)

    inv_l = 1.0 / l_ref[...]
    outs = []
    for h in range(A_HEADS):
        cols = slice(h * Q, (h + 1) * Q)
        o_lat_t = (acc_ref[:, cols] * inv_l[:, cols]).astype(bf16)
        outs.append(jnp.dot(wuvt_ref[h], o_lat_t, preferred_element_type=f32))
    o_ref[0] = jnp.concatenate(outs, axis=0).T


def dsa_pallas(cq, ckv, kw, widx_t, g_cq, g_ckv, g_kidx, w_uq, w_iq, w_uk, w_uv, rel_bias):
    bsz, seq, _ = cq.shape
    f32, bf16 = jnp.float32, jnp.bfloat16
    Q, T, H = Q_BLOCK, KEY_TILE, A_HEADS
    nblk = seq // Q
    topk = min(IDX_TOPK_MAX, seq // 4)
    wiqt = w_iq.reshape(A_Q_LORA, IDX_HEADS * IDX_DIM).T.astype(bf16)
    wuqt = w_uq.reshape(A_Q_LORA, H * A_HEAD_DIM).T.astype(bf16)
    wuk = jnp.transpose(w_uk, (1, 0, 2)).astype(bf16)
    wuvt = jnp.transpose(w_uv, (1, 2, 0)).astype(bf16)
    tok = lambda w: pl.BlockSpec((1, Q, w), lambda b, t: (b, t, 0))
    full = lambda shape: pl.BlockSpec(shape, lambda b, t: (0,) * len(shape))
    blk = lambda r: pl.BlockSpec((1, 1, r, H * Q), lambda b, t: (b, t, 0, 0))
    a_t, qlat_t, ckvn, kidxn = pl.pallas_call(
        _dsa_prep_kernel,
        grid=(bsz, nblk),
        in_specs=[tok(A_Q_LORA), tok(A_KV_LORA), tok(128),
                  full((1, A_Q_LORA)), full((1, A_KV_LORA)), full((1, IDX_DIM)),
                  full(wiqt.shape), full(wuqt.shape), full(wuk.shape)],
        out_specs=[blk(IDX_DIM), blk(A_KV_LORA), tok(A_KV_LORA), tok(IDX_DIM)],
        out_shape=[jax.ShapeDtypeStruct((bsz, nblk, IDX_DIM, H * Q), bf16),
                   jax.ShapeDtypeStruct((bsz, nblk, A_KV_LORA, H * Q), bf16),
                   jax.ShapeDtypeStruct((bsz, seq, A_KV_LORA), bf16),
                   jax.ShapeDtypeStruct((bsz, seq, IDX_DIM), bf16)],
        name="dsa_prep",
    )(cq, ckv, kw, g_cq.reshape(1, -1), g_ckv.reshape(1, -1), g_kidx.reshape(1, -1),
      wiqt, wuqt, wuk)
    ckvp = jnp.pad(ckvn, ((0, 0), (KEY_PAD, 0), (0, 0)))
    kidxp = jnp.pad(kidxn, ((0, 0), (KEY_PAD, 0), (0, 0)))
    rel = (jnp.arange(T, dtype=jnp.int32)[:, None] - jnp.arange(Q, dtype=jnp.int32)[None, :]
           - KEY_PAD)
    bucket0 = t5_bucket(rel)
    bucket_far = t5_bucket(jnp.full((1,), -KEY_PAD - 1, jnp.int32))
    smem = pl.BlockSpec(memory_space=pltpu.SMEM)
    nt_max = (nblk - 1 + T // Q) // (T // Q)
    skey = seq + KEY_PAD
    return pl.pallas_call(
        partial(_dsa_main_kernel, topk=topk),
        grid=(bsz, nblk),
        in_specs=[blk(IDX_DIM), blk(A_KV_LORA),
                  pl.BlockSpec((1, IDX_HEADS, Q), lambda b, t: (b, 1, t)),
                  pl.BlockSpec((1, skey, IDX_DIM), lambda b, t: (b, 0, 0)),
                  pl.BlockSpec((1, skey, A_KV_LORA), lambda b, t: (b, 0, 0)),
                  full((T, Q)), smem, smem, full(wuvt.shape)],
        out_specs=tok(H * A_HEAD_DIM),
        out_shape=jax.ShapeDtypeStruct((bsz, seq, H * A_HEAD_DIM), f32),
        scratch_shapes=[pltpu.VMEM((H, T, Q), f32),
                        pltpu.VMEM((nt_max, T, Q), jnp.int32),
                        pltpu.VMEM((T, H * Q), f32),
                        pltpu.VMEM((A_KV_LORA, H * Q), f32),
                        pltpu.VMEM((1, H * Q), f32),
                        pltpu.VMEM((1, H * Q), f32)],
        compiler_params=pltpu.CompilerParams(
            dimension_semantics=("arbitrary", "arbitrary"), vmem_limit_bytes=VMEM_LIMIT),
        name="dsa_main",
    )(a_t, qlat_t, widx_t, kidxp, ckvp, bucket0, bucket_far, rel_bias.astype(f32), wuvt)


PEER_SCORE_TOKENS = 256
PEER_GATHER_TOKENS = 64
PEER_SLOTS = P_HEADS * P_TOPK
WORDS_PER_ROW = 4
PEER_TILES = 16


def _top16(s, order=None, payload=None):
    if order is None:
        order = lax.broadcasted_iota(jnp.int32, s.shape, 0).astype(jnp.float32)
    vals, picks = [], []
    for _ in range(P_TOPK):
        m = jnp.max(s, axis=0, keepdims=True)
        pos = jnp.min(jnp.where(s == m, order, float(N_EXPERTS)), axis=0, keepdims=True)
        hit = order == pos
        vals.append(m)
        if payload is None:
            picks.append(pos)
        else:
            picks.append(jnp.max(jnp.where(hit, payload, -1.0), axis=0, keepdims=True))
        s = jnp.where(hit, -jnp.inf, s)
    return vals, picks


def _staircase_candidates(v1, i1, v2, i2):
    v1m, i1m = jnp.concatenate(v1, axis=0), jnp.concatenate(i1, axis=0)
    v2m, i2m = jnp.concatenate(v2, axis=0), jnp.concatenate(i2, axis=0)
    t = v1m.shape[1]
    sub8 = lax.broadcasted_iota(jnp.int32, (8, t), 0).astype(jnp.float32)
    cand, cidx, rank = [], [], []
    for a in range(8):
        cand.append(v1[a] + v2m[:8])
        cidx.append(i1[a] * float(N_KEYS) + i2m[:8])
        rank.append(sub8 + float(a * P_TOPK))
    cand.append(v1[0] + v2m[8:])
    cidx.append(i1[0] * float(N_KEYS) + i2m[8:])
    rank.append(sub8 + 8.0)
    cand.append(v1m[8:] + v2[0])
    cidx.append(i1m[8:] * float(N_KEYS) + i2[0])
    rank.append((sub8 + 8.0) * float(P_TOPK))
    cat = lambda xs: jnp.concatenate(xs, axis=0)
    return cat(cand), cat(rank), cat(cidx)


def _peer_score_kernel(x_ref, g_ref, wpqt_ref, sk_ref, hn_ref, eidx_ref, gate_ref):
    f32, bf16 = jnp.float32, jnp.bfloat16
    hb = _rms(x_ref[...], g_ref[...]).astype(bf16)
    hn_ref[...] = hb
    qrt = _nt_dot(wpqt_ref[...], hb)
    half = P_QDIM // 2
    e_rows, g_rows = [], []
    for hd in range(P_HEADS):
        tops = []
        for p in range(2):
            qhp = qrt[(hd * 2 + p) * half:(hd * 2 + p + 1) * half, :].astype(bf16)
            s = jnp.dot(sk_ref[hd * 2 + p], qhp, preferred_element_type=f32)
            tops.append(_top16(s))
        (v1, i1), (v2, i2) = tops
        best, be = _top16(*_staircase_candidates(v1, i1, v2, i2))
        ex = [jnp.exp(b - best[0]) for b in best]
        den = ex[0]
        for k in range(1, P_TOPK):
            den = den + ex[k]
        inv = 1.0 / den
        e_rows += be
        g_rows += [x * inv for x in ex]
    eidx_ref[...] = (jnp.concatenate(e_rows, axis=0).T * float(WORDS_PER_ROW)).astype(jnp.int32)
    gate_ref[...] = jnp.concatenate(g_rows, axis=0).T


def _diag_mask():
    r = lax.broadcasted_iota(jnp.int32, (8, PEER_SLOTS * 8), 0)
    c = lax.broadcasted_iota(jnp.int32, (8, PEER_SLOTS * 8), 1)
    return (c & 7) == r


def _gather_pair(idx_ref, tab_ref, g_refs, t0):
    views = [idx_ref.at[t0 + u] for u in range(2)]
    for r in range(PEER_SLOTS):
        for view, g_ref in zip(views, g_refs):
            row0 = pl.multiple_of(view[r], WORDS_PER_ROW)
            g_ref[r * WORDS_PER_ROW:(r + 1) * WORDS_PER_ROW, :] = tab_ref[
                pl.ds(row0, WORDS_PER_ROW), :]


def _gather_pipeline(idx_ref, tab_ref, tiles, n_tokens, consume):
    pairs = len(tiles) // 2
    for g_ref in tiles[-2:]:
        g_ref[...] = jnp.zeros(g_ref.shape, g_ref.dtype)

    def trip(i, carry):
        for k in range(pairs):
            pair = pairs * i + k
            prev_tiles = tiles[2 * k - 2:2 * k] if k else tiles[-2:]
            _gather_pair(idx_ref, tab_ref, tiles[2 * k:2 * k + 2], pair * 2)
            for u in range(2):
                consume(prev_tiles[u], jnp.maximum(pair - 1, 0), u)
        return carry

    lax.fori_loop(0, n_tokens // (2 * pairs), trip, 0)
    for u in range(2):
        consume(tiles[-2 + u], n_tokens // 2 - 1, u)


def _peer_act_kernel(idx_ref, hn_ref, tab_ref, a_ref, m_ref, *tiles):
    f32, bf16 = jnp.float32, jnp.bfloat16
    tb = a_ref.shape[0]

    def dots(g_ref, tp, u):
        hp = hn_ref[pl.ds(pl.multiple_of(tp * 16, 16), 16), :]
        m = _nt_dot(hp, pltpu.bitcast(g_ref[...], bf16))
        m_ref[pl.ds(pl.multiple_of((tp * 2 + u) * 8, 8), 8), :] = m[u * 8:(u + 1) * 8, :]

    _gather_pipeline(idx_ref, tab_ref, tiles, tb, dots)
    m3 = m_ref[...].reshape(tb, 8, PEER_SLOTS * 8)
    z = jnp.sum(jnp.where(_diag_mask()[None], m3, 0.0), axis=1)
    rr = lax.broadcasted_iota(jnp.int32, (PEER_SLOTS * 8, PEER_SLOTS), 0)
    cc = lax.broadcasted_iota(jnp.int32, (PEER_SLOTS * 8, PEER_SLOTS), 1)
    pool = jnp.where((rr >> 3) == cc, 1.0, 0.0).astype(bf16)
    z_hi = z.astype(bf16)
    z_lo = (z - z_hi.astype(f32)).astype(bf16)
    a_ref[...] = (jnp.dot(z_hi, pool, preferred_element_type=f32)
                  + jnp.dot(z_lo, pool, preferred_element_type=f32))


def _peer_out_kernel(idx_ref, a_ref, gate_ref, tab_ref, o_ref, w_ref, *tiles):
    f32, bf16 = jnp.float32, jnp.bfloat16
    tb = a_ref.shape[0]
    a = a_ref[...]
    act = 0.5 * a * (1.0 + lax.erf(a * (2.0 ** -0.5)))
    wgt = (gate_ref[...] * act).astype(bf16)
    rr = lax.broadcasted_iota(jnp.int32, (PEER_SLOTS, PEER_SLOTS * 8), 0)
    cc = lax.broadcasted_iota(jnp.int32, (PEER_SLOTS, PEER_SLOTS * 8), 1)
    expand = jnp.where((cc >> 3) == rr, 1.0, 0.0).astype(bf16)
    w_ref[...] = jnp.dot(wgt, expand, preferred_element_type=f32)
    diag = _diag_mask()

    def combine(g_ref, tp, u):
        t = tp * 2 + u
        wrow = jnp.broadcast_to(w_ref[pl.ds(t, 1), :], (8, PEER_SLOTS * 8))
        wsel = jnp.where(diag, wrow, 0.0).astype(bf16)
        o_ref[pl.ds(pl.multiple_of(t * 8, 8), 8), :] = jnp.dot(
            wsel, pltpu.bitcast(g_ref[...], bf16), preferred_element_type=f32)

    _gather_pipeline(idx_ref, tab_ref, tiles, tb, combine)


PACK_ROWS = 512


def _pack_table_kernel(x_ref, o_ref):
    rows = x_ref.shape[0]
    bf16_bits = lambda v: lax.bitcast_convert_type(
        v.astype(jnp.bfloat16).astype(jnp.float32), jnp.int32)
    for c in range(WORDS_PER_ROW):
        lo = bf16_bits(x_ref[:, (2 * c) * 128:(2 * c + 1) * 128])
        hi = bf16_bits(x_ref[:, (2 * c + 1) * 128:(2 * c + 2) * 128])
        word = (hi & jnp.int32(-65536)) | lax.shift_right_logical(lo, 16)
        o_ref[pl.ds(c, rows, stride=WORDS_PER_ROW), :] = word


def _pack_table(tab):
    n_e, d = tab.shape
    return pl.pallas_call(
        _pack_table_kernel,
        grid=(n_e // PACK_ROWS,),
        in_specs=[pl.BlockSpec((PACK_ROWS, d), lambda i: (i, 0))],
        out_specs=pl.BlockSpec((PACK_ROWS * WORDS_PER_ROW, 128), lambda i: (i, 0)),
        out_shape=jax.ShapeDtypeStruct((n_e * WORDS_PER_ROW, 128), jnp.int32),
        name="pack_table",
    )(tab)


def peer_pallas(x, g_ffn, w_pq, sub_keys, u_emb, v_emb):
    n_tok, d = x.shape
    f32, bf16 = jnp.float32, jnp.bfloat16
    ts, tg = PEER_SCORE_TOKENS, PEER_GATHER_TOKENS
    wpqt = w_pq.reshape(d, P_HEADS * P_QDIM).T.astype(bf16)
    sk = sub_keys.reshape(P_HEADS * 2, N_KEYS, P_QDIM // 2).astype(bf16)
    hn, eidx, gate = pl.pallas_call(
        _peer_score_kernel,
        grid=(n_tok // ts,),
        in_specs=[pl.BlockSpec((ts, d), lambda i: (i, 0)),
                  pl.BlockSpec((1, d), lambda i: (0, 0)),
                  pl.BlockSpec(wpqt.shape, lambda i: (0, 0)),
                  pl.BlockSpec(sk.shape, lambda i: (0, 0, 0))],
        out_specs=[pl.BlockSpec((ts, d), lambda i: (i, 0)),
                   pl.BlockSpec((ts, PEER_SLOTS), lambda i: (i, 0)),
                   pl.BlockSpec((ts, PEER_SLOTS), lambda i: (i, 0))],
        out_shape=[jax.ShapeDtypeStruct((n_tok, d), bf16),
                   jax.ShapeDtypeStruct((n_tok, PEER_SLOTS), jnp.int32),
                   jax.ShapeDtypeStruct((n_tok, PEER_SLOTS), f32)],
        compiler_params=pltpu.CompilerParams(vmem_limit_bytes=VMEM_LIMIT),
        name="peer_score",
    )(x, g_ffn.reshape(1, d), wpqt, sk)
    utab, vtab = _pack_table(u_emb), _pack_table(v_emb)
    idx_spec = pl.BlockSpec((tg, PEER_SLOTS), lambda i: (i, 0), memory_space=pltpu.SMEM)
    tab_spec = pl.BlockSpec(memory_space=pltpu.VMEM)
    slot_spec = pl.BlockSpec((tg, PEER_SLOTS), lambda i: (i, 0))
    row_spec = pl.BlockSpec((tg * 8, 128), lambda i: (i, 0))
    gbuf = pltpu.VMEM((PEER_SLOTS * WORDS_PER_ROW, 128), jnp.int32)
    cparams = pltpu.CompilerParams(vmem_limit_bytes=VMEM_LIMIT)
    act = pl.pallas_call(
        _peer_act_kernel,
        grid=(n_tok // tg,),
        in_specs=[idx_spec, row_spec, tab_spec],
        out_specs=slot_spec,
        out_shape=jax.ShapeDtypeStruct((n_tok, PEER_SLOTS), f32),
        scratch_shapes=[pltpu.VMEM((tg * 8, PEER_SLOTS * 8), f32)] + [gbuf] * PEER_TILES,
        compiler_params=cparams,
        name="peer_act",
    )(eidx, hn.reshape(n_tok * 8, 128), utab)
    out = pl.pallas_call(
        _peer_out_kernel,
        grid=(n_tok // tg,),
        in_specs=[idx_spec, slot_spec, slot_spec, tab_spec],
        out_specs=row_spec,
        out_shape=jax.ShapeDtypeStruct((n_tok * 8, 128), f32),
        scratch_shapes=[pltpu.VMEM((tg, PEER_SLOTS * 8), f32)] + [gbuf] * PEER_TILES,
        compiler_params=cparams,
        name="peer_out",
    )(eidx, act, gate, vtab)
    return out.reshape(n_tok, d)


def _final_kernel(x_ref, y_ref, g_ref, o_ref):
    o_ref[...] = _rms(x_ref[...] + y_ref[...], g_ref[...])


def final_residual_rmsnorm(x, y, g):
    n, d = x.shape
    tm = 512
    row = pl.BlockSpec((tm, d), lambda i: (i, 0))
    return pl.pallas_call(
        _final_kernel,
        grid=(n // tm,),
        in_specs=[row, row, pl.BlockSpec((1, d), lambda i: (0, 0))],
        out_specs=row,
        out_shape=jax.ShapeDtypeStruct((n, d), x.dtype),
        name="final_rmsnorm",
    )(x, y, g.reshape(1, d))


IN_PROJ_TOKENS = 512
GDN_COLS = 3 * B_QK
GATE_ROWS = 8


def _in_proj_kernel(x_ref, g_ref, wa_ref, wkw_ref, wqkv_ref, wz_ref, wab_ref, wabt_ref,
                    cq_ref, ckv_ref, kw_ref, qkv_ref, z_ref, ab_ref, abt_ref):
    f32 = jnp.float32
    hb = _rms(x_ref[0], g_ref[...]).astype(jnp.bfloat16)
    a = jnp.dot(hb, wa_ref[...], preferred_element_type=f32)
    cq_ref[0] = a[:, :A_Q_LORA]
    ckv_ref[0] = a[:, A_Q_LORA:]
    kw_ref[0] = jnp.dot(hb, wkw_ref[...], preferred_element_type=f32)
    qkv_ref[0] = jnp.dot(hb, wqkv_ref[...], preferred_element_type=f32)
    z_ref[0] = jnp.dot(hb, wz_ref[...], preferred_element_type=f32)
    ab_ref[0] = jnp.dot(hb, wab_ref[...], preferred_element_type=f32)
    abt_ref[0] = _nt_dot(wabt_ref[...], hb)


def in_proj_pallas(x, g_mix, w_in):
    bsz, seq, d = x.shape
    f32, bf16 = jnp.float32, jnp.bfloat16
    tm = min(IN_PROJ_TOKENS, seq)
    o = np.cumsum((0,) + COL_WIDTHS)
    wb = w_in.astype(bf16)
    pad_cols = lambda w: jnp.pad(w, ((0, 0), (0, 128 - w.shape[1])))
    wa = wb[:, o[0]:o[2]]
    wkw = pad_cols(wb[:, o[2]:o[4]])
    wqkv = wb[:, o[4]:o[7]]
    wz = wb[:, o[7]:o[8]]
    wab = pad_cols(wb[:, o[8]:o[10]])
    wabt = jnp.concatenate([wb[:, o[8]:o[10]], wb[:, o[3]:o[4]]], axis=1).T
    full = lambda w: pl.BlockSpec(w.shape, lambda b, t: (0, 0))
    tok = lambda w: pl.BlockSpec((1, tm, w), lambda b, t: (b, t, 0))
    shp = lambda w: jax.ShapeDtypeStruct((bsz, seq, w), f32)
    return pl.pallas_call(
        _in_proj_kernel,
        grid=(bsz, seq // tm),
        in_specs=[tok(d), pl.BlockSpec((1, d), lambda b, t: (0, 0)),
                  full(wa), full(wkw), full(wqkv), full(wz), full(wab), full(wabt)],
        out_specs=[tok(A_Q_LORA), tok(A_KV_LORA), tok(128), tok(GDN_COLS), tok(B_QK), tok(128),
                   pl.BlockSpec((1, GATE_ROWS + IDX_HEADS, tm), lambda b, t: (b, 0, t))],
        out_shape=[shp(A_Q_LORA), shp(A_KV_LORA), shp(128), shp(GDN_COLS), shp(B_QK), shp(128),
                   jax.ShapeDtypeStruct((bsz, GATE_ROWS + IDX_HEADS, seq), f32)],
        compiler_params=pltpu.CompilerParams(vmem_limit_bytes=VMEM_LIMIT),
        name="in_proj",
    )(x, g_mix.reshape(1, d), wa, wkw, wqkv, wz, wab, wabt)


def _softplus(x):
    return jnp.maximum(x, 0.0) + jnp.log1p(jnp.exp(-jnp.abs(x)))


def _sigmoid(x):
    return 1.0 / (1.0 + jnp.exp(-x))


def _gdn_gates(pre, a_log, dt_bias, is_decay):
    g = -jnp.exp(a_log) * _softplus(pre + dt_bias)
    return jnp.where(is_decay, g, _sigmoid(pre))


def _gdn_prep_kernel(qkv_ref, halo_ref, cw_ref, ab_ref, abt_ref, alc_ref, dtc_ref, alr_ref, dtr_ref,
                     q_ref, k_ref, v_ref, gc_ref, gr_ref):
    tm = qkv_ref.shape[1]
    x = qkv_ref[0]
    halo = jnp.where(pl.program_id(1) > 0, halo_ref[0], 0.0)
    full = jnp.concatenate([halo, x], axis=0)
    y = x * cw_ref[CONV_WIDTH - 1:CONV_WIDTH, :]
    for back in range(1, CONV_WIDTH):
        shifted = pltpu.roll(full, back, axis=0)[8:, :]
        y = y + shifted * cw_ref[CONV_WIDTH - 1 - back:CONV_WIDTH - back, :]
    y = y * _sigmoid(y)
    for h in range(B_HEADS):
        cols = slice(h * B_HEAD_DIM, (h + 1) * B_HEAD_DIM)
        qh = y[:, h * B_HEAD_DIM:(h + 1) * B_HEAD_DIM]
        kh = y[:, B_QK + h * B_HEAD_DIM:B_QK + (h + 1) * B_HEAD_DIM]
        q_ref[0, :, cols] = qh * lax.rsqrt(
            jnp.sum(qh * qh, axis=-1, keepdims=True) + EPS) * (B_HEAD_DIM ** -0.5)
        k_ref[0, :, cols] = kh * lax.rsqrt(jnp.sum(kh * kh, axis=-1, keepdims=True) + EPS)
    v_ref[0] = y[:, 2 * B_QK:]
    lane = lax.broadcasted_iota(jnp.int32, (tm, 128), 1)
    gates_c = _gdn_gates(ab_ref[0], alc_ref[...], dtc_ref[...], lane < B_HEADS)
    row = lax.broadcasted_iota(jnp.int32, (GATE_ROWS, tm), 0)
    gates_r = _gdn_gates(abt_ref[0], alr_ref[...], dtr_ref[...], row < B_HEADS)
    ti = lax.broadcasted_iota(jnp.int32, (tm, tm), 0)
    tj = lax.broadcasted_iota(jnp.int32, (tm, tm), 1)
    same_chunk = (ti // CHUNK) == (tj // CHUNK)
    hi = lax.Precision.HIGHEST
    lower = jnp.where(same_chunk & (tj <= ti), 1.0, 0.0)
    upper = jnp.where(same_chunk & (ti <= tj), 1.0, 0.0)
    cum_c = jnp.dot(lower, gates_c, preferred_element_type=jnp.float32, precision=hi)
    cum_r = jnp.dot(gates_r, upper, preferred_element_type=jnp.float32, precision=hi)
    gc_ref[0] = jnp.where(lane < B_HEADS, cum_c, gates_c)
    gr_ref[0] = jnp.where(row < B_HEADS, cum_r, gates_r)


def _gdn_main_kernel(q_ref, k_ref, v_ref, z_ref, gc_ref, gr_ref, gon_ref, o_ref, s_ref):
    f32, bf16 = jnp.float32, jnp.bfloat16
    C, Dh = CHUNK, B_HEAD_DIM
    hi = lax.Precision.HIGHEST

    @pl.when(pl.program_id(1) == 0)
    def _():
        s_ref[...] = jnp.zeros(s_ref.shape, f32)

    ii = lax.broadcasted_iota(jnp.int32, (C, C), 0)
    jj = lax.broadcasted_iota(jnp.int32, (C, C), 1)
    causal = ii >= jj
    strict = ii > jj
    eye = jnp.where(ii == jj, 1.0, 0.0)
    mm = lambda a, b: jnp.dot(a, b, preferred_element_type=f32)
    mmh = lambda a, b: jnp.dot(a, b, preferred_element_type=f32, precision=hi)
    n_chunks = q_ref.shape[1] // C
    units = []
    for c in range(n_chunks):
        rows = slice(c * C, (c + 1) * C)
        gates_c = gc_ref[0, rows, :]
        gates_r = gr_ref[0, :, c * C:(c + 1) * C]
        for h in range(B_HEADS):
            cols = slice(h * Dh, (h + 1) * Dh)
            gcum = jnp.broadcast_to(gates_c[:, h:h + 1], (C, Dh))
            beta = jnp.broadcast_to(gates_c[:, B_HEADS + h:B_HEADS + h + 1], (C, Dh))
            gcum_r = jnp.broadcast_to(gates_r[h:h + 1, :], (C, C))
            decay = jnp.where(causal, jnp.exp(jnp.minimum(gcum[:, :C] - gcum_r, 0.0)), 0.0)
            q, k, v = q_ref[0, rows, cols], k_ref[0, rows, cols], v_ref[0, rows, cols]
            qb, kb = q.astype(bf16), k.astype(bf16)
            kk = _nt_dot(kb, kb)
            qk = _nt_dot(qb, kb)
            egc = jnp.exp(gcum)
            g_last = gcum[C - 1:C, :]
            units.append(dict(
                c=c, h=h, rows=rows, cols=cols,
                neg_m=jnp.where(strict, -(beta[:, :C] * kk * decay), 0.0),
                rhs=jnp.concatenate([v * beta, k * (beta * egc)], axis=1),
                q_dec=(q * egc).astype(bf16), intra=(qk * decay).astype(bf16),
                k_tail=(k * jnp.exp(g_last - gcum)).astype(bf16), chunk_decay=jnp.exp(g_last)))
    powers = [un["neg_m"] for un in units]
    t_inv = [eye + p for p in powers]
    for _ in range(5):
        powers = [mmh(p, p) for p in powers]
        t_inv = [t + mmh(p, t) for p, t in zip(powers, t_inv)]
    sols = [mmh(t, un["rhs"]) for t, un in zip(t_inv, units)]
    states = [s_ref[h] for h in range(B_HEADS)]
    for c in range(n_chunks):
        group = [(un, sol) for un, sol in zip(units, sols) if un["c"] == c]
        sbs = [states[un["h"]].astype(bf16) for un, _ in group]
        vbs = [(sol[:, :Dh] - mm(sol[:, Dh:].astype(bf16), sb)).astype(bf16)
               for (un, sol), sb in zip(group, sbs)]
        outs = [mm(un["q_dec"], sb) + mm(un["intra"], vb)
                for (un, _), sb, vb in zip(group, sbs, vbs)]
        for (un, _), vb in zip(group, vbs):
            states[un["h"]] = states[un["h"]] * un["chunk_decay"] + lax.dot_general(
                un["k_tail"], vb, (((0,), (0,)), ((), ())), preferred_element_type=f32)
        for (un, _), o in zip(group, outs):
            o = o * lax.rsqrt(jnp.mean(o * o, axis=-1, keepdims=True) + EPS) * gon_ref[...]
            zz = z_ref[0, un["rows"], un["cols"]]
            o_ref[0, un["rows"], un["cols"]] = o * (zz * _sigmoid(zz))
    for h in range(B_HEADS):
        s_ref[h] = states[h]


def gdn_pallas(qkv, z, ab, abt, conv_w, a_log, dt_bias, g_onorm):
    bsz, seq, _ = qkv.shape
    f32 = jnp.float32
    tm = min(256, seq)
    zero4 = jnp.zeros((B_HEADS,), f32)
    lane_row = lambda v: jnp.pad(jnp.concatenate([v.astype(f32), zero4]), (0, 120)).reshape(1, 128)
    sub_col = lambda v: jnp.concatenate([v.astype(f32), zero4]).reshape(GATE_ROWS, 1)
    tok = lambda w, t=tm: pl.BlockSpec((1, t, w), lambda b, i: (b, i, 0))
    const = lambda shape: pl.BlockSpec(shape, lambda b, i: (0,) * len(shape))
    shp = lambda w: jax.ShapeDtypeStruct((bsz, seq, w), f32)
    q, k, v, gc, gr = pl.pallas_call(
        _gdn_prep_kernel,
        grid=(bsz, seq // tm),
        in_specs=[tok(GDN_COLS),
                  pl.BlockSpec((1, 8, GDN_COLS), lambda b, i: (b, jnp.maximum(i * (tm // 8) - 1, 0), 0)),
                  const((CONV_WIDTH, GDN_COLS)), tok(128),
                  pl.BlockSpec((1, GATE_ROWS, tm), lambda b, i: (b, 0, i)),
                  const((1, 128)), const((1, 128)), const((GATE_ROWS, 1)), const((GATE_ROWS, 1))],
        out_specs=[tok(B_QK), tok(B_QK), tok(B_QK), tok(128),
                   pl.BlockSpec((1, GATE_ROWS, tm), lambda b, i: (b, 0, i))],
        out_shape=[shp(B_QK), shp(B_QK), shp(B_QK), shp(128),
                   jax.ShapeDtypeStruct((bsz, GATE_ROWS, seq), f32)],
        compiler_params=pltpu.CompilerParams(vmem_limit_bytes=VMEM_LIMIT),
        name="gdn_prep",
    )(qkv, qkv, conv_w.astype(f32), ab, abt, lane_row(a_log), lane_row(dt_bias),
      sub_col(a_log), sub_col(dt_bias))
    ts = 2 * CHUNK
    return pl.pallas_call(
        _gdn_main_kernel,
        grid=(bsz, seq // ts),
        in_specs=[tok(B_QK, ts), tok(B_QK, ts), tok(B_QK, ts), tok(B_QK, ts), tok(128, ts),
                  pl.BlockSpec((1, GATE_ROWS, ts), lambda b, i: (b, 0, i)),
                  const((1, B_HEAD_DIM))],
        out_specs=tok(B_QK, ts),
        out_shape=shp(B_QK),
        scratch_shapes=[pltpu.VMEM((B_HEADS, B_HEAD_DIM, B_HEAD_DIM), f32)],
        compiler_params=pltpu.CompilerParams(dimension_semantics=("arbitrary", "arbitrary")),
        name="gdn_main",
    )(q, k, v, z, gc, gr, g_onorm.astype(f32).reshape(1, B_HEAD_DIM))


def _mem_kv_kernel(mem_ref, g_ref, wk_ref, wv_ref, k_ref, v_ref):
    f32, bf16 = jnp.float32, jnp.bfloat16
    mn = _rms(mem_ref[0], g_ref[...]).astype(bf16)
    k = jnp.dot(mn, wk_ref[...], preferred_element_type=f32)
    v = jnp.dot(mn, wv_ref[...], preferred_element_type=f32)
    for h in range(X_HEADS):
        cols = slice(h * X_HEAD_DIM, (h + 1) * X_HEAD_DIM)
        k_ref[0, h] = k[:, cols].astype(bf16)
        v_ref[0, h] = v[:, cols].astype(bf16)


def _mid_kernel(x_ref, oa_ref, ob_ref, wo_ref, gx_ref, wq_ref, k_ref, v_ref, wox_ref, o_ref):
    f32, bf16 = jnp.float32, jnp.bfloat16
    na = oa_ref.shape[2]
    x1 = (x_ref[0]
          + jnp.dot(oa_ref[0].astype(bf16), wo_ref[:na, :], preferred_element_type=f32)
          + jnp.dot(ob_ref[0].astype(bf16), wo_ref[na:, :], preferred_element_type=f32))
    hq = _rms(x1, gx_ref[...]).astype(bf16)
    q = jnp.dot(hq, wq_ref[...], preferred_element_type=f32)
    heads = []
    for h in range(X_HEADS):
        qh = q[:, h * X_HEAD_DIM:(h + 1) * X_HEAD_DIM].astype(bf16)
        lg = _nt_dot(qh, k_ref[0, h]) * (X_HEAD_DIM ** -0.5)
        p = jnp.exp(lg - jnp.max(lg, axis=-1, keepdims=True))
        p = (p / jnp.sum(p, axis=-1, keepdims=True)).astype(bf16)
        heads.append(jnp.dot(p, v_ref[0, h], preferred_element_type=f32).astype(bf16))
    o = jnp.concatenate(heads, axis=1)
    o_ref[0] = x1 + jnp.dot(o, wox_ref[...], preferred_element_type=f32)


def mid_pallas(x, o_a, o_b, w_out, g_cross, mem, g_mem, wq, wk, wv, wo):
    bsz, seq, d = x.shape
    f32, bf16 = jnp.float32, jnp.bfloat16
    hx = X_HEADS * X_HEAD_DIM
    m_len = mem.shape[1]
    const2 = lambda shape: pl.BlockSpec(shape, lambda b: (0,) * len(shape))
    kv_spec = pl.BlockSpec((1, X_HEADS, m_len, X_HEAD_DIM), lambda b: (b, 0, 0, 0))
    kv_shape = jax.ShapeDtypeStruct((bsz, X_HEADS, m_len, X_HEAD_DIM), bf16)
    k, v = pl.pallas_call(
        _mem_kv_kernel,
        grid=(bsz,),
        in_specs=[pl.BlockSpec((1, m_len, d), lambda b: (b, 0, 0)), const2((1, d)),
                  const2((d, hx)), const2((d, hx))],
        out_specs=[kv_spec, kv_spec],
        out_shape=[kv_shape, kv_shape],
        name="mem_kv",
    )(mem, g_mem.reshape(1, d), wk.reshape(d, hx).astype(bf16), wv.reshape(d, hx).astype(bf16))
    tm = min(256, seq)
    tok = lambda w: pl.BlockSpec((1, tm, w), lambda b, t: (b, t, 0))
    const = lambda shape: pl.BlockSpec(shape, lambda b, t: (0,) * len(shape))
    kv_spec2 = pl.BlockSpec((1, X_HEADS, m_len, X_HEAD_DIM), lambda b, t: (b, 0, 0, 0))
    return pl.pallas_call(
        _mid_kernel,
        grid=(bsz, seq // tm),
        in_specs=[tok(d), tok(o_a.shape[2]), tok(o_b.shape[2]), const((MIX_WIDTH, d)),
                  const((1, d)), const((d, hx)), kv_spec2, kv_spec2, const((hx, d))],
        out_specs=tok(d),
        out_shape=jax.ShapeDtypeStruct((bsz, seq, d), f32),
        compiler_params=pltpu.CompilerParams(vmem_limit_bytes=VMEM_LIMIT),
        name="mid",
    )(x, o_a, o_b, w_out.astype(bf16), g_cross.reshape(1, d), wq.reshape(d, hx).astype(bf16),
      k, v, wo.reshape(hx, d).astype(bf16))


def kernel(x, mem, g_mix, w_in, g_cq, g_ckv, g_kidx, w_uq, w_iq, w_uk, w_uv, rel_bias, conv_w, A_log, dt_bias, g_onorm, w_out, g_cross, g_mem, wq_x, wk_x, wv_x, wo_x, g_ffn, w_pq, sub_keys, u_emb, v_emb, g_final):
    bsz, seq, d = x.shape
    for l in range(DEPTH):
        cq, ckv, kw, qkv, z, ab, abt = in_proj_pallas(x, g_mix[l], w_in[l])
        o_a = dsa_pallas(cq, ckv, kw, abt, g_cq[l], g_ckv[l], g_kidx[l],
                           w_uq[l], w_iq[l], w_uk[l], w_uv[l], rel_bias)
        o_b = gdn_pallas(qkv, z, ab, abt, conv_w[l], A_log[l], dt_bias[l], g_onorm[l])
        x = mid_pallas(x, o_a, o_b, w_out[l], g_cross[l], mem, g_mem[l],
                       wq_x[l], wk_x[l], wv_x[l], wo_x[l])
        xf = x.reshape(bsz * seq, d)
        y = peer_pallas(xf, g_ffn[l], w_pq[l], sub_keys[l], u_emb[l], v_emb[l])
        if l + 1 < DEPTH:
            x = (xf + y).reshape(bsz, seq, d)
    return final_residual_rmsnorm(xf, y, g_final).reshape(bsz, seq, d)
```

```python
import math
from functools import partial
import jax
import jax.numpy as jnp
from jax import lax
import numpy as np
from jax.experimental import pallas as pl
from jax.experimental.pallas import tpu as pltpu

DEPTH = 1

CHUNK = 64
Q_BLOCK = 128
EPS = 1e-6

A_HEADS = 8
A_HEAD_DIM = 64
A_Q_LORA = 256
A_KV_LORA = 256
IDX_HEADS = 8
IDX_DIM = 64
IDX_TOPK_MAX = 256
ATTN_SCALE = A_HEAD_DIM ** -0.5
IDX_SCALE = (IDX_HEADS * IDX_DIM) ** -0.5

B_HEADS = 4
B_HEAD_DIM = 128
B_QK = B_HEADS * B_HEAD_DIM
CONV_WIDTH = 4

REL_BUCKETS = 32
REL_MAX_DIST = 128

X_HEADS = 4
X_HEAD_DIM = 128

P_HEADS = 8
N_KEYS = 128
N_EXPERTS = N_KEYS * N_KEYS
P_TOPK = 16
P_QDIM = 256

COL_WIDTHS = (A_Q_LORA, A_KV_LORA, IDX_DIM, IDX_HEADS, B_QK, B_QK, B_QK, B_QK, B_HEADS, B_HEADS)
MIX_WIDTH = A_HEADS * A_HEAD_DIM + B_HEADS * B_HEAD_DIM


def t5_bucket(rel):
    half = REL_BUCKETS // 2
    max_exact = half // 2
    n = jnp.abs(rel)
    nf = jnp.maximum(n, max_exact).astype(jnp.float32)
    large = max_exact + (jnp.log(nf / max_exact) / math.log(REL_MAX_DIST / max_exact)
                         * (half - max_exact)).astype(jnp.int32)
    large = jnp.minimum(large, half - 1)
    return jnp.where(rel > 0, half, 0) + jnp.where(n < max_exact, n, large)


INT_MIN = -2147483648
NEG_BIG = -1e30
KEY_TILE = 512
KEY_PAD = KEY_TILE - Q_BLOCK
VMEM_LIMIT = 56 * 1024 * 1024


def _rms(x, g):
    return x * lax.rsqrt(jnp.mean(x * x, axis=-1, keepdims=True) + EPS) * g


def _nt_dot(a, b):
    return lax.dot_general(a, b, (((1,), (1,)), ((), ())), preferred_element_type=jnp.float32)


def _dsa_prep_kernel(cq_ref, ckv_ref, kw_ref, gcq_ref, gckv_ref, gk_ref,
                       wiqt_ref, wuqt_ref, wuk_ref,
                       at_ref, qlatt_ref, ckvn_ref, kidxn_ref):
    f32, bf16 = jnp.float32, jnp.bfloat16
    Q = Q_BLOCK
    kw = kw_ref[0]
    cqn = _rms(cq_ref[0], gcq_ref[...]).astype(bf16)
    qit = _nt_dot(wiqt_ref[...], cqn)
    qt = _nt_dot(wuqt_ref[...], cqn)
    for h in range(A_HEADS):
        cols = slice(h * Q, (h + 1) * Q)
        at_ref[0, 0, :, cols] = qit[h * IDX_DIM:(h + 1) * IDX_DIM, :].astype(bf16)
        qh = qt[h * A_HEAD_DIM:(h + 1) * A_HEAD_DIM, :].astype(bf16)
        ql = jnp.dot(wuk_ref[h], qh, preferred_element_type=f32) * ATTN_SCALE
        qlatt_ref[0, 0, :, cols] = ql.astype(bf16)
    ckvn_ref[0] = _rms(ckv_ref[0], gckv_ref[...]).astype(bf16)
    kidxn_ref[0] = _rms(kw[:, :IDX_DIM], gk_ref[...]).astype(bf16)


def _dsa_main_kernel(at_ref, qlatt_ref, wt_ref, kidx_ref, ckv_ref, bucket_ref, far_ref, rb_ref,
                     wuvt_ref, o_ref, bias0_ref, sc_ref, big_ref, acc_ref, m_ref, l_ref, *, topk):
    f32, i32, bf16 = jnp.float32, jnp.int32, jnp.bfloat16
    Q, T = Q_BLOCK, KEY_TILE
    i = pl.program_id(1)

    @pl.when(jnp.logical_and(pl.program_id(0) == 0, i == 0))
    def _():
        bucket = bucket_ref[...]
        for h in range(A_HEADS):
            far_bias = rb_ref[far_ref[0], h]
            bias = jnp.zeros((T, Q), f32)
            for b in range(REL_BUCKETS):
                bias = jnp.where(bucket == b, rb_ref[b, h] - far_bias, bias)
            bias0_ref[h] = bias
    e = (i + 1) * Q
    nt = (i + T // Q) // (T // Q)
    kf = jnp.float32(topk)
    wsc = wt_ref[0] * IDX_SCALE
    sub = lax.broadcasted_iota(i32, (T, Q), 0)
    qlane = lax.broadcasted_iota(i32, (T, Q), 1)
    limit = ((i * Q + qlane) // CHUNK + 1) * CHUNK
    at = at_ref[0, 0]

    def tile_start(j):
        return pl.multiple_of(e + KEY_PAD - (j + 1) * T, 128)

    def key_pos(j):
        return tile_start(j) - KEY_PAD + sub

    def score_tile(j, carry):
        kid = kidx_ref[0, pl.ds(tile_start(j), T), :]
        big_ref[...] = jnp.dot(kid, at, preferred_element_type=f32)
        s = jnp.zeros((T, Q), f32)
        for h in range(IDX_HEADS):
            s = s + wsc[h:h + 1, :] * jnp.maximum(big_ref[:, h * Q:(h + 1) * Q], 0.0)
        bits = lax.bitcast_convert_type(s, i32)
        key = jnp.where(bits < 0, bits ^ jnp.int32(0x7FFFFFFF), bits)
        key = jnp.where(s == 0.0, 0, key)
        kpos = key_pos(j)
        key = jnp.where(kpos >= 0, key, INT_MIN)
        sc_ref[j] = jnp.where(kpos < limit, key, INT_MIN)
        return carry

    lax.fori_loop(0, nt, score_tile, 0)

    def count_where(pred):
        def body(j, cnt):
            hit = jnp.where(pred(sc_ref[j], key_pos(j)), 1.0, 0.0)
            return cnt + jnp.sum(hit.reshape(T // 64, 64, Q), axis=0)
        cnt = lax.fori_loop(0, nt, body, jnp.zeros((64, Q), f32))
        return jnp.sum(cnt, axis=0, keepdims=True)

    def bit_body(b, carry):
        u, cacc = carry
        cand = u | lax.shift_left(jnp.int32(1), 31 - b)
        tvec = cand ^ jnp.int32(INT_MIN)
        tot = count_where(lambda k, kpos: k >= tvec)
        ok = tot >= kf
        return jnp.where(ok, cand, u), jnp.where(ok, tot, cacc)

    u, cacc = lax.fori_loop(0, 32, bit_body, (jnp.zeros((1, Q), i32), jnp.zeros((1, Q), f32)))
    thr = jnp.maximum(u ^ jnp.int32(INT_MIN), INT_MIN + 1)
    overflow = jnp.where(u != 0, cacc, 0.0) > kf
    n_over = jnp.max(jnp.where(overflow, 1.0, 0.0), axis=1, keepdims=True)[0, 0]

    @pl.when(n_over > 0.0)
    def _():
        need = kf - count_where(lambda k, kpos: k > thr)

        def cut_body(b, cut):
            cand = cut | lax.shift_left(jnp.int32(1), 14 - b)
            cnt = count_where(lambda k, kpos: jnp.where(k == thr, kpos, cand) < cand)
            return jnp.where(cnt <= need, cand, cut)

        cut = lax.fori_loop(0, 15, cut_body, jnp.zeros((1, Q), i32))

        def drop_tile(j, carry):
            k = sc_ref[j]
            drop = jnp.where(k == thr, key_pos(j), -1) >= cut
            sc_ref[j] = jnp.where(drop, INT_MIN, k)
            return carry

        lax.fori_loop(0, nt, drop_tile, 0)

    m_ref[...] = jnp.full(m_ref.shape, NEG_BIG, f32)
    l_ref[...] = jnp.zeros(l_ref.shape, f32)
    acc_ref[...] = jnp.zeros(acc_ref.shape, f32)
    qlatt = qlatt_ref[0, 0]

    def attn_tile(j, with_bias):
        kv = ckv_ref[0, pl.ds(tile_start(j), T), :]
        kvt = kv.T
        mask_add = jnp.where(sc_ref[j] >= thr, 0.0, NEG_BIG)
        pair_cols = [slice(g * 2 * Q, (g + 1) * 2 * Q) for g in range(A_HEADS // 2)]
        logits = lambda g: jnp.dot(kv, qlatt[:, pair_cols[g]], preferred_element_type=f32)
        x_next = logits(0)
        for g in range(A_HEADS // 2):
            x_pair = x_next
            if g + 1 < A_HEADS // 2:
                x_next = logits(g + 1)
            ps, alphas = [], []
            for u in range(2):
                h = 2 * g + u
                cols = slice(h * Q, (h + 1) * Q)
                x = x_pair[:, u * Q:(u + 1) * Q] + mask_add
                if with_bias:
                    x = x + bias0_ref[h]
                m_prev = m_ref[:, cols]
                m_new = jnp.maximum(m_prev, jnp.max(x, axis=0, keepdims=True))
                p = jnp.exp(x - m_new)
                alpha = jnp.exp(m_prev - m_new)
                l_ref[:, cols] = alpha * l_ref[:, cols] + jnp.sum(p, axis=0, keepdims=True)
                m_ref[:, cols] = m_new
                ps.append(p.astype(bf16))
                alphas.append(alpha)
            pv = jnp.dot(kvt, jnp.concatenate(ps, axis=1), preferred_element_type=f32)
            acc_ref[:, pair_cols[g]] = (
                jnp.concatenate(alphas, axis=1) * acc_ref[:, pair_cols[g]] + pv)

    attn_tile(0, True)

    def attn_body(j, carry):
        attn_tile(j, False)
        return carry

    lax.fori_loop(1, nt, attn_body, 0)

    inv_l = 1.0 / l_ref[...]
    outs = []
    for h in range(A_HEADS):
        cols = slice(h * Q, (h + 1) * Q)
        o_lat_t = (acc_ref[:, cols] * inv_l[:, cols]).astype(bf16)
        outs.append(jnp.dot(wuvt_ref[h], o_lat_t, preferred_element_type=f32))
    o_ref[0] = jnp.concatenate(outs, axis=0).T


def dsa_pallas(cq, ckv, kw, widx_t, g_cq, g_ckv, g_kidx, w_uq, w_iq, w_uk, w_uv, rel_bias):
    bsz, seq, _ = cq.shape
    f32, bf16 = jnp.float32, jnp.bfloat16
    Q, T, H = Q_BLOCK, KEY_TILE, A_HEADS
    nblk = seq // Q
    topk = min(IDX_TOPK_MAX, seq // 4)
    wiqt = w_iq.reshape(A_Q_LORA, IDX_HEADS * IDX_DIM).T.astype(bf16)
    wuqt = w_uq.reshape(A_Q_LORA, H * A_HEAD_DIM).T.astype(bf16)
    wuk = jnp.transpose(w_uk, (1, 0, 2)).astype(bf16)
    wuvt = jnp.transpose(w_uv, (1, 2, 0)).astype(bf16)
    tok = lambda w: pl.BlockSpec((1, Q, w), lambda b, t: (b, t, 0))
    full = lambda shape: pl.BlockSpec(shape, lambda b, t: (0,) * len(shape))
    blk = lambda r: pl.BlockSpec((1, 1, r, H * Q), lambda b, t: (b, t, 0, 0))
    a_t, qlat_t, ckvn, kidxn = pl.pallas_call(
        _dsa_prep_kernel,
        grid=(bsz, nblk),
        in_specs=[tok(A_Q_LORA), tok(A_KV_LORA), tok(128),
                  full((1, A_Q_LORA)), full((1, A_KV_LORA)), full((1, IDX_DIM)),
                  full(wiqt.shape), full(wuqt.shape), full(wuk.shape)],
        out_specs=[blk(IDX_DIM), blk(A_KV_LORA), tok(A_KV_LORA), tok(IDX_DIM)],
        out_shape=[jax.ShapeDtypeStruct((bsz, nblk, IDX_DIM, H * Q), bf16),
                   jax.ShapeDtypeStruct((bsz, nblk, A_KV_LORA, H * Q), bf16),
                   jax.ShapeDtypeStruct((bsz, seq, A_KV_LORA), bf16),
                   jax.ShapeDtypeStruct((bsz, seq, IDX_DIM), bf16)],
        name="dsa_prep",
    )(cq, ckv, kw, g_cq.reshape(1, -1), g_ckv.reshape(1, -1), g_kidx.reshape(1, -1),
      wiqt, wuqt, wuk)
    ckvp = jnp.pad(ckvn, ((0, 0), (KEY_PAD, 0), (0, 0)))
    kidxp = jnp.pad(kidxn, ((0, 0), (KEY_PAD, 0), (0, 0)))
    rel = (jnp.arange(T, dtype=jnp.int32)[:, None] - jnp.arange(Q, dtype=jnp.int32)[None, :]
           - KEY_PAD)
    bucket0 = t5_bucket(rel)
    bucket_far = t5_bucket(jnp.full((1,), -KEY_PAD - 1, jnp.int32))
    smem = pl.BlockSpec(memory_space=pltpu.SMEM)
    nt_max = (nblk - 1 + T // Q) // (T // Q)
    skey = seq + KEY_PAD
    return pl.pallas_call(
        partial(_dsa_main_kernel, topk=topk),
        grid=(bsz, nblk),
        in_specs=[blk(IDX_DIM), blk(A_KV_LORA),
                  pl.BlockSpec((1, IDX_HEADS, Q), lambda b, t: (b, 1, t)),
                  pl.BlockSpec((1, skey, IDX_DIM), lambda b, t: (b, 0, 0)),
                  pl.BlockSpec((1, skey, A_KV_LORA), lambda b, t: (b, 0, 0)),
                  full((T, Q)), smem, smem, full(wuvt.shape)],
        out_specs=tok(H * A_HEAD_DIM),
        out_shape=jax.ShapeDtypeStruct((bsz, seq, H * A_HEAD_DIM), f32),
        scratch_shapes=[pltpu.VMEM((H, T, Q), f32),
                        pltpu.VMEM((nt_max, T, Q), jnp.int32),
                        pltpu.VMEM((T, H * Q), f32),
                        pltpu.VMEM((A_KV_LORA, H * Q), f32),
                        pltpu.VMEM((1, H * Q), f32),
                        pltpu.VMEM((1, H * Q), f32)],
        compiler_params=pltpu.CompilerParams(
            dimension_semantics=("arbitrary", "arbitrary"), vmem_limit_bytes=VMEM_LIMIT),
        name="dsa_main",
    )(a_t, qlat_t, widx_t, kidxp, ckvp, bucket0, bucket_far, rel_bias.astype(f32), wuvt)


PEER_SCORE_TOKENS = 256
PEER_GATHER_TOKENS = 64
PEER_SLOTS = P_HEADS * P_TOPK
WORDS_PER_ROW = 4
PEER_TILES = 16


def _top16(s, order=None, payload=None):
    if order is None:
        order = lax.broadcasted_iota(jnp.int32, s.shape, 0).astype(jnp.float32)
    vals, picks = [], []
    for _ in range(P_TOPK):
        m = jnp.max(s, axis=0, keepdims=True)
        pos = jnp.min(jnp.where(s == m, order, float(N_EXPERTS)), axis=0, keepdims=True)
        hit = order == pos
        vals.append(m)
        if payload is None:
            picks.append(pos)
        else:
            picks.append(jnp.max(jnp.where(hit, payload, -1.0), axis=0, keepdims=True))
        s = jnp.where(hit, -jnp.inf, s)
    return vals, picks


def _staircase_candidates(v1, i1, v2, i2):
    v1m, i1m = jnp.concatenate(v1, axis=0), jnp.concatenate(i1, axis=0)
    v2m, i2m = jnp.concatenate(v2, axis=0), jnp.concatenate(i2, axis=0)
    t = v1m.shape[1]
    sub8 = lax.broadcasted_iota(jnp.int32, (8, t), 0).astype(jnp.float32)
    cand, cidx, rank = [], [], []
    for a in range(8):
        cand.append(v1[a] + v2m[:8])
        cidx.append(i1[a] * float(N_KEYS) + i2m[:8])
        rank.append(sub8 + float(a * P_TOPK))
    cand.append(v1[0] + v2m[8:])
    cidx.append(i1[0] * float(N_KEYS) + i2m[8:])
    rank.append(sub8 + 8.0)
    cand.append(v1m[8:] + v2[0])
    cidx.append(i1m[8:] * float(N_KEYS) + i2[0])
    rank.append((sub8 + 8.0) * float(P_TOPK))
    cat = lambda xs: jnp.concatenate(xs, axis=0)
    return cat(cand), cat(rank), cat(cidx)


def _peer_score_kernel(x_ref, g_ref, wpqt_ref, sk_ref, hn_ref, eidx_ref, gate_ref):
    f32, bf16 = jnp.float32, jnp.bfloat16
    h = _rms(x_ref[...], g_ref[...])
    hb = h.astype(bf16)
    for s in range(h.shape[1] // 128):
        hn_ref[pl.ds(s, h.shape[0], stride=8), :] = h[:, s * 128:(s + 1) * 128]
    qrt = _nt_dot(wpqt_ref[...], hb)
    half = P_QDIM // 2
    e_rows, g_rows = [], []
    for hd in range(P_HEADS):
        tops = []
        for p in range(2):
            qhp = qrt[(hd * 2 + p) * half:(hd * 2 + p + 1) * half, :].astype(bf16)
            s = jnp.dot(sk_ref[hd * 2 + p], qhp, preferred_element_type=f32)
            tops.append(_top16(s))
        (v1, i1), (v2, i2) = tops
        best, be = _top16(*_staircase_candidates(v1, i1, v2, i2))
        ex = [jnp.exp(b - best[0]) for b in best]
        den = ex[0]
        for k in range(1, P_TOPK):
            den = den + ex[k]
        inv = 1.0 / den
        e_rows += be
        g_rows += [x * inv for x in ex]
    eidx_ref[...] = (jnp.concatenate(e_rows, axis=0).T * float(WORDS_PER_ROW)).astype(jnp.int32)
    gate_ref[...] = jnp.concatenate(g_rows, axis=0).T


def _diag_mask():
    r = lax.broadcasted_iota(jnp.int32, (8, PEER_SLOTS * 8), 0)
    c = lax.broadcasted_iota(jnp.int32, (8, PEER_SLOTS * 8), 1)
    return (c & 7) == r


def _gather_pair(idx_ref, tab_ref, g_refs, t0):
    views = [idx_ref.at[t0 + u] for u in range(2)]
    for r in range(PEER_SLOTS):
        for view, g_ref in zip(views, g_refs):
            row0 = pl.multiple_of(view[r], WORDS_PER_ROW)
            g_ref[r * WORDS_PER_ROW:(r + 1) * WORDS_PER_ROW, :] = tab_ref[
                pl.ds(row0, WORDS_PER_ROW), :]


def _gather_pipeline(idx_ref, tab_ref, tiles, n_tokens, consume):
    pairs = len(tiles) // 2
    for g_ref in tiles[-2:]:
        g_ref[...] = jnp.zeros(g_ref.shape, g_ref.dtype)

    def trip(i, carry):
        for k in range(pairs):
            pair = pairs * i + k
            prev_tiles = tiles[2 * k - 2:2 * k] if k else tiles[-2:]
            _gather_pair(idx_ref, tab_ref, tiles[2 * k:2 * k + 2], pair * 2)
            for u in range(2):
                consume(prev_tiles[u], jnp.maximum(pair - 1, 0), u)
        return carry

    lax.fori_loop(0, n_tokens // (2 * pairs), trip, 0)
    for u in range(2):
        consume(tiles[-2 + u], n_tokens // 2 - 1, u)


def _peer_act_kernel(idx_ref, hn_ref, tab_ref, a_ref, m_ref, *tiles):
    f32, bf16 = jnp.float32, jnp.bfloat16
    tb = a_ref.shape[0]

    def dots(g_ref, tp, u):
        hp = hn_ref[pl.ds(pl.multiple_of(tp * 16, 16), 16), :].astype(bf16)
        m = _nt_dot(hp, pltpu.bitcast(g_ref[...], bf16))
        m_ref[pl.ds(pl.multiple_of((tp * 2 + u) * 8, 8), 8), :] = m[u * 8:(u + 1) * 8, :]

    _gather_pipeline(idx_ref, tab_ref, tiles, tb, dots)
    m3 = m_ref[...].reshape(tb, 8, PEER_SLOTS * 8)
    z = jnp.sum(jnp.where(_diag_mask()[None], m3, 0.0), axis=1)
    rr = lax.broadcasted_iota(jnp.int32, (PEER_SLOTS * 8, PEER_SLOTS), 0)
    cc = lax.broadcasted_iota(jnp.int32, (PEER_SLOTS * 8, PEER_SLOTS), 1)
    pool = jnp.where((rr >> 3) == cc, 1.0, 0.0).astype(bf16)
    z_hi = z.astype(bf16)
    z_lo = (z - z_hi.astype(f32)).astype(bf16)
    a_ref[...] = (jnp.dot(z_hi, pool, preferred_element_type=f32)
                  + jnp.dot(z_lo, pool, preferred_element_type=f32))


def _peer_out_kernel(idx_ref, a_ref, gate_ref, tab_ref, o_ref, w_ref, *tiles):
    f32, bf16 = jnp.float32, jnp.bfloat16
    tb = a_ref.shape[0]
    a = a_ref[...]
    act = 0.5 * a * (1.0 + lax.erf(a * (2.0 ** -0.5)))
    wgt = (gate_ref[...] * act).astype(bf16)
    rr = lax.broadcasted_iota(jnp.int32, (PEER_SLOTS, PEER_SLOTS * 8), 0)
    cc = lax.broadcasted_iota(jnp.int32, (PEER_SLOTS, PEER_SLOTS * 8), 1)
    expand = jnp.where((cc >> 3) == rr, 1.0, 0.0).astype(bf16)
    w_ref[...] = jnp.dot(wgt, expand, preferred_element_type=f32)
    diag = _diag_mask()

    def combine(g_ref, tp, u):
        t = tp * 2 + u
        wrow = jnp.broadcast_to(w_ref[pl.ds(t, 1), :], (8, PEER_SLOTS * 8))
        wsel = jnp.where(diag, wrow, 0.0).astype(bf16)
        o_ref[pl.ds(pl.multiple_of(t * 8, 8), 8), :] = jnp.dot(
            wsel, pltpu.bitcast(g_ref[...], bf16), preferred_element_type=f32)

    _gather_pipeline(idx_ref, tab_ref, tiles, tb, combine)


PACK_ROWS = 512


def _pack_table_kernel(x_ref, o_ref):
    rows = x_ref.shape[0]
    bf16_bits = lambda v: lax.bitcast_convert_type(
        v.astype(jnp.bfloat16).astype(jnp.float32), jnp.int32)
    for c in range(WORDS_PER_ROW):
        lo = bf16_bits(x_ref[:, (2 * c) * 128:(2 * c + 1) * 128])
        hi = bf16_bits(x_ref[:, (2 * c + 1) * 128:(2 * c + 2) * 128])
        word = (hi & jnp.int32(-65536)) | lax.shift_right_logical(lo, 16)
        o_ref[pl.ds(c, rows, stride=WORDS_PER_ROW), :] = word


def _pack_table(tab):
    n_e, d = tab.shape
    return pl.pallas_call(
        _pack_table_kernel,
        grid=(n_e // PACK_ROWS,),
        in_specs=[pl.BlockSpec((PACK_ROWS, d), lambda i: (i, 0))],
        out_specs=pl.BlockSpec((PACK_ROWS * WORDS_PER_ROW, 128), lambda i: (i, 0)),
        out_shape=jax.ShapeDtypeStruct((n_e * WORDS_PER_ROW, 128), jnp.int32),
        name="pack_table",
    )(tab)


def peer_pallas(x, g_ffn, w_pq, sub_keys, u_emb, v_emb):
    n_tok, d = x.shape
    f32, bf16 = jnp.float32, jnp.bfloat16
    ts, tg = PEER_SCORE_TOKENS, PEER_GATHER_TOKENS
    wpqt = w_pq.reshape(d, P_HEADS * P_QDIM).T.astype(bf16)
    sk = sub_keys.reshape(P_HEADS * 2, N_KEYS, P_QDIM // 2).astype(bf16)
    hn, eidx, gate = pl.pallas_call(
        _peer_score_kernel,
        grid=(n_tok // ts,),
        in_specs=[pl.BlockSpec((ts, d), lambda i: (i, 0)),
                  pl.BlockSpec((1, d), lambda i: (0, 0)),
                  pl.BlockSpec(wpqt.shape, lambda i: (0, 0)),
                  pl.BlockSpec(sk.shape, lambda i: (0, 0, 0))],
        out_specs=[pl.BlockSpec((ts * 8, 128), lambda i: (i, 0)),
                   pl.BlockSpec((ts, PEER_SLOTS), lambda i: (i, 0)),
                   pl.BlockSpec((ts, PEER_SLOTS), lambda i: (i, 0))],
        out_shape=[jax.ShapeDtypeStruct((n_tok * 8, 128), f32),
                   jax.ShapeDtypeStruct((n_tok, PEER_SLOTS), jnp.int32),
                   jax.ShapeDtypeStruct((n_tok, PEER_SLOTS), f32)],
        compiler_params=pltpu.CompilerParams(vmem_limit_bytes=VMEM_LIMIT),
        name="peer_score",
    )(x, g_ffn.reshape(1, d), wpqt, sk)
    utab, vtab = _pack_table(u_emb), _pack_table(v_emb)
    idx_spec = pl.BlockSpec((tg, PEER_SLOTS), lambda i: (i, 0), memory_space=pltpu.SMEM)
    tab_spec = pl.BlockSpec(memory_space=pltpu.VMEM)
    slot_spec = pl.BlockSpec((tg, PEER_SLOTS), lambda i: (i, 0))
    row_spec = pl.BlockSpec((tg * 8, 128), lambda i: (i, 0))
    gbuf = pltpu.VMEM((PEER_SLOTS * WORDS_PER_ROW, 128), jnp.int32)
    cparams = pltpu.CompilerParams(vmem_limit_bytes=VMEM_LIMIT)
    act = pl.pallas_call(
        _peer_act_kernel,
        grid=(n_tok // tg,),
        in_specs=[idx_spec, row_spec, tab_spec],
        out_specs=slot_spec,
        out_shape=jax.ShapeDtypeStruct((n_tok, PEER_SLOTS), f32),
        scratch_shapes=[pltpu.VMEM((tg * 8, PEER_SLOTS * 8), f32)] + [gbuf] * PEER_TILES,
        compiler_params=cparams,
        name="peer_act",
    )(eidx, hn, utab)
    out = pl.pallas_call(
        _peer_out_kernel,
        grid=(n_tok // tg,),
        in_specs=[idx_spec, slot_spec, slot_spec, tab_spec],
        out_specs=row_spec,
        out_shape=jax.ShapeDtypeStruct((n_tok * 8, 128), f32),
        scratch_shapes=[pltpu.VMEM((tg, PEER_SLOTS * 8), f32)] + [gbuf] * PEER_TILES,
        compiler_params=cparams,
        name="peer_out",
    )(eidx, act, gate, vtab)
    return out


def _token_rows(y_ref):
    tm = y_ref.shape[0] // 8
    return jnp.concatenate([y_ref[pl.ds(s, tm, stride=8), :] for s in range(8)], axis=1)


def _final_kernel(x_ref, y_ref, g_ref, o_ref):
    o_ref[...] = _rms(x_ref[...] + _token_rows(y_ref), g_ref[...])


def _residual_kernel(x_ref, y_ref, o_ref):
    o_ref[...] = x_ref[...] + _token_rows(y_ref)


def residual_pallas(x, y8, g=None):
    n, d = x.shape
    tm = 512
    row = pl.BlockSpec((tm, d), lambda i: (i, 0))
    row8 = pl.BlockSpec((tm * 8, 128), lambda i: (i, 0))
    if g is None:
        kern, extra, extra_specs, name = _residual_kernel, (), [], "residual"
    else:
        kern, extra, name = _final_kernel, (g.reshape(1, d),), "final_rmsnorm"
        extra_specs = [pl.BlockSpec((1, d), lambda i: (0, 0))]
    return pl.pallas_call(
        kern,
        grid=(n // tm,),
        in_specs=[row, row8] + extra_specs,
        out_specs=row,
        out_shape=jax.ShapeDtypeStruct((n, d), x.dtype),
        name=name,
    )(x, y8, *extra)


IN_PROJ_TOKENS = 512
GDN_COLS = 3 * B_QK
GATE_ROWS = 8


def _in_proj_kernel(x_ref, g_ref, wa_ref, wkw_ref, wqkv_ref, wz_ref, wab_ref, wabt_ref,
                    cq_ref, ckv_ref, kw_ref, qkv_ref, z_ref, ab_ref, abt_ref):
    f32 = jnp.float32
    hb = _rms(x_ref[0], g_ref[...]).astype(jnp.bfloat16)
    a = jnp.dot(hb, wa_ref[...], preferred_element_type=f32)
    cq_ref[0] = a[:, :A_Q_LORA]
    ckv_ref[0] = a[:, A_Q_LORA:]
    kw_ref[0] = jnp.dot(hb, wkw_ref[...], preferred_element_type=f32)
    qkv_ref[0] = jnp.dot(hb, wqkv_ref[...], preferred_element_type=f32)
    z_ref[0] = jnp.dot(hb, wz_ref[...], preferred_element_type=f32)
    ab_ref[0] = jnp.dot(hb, wab_ref[...], preferred_element_type=f32)
    abt_ref[0] = _nt_dot(wabt_ref[...], hb)


def in_proj_pallas(x, g_mix, w_in):
    bsz, seq, d = x.shape
    f32, bf16 = jnp.float32, jnp.bfloat16
    tm = min(IN_PROJ_TOKENS, seq)
    o = np.cumsum((0,) + COL_WIDTHS)
    wb = w_in.astype(bf16)
    pad_cols = lambda w: jnp.pad(w, ((0, 0), (0, 128 - w.shape[1])))
    wa = wb[:, o[0]:o[2]]
    wkw = pad_cols(wb[:, o[2]:o[4]])
    wqkv = wb[:, o[4]:o[7]]
    wz = wb[:, o[7]:o[8]]
    wab = pad_cols(wb[:, o[8]:o[10]])
    wabt = jnp.concatenate([wb[:, o[8]:o[10]], wb[:, o[3]:o[4]]], axis=1).T
    full = lambda w: pl.BlockSpec(w.shape, lambda b, t: (0, 0))
    tok = lambda w: pl.BlockSpec((1, tm, w), lambda b, t: (b, t, 0))
    shp = lambda w: jax.ShapeDtypeStruct((bsz, seq, w), f32)
    return pl.pallas_call(
        _in_proj_kernel,
        grid=(bsz, seq // tm),
        in_specs=[tok(d), pl.BlockSpec((1, d), lambda b, t: (0, 0)),
                  full(wa), full(wkw), full(wqkv), full(wz), full(wab), full(wabt)],
        out_specs=[tok(A_Q_LORA), tok(A_KV_LORA), tok(128), tok(GDN_COLS), tok(B_QK), tok(128),
                   pl.BlockSpec((1, GATE_ROWS + IDX_HEADS, tm), lambda b, t: (b, 0, t))],
        out_shape=[shp(A_Q_LORA), shp(A_KV_LORA), shp(128), shp(GDN_COLS), shp(B_QK), shp(128),
                   jax.ShapeDtypeStruct((bsz, GATE_ROWS + IDX_HEADS, seq), f32)],
        compiler_params=pltpu.CompilerParams(vmem_limit_bytes=VMEM_LIMIT),
        name="in_proj",
    )(x, g_mix.reshape(1, d), wa, wkw, wqkv, wz, wab, wabt)


def _softplus(x):
    return jnp.maximum(x, 0.0) + jnp.log1p(jnp.exp(-jnp.abs(x)))


def _sigmoid(x):
    return 1.0 / (1.0 + jnp.exp(-x))


def _gdn_gates(pre, a_log, dt_bias, is_decay):
    g = -jnp.exp(a_log) * _softplus(pre + dt_bias)
    return jnp.where(is_decay, g, _sigmoid(pre))


def _gdn_prep_kernel(qkv_ref, halo_ref, cw_ref, ab_ref, abt_ref, alc_ref, dtc_ref, alr_ref, dtr_ref,
                     q_ref, k_ref, v_ref, gc_ref, gr_ref):
    tm = qkv_ref.shape[1]
    x = qkv_ref[0]
    halo = jnp.where(pl.program_id(1) > 0, halo_ref[0], 0.0)
    full = jnp.concatenate([halo, x], axis=0)
    y = x * cw_ref[CONV_WIDTH - 1:CONV_WIDTH, :]
    for back in range(1, CONV_WIDTH):
        shifted = pltpu.roll(full, back, axis=0)[8:, :]
        y = y + shifted * cw_ref[CONV_WIDTH - 1 - back:CONV_WIDTH - back, :]
    y = y * _sigmoid(y)
    for h in range(B_HEADS):
        cols = slice(h * B_HEAD_DIM, (h + 1) * B_HEAD_DIM)
        qh = y[:, h * B_HEAD_DIM:(h + 1) * B_HEAD_DIM]
        kh = y[:, B_QK + h * B_HEAD_DIM:B_QK + (h + 1) * B_HEAD_DIM]
        q_ref[0, :, cols] = qh * lax.rsqrt(
            jnp.sum(qh * qh, axis=-1, keepdims=True) + EPS) * (B_HEAD_DIM ** -0.5)
        k_ref[0, :, cols] = kh * lax.rsqrt(jnp.sum(kh * kh, axis=-1, keepdims=True) + EPS)
    v_ref[0] = y[:, 2 * B_QK:]
    lane = lax.broadcasted_iota(jnp.int32, (tm, 128), 1)
    gates_c = _gdn_gates(ab_ref[0], alc_ref[...], dtc_ref[...], lane < B_HEADS)
    row = lax.broadcasted_iota(jnp.int32, (GATE_ROWS, tm), 0)
    gates_r = _gdn_gates(abt_ref[0], alr_ref[...], dtr_ref[...], row < B_HEADS)
    ti = lax.broadcasted_iota(jnp.int32, (tm, tm), 0)
    tj = lax.broadcasted_iota(jnp.int32, (tm, tm), 1)
    same_chunk = (ti // CHUNK) == (tj // CHUNK)
    hi = lax.Precision.HIGHEST
    lower = jnp.where(same_chunk & (tj <= ti), 1.0, 0.0)
    upper = jnp.where(same_chunk & (ti <= tj), 1.0, 0.0)
    cum_c = jnp.dot(lower, gates_c, preferred_element_type=jnp.float32, precision=hi)
    cum_r = jnp.dot(gates_r, upper, preferred_element_type=jnp.float32, precision=hi)
    gc_ref[0] = jnp.where(lane < B_HEADS, cum_c, gates_c)
    gr_ref[0] = jnp.where(row < B_HEADS, cum_r, gates_r)


def _gdn_main_kernel(q_ref, k_ref, v_ref, z_ref, gc_ref, gr_ref, gon_ref, o_ref, s_ref):
    f32, bf16 = jnp.float32, jnp.bfloat16
    C, Dh = CHUNK, B_HEAD_DIM
    hi = lax.Precision.HIGHEST

    @pl.when(pl.program_id(1) == 0)
    def _():
        s_ref[...] = jnp.zeros(s_ref.shape, f32)

    ii = lax.broadcasted_iota(jnp.int32, (C, C), 0)
    jj = lax.broadcasted_iota(jnp.int32, (C, C), 1)
    causal = ii >= jj
    strict = ii > jj
    eye = jnp.where(ii == jj, 1.0, 0.0)
    mm = lambda a, b: jnp.dot(a, b, preferred_element_type=f32)
    mmh = lambda a, b: jnp.dot(a, b, preferred_element_type=f32, precision=hi)
    n_chunks = q_ref.shape[1] // C
    units = []
    for c in range(n_chunks):
        rows = slice(c * C, (c + 1) * C)
        gates_c = gc_ref[0, rows, :]
        gates_r = gr_ref[0, :, c * C:(c + 1) * C]
        for h in range(B_HEADS):
            cols = slice(h * Dh, (h + 1) * Dh)
            gcum = jnp.broadcast_to(gates_c[:, h:h + 1], (C, Dh))
            beta = jnp.broadcast_to(gates_c[:, B_HEADS + h:B_HEADS + h + 1], (C, Dh))
            gcum_r = jnp.broadcast_to(gates_r[h:h + 1, :], (C, C))
            decay = jnp.where(causal, jnp.exp(jnp.minimum(gcum[:, :C] - gcum_r, 0.0)), 0.0)
            q, k, v = q_ref[0, rows, cols], k_ref[0, rows, cols], v_ref[0, rows, cols]
            qb, kb = q.astype(bf16), k.astype(bf16)
            kk = _nt_dot(kb, kb)
            qk = _nt_dot(qb, kb)
            egc = jnp.exp(gcum)
            g_last = gcum[C - 1:C, :]
            units.append(dict(
                c=c, h=h, rows=rows, cols=cols,
                neg_m=jnp.where(strict, -(beta[:, :C] * kk * decay), 0.0),
                rhs=jnp.concatenate([v * beta, k * (beta * egc)], axis=1),
                q_dec=(q * egc).astype(bf16), intra=(qk * decay).astype(bf16),
                k_tail=(k * jnp.exp(g_last - gcum)).astype(bf16), chunk_decay=jnp.exp(g_last)))
    powers = [un["neg_m"] for un in units]
    t_inv = [eye + p for p in powers]
    for _ in range(5):
        powers = [mmh(p, p) for p in powers]
        t_inv = [t + mmh(p, t) for p, t in zip(powers, t_inv)]
    sols = [mmh(t, un["rhs"]) for t, un in zip(t_inv, units)]
    states = [s_ref[h] for h in range(B_HEADS)]
    for c in range(n_chunks):
        group = [(un, sol) for un, sol in zip(units, sols) if un["c"] == c]
        sbs = [states[un["h"]].astype(bf16) for un, _ in group]
        vbs = [(sol[:, :Dh] - mm(sol[:, Dh:].astype(bf16), sb)).astype(bf16)
               for (un, sol), sb in zip(group, sbs)]
        outs = [mm(un["q_dec"], sb) + mm(un["intra"], vb)
                for (un, _), sb, vb in zip(group, sbs, vbs)]
        for (un, _), vb in zip(group, vbs):
            states[un["h"]] = states[un["h"]] * un["chunk_decay"] + lax.dot_general(
                un["k_tail"], vb, (((0,), (0,)), ((), ())), preferred_element_type=f32)
        for (un, _), o in zip(group, outs):
            o = o * lax.rsqrt(jnp.mean(o * o, axis=-1, keepdims=True) + EPS) * gon_ref[...]
            zz = z_ref[0, un["rows"], un["cols"]]
            o_ref[0, un["rows"], un["cols"]] = o * (zz * _sigmoid(zz))
    for h in range(B_HEADS):
        s_ref[h] = states[h]


def gdn_pallas(qkv, z, ab, abt, conv_w, a_log, dt_bias, g_onorm):
    bsz, seq, _ = qkv.shape
    f32 = jnp.float32
    tm = min(256, seq)
    zero4 = jnp.zeros((B_HEADS,), f32)
    lane_row = lambda v: jnp.pad(jnp.concatenate([v.astype(f32), zero4]), (0, 120)).reshape(1, 128)
    sub_col = lambda v: jnp.concatenate([v.astype(f32), zero4]).reshape(GATE_ROWS, 1)
    tok = lambda w, t=tm: pl.BlockSpec((1, t, w), lambda b, i: (b, i, 0))
    const = lambda shape: pl.BlockSpec(shape, lambda b, i: (0,) * len(shape))
    shp = lambda w: jax.ShapeDtypeStruct((bsz, seq, w), f32)
    q, k, v, gc, gr = pl.pallas_call(
        _gdn_prep_kernel,
        grid=(bsz, seq // tm),
        in_specs=[tok(GDN_COLS),
                  pl.BlockSpec((1, 8, GDN_COLS), lambda b, i: (b, jnp.maximum(i * (tm // 8) - 1, 0), 0)),
                  const((CONV_WIDTH, GDN_COLS)), tok(128),
                  pl.BlockSpec((1, GATE_ROWS, tm), lambda b, i: (b, 0, i)),
                  const((1, 128)), const((1, 128)), const((GATE_ROWS, 1)), const((GATE_ROWS, 1))],
        out_specs=[tok(B_QK), tok(B_QK), tok(B_QK), tok(128),
                   pl.BlockSpec((1, GATE_ROWS, tm), lambda b, i: (b, 0, i))],
        out_shape=[shp(B_QK), shp(B_QK), shp(B_QK), shp(128),
                   jax.ShapeDtypeStruct((bsz, GATE_ROWS, seq), f32)],
        compiler_params=pltpu.CompilerParams(vmem_limit_bytes=VMEM_LIMIT),
        name="gdn_prep",
    )(qkv, qkv, conv_w.astype(f32), ab, abt, lane_row(a_log), lane_row(dt_bias),
      sub_col(a_log), sub_col(dt_bias))
    ts = 2 * CHUNK
    return pl.pallas_call(
        _gdn_main_kernel,
        grid=(bsz, seq // ts),
        in_specs=[tok(B_QK, ts), tok(B_QK, ts), tok(B_QK, ts), tok(B_QK, ts), tok(128, ts),
                  pl.BlockSpec((1, GATE_ROWS, ts), lambda b, i: (b, 0, i)),
                  const((1, B_HEAD_DIM))],
        out_specs=tok(B_QK, ts),
        out_shape=shp(B_QK),
        scratch_shapes=[pltpu.VMEM((B_HEADS, B_HEAD_DIM, B_HEAD_DIM), f32)],
        compiler_params=pltpu.CompilerParams(dimension_semantics=("arbitrary", "arbitrary")),
        name="gdn_main",
    )(q, k, v, z, gc, gr, g_onorm.astype(f32).reshape(1, B_HEAD_DIM))


def _mem_kv_kernel(mem_ref, g_ref, wk_ref, wv_ref, k_ref, v_ref):
    f32, bf16 = jnp.float32, jnp.bfloat16
    mn = _rms(mem_ref[0], g_ref[...]).astype(bf16)
    k = jnp.dot(mn, wk_ref[...], preferred_element_type=f32)
    v = jnp.dot(mn, wv_ref[...], preferred_element_type=f32)
    for h in range(X_HEADS):
        cols = slice(h * X_HEAD_DIM, (h + 1) * X_HEAD_DIM)
        k_ref[0, h] = k[:, cols].astype(bf16)
        v_ref[0, h] = v[:, cols].astype(bf16)


def _mid_kernel(x_ref, oa_ref, ob_ref, wo_ref, gx_ref, wq_ref, k_ref, v_ref, wox_ref, o_ref):
    f32, bf16 = jnp.float32, jnp.bfloat16
    na = oa_ref.shape[2]
    x1 = (x_ref[0]
          + jnp.dot(oa_ref[0].astype(bf16), wo_ref[:na, :], preferred_element_type=f32)
          + jnp.dot(ob_ref[0].astype(bf16), wo_ref[na:, :], preferred_element_type=f32))
    hq = _rms(x1, gx_ref[...]).astype(bf16)
    q = jnp.dot(hq, wq_ref[...], preferred_element_type=f32)
    heads = []
    for h in range(X_HEADS):
        qh = q[:, h * X_HEAD_DIM:(h + 1) * X_HEAD_DIM].astype(bf16)
        lg = _nt_dot(qh, k_ref[0, h]) * (X_HEAD_DIM ** -0.5)
        p = jnp.exp(lg - jnp.max(lg, axis=-1, keepdims=True))
        p = (p / jnp.sum(p, axis=-1, keepdims=True)).astype(bf16)
        heads.append(jnp.dot(p, v_ref[0, h], preferred_element_type=f32).astype(bf16))
    o = jnp.concatenate(heads, axis=1)
    o_ref[0] = x1 + jnp.dot(o, wox_ref[...], preferred_element_type=f32)


def mid_pallas(x, o_a, o_b, w_out, g_cross, mem, g_mem, wq, wk, wv, wo):
    bsz, seq, d = x.shape
    f32, bf16 = jnp.float32, jnp.bfloat16
    hx = X_HEADS * X_HEAD_DIM
    m_len = mem.shape[1]
    const2 = lambda shape: pl.BlockSpec(shape, lambda b: (0,) * len(shape))
    kv_spec = pl.BlockSpec((1, X_HEADS, m_len, X_HEAD_DIM), lambda b: (b, 0, 0, 0))
    kv_shape = jax.ShapeDtypeStruct((bsz, X_HEADS, m_len, X_HEAD_DIM), bf16)
    k, v = pl.pallas_call(
        _mem_kv_kernel,
        grid=(bsz,),
        in_specs=[pl.BlockSpec((1, m_len, d), lambda b: (b, 0, 0)), const2((1, d)),
                  const2((d, hx)), const2((d, hx))],
        out_specs=[kv_spec, kv_spec],
        out_shape=[kv_shape, kv_shape],
        name="mem_kv",
    )(mem, g_mem.reshape(1, d), wk.reshape(d, hx).astype(bf16), wv.reshape(d, hx).astype(bf16))
    tm = min(256, seq)
    tok = lambda w: pl.BlockSpec((1, tm, w), lambda b, t: (b, t, 0))
    const = lambda shape: pl.BlockSpec(shape, lambda b, t: (0,) * len(shape))
    kv_spec2 = pl.BlockSpec((1, X_HEADS, m_len, X_HEAD_DIM), lambda b, t: (b, 0, 0, 0))
    return pl.pallas_call(
        _mid_kernel,
        grid=(bsz, seq // tm),
        in_specs=[tok(d), tok(o_a.shape[2]), tok(o_b.shape[2]), const((MIX_WIDTH, d)),
                  const((1, d)), const((d, hx)), kv_spec2, kv_spec2, const((hx, d))],
        out_specs=tok(d),
        out_shape=jax.ShapeDtypeStruct((bsz, seq, d), f32),
        compiler_params=pltpu.CompilerParams(vmem_limit_bytes=VMEM_LIMIT),
        name="mid",
    )(x, o_a, o_b, w_out.astype(bf16), g_cross.reshape(1, d), wq.reshape(d, hx).astype(bf16),
      k, v, wo.reshape(hx, d).astype(bf16))


def kernel(x, mem, g_mix, w_in, g_cq, g_ckv, g_kidx, w_uq, w_iq, w_uk, w_uv, rel_bias, conv_w, A_log, dt_bias, g_onorm, w_out, g_cross, g_mem, wq_x, wk_x, wv_x, wo_x, g_ffn, w_pq, sub_keys, u_emb, v_emb, g_final):
    bsz, seq, d = x.shape
    for l in range(DEPTH):
        cq, ckv, kw, qkv, z, ab, abt = in_proj_pallas(x, g_mix[l], w_in[l])
        o_a = dsa_pallas(cq, ckv, kw, abt, g_cq[l], g_ckv[l], g_kidx[l],
                         w_uq[l], w_iq[l], w_uk[l], w_uv[l], rel_bias)
        o_b = gdn_pallas(qkv, z, ab, abt, conv_w[l], A_log[l], dt_bias[l], g_onorm[l])
        x = mid_pallas(x, o_a, o_b, w_out[l], g_cross[l], mem, g_mem[l],
                       wq_x[l], wk_x[l], wv_x[l], wo_x[l])
        xf = x.reshape(bsz * seq, d)
        y8 = peer_pallas(xf, g_ffn[l], w_pq[l], sub_keys[l], u_emb[l], v_emb[l])
        if l + 1 < DEPTH:
            x = residual_pallas(xf, y8).reshape(bsz, seq, d)
    return residual_pallas(xf, y8, g_final).reshape(bsz, seq, d)
```

```python
import math
from functools import partial
import jax
import jax.numpy as jnp
from jax import lax
import numpy as np
from jax.experimental import pallas as pl
from jax.experimental.pallas import tpu as pltpu

DEPTH = 1

CHUNK = 64
Q_BLOCK = 128
EPS = 1e-6

A_HEADS = 8
A_HEAD_DIM = 64
A_Q_LORA = 256
A_KV_LORA = 256
IDX_HEADS = 8
IDX_DIM = 64
IDX_TOPK_MAX = 256
ATTN_SCALE = A_HEAD_DIM ** -0.5
IDX_SCALE = (IDX_HEADS * IDX_DIM) ** -0.5

B_HEADS = 4
B_HEAD_DIM = 128
B_QK = B_HEADS * B_HEAD_DIM
CONV_WIDTH = 4

REL_BUCKETS = 32
REL_MAX_DIST = 128

X_HEADS = 4
X_HEAD_DIM = 128

P_HEADS = 8
N_KEYS = 128
N_EXPERTS = N_KEYS * N_KEYS
P_TOPK = 16
P_QDIM = 256

COL_WIDTHS = (A_Q_LORA, A_KV_LORA, IDX_DIM, IDX_HEADS, B_QK, B_QK, B_QK, B_QK, B_HEADS, B_HEADS)
MIX_WIDTH = A_HEADS * A_HEAD_DIM + B_HEADS * B_HEAD_DIM


def t5_bucket(rel):
    half = REL_BUCKETS // 2
    max_exact = half // 2
    n = jnp.abs(rel)
    nf = jnp.maximum(n, max_exact).astype(jnp.float32)
    large = max_exact + (jnp.log(nf / max_exact) / math.log(REL_MAX_DIST / max_exact)
                         * (half - max_exact)).astype(jnp.int32)
    large = jnp.minimum(large, half - 1)
    return jnp.where(rel > 0, half, 0) + jnp.where(n < max_exact, n, large)


INT_MIN = -2147483648
NEG_BIG = -1e30
KEY_TILE = 512
KEY_PAD = KEY_TILE - Q_BLOCK
VMEM_LIMIT = 56 * 1024 * 1024


def _rms(x, g):
    return x * lax.rsqrt(jnp.mean(x * x, axis=-1, keepdims=True) + EPS) * g


def _nt_dot(a, b):
    return lax.dot_general(a, b, (((1,), (1,)), ((), ())), preferred_element_type=jnp.float32)


def _dsa_prep_kernel(cq_ref, ckv_ref, kw_ref, gcq_ref, gckv_ref, gk_ref,
                       wiqt_ref, wuqt_ref, wuk_ref,
                       at_ref, qlatt_ref, ckvn_ref, kidxn_ref):
    f32, bf16 = jnp.float32, jnp.bfloat16
    Q = Q_BLOCK
    kw = kw_ref[0]
    cqn = _rms(cq_ref[0], gcq_ref[...]).astype(bf16)
    qit = _nt_dot(wiqt_ref[...], cqn)
    qt = _nt_dot(wuqt_ref[...], cqn)
    for h in range(A_HEADS):
        cols = slice(h * Q, (h + 1) * Q)
        at_ref[0, 0, :, cols] = qit[h * IDX_DIM:(h + 1) * IDX_DIM, :].astype(bf16)
        qh = qt[h * A_HEAD_DIM:(h + 1) * A_HEAD_DIM, :].astype(bf16)
        ql = jnp.dot(wuk_ref[h], qh, preferred_element_type=f32) * ATTN_SCALE
        qlatt_ref[0, 0, :, cols] = ql.astype(bf16)
    ckvn_ref[0] = _rms(ckv_ref[0], gckv_ref[...]).astype(bf16)
    kidxn_ref[0] = _rms(kw[:, :IDX_DIM], gk_ref[...]).astype(bf16)


def _dsa_main_kernel(at_ref, qlatt_ref, wt_ref, kidx_ref, ckv_ref, bucket_ref, far_ref, rb_ref,
                     wuvt_ref, o_ref, bias0_ref, sc_ref, big_ref, acc_ref, m_ref, l_ref, *, topk):
    f32, i32, bf16 = jnp.float32, jnp.int32, jnp.bfloat16
    Q, T = Q_BLOCK, KEY_TILE
    i = pl.program_id(1)

    @pl.when(jnp.logical_and(pl.program_id(0) == 0, i == 0))
    def _():
        bucket = bucket_ref[...]
        for h in range(A_HEADS):
            far_bias = rb_ref[far_ref[0], h]
            bias = jnp.zeros((T, Q), f32)
            for b in range(REL_BUCKETS):
                bias = jnp.where(bucket == b, rb_ref[b, h] - far_bias, bias)
            bias0_ref[h] = bias
    e = (i + 1) * Q
    nt = (i + T // Q) // (T // Q)
    kf = jnp.float32(topk)
    wsc = wt_ref[0] * IDX_SCALE
    sub = lax.broadcasted_iota(i32, (T, Q), 0)
    qlane = lax.broadcasted_iota(i32, (T, Q), 1)
    limit = ((i * Q + qlane) // CHUNK + 1) * CHUNK
    at = at_ref[0, 0]

    def tile_start(j):
        return pl.multiple_of(e + KEY_PAD - (j + 1) * T, 128)

    def key_pos(j):
        return tile_start(j) - KEY_PAD + sub

    def score_tile(j, carry):
        kid = kidx_ref[0, pl.ds(tile_start(j), T), :]
        big_ref[...] = jnp.dot(kid, at, preferred_element_type=f32)
        s = jnp.zeros((T, Q), f32)
        for h in range(IDX_HEADS):
            s = s + wsc[h:h + 1, :] * jnp.maximum(big_ref[:, h * Q:(h + 1) * Q], 0.0)
        bits = lax.bitcast_convert_type(s, i32)
        key = jnp.where(bits < 0, bits ^ jnp.int32(0x7FFFFFFF), bits)
        key = jnp.where(s == 0.0, 0, key)
        kpos = key_pos(j)
        key = jnp.where(kpos >= 0, key, INT_MIN)
        sc_ref[j] = jnp.where(kpos < limit, key, INT_MIN)
        return carry

    lax.fori_loop(0, nt, score_tile, 0)

    def count_where(pred):
        def body(j, cnt):
            hit = jnp.where(pred(sc_ref[j], key_pos(j)), 1.0, 0.0)
            return cnt + jnp.sum(hit.reshape(T // 64, 64, Q), axis=0)
        cnt = lax.fori_loop(0, nt, body, jnp.zeros((64, Q), f32))
        return jnp.sum(cnt, axis=0, keepdims=True)

    def bit_body(b, carry):
        u, cacc = carry
        cand = u | lax.shift_left(jnp.int32(1), 31 - b)
        tvec = cand ^ jnp.int32(INT_MIN)
        tot = count_where(lambda k, kpos: k >= tvec)
        ok = tot >= kf
        return jnp.where(ok, cand, u), jnp.where(ok, tot, cacc)

    u, cacc = lax.fori_loop(0, 32, bit_body, (jnp.zeros((1, Q), i32), jnp.zeros((1, Q), f32)))
    thr = jnp.maximum(u ^ jnp.int32(INT_MIN), INT_MIN + 1)
    overflow = jnp.where(u != 0, cacc, 0.0) > kf
    n_over = jnp.max(jnp.where(overflow, 1.0, 0.0), axis=1, keepdims=True)[0, 0]

    @pl.when(n_over > 0.0)
    def _():
        need = kf - count_where(lambda k, kpos: k > thr)

        def cut_body(b, cut):
            cand = cut | lax.shift_left(jnp.int32(1), 14 - b)
            cnt = count_where(lambda k, kpos: jnp.where(k == thr, kpos, cand) < cand)
            return jnp.where(cnt <= need, cand, cut)

        cut = lax.fori_loop(0, 15, cut_body, jnp.zeros((1, Q), i32))

        def drop_tile(j, carry):
            k = sc_ref[j]
            drop = jnp.where(k == thr, key_pos(j), -1) >= cut
            sc_ref[j] = jnp.where(drop, INT_MIN, k)
            return carry

        lax.fori_loop(0, nt, drop_tile, 0)

    m_ref[...] = jnp.full(m_ref.shape, NEG_BIG, f32)
    l_ref[...] = jnp.zeros(l_ref.shape, f32)
    acc_ref[...] = jnp.zeros(acc_ref.shape, f32)
    qlatt = qlatt_ref[0, 0]

    def attn_tile(j, with_bias):
        kv = ckv_ref[0, pl.ds(tile_start(j), T), :]
        kvt = kv.T
        mask_add = jnp.where(sc_ref[j] >= thr, 0.0, NEG_BIG)
        pair_cols = [slice(g * 2 * Q, (g + 1) * 2 * Q) for g in range(A_HEADS // 2)]
        logits = lambda g: jnp.dot(kv, qlatt[:, pair_cols[g]], preferred_element_type=f32)
        x_next = logits(0)
        for g in range(A_HEADS // 2):
            x_pair = x_next
            if g + 1 < A_HEADS // 2:
                x_next = logits(g + 1)
            ps, alphas = [], []
            for u in range(2):
                h = 2 * g + u
                cols = slice(h * Q, (h + 1) * Q)
                x = x_pair[:, u * Q:(u + 1) * Q] + mask_add
                if with_bias:
                    x = x + bias0_ref[h]
                m_prev = m_ref[:, cols]
                m_new = jnp.maximum(m_prev, jnp.max(x, axis=0, keepdims=True))
                p = jnp.exp(x - m_new)
                alpha = jnp.exp(m_prev - m_new)
                l_ref[:, cols] = alpha * l_ref[:, cols] + jnp.sum(p, axis=0, keepdims=True)
                m_ref[:, cols] = m_new
                ps.append(p.astype(bf16))
                alphas.append(alpha)
            pv = jnp.dot(kvt, jnp.concatenate(ps, axis=1), preferred_element_type=f32)
            acc_ref[:, pair_cols[g]] = (
                jnp.concatenate(alphas, axis=1) * acc_ref[:, pair_cols[g]] + pv)

    attn_tile(0, True)

    def attn_body(j, carry):
        attn_tile(j, False)
        return carry

    lax.fori_loop(1, nt, attn_body, 0)

    inv_l = 1.0 / l_ref[...]
    outs = []
    for h in range(A_HEADS):
        cols = slice(h * Q, (h + 1) * Q)
        o_lat_t = (acc_ref[:, cols] * inv_l[:, cols]).astype(bf16)
        outs.append(jnp.dot(wuvt_ref[h], o_lat_t, preferred_element_type=f32))
    o_ref[0] = jnp.concatenate(outs, axis=0).T


def dsa_pallas(cq, ckv, kw, widx_t, g_cq, g_ckv, g_kidx, w_uq, w_iq, w_uk, w_uv, rel_bias):
    bsz, seq, _ = cq.shape
    f32, bf16 = jnp.float32, jnp.bfloat16
    Q, T, H = Q_BLOCK, KEY_TILE, A_HEADS
    nblk = seq // Q
    topk = min(IDX_TOPK_MAX, seq // 4)
    wiqt = w_iq.reshape(A_Q_LORA, IDX_HEADS * IDX_DIM).T.astype(bf16)
    wuqt = w_uq.reshape(A_Q_LORA, H * A_HEAD_DIM).T.astype(bf16)
    wuk = jnp.transpose(w_uk, (1, 0, 2)).astype(bf16)
    wuvt = jnp.transpose(w_uv, (1, 2, 0)).astype(bf16)
    tok = lambda w: pl.BlockSpec((1, Q, w), lambda b, t: (b, t, 0))
    full = lambda shape: pl.BlockSpec(shape, lambda b, t: (0,) * len(shape))
    blk = lambda r: pl.BlockSpec((1, 1, r, H * Q), lambda b, t: (b, t, 0, 0))
    a_t, qlat_t, ckvn, kidxn = pl.pallas_call(
        _dsa_prep_kernel,
        grid=(bsz, nblk),
        in_specs=[tok(A_Q_LORA), tok(A_KV_LORA), tok(128),
                  full((1, A_Q_LORA)), full((1, A_KV_LORA)), full((1, IDX_DIM)),
                  full(wiqt.shape), full(wuqt.shape), full(wuk.shape)],
        out_specs=[blk(IDX_DIM), blk(A_KV_LORA), tok(A_KV_LORA), tok(IDX_DIM)],
        out_shape=[jax.ShapeDtypeStruct((bsz, nblk, IDX_DIM, H * Q), bf16),
                   jax.ShapeDtypeStruct((bsz, nblk, A_KV_LORA, H * Q), bf16),
                   jax.ShapeDtypeStruct((bsz, seq, A_KV_LORA), bf16),
                   jax.ShapeDtypeStruct((bsz, seq, IDX_DIM), bf16)],
        name="dsa_prep",
    )(cq, ckv, kw, g_cq.reshape(1, -1), g_ckv.reshape(1, -1), g_kidx.reshape(1, -1),
      wiqt, wuqt, wuk)
    ckvp = jnp.pad(ckvn, ((0, 0), (KEY_PAD, 0), (0, 0)))
    kidxp = jnp.pad(kidxn, ((0, 0), (KEY_PAD, 0), (0, 0)))
    rel = (jnp.arange(T, dtype=jnp.int32)[:, None] - jnp.arange(Q, dtype=jnp.int32)[None, :]
           - KEY_PAD)
    bucket0 = t5_bucket(rel)
    bucket_far = t5_bucket(jnp.full((1,), -KEY_PAD - 1, jnp.int32))
    smem = pl.BlockSpec(memory_space=pltpu.SMEM)
    nt_max = (nblk - 1 + T // Q) // (T // Q)
    skey = seq + KEY_PAD
    return pl.pallas_call(
        partial(_dsa_main_kernel, topk=topk),
        grid=(bsz, nblk),
        in_specs=[blk(IDX_DIM), blk(A_KV_LORA),
                  pl.BlockSpec((1, IDX_HEADS, Q), lambda b, t: (b, 1, t)),
                  pl.BlockSpec((1, skey, IDX_DIM), lambda b, t: (b, 0, 0)),
                  pl.BlockSpec((1, skey, A_KV_LORA), lambda b, t: (b, 0, 0)),
                  full((T, Q)), smem, smem, full(wuvt.shape)],
        out_specs=tok(H * A_HEAD_DIM),
        out_shape=jax.ShapeDtypeStruct((bsz, seq, H * A_HEAD_DIM), f32),
        scratch_shapes=[pltpu.VMEM((H, T, Q), f32),
                        pltpu.VMEM((nt_max, T, Q), jnp.int32),
                        pltpu.VMEM((T, H * Q), f32),
                        pltpu.VMEM((A_KV_LORA, H * Q), f32),
                        pltpu.VMEM((1, H * Q), f32),
                        pltpu.VMEM((1, H * Q), f32)],
        compiler_params=pltpu.CompilerParams(
            dimension_semantics=("arbitrary", "arbitrary"), vmem_limit_bytes=VMEM_LIMIT),
        name="dsa_main",
    )(a_t, qlat_t, widx_t, kidxp, ckvp, bucket0, bucket_far, rel_bias.astype(f32), wuvt)


PEER_SCORE_TOKENS = 256
PEER_GATHER_TOKENS = 64
PEER_SLOTS = P_HEADS * P_TOPK
WORDS_PER_ROW = 4
PEER_TILES = 16


def _top16(s, order=None, payload=None):
    if order is None:
        order = lax.broadcasted_iota(jnp.int32, s.shape, 0).astype(jnp.float32)
    vals, picks = [], []
    for _ in range(P_TOPK):
        m = jnp.max(s, axis=0, keepdims=True)
        pos = jnp.min(jnp.where(s == m, order, float(N_EXPERTS)), axis=0, keepdims=True)
        hit = order == pos
        vals.append(m)
        if payload is None:
            picks.append(pos)
        else:
            picks.append(jnp.max(jnp.where(hit, payload, -1.0), axis=0, keepdims=True))
        s = jnp.where(hit, -jnp.inf, s)
    return vals, picks


def _staircase_candidates(v1, i1, v2, i2):
    v1m, i1m = jnp.concatenate(v1, axis=0), jnp.concatenate(i1, axis=0)
    v2m, i2m = jnp.concatenate(v2, axis=0), jnp.concatenate(i2, axis=0)
    t = v1m.shape[1]
    sub8 = lax.broadcasted_iota(jnp.int32, (8, t), 0).astype(jnp.float32)
    cand, cidx, rank = [], [], []
    for a in range(8):
        cand.append(v1[a] + v2m[:8])
        cidx.append(i1[a] * float(N_KEYS) + i2m[:8])
        rank.append(sub8 + float(a * P_TOPK))
    cand.append(v1[0] + v2m[8:])
    cidx.append(i1[0] * float(N_KEYS) + i2m[8:])
    rank.append(sub8 + 8.0)
    cand.append(v1m[8:] + v2[0])
    cidx.append(i1m[8:] * float(N_KEYS) + i2[0])
    rank.append((sub8 + 8.0) * float(P_TOPK))
    cat = lambda xs: jnp.concatenate(xs, axis=0)
    return cat(cand), cat(rank), cat(cidx)


def _peer_score_kernel(x_ref, g_ref, wpqt_ref, sk_ref, hn_ref, eidx_ref, gate_ref):
    f32, bf16 = jnp.float32, jnp.bfloat16
    h = _rms(x_ref[...], g_ref[...])
    hb = h.astype(bf16)
    for s in range(h.shape[1] // 128):
        hn_ref[pl.ds(s, h.shape[0], stride=8), :] = h[:, s * 128:(s + 1) * 128]
    qrt = _nt_dot(wpqt_ref[...], hb)
    half = P_QDIM // 2
    e_rows, g_rows = [], []
    for hd in range(P_HEADS):
        tops = []
        for p in range(2):
            qhp = qrt[(hd * 2 + p) * half:(hd * 2 + p + 1) * half, :].astype(bf16)
            s = jnp.dot(sk_ref[hd * 2 + p], qhp, preferred_element_type=f32)
            tops.append(_top16(s))
        (v1, i1), (v2, i2) = tops
        best, be = _top16(*_staircase_candidates(v1, i1, v2, i2))
        ex = [jnp.exp(b - best[0]) for b in best]
        den = ex[0]
        for k in range(1, P_TOPK):
            den = den + ex[k]
        inv = 1.0 / den
        e_rows += be
        g_rows += [x * inv for x in ex]
    eidx_ref[...] = (jnp.concatenate(e_rows, axis=0).T * float(WORDS_PER_ROW)).astype(jnp.int32)
    gate_ref[...] = jnp.concatenate(g_rows, axis=0).T


def _diag_mask():
    r = lax.broadcasted_iota(jnp.int32, (8, PEER_SLOTS * 8), 0)
    c = lax.broadcasted_iota(jnp.int32, (8, PEER_SLOTS * 8), 1)
    return (c & 7) == r


def _gather_pair(idx_ref, tab_ref, g_refs, t0):
    views = [idx_ref.at[t0 + u] for u in range(2)]
    for r in range(PEER_SLOTS):
        for view, g_ref in zip(views, g_refs):
            row0 = pl.multiple_of(view[r], WORDS_PER_ROW)
            g_ref[r * WORDS_PER_ROW:(r + 1) * WORDS_PER_ROW, :] = tab_ref[
                pl.ds(row0, WORDS_PER_ROW), :]


def _gather_pipeline(idx_ref, tab_ref, tiles, n_tokens, consume):
    pairs = len(tiles) // 2
    for g_ref in tiles[-2:]:
        g_ref[...] = jnp.zeros(g_ref.shape, g_ref.dtype)

    def trip(i, carry):
        for k in range(pairs):
            pair = pairs * i + k
            prev_tiles = tiles[2 * k - 2:2 * k] if k else tiles[-2:]
            _gather_pair(idx_ref, tab_ref, tiles[2 * k:2 * k + 2], pair * 2)
            for u in range(2):
                consume(prev_tiles[u], jnp.maximum(pair - 1, 0), u)
        return carry

    lax.fori_loop(0, n_tokens // (2 * pairs), trip, 0)
    for u in range(2):
        consume(tiles[-2 + u], n_tokens // 2 - 1, u)


def _peer_act_kernel(idx_ref, hn_ref, tab_ref, a_ref, m_ref, *tiles):
    f32, bf16 = jnp.float32, jnp.bfloat16
    tb = a_ref.shape[0]

    def dots(g_ref, tp, u):
        hp = hn_ref[pl.ds(pl.multiple_of(tp * 16, 16), 16), :].astype(bf16)
        m = _nt_dot(hp, pltpu.bitcast(g_ref[...], bf16))
        m_ref[pl.ds(pl.multiple_of((tp * 2 + u) * 8, 8), 8), :] = m[u * 8:(u + 1) * 8, :]

    _gather_pipeline(idx_ref, tab_ref, tiles, tb, dots)
    m3 = m_ref[...].reshape(tb, 8, PEER_SLOTS * 8)
    z = jnp.sum(jnp.where(_diag_mask()[None], m3, 0.0), axis=1)
    rr = lax.broadcasted_iota(jnp.int32, (PEER_SLOTS * 8, PEER_SLOTS), 0)
    cc = lax.broadcasted_iota(jnp.int32, (PEER_SLOTS * 8, PEER_SLOTS), 1)
    pool = jnp.where((rr >> 3) == cc, 1.0, 0.0).astype(bf16)
    z_hi = z.astype(bf16)
    z_lo = (z - z_hi.astype(f32)).astype(bf16)
    a_ref[...] = (jnp.dot(z_hi, pool, preferred_element_type=f32)
                  + jnp.dot(z_lo, pool, preferred_element_type=f32))


def _peer_out_kernel(idx_ref, a_ref, gate_ref, tab_ref, o_ref, w_ref, *tiles):
    f32, bf16 = jnp.float32, jnp.bfloat16
    tb = a_ref.shape[0]
    a = a_ref[...]
    act = 0.5 * a * (1.0 + lax.erf(a * (2.0 ** -0.5)))
    wgt = (gate_ref[...] * act).astype(bf16)
    rr = lax.broadcasted_iota(jnp.int32, (PEER_SLOTS, PEER_SLOTS * 8), 0)
    cc = lax.broadcasted_iota(jnp.int32, (PEER_SLOTS, PEER_SLOTS * 8), 1)
    expand = jnp.where((cc >> 3) == rr, 1.0, 0.0).astype(bf16)
    w_ref[...] = jnp.dot(wgt, expand, preferred_element_type=f32)
    diag = _diag_mask()

    def combine(g_ref, tp, u):
        t = tp * 2 + u
        wrow = jnp.broadcast_to(w_ref[pl.ds(t, 1), :], (8, PEER_SLOTS * 8))
        wsel = jnp.where(diag, wrow, 0.0).astype(bf16)
        o_ref[pl.ds(pl.multiple_of(t * 8, 8), 8), :] = jnp.dot(
            wsel, pltpu.bitcast(g_ref[...], bf16), preferred_element_type=f32)

    _gather_pipeline(idx_ref, tab_ref, tiles, tb, combine)


PACK_ROWS = 512


def _pack_table_kernel(x_ref, o_ref):
    rows = x_ref.shape[0]
    bf16_bits = lambda v: lax.bitcast_convert_type(
        v.astype(jnp.bfloat16).astype(jnp.float32), jnp.int32)
    for c in range(WORDS_PER_ROW):
        lo = bf16_bits(x_ref[:, (2 * c) * 128:(2 * c + 1) * 128])
        hi = bf16_bits(x_ref[:, (2 * c + 1) * 128:(2 * c + 2) * 128])
        word = (hi & jnp.int32(-65536)) | lax.shift_right_logical(lo, 16)
        o_ref[pl.ds(c, rows, stride=WORDS_PER_ROW), :] = word


def _pack_table(tab):
    n_e, d = tab.shape
    return pl.pallas_call(
        _pack_table_kernel,
        grid=(n_e // PACK_ROWS,),
        in_specs=[pl.BlockSpec((PACK_ROWS, d), lambda i: (i, 0))],
        out_specs=pl.BlockSpec((PACK_ROWS * WORDS_PER_ROW, 128), lambda i: (i, 0)),
        out_shape=jax.ShapeDtypeStruct((n_e * WORDS_PER_ROW, 128), jnp.int32),
        name="pack_table",
    )(tab)


def peer_pallas(x, g_ffn, w_pq, sub_keys, u_emb, v_emb):
    n_tok, d = x.shape
    f32, bf16 = jnp.float32, jnp.bfloat16
    ts, tg = PEER_SCORE_TOKENS, PEER_GATHER_TOKENS
    wpqt = w_pq.reshape(d, P_HEADS * P_QDIM).T.astype(bf16)
    sk = sub_keys.reshape(P_HEADS * 2, N_KEYS, P_QDIM // 2).astype(bf16)
    hn, eidx, gate = pl.pallas_call(
        _peer_score_kernel,
        grid=(n_tok // ts,),
        in_specs=[pl.BlockSpec((ts, d), lambda i: (i, 0)),
                  pl.BlockSpec((1, d), lambda i: (0, 0)),
                  pl.BlockSpec(wpqt.shape, lambda i: (0, 0)),
                  pl.BlockSpec(sk.shape, lambda i: (0, 0, 0))],
        out_specs=[pl.BlockSpec((ts * 8, 128), lambda i: (i, 0)),
                   pl.BlockSpec((ts, PEER_SLOTS), lambda i: (i, 0)),
                   pl.BlockSpec((ts, PEER_SLOTS), lambda i: (i, 0))],
        out_shape=[jax.ShapeDtypeStruct((n_tok * 8, 128), f32),
                   jax.ShapeDtypeStruct((n_tok, PEER_SLOTS), jnp.int32),
                   jax.ShapeDtypeStruct((n_tok, PEER_SLOTS), f32)],
        compiler_params=pltpu.CompilerParams(vmem_limit_bytes=VMEM_LIMIT),
        name="peer_score",
    )(x, g_ffn.reshape(1, d), wpqt, sk)
    utab, vtab = _pack_table(u_emb), _pack_table(v_emb)
    idx_spec = pl.BlockSpec((tg, PEER_SLOTS), lambda i: (i, 0), memory_space=pltpu.SMEM)
    tab_spec = pl.BlockSpec(memory_space=pltpu.VMEM)
    slot_spec = pl.BlockSpec((tg, PEER_SLOTS), lambda i: (i, 0))
    row_spec = pl.BlockSpec((tg * 8, 128), lambda i: (i, 0))
    gbuf = pltpu.VMEM((PEER_SLOTS * WORDS_PER_ROW, 128), jnp.int32)
    cparams = pltpu.CompilerParams(vmem_limit_bytes=VMEM_LIMIT)
    act = pl.pallas_call(
        _peer_act_kernel,
        grid=(n_tok // tg,),
        in_specs=[idx_spec, row_spec, tab_spec],
        out_specs=slot_spec,
        out_shape=jax.ShapeDtypeStruct((n_tok, PEER_SLOTS), f32),
        scratch_shapes=[pltpu.VMEM((tg * 8, PEER_SLOTS * 8), f32)] + [gbuf] * PEER_TILES,
        compiler_params=cparams,
        name="peer_act",
    )(eidx, hn, utab)
    out = pl.pallas_call(
        _peer_out_kernel,
        grid=(n_tok // tg,),
        in_specs=[idx_spec, slot_spec, slot_spec, tab_spec],
        out_specs=row_spec,
        out_shape=jax.ShapeDtypeStruct((n_tok * 8, 128), f32),
        scratch_shapes=[pltpu.VMEM((tg, PEER_SLOTS * 8), f32)] + [gbuf] * PEER_TILES,
        compiler_params=cparams,
        name="peer_out",
    )(eidx, act, gate, vtab)
    return out


def _token_rows(y_ref):
    tm = y_ref.shape[0] // 8
    return jnp.concatenate([y_ref[pl.ds(s, tm, stride=8), :] for s in range(8)], axis=1)


def _final_kernel(x_ref, y_ref, g_ref, o_ref):
    o_ref[...] = _rms(x_ref[...] + _token_rows(y_ref), g_ref[...])


def _residual_kernel(x_ref, y_ref, o_ref):
    o_ref[...] = x_ref[...] + _token_rows(y_ref)


def residual_pallas(x, y8, g=None):
    n, d = x.shape
    tm = 512
    row = pl.BlockSpec((tm, d), lambda i: (i, 0))
    row8 = pl.BlockSpec((tm * 8, 128), lambda i: (i, 0))
    if g is None:
        kern, extra, extra_specs, name = _residual_kernel, (), [], "residual"
    else:
        kern, extra, name = _final_kernel, (g.reshape(1, d),), "final_rmsnorm"
        extra_specs = [pl.BlockSpec((1, d), lambda i: (0, 0))]
    return pl.pallas_call(
        kern,
        grid=(n // tm,),
        in_specs=[row, row8] + extra_specs,
        out_specs=row,
        out_shape=jax.ShapeDtypeStruct((n, d), x.dtype),
        name=name,
    )(x, y8, *extra)


IN_PROJ_TOKENS = 512
GDN_COLS = 3 * B_QK
GATE_ROWS = 8


def _in_proj_kernel(x_ref, g_ref, wa_ref, wkw_ref, wqkv_ref, wz_ref, wab_ref, wabt_ref,
                    cq_ref, ckv_ref, kw_ref, qkv_ref, z_ref, ab_ref, abt_ref):
    f32 = jnp.float32
    hb = _rms(x_ref[0], g_ref[...]).astype(jnp.bfloat16)
    a = jnp.dot(hb, wa_ref[...], preferred_element_type=f32)
    cq_ref[0] = a[:, :A_Q_LORA]
    ckv_ref[0] = a[:, A_Q_LORA:]
    kw_ref[0] = jnp.dot(hb, wkw_ref[...], preferred_element_type=f32)
    qkv_ref[0] = jnp.dot(hb, wqkv_ref[...], preferred_element_type=f32)
    z_ref[0] = jnp.dot(hb, wz_ref[...], preferred_element_type=f32)
    ab_ref[0] = jnp.dot(hb, wab_ref[...], preferred_element_type=f32)
    abt_ref[0] = _nt_dot(wabt_ref[...], hb)


def in_proj_pallas(x, g_mix, w_in):
    bsz, seq, d = x.shape
    f32, bf16 = jnp.float32, jnp.bfloat16
    tm = min(IN_PROJ_TOKENS, seq)
    o = np.cumsum((0,) + COL_WIDTHS)
    wb = w_in.astype(bf16)
    pad_cols = lambda w: jnp.pad(w, ((0, 0), (0, 128 - w.shape[1])))
    wa = wb[:, o[0]:o[2]]
    wkw = pad_cols(wb[:, o[2]:o[4]])
    wqkv = wb[:, o[4]:o[7]]
    wz = wb[:, o[7]:o[8]]
    wab = pad_cols(wb[:, o[8]:o[10]])
    wabt = jnp.concatenate([wb[:, o[8]:o[10]], wb[:, o[3]:o[4]]], axis=1).T
    full = lambda w: pl.BlockSpec(w.shape, lambda b, t: (0, 0))
    tok = lambda w: pl.BlockSpec((1, tm, w), lambda b, t: (b, t, 0))
    shp = lambda w: jax.ShapeDtypeStruct((bsz, seq, w), f32)
    return pl.pallas_call(
        _in_proj_kernel,
        grid=(bsz, seq // tm),
        in_specs=[tok(d), pl.BlockSpec((1, d), lambda b, t: (0, 0)),
                  full(wa), full(wkw), full(wqkv), full(wz), full(wab), full(wabt)],
        out_specs=[tok(A_Q_LORA), tok(A_KV_LORA), tok(128), tok(GDN_COLS), tok(B_QK), tok(128),
                   pl.BlockSpec((1, GATE_ROWS + IDX_HEADS, tm), lambda b, t: (b, 0, t))],
        out_shape=[shp(A_Q_LORA), shp(A_KV_LORA), shp(128), shp(GDN_COLS), shp(B_QK), shp(128),
                   jax.ShapeDtypeStruct((bsz, GATE_ROWS + IDX_HEADS, seq), f32)],
        compiler_params=pltpu.CompilerParams(vmem_limit_bytes=VMEM_LIMIT),
        name="in_proj",
    )(x, g_mix.reshape(1, d), wa, wkw, wqkv, wz, wab, wabt)


def _softplus(x):
    return jnp.maximum(x, 0.0) + jnp.log1p(jnp.exp(-jnp.abs(x)))


def _sigmoid(x):
    return 1.0 / (1.0 + jnp.exp(-x))


def _gdn_gates(pre, a_log, dt_bias, is_decay):
    g = -jnp.exp(a_log) * _softplus(pre + dt_bias)
    return jnp.where(is_decay, g, _sigmoid(pre))


def _gdn_prep_kernel(qkv_ref, halo_ref, cw_ref, ab_ref, abt_ref, alc_ref, dtc_ref, alr_ref, dtr_ref,
                     q_ref, k_ref, v_ref, gc_ref, gr_ref):
    tm = qkv_ref.shape[1]
    x = qkv_ref[0]
    halo = jnp.where(pl.program_id(1) > 0, halo_ref[0], 0.0)
    full = jnp.concatenate([halo, x], axis=0)
    y = x * cw_ref[CONV_WIDTH - 1:CONV_WIDTH, :]
    for back in range(1, CONV_WIDTH):
        shifted = pltpu.roll(full, back, axis=0)[8:, :]
        y = y + shifted * cw_ref[CONV_WIDTH - 1 - back:CONV_WIDTH - back, :]
    y = y * _sigmoid(y)
    for h in range(B_HEADS):
        cols = slice(h * B_HEAD_DIM, (h + 1) * B_HEAD_DIM)
        qh = y[:, h * B_HEAD_DIM:(h + 1) * B_HEAD_DIM]
        kh = y[:, B_QK + h * B_HEAD_DIM:B_QK + (h + 1) * B_HEAD_DIM]
        q_ref[0, :, cols] = qh * lax.rsqrt(
            jnp.sum(qh * qh, axis=-1, keepdims=True) + EPS) * (B_HEAD_DIM ** -0.5)
        k_ref[0, :, cols] = kh * lax.rsqrt(jnp.sum(kh * kh, axis=-1, keepdims=True) + EPS)
    v_ref[0] = y[:, 2 * B_QK:]
    lane = lax.broadcasted_iota(jnp.int32, (tm, 128), 1)
    gates_c = _gdn_gates(ab_ref[0], alc_ref[...], dtc_ref[...], lane < B_HEADS)
    row = lax.broadcasted_iota(jnp.int32, (GATE_ROWS, tm), 0)
    gates_r = _gdn_gates(abt_ref[0], alr_ref[...], dtr_ref[...], row < B_HEADS)
    ti = lax.broadcasted_iota(jnp.int32, (tm, tm), 0)
    tj = lax.broadcasted_iota(jnp.int32, (tm, tm), 1)
    same_chunk = (ti // CHUNK) == (tj // CHUNK)
    hi = lax.Precision.HIGHEST
    lower = jnp.where(same_chunk & (tj <= ti), 1.0, 0.0)
    upper = jnp.where(same_chunk & (ti <= tj), 1.0, 0.0)
    cum_c = jnp.dot(lower, gates_c, preferred_element_type=jnp.float32, precision=hi)
    cum_r = jnp.dot(gates_r, upper, preferred_element_type=jnp.float32, precision=hi)
    gc_ref[0] = jnp.where(lane < B_HEADS, cum_c, gates_c)
    gr_ref[0] = jnp.where(row < B_HEADS, cum_r, gates_r)


def _gdn_main_kernel(q_ref, k_ref, v_ref, z_ref, gc_ref, gr_ref, gon_ref, o_ref, s_ref):
    f32, bf16 = jnp.float32, jnp.bfloat16
    C, Dh = CHUNK, B_HEAD_DIM

    @pl.when(pl.program_id(1) == 0)
    def _():
        s_ref[...] = jnp.zeros(s_ref.shape, f32)

    ii = lax.broadcasted_iota(jnp.int32, (C, C), 0)
    jj = lax.broadcasted_iota(jnp.int32, (C, C), 1)
    causal = ii >= jj
    strict = ii > jj
    eye = jnp.where(ii == jj, 1.0, 0.0)
    mm = lambda a, b: jnp.dot(a, b, preferred_element_type=f32)
    n_chunks = q_ref.shape[1] // C
    units = []
    for c in range(n_chunks):
        rows = slice(c * C, (c + 1) * C)
        gates_c = gc_ref[0, rows, :]
        gates_r = gr_ref[0, :, c * C:(c + 1) * C]
        for h in range(B_HEADS):
            cols = slice(h * Dh, (h + 1) * Dh)
            gcum = jnp.broadcast_to(gates_c[:, h:h + 1], (C, Dh))
            beta = jnp.broadcast_to(gates_c[:, B_HEADS + h:B_HEADS + h + 1], (C, Dh))
            gcum_r = jnp.broadcast_to(gates_r[h:h + 1, :], (C, C))
            decay = jnp.where(causal, jnp.exp(jnp.minimum(gcum[:, :C] - gcum_r, 0.0)), 0.0)
            q, k, v = q_ref[0, rows, cols], k_ref[0, rows, cols], v_ref[0, rows, cols]
            qb, kb = q.astype(bf16), k.astype(bf16)
            kk = _nt_dot(kb, kb)
            qk = _nt_dot(qb, kb)
            egc = jnp.exp(gcum)
            g_last = gcum[C - 1:C, :]
            units.append(dict(
                c=c, h=h, rows=rows, cols=cols,
                neg_m=jnp.where(strict, -(beta[:, :C] * kk * decay), 0.0),
                rhs=jnp.concatenate([v * beta, k * (beta * egc)], axis=1),
                q_dec=(q * egc).astype(bf16), intra=(qk * decay).astype(bf16),
                k_tail=(k * jnp.exp(g_last - gcum)).astype(bf16), chunk_decay=jnp.exp(g_last)))
    def halves(a):
        a_h = a.astype(bf16)
        return a_h, (a - a_h.astype(f32)).astype(bf16)

    def mm3(a, b):
        (a_h, a_l), (b_h, b_l) = a, b
        return mm(a_h, b_h) + (mm(a_h, b_l) + mm(a_l, b_h))

    t_inv = [eye + un["neg_m"] for un in units]
    p_halves = [halves(un["neg_m"]) for un in units]
    for _ in range(5):
        p_halves = [halves(mm3(ph, ph)) for ph in p_halves]
        t_inv = [t + mm3(ph, halves(t)) for ph, t in zip(p_halves, t_inv)]
    sols = [mm3(halves(t), halves(un["rhs"])) for t, un in zip(t_inv, units)]
    states = [s_ref[h] for h in range(B_HEADS)]
    for c in range(n_chunks):
        group = [(un, sol) for un, sol in zip(units, sols) if un["c"] == c]
        sbs = [states[un["h"]].astype(bf16) for un, _ in group]
        vbs = [(sol[:, :Dh] - mm(sol[:, Dh:].astype(bf16), sb)).astype(bf16)
               for (un, sol), sb in zip(group, sbs)]
        outs = [mm(un["q_dec"], sb) + mm(un["intra"], vb)
                for (un, _), sb, vb in zip(group, sbs, vbs)]
        for (un, _), vb in zip(group, vbs):
            states[un["h"]] = states[un["h"]] * un["chunk_decay"] + lax.dot_general(
                un["k_tail"], vb, (((0,), (0,)), ((), ())), preferred_element_type=f32)
        for (un, _), o in zip(group, outs):
            o = o * lax.rsqrt(jnp.mean(o * o, axis=-1, keepdims=True) + EPS) * gon_ref[...]
            zz = z_ref[0, un["rows"], un["cols"]]
            o_ref[0, un["rows"], un["cols"]] = o * (zz * _sigmoid(zz))
    for h in range(B_HEADS):
        s_ref[h] = states[h]


def gdn_pallas(qkv, z, ab, abt, conv_w, a_log, dt_bias, g_onorm):
    bsz, seq, _ = qkv.shape
    f32 = jnp.float32
    tm = min(256, seq)
    zero4 = jnp.zeros((B_HEADS,), f32)
    lane_row = lambda v: jnp.pad(jnp.concatenate([v.astype(f32), zero4]), (0, 120)).reshape(1, 128)
    sub_col = lambda v: jnp.concatenate([v.astype(f32), zero4]).reshape(GATE_ROWS, 1)
    tok = lambda w, t=tm: pl.BlockSpec((1, t, w), lambda b, i: (b, i, 0))
    const = lambda shape: pl.BlockSpec(shape, lambda b, i: (0,) * len(shape))
    shp = lambda w: jax.ShapeDtypeStruct((bsz, seq, w), f32)
    q, k, v, gc, gr = pl.pallas_call(
        _gdn_prep_kernel,
        grid=(bsz, seq // tm),
        in_specs=[tok(GDN_COLS),
                  pl.BlockSpec((1, 8, GDN_COLS), lambda b, i: (b, jnp.maximum(i * (tm // 8) - 1, 0), 0)),
                  const((CONV_WIDTH, GDN_COLS)), tok(128),
                  pl.BlockSpec((1, GATE_ROWS, tm), lambda b, i: (b, 0, i)),
                  const((1, 128)), const((1, 128)), const((GATE_ROWS, 1)), const((GATE_ROWS, 1))],
        out_specs=[tok(B_QK), tok(B_QK), tok(B_QK), tok(128),
                   pl.BlockSpec((1, GATE_ROWS, tm), lambda b, i: (b, 0, i))],
        out_shape=[shp(B_QK), shp(B_QK), shp(B_QK), shp(128),
                   jax.ShapeDtypeStruct((bsz, GATE_ROWS, seq), f32)],
        compiler_params=pltpu.CompilerParams(vmem_limit_bytes=VMEM_LIMIT),
        name="gdn_prep",
    )(qkv, qkv, conv_w.astype(f32), ab, abt, lane_row(a_log), lane_row(dt_bias),
      sub_col(a_log), sub_col(dt_bias))
    ts = 2 * CHUNK
    return pl.pallas_call(
        _gdn_main_kernel,
        grid=(bsz, seq // ts),
        in_specs=[tok(B_QK, ts), tok(B_QK, ts), tok(B_QK, ts), tok(B_QK, ts), tok(128, ts),
                  pl.BlockSpec((1, GATE_ROWS, ts), lambda b, i: (b, 0, i)),
                  const((1, B_HEAD_DIM))],
        out_specs=tok(B_QK, ts),
        out_shape=shp(B_QK),
        scratch_shapes=[pltpu.VMEM((B_HEADS, B_HEAD_DIM, B_HEAD_DIM), f32)],
        compiler_params=pltpu.CompilerParams(dimension_semantics=("arbitrary", "arbitrary")),
        name="gdn_main",
    )(q, k, v, z, gc, gr, g_onorm.astype(f32).reshape(1, B_HEAD_DIM))


def _mem_kv_kernel(mem_ref, g_ref, wk_ref, wv_ref, k_ref, v_ref):
    f32, bf16 = jnp.float32, jnp.bfloat16
    mn = _rms(mem_ref[0], g_ref[...]).astype(bf16)
    k = jnp.dot(mn, wk_ref[...], preferred_element_type=f32)
    v = jnp.dot(mn, wv_ref[...], preferred_element_type=f32)
    for h in range(X_HEADS):
        cols = slice(h * X_HEAD_DIM, (h + 1) * X_HEAD_DIM)
        k_ref[0, h] = k[:, cols].astype(bf16)
        v_ref[0, h] = v[:, cols].astype(bf16)


def _mid_kernel(x_ref, oa_ref, ob_ref, wo_ref, gx_ref, wq_ref, k_ref, v_ref, wox_ref, o_ref):
    f32, bf16 = jnp.float32, jnp.bfloat16
    na = oa_ref.shape[2]
    x1 = (x_ref[0]
          + jnp.dot(oa_ref[0].astype(bf16), wo_ref[:na, :], preferred_element_type=f32)
          + jnp.dot(ob_ref[0].astype(bf16), wo_ref[na:, :], preferred_element_type=f32))
    hq = _rms(x1, gx_ref[...]).astype(bf16)
    q = jnp.dot(hq, wq_ref[...], preferred_element_type=f32)
    heads = []
    for h in range(X_HEADS):
        qh = q[:, h * X_HEAD_DIM:(h + 1) * X_HEAD_DIM].astype(bf16)
        lg = _nt_dot(qh, k_ref[0, h]) * (X_HEAD_DIM ** -0.5)
        p = jnp.exp(lg - jnp.max(lg, axis=-1, keepdims=True))
        p = (p / jnp.sum(p, axis=-1, keepdims=True)).astype(bf16)
        heads.append(jnp.dot(p, v_ref[0, h], preferred_element_type=f32).astype(bf16))
    o = jnp.concatenate(heads, axis=1)
    o_ref[0] = x1 + jnp.dot(o, wox_ref[...], preferred_element_type=f32)


def mid_pallas(x, o_a, o_b, w_out, g_cross, mem, g_mem, wq, wk, wv, wo):
    bsz, seq, d = x.shape
    f32, bf16 = jnp.float32, jnp.bfloat16
    hx = X_HEADS * X_HEAD_DIM
    m_len = mem.shape[1]
    const2 = lambda shape: pl.BlockSpec(shape, lambda b: (0,) * len(shape))
    kv_spec = pl.BlockSpec((1, X_HEADS, m_len, X_HEAD_DIM), lambda b: (b, 0, 0, 0))
    kv_shape = jax.ShapeDtypeStruct((bsz, X_HEADS, m_len, X_HEAD_DIM), bf16)
    k, v = pl.pallas_call(
        _mem_kv_kernel,
        grid=(bsz,),
        in_specs=[pl.BlockSpec((1, m_len, d), lambda b: (b, 0, 0)), const2((1, d)),
                  const2((d, hx)), const2((d, hx))],
        out_specs=[kv_spec, kv_spec],
        out_shape=[kv_shape, kv_shape],
        name="mem_kv",
    )(mem, g_mem.reshape(1, d), wk.reshape(d, hx).astype(bf16), wv.reshape(d, hx).astype(bf16))
    tm = min(256, seq)
    tok = lambda w: pl.BlockSpec((1, tm, w), lambda b, t: (b, t, 0))
    const = lambda shape: pl.BlockSpec(shape, lambda b, t: (0,) * len(shape))
    kv_spec2 = pl.BlockSpec((1, X_HEADS, m_len, X_HEAD_DIM), lambda b, t: (b, 0, 0, 0))
    return pl.pallas_call(
        _mid_kernel,
        grid=(bsz, seq // tm),
        in_specs=[tok(d), tok(o_a.shape[2]), tok(o_b.shape[2]), const((MIX_WIDTH, d)),
                  const((1, d)), const((d, hx)), kv_spec2, kv_spec2, const((hx, d))],
        out_specs=tok(d),
        out_shape=jax.ShapeDtypeStruct((bsz, seq, d), f32),
        compiler_params=pltpu.CompilerParams(vmem_limit_bytes=VMEM_LIMIT),
        name="mid",
    )(x, o_a, o_b, w_out.astype(bf16), g_cross.reshape(1, d), wq.reshape(d, hx).astype(bf16),
      k, v, wo.reshape(hx, d).astype(bf16))


def kernel(x, mem, g_mix, w_in, g_cq, g_ckv, g_kidx, w_uq, w_iq, w_uk, w_uv, rel_bias, conv_w, A_log, dt_bias, g_onorm, w_out, g_cross, g_mem, wq_x, wk_x, wv_x, wo_x, g_ffn, w_pq, sub_keys, u_emb, v_emb, g_final):
    bsz, seq, d = x.shape
    for l in range(DEPTH):
        cq, ckv, kw, qkv, z, ab, abt = in_proj_pallas(x, g_mix[l], w_in[l])
        o_a = dsa_pallas(cq, ckv, kw, abt, g_cq[l], g_ckv[l], g_kidx[l],
                         w_uq[l], w_iq[l], w_uk[l], w_uv[l], rel_bias)
        o_b = gdn_pallas(qkv, z, ab, abt, conv_w[l], A_log[l], dt_bias[l], g_onorm[l])
        x = mid_pallas(x, o_a, o_b, w_out[l], g_cross[l], mem, g_mem[l],
                       wq_x[l], wk_x[l], wv_x[l], wo_x[l])
        xf = x.reshape(bsz * seq, d)
        y8 = peer_pallas(xf, g_ffn[l], w_pq[l], sub_keys[l], u_emb[l], v_emb[l])
        if l + 1 < DEPTH:
            x = residual_pallas(xf, y8).reshape(bsz, seq, d)
    return residual_pallas(xf, y8, g_final).reshape(bsz, seq, d)
```

```python
import math
from functools import partial
import jax
import jax.numpy as jnp
from jax import lax
import numpy as np
from jax.experimental import pallas as pl
from jax.experimental.pallas import tpu as pltpu

DEPTH = 1

CHUNK = 64
Q_BLOCK = 128
EPS = 1e-6

A_HEADS = 8
A_HEAD_DIM = 64
A_Q_LORA = 256
A_KV_LORA = 256
IDX_HEADS = 8
IDX_DIM = 64
IDX_TOPK_MAX = 256
ATTN_SCALE = A_HEAD_DIM ** -0.5
IDX_SCALE = (IDX_HEADS * IDX_DIM) ** -0.5

B_HEADS = 4
B_HEAD_DIM = 128
B_QK = B_HEADS * B_HEAD_DIM
CONV_WIDTH = 4

REL_BUCKETS = 32
REL_MAX_DIST = 128

X_HEADS = 4
X_HEAD_DIM = 128

P_HEADS = 8
N_KEYS = 128
N_EXPERTS = N_KEYS * N_KEYS
P_TOPK = 16
P_QDIM = 256

COL_WIDTHS = (A_Q_LORA, A_KV_LORA, IDX_DIM, IDX_HEADS, B_QK, B_QK, B_QK, B_QK, B_HEADS, B_HEADS)
MIX_WIDTH = A_HEADS * A_HEAD_DIM + B_HEADS * B_HEAD_DIM


def t5_bucket(rel):
    half = REL_BUCKETS // 2
    max_exact = half // 2
    n = jnp.abs(rel)
    nf = jnp.maximum(n, max_exact).astype(jnp.float32)
    large = max_exact + (jnp.log(nf / max_exact) / math.log(REL_MAX_DIST / max_exact)
                         * (half - max_exact)).astype(jnp.int32)
    large = jnp.minimum(large, half - 1)
    return jnp.where(rel > 0, half, 0) + jnp.where(n < max_exact, n, large)


INT_MIN = -2147483648
NEG_BIG = -1e30
KEY_TILE = 512
KEY_PAD = KEY_TILE - Q_BLOCK
VMEM_LIMIT = 56 * 1024 * 1024


def _rms(x, g):
    return x * lax.rsqrt(jnp.mean(x * x, axis=-1, keepdims=True) + EPS) * g


def _nt_dot(a, b):
    return lax.dot_general(a, b, (((1,), (1,)), ((), ())), preferred_element_type=jnp.float32)


def _dsa_prep_kernel(cq_ref, ckv_ref, kw_ref, gcq_ref, gckv_ref, gk_ref,
                       wiqt_ref, wuqt_ref, wuk_ref,
                       at_ref, qlatt_ref, ckvn_ref, kidxn_ref):
    f32, bf16 = jnp.float32, jnp.bfloat16
    Q = Q_BLOCK
    kw = kw_ref[0]
    cqn = _rms(cq_ref[0], gcq_ref[...]).astype(bf16)
    qit = _nt_dot(wiqt_ref[...], cqn)
    qt = _nt_dot(wuqt_ref[...], cqn)
    for h in range(A_HEADS):
        cols = slice(h * Q, (h + 1) * Q)
        at_ref[0, 0, :, cols] = qit[h * IDX_DIM:(h + 1) * IDX_DIM, :].astype(bf16)
        qh = qt[h * A_HEAD_DIM:(h + 1) * A_HEAD_DIM, :].astype(bf16)
        ql = jnp.dot(wuk_ref[h], qh, preferred_element_type=f32) * ATTN_SCALE
        qlatt_ref[0, 0, :, cols] = ql.astype(bf16)
    ckvn_ref[0] = _rms(ckv_ref[0], gckv_ref[...]).astype(bf16)
    kidxn_ref[0] = _rms(kw[:, :IDX_DIM], gk_ref[...]).astype(bf16)


def _dsa_main_kernel(at_ref, qlatt_ref, wt_ref, kidx_ref, ckv_ref, bucket_ref, far_ref, rb_ref,
                     wuvt_ref, o_ref, bias0_ref, sc_ref, big_ref, acc_ref, m_ref, l_ref, *, topk):
    f32, i32, bf16 = jnp.float32, jnp.int32, jnp.bfloat16
    Q, T = Q_BLOCK, KEY_TILE
    i = pl.program_id(1)

    @pl.when(jnp.logical_and(pl.program_id(0) == 0, i == 0))
    def _():
        bucket = bucket_ref[...]
        for h in range(A_HEADS):
            far_bias = rb_ref[far_ref[0], h]
            bias = jnp.zeros((T, Q), f32)
            for b in range(REL_BUCKETS):
                bias = jnp.where(bucket == b, rb_ref[b, h] - far_bias, bias)
            bias0_ref[h] = bias
    e = (i + 1) * Q
    nt = (i + T // Q) // (T // Q)
    kf = jnp.float32(topk)
    wsc = wt_ref[0] * IDX_SCALE
    sub = lax.broadcasted_iota(i32, (T, Q), 0)
    qlane = lax.broadcasted_iota(i32, (T, Q), 1)
    limit = ((i * Q + qlane) // CHUNK + 1) * CHUNK
    at = at_ref[0, 0]

    def tile_start(j):
        return pl.multiple_of(e + KEY_PAD - (j + 1) * T, 128)

    def key_pos(j):
        return tile_start(j) - KEY_PAD + sub

    def score_tile(j, carry):
        kid = kidx_ref[0, pl.ds(tile_start(j), T), :]
        big_ref[...] = jnp.dot(kid, at, preferred_element_type=f32)
        s = jnp.zeros((T, Q), f32)
        for h in range(IDX_HEADS):
            s = s + wsc[h:h + 1, :] * jnp.maximum(big_ref[:, h * Q:(h + 1) * Q], 0.0)
        bits = lax.bitcast_convert_type(s, i32)
        key = jnp.where(bits < 0, bits ^ jnp.int32(0x7FFFFFFF), bits)
        key = jnp.where(s == 0.0, 0, key)
        kpos = key_pos(j)
        key = jnp.where(kpos >= 0, key, INT_MIN)
        sc_ref[j] = jnp.where(kpos < limit, key, INT_MIN)
        return carry

    lax.fori_loop(0, nt, score_tile, 0)

    def count_where(pred):
        def body(j, cnt):
            hit = jnp.where(pred(sc_ref[j], key_pos(j)), 1.0, 0.0)
            return cnt + jnp.sum(hit.reshape(T // 64, 64, Q), axis=0)
        cnt = lax.fori_loop(0, nt, body, jnp.zeros((64, Q), f32))
        return jnp.sum(cnt, axis=0, keepdims=True)

    def bit_body(b, carry):
        u, cacc = carry
        cand = u | lax.shift_left(jnp.int32(1), 31 - b)
        tvec = cand ^ jnp.int32(INT_MIN)
        tot = count_where(lambda k, kpos: k >= tvec)
        ok = tot >= kf
        return jnp.where(ok, cand, u), jnp.where(ok, tot, cacc)

    u, cacc = lax.fori_loop(0, 32, bit_body, (jnp.zeros((1, Q), i32), jnp.zeros((1, Q), f32)))
    thr = jnp.maximum(u ^ jnp.int32(INT_MIN), INT_MIN + 1)
    overflow = jnp.where(u != 0, cacc, 0.0) > kf
    n_over = jnp.max(jnp.where(overflow, 1.0, 0.0), axis=1, keepdims=True)[0, 0]

    @pl.when(n_over > 0.0)
    def _():
        need = kf - count_where(lambda k, kpos: k > thr)

        def cut_body(b, cut):
            cand = cut | lax.shift_left(jnp.int32(1), 14 - b)
            cnt = count_where(lambda k, kpos: jnp.where(k == thr, kpos, cand) < cand)
            return jnp.where(cnt <= need, cand, cut)

        cut = lax.fori_loop(0, 15, cut_body, jnp.zeros((1, Q), i32))

        def drop_tile(j, carry):
            k = sc_ref[j]
            drop = jnp.where(k == thr, key_pos(j), -1) >= cut
            sc_ref[j] = jnp.where(drop, INT_MIN, k)
            return carry

        lax.fori_loop(0, nt, drop_tile, 0)

    m_ref[...] = jnp.full(m_ref.shape, NEG_BIG, f32)
    l_ref[...] = jnp.zeros(l_ref.shape, f32)
    acc_ref[...] = jnp.zeros(acc_ref.shape, f32)
    qlatt = qlatt_ref[0, 0]

    def attn_tile(j, with_bias):
        kv = ckv_ref[0, pl.ds(tile_start(j), T), :]
        kvt = kv.T
        mask_add = jnp.where(sc_ref[j] >= thr, 0.0, NEG_BIG)
        pair_cols = [slice(g * 2 * Q, (g + 1) * 2 * Q) for g in range(A_HEADS // 2)]
        logits = lambda g: jnp.dot(kv, qlatt[:, pair_cols[g]], preferred_element_type=f32)
        x_next = logits(0)
        for g in range(A_HEADS // 2):
            x_pair = x_next
            if g + 1 < A_HEADS // 2:
                x_next = logits(g + 1)
            ps, alphas = [], []
            for u in range(2):
                h = 2 * g + u
                cols = slice(h * Q, (h + 1) * Q)
                x = x_pair[:, u * Q:(u + 1) * Q] + mask_add
                if with_bias:
                    x = x + bias0_ref[h]
                m_prev = m_ref[:, cols]
                m_new = jnp.maximum(m_prev, jnp.max(x, axis=0, keepdims=True))
                p = jnp.exp(x - m_new)
                alpha = jnp.exp(m_prev - m_new)
                l_ref[:, cols] = alpha * l_ref[:, cols] + jnp.sum(p, axis=0, keepdims=True)
                m_ref[:, cols] = m_new
                ps.append(p.astype(bf16))
                alphas.append(alpha)
            pv = jnp.dot(kvt, jnp.concatenate(ps, axis=1), preferred_element_type=f32)
            acc_ref[:, pair_cols[g]] = (
                jnp.concatenate(alphas, axis=1) * acc_ref[:, pair_cols[g]] + pv)

    attn_tile(0, True)

    def attn_body(j, carry):
        attn_tile(j, False)
        return carry

    lax.fori_loop(1, nt, attn_body, 0)

    inv_l = 1.0 / l_ref[...]
    outs = []
    for h in range(A_HEADS):
        cols = slice(h * Q, (h + 1) * Q)
        o_lat_t = (acc_ref[:, cols] * inv_l[:, cols]).astype(bf16)
        outs.append(jnp.dot(wuvt_ref[h], o_lat_t, preferred_element_type=f32))
    o_ref[0] = jnp.concatenate(outs, axis=0).T


def dsa_pallas(cq, ckv, kw, widx_t, g_cq, g_ckv, g_kidx, w_uq, w_iq, w_uk, w_uv, rel_bias):
    bsz, seq, _ = cq.shape
    f32, bf16 = jnp.float32, jnp.bfloat16
    Q, T, H = Q_BLOCK, KEY_TILE, A_HEADS
    nblk = seq // Q
    topk = min(IDX_TOPK_MAX, seq // 4)
    wiqt = w_iq.reshape(A_Q_LORA, IDX_HEADS * IDX_DIM).T.astype(bf16)
    wuqt = w_uq.reshape(A_Q_LORA, H * A_HEAD_DIM).T.astype(bf16)
    wuk = jnp.transpose(w_uk, (1, 0, 2)).astype(bf16)
    wuvt = jnp.transpose(w_uv, (1, 2, 0)).astype(bf16)
    tok = lambda w: pl.BlockSpec((1, Q, w), lambda b, t: (b, t, 0))
    full = lambda shape: pl.BlockSpec(shape, lambda b, t: (0,) * len(shape))
    blk = lambda r: pl.BlockSpec((1, 1, r, H * Q), lambda b, t: (b, t, 0, 0))
    a_t, qlat_t, ckvn, kidxn = pl.pallas_call(
        _dsa_prep_kernel,
        grid=(bsz, nblk),
        in_specs=[tok(A_Q_LORA), tok(A_KV_LORA), tok(128),
                  full((1, A_Q_LORA)), full((1, A_KV_LORA)), full((1, IDX_DIM)),
                  full(wiqt.shape), full(wuqt.shape), full(wuk.shape)],
        out_specs=[blk(IDX_DIM), blk(A_KV_LORA), tok(A_KV_LORA), tok(IDX_DIM)],
        out_shape=[jax.ShapeDtypeStruct((bsz, nblk, IDX_DIM, H * Q), bf16),
                   jax.ShapeDtypeStruct((bsz, nblk, A_KV_LORA, H * Q), bf16),
                   jax.ShapeDtypeStruct((bsz, seq, A_KV_LORA), bf16),
                   jax.ShapeDtypeStruct((bsz, seq, IDX_DIM), bf16)],
        name="dsa_prep",
    )(cq, ckv, kw, g_cq.reshape(1, -1), g_ckv.reshape(1, -1), g_kidx.reshape(1, -1),
      wiqt, wuqt, wuk)
    ckvp = jnp.pad(ckvn, ((0, 0), (KEY_PAD, 0), (0, 0)))
    kidxp = jnp.pad(kidxn, ((0, 0), (KEY_PAD, 0), (0, 0)))
    rel = (jnp.arange(T, dtype=jnp.int32)[:, None] - jnp.arange(Q, dtype=jnp.int32)[None, :]
           - KEY_PAD)
    bucket0 = t5_bucket(rel)
    bucket_far = t5_bucket(jnp.full((1,), -KEY_PAD - 1, jnp.int32))
    smem = pl.BlockSpec(memory_space=pltpu.SMEM)
    nt_max = (nblk - 1 + T // Q) // (T // Q)
    skey = seq + KEY_PAD
    return pl.pallas_call(
        partial(_dsa_main_kernel, topk=topk),
        grid=(bsz, nblk),
        in_specs=[blk(IDX_DIM), blk(A_KV_LORA),
                  pl.BlockSpec((1, IDX_HEADS, Q), lambda b, t: (b, 1, t)),
                  pl.BlockSpec((1, skey, IDX_DIM), lambda b, t: (b, 0, 0)),
                  pl.BlockSpec((1, skey, A_KV_LORA), lambda b, t: (b, 0, 0)),
                  full((T, Q)), smem, smem, full(wuvt.shape)],
        out_specs=tok(H * A_HEAD_DIM),
        out_shape=jax.ShapeDtypeStruct((bsz, seq, H * A_HEAD_DIM), f32),
        scratch_shapes=[pltpu.VMEM((H, T, Q), f32),
                        pltpu.VMEM((nt_max, T, Q), jnp.int32),
                        pltpu.VMEM((T, H * Q), f32),
                        pltpu.VMEM((A_KV_LORA, H * Q), f32),
                        pltpu.VMEM((1, H * Q), f32),
                        pltpu.VMEM((1, H * Q), f32)],
        compiler_params=pltpu.CompilerParams(
            dimension_semantics=("arbitrary", "arbitrary"), vmem_limit_bytes=VMEM_LIMIT),
        name="dsa_main",
    )(a_t, qlat_t, widx_t, kidxp, ckvp, bucket0, bucket_far, rel_bias.astype(f32), wuvt)


PEER_SCORE_TOKENS = 256
PEER_GATHER_TOKENS = 64
PEER_SLOTS = P_HEADS * P_TOPK
WORDS_PER_ROW = 4
PEER_TILES = 16


def _top16(s, order=None, payload=None):
    if order is None:
        order = lax.broadcasted_iota(jnp.int32, s.shape, 0).astype(jnp.float32)
    vals, picks = [], []
    for _ in range(P_TOPK):
        m = jnp.max(s, axis=0, keepdims=True)
        pos = jnp.min(jnp.where(s == m, order, float(N_EXPERTS)), axis=0, keepdims=True)
        hit = order == pos
        vals.append(m)
        if payload is None:
            picks.append(pos)
        else:
            picks.append(jnp.max(jnp.where(hit, payload, -1.0), axis=0, keepdims=True))
        s = jnp.where(hit, -jnp.inf, s)
    return vals, picks


def _staircase_candidates(v1, i1, v2, i2):
    v1m, i1m = jnp.concatenate(v1, axis=0), jnp.concatenate(i1, axis=0)
    v2m, i2m = jnp.concatenate(v2, axis=0), jnp.concatenate(i2, axis=0)
    t = v1m.shape[1]
    sub8 = lax.broadcasted_iota(jnp.int32, (8, t), 0).astype(jnp.float32)
    cand, cidx, rank = [], [], []
    for a in range(8):
        cand.append(v1[a] + v2m[:8])
        cidx.append(i1[a] * float(N_KEYS) + i2m[:8])
        rank.append(sub8 + float(a * P_TOPK))
    cand.append(v1[0] + v2m[8:])
    cidx.append(i1[0] * float(N_KEYS) + i2m[8:])
    rank.append(sub8 + 8.0)
    cand.append(v1m[8:] + v2[0])
    cidx.append(i1m[8:] * float(N_KEYS) + i2[0])
    rank.append((sub8 + 8.0) * float(P_TOPK))
    cat = lambda xs: jnp.concatenate(xs, axis=0)
    return cat(cand), cat(rank), cat(cidx)


def _peer_score_kernel(x_ref, g_ref, wpqt_ref, sk_ref, hn_ref, eidx_ref, gate_ref):
    f32, bf16 = jnp.float32, jnp.bfloat16
    h = _rms(x_ref[...], g_ref[...])
    hb = h.astype(bf16)
    for s in range(h.shape[1] // 128):
        hn_ref[pl.ds(s, h.shape[0], stride=8), :] = h[:, s * 128:(s + 1) * 128]
    qrt = _nt_dot(wpqt_ref[...], hb)
    half = P_QDIM // 2
    e_rows, g_rows = [], []
    for hd in range(P_HEADS):
        tops = []
        for p in range(2):
            qhp = qrt[(hd * 2 + p) * half:(hd * 2 + p + 1) * half, :].astype(bf16)
            s = jnp.dot(sk_ref[hd * 2 + p], qhp, preferred_element_type=f32)
            tops.append(_top16(s))
        (v1, i1), (v2, i2) = tops
        best, be = _top16(*_staircase_candidates(v1, i1, v2, i2))
        ex = [jnp.exp(b - best[0]) for b in best]
        den = ex[0]
        for k in range(1, P_TOPK):
            den = den + ex[k]
        inv = 1.0 / den
        e_rows += be
        g_rows += [x * inv for x in ex]
    eidx_ref[...] = (jnp.concatenate(e_rows, axis=0).T * float(WORDS_PER_ROW)).astype(jnp.int32)
    gate_ref[...] = jnp.concatenate(g_rows, axis=0).T


def _diag_mask():
    r = lax.broadcasted_iota(jnp.int32, (8, PEER_SLOTS * 8), 0)
    c = lax.broadcasted_iota(jnp.int32, (8, PEER_SLOTS * 8), 1)
    return (c & 7) == r


def _gather_pair(idx_ref, tab_ref, g_refs, t0):
    views = [idx_ref.at[t0 + u] for u in range(2)]
    for r in range(PEER_SLOTS):
        for view, g_ref in zip(views, g_refs):
            row0 = pl.multiple_of(view[r], WORDS_PER_ROW)
            g_ref[r * WORDS_PER_ROW:(r + 1) * WORDS_PER_ROW, :] = tab_ref[
                pl.ds(row0, WORDS_PER_ROW), :]


def _gather_pipeline(idx_ref, tab_ref, tiles, n_tokens, consume):
    pairs = len(tiles) // 2
    for g_ref in tiles[-2:]:
        g_ref[...] = jnp.zeros(g_ref.shape, g_ref.dtype)

    def trip(i, carry):
        for k in range(pairs):
            pair = pairs * i + k
            prev_tiles = tiles[2 * k - 2:2 * k] if k else tiles[-2:]
            _gather_pair(idx_ref, tab_ref, tiles[2 * k:2 * k + 2], pair * 2)
            for u in range(2):
                consume(prev_tiles[u], jnp.maximum(pair - 1, 0), u)
        return carry

    lax.fori_loop(0, n_tokens // (2 * pairs), trip, 0)
    for u in range(2):
        consume(tiles[-2 + u], n_tokens // 2 - 1, u)


def _peer_act_kernel(idx_ref, hn_ref, tab_ref, pick_ref, a_ref, m_ref, *tiles):
    f32, bf16 = jnp.float32, jnp.bfloat16
    tb = a_ref.shape[0]
    width = PEER_SLOTS * 8

    def dots(g_ref, tp, u):
        hp = hn_ref[pl.ds(pl.multiple_of(tp * 16, 16), 16), :].astype(bf16)
        m = _nt_dot(hp, pltpu.bitcast(g_ref[...], bf16))
        for s in range(8):
            m_ref[pl.ds(tp * 2 + u, 1), s * width:(s + 1) * width] = m[u * 8 + s:u * 8 + s + 1, :]

    _gather_pipeline(idx_ref, tab_ref, tiles, tb, dots)
    mf = m_ref[...]
    m_hi = mf.astype(bf16)
    m_lo = (mf - m_hi.astype(f32)).astype(bf16)
    both = jnp.dot(jnp.concatenate([m_hi, m_lo], axis=0), pick_ref[...],
                   preferred_element_type=f32)
    a_ref[...] = both[:tb] + both[tb:]


def _peer_out_kernel(idx_ref, a_ref, gate_ref, tab_ref, o_ref, w_ref, *tiles):
    f32, bf16 = jnp.float32, jnp.bfloat16
    tb = a_ref.shape[0]
    a = a_ref[...]
    act = 0.5 * a * (1.0 + lax.erf(a * (2.0 ** -0.5)))
    wgt = (gate_ref[...] * act).astype(bf16)
    rr = lax.broadcasted_iota(jnp.int32, (PEER_SLOTS, PEER_SLOTS * 8), 0)
    cc = lax.broadcasted_iota(jnp.int32, (PEER_SLOTS, PEER_SLOTS * 8), 1)
    expand = jnp.where((cc >> 3) == rr, 1.0, 0.0).astype(bf16)
    w_ref[...] = jnp.dot(wgt, expand, preferred_element_type=f32)
    diag = _diag_mask()

    def combine(g_ref, tp, u):
        t = tp * 2 + u
        wrow = jnp.broadcast_to(w_ref[pl.ds(t, 1), :], (8, PEER_SLOTS * 8))
        wsel = jnp.where(diag, wrow, 0.0).astype(bf16)
        o_ref[pl.ds(pl.multiple_of(t * 8, 8), 8), :] = jnp.dot(
            wsel, pltpu.bitcast(g_ref[...], bf16), preferred_element_type=f32)

    _gather_pipeline(idx_ref, tab_ref, tiles, tb, combine)


PACK_ROWS = 512


def _pack_table_kernel(x_ref, o_ref):
    rows = x_ref.shape[0]
    bf16_bits = lambda v: lax.bitcast_convert_type(
        v.astype(jnp.bfloat16).astype(jnp.float32), jnp.int32)
    for c in range(WORDS_PER_ROW):
        lo = bf16_bits(x_ref[:, (2 * c) * 128:(2 * c + 1) * 128])
        hi = bf16_bits(x_ref[:, (2 * c + 1) * 128:(2 * c + 2) * 128])
        word = (hi & jnp.int32(-65536)) | lax.shift_right_logical(lo, 16)
        o_ref[pl.ds(c, rows, stride=WORDS_PER_ROW), :] = word


def _pack_table(tab):
    n_e, d = tab.shape
    return pl.pallas_call(
        _pack_table_kernel,
        grid=(n_e // PACK_ROWS,),
        in_specs=[pl.BlockSpec((PACK_ROWS, d), lambda i: (i, 0))],
        out_specs=pl.BlockSpec((PACK_ROWS * WORDS_PER_ROW, 128), lambda i: (i, 0)),
        out_shape=jax.ShapeDtypeStruct((n_e * WORDS_PER_ROW, 128), jnp.int32),
        name="pack_table",
    )(tab)


def peer_pallas(x, g_ffn, w_pq, sub_keys, u_emb, v_emb):
    n_tok, d = x.shape
    f32, bf16 = jnp.float32, jnp.bfloat16
    ts, tg = PEER_SCORE_TOKENS, PEER_GATHER_TOKENS
    wpqt = w_pq.reshape(d, P_HEADS * P_QDIM).T.astype(bf16)
    sk = sub_keys.reshape(P_HEADS * 2, N_KEYS, P_QDIM // 2).astype(bf16)
    hn, eidx, gate = pl.pallas_call(
        _peer_score_kernel,
        grid=(n_tok // ts,),
        in_specs=[pl.BlockSpec((ts, d), lambda i: (i, 0)),
                  pl.BlockSpec((1, d), lambda i: (0, 0)),
                  pl.BlockSpec(wpqt.shape, lambda i: (0, 0)),
                  pl.BlockSpec(sk.shape, lambda i: (0, 0, 0))],
        out_specs=[pl.BlockSpec((ts * 8, 128), lambda i: (i, 0)),
                   pl.BlockSpec((ts, PEER_SLOTS), lambda i: (i, 0)),
                   pl.BlockSpec((ts, PEER_SLOTS), lambda i: (i, 0))],
        out_shape=[jax.ShapeDtypeStruct((n_tok * 8, 128), f32),
                   jax.ShapeDtypeStruct((n_tok, PEER_SLOTS), jnp.int32),
                   jax.ShapeDtypeStruct((n_tok, PEER_SLOTS), f32)],
        compiler_params=pltpu.CompilerParams(vmem_limit_bytes=VMEM_LIMIT),
        name="peer_score",
    )(x, g_ffn.reshape(1, d), wpqt, sk)
    utab, vtab = _pack_table(u_emb), _pack_table(v_emb)
    idx_spec = pl.BlockSpec((tg, PEER_SLOTS), lambda i: (i, 0), memory_space=pltpu.SMEM)
    tab_spec = pl.BlockSpec(memory_space=pltpu.VMEM)
    slot_spec = pl.BlockSpec((tg, PEER_SLOTS), lambda i: (i, 0))
    row_spec = pl.BlockSpec((tg * 8, 128), lambda i: (i, 0))
    gbuf = pltpu.VMEM((PEER_SLOTS * WORDS_PER_ROW, 128), jnp.int32)
    cparams = pltpu.CompilerParams(vmem_limit_bytes=VMEM_LIMIT)
    flat = jnp.arange(8 * PEER_SLOTS * 8, dtype=jnp.int32)
    part, col = flat // (PEER_SLOTS * 8), flat % (PEER_SLOTS * 8)
    pick = (col[:, None] == 8 * jnp.arange(PEER_SLOTS, dtype=jnp.int32)[None, :] + part[:, None]
            ).astype(bf16)
    act = pl.pallas_call(
        _peer_act_kernel,
        grid=(n_tok // tg,),
        in_specs=[idx_spec, row_spec, tab_spec, pl.BlockSpec(pick.shape, lambda i: (0, 0))],
        out_specs=slot_spec,
        out_shape=jax.ShapeDtypeStruct((n_tok, PEER_SLOTS), f32),
        scratch_shapes=[pltpu.VMEM((tg, 8 * PEER_SLOTS * 8), f32)] + [gbuf] * PEER_TILES,
        compiler_params=cparams,
        name="peer_act",
    )(eidx, hn, utab, pick)
    out = pl.pallas_call(
        _peer_out_kernel,
        grid=(n_tok // tg,),
        in_specs=[idx_spec, slot_spec, slot_spec, tab_spec],
        out_specs=row_spec,
        out_shape=jax.ShapeDtypeStruct((n_tok * 8, 128), f32),
        scratch_shapes=[pltpu.VMEM((tg, PEER_SLOTS * 8), f32)] + [gbuf] * PEER_TILES,
        compiler_params=cparams,
        name="peer_out",
    )(eidx, act, gate, vtab)
    return out


def _token_rows(y_ref):
    tm = y_ref.shape[0] // 8
    return jnp.concatenate([y_ref[pl.ds(s, tm, stride=8), :] for s in range(8)], axis=1)


def _final_kernel(x_ref, y_ref, g_ref, o_ref):
    o_ref[...] = _rms(x_ref[...] + _token_rows(y_ref), g_ref[...])


def _residual_kernel(x_ref, y_ref, o_ref):
    o_ref[...] = x_ref[...] + _token_rows(y_ref)


def residual_pallas(x, y8, g=None):
    n, d = x.shape
    tm = 512
    row = pl.BlockSpec((tm, d), lambda i: (i, 0))
    row8 = pl.BlockSpec((tm * 8, 128), lambda i: (i, 0))
    if g is None:
        kern, extra, extra_specs, name = _residual_kernel, (), [], "residual"
    else:
        kern, extra, name = _final_kernel, (g.reshape(1, d),), "final_rmsnorm"
        extra_specs = [pl.BlockSpec((1, d), lambda i: (0, 0))]
    return pl.pallas_call(
        kern,
        grid=(n // tm,),
        in_specs=[row, row8] + extra_specs,
        out_specs=row,
        out_shape=jax.ShapeDtypeStruct((n, d), x.dtype),
        name=name,
    )(x, y8, *extra)


IN_PROJ_TOKENS = 512
GDN_COLS = 3 * B_QK
GATE_ROWS = 8


def _in_proj_kernel(x_ref, g_ref, wa_ref, wkw_ref, wqkv_ref, wz_ref, wab_ref, wabt_ref,
                    cq_ref, ckv_ref, kw_ref, qkv_ref, z_ref, ab_ref, abt_ref):
    f32 = jnp.float32
    hb = _rms(x_ref[0], g_ref[...]).astype(jnp.bfloat16)
    a = jnp.dot(hb, wa_ref[...], preferred_element_type=f32)
    cq_ref[0] = a[:, :A_Q_LORA]
    ckv_ref[0] = a[:, A_Q_LORA:]
    kw_ref[0] = jnp.dot(hb, wkw_ref[...], preferred_element_type=f32)
    qkv_ref[0] = jnp.dot(hb, wqkv_ref[...], preferred_element_type=f32)
    z_ref[0] = jnp.dot(hb, wz_ref[...], preferred_element_type=f32)
    ab_ref[0] = jnp.dot(hb, wab_ref[...], preferred_element_type=f32)
    abt_ref[0] = _nt_dot(wabt_ref[...], hb)


def in_proj_pallas(x, g_mix, w_in):
    bsz, seq, d = x.shape
    f32, bf16 = jnp.float32, jnp.bfloat16
    tm = min(IN_PROJ_TOKENS, seq)
    o = np.cumsum((0,) + COL_WIDTHS)
    wb = w_in.astype(bf16)
    pad_cols = lambda w: jnp.pad(w, ((0, 0), (0, 128 - w.shape[1])))
    wa = wb[:, o[0]:o[2]]
    wkw = pad_cols(wb[:, o[2]:o[4]])
    wqkv = wb[:, o[4]:o[7]]
    wz = wb[:, o[7]:o[8]]
    wab = pad_cols(wb[:, o[8]:o[10]])
    wabt = jnp.concatenate([wb[:, o[8]:o[10]], wb[:, o[3]:o[4]]], axis=1).T
    full = lambda w: pl.BlockSpec(w.shape, lambda b, t: (0, 0))
    tok = lambda w: pl.BlockSpec((1, tm, w), lambda b, t: (b, t, 0))
    shp = lambda w: jax.ShapeDtypeStruct((bsz, seq, w), f32)
    return pl.pallas_call(
        _in_proj_kernel,
        grid=(bsz, seq // tm),
        in_specs=[tok(d), pl.BlockSpec((1, d), lambda b, t: (0, 0)),
                  full(wa), full(wkw), full(wqkv), full(wz), full(wab), full(wabt)],
        out_specs=[tok(A_Q_LORA), tok(A_KV_LORA), tok(128), tok(GDN_COLS), tok(B_QK), tok(128),
                   pl.BlockSpec((1, GATE_ROWS + IDX_HEADS, tm), lambda b, t: (b, 0, t))],
        out_shape=[shp(A_Q_LORA), shp(A_KV_LORA), shp(128), shp(GDN_COLS), shp(B_QK), shp(128),
                   jax.ShapeDtypeStruct((bsz, GATE_ROWS + IDX_HEADS, seq), f32)],
        compiler_params=pltpu.CompilerParams(vmem_limit_bytes=VMEM_LIMIT),
        name="in_proj",
    )(x, g_mix.reshape(1, d), wa, wkw, wqkv, wz, wab, wabt)


def _softplus(x):
    return jnp.maximum(x, 0.0) + jnp.log1p(jnp.exp(-jnp.abs(x)))


def _sigmoid(x):
    return 1.0 / (1.0 + jnp.exp(-x))


def _gdn_gates(pre, a_log, dt_bias, is_decay):
    g = -jnp.exp(a_log) * _softplus(pre + dt_bias)
    return jnp.where(is_decay, g, _sigmoid(pre))


def _gdn_prep_kernel(qkv_ref, halo_ref, cw_ref, ab_ref, abt_ref, alc_ref, dtc_ref, alr_ref, dtr_ref,
                     q_ref, k_ref, v_ref, gc_ref, gr_ref):
    tm = qkv_ref.shape[1]
    x = qkv_ref[0]
    halo = jnp.where(pl.program_id(1) > 0, halo_ref[0], 0.0)
    full = jnp.concatenate([halo, x], axis=0)
    y = x * cw_ref[CONV_WIDTH - 1:CONV_WIDTH, :]
    for back in range(1, CONV_WIDTH):
        shifted = pltpu.roll(full, back, axis=0)[8:, :]
        y = y + shifted * cw_ref[CONV_WIDTH - 1 - back:CONV_WIDTH - back, :]
    y = y * _sigmoid(y)
    for h in range(B_HEADS):
        cols = slice(h * B_HEAD_DIM, (h + 1) * B_HEAD_DIM)
        qh = y[:, h * B_HEAD_DIM:(h + 1) * B_HEAD_DIM]
        kh = y[:, B_QK + h * B_HEAD_DIM:B_QK + (h + 1) * B_HEAD_DIM]
        q_ref[0, :, cols] = qh * lax.rsqrt(
            jnp.sum(qh * qh, axis=-1, keepdims=True) + EPS) * (B_HEAD_DIM ** -0.5)
        k_ref[0, :, cols] = kh * lax.rsqrt(jnp.sum(kh * kh, axis=-1, keepdims=True) + EPS)
    v_ref[0] = y[:, 2 * B_QK:]
    lane = lax.broadcasted_iota(jnp.int32, (tm, 128), 1)
    gates_c = _gdn_gates(ab_ref[0], alc_ref[...], dtc_ref[...], lane < B_HEADS)
    row = lax.broadcasted_iota(jnp.int32, (GATE_ROWS, tm), 0)
    gates_r = _gdn_gates(abt_ref[0], alr_ref[...], dtr_ref[...], row < B_HEADS)
    ti = lax.broadcasted_iota(jnp.int32, (tm, tm), 0)
    tj = lax.broadcasted_iota(jnp.int32, (tm, tm), 1)
    same_chunk = (ti // CHUNK) == (tj // CHUNK)
    hi = lax.Precision.HIGHEST
    lower = jnp.where(same_chunk & (tj <= ti), 1.0, 0.0)
    upper = jnp.where(same_chunk & (ti <= tj), 1.0, 0.0)
    cum_c = jnp.dot(lower, gates_c, preferred_element_type=jnp.float32, precision=hi)
    cum_r = jnp.dot(gates_r, upper, preferred_element_type=jnp.float32, precision=hi)
    gc_ref[0] = jnp.where(lane < B_HEADS, cum_c, gates_c)
    gr_ref[0] = jnp.where(row < B_HEADS, cum_r, gates_r)


def _gdn_main_kernel(q_ref, k_ref, v_ref, z_ref, gc_ref, gr_ref, gon_ref, o_ref, s_ref):
    f32, bf16 = jnp.float32, jnp.bfloat16
    C, Dh = CHUNK, B_HEAD_DIM

    @pl.when(pl.program_id(1) == 0)
    def _():
        s_ref[...] = jnp.zeros(s_ref.shape, f32)

    ii = lax.broadcasted_iota(jnp.int32, (C, C), 0)
    jj = lax.broadcasted_iota(jnp.int32, (C, C), 1)
    causal = ii >= jj
    strict = ii > jj
    eye = jnp.where(ii == jj, 1.0, 0.0)
    mm = lambda a, b: jnp.dot(a, b, preferred_element_type=f32)
    n_chunks = q_ref.shape[1] // C
    units = []
    for c in range(n_chunks):
        rows = slice(c * C, (c + 1) * C)
        gates_c = gc_ref[0, rows, :]
        gates_r = gr_ref[0, :, c * C:(c + 1) * C]
        for h in range(B_HEADS):
            cols = slice(h * Dh, (h + 1) * Dh)
            gcum = jnp.broadcast_to(gates_c[:, h:h + 1], (C, Dh))
            beta = jnp.broadcast_to(gates_c[:, B_HEADS + h:B_HEADS + h + 1], (C, Dh))
            gcum_r = jnp.broadcast_to(gates_r[h:h + 1, :], (C, C))
            decay = jnp.where(causal, jnp.exp(jnp.minimum(gcum[:, :C] - gcum_r, 0.0)), 0.0)
            q, k, v = q_ref[0, rows, cols], k_ref[0, rows, cols], v_ref[0, rows, cols]
            qb, kb = q.astype(bf16), k.astype(bf16)
            kk = _nt_dot(kb, kb)
            qk = _nt_dot(qb, kb)
            egc = jnp.exp(gcum)
            g_last = gcum[C - 1:C, :]
            units.append(dict(
                c=c, h=h, rows=rows, cols=cols,
                neg_m=jnp.where(strict, -(beta[:, :C] * kk * decay), 0.0),
                rhs=jnp.concatenate([v * beta, k * (beta * egc)], axis=1),
                q_dec=(q * egc).astype(bf16), intra=(qk * decay).astype(bf16),
                k_tail=(k * jnp.exp(g_last - gcum)).astype(bf16), chunk_decay=jnp.exp(g_last)))
    def halves(a):
        a_h = a.astype(bf16)
        return a_h, (a - a_h.astype(f32)).astype(bf16)

    def mm3(a, b):
        (a_h, a_l), (b_h, b_l) = a, b
        return mm(a_h, b_h) + (mm(a_h, b_l) + mm(a_l, b_h))

    t_inv = [eye + un["neg_m"] for un in units]
    p_halves = [halves(un["neg_m"]) for un in units]
    for _ in range(5):
        p_halves = [halves(mm3(ph, ph)) for ph in p_halves]
        t_inv = [t + mm3(ph, halves(t)) for ph, t in zip(p_halves, t_inv)]
    sols = [mm3(halves(t), halves(un["rhs"])) for t, un in zip(t_inv, units)]
    states = [s_ref[h] for h in range(B_HEADS)]
    for c in range(n_chunks):
        group = [(un, sol) for un, sol in zip(units, sols) if un["c"] == c]
        sbs = [states[un["h"]].astype(bf16) for un, _ in group]
        vbs = [(sol[:, :Dh] - mm(sol[:, Dh:].astype(bf16), sb)).astype(bf16)
               for (un, sol), sb in zip(group, sbs)]
        outs = [mm(un["q_dec"], sb) + mm(un["intra"], vb)
                for (un, _), sb, vb in zip(group, sbs, vbs)]
        for (un, _), vb in zip(group, vbs):
            states[un["h"]] = states[un["h"]] * un["chunk_decay"] + lax.dot_general(
                un["k_tail"], vb, (((0,), (0,)), ((), ())), preferred_element_type=f32)
        for (un, _), o in zip(group, outs):
            o = o * lax.rsqrt(jnp.mean(o * o, axis=-1, keepdims=True) + EPS) * gon_ref[...]
            zz = z_ref[0, un["rows"], un["cols"]]
            o_ref[0, un["rows"], un["cols"]] = o * (zz * _sigmoid(zz))
    for h in range(B_HEADS):
        s_ref[h] = states[h]


def gdn_pallas(qkv, z, ab, abt, conv_w, a_log, dt_bias, g_onorm):
    bsz, seq, _ = qkv.shape
    f32 = jnp.float32
    tm = min(256, seq)
    zero4 = jnp.zeros((B_HEADS,), f32)
    lane_row = lambda v: jnp.pad(jnp.concatenate([v.astype(f32), zero4]), (0, 120)).reshape(1, 128)
    sub_col = lambda v: jnp.concatenate([v.astype(f32), zero4]).reshape(GATE_ROWS, 1)
    tok = lambda w, t=tm: pl.BlockSpec((1, t, w), lambda b, i: (b, i, 0))
    const = lambda shape: pl.BlockSpec(shape, lambda b, i: (0,) * len(shape))
    shp = lambda w: jax.ShapeDtypeStruct((bsz, seq, w), f32)
    q, k, v, gc, gr = pl.pallas_call(
        _gdn_prep_kernel,
        grid=(bsz, seq // tm),
        in_specs=[tok(GDN_COLS),
                  pl.BlockSpec((1, 8, GDN_COLS), lambda b, i: (b, jnp.maximum(i * (tm // 8) - 1, 0), 0)),
                  const((CONV_WIDTH, GDN_COLS)), tok(128),
                  pl.BlockSpec((1, GATE_ROWS, tm), lambda b, i: (b, 0, i)),
                  const((1, 128)), const((1, 128)), const((GATE_ROWS, 1)), const((GATE_ROWS, 1))],
        out_specs=[tok(B_QK), tok(B_QK), tok(B_QK), tok(128),
                   pl.BlockSpec((1, GATE_ROWS, tm), lambda b, i: (b, 0, i))],
        out_shape=[shp(B_QK), shp(B_QK), shp(B_QK), shp(128),
                   jax.ShapeDtypeStruct((bsz, GATE_ROWS, seq), f32)],
        compiler_params=pltpu.CompilerParams(vmem_limit_bytes=VMEM_LIMIT),
        name="gdn_prep",
    )(qkv, qkv, conv_w.astype(f32), ab, abt, lane_row(a_log), lane_row(dt_bias),
      sub_col(a_log), sub_col(dt_bias))
    ts = 2 * CHUNK
    return pl.pallas_call(
        _gdn_main_kernel,
        grid=(bsz, seq // ts),
        in_specs=[tok(B_QK, ts), tok(B_QK, ts), tok(B_QK, ts), tok(B_QK, ts), tok(128, ts),
                  pl.BlockSpec((1, GATE_ROWS, ts), lambda b, i: (b, 0, i)),
                  const((1, B_HEAD_DIM))],
        out_specs=tok(B_QK, ts),
        out_shape=shp(B_QK),
        scratch_shapes=[pltpu.VMEM((B_HEADS, B_HEAD_DIM, B_HEAD_DIM), f32)],
        compiler_params=pltpu.CompilerParams(dimension_semantics=("arbitrary", "arbitrary")),
        name="gdn_main",
    )(q, k, v, z, gc, gr, g_onorm.astype(f32).reshape(1, B_HEAD_DIM))


def _mem_kv_kernel(mem_ref, g_ref, wk_ref, wv_ref, k_ref, v_ref):
    f32, bf16 = jnp.float32, jnp.bfloat16
    mn = _rms(mem_ref[0], g_ref[...]).astype(bf16)
    k = jnp.dot(mn, wk_ref[...], preferred_element_type=f32)
    v = jnp.dot(mn, wv_ref[...], preferred_element_type=f32)
    for h in range(X_HEADS):
        cols = slice(h * X_HEAD_DIM, (h + 1) * X_HEAD_DIM)
        k_ref[0, h] = k[:, cols].astype(bf16)
        v_ref[0, h] = v[:, cols].astype(bf16)


def _mid_kernel(x_ref, oa_ref, ob_ref, wo_ref, gx_ref, wq_ref, k_ref, v_ref, wox_ref, o_ref):
    f32, bf16 = jnp.float32, jnp.bfloat16
    na = oa_ref.shape[2]
    x1 = (x_ref[0]
          + jnp.dot(oa_ref[0].astype(bf16), wo_ref[:na, :], preferred_element_type=f32)
          + jnp.dot(ob_ref[0].astype(bf16), wo_ref[na:, :], preferred_element_type=f32))
    hq = _rms(x1, gx_ref[...]).astype(bf16)
    q = jnp.dot(hq, wq_ref[...], preferred_element_type=f32)
    heads = []
    for h in range(X_HEADS):
        qh = q[:, h * X_HEAD_DIM:(h + 1) * X_HEAD_DIM].astype(bf16)
        lg = _nt_dot(qh, k_ref[0, h]) * (X_HEAD_DIM ** -0.5)
        p = jnp.exp(lg - jnp.max(lg, axis=-1, keepdims=True))
        p = (p / jnp.sum(p, axis=-1, keepdims=True)).astype(bf16)
        heads.append(jnp.dot(p, v_ref[0, h], preferred_element_type=f32).astype(bf16))
    o = jnp.concatenate(heads, axis=1)
    o_ref[0] = x1 + jnp.dot(o, wox_ref[...], preferred_element_type=f32)


def mid_pallas(x, o_a, o_b, w_out, g_cross, mem, g_mem, wq, wk, wv, wo):
    bsz, seq, d = x.shape
    f32, bf16 = jnp.float32, jnp.bfloat16
    hx = X_HEADS * X_HEAD_DIM
    m_len = mem.shape[1]
    const2 = lambda shape: pl.BlockSpec(shape, lambda b: (0,) * len(shape))
    kv_spec = pl.BlockSpec((1, X_HEADS, m_len, X_HEAD_DIM), lambda b: (b, 0, 0, 0))
    kv_shape = jax.ShapeDtypeStruct((bsz, X_HEADS, m_len, X_HEAD_DIM), bf16)
    k, v = pl.pallas_call(
        _mem_kv_kernel,
        grid=(bsz,),
        in_specs=[pl.BlockSpec((1, m_len, d), lambda b: (b, 0, 0)), const2((1, d)),
                  const2((d, hx)), const2((d, hx))],
        out_specs=[kv_spec, kv_spec],
        out_shape=[kv_shape, kv_shape],
        name="mem_kv",
    )(mem, g_mem.reshape(1, d), wk.reshape(d, hx).astype(bf16), wv.reshape(d, hx).astype(bf16))
    tm = min(256, seq)
    tok = lambda w: pl.BlockSpec((1, tm, w), lambda b, t: (b, t, 0))
    const = lambda shape: pl.BlockSpec(shape, lambda b, t: (0,) * len(shape))
    kv_spec2 = pl.BlockSpec((1, X_HEADS, m_len, X_HEAD_DIM), lambda b, t: (b, 0, 0, 0))
    return pl.pallas_call(
        _mid_kernel,
        grid=(bsz, seq // tm),
        in_specs=[tok(d), tok(o_a.shape[2]), tok(o_b.shape[2]), const((MIX_WIDTH, d)),
                  const((1, d)), const((d, hx)), kv_spec2, kv_spec2, const((hx, d))],
        out_specs=tok(d),
        out_shape=jax.ShapeDtypeStruct((bsz, seq, d), f32),
        compiler_params=pltpu.CompilerParams(vmem_limit_bytes=VMEM_LIMIT),
        name="mid",
    )(x, o_a, o_b, w_out.astype(bf16), g_cross.reshape(1, d), wq.reshape(d, hx).astype(bf16),
      k, v, wo.reshape(hx, d).astype(bf16))


def kernel(x, mem, g_mix, w_in, g_cq, g_ckv, g_kidx, w_uq, w_iq, w_uk, w_uv, rel_bias, conv_w, A_log, dt_bias, g_onorm, w_out, g_cross, g_mem, wq_x, wk_x, wv_x, wo_x, g_ffn, w_pq, sub_keys, u_emb, v_emb, g_final):
    bsz, seq, d = x.shape
    for l in range(DEPTH):
        cq, ckv, kw, qkv, z, ab, abt = in_proj_pallas(x, g_mix[l], w_in[l])
        o_a = dsa_pallas(cq, ckv, kw, abt, g_cq[l], g_ckv[l], g_kidx[l],
                         w_uq[l], w_iq[l], w_uk[l], w_uv[l], rel_bias)
        o_b = gdn_pallas(qkv, z, ab, abt, conv_w[l], A_log[l], dt_bias[l], g_onorm[l])
        x = mid_pallas(x, o_a, o_b, w_out[l], g_cross[l], mem, g_mem[l],
                       wq_x[l], wk_x[l], wv_x[l], wo_x[l])
        xf = x.reshape(bsz * seq, d)
        y8 = peer_pallas(xf, g_ffn[l], w_pq[l], sub_keys[l], u_emb[l], v_emb[l])
        if l + 1 < DEPTH:
            x = residual_pallas(xf, y8).reshape(bsz, seq, d)
    return residual_pallas(xf, y8, g_final).reshape(bsz, seq, d)
```

```python
import math
from functools import partial
import jax
import jax.numpy as jnp
from jax import lax
import numpy as np
from jax.experimental import pallas as pl
from jax.experimental.pallas import tpu as pltpu

DEPTH = 1

CHUNK = 64
Q_BLOCK = 128
EPS = 1e-6

A_HEADS = 8
A_HEAD_DIM = 64
A_Q_LORA = 256
A_KV_LORA = 256
IDX_HEADS = 8
IDX_DIM = 64
IDX_TOPK_MAX = 256
ATTN_SCALE = A_HEAD_DIM ** -0.5
IDX_SCALE = (IDX_HEADS * IDX_DIM) ** -0.5

B_HEADS = 4
B_HEAD_DIM = 128
B_QK = B_HEADS * B_HEAD_DIM
CONV_WIDTH = 4

REL_BUCKETS = 32
REL_MAX_DIST = 128

X_HEADS = 4
X_HEAD_DIM = 128

P_HEADS = 8
N_KEYS = 128
N_EXPERTS = N_KEYS * N_KEYS
P_TOPK = 16
P_QDIM = 256

COL_WIDTHS = (A_Q_LORA, A_KV_LORA, IDX_DIM, IDX_HEADS, B_QK, B_QK, B_QK, B_QK, B_HEADS, B_HEADS)
MIX_WIDTH = A_HEADS * A_HEAD_DIM + B_HEADS * B_HEAD_DIM


def t5_bucket(rel):
    half = REL_BUCKETS // 2
    max_exact = half // 2
    n = jnp.abs(rel)
    nf = jnp.maximum(n, max_exact).astype(jnp.float32)
    large = max_exact + (jnp.log(nf / max_exact) / math.log(REL_MAX_DIST / max_exact)
                         * (half - max_exact)).astype(jnp.int32)
    large = jnp.minimum(large, half - 1)
    return jnp.where(rel > 0, half, 0) + jnp.where(n < max_exact, n, large)


INT_MIN = -2147483648
NEG_BIG = -1e30
KEY_TILE = 512
KEY_PAD = KEY_TILE - Q_BLOCK
VMEM_LIMIT = 56 * 1024 * 1024


def _rms(x, g):
    return x * lax.rsqrt(jnp.mean(x * x, axis=-1, keepdims=True) + EPS) * g


def _nt_dot(a, b):
    return lax.dot_general(a, b, (((1,), (1,)), ((), ())), preferred_element_type=jnp.float32)


def _dsa_prep_kernel(cq_ref, ckv_ref, kw_ref, gcq_ref, gckv_ref, gk_ref,
                       wiqt_ref, wuqt_ref, wuk_ref,
                       at_ref, qlatt_ref, ckvn_ref, kidxn_ref):
    f32, bf16 = jnp.float32, jnp.bfloat16
    Q = Q_BLOCK
    kw = kw_ref[0]
    cqn = _rms(cq_ref[0], gcq_ref[...]).astype(bf16)
    qit = _nt_dot(wiqt_ref[...], cqn)
    qt = _nt_dot(wuqt_ref[...], cqn)
    for h in range(A_HEADS):
        cols = slice(h * Q, (h + 1) * Q)
        at_ref[0, 0, :, cols] = qit[h * IDX_DIM:(h + 1) * IDX_DIM, :].astype(bf16)
        qh = qt[h * A_HEAD_DIM:(h + 1) * A_HEAD_DIM, :].astype(bf16)
        ql = jnp.dot(wuk_ref[h], qh, preferred_element_type=f32) * ATTN_SCALE
        qlatt_ref[0, 0, :, cols] = ql.astype(bf16)
    ckvn_ref[0] = _rms(ckv_ref[0], gckv_ref[...]).astype(bf16)
    kidxn_ref[0] = _rms(kw[:, :IDX_DIM], gk_ref[...]).astype(bf16)


def _dsa_main_kernel(at_ref, qlatt_ref, wt_ref, kidx_ref, ckv_ref, bucket_ref, far_ref, rb_ref,
                     wuvt_ref, o_ref, bias0_ref, sc_ref, big_ref, acc_ref, m_ref, l_ref, *, topk):
    f32, i32, bf16 = jnp.float32, jnp.int32, jnp.bfloat16
    Q, T = Q_BLOCK, KEY_TILE
    i = pl.program_id(1)

    @pl.when(jnp.logical_and(pl.program_id(0) == 0, i == 0))
    def _():
        bucket = bucket_ref[...]
        for h in range(A_HEADS):
            far_bias = rb_ref[far_ref[0], h]
            bias = jnp.zeros((T, Q), f32)
            for b in range(REL_BUCKETS):
                bias = jnp.where(bucket == b, rb_ref[b, h] - far_bias, bias)
            bias0_ref[h] = bias
    e = (i + 1) * Q
    nt = (i + T // Q) // (T // Q)
    kf = jnp.float32(topk)
    wsc = wt_ref[0] * IDX_SCALE
    sub = lax.broadcasted_iota(i32, (T, Q), 0)
    qlane = lax.broadcasted_iota(i32, (T, Q), 1)
    limit = ((i * Q + qlane) // CHUNK + 1) * CHUNK
    at = at_ref[0, 0]

    def tile_start(j):
        return pl.multiple_of(e + KEY_PAD - (j + 1) * T, 128)

    def key_pos(j):
        return tile_start(j) - KEY_PAD + sub

    def score_tile(j, carry):
        kid = kidx_ref[0, pl.ds(tile_start(j), T), :]
        big_ref[...] = jnp.dot(kid, at, preferred_element_type=f32)
        s = jnp.zeros((T, Q), f32)
        for h in range(IDX_HEADS):
            s = s + wsc[h:h + 1, :] * jnp.maximum(big_ref[:, h * Q:(h + 1) * Q], 0.0)
        bits = lax.bitcast_convert_type(s, i32)
        key = jnp.where(bits < 0, bits ^ jnp.int32(0x7FFFFFFF), bits)
        key = jnp.where(s == 0.0, 0, key)
        kpos = key_pos(j)
        key = jnp.where(kpos >= 0, key, INT_MIN)
        sc_ref[j] = jnp.where(kpos < limit, key, INT_MIN)
        return carry

    lax.fori_loop(0, nt, score_tile, 0)

    def count_where(pred):
        def body(j, cnt):
            hit = jnp.where(pred(sc_ref[j], key_pos(j)), 1.0, 0.0)
            return cnt + jnp.sum(hit.reshape(T // 64, 64, Q), axis=0)
        cnt = lax.fori_loop(0, nt, body, jnp.zeros((64, Q), f32))
        return jnp.sum(cnt, axis=0, keepdims=True)

    def bit_body(b, carry):
        u, cacc = carry
        cand = u | lax.shift_left(jnp.int32(1), 31 - b)
        tvec = cand ^ jnp.int32(INT_MIN)
        tot = count_where(lambda k, kpos: k >= tvec)
        ok = tot >= kf
        return jnp.where(ok, cand, u), jnp.where(ok, tot, cacc)

    u, cacc = lax.fori_loop(0, 32, bit_body, (jnp.zeros((1, Q), i32), jnp.zeros((1, Q), f32)))
    thr = jnp.maximum(u ^ jnp.int32(INT_MIN), INT_MIN + 1)
    overflow = jnp.where(u != 0, cacc, 0.0) > kf
    n_over = jnp.max(jnp.where(overflow, 1.0, 0.0), axis=1, keepdims=True)[0, 0]

    @pl.when(n_over > 0.0)
    def _():
        need = kf - count_where(lambda k, kpos: k > thr)

        def cut_body(b, cut):
            cand = cut | lax.shift_left(jnp.int32(1), 14 - b)
            cnt = count_where(lambda k, kpos: jnp.where(k == thr, kpos, cand) < cand)
            return jnp.where(cnt <= need, cand, cut)

        cut = lax.fori_loop(0, 15, cut_body, jnp.zeros((1, Q), i32))

        def drop_tile(j, carry):
            k = sc_ref[j]
            drop = jnp.where(k == thr, key_pos(j), -1) >= cut
            sc_ref[j] = jnp.where(drop, INT_MIN, k)
            return carry

        lax.fori_loop(0, nt, drop_tile, 0)

    m_ref[...] = jnp.full(m_ref.shape, NEG_BIG, f32)
    l_ref[...] = jnp.zeros(l_ref.shape, f32)
    acc_ref[...] = jnp.zeros(acc_ref.shape, f32)
    qlatt = qlatt_ref[0, 0]

    def attn_tile(j, with_bias):
        kv = ckv_ref[0, pl.ds(tile_start(j), T), :]
        kvt = kv.T
        mask_add = jnp.where(sc_ref[j] >= thr, 0.0, NEG_BIG)
        pair_cols = [slice(g * 2 * Q, (g + 1) * 2 * Q) for g in range(A_HEADS // 2)]
        logits = lambda g: jnp.dot(kv, qlatt[:, pair_cols[g]], preferred_element_type=f32)
        x_next = logits(0)
        for g in range(A_HEADS // 2):
            x_pair = x_next
            if g + 1 < A_HEADS // 2:
                x_next = logits(g + 1)
            ps, alphas = [], []
            for u in range(2):
                h = 2 * g + u
                cols = slice(h * Q, (h + 1) * Q)
                x = x_pair[:, u * Q:(u + 1) * Q] + mask_add
                if with_bias:
                    x = x + bias0_ref[h]
                m_prev = m_ref[:, cols]
                m_new = jnp.maximum(m_prev, jnp.max(x, axis=0, keepdims=True))
                p = jnp.exp(x - m_new)
                alpha = jnp.exp(m_prev - m_new)
                l_ref[:, cols] = alpha * l_ref[:, cols] + jnp.sum(p, axis=0, keepdims=True)
                m_ref[:, cols] = m_new
                ps.append(p.astype(bf16))
                alphas.append(alpha)
            pv = jnp.dot(kvt, jnp.concatenate(ps, axis=1), preferred_element_type=f32)
            acc_ref[:, pair_cols[g]] = (
                jnp.concatenate(alphas, axis=1) * acc_ref[:, pair_cols[g]] + pv)

    attn_tile(0, True)

    def attn_body(j, carry):
        attn_tile(j, False)
        return carry

    lax.fori_loop(1, nt, attn_body, 0)

    inv_l = 1.0 / l_ref[...]
    outs = []
    for h in range(A_HEADS):
        cols = slice(h * Q, (h + 1) * Q)
        o_lat_t = (acc_ref[:, cols] * inv_l[:, cols]).astype(bf16)
        outs.append(jnp.dot(wuvt_ref[h], o_lat_t, preferred_element_type=f32))
    o_ref[0] = jnp.concatenate(outs, axis=0).T


def dsa_pallas(cq, ckv, kw, widx_t, g_cq, g_ckv, g_kidx, w_uq, w_iq, w_uk, w_uv, rel_bias):
    bsz, seq, _ = cq.shape
    f32, bf16 = jnp.float32, jnp.bfloat16
    Q, T, H = Q_BLOCK, KEY_TILE, A_HEADS
    nblk = seq // Q
    topk = min(IDX_TOPK_MAX, seq // 4)
    wiqt = w_iq.reshape(A_Q_LORA, IDX_HEADS * IDX_DIM).T.astype(bf16)
    wuqt = w_uq.reshape(A_Q_LORA, H * A_HEAD_DIM).T.astype(bf16)
    wuk = jnp.transpose(w_uk, (1, 0, 2)).astype(bf16)
    wuvt = jnp.transpose(w_uv, (1, 2, 0)).astype(bf16)
    tok = lambda w: pl.BlockSpec((1, Q, w), lambda b, t: (b, t, 0))
    full = lambda shape: pl.BlockSpec(shape, lambda b, t: (0,) * len(shape))
    blk = lambda r: pl.BlockSpec((1, 1, r, H * Q), lambda b, t: (b, t, 0, 0))
    a_t, qlat_t, ckvn, kidxn = pl.pallas_call(
        _dsa_prep_kernel,
        grid=(bsz, nblk),
        in_specs=[tok(A_Q_LORA), tok(A_KV_LORA), tok(128),
                  full((1, A_Q_LORA)), full((1, A_KV_LORA)), full((1, IDX_DIM)),
                  full(wiqt.shape), full(wuqt.shape), full(wuk.shape)],
        out_specs=[blk(IDX_DIM), blk(A_KV_LORA), tok(A_KV_LORA), tok(IDX_DIM)],
        out_shape=[jax.ShapeDtypeStruct((bsz, nblk, IDX_DIM, H * Q), bf16),
                   jax.ShapeDtypeStruct((bsz, nblk, A_KV_LORA, H * Q), bf16),
                   jax.ShapeDtypeStruct((bsz, seq, A_KV_LORA), bf16),
                   jax.ShapeDtypeStruct((bsz, seq, IDX_DIM), bf16)],
        name="dsa_prep",
    )(cq, ckv, kw, g_cq.reshape(1, -1), g_ckv.reshape(1, -1), g_kidx.reshape(1, -1),
      wiqt, wuqt, wuk)
    ckvp = jnp.pad(ckvn, ((0, 0), (KEY_PAD, 0), (0, 0)))
    kidxp = jnp.pad(kidxn, ((0, 0), (KEY_PAD, 0), (0, 0)))
    rel = (jnp.arange(T, dtype=jnp.int32)[:, None] - jnp.arange(Q, dtype=jnp.int32)[None, :]
           - KEY_PAD)
    bucket0 = t5_bucket(rel)
    bucket_far = t5_bucket(jnp.full((1,), -KEY_PAD - 1, jnp.int32))
    smem = pl.BlockSpec(memory_space=pltpu.SMEM)
    nt_max = (nblk - 1 + T // Q) // (T // Q)
    skey = seq + KEY_PAD
    return pl.pallas_call(
        partial(_dsa_main_kernel, topk=topk),
        grid=(bsz, nblk),
        in_specs=[blk(IDX_DIM), blk(A_KV_LORA),
                  pl.BlockSpec((1, IDX_HEADS, Q), lambda b, t: (b, 1, t)),
                  pl.BlockSpec((1, skey, IDX_DIM), lambda b, t: (b, 0, 0)),
                  pl.BlockSpec((1, skey, A_KV_LORA), lambda b, t: (b, 0, 0)),
                  full((T, Q)), smem, smem, full(wuvt.shape)],
        out_specs=tok(H * A_HEAD_DIM),
        out_shape=jax.ShapeDtypeStruct((bsz, seq, H * A_HEAD_DIM), f32),
        scratch_shapes=[pltpu.VMEM((H, T, Q), f32),
                        pltpu.VMEM((nt_max, T, Q), jnp.int32),
                        pltpu.VMEM((T, H * Q), f32),
                        pltpu.VMEM((A_KV_LORA, H * Q), f32),
                        pltpu.VMEM((1, H * Q), f32),
                        pltpu.VMEM((1, H * Q), f32)],
        compiler_params=pltpu.CompilerParams(
            dimension_semantics=("arbitrary", "arbitrary"), vmem_limit_bytes=VMEM_LIMIT),
        name="dsa_main",
    )(a_t, qlat_t, widx_t, kidxp, ckvp, bucket0, bucket_far, rel_bias.astype(f32), wuvt)


PEER_SCORE_TOKENS = 256
PEER_GATHER_TOKENS = 64
PEER_SLOTS = P_HEADS * P_TOPK
WORDS_PER_ROW = 4
PEER_TILES = 16
PEER_GROUP = 2


def _top16(s, order=None, payload=None):
    if order is None:
        order = lax.broadcasted_iota(jnp.int32, s.shape, 0).astype(jnp.float32)
    vals, picks = [], []
    for _ in range(P_TOPK):
        m = jnp.max(s, axis=0, keepdims=True)
        pos = jnp.min(jnp.where(s == m, order, float(N_EXPERTS)), axis=0, keepdims=True)
        hit = order == pos
        vals.append(m)
        if payload is None:
            picks.append(pos)
        else:
            picks.append(jnp.max(jnp.where(hit, payload, -1.0), axis=0, keepdims=True))
        s = jnp.where(hit, -jnp.inf, s)
    return vals, picks


def _staircase_candidates(v1, i1, v2, i2):
    v1m, i1m = jnp.concatenate(v1, axis=0), jnp.concatenate(i1, axis=0)
    v2m, i2m = jnp.concatenate(v2, axis=0), jnp.concatenate(i2, axis=0)
    t = v1m.shape[1]
    sub8 = lax.broadcasted_iota(jnp.int32, (8, t), 0).astype(jnp.float32)
    cand, cidx, rank = [], [], []
    for a in range(8):
        cand.append(v1[a] + v2m[:8])
        cidx.append(i1[a] * float(N_KEYS) + i2m[:8])
        rank.append(sub8 + float(a * P_TOPK))
    cand.append(v1[0] + v2m[8:])
    cidx.append(i1[0] * float(N_KEYS) + i2m[8:])
    rank.append(sub8 + 8.0)
    cand.append(v1m[8:] + v2[0])
    cidx.append(i1m[8:] * float(N_KEYS) + i2[0])
    rank.append((sub8 + 8.0) * float(P_TOPK))
    cat = lambda xs: jnp.concatenate(xs, axis=0)
    return cat(cand), cat(rank), cat(cidx)


def _peer_score_kernel(x_ref, g_ref, wpqt_ref, sk_ref, hn_ref, eidx_ref, gate_ref):
    f32, bf16 = jnp.float32, jnp.bfloat16
    h = _rms(x_ref[...], g_ref[...])
    hb = h.astype(bf16)
    for s in range(h.shape[1] // 128):
        hn_ref[pl.ds(s, h.shape[0], stride=8), :] = h[:, s * 128:(s + 1) * 128]
    qrt = _nt_dot(wpqt_ref[...], hb)
    half = P_QDIM // 2
    e_rows, g_rows = [], []
    for hd in range(P_HEADS):
        tops = []
        for p in range(2):
            qhp = qrt[(hd * 2 + p) * half:(hd * 2 + p + 1) * half, :].astype(bf16)
            s = jnp.dot(sk_ref[hd * 2 + p], qhp, preferred_element_type=f32)
            tops.append(_top16(s))
        (v1, i1), (v2, i2) = tops
        best, be = _top16(*_staircase_candidates(v1, i1, v2, i2))
        ex = [jnp.exp(b - best[0]) for b in best]
        den = ex[0]
        for k in range(1, P_TOPK):
            den = den + ex[k]
        inv = 1.0 / den
        e_rows += be
        g_rows += [x * inv for x in ex]
    eidx_ref[...] = (jnp.concatenate(e_rows, axis=0).T * float(WORDS_PER_ROW)).astype(jnp.int32)
    gate_ref[...] = jnp.concatenate(g_rows, axis=0).T


def _diag_mask():
    r = lax.broadcasted_iota(jnp.int32, (8, PEER_SLOTS * 8), 0)
    c = lax.broadcasted_iota(jnp.int32, (8, PEER_SLOTS * 8), 1)
    return (c & 7) == r


def _gather_group(idx_ref, tab_ref, g_refs, t0):
    views = [idx_ref.at[t0 + u] for u in range(len(g_refs))]
    for r in range(PEER_SLOTS):
        for view, g_ref in zip(views, g_refs):
            row0 = pl.multiple_of(view[r], WORDS_PER_ROW)
            g_ref[r * WORDS_PER_ROW:(r + 1) * WORDS_PER_ROW, :] = tab_ref[
                pl.ds(row0, WORDS_PER_ROW), :]


def _gather_pipeline(idx_ref, tab_ref, tiles, n_tokens, consume):
    grp = PEER_GROUP
    groups = len(tiles) // grp
    for g_ref in tiles[-grp:]:
        g_ref[...] = jnp.zeros(g_ref.shape, g_ref.dtype)

    def drain(group_tiles, group):
        for u in range(grp):
            consume(group_tiles[u], group * (grp // 2) + u // 2, u % 2)

    def trip(i, carry):
        for k in range(groups):
            group = groups * i + k
            _gather_group(idx_ref, tab_ref, tiles[k * grp:(k + 1) * grp], group * grp)
            drain(tiles[(k - 1) * grp:k * grp] if k else tiles[-grp:], jnp.maximum(group - 1, 0))
        return carry

    lax.fori_loop(0, n_tokens // len(tiles), trip, 0)
    drain(tiles[-grp:], n_tokens // grp - 1)


def _peer_act_kernel(idx_ref, hn_ref, tab_ref, a_ref, m_ref, *tiles):
    f32, bf16 = jnp.float32, jnp.bfloat16
    tb = a_ref.shape[0]

    def dots(g_ref, tp, u):
        hp = hn_ref[pl.ds(pl.multiple_of(tp * 16, 16), 16), :].astype(bf16)
        m = _nt_dot(hp, pltpu.bitcast(g_ref[...], bf16))
        m_ref[pl.ds(pl.multiple_of((tp * 2 + u) * 8, 8), 8), :] = m[u * 8:(u + 1) * 8, :]

    _gather_pipeline(idx_ref, tab_ref, tiles, tb, dots)
    m3 = m_ref[...].reshape(tb, 8, PEER_SLOTS * 8)
    z = jnp.sum(jnp.where(_diag_mask()[None], m3, 0.0), axis=1)
    rr = lax.broadcasted_iota(jnp.int32, (PEER_SLOTS * 8, PEER_SLOTS), 0)
    cc = lax.broadcasted_iota(jnp.int32, (PEER_SLOTS * 8, PEER_SLOTS), 1)
    pool = jnp.where((rr >> 3) == cc, 1.0, 0.0).astype(bf16)
    z_hi = z.astype(bf16)
    z_lo = (z - z_hi.astype(f32)).astype(bf16)
    a_ref[...] = (jnp.dot(z_hi, pool, preferred_element_type=f32)
                  + jnp.dot(z_lo, pool, preferred_element_type=f32))


def _peer_out_kernel(idx_ref, a_ref, gate_ref, tab_ref, o_ref, w_ref, *tiles):
    f32, bf16 = jnp.float32, jnp.bfloat16
    tb = a_ref.shape[0]
    a = a_ref[...]
    act = 0.5 * a * (1.0 + lax.erf(a * (2.0 ** -0.5)))
    wgt = (gate_ref[...] * act).astype(bf16)
    rr = lax.broadcasted_iota(jnp.int32, (PEER_SLOTS, PEER_SLOTS * 8), 0)
    cc = lax.broadcasted_iota(jnp.int32, (PEER_SLOTS, PEER_SLOTS * 8), 1)
    expand = jnp.where((cc >> 3) == rr, 1.0, 0.0).astype(bf16)
    w_ref[...] = jnp.dot(wgt, expand, preferred_element_type=f32)
    diag = _diag_mask()

    def combine(g_ref, tp, u):
        t = tp * 2 + u
        wrow = jnp.broadcast_to(w_ref[pl.ds(t, 1), :], (8, PEER_SLOTS * 8))
        wsel = jnp.where(diag, wrow, 0.0).astype(bf16)
        o_ref[pl.ds(pl.multiple_of(t * 8, 8), 8), :] = jnp.dot(
            wsel, pltpu.bitcast(g_ref[...], bf16), preferred_element_type=f32)

    _gather_pipeline(idx_ref, tab_ref, tiles, tb, combine)


PACK_ROWS = 512


def _pack_table_kernel(x_ref, o_ref):
    rows = x_ref.shape[0]
    bf16_bits = lambda v: lax.bitcast_convert_type(
        v.astype(jnp.bfloat16).astype(jnp.float32), jnp.int32)
    for c in range(WORDS_PER_ROW):
        lo = bf16_bits(x_ref[:, (2 * c) * 128:(2 * c + 1) * 128])
        hi = bf16_bits(x_ref[:, (2 * c + 1) * 128:(2 * c + 2) * 128])
        word = (hi & jnp.int32(-65536)) | lax.shift_right_logical(lo, 16)
        o_ref[pl.ds(c, rows, stride=WORDS_PER_ROW), :] = word


def _pack_table(tab):
    n_e, d = tab.shape
    return pl.pallas_call(
        _pack_table_kernel,
        grid=(n_e // PACK_ROWS,),
        in_specs=[pl.BlockSpec((PACK_ROWS, d), lambda i: (i, 0))],
        out_specs=pl.BlockSpec((PACK_ROWS * WORDS_PER_ROW, 128), lambda i: (i, 0)),
        out_shape=jax.ShapeDtypeStruct((n_e * WORDS_PER_ROW, 128), jnp.int32),
        name="pack_table",
    )(tab)


def peer_pallas(x, g_ffn, w_pq, sub_keys, u_emb, v_emb):
    n_tok, d = x.shape
    f32, bf16 = jnp.float32, jnp.bfloat16
    ts, tg = PEER_SCORE_TOKENS, PEER_GATHER_TOKENS
    wpqt = w_pq.reshape(d, P_HEADS * P_QDIM).T.astype(bf16)
    sk = sub_keys.reshape(P_HEADS * 2, N_KEYS, P_QDIM // 2).astype(bf16)
    hn, eidx, gate = pl.pallas_call(
        _peer_score_kernel,
        grid=(n_tok // ts,),
        in_specs=[pl.BlockSpec((ts, d), lambda i: (i, 0)),
                  pl.BlockSpec((1, d), lambda i: (0, 0)),
                  pl.BlockSpec(wpqt.shape, lambda i: (0, 0)),
                  pl.BlockSpec(sk.shape, lambda i: (0, 0, 0))],
        out_specs=[pl.BlockSpec((ts * 8, 128), lambda i: (i, 0)),
                   pl.BlockSpec((ts, PEER_SLOTS), lambda i: (i, 0)),
                   pl.BlockSpec((ts, PEER_SLOTS), lambda i: (i, 0))],
        out_shape=[jax.ShapeDtypeStruct((n_tok * 8, 128), f32),
                   jax.ShapeDtypeStruct((n_tok, PEER_SLOTS), jnp.int32),
                   jax.ShapeDtypeStruct((n_tok, PEER_SLOTS), f32)],
        compiler_params=pltpu.CompilerParams(vmem_limit_bytes=VMEM_LIMIT),
        name="peer_score",
    )(x, g_ffn.reshape(1, d), wpqt, sk)
    utab, vtab = _pack_table(u_emb), _pack_table(v_emb)
    idx_spec = pl.BlockSpec((tg, PEER_SLOTS), lambda i: (i, 0), memory_space=pltpu.SMEM)
    tab_spec = pl.BlockSpec(memory_space=pltpu.VMEM)
    slot_spec = pl.BlockSpec((tg, PEER_SLOTS), lambda i: (i, 0))
    row_spec = pl.BlockSpec((tg * 8, 128), lambda i: (i, 0))
    gbuf = pltpu.VMEM((PEER_SLOTS * WORDS_PER_ROW, 128), jnp.int32)
    cparams = pltpu.CompilerParams(vmem_limit_bytes=VMEM_LIMIT)
    act = pl.pallas_call(
        _peer_act_kernel,
        grid=(n_tok // tg,),
        in_specs=[idx_spec, row_spec, tab_spec],
        out_specs=slot_spec,
        out_shape=jax.ShapeDtypeStruct((n_tok, PEER_SLOTS), f32),
        scratch_shapes=[pltpu.VMEM((tg * 8, PEER_SLOTS * 8), f32)] + [gbuf] * PEER_TILES,
        compiler_params=cparams,
        name="peer_act",
    )(eidx, hn, utab)
    out = pl.pallas_call(
        _peer_out_kernel,
        grid=(n_tok // tg,),
        in_specs=[idx_spec, slot_spec, slot_spec, tab_spec],
        out_specs=row_spec,
        out_shape=jax.ShapeDtypeStruct((n_tok * 8, 128), f32),
        scratch_shapes=[pltpu.VMEM((tg, PEER_SLOTS * 8), f32)] + [gbuf] * PEER_TILES,
        compiler_params=cparams,
        name="peer_out",
    )(eidx, act, gate, vtab)
    return out


def _token_rows(y_ref):
    tm = y_ref.shape[0] // 8
    return jnp.concatenate([y_ref[pl.ds(s, tm, stride=8), :] for s in range(8)], axis=1)


def _final_kernel(x_ref, y_ref, g_ref, o_ref):
    o_ref[...] = _rms(x_ref[...] + _token_rows(y_ref), g_ref[...])


def _residual_kernel(x_ref, y_ref, o_ref):
    o_ref[...] = x_ref[...] + _token_rows(y_ref)


def residual_pallas(x, y8, g=None):
    n, d = x.shape
    tm = 512
    row = pl.BlockSpec((tm, d), lambda i: (i, 0))
    row8 = pl.BlockSpec((tm * 8, 128), lambda i: (i, 0))
    if g is None:
        kern, extra, extra_specs, name = _residual_kernel, (), [], "residual"
    else:
        kern, extra, name = _final_kernel, (g.reshape(1, d),), "final_rmsnorm"
        extra_specs = [pl.BlockSpec((1, d), lambda i: (0, 0))]
    return pl.pallas_call(
        kern,
        grid=(n // tm,),
        in_specs=[row, row8] + extra_specs,
        out_specs=row,
        out_shape=jax.ShapeDtypeStruct((n, d), x.dtype),
        name=name,
    )(x, y8, *extra)


IN_PROJ_TOKENS = 512
GDN_COLS = 3 * B_QK
GATE_ROWS = 8


def _in_proj_kernel(x_ref, g_ref, wa_ref, wkw_ref, wqkv_ref, wz_ref, wab_ref, wabt_ref,
                    cq_ref, ckv_ref, kw_ref, qkv_ref, z_ref, ab_ref, abt_ref):
    f32 = jnp.float32
    hb = _rms(x_ref[0], g_ref[...]).astype(jnp.bfloat16)
    a = jnp.dot(hb, wa_ref[...], preferred_element_type=f32)
    cq_ref[0] = a[:, :A_Q_LORA]
    ckv_ref[0] = a[:, A_Q_LORA:]
    kw_ref[0] = jnp.dot(hb, wkw_ref[...], preferred_element_type=f32)
    qkv_ref[0] = jnp.dot(hb, wqkv_ref[...], preferred_element_type=f32)
    z_ref[0] = jnp.dot(hb, wz_ref[...], preferred_element_type=f32)
    ab_ref[0] = jnp.dot(hb, wab_ref[...], preferred_element_type=f32)
    abt_ref[0] = _nt_dot(wabt_ref[...], hb)


def in_proj_pallas(x, g_mix, w_in):
    bsz, seq, d = x.shape
    f32, bf16 = jnp.float32, jnp.bfloat16
    tm = min(IN_PROJ_TOKENS, seq)
    o = np.cumsum((0,) + COL_WIDTHS)
    wb = w_in.astype(bf16)
    pad_cols = lambda w: jnp.pad(w, ((0, 0), (0, 128 - w.shape[1])))
    wa = wb[:, o[0]:o[2]]
    wkw = pad_cols(wb[:, o[2]:o[4]])
    wqkv = wb[:, o[4]:o[7]]
    wz = wb[:, o[7]:o[8]]
    wab = pad_cols(wb[:, o[8]:o[10]])
    wabt = jnp.concatenate([wb[:, o[8]:o[10]], wb[:, o[3]:o[4]]], axis=1).T
    full = lambda w: pl.BlockSpec(w.shape, lambda b, t: (0, 0))
    tok = lambda w: pl.BlockSpec((1, tm, w), lambda b, t: (b, t, 0))
    shp = lambda w: jax.ShapeDtypeStruct((bsz, seq, w), f32)
    return pl.pallas_call(
        _in_proj_kernel,
        grid=(bsz, seq // tm),
        in_specs=[tok(d), pl.BlockSpec((1, d), lambda b, t: (0, 0)),
                  full(wa), full(wkw), full(wqkv), full(wz), full(wab), full(wabt)],
        out_specs=[tok(A_Q_LORA), tok(A_KV_LORA), tok(128), tok(GDN_COLS), tok(B_QK), tok(128),
                   pl.BlockSpec((1, GATE_ROWS + IDX_HEADS, tm), lambda b, t: (b, 0, t))],
        out_shape=[shp(A_Q_LORA), shp(A_KV_LORA), shp(128), shp(GDN_COLS), shp(B_QK), shp(128),
                   jax.ShapeDtypeStruct((bsz, GATE_ROWS + IDX_HEADS, seq), f32)],
        compiler_params=pltpu.CompilerParams(vmem_limit_bytes=VMEM_LIMIT),
        name="in_proj",
    )(x, g_mix.reshape(1, d), wa, wkw, wqkv, wz, wab, wabt)


def _softplus(x):
    return jnp.maximum(x, 0.0) + jnp.log1p(jnp.exp(-jnp.abs(x)))


def _sigmoid(x):
    return 1.0 / (1.0 + jnp.exp(-x))


def _gdn_gates(pre, a_log, dt_bias, is_decay):
    g = -jnp.exp(a_log) * _softplus(pre + dt_bias)
    return jnp.where(is_decay, g, _sigmoid(pre))


def _gdn_prep_kernel(qkv_ref, halo_ref, cw_ref, ab_ref, abt_ref, alc_ref, dtc_ref, alr_ref, dtr_ref,
                     q_ref, k_ref, v_ref, gc_ref, gr_ref):
    tm = qkv_ref.shape[1]
    x = qkv_ref[0]
    halo = jnp.where(pl.program_id(1) > 0, halo_ref[0], 0.0)
    full = jnp.concatenate([halo, x], axis=0)
    y = x * cw_ref[CONV_WIDTH - 1:CONV_WIDTH, :]
    for back in range(1, CONV_WIDTH):
        shifted = pltpu.roll(full, back, axis=0)[8:, :]
        y = y + shifted * cw_ref[CONV_WIDTH - 1 - back:CONV_WIDTH - back, :]
    y = y * _sigmoid(y)
    for h in range(B_HEADS):
        cols = slice(h * B_HEAD_DIM, (h + 1) * B_HEAD_DIM)
        qh = y[:, h * B_HEAD_DIM:(h + 1) * B_HEAD_DIM]
        kh = y[:, B_QK + h * B_HEAD_DIM:B_QK + (h + 1) * B_HEAD_DIM]
        q_ref[0, :, cols] = qh * lax.rsqrt(
            jnp.sum(qh * qh, axis=-1, keepdims=True) + EPS) * (B_HEAD_DIM ** -0.5)
        k_ref[0, :, cols] = kh * lax.rsqrt(jnp.sum(kh * kh, axis=-1, keepdims=True) + EPS)
    v_ref[0] = y[:, 2 * B_QK:]
    lane = lax.broadcasted_iota(jnp.int32, (tm, 128), 1)
    gates_c = _gdn_gates(ab_ref[0], alc_ref[...], dtc_ref[...], lane < B_HEADS)
    row = lax.broadcasted_iota(jnp.int32, (GATE_ROWS, tm), 0)
    gates_r = _gdn_gates(abt_ref[0], alr_ref[...], dtr_ref[...], row < B_HEADS)
    ti = lax.broadcasted_iota(jnp.int32, (tm, tm), 0)
    tj = lax.broadcasted_iota(jnp.int32, (tm, tm), 1)
    same_chunk = (ti // CHUNK) == (tj // CHUNK)
    hi = lax.Precision.HIGHEST
    lower = jnp.where(same_chunk & (tj <= ti), 1.0, 0.0)
    upper = jnp.where(same_chunk & (ti <= tj), 1.0, 0.0)
    cum_c = jnp.dot(lower, gates_c, preferred_element_type=jnp.float32, precision=hi)
    cum_r = jnp.dot(gates_r, upper, preferred_element_type=jnp.float32, precision=hi)
    gc_ref[0] = jnp.where(lane < B_HEADS, cum_c, gates_c)
    gr_ref[0] = jnp.where(row < B_HEADS, cum_r, gates_r)


def _gdn_main_kernel(q_ref, k_ref, v_ref, z_ref, gc_ref, gr_ref, gon_ref, o_ref, s_ref):
    f32, bf16 = jnp.float32, jnp.bfloat16
    C, Dh = CHUNK, B_HEAD_DIM

    @pl.when(pl.program_id(1) == 0)
    def _():
        s_ref[...] = jnp.zeros(s_ref.shape, f32)

    ii = lax.broadcasted_iota(jnp.int32, (C, C), 0)
    jj = lax.broadcasted_iota(jnp.int32, (C, C), 1)
    causal = ii >= jj
    strict = ii > jj
    eye = jnp.where(ii == jj, 1.0, 0.0)
    mm = lambda a, b: jnp.dot(a, b, preferred_element_type=f32)
    n_chunks = q_ref.shape[1] // C
    units = []
    for c in range(n_chunks):
        rows = slice(c * C, (c + 1) * C)
        gates_c = gc_ref[0, rows, :]
        gates_r = gr_ref[0, :, c * C:(c + 1) * C]
        for h in range(B_HEADS):
            cols = slice(h * Dh, (h + 1) * Dh)
            gcum = jnp.broadcast_to(gates_c[:, h:h + 1], (C, Dh))
            beta = jnp.broadcast_to(gates_c[:, B_HEADS + h:B_HEADS + h + 1], (C, Dh))
            gcum_r = jnp.broadcast_to(gates_r[h:h + 1, :], (C, C))
            decay = jnp.where(causal, jnp.exp(jnp.minimum(gcum[:, :C] - gcum_r, 0.0)), 0.0)
            q, k, v = q_ref[0, rows, cols], k_ref[0, rows, cols], v_ref[0, rows, cols]
            qb, kb = q.astype(bf16), k.astype(bf16)
            kk = _nt_dot(kb, kb)
            qk = _nt_dot(qb, kb)
            egc = jnp.exp(gcum)
            g_last = gcum[C - 1:C, :]
            units.append(dict(
                c=c, h=h, rows=rows, cols=cols,
                neg_m=jnp.where(strict, -(beta[:, :C] * kk * decay), 0.0),
                rhs=jnp.concatenate([v * beta, k * (beta * egc)], axis=1),
                q_dec=(q * egc).astype(bf16), intra=(qk * decay).astype(bf16),
                k_tail=(k * jnp.exp(g_last - gcum)).astype(bf16), chunk_decay=jnp.exp(g_last)))
    def halves(a):
        a_h = a.astype(bf16)
        return a_h, (a - a_h.astype(f32)).astype(bf16)

    def mm3(a, b):
        (a_h, a_l), (b_h, b_l) = a, b
        return mm(a_h, b_h) + (mm(a_h, b_l) + mm(a_l, b_h))

    t_inv = [eye + un["neg_m"] for un in units]
    p_halves = [halves(un["neg_m"]) for un in units]
    for _ in range(5):
        p_halves = [halves(mm3(ph, ph)) for ph in p_halves]
        t_inv = [t + mm3(ph, halves(t)) for ph, t in zip(p_halves, t_inv)]
    sols = [mm3(halves(t), halves(un["rhs"])) for t, un in zip(t_inv, units)]
    states = [s_ref[h] for h in range(B_HEADS)]
    for c in range(n_chunks):
        group = [(un, sol) for un, sol in zip(units, sols) if un["c"] == c]
        sbs = [states[un["h"]].astype(bf16) for un, _ in group]
        vbs = [(sol[:, :Dh] - mm(sol[:, Dh:].astype(bf16), sb)).astype(bf16)
               for (un, sol), sb in zip(group, sbs)]
        outs = [mm(un["q_dec"], sb) + mm(un["intra"], vb)
                for (un, _), sb, vb in zip(group, sbs, vbs)]
        for (un, _), vb in zip(group, vbs):
            states[un["h"]] = states[un["h"]] * un["chunk_decay"] + lax.dot_general(
                un["k_tail"], vb, (((0,), (0,)), ((), ())), preferred_element_type=f32)
        for (un, _), o in zip(group, outs):
            o = o * lax.rsqrt(jnp.mean(o * o, axis=-1, keepdims=True) + EPS) * gon_ref[...]
            zz = z_ref[0, un["rows"], un["cols"]]
            o_ref[0, un["rows"], un["cols"]] = o * (zz * _sigmoid(zz))
    for h in range(B_HEADS):
        s_ref[h] = states[h]


def gdn_pallas(qkv, z, ab, abt, conv_w, a_log, dt_bias, g_onorm):
    bsz, seq, _ = qkv.shape
    f32 = jnp.float32
    tm = min(256, seq)
    zero4 = jnp.zeros((B_HEADS,), f32)
    lane_row = lambda v: jnp.pad(jnp.concatenate([v.astype(f32), zero4]), (0, 120)).reshape(1, 128)
    sub_col = lambda v: jnp.concatenate([v.astype(f32), zero4]).reshape(GATE_ROWS, 1)
    tok = lambda w, t=tm: pl.BlockSpec((1, t, w), lambda b, i: (b, i, 0))
    const = lambda shape: pl.BlockSpec(shape, lambda b, i: (0,) * len(shape))
    shp = lambda w: jax.ShapeDtypeStruct((bsz, seq, w), f32)
    q, k, v, gc, gr = pl.pallas_call(
        _gdn_prep_kernel,
        grid=(bsz, seq // tm),
        in_specs=[tok(GDN_COLS),
                  pl.BlockSpec((1, 8, GDN_COLS), lambda b, i: (b, jnp.maximum(i * (tm // 8) - 1, 0), 0)),
                  const((CONV_WIDTH, GDN_COLS)), tok(128),
                  pl.BlockSpec((1, GATE_ROWS, tm), lambda b, i: (b, 0, i)),
                  const((1, 128)), const((1, 128)), const((GATE_ROWS, 1)), const((GATE_ROWS, 1))],
        out_specs=[tok(B_QK), tok(B_QK), tok(B_QK), tok(128),
                   pl.BlockSpec((1, GATE_ROWS, tm), lambda b, i: (b, 0, i))],
        out_shape=[shp(B_QK), shp(B_QK), shp(B_QK), shp(128),
                   jax.ShapeDtypeStruct((bsz, GATE_ROWS, seq), f32)],
        compiler_params=pltpu.CompilerParams(vmem_limit_bytes=VMEM_LIMIT),
        name="gdn_prep",
    )(qkv, qkv, conv_w.astype(f32), ab, abt, lane_row(a_log), lane_row(dt_bias),
      sub_col(a_log), sub_col(dt_bias))
    ts = 2 * CHUNK
    return pl.pallas_call(
        _gdn_main_kernel,
        grid=(bsz, seq // ts),
        in_specs=[tok(B_QK, ts), tok(B_QK, ts), tok(B_QK, ts), tok(B_QK, ts), tok(128, ts),
                  pl.BlockSpec((1, GATE_ROWS, ts), lambda b, i: (b, 0, i)),
                  const((1, B_HEAD_DIM))],
        out_specs=tok(B_QK, ts),
        out_shape=shp(B_QK),
        scratch_shapes=[pltpu.VMEM((B_HEADS, B_HEAD_DIM, B_HEAD_DIM), f32)],
        compiler_params=pltpu.CompilerParams(dimension_semantics=("arbitrary", "arbitrary")),
        name="gdn_main",
    )(q, k, v, z, gc, gr, g_onorm.astype(f32).reshape(1, B_HEAD_DIM))


def _mem_kv_kernel(mem_ref, g_ref, wk_ref, wv_ref, k_ref, v_ref):
    f32, bf16 = jnp.float32, jnp.bfloat16
    mn = _rms(mem_ref[0], g_ref[...]).astype(bf16)
    k = jnp.dot(mn, wk_ref[...], preferred_element_type=f32)
    v = jnp.dot(mn, wv_ref[...], preferred_element_type=f32)
    for h in range(X_HEADS):
        cols = slice(h * X_HEAD_DIM, (h + 1) * X_HEAD_DIM)
        k_ref[0, h] = k[:, cols].astype(bf16)
        v_ref[0, h] = v[:, cols].astype(bf16)


def _mid_kernel(x_ref, oa_ref, ob_ref, wo_ref, gx_ref, wq_ref, k_ref, v_ref, wox_ref, o_ref):
    f32, bf16 = jnp.float32, jnp.bfloat16
    na = oa_ref.shape[2]
    x1 = (x_ref[0]
          + jnp.dot(oa_ref[0].astype(bf16), wo_ref[:na, :], preferred_element_type=f32)
          + jnp.dot(ob_ref[0].astype(bf16), wo_ref[na:, :], preferred_element_type=f32))
    hq = _rms(x1, gx_ref[...]).astype(bf16)
    q = jnp.dot(hq, wq_ref[...], preferred_element_type=f32)
    heads = []
    for h in range(X_HEADS):
        qh = q[:, h * X_HEAD_DIM:(h + 1) * X_HEAD_DIM].astype(bf16)
        lg = _nt_dot(qh, k_ref[0, h]) * (X_HEAD_DIM ** -0.5)
        p = jnp.exp(lg - jnp.max(lg, axis=-1, keepdims=True))
        p = (p / jnp.sum(p, axis=-1, keepdims=True)).astype(bf16)
        heads.append(jnp.dot(p, v_ref[0, h], preferred_element_type=f32).astype(bf16))
    o = jnp.concatenate(heads, axis=1)
    o_ref[0] = x1 + jnp.dot(o, wox_ref[...], preferred_element_type=f32)


def mid_pallas(x, o_a, o_b, w_out, g_cross, mem, g_mem, wq, wk, wv, wo):
    bsz, seq, d = x.shape
    f32, bf16 = jnp.float32, jnp.bfloat16
    hx = X_HEADS * X_HEAD_DIM
    m_len = mem.shape[1]
    const2 = lambda shape: pl.BlockSpec(shape, lambda b: (0,) * len(shape))
    kv_spec = pl.BlockSpec((1, X_HEADS, m_len, X_HEAD_DIM), lambda b: (b, 0, 0, 0))
    kv_shape = jax.ShapeDtypeStruct((bsz, X_HEADS, m_len, X_HEAD_DIM), bf16)
    k, v = pl.pallas_call(
        _mem_kv_kernel,
        grid=(bsz,),
        in_specs=[pl.BlockSpec((1, m_len, d), lambda b: (b, 0, 0)), const2((1, d)),
                  const2((d, hx)), const2((d, hx))],
        out_specs=[kv_spec, kv_spec],
        out_shape=[kv_shape, kv_shape],
        name="mem_kv",
    )(mem, g_mem.reshape(1, d), wk.reshape(d, hx).astype(bf16), wv.reshape(d, hx).astype(bf16))
    tm = min(256, seq)
    tok = lambda w: pl.BlockSpec((1, tm, w), lambda b, t: (b, t, 0))
    const = lambda shape: pl.BlockSpec(shape, lambda b, t: (0,) * len(shape))
    kv_spec2 = pl.BlockSpec((1, X_HEADS, m_len, X_HEAD_DIM), lambda b, t: (b, 0, 0, 0))
    return pl.pallas_call(
        _mid_kernel,
        grid=(bsz, seq // tm),
        in_specs=[tok(d), tok(o_a.shape[2]), tok(o_b.shape[2]), const((MIX_WIDTH, d)),
                  const((1, d)), const((d, hx)), kv_spec2, kv_spec2, const((hx, d))],
        out_specs=tok(d),
        out_shape=jax.ShapeDtypeStruct((bsz, seq, d), f32),
        compiler_params=pltpu.CompilerParams(vmem_limit_bytes=VMEM_LIMIT),
        name="mid",
    )(x, o_a, o_b, w_out.astype(bf16), g_cross.reshape(1, d), wq.reshape(d, hx).astype(bf16),
      k, v, wo.reshape(hx, d).astype(bf16))


def kernel(x, mem, g_mix, w_in, g_cq, g_ckv, g_kidx, w_uq, w_iq, w_uk, w_uv, rel_bias, conv_w, A_log, dt_bias, g_onorm, w_out, g_cross, g_mem, wq_x, wk_x, wv_x, wo_x, g_ffn, w_pq, sub_keys, u_emb, v_emb, g_final):
    bsz, seq, d = x.shape
    for l in range(DEPTH):
        cq, ckv, kw, qkv, z, ab, abt = in_proj_pallas(x, g_mix[l], w_in[l])
        o_a = dsa_pallas(cq, ckv, kw, abt, g_cq[l], g_ckv[l], g_kidx[l],
                         w_uq[l], w_iq[l], w_uk[l], w_uv[l], rel_bias)
        o_b = gdn_pallas(qkv, z, ab, abt, conv_w[l], A_log[l], dt_bias[l], g_onorm[l])
        x = mid_pallas(x, o_a, o_b, w_out[l], g_cross[l], mem, g_mem[l],
                       wq_x[l], wk_x[l], wv_x[l], wo_x[l])
        xf = x.reshape(bsz * seq, d)
        y8 = peer_pallas(xf, g_ffn[l], w_pq[l], sub_keys[l], u_emb[l], v_emb[l])
        if l + 1 < DEPTH:
            x = residual_pallas(xf, y8).reshape(bsz, seq, d)
    return residual_pallas(xf, y8, g_final).reshape(bsz, seq, d)
```

```python
import math
from functools import partial
import jax
import jax.numpy as jnp
from jax import lax
import numpy as np
from jax.experimental import pallas as pl
from jax.experimental.pallas import tpu as pltpu

DEPTH = 1

CHUNK = 64
Q_BLOCK = 128
EPS = 1e-6

A_HEADS = 8
A_HEAD_DIM = 64
A_Q_LORA = 256
A_KV_LORA = 256
IDX_HEADS = 8
IDX_DIM = 64
IDX_TOPK_MAX = 256
ATTN_SCALE = A_HEAD_DIM ** -0.5
IDX_SCALE = (IDX_HEADS * IDX_DIM) ** -0.5

B_HEADS = 4
B_HEAD_DIM = 128
B_QK = B_HEADS * B_HEAD_DIM
CONV_WIDTH = 4

REL_BUCKETS = 32
REL_MAX_DIST = 128

X_HEADS = 4
X_HEAD_DIM = 128

P_HEADS = 8
N_KEYS = 128
N_EXPERTS = N_KEYS * N_KEYS
P_TOPK = 16
P_QDIM = 256

COL_WIDTHS = (A_Q_LORA, A_KV_LORA, IDX_DIM, IDX_HEADS, B_QK, B_QK, B_QK, B_QK, B_HEADS, B_HEADS)
MIX_WIDTH = A_HEADS * A_HEAD_DIM + B_HEADS * B_HEAD_DIM


def t5_bucket(rel):
    half = REL_BUCKETS // 2
    max_exact = half // 2
    n = jnp.abs(rel)
    nf = jnp.maximum(n, max_exact).astype(jnp.float32)
    large = max_exact + (jnp.log(nf / max_exact) / math.log(REL_MAX_DIST / max_exact)
                         * (half - max_exact)).astype(jnp.int32)
    large = jnp.minimum(large, half - 1)
    return jnp.where(rel > 0, half, 0) + jnp.where(n < max_exact, n, large)


INT_MIN = -2147483648
HALF_BIAS = 32768
NEG_BIG = -1e30
KEY_TILE = 512
KEY_PAD = KEY_TILE - Q_BLOCK
VMEM_LIMIT = 56 * 1024 * 1024


def _rms(x, g):
    return x * lax.rsqrt(jnp.mean(x * x, axis=-1, keepdims=True) + EPS) * g


def _nt_dot(a, b):
    return lax.dot_general(a, b, (((1,), (1,)), ((), ())), preferred_element_type=jnp.float32)


def _dsa_prep_kernel(cq_ref, ckv_ref, kw_ref, gcq_ref, gckv_ref, gk_ref,
                       wiqt_ref, wuqt_ref, wuk_ref,
                       at_ref, qlatt_ref, ckvn_ref, kidxn_ref):
    f32, bf16 = jnp.float32, jnp.bfloat16
    Q = Q_BLOCK
    kw = kw_ref[0]
    cqn = _rms(cq_ref[0], gcq_ref[...]).astype(bf16)
    qit = _nt_dot(wiqt_ref[...], cqn)
    qt = _nt_dot(wuqt_ref[...], cqn)
    for h in range(A_HEADS):
        cols = slice(h * Q, (h + 1) * Q)
        at_ref[0, 0, :, cols] = qit[h * IDX_DIM:(h + 1) * IDX_DIM, :].astype(bf16)
        qh = qt[h * A_HEAD_DIM:(h + 1) * A_HEAD_DIM, :].astype(bf16)
        ql = jnp.dot(wuk_ref[h], qh, preferred_element_type=f32) * ATTN_SCALE
        qlatt_ref[0, 0, :, cols] = ql.astype(bf16)
    ckvn_ref[0] = _rms(ckv_ref[0], gckv_ref[...]).astype(bf16)
    kidxn_ref[0] = _rms(kw[:, :IDX_DIM], gk_ref[...]).astype(bf16)


def _dsa_main_kernel(at_ref, qlatt_ref, wt_ref, kidx_ref, ckv_ref, bucket_ref, far_ref, rb_ref,
                     wuvt_ref, o_ref, bias0_ref, sc_ref, hi_ref, lo_ref, big_ref, acc_ref, m_ref,
                     l_ref, *, topk):
    f32, i32, i16, bf16 = jnp.float32, jnp.int32, jnp.int16, jnp.bfloat16
    Q, T = Q_BLOCK, KEY_TILE
    i = pl.program_id(1)

    @pl.when(jnp.logical_and(pl.program_id(0) == 0, i == 0))
    def _():
        bucket = bucket_ref[...]
        for h in range(A_HEADS):
            far_bias = rb_ref[far_ref[0], h]
            bias = jnp.zeros((T, Q), f32)
            for b in range(REL_BUCKETS):
                bias = jnp.where(bucket == b, rb_ref[b, h] - far_bias, bias)
            bias0_ref[h] = bias
    e = (i + 1) * Q
    nt = (i + T // Q) // (T // Q)
    kf = jnp.float32(topk)
    wsc = wt_ref[0] * IDX_SCALE
    sub = lax.broadcasted_iota(i32, (T, Q), 0)
    qlane = lax.broadcasted_iota(i32, (T, Q), 1)
    limit = ((i * Q + qlane) // CHUNK + 1) * CHUNK
    at = at_ref[0, 0]

    def tile_start(j):
        return pl.multiple_of(e + KEY_PAD - (j + 1) * T, 128)

    def key_pos(j):
        return tile_start(j) - KEY_PAD + sub

    def score_tile(j, carry):
        kid = kidx_ref[0, pl.ds(tile_start(j), T), :]
        big_ref[...] = jnp.dot(kid, at, preferred_element_type=f32)
        s = jnp.zeros((T, Q), f32)
        for h in range(IDX_HEADS):
            s = s + wsc[h:h + 1, :] * jnp.maximum(big_ref[:, h * Q:(h + 1) * Q], 0.0)
        bits = lax.bitcast_convert_type(s, i32)
        key = jnp.where(bits < 0, bits ^ jnp.int32(0x7FFFFFFF), bits)
        key = jnp.where(s == 0.0, 0, key)
        kpos = key_pos(j)
        key = jnp.where(kpos >= 0, key, INT_MIN)
        key = jnp.where(kpos < limit, key, INT_MIN)
        sc_ref[j] = key
        hi_ref[j] = lax.shift_right_arithmetic(key, 16).astype(i16)
        lo_ref[j] = ((key & 0xFFFF) - HALF_BIAS).astype(i16)
        return carry

    lax.fori_loop(0, nt, score_tile, 0)

    def count_where(pred):
        def body(j, cnt):
            hit = jnp.where(pred(sc_ref[j], key_pos(j)), 1.0, 0.0)
            return cnt + jnp.sum(hit.reshape(T // 64, 64, Q), axis=0)
        cnt = lax.fori_loop(0, nt, body, jnp.zeros((64, Q), f32))
        return jnp.sum(cnt, axis=0, keepdims=True)

    def count_half(ref, pred):
        def body(j, cnt):
            hit = jnp.where(pred(ref[j]), jnp.int16(1), jnp.int16(0))
            for part in range(T // 64):
                cnt = cnt + hit[part * 64:(part + 1) * 64, :]
            return cnt
        cnt = lax.fori_loop(0, nt, body, jnp.zeros((64, Q), i16))
        return jnp.sum(cnt.astype(f32), axis=0, keepdims=True)

    def half_search(ref, base):
        def bit_body(b, carry):
            v, cacc = carry
            cand = v | lax.shift_left(jnp.int32(1), 15 - b)
            t16 = (cand - HALF_BIAS).astype(i16)
            tot = base + count_half(ref, lambda x: x >= t16)
            ok = tot >= kf
            return jnp.where(ok, cand, v), jnp.where(ok, tot, cacc)
        return lax.fori_loop(0, 16, bit_body,
                             (jnp.zeros((1, Q), i32), jnp.zeros((1, Q), f32)))

    u_hi, cacc_hi = half_search(hi_ref, 0.0)
    thr_hi = (u_hi - HALF_BIAS).astype(i16)
    above = count_half(hi_ref, lambda x: x > thr_hi)

    def mask_low(j, carry):
        lo_ref[j] = jnp.where(hi_ref[j] == thr_hi, lo_ref[j], jnp.int16(-HALF_BIAS))
        return carry

    lax.fori_loop(0, nt, mask_low, 0)
    u_lo, cacc_lo = half_search(lo_ref, above)
    thr = jnp.maximum(lax.shift_left(u_hi - HALF_BIAS, 16) + u_lo, INT_MIN + 1)
    cacc = jnp.where(u_lo != 0, cacc_lo, jnp.where(u_hi != 0, cacc_hi, 0.0))
    overflow = cacc > kf
    n_over = jnp.max(jnp.where(overflow, 1.0, 0.0), axis=1, keepdims=True)[0, 0]

    @pl.when(n_over > 0.0)
    def _():
        need = kf - count_where(lambda k, kpos: k > thr)

        def cut_body(b, cut):
            cand = cut | lax.shift_left(jnp.int32(1), 14 - b)
            cnt = count_where(lambda k, kpos: jnp.where(k == thr, kpos, cand) < cand)
            return jnp.where(cnt <= need, cand, cut)

        cut = lax.fori_loop(0, 15, cut_body, jnp.zeros((1, Q), i32))

        def drop_tile(j, carry):
            k = sc_ref[j]
            drop = jnp.where(k == thr, key_pos(j), -1) >= cut
            sc_ref[j] = jnp.where(drop, INT_MIN, k)
            return carry

        lax.fori_loop(0, nt, drop_tile, 0)

    m_ref[...] = jnp.full(m_ref.shape, NEG_BIG, f32)
    l_ref[...] = jnp.zeros(l_ref.shape, f32)
    acc_ref[...] = jnp.zeros(acc_ref.shape, f32)
    qlatt = qlatt_ref[0, 0]

    def attn_tile(j, with_bias):
        kv = ckv_ref[0, pl.ds(tile_start(j), T), :]
        kvt = kv.T
        mask_add = jnp.where(sc_ref[j] >= thr, 0.0, NEG_BIG)
        pair_cols = [slice(g * 2 * Q, (g + 1) * 2 * Q) for g in range(A_HEADS // 2)]
        logits = lambda g: jnp.dot(kv, qlatt[:, pair_cols[g]], preferred_element_type=f32)
        x_next = logits(0)
        for g in range(A_HEADS // 2):
            x_pair = x_next
            if g + 1 < A_HEADS // 2:
                x_next = logits(g + 1)
            ps, alphas = [], []
            for u in range(2):
                h = 2 * g + u
                cols = slice(h * Q, (h + 1) * Q)
                x = x_pair[:, u * Q:(u + 1) * Q] + mask_add
                if with_bias:
                    x = x + bias0_ref[h]
                m_prev = m_ref[:, cols]
                m_new = jnp.maximum(m_prev, jnp.max(x, axis=0, keepdims=True))
                p = jnp.exp(x - m_new)
                alpha = jnp.exp(m_prev - m_new)
                l_ref[:, cols] = alpha * l_ref[:, cols] + jnp.sum(p, axis=0, keepdims=True)
                m_ref[:, cols] = m_new
                ps.append(p.astype(bf16))
                alphas.append(alpha)
            pv = jnp.dot(kvt, jnp.concatenate(ps, axis=1), preferred_element_type=f32)
            acc_ref[:, pair_cols[g]] = (
                jnp.concatenate(alphas, axis=1) * acc_ref[:, pair_cols[g]] + pv)

    attn_tile(0, True)

    def attn_body(j, carry):
        attn_tile(j, False)
        return carry

    lax.fori_loop(1, nt, attn_body, 0)

    inv_l = 1.0 / l_ref[...]
    outs = []
    for h in range(A_HEADS):
        cols = slice(h * Q, (h + 1) * Q)
        o_lat_t = (acc_ref[:, cols] * inv_l[:, cols]).astype(bf16)
        outs.append(jnp.dot(wuvt_ref[h], o_lat_t, preferred_element_type=f32))
    o_ref[0] = jnp.concatenate(outs, axis=0).T


def dsa_pallas(cq, ckv, kw, widx_t, g_cq, g_ckv, g_kidx, w_uq, w_iq, w_uk, w_uv, rel_bias):
    bsz, seq, _ = cq.shape
    f32, bf16 = jnp.float32, jnp.bfloat16
    Q, T, H = Q_BLOCK, KEY_TILE, A_HEADS
    nblk = seq // Q
    topk = min(IDX_TOPK_MAX, seq // 4)
    wiqt = w_iq.reshape(A_Q_LORA, IDX_HEADS * IDX_DIM).T.astype(bf16)
    wuqt = w_uq.reshape(A_Q_LORA, H * A_HEAD_DIM).T.astype(bf16)
    wuk = jnp.transpose(w_uk, (1, 0, 2)).astype(bf16)
    wuvt = jnp.transpose(w_uv, (1, 2, 0)).astype(bf16)
    tok = lambda w: pl.BlockSpec((1, Q, w), lambda b, t: (b, t, 0))
    full = lambda shape: pl.BlockSpec(shape, lambda b, t: (0,) * len(shape))
    blk = lambda r: pl.BlockSpec((1, 1, r, H * Q), lambda b, t: (b, t, 0, 0))
    a_t, qlat_t, ckvn, kidxn = pl.pallas_call(
        _dsa_prep_kernel,
        grid=(bsz, nblk),
        in_specs=[tok(A_Q_LORA), tok(A_KV_LORA), tok(128),
                  full((1, A_Q_LORA)), full((1, A_KV_LORA)), full((1, IDX_DIM)),
                  full(wiqt.shape), full(wuqt.shape), full(wuk.shape)],
        out_specs=[blk(IDX_DIM), blk(A_KV_LORA), tok(A_KV_LORA), tok(IDX_DIM)],
        out_shape=[jax.ShapeDtypeStruct((bsz, nblk, IDX_DIM, H * Q), bf16),
                   jax.ShapeDtypeStruct((bsz, nblk, A_KV_LORA, H * Q), bf16),
                   jax.ShapeDtypeStruct((bsz, seq, A_KV_LORA), bf16),
                   jax.ShapeDtypeStruct((bsz, seq, IDX_DIM), bf16)],
        name="dsa_prep",
    )(cq, ckv, kw, g_cq.reshape(1, -1), g_ckv.reshape(1, -1), g_kidx.reshape(1, -1),
      wiqt, wuqt, wuk)
    ckvp = jnp.pad(ckvn, ((0, 0), (KEY_PAD, 0), (0, 0)))
    kidxp = jnp.pad(kidxn, ((0, 0), (KEY_PAD, 0), (0, 0)))
    rel = (jnp.arange(T, dtype=jnp.int32)[:, None] - jnp.arange(Q, dtype=jnp.int32)[None, :]
           - KEY_PAD)
    bucket0 = t5_bucket(rel)
    bucket_far = t5_bucket(jnp.full((1,), -KEY_PAD - 1, jnp.int32))
    smem = pl.BlockSpec(memory_space=pltpu.SMEM)
    nt_max = (nblk - 1 + T // Q) // (T // Q)
    skey = seq + KEY_PAD
    return pl.pallas_call(
        partial(_dsa_main_kernel, topk=topk),
        grid=(bsz, nblk),
        in_specs=[blk(IDX_DIM), blk(A_KV_LORA),
                  pl.BlockSpec((1, IDX_HEADS, Q), lambda b, t: (b, 1, t)),
                  pl.BlockSpec((1, skey, IDX_DIM), lambda b, t: (b, 0, 0)),
                  pl.BlockSpec((1, skey, A_KV_LORA), lambda b, t: (b, 0, 0)),
                  full((T, Q)), smem, smem, full(wuvt.shape)],
        out_specs=tok(H * A_HEAD_DIM),
        out_shape=jax.ShapeDtypeStruct((bsz, seq, H * A_HEAD_DIM), f32),
        scratch_shapes=[pltpu.VMEM((H, T, Q), f32),
                        pltpu.VMEM((nt_max, T, Q), jnp.int32),
                        pltpu.VMEM((nt_max, T, Q), jnp.int16),
                        pltpu.VMEM((nt_max, T, Q), jnp.int16),
                        pltpu.VMEM((T, H * Q), f32),
                        pltpu.VMEM((A_KV_LORA, H * Q), f32),
                        pltpu.VMEM((1, H * Q), f32),
                        pltpu.VMEM((1, H * Q), f32)],
        compiler_params=pltpu.CompilerParams(
            dimension_semantics=("arbitrary", "arbitrary"), vmem_limit_bytes=VMEM_LIMIT),
        name="dsa_main",
    )(a_t, qlat_t, widx_t, kidxp, ckvp, bucket0, bucket_far, rel_bias.astype(f32), wuvt)


PEER_SCORE_TOKENS = 256
PEER_GATHER_TOKENS = 64
PEER_SLOTS = P_HEADS * P_TOPK
WORDS_PER_ROW = 4
PEER_TILES = 16
PEER_GROUP = 2


def _top16(s, order=None, payload=None):
    if order is None:
        order = lax.broadcasted_iota(jnp.int32, s.shape, 0).astype(jnp.float32)
    vals, picks = [], []
    for _ in range(P_TOPK):
        m = jnp.max(s, axis=0, keepdims=True)
        pos = jnp.min(jnp.where(s == m, order, float(N_EXPERTS)), axis=0, keepdims=True)
        hit = order == pos
        vals.append(m)
        if payload is None:
            picks.append(pos)
        else:
            picks.append(jnp.max(jnp.where(hit, payload, -1.0), axis=0, keepdims=True))
        s = jnp.where(hit, -jnp.inf, s)
    return vals, picks


def _staircase_candidates(v1, i1, v2, i2):
    v1m, i1m = jnp.concatenate(v1, axis=0), jnp.concatenate(i1, axis=0)
    v2m, i2m = jnp.concatenate(v2, axis=0), jnp.concatenate(i2, axis=0)
    t = v1m.shape[1]
    sub8 = lax.broadcasted_iota(jnp.int32, (8, t), 0).astype(jnp.float32)
    cand, cidx, rank = [], [], []
    for a in range(8):
        cand.append(v1[a] + v2m[:8])
        cidx.append(i1[a] * float(N_KEYS) + i2m[:8])
        rank.append(sub8 + float(a * P_TOPK))
    cand.append(v1[0] + v2m[8:])
    cidx.append(i1[0] * float(N_KEYS) + i2m[8:])
    rank.append(sub8 + 8.0)
    cand.append(v1m[8:] + v2[0])
    cidx.append(i1m[8:] * float(N_KEYS) + i2[0])
    rank.append((sub8 + 8.0) * float(P_TOPK))
    cat = lambda xs: jnp.concatenate(xs, axis=0)
    return cat(cand), cat(rank), cat(cidx)


def _peer_score_kernel(x_ref, g_ref, wpqt_ref, sk_ref, hn_ref, eidx_ref, gate_ref):
    f32, bf16 = jnp.float32, jnp.bfloat16
    h = _rms(x_ref[...], g_ref[...])
    hb = h.astype(bf16)
    for s in range(h.shape[1] // 128):
        hn_ref[pl.ds(s, h.shape[0], stride=8), :] = h[:, s * 128:(s + 1) * 128]
    qrt = _nt_dot(wpqt_ref[...], hb)
    half = P_QDIM // 2
    e_rows, g_rows = [], []
    for hd in range(P_HEADS):
        tops = []
        for p in range(2):
            qhp = qrt[(hd * 2 + p) * half:(hd * 2 + p + 1) * half, :].astype(bf16)
            s = jnp.dot(sk_ref[hd * 2 + p], qhp, preferred_element_type=f32)
            tops.append(_top16(s))
        (v1, i1), (v2, i2) = tops
        best, be = _top16(*_staircase_candidates(v1, i1, v2, i2))
        ex = [jnp.exp(b - best[0]) for b in best]
        den = ex[0]
        for k in range(1, P_TOPK):
            den = den + ex[k]
        inv = 1.0 / den
        e_rows += be
        g_rows += [x * inv for x in ex]
    eidx_ref[...] = (jnp.concatenate(e_rows, axis=0).T * float(WORDS_PER_ROW)).astype(jnp.int32)
    gate_ref[...] = jnp.concatenate(g_rows, axis=0).T


def _diag_mask():
    r = lax.broadcasted_iota(jnp.int32, (8, PEER_SLOTS * 8), 0)
    c = lax.broadcasted_iota(jnp.int32, (8, PEER_SLOTS * 8), 1)
    return (c & 7) == r


def _gather_group(idx_ref, tab_ref, g_refs, t0):
    views = [idx_ref.at[t0 + u] for u in range(len(g_refs))]
    for r in range(PEER_SLOTS):
        for view, g_ref in zip(views, g_refs):
            row0 = pl.multiple_of(view[r], WORDS_PER_ROW)
            g_ref[r * WORDS_PER_ROW:(r + 1) * WORDS_PER_ROW, :] = tab_ref[
                pl.ds(row0, WORDS_PER_ROW), :]


def _gather_pipeline(idx_ref, tab_ref, tiles, n_tokens, consume):
    grp = PEER_GROUP
    groups = len(tiles) // grp
    for g_ref in tiles[-grp:]:
        g_ref[...] = jnp.zeros(g_ref.shape, g_ref.dtype)

    def drain(group_tiles, group):
        for u in range(grp):
            consume(group_tiles[u], group * (grp // 2) + u // 2, u % 2)

    def trip(i, carry):
        for k in range(groups):
            group = groups * i + k
            _gather_group(idx_ref, tab_ref, tiles[k * grp:(k + 1) * grp], group * grp)
            drain(tiles[(k - 1) * grp:k * grp] if k else tiles[-grp:], jnp.maximum(group - 1, 0))
        return carry

    lax.fori_loop(0, n_tokens // len(tiles), trip, 0)
    drain(tiles[-grp:], n_tokens // grp - 1)


def _peer_act_kernel(idx_ref, hn_ref, tab_ref, a_ref, m_ref, *tiles):
    f32, bf16 = jnp.float32, jnp.bfloat16
    tb = a_ref.shape[0]

    def dots(g_ref, tp, u):
        hp = hn_ref[pl.ds(pl.multiple_of(tp * 16, 16), 16), :].astype(bf16)
        m = _nt_dot(hp, pltpu.bitcast(g_ref[...], bf16))
        m_ref[pl.ds(pl.multiple_of((tp * 2 + u) * 8, 8), 8), :] = m[u * 8:(u + 1) * 8, :]

    _gather_pipeline(idx_ref, tab_ref, tiles, tb, dots)
    m3 = m_ref[...].reshape(tb, 8, PEER_SLOTS * 8)
    z = jnp.sum(jnp.where(_diag_mask()[None], m3, 0.0), axis=1)
    rr = lax.broadcasted_iota(jnp.int32, (PEER_SLOTS * 8, PEER_SLOTS), 0)
    cc = lax.broadcasted_iota(jnp.int32, (PEER_SLOTS * 8, PEER_SLOTS), 1)
    pool = jnp.where((rr >> 3) == cc, 1.0, 0.0).astype(bf16)
    z_hi = z.astype(bf16)
    z_lo = (z - z_hi.astype(f32)).astype(bf16)
    a_ref[...] = (jnp.dot(z_hi, pool, preferred_element_type=f32)
                  + jnp.dot(z_lo, pool, preferred_element_type=f32))


def _peer_out_kernel(idx_ref, a_ref, gate_ref, tab_ref, o_ref, w_ref, *tiles):
    f32, bf16 = jnp.float32, jnp.bfloat16
    tb = a_ref.shape[0]
    a = a_ref[...]
    act = 0.5 * a * (1.0 + lax.erf(a * (2.0 ** -0.5)))
    wgt = (gate_ref[...] * act).astype(bf16)
    rr = lax.broadcasted_iota(jnp.int32, (PEER_SLOTS, PEER_SLOTS * 8), 0)
    cc = lax.broadcasted_iota(jnp.int32, (PEER_SLOTS, PEER_SLOTS * 8), 1)
    expand = jnp.where((cc >> 3) == rr, 1.0, 0.0).astype(bf16)
    w_ref[...] = jnp.dot(wgt, expand, preferred_element_type=f32)
    diag = _diag_mask()

    def combine(g_ref, tp, u):
        t = tp * 2 + u
        wrow = jnp.broadcast_to(w_ref[pl.ds(t, 1), :], (8, PEER_SLOTS * 8))
        wsel = jnp.where(diag, wrow, 0.0).astype(bf16)
        o_ref[pl.ds(pl.multiple_of(t * 8, 8), 8), :] = jnp.dot(
            wsel, pltpu.bitcast(g_ref[...], bf16), preferred_element_type=f32)

    _gather_pipeline(idx_ref, tab_ref, tiles, tb, combine)


PACK_ROWS = 512


def _pack_table_kernel(x_ref, o_ref):
    rows = x_ref.shape[0]
    bf16_bits = lambda v: lax.bitcast_convert_type(
        v.astype(jnp.bfloat16).astype(jnp.float32), jnp.int32)
    for c in range(WORDS_PER_ROW):
        lo = bf16_bits(x_ref[:, (2 * c) * 128:(2 * c + 1) * 128])
        hi = bf16_bits(x_ref[:, (2 * c + 1) * 128:(2 * c + 2) * 128])
        word = (hi & jnp.int32(-65536)) | lax.shift_right_logical(lo, 16)
        o_ref[pl.ds(c, rows, stride=WORDS_PER_ROW), :] = word


def _pack_table(tab):
    n_e, d = tab.shape
    return pl.pallas_call(
        _pack_table_kernel,
        grid=(n_e // PACK_ROWS,),
        in_specs=[pl.BlockSpec((PACK_ROWS, d), lambda i: (i, 0))],
        out_specs=pl.BlockSpec((PACK_ROWS * WORDS_PER_ROW, 128), lambda i: (i, 0)),
        out_shape=jax.ShapeDtypeStruct((n_e * WORDS_PER_ROW, 128), jnp.int32),
        name="pack_table",
    )(tab)


def peer_pallas(x, g_ffn, w_pq, sub_keys, u_emb, v_emb):
    n_tok, d = x.shape
    f32, bf16 = jnp.float32, jnp.bfloat16
    ts, tg = PEER_SCORE_TOKENS, PEER_GATHER_TOKENS
    wpqt = w_pq.reshape(d, P_HEADS * P_QDIM).T.astype(bf16)
    sk = sub_keys.reshape(P_HEADS * 2, N_KEYS, P_QDIM // 2).astype(bf16)
    hn, eidx, gate = pl.pallas_call(
        _peer_score_kernel,
        grid=(n_tok // ts,),
        in_specs=[pl.BlockSpec((ts, d), lambda i: (i, 0)),
                  pl.BlockSpec((1, d), lambda i: (0, 0)),
                  pl.BlockSpec(wpqt.shape, lambda i: (0, 0)),
                  pl.BlockSpec(sk.shape, lambda i: (0, 0, 0))],
        out_specs=[pl.BlockSpec((ts * 8, 128), lambda i: (i, 0)),
                   pl.BlockSpec((ts, PEER_SLOTS), lambda i: (i, 0)),
                   pl.BlockSpec((ts, PEER_SLOTS), lambda i: (i, 0))],
        out_shape=[jax.ShapeDtypeStruct((n_tok * 8, 128), f32),
                   jax.ShapeDtypeStruct((n_tok, PEER_SLOTS), jnp.int32),
                   jax.ShapeDtypeStruct((n_tok, PEER_SLOTS), f32)],
        compiler_params=pltpu.CompilerParams(vmem_limit_bytes=VMEM_LIMIT),
        name="peer_score",
    )(x, g_ffn.reshape(1, d), wpqt, sk)
    utab, vtab = _pack_table(u_emb), _pack_table(v_emb)
    idx_spec = pl.BlockSpec((tg, PEER_SLOTS), lambda i: (i, 0), memory_space=pltpu.SMEM)
    tab_spec = pl.BlockSpec(memory_space=pltpu.VMEM)
    slot_spec = pl.BlockSpec((tg, PEER_SLOTS), lambda i: (i, 0))
    row_spec = pl.BlockSpec((tg * 8, 128), lambda i: (i, 0))
    gbuf = pltpu.VMEM((PEER_SLOTS * WORDS_PER_ROW, 128), jnp.int32)
    cparams = pltpu.CompilerParams(vmem_limit_bytes=VMEM_LIMIT)
    act = pl.pallas_call(
        _peer_act_kernel,
        grid=(n_tok // tg,),
        in_specs=[idx_spec, row_spec, tab_spec],
        out_specs=slot_spec,
        out_shape=jax.ShapeDtypeStruct((n_tok, PEER_SLOTS), f32),
        scratch_shapes=[pltpu.VMEM((tg * 8, PEER_SLOTS * 8), f32)] + [gbuf] * PEER_TILES,
        compiler_params=cparams,
        name="peer_act",
    )(eidx, hn, utab)
    out = pl.pallas_call(
        _peer_out_kernel,
        grid=(n_tok // tg,),
        in_specs=[idx_spec, slot_spec, slot_spec, tab_spec],
        out_specs=row_spec,
        out_shape=jax.ShapeDtypeStruct((n_tok * 8, 128), f32),
        scratch_shapes=[pltpu.VMEM((tg, PEER_SLOTS * 8), f32)] + [gbuf] * PEER_TILES,
        compiler_params=cparams,
        name="peer_out",
    )(eidx, act, gate, vtab)
    return out


def _token_rows(y_ref):
    tm = y_ref.shape[0] // 8
    return jnp.concatenate([y_ref[pl.ds(s, tm, stride=8), :] for s in range(8)], axis=1)


def _final_kernel(x_ref, y_ref, g_ref, o_ref):
    o_ref[...] = _rms(x_ref[...] + _token_rows(y_ref), g_ref[...])


def _residual_kernel(x_ref, y_ref, o_ref):
    o_ref[...] = x_ref[...] + _token_rows(y_ref)


def residual_pallas(x, y8, g=None):
    n, d = x.shape
    tm = 512
    row = pl.BlockSpec((tm, d), lambda i: (i, 0))
    row8 = pl.BlockSpec((tm * 8, 128), lambda i: (i, 0))
    if g is None:
        kern, extra, extra_specs, name = _residual_kernel, (), [], "residual"
    else:
        kern, extra, name = _final_kernel, (g.reshape(1, d),), "final_rmsnorm"
        extra_specs = [pl.BlockSpec((1, d), lambda i: (0, 0))]
    return pl.pallas_call(
        kern,
        grid=(n // tm,),
        in_specs=[row, row8] + extra_specs,
        out_specs=row,
        out_shape=jax.ShapeDtypeStruct((n, d), x.dtype),
        name=name,
    )(x, y8, *extra)


IN_PROJ_TOKENS = 512
GDN_COLS = 3 * B_QK
GATE_ROWS = 8


def _in_proj_kernel(x_ref, g_ref, wa_ref, wkw_ref, wqkv_ref, wz_ref, wab_ref, wabt_ref,
                    cq_ref, ckv_ref, kw_ref, qkv_ref, z_ref, ab_ref, abt_ref):
    f32 = jnp.float32
    hb = _rms(x_ref[0], g_ref[...]).astype(jnp.bfloat16)
    a = jnp.dot(hb, wa_ref[...], preferred_element_type=f32)
    cq_ref[0] = a[:, :A_Q_LORA]
    ckv_ref[0] = a[:, A_Q_LORA:]
    kw_ref[0] = jnp.dot(hb, wkw_ref[...], preferred_element_type=f32)
    qkv_ref[0] = jnp.dot(hb, wqkv_ref[...], preferred_element_type=f32)
    z_ref[0] = jnp.dot(hb, wz_ref[...], preferred_element_type=f32)
    ab_ref[0] = jnp.dot(hb, wab_ref[...], preferred_element_type=f32)
    abt_ref[0] = _nt_dot(wabt_ref[...], hb)


def in_proj_pallas(x, g_mix, w_in):
    bsz, seq, d = x.shape
    f32, bf16 = jnp.float32, jnp.bfloat16
    tm = min(IN_PROJ_TOKENS, seq)
    o = np.cumsum((0,) + COL_WIDTHS)
    wb = w_in.astype(bf16)
    pad_cols = lambda w: jnp.pad(w, ((0, 0), (0, 128 - w.shape[1])))
    wa = wb[:, o[0]:o[2]]
    wkw = pad_cols(wb[:, o[2]:o[4]])
    wqkv = wb[:, o[4]:o[7]]
    wz = wb[:, o[7]:o[8]]
    wab = pad_cols(wb[:, o[8]:o[10]])
    wabt = jnp.concatenate([wb[:, o[8]:o[10]], wb[:, o[3]:o[4]]], axis=1).T
    full = lambda w: pl.BlockSpec(w.shape, lambda b, t: (0, 0))
    tok = lambda w: pl.BlockSpec((1, tm, w), lambda b, t: (b, t, 0))
    shp = lambda w: jax.ShapeDtypeStruct((bsz, seq, w), f32)
    return pl.pallas_call(
        _in_proj_kernel,
        grid=(bsz, seq // tm),
        in_specs=[tok(d), pl.BlockSpec((1, d), lambda b, t: (0, 0)),
                  full(wa), full(wkw), full(wqkv), full(wz), full(wab), full(wabt)],
        out_specs=[tok(A_Q_LORA), tok(A_KV_LORA), tok(128), tok(GDN_COLS), tok(B_QK), tok(128),
                   pl.BlockSpec((1, GATE_ROWS + IDX_HEADS, tm), lambda b, t: (b, 0, t))],
        out_shape=[shp(A_Q_LORA), shp(A_KV_LORA), shp(128), shp(GDN_COLS), shp(B_QK), shp(128),
                   jax.ShapeDtypeStruct((bsz, GATE_ROWS + IDX_HEADS, seq), f32)],
        compiler_params=pltpu.CompilerParams(vmem_limit_bytes=VMEM_LIMIT),
        name="in_proj",
    )(x, g_mix.reshape(1, d), wa, wkw, wqkv, wz, wab, wabt)


def _softplus(x):
    return jnp.maximum(x, 0.0) + jnp.log1p(jnp.exp(-jnp.abs(x)))


def _sigmoid(x):
    return 1.0 / (1.0 + jnp.exp(-x))


def _gdn_gates(pre, a_log, dt_bias, is_decay):
    g = -jnp.exp(a_log) * _softplus(pre + dt_bias)
    return jnp.where(is_decay, g, _sigmoid(pre))


def _gdn_prep_kernel(qkv_ref, halo_ref, cw_ref, ab_ref, abt_ref, alc_ref, dtc_ref, alr_ref, dtr_ref,
                     q_ref, k_ref, v_ref, gc_ref, gr_ref):
    tm = qkv_ref.shape[1]
    x = qkv_ref[0]
    halo = jnp.where(pl.program_id(1) > 0, halo_ref[0], 0.0)
    full = jnp.concatenate([halo, x], axis=0)
    y = x * cw_ref[CONV_WIDTH - 1:CONV_WIDTH, :]
    for back in range(1, CONV_WIDTH):
        shifted = pltpu.roll(full, back, axis=0)[8:, :]
        y = y + shifted * cw_ref[CONV_WIDTH - 1 - back:CONV_WIDTH - back, :]
    y = y * _sigmoid(y)
    for h in range(B_HEADS):
        cols = slice(h * B_HEAD_DIM, (h + 1) * B_HEAD_DIM)
        qh = y[:, h * B_HEAD_DIM:(h + 1) * B_HEAD_DIM]
        kh = y[:, B_QK + h * B_HEAD_DIM:B_QK + (h + 1) * B_HEAD_DIM]
        q_ref[0, :, cols] = qh * lax.rsqrt(
            jnp.sum(qh * qh, axis=-1, keepdims=True) + EPS) * (B_HEAD_DIM ** -0.5)
        k_ref[0, :, cols] = kh * lax.rsqrt(jnp.sum(kh * kh, axis=-1, keepdims=True) + EPS)
    v_ref[0] = y[:, 2 * B_QK:]
    lane = lax.broadcasted_iota(jnp.int32, (tm, 128), 1)
    gates_c = _gdn_gates(ab_ref[0], alc_ref[...], dtc_ref[...], lane < B_HEADS)
    row = lax.broadcasted_iota(jnp.int32, (GATE_ROWS, tm), 0)
    gates_r = _gdn_gates(abt_ref[0], alr_ref[...], dtr_ref[...], row < B_HEADS)
    ti = lax.broadcasted_iota(jnp.int32, (tm, tm), 0)
    tj = lax.broadcasted_iota(jnp.int32, (tm, tm), 1)
    same_chunk = (ti // CHUNK) == (tj // CHUNK)
    hi = lax.Precision.HIGHEST
    lower = jnp.where(same_chunk & (tj <= ti), 1.0, 0.0)
    upper = jnp.where(same_chunk & (ti <= tj), 1.0, 0.0)
    cum_c = jnp.dot(lower, gates_c, preferred_element_type=jnp.float32, precision=hi)
    cum_r = jnp.dot(gates_r, upper, preferred_element_type=jnp.float32, precision=hi)
    gc_ref[0] = jnp.where(lane < B_HEADS, cum_c, gates_c)
    gr_ref[0] = jnp.where(row < B_HEADS, cum_r, gates_r)


def _gdn_main_kernel(q_ref, k_ref, v_ref, z_ref, gc_ref, gr_ref, gon_ref, o_ref, s_ref):
    f32, bf16 = jnp.float32, jnp.bfloat16
    C, Dh = CHUNK, B_HEAD_DIM

    @pl.when(pl.program_id(1) == 0)
    def _():
        s_ref[...] = jnp.zeros(s_ref.shape, f32)

    ii = lax.broadcasted_iota(jnp.int32, (C, C), 0)
    jj = lax.broadcasted_iota(jnp.int32, (C, C), 1)
    causal = ii >= jj
    strict = ii > jj
    eye = jnp.where(ii == jj, 1.0, 0.0)
    mm = lambda a, b: jnp.dot(a, b, preferred_element_type=f32)
    n_chunks = q_ref.shape[1] // C
    units = []
    for c in range(n_chunks):
        rows = slice(c * C, (c + 1) * C)
        gates_c = gc_ref[0, rows, :]
        gates_r = gr_ref[0, :, c * C:(c + 1) * C]
        for h in range(B_HEADS):
            cols = slice(h * Dh, (h + 1) * Dh)
            gcum = jnp.broadcast_to(gates_c[:, h:h + 1], (C, Dh))
            beta = jnp.broadcast_to(gates_c[:, B_HEADS + h:B_HEADS + h + 1], (C, Dh))
            gcum_r = jnp.broadcast_to(gates_r[h:h + 1, :], (C, C))
            decay = jnp.where(causal, jnp.exp(jnp.minimum(gcum[:, :C] - gcum_r, 0.0)), 0.0)
            q, k, v = q_ref[0, rows, cols], k_ref[0, rows, cols], v_ref[0, rows, cols]
            qb, kb = q.astype(bf16), k.astype(bf16)
            kk = _nt_dot(kb, kb)
            qk = _nt_dot(qb, kb)
            egc = jnp.exp(gcum)
            g_last = gcum[C - 1:C, :]
            units.append(dict(
                c=c, h=h, rows=rows, cols=cols,
                neg_m=jnp.where(strict, -(beta[:, :C] * kk * decay), 0.0),
                rhs=jnp.concatenate([v * beta, k * (beta * egc)], axis=1),
                q_dec=(q * egc).astype(bf16), intra=(qk * decay).astype(bf16),
                k_tail=(k * jnp.exp(g_last - gcum)).astype(bf16), chunk_decay=jnp.exp(g_last)))
    def halves(a):
        a_h = a.astype(bf16)
        return a_h, (a - a_h.astype(f32)).astype(bf16)

    def mm3(a, b):
        (a_h, a_l), (b_h, b_l) = a, b
        return mm(a_h, b_h) + (mm(a_h, b_l) + mm(a_l, b_h))

    t_inv = [eye + un["neg_m"] for un in units]
    p_halves = [halves(un["neg_m"]) for un in units]
    for _ in range(5):
        p_halves = [halves(mm3(ph, ph)) for ph in p_halves]
        t_inv = [t + mm3(ph, halves(t)) for ph, t in zip(p_halves, t_inv)]
    sols = [mm3(halves(t), halves(un["rhs"])) for t, un in zip(t_inv, units)]
    states = [s_ref[h] for h in range(B_HEADS)]
    for c in range(n_chunks):
        group = [(un, sol) for un, sol in zip(units, sols) if un["c"] == c]
        sbs = [states[un["h"]].astype(bf16) for un, _ in group]
        vbs = [(sol[:, :Dh] - mm(sol[:, Dh:].astype(bf16), sb)).astype(bf16)
               for (un, sol), sb in zip(group, sbs)]
        outs = [mm(un["q_dec"], sb) + mm(un["intra"], vb)
                for (un, _), sb, vb in zip(group, sbs, vbs)]
        for (un, _), vb in zip(group, vbs):
            states[un["h"]] = states[un["h"]] * un["chunk_decay"] + lax.dot_general(
                un["k_tail"], vb, (((0,), (0,)), ((), ())), preferred_element_type=f32)
        for (un, _), o in zip(group, outs):
            o = o * lax.rsqrt(jnp.mean(o * o, axis=-1, keepdims=True) + EPS) * gon_ref[...]
            zz = z_ref[0, un["rows"], un["cols"]]
            o_ref[0, un["rows"], un["cols"]] = o * (zz * _sigmoid(zz))
    for h in range(B_HEADS):
        s_ref[h] = states[h]


def gdn_pallas(qkv, z, ab, abt, conv_w, a_log, dt_bias, g_onorm):
    bsz, seq, _ = qkv.shape
    f32 = jnp.float32
    tm = min(256, seq)
    zero4 = jnp.zeros((B_HEADS,), f32)
    lane_row = lambda v: jnp.pad(jnp.concatenate([v.astype(f32), zero4]), (0, 120)).reshape(1, 128)
    sub_col = lambda v: jnp.concatenate([v.astype(f32), zero4]).reshape(GATE_ROWS, 1)
    tok = lambda w, t=tm: pl.BlockSpec((1, t, w), lambda b, i: (b, i, 0))
    const = lambda shape: pl.BlockSpec(shape, lambda b, i: (0,) * len(shape))
    shp = lambda w: jax.ShapeDtypeStruct((bsz, seq, w), f32)
    q, k, v, gc, gr = pl.pallas_call(
        _gdn_prep_kernel,
        grid=(bsz, seq // tm),
        in_specs=[tok(GDN_COLS),
                  pl.BlockSpec((1, 8, GDN_COLS), lambda b, i: (b, jnp.maximum(i * (tm // 8) - 1, 0), 0)),
                  const((CONV_WIDTH, GDN_COLS)), tok(128),
                  pl.BlockSpec((1, GATE_ROWS, tm), lambda b, i: (b, 0, i)),
                  const((1, 128)), const((1, 128)), const((GATE_ROWS, 1)), const((GATE_ROWS, 1))],
        out_specs=[tok(B_QK), tok(B_QK), tok(B_QK), tok(128),
                   pl.BlockSpec((1, GATE_ROWS, tm), lambda b, i: (b, 0, i))],
        out_shape=[shp(B_QK), shp(B_QK), shp(B_QK), shp(128),
                   jax.ShapeDtypeStruct((bsz, GATE_ROWS, seq), f32)],
        compiler_params=pltpu.CompilerParams(vmem_limit_bytes=VMEM_LIMIT),
        name="gdn_prep",
    )(qkv, qkv, conv_w.astype(f32), ab, abt, lane_row(a_log), lane_row(dt_bias),
      sub_col(a_log), sub_col(dt_bias))
    ts = 2 * CHUNK
    return pl.pallas_call(
        _gdn_main_kernel,
        grid=(bsz, seq // ts),
        in_specs=[tok(B_QK, ts), tok(B_QK, ts), tok(B_QK, ts), tok(B_QK, ts), tok(128, ts),
                  pl.BlockSpec((1, GATE_ROWS, ts), lambda b, i: (b, 0, i)),
                  const((1, B_HEAD_DIM))],
        out_specs=tok(B_QK, ts),
        out_shape=shp(B_QK),
        scratch_shapes=[pltpu.VMEM((B_HEADS, B_HEAD_DIM, B_HEAD_DIM), f32)],
        compiler_params=pltpu.CompilerParams(dimension_semantics=("arbitrary", "arbitrary")),
        name="gdn_main",
    )(q, k, v, z, gc, gr, g_onorm.astype(f32).reshape(1, B_HEAD_DIM))


def _mem_kv_kernel(mem_ref, g_ref, wk_ref, wv_ref, k_ref, v_ref):
    f32, bf16 = jnp.float32, jnp.bfloat16
    mn = _rms(mem_ref[0], g_ref[...]).astype(bf16)
    k = jnp.dot(mn, wk_ref[...], preferred_element_type=f32)
    v = jnp.dot(mn, wv_ref[...], preferred_element_type=f32)
    for h in range(X_HEADS):
        cols = slice(h * X_HEAD_DIM, (h + 1) * X_HEAD_DIM)
        k_ref[0, h] = k[:, cols].astype(bf16)
        v_ref[0, h] = v[:, cols].astype(bf16)


def _mid_kernel(x_ref, oa_ref, ob_ref, wo_ref, gx_ref, wq_ref, k_ref, v_ref, wox_ref, o_ref):
    f32, bf16 = jnp.float32, jnp.bfloat16
    na = oa_ref.shape[2]
    x1 = (x_ref[0]
          + jnp.dot(oa_ref[0].astype(bf16), wo_ref[:na, :], preferred_element_type=f32)
          + jnp.dot(ob_ref[0].astype(bf16), wo_ref[na:, :], preferred_element_type=f32))
    hq = _rms(x1, gx_ref[...]).astype(bf16)
    q = jnp.dot(hq, wq_ref[...], preferred_element_type=f32)
    heads = []
    for h in range(X_HEADS):
        qh = q[:, h * X_HEAD_DIM:(h + 1) * X_HEAD_DIM].astype(bf16)
        lg = _nt_dot(qh, k_ref[0, h]) * (X_HEAD_DIM ** -0.5)
        p = jnp.exp(lg - jnp.max(lg, axis=-1, keepdims=True))
        p = (p / jnp.sum(p, axis=-1, keepdims=True)).astype(bf16)
        heads.append(jnp.dot(p, v_ref[0, h], preferred_element_type=f32).astype(bf16))
    o = jnp.concatenate(heads, axis=1)
    o_ref[0] = x1 + jnp.dot(o, wox_ref[...], preferred_element_type=f32)


def mid_pallas(x, o_a, o_b, w_out, g_cross, mem, g_mem, wq, wk, wv, wo):
    bsz, seq, d = x.shape
    f32, bf16 = jnp.float32, jnp.bfloat16
    hx = X_HEADS * X_HEAD_DIM
    m_len = mem.shape[1]
    const2 = lambda shape: pl.BlockSpec(shape, lambda b: (0,) * len(shape))
    kv_spec = pl.BlockSpec((1, X_HEADS, m_len, X_HEAD_DIM), lambda b: (b, 0, 0, 0))
    kv_shape = jax.ShapeDtypeStruct((bsz, X_HEADS, m_len, X_HEAD_DIM), bf16)
    k, v = pl.pallas_call(
        _mem_kv_kernel,
        grid=(bsz,),
        in_specs=[pl.BlockSpec((1, m_len, d), lambda b: (b, 0, 0)), const2((1, d)),
                  const2((d, hx)), const2((d, hx))],
        out_specs=[kv_spec, kv_spec],
        out_shape=[kv_shape, kv_shape],
        name="mem_kv",
    )(mem, g_mem.reshape(1, d), wk.reshape(d, hx).astype(bf16), wv.reshape(d, hx).astype(bf16))
    tm = min(256, seq)
    tok = lambda w: pl.BlockSpec((1, tm, w), lambda b, t: (b, t, 0))
    const = lambda shape: pl.BlockSpec(shape, lambda b, t: (0,) * len(shape))
    kv_spec2 = pl.BlockSpec((1, X_HEADS, m_len, X_HEAD_DIM), lambda b, t: (b, 0, 0, 0))
    return pl.pallas_call(
        _mid_kernel,
        grid=(bsz, seq // tm),
        in_specs=[tok(d), tok(o_a.shape[2]), tok(o_b.shape[2]), const((MIX_WIDTH, d)),
                  const((1, d)), const((d, hx)), kv_spec2, kv_spec2, const((hx, d))],
        out_specs=tok(d),
        out_shape=jax.ShapeDtypeStruct((bsz, seq, d), f32),
        compiler_params=pltpu.CompilerParams(vmem_limit_bytes=VMEM_LIMIT),
        name="mid",
    )(x, o_a, o_b, w_out.astype(bf16), g_cross.reshape(1, d), wq.reshape(d, hx).astype(bf16),
      k, v, wo.reshape(hx, d).astype(bf16))


def kernel(x, mem, g_mix, w_in, g_cq, g_ckv, g_kidx, w_uq, w_iq, w_uk, w_uv, rel_bias, conv_w, A_log, dt_bias, g_onorm, w_out, g_cross, g_mem, wq_x, wk_x, wv_x, wo_x, g_ffn, w_pq, sub_keys, u_emb, v_emb, g_final):
    bsz, seq, d = x.shape
    for l in range(DEPTH):
        cq, ckv, kw, qkv, z, ab, abt = in_proj_pallas(x, g_mix[l], w_in[l])
        o_a = dsa_pallas(cq, ckv, kw, abt, g_cq[l], g_ckv[l], g_kidx[l],
                         w_uq[l], w_iq[l], w_uk[l], w_uv[l], rel_bias)
        o_b = gdn_pallas(qkv, z, ab, abt, conv_w[l], A_log[l], dt_bias[l], g_onorm[l])
        x = mid_pallas(x, o_a, o_b, w_out[l], g_cross[l], mem, g_mem[l],
                       wq_x[l], wk_x[l], wv_x[l], wo_x[l])
        xf = x.reshape(bsz * seq, d)
        y8 = peer_pallas(xf, g_ffn[l], w_pq[l], sub_keys[l], u_emb[l], v_emb[l])
        if l + 1 < DEPTH:
            x = residual_pallas(xf, y8).reshape(bsz, seq, d)
    return residual_pallas(xf, y8, g_final).reshape(bsz, seq, d)
```

```python
import math
from functools import partial
import jax
import jax.numpy as jnp
from jax import lax
import numpy as np
from jax.experimental import pallas as pl
from jax.experimental.pallas import tpu as pltpu

DEPTH = 1

CHUNK = 64
Q_BLOCK = 128
EPS = 1e-6

A_HEADS = 8
A_HEAD_DIM = 64
A_Q_LORA = 256
A_KV_LORA = 256
IDX_HEADS = 8
IDX_DIM = 64
IDX_TOPK_MAX = 256
ATTN_SCALE = A_HEAD_DIM ** -0.5
IDX_SCALE = (IDX_HEADS * IDX_DIM) ** -0.5

B_HEADS = 4
B_HEAD_DIM = 128
B_QK = B_HEADS * B_HEAD_DIM
CONV_WIDTH = 4

REL_BUCKETS = 32
REL_MAX_DIST = 128

X_HEADS = 4
X_HEAD_DIM = 128

P_HEADS = 8
N_KEYS = 128
N_EXPERTS = N_KEYS * N_KEYS
P_TOPK = 16
P_QDIM = 256

COL_WIDTHS = (A_Q_LORA, A_KV_LORA, IDX_DIM, IDX_HEADS, B_QK, B_QK, B_QK, B_QK, B_HEADS, B_HEADS)
MIX_WIDTH = A_HEADS * A_HEAD_DIM + B_HEADS * B_HEAD_DIM


def t5_bucket(rel):
    half = REL_BUCKETS // 2
    max_exact = half // 2
    n = jnp.abs(rel)
    nf = jnp.maximum(n, max_exact).astype(jnp.float32)
    large = max_exact + (jnp.log(nf / max_exact) / math.log(REL_MAX_DIST / max_exact)
                         * (half - max_exact)).astype(jnp.int32)
    large = jnp.minimum(large, half - 1)
    return jnp.where(rel > 0, half, 0) + jnp.where(n < max_exact, n, large)


INT_MIN = -2147483648
NEG_BIG = -1e30
KEY_TILE = 512
KEY_PAD = KEY_TILE - Q_BLOCK
VMEM_LIMIT = 56 * 1024 * 1024


def _rms(x, g):
    return x * lax.rsqrt(jnp.mean(x * x, axis=-1, keepdims=True) + EPS) * g


def _nt_dot(a, b):
    return lax.dot_general(a, b, (((1,), (1,)), ((), ())), preferred_element_type=jnp.float32)


def _dsa_prep_kernel(cq_ref, ckv_ref, kw_ref, gcq_ref, gckv_ref, gk_ref,
                       wiqt_ref, wuqt_ref, wuk_ref,
                       at_ref, qlatt_ref, ckvn_ref, kidxn_ref):
    f32, bf16 = jnp.float32, jnp.bfloat16
    Q = Q_BLOCK
    kw = kw_ref[0]
    cqn = _rms(cq_ref[0], gcq_ref[...]).astype(bf16)
    qit = _nt_dot(wiqt_ref[...], cqn)
    qt = _nt_dot(wuqt_ref[...], cqn)
    for h in range(A_HEADS):
        cols = slice(h * Q, (h + 1) * Q)
        at_ref[0, 0, :, cols] = qit[h * IDX_DIM:(h + 1) * IDX_DIM, :].astype(bf16)
        qh = qt[h * A_HEAD_DIM:(h + 1) * A_HEAD_DIM, :].astype(bf16)
        ql = jnp.dot(wuk_ref[h], qh, preferred_element_type=f32) * ATTN_SCALE
        qlatt_ref[0, 0, :, cols] = ql.astype(bf16)
    ckvn_ref[0] = _rms(ckv_ref[0], gckv_ref[...]).astype(bf16)
    kidxn_ref[0] = _rms(kw[:, :IDX_DIM], gk_ref[...]).astype(bf16)


def _dsa_main_kernel(at_ref, qlatt_ref, wt_ref, kidx_ref, ckv_ref, bucket_ref, far_ref, rb_ref,
                     wuvt_ref, o_ref, bias0_ref, sc_ref, big_ref, acc_ref, m_ref, l_ref, *, topk):
    f32, i32, bf16 = jnp.float32, jnp.int32, jnp.bfloat16
    Q, T = Q_BLOCK, KEY_TILE
    i = pl.program_id(1)

    @pl.when(jnp.logical_and(pl.program_id(0) == 0, i == 0))
    def _():
        bucket = bucket_ref[...]
        for h in range(A_HEADS):
            far_bias = rb_ref[far_ref[0], h]
            bias = jnp.zeros((T, Q), f32)
            for b in range(REL_BUCKETS):
                bias = jnp.where(bucket == b, rb_ref[b, h] - far_bias, bias)
            bias0_ref[h] = bias
    e = (i + 1) * Q
    nt = (i + T // Q) // (T // Q)
    kf = jnp.float32(topk)
    wsc = wt_ref[0] * IDX_SCALE
    sub = lax.broadcasted_iota(i32, (T, Q), 0)
    qlane = lax.broadcasted_iota(i32, (T, Q), 1)
    limit = ((i * Q + qlane) // CHUNK + 1) * CHUNK
    at = at_ref[0, 0]

    def tile_start(j):
        return pl.multiple_of(e + KEY_PAD - (j + 1) * T, 128)

    def key_pos(j):
        return tile_start(j) - KEY_PAD + sub

    def score_tile(j, carry):
        kid = kidx_ref[0, pl.ds(tile_start(j), T), :]
        big_ref[...] = jnp.dot(kid, at, preferred_element_type=f32)
        s = jnp.zeros((T, Q), f32)
        for h in range(IDX_HEADS):
            s = s + wsc[h:h + 1, :] * jnp.maximum(big_ref[:, h * Q:(h + 1) * Q], 0.0)
        bits = lax.bitcast_convert_type(s, i32)
        key = jnp.where(bits < 0, bits ^ jnp.int32(0x7FFFFFFF), bits)
        key = jnp.where(s == 0.0, 0, key)
        kpos = key_pos(j)
        key = jnp.where(kpos >= 0, key, INT_MIN)
        sc_ref[j] = jnp.where(kpos < limit, key, INT_MIN)
        return carry

    lax.fori_loop(0, nt, score_tile, 0)

    def count_where(pred):
        def body(j, cnt):
            hit = jnp.where(pred(sc_ref[j], key_pos(j)), 1.0, 0.0)
            return cnt + jnp.sum(hit.reshape(T // 64, 64, Q), axis=0)
        cnt = lax.fori_loop(0, nt, body, jnp.zeros((64, Q), f32))
        return jnp.sum(cnt, axis=0, keepdims=True)

    def bit_body(b, carry):
        u, cacc = carry
        cand = u | lax.shift_left(jnp.int32(1), 31 - b)
        tvec = cand ^ jnp.int32(INT_MIN)
        tot = count_where(lambda k, kpos: k >= tvec)
        ok = tot >= kf
        return jnp.where(ok, cand, u), jnp.where(ok, tot, cacc)

    u, cacc = lax.fori_loop(0, 32, bit_body, (jnp.zeros((1, Q), i32), jnp.zeros((1, Q), f32)))
    thr = jnp.maximum(u ^ jnp.int32(INT_MIN), INT_MIN + 1)
    overflow = jnp.where(u != 0, cacc, 0.0) > kf
    n_over = jnp.max(jnp.where(overflow, 1.0, 0.0), axis=1, keepdims=True)[0, 0]

    @pl.when(n_over > 0.0)
    def _():
        need = kf - count_where(lambda k, kpos: k > thr)

        def cut_body(b, cut):
            cand = cut | lax.shift_left(jnp.int32(1), 14 - b)
            cnt = count_where(lambda k, kpos: jnp.where(k == thr, kpos, cand) < cand)
            return jnp.where(cnt <= need, cand, cut)

        cut = lax.fori_loop(0, 15, cut_body, jnp.zeros((1, Q), i32))

        def drop_tile(j, carry):
            k = sc_ref[j]
            drop = jnp.where(k == thr, key_pos(j), -1) >= cut
            sc_ref[j] = jnp.where(drop, INT_MIN, k)
            return carry

        lax.fori_loop(0, nt, drop_tile, 0)

    m_ref[...] = jnp.full(m_ref.shape, NEG_BIG, f32)
    l_ref[...] = jnp.zeros(l_ref.shape, f32)
    acc_ref[...] = jnp.zeros(acc_ref.shape, f32)
    qlatt = qlatt_ref[0, 0]

    def attn_tile(j, with_bias):
        kv = ckv_ref[0, pl.ds(tile_start(j), T), :]
        kvt = kv.T
        mask_add = jnp.where(sc_ref[j] >= thr, 0.0, NEG_BIG)
        pair_cols = [slice(g * 2 * Q, (g + 1) * 2 * Q) for g in range(A_HEADS // 2)]
        logits = lambda g: jnp.dot(kv, qlatt[:, pair_cols[g]], preferred_element_type=f32)
        x_next = logits(0)
        for g in range(A_HEADS // 2):
            x_pair = x_next
            if g + 1 < A_HEADS // 2:
                x_next = logits(g + 1)
            ps, alphas = [], []
            for u in range(2):
                h = 2 * g + u
                cols = slice(h * Q, (h + 1) * Q)
                x = x_pair[:, u * Q:(u + 1) * Q] + mask_add
                if with_bias:
                    x = x + bias0_ref[h]
                m_prev = m_ref[:, cols]
                m_new = jnp.maximum(m_prev, jnp.max(x, axis=0, keepdims=True))
                p = jnp.exp(x - m_new)
                alpha = jnp.exp(m_prev - m_new)
                l_ref[:, cols] = alpha * l_ref[:, cols] + jnp.sum(p, axis=0, keepdims=True)
                m_ref[:, cols] = m_new
                ps.append(p.astype(bf16))
                alphas.append(alpha)
            pv = jnp.dot(kvt, jnp.concatenate(ps, axis=1), preferred_element_type=f32)
            acc_ref[:, pair_cols[g]] = (
                jnp.concatenate(alphas, axis=1) * acc_ref[:, pair_cols[g]] + pv)

    attn_tile(0, True)

    def attn_body(j, carry):
        attn_tile(j, False)
        return carry

    lax.fori_loop(1, nt, attn_body, 0)

    inv_l = 1.0 / l_ref[...]
    outs = []
    for h in range(A_HEADS):
        cols = slice(h * Q, (h + 1) * Q)
        o_lat_t = (acc_ref[:, cols] * inv_l[:, cols]).astype(bf16)
        outs.append(jnp.dot(wuvt_ref[h], o_lat_t, preferred_element_type=f32))
    o_ref[0] = jnp.concatenate(outs, axis=0).T


def dsa_pallas(cq, ckv, kw, widx_t, g_cq, g_ckv, g_kidx, w_uq, w_iq, w_uk, w_uv, rel_bias):
    bsz, seq, _ = cq.shape
    f32, bf16 = jnp.float32, jnp.bfloat16
    Q, T, H = Q_BLOCK, KEY_TILE, A_HEADS
    nblk = seq // Q
    topk = min(IDX_TOPK_MAX, seq // 4)
    wiqt = w_iq.reshape(A_Q_LORA, IDX_HEADS * IDX_DIM).T.astype(bf16)
    wuqt = w_uq.reshape(A_Q_LORA, H * A_HEAD_DIM).T.astype(bf16)
    wuk = jnp.transpose(w_uk, (1, 0, 2)).astype(bf16)
    wuvt = jnp.transpose(w_uv, (1, 2, 0)).astype(bf16)
    tok = lambda w: pl.BlockSpec((1, Q, w), lambda b, t: (b, t, 0))
    full = lambda shape: pl.BlockSpec(shape, lambda b, t: (0,) * len(shape))
    blk = lambda r: pl.BlockSpec((1, 1, r, H * Q), lambda b, t: (b, t, 0, 0))
    a_t, qlat_t, ckvn, kidxn = pl.pallas_call(
        _dsa_prep_kernel,
        grid=(bsz, nblk),
        in_specs=[tok(A_Q_LORA), tok(A_KV_LORA), tok(128),
                  full((1, A_Q_LORA)), full((1, A_KV_LORA)), full((1, IDX_DIM)),
                  full(wiqt.shape), full(wuqt.shape), full(wuk.shape)],
        out_specs=[blk(IDX_DIM), blk(A_KV_LORA), tok(A_KV_LORA), tok(IDX_DIM)],
        out_shape=[jax.ShapeDtypeStruct((bsz, nblk, IDX_DIM, H * Q), bf16),
                   jax.ShapeDtypeStruct((bsz, nblk, A_KV_LORA, H * Q), bf16),
                   jax.ShapeDtypeStruct((bsz, seq, A_KV_LORA), bf16),
                   jax.ShapeDtypeStruct((bsz, seq, IDX_DIM), bf16)],
        name="dsa_prep",
    )(cq, ckv, kw, g_cq.reshape(1, -1), g_ckv.reshape(1, -1), g_kidx.reshape(1, -1),
      wiqt, wuqt, wuk)
    ckvp = jnp.pad(ckvn, ((0, 0), (KEY_PAD, 0), (0, 0)))
    kidxp = jnp.pad(kidxn, ((0, 0), (KEY_PAD, 0), (0, 0)))
    rel = (jnp.arange(T, dtype=jnp.int32)[:, None] - jnp.arange(Q, dtype=jnp.int32)[None, :]
           - KEY_PAD)
    bucket0 = t5_bucket(rel)
    bucket_far = t5_bucket(jnp.full((1,), -KEY_PAD - 1, jnp.int32))
    smem = pl.BlockSpec(memory_space=pltpu.SMEM)
    nt_max = (nblk - 1 + T // Q) // (T // Q)
    skey = seq + KEY_PAD
    return pl.pallas_call(
        partial(_dsa_main_kernel, topk=topk),
        grid=(bsz, nblk),
        in_specs=[blk(IDX_DIM), blk(A_KV_LORA),
                  pl.BlockSpec((1, IDX_HEADS, Q), lambda b, t: (b, 1, t)),
                  pl.BlockSpec((1, skey, IDX_DIM), lambda b, t: (b, 0, 0)),
                  pl.BlockSpec((1, skey, A_KV_LORA), lambda b, t: (b, 0, 0)),
                  full((T, Q)), smem, smem, full(wuvt.shape)],
        out_specs=tok(H * A_HEAD_DIM),
        out_shape=jax.ShapeDtypeStruct((bsz, seq, H * A_HEAD_DIM), f32),
        scratch_shapes=[pltpu.VMEM((H, T, Q), f32),
                        pltpu.VMEM((nt_max, T, Q), jnp.int32),
                        pltpu.VMEM((T, H * Q), f32),
                        pltpu.VMEM((A_KV_LORA, H * Q), f32),
                        pltpu.VMEM((1, H * Q), f32),
                        pltpu.VMEM((1, H * Q), f32)],
        compiler_params=pltpu.CompilerParams(
            dimension_semantics=("arbitrary", "arbitrary"), vmem_limit_bytes=VMEM_LIMIT),
        name="dsa_main",
    )(a_t, qlat_t, widx_t, kidxp, ckvp, bucket0, bucket_far, rel_bias.astype(f32), wuvt)


PEER_SCORE_TOKENS = 256
PEER_GATHER_TOKENS = 128
PEER_SLOTS = P_HEADS * P_TOPK
WORDS_PER_ROW = 4
PEER_TILES = 16
PEER_GROUP = 2


def _top16(s, order=None, payload=None):
    if order is None:
        order = lax.broadcasted_iota(jnp.int32, s.shape, 0).astype(jnp.float32)
    vals, picks = [], []
    for _ in range(P_TOPK):
        m = jnp.max(s, axis=0, keepdims=True)
        pos = jnp.min(jnp.where(s == m, order, float(N_EXPERTS)), axis=0, keepdims=True)
        hit = order == pos
        vals.append(m)
        if payload is None:
            picks.append(pos)
        else:
            picks.append(jnp.max(jnp.where(hit, payload, -1.0), axis=0, keepdims=True))
        s = jnp.where(hit, -jnp.inf, s)
    return vals, picks


def _staircase_candidates(v1, i1, v2, i2):
    v1m, i1m = jnp.concatenate(v1, axis=0), jnp.concatenate(i1, axis=0)
    v2m, i2m = jnp.concatenate(v2, axis=0), jnp.concatenate(i2, axis=0)
    t = v1m.shape[1]
    sub8 = lax.broadcasted_iota(jnp.int32, (8, t), 0).astype(jnp.float32)
    cand, cidx, rank = [], [], []
    for a in range(8):
        cand.append(v1[a] + v2m[:8])
        cidx.append(i1[a] * float(N_KEYS) + i2m[:8])
        rank.append(sub8 + float(a * P_TOPK))
    cand.append(v1[0] + v2m[8:])
    cidx.append(i1[0] * float(N_KEYS) + i2m[8:])
    rank.append(sub8 + 8.0)
    cand.append(v1m[8:] + v2[0])
    cidx.append(i1m[8:] * float(N_KEYS) + i2[0])
    rank.append((sub8 + 8.0) * float(P_TOPK))
    cat = lambda xs: jnp.concatenate(xs, axis=0)
    return cat(cand), cat(rank), cat(cidx)


def _peer_score_kernel(x_ref, g_ref, wpqt_ref, sk_ref, hn_ref, eidx_ref, gate_ref):
    f32, bf16 = jnp.float32, jnp.bfloat16
    h = _rms(x_ref[...], g_ref[...])
    hb = h.astype(bf16)
    for s in range(h.shape[1] // 128):
        hn_ref[pl.ds(s, h.shape[0], stride=8), :] = h[:, s * 128:(s + 1) * 128]
    qrt = _nt_dot(wpqt_ref[...], hb)
    half = P_QDIM // 2
    e_rows, g_rows = [], []
    for hd in range(P_HEADS):
        tops = []
        for p in range(2):
            qhp = qrt[(hd * 2 + p) * half:(hd * 2 + p + 1) * half, :].astype(bf16)
            s = jnp.dot(sk_ref[hd * 2 + p], qhp, preferred_element_type=f32)
            tops.append(_top16(s))
        (v1, i1), (v2, i2) = tops
        best, be = _top16(*_staircase_candidates(v1, i1, v2, i2))
        ex = [jnp.exp(b - best[0]) for b in best]
        den = ex[0]
        for k in range(1, P_TOPK):
            den = den + ex[k]
        inv = 1.0 / den
        e_rows += be
        g_rows += [x * inv for x in ex]
    eidx_ref[...] = (jnp.concatenate(e_rows, axis=0).T * float(WORDS_PER_ROW)).astype(jnp.int32)
    gate_ref[...] = jnp.concatenate(g_rows, axis=0).T


def _diag_mask():
    r = lax.broadcasted_iota(jnp.int32, (8, PEER_SLOTS * 8), 0)
    c = lax.broadcasted_iota(jnp.int32, (8, PEER_SLOTS * 8), 1)
    return (c & 7) == r


def _gather_group(idx_ref, tab_ref, g_refs, t0):
    views = [idx_ref.at[t0 + u] for u in range(len(g_refs))]
    for r in range(PEER_SLOTS):
        for view, g_ref in zip(views, g_refs):
            row0 = pl.multiple_of(view[r], WORDS_PER_ROW)
            g_ref[r * WORDS_PER_ROW:(r + 1) * WORDS_PER_ROW, :] = tab_ref[
                pl.ds(row0, WORDS_PER_ROW), :]


def _gather_pipeline(idx_ref, tab_ref, tiles, n_tokens, consume):
    grp = PEER_GROUP
    groups = len(tiles) // grp
    for g_ref in tiles[-grp:]:
        g_ref[...] = jnp.zeros(g_ref.shape, g_ref.dtype)

    def drain(group_tiles, group):
        for u in range(grp):
            consume(group_tiles[u], group * (grp // 2) + u // 2, u % 2)

    def trip(i, carry):
        for k in range(groups):
            group = groups * i + k
            _gather_group(idx_ref, tab_ref, tiles[k * grp:(k + 1) * grp], group * grp)
            drain(tiles[(k - 1) * grp:k * grp] if k else tiles[-grp:], jnp.maximum(group - 1, 0))
        return carry

    lax.fori_loop(0, n_tokens // len(tiles), trip, 0)
    drain(tiles[-grp:], n_tokens // grp - 1)


def _peer_act_kernel(idx_ref, hn_ref, tab_ref, a_ref, m_ref, *tiles):
    f32, bf16 = jnp.float32, jnp.bfloat16
    tb = a_ref.shape[0]

    def dots(g_ref, tp, u):
        hp = hn_ref[pl.ds(pl.multiple_of(tp * 16, 16), 16), :].astype(bf16)
        m = _nt_dot(hp, pltpu.bitcast(g_ref[...], bf16))
        m_ref[pl.ds(pl.multiple_of((tp * 2 + u) * 8, 8), 8), :] = m[u * 8:(u + 1) * 8, :]

    _gather_pipeline(idx_ref, tab_ref, tiles, tb, dots)
    m3 = m_ref[...].reshape(tb, 8, PEER_SLOTS * 8)
    z = jnp.sum(jnp.where(_diag_mask()[None], m3, 0.0), axis=1)
    rr = lax.broadcasted_iota(jnp.int32, (PEER_SLOTS * 8, PEER_SLOTS), 0)
    cc = lax.broadcasted_iota(jnp.int32, (PEER_SLOTS * 8, PEER_SLOTS), 1)
    pool = jnp.where((rr >> 3) == cc, 1.0, 0.0).astype(bf16)
    z_hi = z.astype(bf16)
    z_lo = (z - z_hi.astype(f32)).astype(bf16)
    a_ref[...] = (jnp.dot(z_hi, pool, preferred_element_type=f32)
                  + jnp.dot(z_lo, pool, preferred_element_type=f32))


def _peer_out_kernel(idx_ref, a_ref, gate_ref, tab_ref, o_ref, w_ref, *tiles):
    f32, bf16 = jnp.float32, jnp.bfloat16
    tb = a_ref.shape[0]
    a = a_ref[...]
    act = 0.5 * a * (1.0 + lax.erf(a * (2.0 ** -0.5)))
    wgt = (gate_ref[...] * act).astype(bf16)
    rr = lax.broadcasted_iota(jnp.int32, (PEER_SLOTS, PEER_SLOTS * 8), 0)
    cc = lax.broadcasted_iota(jnp.int32, (PEER_SLOTS, PEER_SLOTS * 8), 1)
    expand = jnp.where((cc >> 3) == rr, 1.0, 0.0).astype(bf16)
    w_ref[...] = jnp.dot(wgt, expand, preferred_element_type=f32)
    diag = _diag_mask()

    def combine(g_ref, tp, u):
        t = tp * 2 + u
        wrow = jnp.broadcast_to(w_ref[pl.ds(t, 1), :], (8, PEER_SLOTS * 8))
        wsel = jnp.where(diag, wrow, 0.0).astype(bf16)
        o_ref[pl.ds(pl.multiple_of(t * 8, 8), 8), :] = jnp.dot(
            wsel, pltpu.bitcast(g_ref[...], bf16), preferred_element_type=f32)

    _gather_pipeline(idx_ref, tab_ref, tiles, tb, combine)


PACK_ROWS = 512


def _pack_table_kernel(x_ref, o_ref):
    rows = x_ref.shape[0]
    bf16_bits = lambda v: lax.bitcast_convert_type(
        v.astype(jnp.bfloat16).astype(jnp.float32), jnp.int32)
    for c in range(WORDS_PER_ROW):
        lo = bf16_bits(x_ref[:, (2 * c) * 128:(2 * c + 1) * 128])
        hi = bf16_bits(x_ref[:, (2 * c + 1) * 128:(2 * c + 2) * 128])
        word = (hi & jnp.int32(-65536)) | lax.shift_right_logical(lo, 16)
        o_ref[pl.ds(c, rows, stride=WORDS_PER_ROW), :] = word


def _pack_table(tab):
    n_e, d = tab.shape
    return pl.pallas_call(
        _pack_table_kernel,
        grid=(n_e // PACK_ROWS,),
        in_specs=[pl.BlockSpec((PACK_ROWS, d), lambda i: (i, 0))],
        out_specs=pl.BlockSpec((PACK_ROWS * WORDS_PER_ROW, 128), lambda i: (i, 0)),
        out_shape=jax.ShapeDtypeStruct((n_e * WORDS_PER_ROW, 128), jnp.int32),
        name="pack_table",
    )(tab)


def peer_pallas(x, g_ffn, w_pq, sub_keys, u_emb, v_emb):
    n_tok, d = x.shape
    f32, bf16 = jnp.float32, jnp.bfloat16
    ts, tg = PEER_SCORE_TOKENS, PEER_GATHER_TOKENS
    wpqt = w_pq.reshape(d, P_HEADS * P_QDIM).T.astype(bf16)
    sk = sub_keys.reshape(P_HEADS * 2, N_KEYS, P_QDIM // 2).astype(bf16)
    hn, eidx, gate = pl.pallas_call(
        _peer_score_kernel,
        grid=(n_tok // ts,),
        in_specs=[pl.BlockSpec((ts, d), lambda i: (i, 0)),
                  pl.BlockSpec((1, d), lambda i: (0, 0)),
                  pl.BlockSpec(wpqt.shape, lambda i: (0, 0)),
                  pl.BlockSpec(sk.shape, lambda i: (0, 0, 0))],
        out_specs=[pl.BlockSpec((ts * 8, 128), lambda i: (i, 0)),
                   pl.BlockSpec((ts, PEER_SLOTS), lambda i: (i, 0)),
                   pl.BlockSpec((ts, PEER_SLOTS), lambda i: (i, 0))],
        out_shape=[jax.ShapeDtypeStruct((n_tok * 8, 128), f32),
                   jax.ShapeDtypeStruct((n_tok, PEER_SLOTS), jnp.int32),
                   jax.ShapeDtypeStruct((n_tok, PEER_SLOTS), f32)],
        compiler_params=pltpu.CompilerParams(vmem_limit_bytes=VMEM_LIMIT),
        name="peer_score",
    )(x, g_ffn.reshape(1, d), wpqt, sk)
    utab, vtab = _pack_table(u_emb), _pack_table(v_emb)
    idx_spec = pl.BlockSpec((tg, PEER_SLOTS), lambda i: (i, 0), memory_space=pltpu.SMEM)
    tab_spec = pl.BlockSpec(memory_space=pltpu.VMEM)
    slot_spec = pl.BlockSpec((tg, PEER_SLOTS), lambda i: (i, 0))
    row_spec = pl.BlockSpec((tg * 8, 128), lambda i: (i, 0))
    gbuf = pltpu.VMEM((PEER_SLOTS * WORDS_PER_ROW, 128), jnp.int32)
    cparams = pltpu.CompilerParams(vmem_limit_bytes=VMEM_LIMIT)
    act = pl.pallas_call(
        _peer_act_kernel,
        grid=(n_tok // tg,),
        in_specs=[idx_spec, row_spec, tab_spec],
        out_specs=slot_spec,
        out_shape=jax.ShapeDtypeStruct((n_tok, PEER_SLOTS), f32),
        scratch_shapes=[pltpu.VMEM((tg * 8, PEER_SLOTS * 8), f32)] + [gbuf] * PEER_TILES,
        compiler_params=cparams,
        name="peer_act",
    )(eidx, hn, utab)
    out = pl.pallas_call(
        _peer_out_kernel,
        grid=(n_tok // tg,),
        in_specs=[idx_spec, slot_spec, slot_spec, tab_spec],
        out_specs=row_spec,
        out_shape=jax.ShapeDtypeStruct((n_tok * 8, 128), f32),
        scratch_shapes=[pltpu.VMEM((tg, PEER_SLOTS * 8), f32)] + [gbuf] * PEER_TILES,
        compiler_params=cparams,
        name="peer_out",
    )(eidx, act, gate, vtab)
    return out


def _token_rows(y_ref):
    tm = y_ref.shape[0] // 8
    return jnp.concatenate([y_ref[pl.ds(s, tm, stride=8), :] for s in range(8)], axis=1)


def _final_kernel(x_ref, y_ref, g_ref, o_ref):
    o_ref[...] = _rms(x_ref[...] + _token_rows(y_ref), g_ref[...])


def _residual_kernel(x_ref, y_ref, o_ref):
    o_ref[...] = x_ref[...] + _token_rows(y_ref)


def residual_pallas(x, y8, g=None):
    n, d = x.shape
    tm = 512
    row = pl.BlockSpec((tm, d), lambda i: (i, 0))
    row8 = pl.BlockSpec((tm * 8, 128), lambda i: (i, 0))
    if g is None:
        kern, extra, extra_specs, name = _residual_kernel, (), [], "residual"
    else:
        kern, extra, name = _final_kernel, (g.reshape(1, d),), "final_rmsnorm"
        extra_specs = [pl.BlockSpec((1, d), lambda i: (0, 0))]
    return pl.pallas_call(
        kern,
        grid=(n // tm,),
        in_specs=[row, row8] + extra_specs,
        out_specs=row,
        out_shape=jax.ShapeDtypeStruct((n, d), x.dtype),
        name=name,
    )(x, y8, *extra)


IN_PROJ_TOKENS = 512
GDN_COLS = 3 * B_QK
GATE_ROWS = 8


def _in_proj_kernel(x_ref, g_ref, wa_ref, wkw_ref, wqkv_ref, wz_ref, wab_ref, wabt_ref,
                    cq_ref, ckv_ref, kw_ref, qkv_ref, z_ref, ab_ref, abt_ref):
    f32 = jnp.float32
    hb = _rms(x_ref[0], g_ref[...]).astype(jnp.bfloat16)
    a = jnp.dot(hb, wa_ref[...], preferred_element_type=f32)
    cq_ref[0] = a[:, :A_Q_LORA]
    ckv_ref[0] = a[:, A_Q_LORA:]
    kw_ref[0] = jnp.dot(hb, wkw_ref[...], preferred_element_type=f32)
    qkv_ref[0] = jnp.dot(hb, wqkv_ref[...], preferred_element_type=f32)
    z_ref[0] = jnp.dot(hb, wz_ref[...], preferred_element_type=f32)
    ab_ref[0] = jnp.dot(hb, wab_ref[...], preferred_element_type=f32)
    abt_ref[0] = _nt_dot(wabt_ref[...], hb)


def in_proj_pallas(x, g_mix, w_in):
    bsz, seq, d = x.shape
    f32, bf16 = jnp.float32, jnp.bfloat16
    tm = min(IN_PROJ_TOKENS, seq)
    o = np.cumsum((0,) + COL_WIDTHS)
    wb = w_in.astype(bf16)
    pad_cols = lambda w: jnp.pad(w, ((0, 0), (0, 128 - w.shape[1])))
    wa = wb[:, o[0]:o[2]]
    wkw = pad_cols(wb[:, o[2]:o[4]])
    wqkv = wb[:, o[4]:o[7]]
    wz = wb[:, o[7]:o[8]]
    wab = pad_cols(wb[:, o[8]:o[10]])
    wabt = jnp.concatenate([wb[:, o[8]:o[10]], wb[:, o[3]:o[4]]], axis=1).T
    full = lambda w: pl.BlockSpec(w.shape, lambda b, t: (0, 0))
    tok = lambda w: pl.BlockSpec((1, tm, w), lambda b, t: (b, t, 0))
    shp = lambda w: jax.ShapeDtypeStruct((bsz, seq, w), f32)
    return pl.pallas_call(
        _in_proj_kernel,
        grid=(bsz, seq // tm),
        in_specs=[tok(d), pl.BlockSpec((1, d), lambda b, t: (0, 0)),
                  full(wa), full(wkw), full(wqkv), full(wz), full(wab), full(wabt)],
        out_specs=[tok(A_Q_LORA), tok(A_KV_LORA), tok(128), tok(GDN_COLS), tok(B_QK), tok(128),
                   pl.BlockSpec((1, GATE_ROWS + IDX_HEADS, tm), lambda b, t: (b, 0, t))],
        out_shape=[shp(A_Q_LORA), shp(A_KV_LORA), shp(128), shp(GDN_COLS), shp(B_QK), shp(128),
                   jax.ShapeDtypeStruct((bsz, GATE_ROWS + IDX_HEADS, seq), f32)],
        compiler_params=pltpu.CompilerParams(vmem_limit_bytes=VMEM_LIMIT),
        name="in_proj",
    )(x, g_mix.reshape(1, d), wa, wkw, wqkv, wz, wab, wabt)


def _softplus(x):
    return jnp.maximum(x, 0.0) + jnp.log1p(jnp.exp(-jnp.abs(x)))


def _sigmoid(x):
    return 1.0 / (1.0 + jnp.exp(-x))


def _gdn_gates(pre, a_log, dt_bias, is_decay):
    g = -jnp.exp(a_log) * _softplus(pre + dt_bias)
    return jnp.where(is_decay, g, _sigmoid(pre))


def _gdn_prep_kernel(qkv_ref, halo_ref, cw_ref, ab_ref, abt_ref, alc_ref, dtc_ref, alr_ref, dtr_ref,
                     q_ref, k_ref, v_ref, gc_ref, gr_ref):
    tm = qkv_ref.shape[1]
    x = qkv_ref[0]
    halo = jnp.where(pl.program_id(1) > 0, halo_ref[0], 0.0)
    full = jnp.concatenate([halo, x], axis=0)
    y = x * cw_ref[CONV_WIDTH - 1:CONV_WIDTH, :]
    for back in range(1, CONV_WIDTH):
        shifted = pltpu.roll(full, back, axis=0)[8:, :]
        y = y + shifted * cw_ref[CONV_WIDTH - 1 - back:CONV_WIDTH - back, :]
    y = y * _sigmoid(y)
    for h in range(B_HEADS):
        cols = slice(h * B_HEAD_DIM, (h + 1) * B_HEAD_DIM)
        qh = y[:, h * B_HEAD_DIM:(h + 1) * B_HEAD_DIM]
        kh = y[:, B_QK + h * B_HEAD_DIM:B_QK + (h + 1) * B_HEAD_DIM]
        q_ref[0, :, cols] = qh * lax.rsqrt(
            jnp.sum(qh * qh, axis=-1, keepdims=True) + EPS) * (B_HEAD_DIM ** -0.5)
        k_ref[0, :, cols] = kh * lax.rsqrt(jnp.sum(kh * kh, axis=-1, keepdims=True) + EPS)
    v_ref[0] = y[:, 2 * B_QK:]
    lane = lax.broadcasted_iota(jnp.int32, (tm, 128), 1)
    gates_c = _gdn_gates(ab_ref[0], alc_ref[...], dtc_ref[...], lane < B_HEADS)
    row = lax.broadcasted_iota(jnp.int32, (GATE_ROWS, tm), 0)
    gates_r = _gdn_gates(abt_ref[0], alr_ref[...], dtr_ref[...], row < B_HEADS)
    ti = lax.broadcasted_iota(jnp.int32, (tm, tm), 0)
    tj = lax.broadcasted_iota(jnp.int32, (tm, tm), 1)
    same_chunk = (ti // CHUNK) == (tj // CHUNK)
    hi = lax.Precision.HIGHEST
    lower = jnp.where(same_chunk & (tj <= ti), 1.0, 0.0)
    upper = jnp.where(same_chunk & (ti <= tj), 1.0, 0.0)
    cum_c = jnp.dot(lower, gates_c, preferred_element_type=jnp.float32, precision=hi)
    cum_r = jnp.dot(gates_r, upper, preferred_element_type=jnp.float32, precision=hi)
    gc_ref[0] = jnp.where(lane < B_HEADS, cum_c, gates_c)
    gr_ref[0] = jnp.where(row < B_HEADS, cum_r, gates_r)


def _gdn_main_kernel(q_ref, k_ref, v_ref, z_ref, gc_ref, gr_ref, gon_ref, o_ref, s_ref):
    f32, bf16 = jnp.float32, jnp.bfloat16
    C, Dh = CHUNK, B_HEAD_DIM

    @pl.when(pl.program_id(1) == 0)
    def _():
        s_ref[...] = jnp.zeros(s_ref.shape, f32)

    ii = lax.broadcasted_iota(jnp.int32, (C, C), 0)
    jj = lax.broadcasted_iota(jnp.int32, (C, C), 1)
    causal = ii >= jj
    strict = ii > jj
    eye = jnp.where(ii == jj, 1.0, 0.0)
    mm = lambda a, b: jnp.dot(a, b, preferred_element_type=f32)
    n_chunks = q_ref.shape[1] // C
    units = []
    for c in range(n_chunks):
        rows = slice(c * C, (c + 1) * C)
        gates_c = gc_ref[0, rows, :]
        gates_r = gr_ref[0, :, c * C:(c + 1) * C]
        for h in range(B_HEADS):
            cols = slice(h * Dh, (h + 1) * Dh)
            gcum = jnp.broadcast_to(gates_c[:, h:h + 1], (C, Dh))
            beta = jnp.broadcast_to(gates_c[:, B_HEADS + h:B_HEADS + h + 1], (C, Dh))
            gcum_r = jnp.broadcast_to(gates_r[h:h + 1, :], (C, C))
            decay = jnp.where(causal, jnp.exp(jnp.minimum(gcum[:, :C] - gcum_r, 0.0)), 0.0)
            q, k, v = q_ref[0, rows, cols], k_ref[0, rows, cols], v_ref[0, rows, cols]
            qb, kb = q.astype(bf16), k.astype(bf16)
            kk = _nt_dot(kb, kb)
            qk = _nt_dot(qb, kb)
            egc = jnp.exp(gcum)
            g_last = gcum[C - 1:C, :]
            units.append(dict(
                c=c, h=h, rows=rows, cols=cols,
                neg_m=jnp.where(strict, -(beta[:, :C] * kk * decay), 0.0),
                rhs=jnp.concatenate([v * beta, k * (beta * egc)], axis=1),
                q_dec=(q * egc).astype(bf16), intra=(qk * decay).astype(bf16),
                k_tail=(k * jnp.exp(g_last - gcum)).astype(bf16), chunk_decay=jnp.exp(g_last)))
    def halves(a):
        a_h = a.astype(bf16)
        return a_h, (a - a_h.astype(f32)).astype(bf16)

    def mm3(a, b):
        (a_h, a_l), (b_h, b_l) = a, b
        return mm(a_h, b_h) + (mm(a_h, b_l) + mm(a_l, b_h))

    t_inv = [eye + un["neg_m"] for un in units]
    p_halves = [halves(un["neg_m"]) for un in units]
    for _ in range(5):
        p_halves = [halves(mm3(ph, ph)) for ph in p_halves]
        t_inv = [t + mm3(ph, halves(t)) for ph, t in zip(p_halves, t_inv)]
    sols = [mm3(halves(t), halves(un["rhs"])) for t, un in zip(t_inv, units)]
    states = [s_ref[h] for h in range(B_HEADS)]
    for c in range(n_chunks):
        group = [(un, sol) for un, sol in zip(units, sols) if un["c"] == c]
        sbs = [states[un["h"]].astype(bf16) for un, _ in group]
        vbs = [(sol[:, :Dh] - mm(sol[:, Dh:].astype(bf16), sb)).astype(bf16)
               for (un, sol), sb in zip(group, sbs)]
        outs = [mm(un["q_dec"], sb) + mm(un["intra"], vb)
                for (un, _), sb, vb in zip(group, sbs, vbs)]
        for (un, _), vb in zip(group, vbs):
            states[un["h"]] = states[un["h"]] * un["chunk_decay"] + lax.dot_general(
                un["k_tail"], vb, (((0,), (0,)), ((), ())), preferred_element_type=f32)
        for (un, _), o in zip(group, outs):
            o = o * lax.rsqrt(jnp.mean(o * o, axis=-1, keepdims=True) + EPS) * gon_ref[...]
            zz = z_ref[0, un["rows"], un["cols"]]
            o_ref[0, un["rows"], un["cols"]] = o * (zz * _sigmoid(zz))
    for h in range(B_HEADS):
        s_ref[h] = states[h]


def gdn_pallas(qkv, z, ab, abt, conv_w, a_log, dt_bias, g_onorm):
    bsz, seq, _ = qkv.shape
    f32 = jnp.float32
    tm = min(256, seq)
    zero4 = jnp.zeros((B_HEADS,), f32)
    lane_row = lambda v: jnp.pad(jnp.concatenate([v.astype(f32), zero4]), (0, 120)).reshape(1, 128)
    sub_col = lambda v: jnp.concatenate([v.astype(f32), zero4]).reshape(GATE_ROWS, 1)
    tok = lambda w, t=tm: pl.BlockSpec((1, t, w), lambda b, i: (b, i, 0))
    const = lambda shape: pl.BlockSpec(shape, lambda b, i: (0,) * len(shape))
    shp = lambda w: jax.ShapeDtypeStruct((bsz, seq, w), f32)
    q, k, v, gc, gr = pl.pallas_call(
        _gdn_prep_kernel,
        grid=(bsz, seq // tm),
        in_specs=[tok(GDN_COLS),
                  pl.BlockSpec((1, 8, GDN_COLS), lambda b, i: (b, jnp.maximum(i * (tm // 8) - 1, 0), 0)),
                  const((CONV_WIDTH, GDN_COLS)), tok(128),
                  pl.BlockSpec((1, GATE_ROWS, tm), lambda b, i: (b, 0, i)),
                  const((1, 128)), const((1, 128)), const((GATE_ROWS, 1)), const((GATE_ROWS, 1))],
        out_specs=[tok(B_QK), tok(B_QK), tok(B_QK), tok(128),
                   pl.BlockSpec((1, GATE_ROWS, tm), lambda b, i: (b, 0, i))],
        out_shape=[shp(B_QK), shp(B_QK), shp(B_QK), shp(128),
                   jax.ShapeDtypeStruct((bsz, GATE_ROWS, seq), f32)],
        compiler_params=pltpu.CompilerParams(vmem_limit_bytes=VMEM_LIMIT),
        name="gdn_prep",
    )(qkv, qkv, conv_w.astype(f32), ab, abt, lane_row(a_log), lane_row(dt_bias),
      sub_col(a_log), sub_col(dt_bias))
    ts = 2 * CHUNK
    return pl.pallas_call(
        _gdn_main_kernel,
        grid=(bsz, seq // ts),
        in_specs=[tok(B_QK, ts), tok(B_QK, ts), tok(B_QK, ts), tok(B_QK, ts), tok(128, ts),
                  pl.BlockSpec((1, GATE_ROWS, ts), lambda b, i: (b, 0, i)),
                  const((1, B_HEAD_DIM))],
        out_specs=tok(B_QK, ts),
        out_shape=shp(B_QK),
        scratch_shapes=[pltpu.VMEM((B_HEADS, B_HEAD_DIM, B_HEAD_DIM), f32)],
        compiler_params=pltpu.CompilerParams(dimension_semantics=("arbitrary", "arbitrary")),
        name="gdn_main",
    )(q, k, v, z, gc, gr, g_onorm.astype(f32).reshape(1, B_HEAD_DIM))


def _mem_kv_kernel(mem_ref, g_ref, wk_ref, wv_ref, k_ref, v_ref):
    f32, bf16 = jnp.float32, jnp.bfloat16
    mn = _rms(mem_ref[0], g_ref[...]).astype(bf16)
    k = jnp.dot(mn, wk_ref[...], preferred_element_type=f32)
    v = jnp.dot(mn, wv_ref[...], preferred_element_type=f32)
    for h in range(X_HEADS):
        cols = slice(h * X_HEAD_DIM, (h + 1) * X_HEAD_DIM)
        k_ref[0, h] = k[:, cols].astype(bf16)
        v_ref[0, h] = v[:, cols].astype(bf16)


def _mid_kernel(x_ref, oa_ref, ob_ref, wo_ref, gx_ref, wq_ref, k_ref, v_ref, wox_ref, o_ref):
    f32, bf16 = jnp.float32, jnp.bfloat16
    na = oa_ref.shape[2]
    x1 = (x_ref[0]
          + jnp.dot(oa_ref[0].astype(bf16), wo_ref[:na, :], preferred_element_type=f32)
          + jnp.dot(ob_ref[0].astype(bf16), wo_ref[na:, :], preferred_element_type=f32))
    hq = _rms(x1, gx_ref[...]).astype(bf16)
    q = jnp.dot(hq, wq_ref[...], preferred_element_type=f32)
    heads = []
    for h in range(X_HEADS):
        qh = q[:, h * X_HEAD_DIM:(h + 1) * X_HEAD_DIM].astype(bf16)
        lg = _nt_dot(qh, k_ref[0, h]) * (X_HEAD_DIM ** -0.5)
        p = jnp.exp(lg - jnp.max(lg, axis=-1, keepdims=True))
        p = (p / jnp.sum(p, axis=-1, keepdims=True)).astype(bf16)
        heads.append(jnp.dot(p, v_ref[0, h], preferred_element_type=f32).astype(bf16))
    o = jnp.concatenate(heads, axis=1)
    o_ref[0] = x1 + jnp.dot(o, wox_ref[...], preferred_element_type=f32)


def mid_pallas(x, o_a, o_b, w_out, g_cross, mem, g_mem, wq, wk, wv, wo):
    bsz, seq, d = x.shape
    f32, bf16 = jnp.float32, jnp.bfloat16
    hx = X_HEADS * X_HEAD_DIM
    m_len = mem.shape[1]
    const2 = lambda shape: pl.BlockSpec(shape, lambda b: (0,) * len(shape))
    kv_spec = pl.BlockSpec((1, X_HEADS, m_len, X_HEAD_DIM), lambda b: (b, 0, 0, 0))
    kv_shape = jax.ShapeDtypeStruct((bsz, X_HEADS, m_len, X_HEAD_DIM), bf16)
    k, v = pl.pallas_call(
        _mem_kv_kernel,
        grid=(bsz,),
        in_specs=[pl.BlockSpec((1, m_len, d), lambda b: (b, 0, 0)), const2((1, d)),
                  const2((d, hx)), const2((d, hx))],
        out_specs=[kv_spec, kv_spec],
        out_shape=[kv_shape, kv_shape],
        name="mem_kv",
    )(mem, g_mem.reshape(1, d), wk.reshape(d, hx).astype(bf16), wv.reshape(d, hx).astype(bf16))
    tm = min(256, seq)
    tok = lambda w: pl.BlockSpec((1, tm, w), lambda b, t: (b, t, 0))
    const = lambda shape: pl.BlockSpec(shape, lambda b, t: (0,) * len(shape))
    kv_spec2 = pl.BlockSpec((1, X_HEADS, m_len, X_HEAD_DIM), lambda b, t: (b, 0, 0, 0))
    return pl.pallas_call(
        _mid_kernel,
        grid=(bsz, seq // tm),
        in_specs=[tok(d), tok(o_a.shape[2]), tok(o_b.shape[2]), const((MIX_WIDTH, d)),
                  const((1, d)), const((d, hx)), kv_spec2, kv_spec2, const((hx, d))],
        out_specs=tok(d),
        out_shape=jax.ShapeDtypeStruct((bsz, seq, d), f32),
        compiler_params=pltpu.CompilerParams(vmem_limit_bytes=VMEM_LIMIT),
        name="mid",
    )(x, o_a, o_b, w_out.astype(bf16), g_cross.reshape(1, d), wq.reshape(d, hx).astype(bf16),
      k, v, wo.reshape(hx, d).astype(bf16))


def kernel(x, mem, g_mix, w_in, g_cq, g_ckv, g_kidx, w_uq, w_iq, w_uk, w_uv, rel_bias, conv_w, A_log, dt_bias, g_onorm, w_out, g_cross, g_mem, wq_x, wk_x, wv_x, wo_x, g_ffn, w_pq, sub_keys, u_emb, v_emb, g_final):
    bsz, seq, d = x.shape
    for l in range(DEPTH):
        cq, ckv, kw, qkv, z, ab, abt = in_proj_pallas(x, g_mix[l], w_in[l])
        o_a = dsa_pallas(cq, ckv, kw, abt, g_cq[l], g_ckv[l], g_kidx[l],
                         w_uq[l], w_iq[l], w_uk[l], w_uv[l], rel_bias)
        o_b = gdn_pallas(qkv, z, ab, abt, conv_w[l], A_log[l], dt_bias[l], g_onorm[l])
        x = mid_pallas(x, o_a, o_b, w_out[l], g_cross[l], mem, g_mem[l],
                       wq_x[l], wk_x[l], wv_x[l], wo_x[l])
        xf = x.reshape(bsz * seq, d)
        y8 = peer_pallas(xf, g_ffn[l], w_pq[l], sub_keys[l], u_emb[l], v_emb[l])
        if l + 1 < DEPTH:
            x = residual_pallas(xf, y8).reshape(bsz, seq, d)
    return residual_pallas(xf, y8, g_final).reshape(bsz, seq, d)
```

```python
import math
from functools import partial
import jax
import jax.numpy as jnp
from jax import lax
import numpy as np
from jax.experimental import pallas as pl
from jax.experimental.pallas import tpu as pltpu

DEPTH = 1

CHUNK = 64
Q_BLOCK = 128
EPS = 1e-6

A_HEADS = 8
A_HEAD_DIM = 64
A_Q_LORA = 256
A_KV_LORA = 256
IDX_HEADS = 8
IDX_DIM = 64
IDX_TOPK_MAX = 256
ATTN_SCALE = A_HEAD_DIM ** -0.5
IDX_SCALE = (IDX_HEADS * IDX_DIM) ** -0.5

B_HEADS = 4
B_HEAD_DIM = 128
B_QK = B_HEADS * B_HEAD_DIM
CONV_WIDTH = 4

REL_BUCKETS = 32
REL_MAX_DIST = 128

X_HEADS = 4
X_HEAD_DIM = 128

P_HEADS = 8
N_KEYS = 128
N_EXPERTS = N_KEYS * N_KEYS
P_TOPK = 16
P_QDIM = 256

COL_WIDTHS = (A_Q_LORA, A_KV_LORA, IDX_DIM, IDX_HEADS, B_QK, B_QK, B_QK, B_QK, B_HEADS, B_HEADS)
MIX_WIDTH = A_HEADS * A_HEAD_DIM + B_HEADS * B_HEAD_DIM


def t5_bucket(rel):
    half = REL_BUCKETS // 2
    max_exact = half // 2
    n = jnp.abs(rel)
    nf = jnp.maximum(n, max_exact).astype(jnp.float32)
    large = max_exact + (jnp.log(nf / max_exact) / math.log(REL_MAX_DIST / max_exact)
                         * (half - max_exact)).astype(jnp.int32)
    large = jnp.minimum(large, half - 1)
    return jnp.where(rel > 0, half, 0) + jnp.where(n < max_exact, n, large)


INT_MIN = -2147483648
NEG_BIG = -1e30
KEY_TILE = 512
KEY_PAD = KEY_TILE - Q_BLOCK
VMEM_LIMIT = 56 * 1024 * 1024


def _rms(x, g):
    return x * lax.rsqrt(jnp.mean(x * x, axis=-1, keepdims=True) + EPS) * g


def _nt_dot(a, b):
    return lax.dot_general(a, b, (((1,), (1,)), ((), ())), preferred_element_type=jnp.float32)


def _dsa_prep_kernel(cq_ref, ckv_ref, kw_ref, gcq_ref, gckv_ref, gk_ref,
                       wiqt_ref, wuqt_ref, wuk_ref,
                       at_ref, qlatt_ref, ckvn_ref, kidxn_ref):
    f32, bf16 = jnp.float32, jnp.bfloat16
    Q = Q_BLOCK
    kw = kw_ref[0]
    cqn = _rms(cq_ref[0], gcq_ref[...]).astype(bf16)
    qit = _nt_dot(wiqt_ref[...], cqn)
    qt = _nt_dot(wuqt_ref[...], cqn)
    for h in range(A_HEADS):
        cols = slice(h * Q, (h + 1) * Q)
        at_ref[0, 0, :, cols] = qit[h * IDX_DIM:(h + 1) * IDX_DIM, :].astype(bf16)
        qh = qt[h * A_HEAD_DIM:(h + 1) * A_HEAD_DIM, :].astype(bf16)
        ql = jnp.dot(wuk_ref[h], qh, preferred_element_type=f32) * ATTN_SCALE
        qlatt_ref[0, 0, :, cols] = ql.astype(bf16)
    ckvn_ref[0] = _rms(ckv_ref[0], gckv_ref[...]).astype(bf16)
    kidxn_ref[0] = _rms(kw[:, :IDX_DIM], gk_ref[...]).astype(bf16)


def _dsa_main_kernel(at_ref, qlatt_ref, wt_ref, kidx_ref, ckv_ref, bucket_ref, far_ref, rb_ref,
                     wuvt_ref, o_ref, bias0_ref, sc_ref, big_ref, acc_ref, m_ref, l_ref, *, topk):
    f32, i32, bf16 = jnp.float32, jnp.int32, jnp.bfloat16
    Q, T = Q_BLOCK, KEY_TILE
    i = pl.program_id(1)

    @pl.when(jnp.logical_and(pl.program_id(0) == 0, i == 0))
    def _():
        bucket = bucket_ref[...]
        for h in range(A_HEADS):
            far_bias = rb_ref[far_ref[0], h]
            bias = jnp.zeros((T, Q), f32)
            for b in range(REL_BUCKETS):
                bias = jnp.where(bucket == b, rb_ref[b, h] - far_bias, bias)
            bias0_ref[h] = bias
    e = (i + 1) * Q
    nt = (i + T // Q) // (T // Q)
    kf = jnp.float32(topk)
    wsc = wt_ref[0] * IDX_SCALE
    sub = lax.broadcasted_iota(i32, (T, Q), 0)
    qlane = lax.broadcasted_iota(i32, (T, Q), 1)
    limit = ((i * Q + qlane) // CHUNK + 1) * CHUNK
    at = at_ref[0, 0]

    def tile_start(j):
        return pl.multiple_of(e + KEY_PAD - (j + 1) * T, 128)

    def key_pos(j):
        return tile_start(j) - KEY_PAD + sub

    def score_tile(j, carry):
        kid = kidx_ref[0, pl.ds(tile_start(j), T), :]
        big_ref[...] = jnp.dot(kid, at, preferred_element_type=f32)
        s = jnp.zeros((T, Q), f32)
        for h in range(IDX_HEADS):
            s = s + wsc[h:h + 1, :] * jnp.maximum(big_ref[:, h * Q:(h + 1) * Q], 0.0)
        bits = lax.bitcast_convert_type(s, i32)
        key = jnp.where(bits < 0, bits ^ jnp.int32(0x7FFFFFFF), bits)
        key = jnp.where(s == 0.0, 0, key)
        kpos = key_pos(j)
        key = jnp.where(kpos >= 0, key, INT_MIN)
        sc_ref[j] = jnp.where(kpos < limit, key, INT_MIN)
        return carry

    lax.fori_loop(0, nt, score_tile, 0)

    def count_where(pred):
        def body(j, cnt):
            hit = jnp.where(pred(sc_ref[j], key_pos(j)), 1.0, 0.0)
            return cnt + jnp.sum(hit.reshape(T // 64, 64, Q), axis=0)
        cnt = lax.fori_loop(0, nt, body, jnp.zeros((64, Q), f32))
        return jnp.sum(cnt, axis=0, keepdims=True)

    def bit_body(b, carry):
        u, cacc = carry
        cand = u | lax.shift_left(jnp.int32(1), 31 - b)
        tvec = cand ^ jnp.int32(INT_MIN)
        tot = count_where(lambda k, kpos: k >= tvec)
        ok = tot >= kf
        return jnp.where(ok, cand, u), jnp.where(ok, tot, cacc)

    u, cacc = lax.fori_loop(0, 32, bit_body, (jnp.zeros((1, Q), i32), jnp.zeros((1, Q), f32)))
    thr = jnp.maximum(u ^ jnp.int32(INT_MIN), INT_MIN + 1)
    overflow = jnp.where(u != 0, cacc, 0.0) > kf
    n_over = jnp.max(jnp.where(overflow, 1.0, 0.0), axis=1, keepdims=True)[0, 0]

    @pl.when(n_over > 0.0)
    def _():
        need = kf - count_where(lambda k, kpos: k > thr)

        def cut_body(b, cut):
            cand = cut | lax.shift_left(jnp.int32(1), 14 - b)
            cnt = count_where(lambda k, kpos: jnp.where(k == thr, kpos, cand) < cand)
            return jnp.where(cnt <= need, cand, cut)

        cut = lax.fori_loop(0, 15, cut_body, jnp.zeros((1, Q), i32))

        def drop_tile(j, carry):
            k = sc_ref[j]
            drop = jnp.where(k == thr, key_pos(j), -1) >= cut
            sc_ref[j] = jnp.where(drop, INT_MIN, k)
            return carry

        lax.fori_loop(0, nt, drop_tile, 0)

    m_ref[...] = jnp.full(m_ref.shape, NEG_BIG, f32)
    l_ref[...] = jnp.zeros(l_ref.shape, f32)
    acc_ref[...] = jnp.zeros(acc_ref.shape, f32)
    qlatt = qlatt_ref[0, 0]

    def attn_tile(j, with_bias):
        kv = ckv_ref[0, pl.ds(tile_start(j), T), :]
        kvt = kv.T
        mask_add = jnp.where(sc_ref[j] >= thr, 0.0, NEG_BIG)
        pair_cols = [slice(g * 2 * Q, (g + 1) * 2 * Q) for g in range(A_HEADS // 2)]
        logits = lambda g: jnp.dot(kv, qlatt[:, pair_cols[g]], preferred_element_type=f32)
        x_next = logits(0)
        for g in range(A_HEADS // 2):
            x_pair = x_next
            if g + 1 < A_HEADS // 2:
                x_next = logits(g + 1)
            ps, alphas = [], []
            for u in range(2):
                h = 2 * g + u
                cols = slice(h * Q, (h + 1) * Q)
                x = x_pair[:, u * Q:(u + 1) * Q] + mask_add
                if with_bias:
                    x = x + bias0_ref[h]
                m_prev = m_ref[:, cols]
                m_new = jnp.maximum(m_prev, jnp.max(x, axis=0, keepdims=True))
                p = jnp.exp(x - m_new)
                alpha = jnp.exp(m_prev - m_new)
                l_ref[:, cols] = alpha * l_ref[:, cols] + jnp.sum(p, axis=0, keepdims=True)
                m_ref[:, cols] = m_new
                ps.append(p.astype(bf16))
                alphas.append(alpha)
            pv = jnp.dot(kvt, jnp.concatenate(ps, axis=1), preferred_element_type=f32)
            acc_ref[:, pair_cols[g]] = (
                jnp.concatenate(alphas, axis=1) * acc_ref[:, pair_cols[g]] + pv)

    attn_tile(0, True)

    def attn_body(j, carry):
        attn_tile(j, False)
        return carry

    lax.fori_loop(1, nt, attn_body, 0)

    inv_l = 1.0 / l_ref[...]
    outs = []
    for h in range(A_HEADS):
        cols = slice(h * Q, (h + 1) * Q)
        o_lat_t = (acc_ref[:, cols] * inv_l[:, cols]).astype(bf16)
        outs.append(jnp.dot(wuvt_ref[h], o_lat_t, preferred_element_type=f32))
    o_ref[0] = jnp.concatenate(outs, axis=0).T


def dsa_pallas(cq, ckv, kw, widx_t, g_cq, g_ckv, g_kidx, w_uq, w_iq, w_uk, w_uv, rel_bias):
    bsz, seq, _ = cq.shape
    f32, bf16 = jnp.float32, jnp.bfloat16
    Q, T, H = Q_BLOCK, KEY_TILE, A_HEADS
    nblk = seq // Q
    topk = min(IDX_TOPK_MAX, seq // 4)
    wiqt = w_iq.reshape(A_Q_LORA, IDX_HEADS * IDX_DIM).T.astype(bf16)
    wuqt = w_uq.reshape(A_Q_LORA, H * A_HEAD_DIM).T.astype(bf16)
    wuk = jnp.transpose(w_uk, (1, 0, 2)).astype(bf16)
    wuvt = jnp.transpose(w_uv, (1, 2, 0)).astype(bf16)
    tok = lambda w: pl.BlockSpec((1, Q, w), lambda b, t: (b, t, 0))
    full = lambda shape: pl.BlockSpec(shape, lambda b, t: (0,) * len(shape))
    blk = lambda r: pl.BlockSpec((1, 1, r, H * Q), lambda b, t: (b, t, 0, 0))
    a_t, qlat_t, ckvn, kidxn = pl.pallas_call(
        _dsa_prep_kernel,
        grid=(bsz, nblk),
        in_specs=[tok(A_Q_LORA), tok(A_KV_LORA), tok(128),
                  full((1, A_Q_LORA)), full((1, A_KV_LORA)), full((1, IDX_DIM)),
                  full(wiqt.shape), full(wuqt.shape), full(wuk.shape)],
        out_specs=[blk(IDX_DIM), blk(A_KV_LORA), tok(A_KV_LORA), tok(IDX_DIM)],
        out_shape=[jax.ShapeDtypeStruct((bsz, nblk, IDX_DIM, H * Q), bf16),
                   jax.ShapeDtypeStruct((bsz, nblk, A_KV_LORA, H * Q), bf16),
                   jax.ShapeDtypeStruct((bsz, seq, A_KV_LORA), bf16),
                   jax.ShapeDtypeStruct((bsz, seq, IDX_DIM), bf16)],
        name="dsa_prep",
    )(cq, ckv, kw, g_cq.reshape(1, -1), g_ckv.reshape(1, -1), g_kidx.reshape(1, -1),
      wiqt, wuqt, wuk)
    ckvp = jnp.pad(ckvn, ((0, 0), (KEY_PAD, 0), (0, 0)))
    kidxp = jnp.pad(kidxn, ((0, 0), (KEY_PAD, 0), (0, 0)))
    rel = (jnp.arange(T, dtype=jnp.int32)[:, None] - jnp.arange(Q, dtype=jnp.int32)[None, :]
           - KEY_PAD)
    bucket0 = t5_bucket(rel)
    bucket_far = t5_bucket(jnp.full((1,), -KEY_PAD - 1, jnp.int32))
    smem = pl.BlockSpec(memory_space=pltpu.SMEM)
    nt_max = (nblk - 1 + T // Q) // (T // Q)
    skey = seq + KEY_PAD
    return pl.pallas_call(
        partial(_dsa_main_kernel, topk=topk),
        grid=(bsz, nblk),
        in_specs=[blk(IDX_DIM), blk(A_KV_LORA),
                  pl.BlockSpec((1, IDX_HEADS, Q), lambda b, t: (b, 1, t)),
                  pl.BlockSpec((1, skey, IDX_DIM), lambda b, t: (b, 0, 0)),
                  pl.BlockSpec((1, skey, A_KV_LORA), lambda b, t: (b, 0, 0)),
                  full((T, Q)), smem, smem, full(wuvt.shape)],
        out_specs=tok(H * A_HEAD_DIM),
        out_shape=jax.ShapeDtypeStruct((bsz, seq, H * A_HEAD_DIM), f32),
        scratch_shapes=[pltpu.VMEM((H, T, Q), f32),
                        pltpu.VMEM((nt_max, T, Q), jnp.int32),
                        pltpu.VMEM((T, H * Q), f32),
                        pltpu.VMEM((A_KV_LORA, H * Q), f32),
                        pltpu.VMEM((1, H * Q), f32),
                        pltpu.VMEM((1, H * Q), f32)],
        compiler_params=pltpu.CompilerParams(
            dimension_semantics=("arbitrary", "arbitrary"), vmem_limit_bytes=VMEM_LIMIT),
        name="dsa_main",
    )(a_t, qlat_t, widx_t, kidxp, ckvp, bucket0, bucket_far, rel_bias.astype(f32), wuvt)


PEER_SCORE_TOKENS = 256
PEER_GATHER_TOKENS = 128
PEER_SLOTS = P_HEADS * P_TOPK
WORDS_PER_ROW = 4
PEER_TILES = 32
PEER_GROUP = 2


def _top16(s, order=None, payload=None):
    if order is None:
        order = lax.broadcasted_iota(jnp.int32, s.shape, 0).astype(jnp.float32)
    vals, picks = [], []
    for _ in range(P_TOPK):
        m = jnp.max(s, axis=0, keepdims=True)
        pos = jnp.min(jnp.where(s == m, order, float(N_EXPERTS)), axis=0, keepdims=True)
        hit = order == pos
        vals.append(m)
        if payload is None:
            picks.append(pos)
        else:
            picks.append(jnp.max(jnp.where(hit, payload, -1.0), axis=0, keepdims=True))
        s = jnp.where(hit, -jnp.inf, s)
    return vals, picks


def _staircase_candidates(v1, i1, v2, i2):
    v1m, i1m = jnp.concatenate(v1, axis=0), jnp.concatenate(i1, axis=0)
    v2m, i2m = jnp.concatenate(v2, axis=0), jnp.concatenate(i2, axis=0)
    t = v1m.shape[1]
    sub8 = lax.broadcasted_iota(jnp.int32, (8, t), 0).astype(jnp.float32)
    cand, cidx, rank = [], [], []
    for a in range(8):
        cand.append(v1[a] + v2m[:8])
        cidx.append(i1[a] * float(N_KEYS) + i2m[:8])
        rank.append(sub8 + float(a * P_TOPK))
    cand.append(v1[0] + v2m[8:])
    cidx.append(i1[0] * float(N_KEYS) + i2m[8:])
    rank.append(sub8 + 8.0)
    cand.append(v1m[8:] + v2[0])
    cidx.append(i1m[8:] * float(N_KEYS) + i2[0])
    rank.append((sub8 + 8.0) * float(P_TOPK))
    cat = lambda xs: jnp.concatenate(xs, axis=0)
    return cat(cand), cat(rank), cat(cidx)


def _peer_score_kernel(x_ref, g_ref, wpqt_ref, sk_ref, hn_ref, eidx_ref, gate_ref):
    f32, bf16 = jnp.float32, jnp.bfloat16
    h = _rms(x_ref[...], g_ref[...])
    hb = h.astype(bf16)
    for s in range(h.shape[1] // 128):
        hn_ref[pl.ds(s, h.shape[0], stride=8), :] = h[:, s * 128:(s + 1) * 128]
    qrt = _nt_dot(wpqt_ref[...], hb)
    half = P_QDIM // 2
    e_rows, g_rows = [], []
    for hd in range(P_HEADS):
        tops = []
        for p in range(2):
            qhp = qrt[(hd * 2 + p) * half:(hd * 2 + p + 1) * half, :].astype(bf16)
            s = jnp.dot(sk_ref[hd * 2 + p], qhp, preferred_element_type=f32)
            tops.append(_top16(s))
        (v1, i1), (v2, i2) = tops
        best, be = _top16(*_staircase_candidates(v1, i1, v2, i2))
        ex = [jnp.exp(b - best[0]) for b in best]
        den = ex[0]
        for k in range(1, P_TOPK):
            den = den + ex[k]
        inv = 1.0 / den
        e_rows += be
        g_rows += [x * inv for x in ex]
    eidx_ref[...] = (jnp.concatenate(e_rows, axis=0).T * float(WORDS_PER_ROW)).astype(jnp.int32)
    gate_ref[...] = jnp.concatenate(g_rows, axis=0).T


def _diag_mask():
    r = lax.broadcasted_iota(jnp.int32, (8, PEER_SLOTS * 8), 0)
    c = lax.broadcasted_iota(jnp.int32, (8, PEER_SLOTS * 8), 1)
    return (c & 7) == r


def _gather_group(idx_ref, tab_ref, g_refs, t0):
    views = [idx_ref.at[t0 + u] for u in range(len(g_refs))]
    for r in range(PEER_SLOTS):
        for view, g_ref in zip(views, g_refs):
            row0 = pl.multiple_of(view[r], WORDS_PER_ROW)
            g_ref[r * WORDS_PER_ROW:(r + 1) * WORDS_PER_ROW, :] = tab_ref[
                pl.ds(row0, WORDS_PER_ROW), :]


def _gather_pipeline(idx_ref, tab_ref, tiles, n_tokens, consume):
    grp = PEER_GROUP
    groups = len(tiles) // grp
    for g_ref in tiles[-grp:]:
        g_ref[...] = jnp.zeros(g_ref.shape, g_ref.dtype)

    def drain(group_tiles, group):
        for u in range(grp):
            consume(group_tiles[u], group * (grp // 2) + u // 2, u % 2)

    def trip(i, carry):
        for k in range(groups):
            group = groups * i + k
            _gather_group(idx_ref, tab_ref, tiles[k * grp:(k + 1) * grp], group * grp)
            drain(tiles[(k - 1) * grp:k * grp] if k else tiles[-grp:], jnp.maximum(group - 1, 0))
        return carry

    lax.fori_loop(0, n_tokens // len(tiles), trip, 0)
    drain(tiles[-grp:], n_tokens // grp - 1)


def _peer_act_kernel(idx_ref, hn_ref, tab_ref, a_ref, m_ref, *tiles):
    f32, bf16 = jnp.float32, jnp.bfloat16
    tb = a_ref.shape[0]

    def dots(g_ref, tp, u):
        hp = hn_ref[pl.ds(pl.multiple_of(tp * 16, 16), 16), :].astype(bf16)
        m = _nt_dot(hp, pltpu.bitcast(g_ref[...], bf16))
        m_ref[pl.ds(pl.multiple_of((tp * 2 + u) * 8, 8), 8), :] = m[u * 8:(u + 1) * 8, :]

    _gather_pipeline(idx_ref, tab_ref, tiles, tb, dots)
    m3 = m_ref[...].reshape(tb, 8, PEER_SLOTS * 8)
    z = jnp.sum(jnp.where(_diag_mask()[None], m3, 0.0), axis=1)
    rr = lax.broadcasted_iota(jnp.int32, (PEER_SLOTS * 8, PEER_SLOTS), 0)
    cc = lax.broadcasted_iota(jnp.int32, (PEER_SLOTS * 8, PEER_SLOTS), 1)
    pool = jnp.where((rr >> 3) == cc, 1.0, 0.0).astype(bf16)
    z_hi = z.astype(bf16)
    z_lo = (z - z_hi.astype(f32)).astype(bf16)
    a_ref[...] = (jnp.dot(z_hi, pool, preferred_element_type=f32)
                  + jnp.dot(z_lo, pool, preferred_element_type=f32))


def _peer_out_kernel(idx_ref, a_ref, gate_ref, tab_ref, o_ref, w_ref, *tiles):
    f32, bf16 = jnp.float32, jnp.bfloat16
    tb = a_ref.shape[0]
    a = a_ref[...]
    act = 0.5 * a * (1.0 + lax.erf(a * (2.0 ** -0.5)))
    wgt = (gate_ref[...] * act).astype(bf16)
    rr = lax.broadcasted_iota(jnp.int32, (PEER_SLOTS, PEER_SLOTS * 8), 0)
    cc = lax.broadcasted_iota(jnp.int32, (PEER_SLOTS, PEER_SLOTS * 8), 1)
    expand = jnp.where((cc >> 3) == rr, 1.0, 0.0).astype(bf16)
    w_ref[...] = jnp.dot(wgt, expand, preferred_element_type=f32)
    diag = _diag_mask()

    def combine(g_ref, tp, u):
        t = tp * 2 + u
        wrow = jnp.broadcast_to(w_ref[pl.ds(t, 1), :], (8, PEER_SLOTS * 8))
        wsel = jnp.where(diag, wrow, 0.0).astype(bf16)
        o_ref[pl.ds(pl.multiple_of(t * 8, 8), 8), :] = jnp.dot(
            wsel, pltpu.bitcast(g_ref[...], bf16), preferred_element_type=f32)

    _gather_pipeline(idx_ref, tab_ref, tiles, tb, combine)


PACK_ROWS = 512


def _pack_table_kernel(x_ref, o_ref):
    rows = x_ref.shape[0]
    bf16_bits = lambda v: lax.bitcast_convert_type(
        v.astype(jnp.bfloat16).astype(jnp.float32), jnp.int32)
    for c in range(WORDS_PER_ROW):
        lo = bf16_bits(x_ref[:, (2 * c) * 128:(2 * c + 1) * 128])
        hi = bf16_bits(x_ref[:, (2 * c + 1) * 128:(2 * c + 2) * 128])
        word = (hi & jnp.int32(-65536)) | lax.shift_right_logical(lo, 16)
        o_ref[pl.ds(c, rows, stride=WORDS_PER_ROW), :] = word


def _pack_table(tab):
    n_e, d = tab.shape
    return pl.pallas_call(
        _pack_table_kernel,
        grid=(n_e // PACK_ROWS,),
        in_specs=[pl.BlockSpec((PACK_ROWS, d), lambda i: (i, 0))],
        out_specs=pl.BlockSpec((PACK_ROWS * WORDS_PER_ROW, 128), lambda i: (i, 0)),
        out_shape=jax.ShapeDtypeStruct((n_e * WORDS_PER_ROW, 128), jnp.int32),
        name="pack_table",
    )(tab)


def peer_pallas(x, g_ffn, w_pq, sub_keys, u_emb, v_emb):
    n_tok, d = x.shape
    f32, bf16 = jnp.float32, jnp.bfloat16
    ts, tg = PEER_SCORE_TOKENS, PEER_GATHER_TOKENS
    wpqt = w_pq.reshape(d, P_HEADS * P_QDIM).T.astype(bf16)
    sk = sub_keys.reshape(P_HEADS * 2, N_KEYS, P_QDIM // 2).astype(bf16)
    hn, eidx, gate = pl.pallas_call(
        _peer_score_kernel,
        grid=(n_tok // ts,),
        in_specs=[pl.BlockSpec((ts, d), lambda i: (i, 0)),
                  pl.BlockSpec((1, d), lambda i: (0, 0)),
                  pl.BlockSpec(wpqt.shape, lambda i: (0, 0)),
                  pl.BlockSpec(sk.shape, lambda i: (0, 0, 0))],
        out_specs=[pl.BlockSpec((ts * 8, 128), lambda i: (i, 0)),
                   pl.BlockSpec((ts, PEER_SLOTS), lambda i: (i, 0)),
                   pl.BlockSpec((ts, PEER_SLOTS), lambda i: (i, 0))],
        out_shape=[jax.ShapeDtypeStruct((n_tok * 8, 128), f32),
                   jax.ShapeDtypeStruct((n_tok, PEER_SLOTS), jnp.int32),
                   jax.ShapeDtypeStruct((n_tok, PEER_SLOTS), f32)],
        compiler_params=pltpu.CompilerParams(vmem_limit_bytes=VMEM_LIMIT),
        name="peer_score",
    )(x, g_ffn.reshape(1, d), wpqt, sk)
    utab, vtab = _pack_table(u_emb), _pack_table(v_emb)
    idx_spec = pl.BlockSpec((tg, PEER_SLOTS), lambda i: (i, 0), memory_space=pltpu.SMEM)
    tab_spec = pl.BlockSpec(memory_space=pltpu.VMEM)
    slot_spec = pl.BlockSpec((tg, PEER_SLOTS), lambda i: (i, 0))
    row_spec = pl.BlockSpec((tg * 8, 128), lambda i: (i, 0))
    gbuf = pltpu.VMEM((PEER_SLOTS * WORDS_PER_ROW, 128), jnp.int32)
    cparams = pltpu.CompilerParams(vmem_limit_bytes=VMEM_LIMIT)
    act = pl.pallas_call(
        _peer_act_kernel,
        grid=(n_tok // tg,),
        in_specs=[idx_spec, row_spec, tab_spec],
        out_specs=slot_spec,
        out_shape=jax.ShapeDtypeStruct((n_tok, PEER_SLOTS), f32),
        scratch_shapes=[pltpu.VMEM((tg * 8, PEER_SLOTS * 8), f32)] + [gbuf] * PEER_TILES,
        compiler_params=cparams,
        name="peer_act",
    )(eidx, hn, utab)
    out = pl.pallas_call(
        _peer_out_kernel,
        grid=(n_tok // tg,),
        in_specs=[idx_spec, slot_spec, slot_spec, tab_spec],
        out_specs=row_spec,
        out_shape=jax.ShapeDtypeStruct((n_tok * 8, 128), f32),
        scratch_shapes=[pltpu.VMEM((tg, PEER_SLOTS * 8), f32)] + [gbuf] * PEER_TILES,
        compiler_params=cparams,
        name="peer_out",
    )(eidx, act, gate, vtab)
    return out


def _token_rows(y_ref):
    tm = y_ref.shape[0] // 8
    return jnp.concatenate([y_ref[pl.ds(s, tm, stride=8), :] for s in range(8)], axis=1)


def _final_kernel(x_ref, y_ref, g_ref, o_ref):
    o_ref[...] = _rms(x_ref[...] + _token_rows(y_ref), g_ref[...])


def _residual_kernel(x_ref, y_ref, o_ref):
    o_ref[...] = x_ref[...] + _token_rows(y_ref)


def residual_pallas(x, y8, g=None):
    n, d = x.shape
    tm = 512
    row = pl.BlockSpec((tm, d), lambda i: (i, 0))
    row8 = pl.BlockSpec((tm * 8, 128), lambda i: (i, 0))
    if g is None:
        kern, extra, extra_specs, name = _residual_kernel, (), [], "residual"
    else:
        kern, extra, name = _final_kernel, (g.reshape(1, d),), "final_rmsnorm"
        extra_specs = [pl.BlockSpec((1, d), lambda i: (0, 0))]
    return pl.pallas_call(
        kern,
        grid=(n // tm,),
        in_specs=[row, row8] + extra_specs,
        out_specs=row,
        out_shape=jax.ShapeDtypeStruct((n, d), x.dtype),
        name=name,
    )(x, y8, *extra)


IN_PROJ_TOKENS = 512
GDN_COLS = 3 * B_QK
GATE_ROWS = 8


def _in_proj_kernel(x_ref, g_ref, wa_ref, wkw_ref, wqkv_ref, wz_ref, wab_ref, wabt_ref,
                    cq_ref, ckv_ref, kw_ref, qkv_ref, z_ref, ab_ref, abt_ref):
    f32 = jnp.float32
    hb = _rms(x_ref[0], g_ref[...]).astype(jnp.bfloat16)
    a = jnp.dot(hb, wa_ref[...], preferred_element_type=f32)
    cq_ref[0] = a[:, :A_Q_LORA]
    ckv_ref[0] = a[:, A_Q_LORA:]
    kw_ref[0] = jnp.dot(hb, wkw_ref[...], preferred_element_type=f32)
    qkv_ref[0] = jnp.dot(hb, wqkv_ref[...], preferred_element_type=f32)
    z_ref[0] = jnp.dot(hb, wz_ref[...], preferred_element_type=f32)
    ab_ref[0] = jnp.dot(hb, wab_ref[...], preferred_element_type=f32)
    abt_ref[0] = _nt_dot(wabt_ref[...], hb)


def in_proj_pallas(x, g_mix, w_in):
    bsz, seq, d = x.shape
    f32, bf16 = jnp.float32, jnp.bfloat16
    tm = min(IN_PROJ_TOKENS, seq)
    o = np.cumsum((0,) + COL_WIDTHS)
    wb = w_in.astype(bf16)
    pad_cols = lambda w: jnp.pad(w, ((0, 0), (0, 128 - w.shape[1])))
    wa = wb[:, o[0]:o[2]]
    wkw = pad_cols(wb[:, o[2]:o[4]])
    wqkv = wb[:, o[4]:o[7]]
    wz = wb[:, o[7]:o[8]]
    wab = pad_cols(wb[:, o[8]:o[10]])
    wabt = jnp.concatenate([wb[:, o[8]:o[10]], wb[:, o[3]:o[4]]], axis=1).T
    full = lambda w: pl.BlockSpec(w.shape, lambda b, t: (0, 0))
    tok = lambda w: pl.BlockSpec((1, tm, w), lambda b, t: (b, t, 0))
    shp = lambda w: jax.ShapeDtypeStruct((bsz, seq, w), f32)
    return pl.pallas_call(
        _in_proj_kernel,
        grid=(bsz, seq // tm),
        in_specs=[tok(d), pl.BlockSpec((1, d), lambda b, t: (0, 0)),
                  full(wa), full(wkw), full(wqkv), full(wz), full(wab), full(wabt)],
        out_specs=[tok(A_Q_LORA), tok(A_KV_LORA), tok(128), tok(GDN_COLS), tok(B_QK), tok(128),
                   pl.BlockSpec((1, GATE_ROWS + IDX_HEADS, tm), lambda b, t: (b, 0, t))],
        out_shape=[shp(A_Q_LORA), shp(A_KV_LORA), shp(128), shp(GDN_COLS), shp(B_QK), shp(128),
                   jax.ShapeDtypeStruct((bsz, GATE_ROWS + IDX_HEADS, seq), f32)],
        compiler_params=pltpu.CompilerParams(vmem_limit_bytes=VMEM_LIMIT),
        name="in_proj",
    )(x, g_mix.reshape(1, d), wa, wkw, wqkv, wz, wab, wabt)


def _softplus(x):
    return jnp.maximum(x, 0.0) + jnp.log1p(jnp.exp(-jnp.abs(x)))


def _sigmoid(x):
    return 1.0 / (1.0 + jnp.exp(-x))


def _gdn_gates(pre, a_log, dt_bias, is_decay):
    g = -jnp.exp(a_log) * _softplus(pre + dt_bias)
    return jnp.where(is_decay, g, _sigmoid(pre))


def _gdn_prep_kernel(qkv_ref, halo_ref, cw_ref, ab_ref, abt_ref, alc_ref, dtc_ref, alr_ref, dtr_ref,
                     q_ref, k_ref, v_ref, gc_ref, gr_ref):
    tm = qkv_ref.shape[1]
    x = qkv_ref[0]
    halo = jnp.where(pl.program_id(1) > 0, halo_ref[0], 0.0)
    full = jnp.concatenate([halo, x], axis=0)
    y = x * cw_ref[CONV_WIDTH - 1:CONV_WIDTH, :]
    for back in range(1, CONV_WIDTH):
        shifted = pltpu.roll(full, back, axis=0)[8:, :]
        y = y + shifted * cw_ref[CONV_WIDTH - 1 - back:CONV_WIDTH - back, :]
    y = y * _sigmoid(y)
    for h in range(B_HEADS):
        cols = slice(h * B_HEAD_DIM, (h + 1) * B_HEAD_DIM)
        qh = y[:, h * B_HEAD_DIM:(h + 1) * B_HEAD_DIM]
        kh = y[:, B_QK + h * B_HEAD_DIM:B_QK + (h + 1) * B_HEAD_DIM]
        q_ref[0, :, cols] = qh * lax.rsqrt(
            jnp.sum(qh * qh, axis=-1, keepdims=True) + EPS) * (B_HEAD_DIM ** -0.5)
        k_ref[0, :, cols] = kh * lax.rsqrt(jnp.sum(kh * kh, axis=-1, keepdims=True) + EPS)
    v_ref[0] = y[:, 2 * B_QK:]
    lane = lax.broadcasted_iota(jnp.int32, (tm, 128), 1)
    gates_c = _gdn_gates(ab_ref[0], alc_ref[...], dtc_ref[...], lane < B_HEADS)
    row = lax.broadcasted_iota(jnp.int32, (GATE_ROWS, tm), 0)
    gates_r = _gdn_gates(abt_ref[0], alr_ref[...], dtr_ref[...], row < B_HEADS)
    ti = lax.broadcasted_iota(jnp.int32, (tm, tm), 0)
    tj = lax.broadcasted_iota(jnp.int32, (tm, tm), 1)
    same_chunk = (ti // CHUNK) == (tj // CHUNK)
    hi = lax.Precision.HIGHEST
    lower = jnp.where(same_chunk & (tj <= ti), 1.0, 0.0)
    upper = jnp.where(same_chunk & (ti <= tj), 1.0, 0.0)
    cum_c = jnp.dot(lower, gates_c, preferred_element_type=jnp.float32, precision=hi)
    cum_r = jnp.dot(gates_r, upper, preferred_element_type=jnp.float32, precision=hi)
    gc_ref[0] = jnp.where(lane < B_HEADS, cum_c, gates_c)
    gr_ref[0] = jnp.where(row < B_HEADS, cum_r, gates_r)


def _gdn_main_kernel(q_ref, k_ref, v_ref, z_ref, gc_ref, gr_ref, gon_ref, o_ref, s_ref):
    f32, bf16 = jnp.float32, jnp.bfloat16
    C, Dh = CHUNK, B_HEAD_DIM

    @pl.when(pl.program_id(1) == 0)
    def _():
        s_ref[...] = jnp.zeros(s_ref.shape, f32)

    ii = lax.broadcasted_iota(jnp.int32, (C, C), 0)
    jj = lax.broadcasted_iota(jnp.int32, (C, C), 1)
    causal = ii >= jj
    strict = ii > jj
    eye = jnp.where(ii == jj, 1.0, 0.0)
    mm = lambda a, b: jnp.dot(a, b, preferred_element_type=f32)
    n_chunks = q_ref.shape[1] // C
    units = []
    for c in range(n_chunks):
        rows = slice(c * C, (c + 1) * C)
        gates_c = gc_ref[0, rows, :]
        gates_r = gr_ref[0, :, c * C:(c + 1) * C]
        for h in range(B_HEADS):
            cols = slice(h * Dh, (h + 1) * Dh)
            gcum = jnp.broadcast_to(gates_c[:, h:h + 1], (C, Dh))
            beta = jnp.broadcast_to(gates_c[:, B_HEADS + h:B_HEADS + h + 1], (C, Dh))
            gcum_r = jnp.broadcast_to(gates_r[h:h + 1, :], (C, C))
            decay = jnp.where(causal, jnp.exp(jnp.minimum(gcum[:, :C] - gcum_r, 0.0)), 0.0)
            q, k, v = q_ref[0, rows, cols], k_ref[0, rows, cols], v_ref[0, rows, cols]
            qb, kb = q.astype(bf16), k.astype(bf16)
            kk = _nt_dot(kb, kb)
            qk = _nt_dot(qb, kb)
            egc = jnp.exp(gcum)
            g_last = gcum[C - 1:C, :]
            units.append(dict(
                c=c, h=h, rows=rows, cols=cols,
                neg_m=jnp.where(strict, -(beta[:, :C] * kk * decay), 0.0),
                rhs=jnp.concatenate([v * beta, k * (beta * egc)], axis=1),
                q_dec=(q * egc).astype(bf16), intra=(qk * decay).astype(bf16),
                k_tail=(k * jnp.exp(g_last - gcum)).astype(bf16), chunk_decay=jnp.exp(g_last)))
    def halves(a):
        a_h = a.astype(bf16)
        return a_h, (a - a_h.astype(f32)).astype(bf16)

    def mm3(a, b):
        (a_h, a_l), (b_h, b_l) = a, b
        return mm(a_h, b_h) + (mm(a_h, b_l) + mm(a_l, b_h))

    t_inv = [eye + un["neg_m"] for un in units]
    p_halves = [halves(un["neg_m"]) for un in units]
    for _ in range(5):
        p_halves = [halves(mm3(ph, ph)) for ph in p_halves]
        t_inv = [t + mm3(ph, halves(t)) for ph, t in zip(p_halves, t_inv)]
    sols = [mm3(halves(t), halves(un["rhs"])) for t, un in zip(t_inv, units)]
    states = [s_ref[h] for h in range(B_HEADS)]
    for c in range(n_chunks):
        group = [(un, sol) for un, sol in zip(units, sols) if un["c"] == c]
        sbs = [states[un["h"]].astype(bf16) for un, _ in group]
        vbs = [(sol[:, :Dh] - mm(sol[:, Dh:].astype(bf16), sb)).astype(bf16)
               for (un, sol), sb in zip(group, sbs)]
        outs = [mm(un["q_dec"], sb) + mm(un["intra"], vb)
                for (un, _), sb, vb in zip(group, sbs, vbs)]
        for (un, _), vb in zip(group, vbs):
            states[un["h"]] = states[un["h"]] * un["chunk_decay"] + lax.dot_general(
                un["k_tail"], vb, (((0,), (0,)), ((), ())), preferred_element_type=f32)
        for (un, _), o in zip(group, outs):
            o = o * lax.rsqrt(jnp.mean(o * o, axis=-1, keepdims=True) + EPS) * gon_ref[...]
            zz = z_ref[0, un["rows"], un["cols"]]
            o_ref[0, un["rows"], un["cols"]] = o * (zz * _sigmoid(zz))
    for h in range(B_HEADS):
        s_ref[h] = states[h]


def gdn_pallas(qkv, z, ab, abt, conv_w, a_log, dt_bias, g_onorm):
    bsz, seq, _ = qkv.shape
    f32 = jnp.float32
    tm = min(256, seq)
    zero4 = jnp.zeros((B_HEADS,), f32)
    lane_row = lambda v: jnp.pad(jnp.concatenate([v.astype(f32), zero4]), (0, 120)).reshape(1, 128)
    sub_col = lambda v: jnp.concatenate([v.astype(f32), zero4]).reshape(GATE_ROWS, 1)
    tok = lambda w, t=tm: pl.BlockSpec((1, t, w), lambda b, i: (b, i, 0))
    const = lambda shape: pl.BlockSpec(shape, lambda b, i: (0,) * len(shape))
    shp = lambda w: jax.ShapeDtypeStruct((bsz, seq, w), f32)
    q, k, v, gc, gr = pl.pallas_call(
        _gdn_prep_kernel,
        grid=(bsz, seq // tm),
        in_specs=[tok(GDN_COLS),
                  pl.BlockSpec((1, 8, GDN_COLS), lambda b, i: (b, jnp.maximum(i * (tm // 8) - 1, 0), 0)),
                  const((CONV_WIDTH, GDN_COLS)), tok(128),
                  pl.BlockSpec((1, GATE_ROWS, tm), lambda b, i: (b, 0, i)),
                  const((1, 128)), const((1, 128)), const((GATE_ROWS, 1)), const((GATE_ROWS, 1))],
        out_specs=[tok(B_QK), tok(B_QK), tok(B_QK), tok(128),
                   pl.BlockSpec((1, GATE_ROWS, tm), lambda b, i: (b, 0, i))],
        out_shape=[shp(B_QK), shp(B_QK), shp(B_QK), shp(128),
                   jax.ShapeDtypeStruct((bsz, GATE_ROWS, seq), f32)],
        compiler_params=pltpu.CompilerParams(vmem_limit_bytes=VMEM_LIMIT),
        name="gdn_prep",
    )(qkv, qkv, conv_w.astype(f32), ab, abt, lane_row(a_log), lane_row(dt_bias),
      sub_col(a_log), sub_col(dt_bias))
    ts = 2 * CHUNK
    return pl.pallas_call(
        _gdn_main_kernel,
        grid=(bsz, seq // ts),
        in_specs=[tok(B_QK, ts), tok(B_QK, ts), tok(B_QK, ts), tok(B_QK, ts), tok(128, ts),
                  pl.BlockSpec((1, GATE_ROWS, ts), lambda b, i: (b, 0, i)),
                  const((1, B_HEAD_DIM))],
        out_specs=tok(B_QK, ts),
        out_shape=shp(B_QK),
        scratch_shapes=[pltpu.VMEM((B_HEADS, B_HEAD_DIM, B_HEAD_DIM), f32)],
        compiler_params=pltpu.CompilerParams(dimension_semantics=("arbitrary", "arbitrary")),
        name="gdn_main",
    )(q, k, v, z, gc, gr, g_onorm.astype(f32).reshape(1, B_HEAD_DIM))


def _mem_kv_kernel(mem_ref, g_ref, wk_ref, wv_ref, k_ref, v_ref):
    f32, bf16 = jnp.float32, jnp.bfloat16
    mn = _rms(mem_ref[0], g_ref[...]).astype(bf16)
    k = jnp.dot(mn, wk_ref[...], preferred_element_type=f32)
    v = jnp.dot(mn, wv_ref[...], preferred_element_type=f32)
    for h in range(X_HEADS):
        cols = slice(h * X_HEAD_DIM, (h + 1) * X_HEAD_DIM)
        k_ref[0, h] = k[:, cols].astype(bf16)
        v_ref[0, h] = v[:, cols].astype(bf16)


def _mid_kernel(x_ref, oa_ref, ob_ref, wo_ref, gx_ref, wq_ref, k_ref, v_ref, wox_ref, o_ref):
    f32, bf16 = jnp.float32, jnp.bfloat16
    na = oa_ref.shape[2]
    x1 = (x_ref[0]
          + jnp.dot(oa_ref[0].astype(bf16), wo_ref[:na, :], preferred_element_type=f32)
          + jnp.dot(ob_ref[0].astype(bf16), wo_ref[na:, :], preferred_element_type=f32))
    hq = _rms(x1, gx_ref[...]).astype(bf16)
    q = jnp.dot(hq, wq_ref[...], preferred_element_type=f32)
    heads = []
    for h in range(X_HEADS):
        qh = q[:, h * X_HEAD_DIM:(h + 1) * X_HEAD_DIM].astype(bf16)
        lg = _nt_dot(qh, k_ref[0, h]) * (X_HEAD_DIM ** -0.5)
        p = jnp.exp(lg - jnp.max(lg, axis=-1, keepdims=True))
        p = (p / jnp.sum(p, axis=-1, keepdims=True)).astype(bf16)
        heads.append(jnp.dot(p, v_ref[0, h], preferred_element_type=f32).astype(bf16))
    o = jnp.concatenate(heads, axis=1)
    o_ref[0] = x1 + jnp.dot(o, wox_ref[...], preferred_element_type=f32)


def mid_pallas(x, o_a, o_b, w_out, g_cross, mem, g_mem, wq, wk, wv, wo):
    bsz, seq, d = x.shape
    f32, bf16 = jnp.float32, jnp.bfloat16
    hx = X_HEADS * X_HEAD_DIM
    m_len = mem.shape[1]
    const2 = lambda shape: pl.BlockSpec(shape, lambda b: (0,) * len(shape))
    kv_spec = pl.BlockSpec((1, X_HEADS, m_len, X_HEAD_DIM), lambda b: (b, 0, 0, 0))
    kv_shape = jax.ShapeDtypeStruct((bsz, X_HEADS, m_len, X_HEAD_DIM), bf16)
    k, v = pl.pallas_call(
        _mem_kv_kernel,
        grid=(bsz,),
        in_specs=[pl.BlockSpec((1, m_len, d), lambda b: (b, 0, 0)), const2((1, d)),
                  const2((d, hx)), const2((d, hx))],
        out_specs=[kv_spec, kv_spec],
        out_shape=[kv_shape, kv_shape],
        name="mem_kv",
    )(mem, g_mem.reshape(1, d), wk.reshape(d, hx).astype(bf16), wv.reshape(d, hx).astype(bf16))
    tm = min(256, seq)
    tok = lambda w: pl.BlockSpec((1, tm, w), lambda b, t: (b, t, 0))
    const = lambda shape: pl.BlockSpec(shape, lambda b, t: (0,) * len(shape))
    kv_spec2 = pl.BlockSpec((1, X_HEADS, m_len, X_HEAD_DIM), lambda b, t: (b, 0, 0, 0))
    return pl.pallas_call(
        _mid_kernel,
        grid=(bsz, seq // tm),
        in_specs=[tok(d), tok(o_a.shape[2]), tok(o_b.shape[2]), const((MIX_WIDTH, d)),
                  const((1, d)), const((d, hx)), kv_spec2, kv_spec2, const((hx, d))],
        out_specs=tok(d),
        out_shape=jax.ShapeDtypeStruct((bsz, seq, d), f32),
        compiler_params=pltpu.CompilerParams(vmem_limit_bytes=VMEM_LIMIT),
        name="mid",
    )(x, o_a, o_b, w_out.astype(bf16), g_cross.reshape(1, d), wq.reshape(d, hx).astype(bf16),
      k, v, wo.reshape(hx, d).astype(bf16))


def kernel(x, mem, g_mix, w_in, g_cq, g_ckv, g_kidx, w_uq, w_iq, w_uk, w_uv, rel_bias, conv_w, A_log, dt_bias, g_onorm, w_out, g_cross, g_mem, wq_x, wk_x, wv_x, wo_x, g_ffn, w_pq, sub_keys, u_emb, v_emb, g_final):
    bsz, seq, d = x.shape
    for l in range(DEPTH):
        cq, ckv, kw, qkv, z, ab, abt = in_proj_pallas(x, g_mix[l], w_in[l])
        o_a = dsa_pallas(cq, ckv, kw, abt, g_cq[l], g_ckv[l], g_kidx[l],
                         w_uq[l], w_iq[l], w_uk[l], w_uv[l], rel_bias)
        o_b = gdn_pallas(qkv, z, ab, abt, conv_w[l], A_log[l], dt_bias[l], g_onorm[l])
        x = mid_pallas(x, o_a, o_b, w_out[l], g_cross[l], mem, g_mem[l],
                       wq_x[l], wk_x[l], wv_x[l], wo_x[l])
        xf = x.reshape(bsz * seq, d)
        y8 = peer_pallas(xf, g_ffn[l], w_pq[l], sub_keys[l], u_emb[l], v_emb[l])
        if l + 1 < DEPTH:
            x = residual_pallas(xf, y8).reshape(bsz, seq, d)
    return residual_pallas(xf, y8, g_final).reshape(bsz, seq, d)
```

```python
import math
from functools import partial
import jax
import jax.numpy as jnp
from jax import lax
import numpy as np
from jax.experimental import pallas as pl
from jax.experimental.pallas import tpu as pltpu

DEPTH = 1

CHUNK = 64
Q_BLOCK = 128
EPS = 1e-6

A_HEADS = 8
A_HEAD_DIM = 64
A_Q_LORA = 256
A_KV_LORA = 256
IDX_HEADS = 8
IDX_DIM = 64
IDX_TOPK_MAX = 256
ATTN_SCALE = A_HEAD_DIM ** -0.5
IDX_SCALE = (IDX_HEADS * IDX_DIM) ** -0.5

B_HEADS = 4
B_HEAD_DIM = 128
B_QK = B_HEADS * B_HEAD_DIM
CONV_WIDTH = 4

REL_BUCKETS = 32
REL_MAX_DIST = 128

X_HEADS = 4
X_HEAD_DIM = 128

P_HEADS = 8
N_KEYS = 128
N_EXPERTS = N_KEYS * N_KEYS
P_TOPK = 16
P_QDIM = 256

COL_WIDTHS = (A_Q_LORA, A_KV_LORA, IDX_DIM, IDX_HEADS, B_QK, B_QK, B_QK, B_QK, B_HEADS, B_HEADS)
MIX_WIDTH = A_HEADS * A_HEAD_DIM + B_HEADS * B_HEAD_DIM


def t5_bucket(rel):
    half = REL_BUCKETS // 2
    max_exact = half // 2
    n = jnp.abs(rel)
    nf = jnp.maximum(n, max_exact).astype(jnp.float32)
    large = max_exact + (jnp.log(nf / max_exact) / math.log(REL_MAX_DIST / max_exact)
                         * (half - max_exact)).astype(jnp.int32)
    large = jnp.minimum(large, half - 1)
    return jnp.where(rel > 0, half, 0) + jnp.where(n < max_exact, n, large)


INT_MIN = -2147483648
NEG_BIG = -1e30
KEY_TILE = 512
KEY_PAD = KEY_TILE - Q_BLOCK
VMEM_LIMIT = 56 * 1024 * 1024


def _rms(x, g):
    return x * lax.rsqrt(jnp.mean(x * x, axis=-1, keepdims=True) + EPS) * g


def _nt_dot(a, b):
    return lax.dot_general(a, b, (((1,), (1,)), ((), ())), preferred_element_type=jnp.float32)


def _dsa_prep_kernel(cq_ref, ckv_ref, kw_ref, gcq_ref, gckv_ref, gk_ref,
                     wiqt_ref, wuqt_ref, wuk_ref,
                     at_ref, qlatt_ref, ckvn_ref, kidxn_ref):
    f32, bf16 = jnp.float32, jnp.bfloat16
    Q = Q_BLOCK
    kw = kw_ref[0]
    cqn = _rms(cq_ref[0], gcq_ref[...]).astype(bf16)
    qit = _nt_dot(wiqt_ref[...], cqn)
    qt = _nt_dot(wuqt_ref[...], cqn)
    for h in range(A_HEADS):
        cols = slice(h * Q, (h + 1) * Q)
        at_ref[0, 0, :, cols] = qit[h * IDX_DIM:(h + 1) * IDX_DIM, :].astype(bf16)
        qh = qt[h * A_HEAD_DIM:(h + 1) * A_HEAD_DIM, :].astype(bf16)
        ql = jnp.dot(wuk_ref[h], qh, preferred_element_type=f32) * ATTN_SCALE
        qlatt_ref[0, 0, :, cols] = ql.astype(bf16)
    ckvn_ref[0] = _rms(ckv_ref[0], gckv_ref[...]).astype(bf16)
    kidxn_ref[0] = _rms(kw[:, :IDX_DIM], gk_ref[...]).astype(bf16)


def _dsa_main_kernel(at_ref, qlatt_ref, wt_ref, kidx_ref, ckv_ref, bucket_ref, far_ref, rb_ref,
                     wuvt_ref, o_ref, bias0_ref, sc_ref, big_ref, acc_ref, m_ref, l_ref, *, topk):
    f32, i32, bf16 = jnp.float32, jnp.int32, jnp.bfloat16
    Q, T = Q_BLOCK, KEY_TILE
    i = pl.program_id(1)

    @pl.when(jnp.logical_and(pl.program_id(0) == 0, i == 0))
    def _():
        bucket = bucket_ref[...]
        for h in range(A_HEADS):
            far_bias = rb_ref[far_ref[0], h]
            bias = jnp.zeros((T, Q), f32)
            for b in range(REL_BUCKETS):
                bias = jnp.where(bucket == b, rb_ref[b, h] - far_bias, bias)
            bias0_ref[h] = bias
    e = (i + 1) * Q
    nt = (i + T // Q) // (T // Q)
    kf = jnp.float32(topk)
    wsc = wt_ref[0] * IDX_SCALE
    sub = lax.broadcasted_iota(i32, (T, Q), 0)
    qlane = lax.broadcasted_iota(i32, (T, Q), 1)
    limit = ((i * Q + qlane) // CHUNK + 1) * CHUNK
    at = at_ref[0, 0]

    def tile_start(j):
        return pl.multiple_of(e + KEY_PAD - (j + 1) * T, 128)

    def key_pos(j):
        return tile_start(j) - KEY_PAD + sub

    def score_tile(j, carry):
        kid = kidx_ref[0, pl.ds(tile_start(j), T), :]
        big_ref[...] = jnp.dot(kid, at, preferred_element_type=f32)
        s = jnp.zeros((T, Q), f32)
        for h in range(IDX_HEADS):
            s = s + wsc[h:h + 1, :] * jnp.maximum(big_ref[:, h * Q:(h + 1) * Q], 0.0)
        bits = lax.bitcast_convert_type(s, i32)
        key = jnp.where(bits < 0, bits ^ jnp.int32(0x7FFFFFFF), bits)
        key = jnp.where(s == 0.0, 0, key)
        kpos = key_pos(j)
        key = jnp.where(kpos >= 0, key, INT_MIN)
        sc_ref[j] = jnp.where(kpos < limit, key, INT_MIN)
        return carry

    lax.fori_loop(0, nt, score_tile, 0)

    def count_where(pred):
        def body(j, cnt):
            hit = jnp.where(pred(sc_ref[j], key_pos(j)), 1.0, 0.0)
            return cnt + jnp.sum(hit.reshape(T // 64, 64, Q), axis=0)
        cnt = lax.fori_loop(0, nt, body, jnp.zeros((64, Q), f32))
        return jnp.sum(cnt, axis=0, keepdims=True)

    def bit_body(b, carry):
        u, cacc = carry
        cand = u | lax.shift_left(jnp.int32(1), 31 - b)
        tvec = cand ^ jnp.int32(INT_MIN)
        tot = count_where(lambda k, kpos: k >= tvec)
        ok = tot >= kf
        return jnp.where(ok, cand, u), jnp.where(ok, tot, cacc)

    u, cacc = lax.fori_loop(0, 32, bit_body, (jnp.zeros((1, Q), i32), jnp.zeros((1, Q), f32)))
    thr = jnp.maximum(u ^ jnp.int32(INT_MIN), INT_MIN + 1)
    overflow = jnp.where(u != 0, cacc, 0.0) > kf
    n_over = jnp.max(jnp.where(overflow, 1.0, 0.0), axis=1, keepdims=True)[0, 0]

    @pl.when(n_over > 0.0)
    def _():
        need = kf - count_where(lambda k, kpos: k > thr)

        def cut_body(b, cut):
            cand = cut | lax.shift_left(jnp.int32(1), 14 - b)
            cnt = count_where(lambda k, kpos: jnp.where(k == thr, kpos, cand) < cand)
            return jnp.where(cnt <= need, cand, cut)

        cut = lax.fori_loop(0, 15, cut_body, jnp.zeros((1, Q), i32))

        def drop_tile(j, carry):
            k = sc_ref[j]
            drop = jnp.where(k == thr, key_pos(j), -1) >= cut
            sc_ref[j] = jnp.where(drop, INT_MIN, k)
            return carry

        lax.fori_loop(0, nt, drop_tile, 0)

    m_ref[...] = jnp.full(m_ref.shape, NEG_BIG, f32)
    l_ref[...] = jnp.zeros(l_ref.shape, f32)
    acc_ref[...] = jnp.zeros(acc_ref.shape, f32)
    qlatt = qlatt_ref[0, 0]

    def attn_tile(j, with_bias):
        kv = ckv_ref[0, pl.ds(tile_start(j), T), :]
        kvt = kv.T
        mask_add = jnp.where(sc_ref[j] >= thr, 0.0, NEG_BIG)
        pair_cols = [slice(g * 2 * Q, (g + 1) * 2 * Q) for g in range(A_HEADS // 2)]
        logits = lambda g: jnp.dot(kv, qlatt[:, pair_cols[g]], preferred_element_type=f32)
        x_next = logits(0)
        for g in range(A_HEADS // 2):
            x_pair = x_next
            if g + 1 < A_HEADS // 2:
                x_next = logits(g + 1)
            ps, alphas = [], []
            for u in range(2):
                h = 2 * g + u
                cols = slice(h * Q, (h + 1) * Q)
                x = x_pair[:, u * Q:(u + 1) * Q] + mask_add
                if with_bias:
                    x = x + bias0_ref[h]
                m_prev = m_ref[:, cols]
                m_new = jnp.maximum(m_prev, jnp.max(x, axis=0, keepdims=True))
                p = jnp.exp(x - m_new)
                alpha = jnp.exp(m_prev - m_new)
                l_ref[:, cols] = alpha * l_ref[:, cols] + jnp.sum(p, axis=0, keepdims=True)
                m_ref[:, cols] = m_new
                ps.append(p.astype(bf16))
                alphas.append(alpha)
            pv = jnp.dot(kvt, jnp.concatenate(ps, axis=1), preferred_element_type=f32)
            acc_ref[:, pair_cols[g]] = (
                jnp.concatenate(alphas, axis=1) * acc_ref[:, pair_cols[g]] + pv)

    attn_tile(0, True)

    def attn_body(j, carry):
        attn_tile(j, False)
        return carry

    lax.fori_loop(1, nt, attn_body, 0)

    inv_l = 1.0 / l_ref[...]
    outs = []
    for h in range(A_HEADS):
        cols = slice(h * Q, (h + 1) * Q)
        o_lat_t = (acc_ref[:, cols] * inv_l[:, cols]).astype(bf16)
        outs.append(jnp.dot(wuvt_ref[h], o_lat_t, preferred_element_type=f32))
    o_ref[0] = jnp.concatenate(outs, axis=0).T


def dsa_pallas(cq, ckv, kw, widx_t, g_cq, g_ckv, g_kidx, w_uq, w_iq, w_uk, w_uv, rel_bias):
    bsz, seq, _ = cq.shape
    f32, bf16 = jnp.float32, jnp.bfloat16
    Q, T, H = Q_BLOCK, KEY_TILE, A_HEADS
    nblk = seq // Q
    topk = min(IDX_TOPK_MAX, seq // 4)
    wiqt = w_iq.reshape(A_Q_LORA, IDX_HEADS * IDX_DIM).T.astype(bf16)
    wuqt = w_uq.reshape(A_Q_LORA, H * A_HEAD_DIM).T.astype(bf16)
    wuk = jnp.transpose(w_uk, (1, 0, 2)).astype(bf16)
    wuvt = jnp.transpose(w_uv, (1, 2, 0)).astype(bf16)
    tok = lambda w: pl.BlockSpec((1, Q, w), lambda b, t: (b, t, 0))
    full = lambda shape: pl.BlockSpec(shape, lambda b, t: (0,) * len(shape))
    blk = lambda r: pl.BlockSpec((1, 1, r, H * Q), lambda b, t: (b, t, 0, 0))
    a_t, qlat_t, ckvn, kidxn = pl.pallas_call(
        _dsa_prep_kernel,
        grid=(bsz, nblk),
        in_specs=[tok(A_Q_LORA), tok(A_KV_LORA), tok(128),
                  full((1, A_Q_LORA)), full((1, A_KV_LORA)), full((1, IDX_DIM)),
                  full(wiqt.shape), full(wuqt.shape), full(wuk.shape)],
        out_specs=[blk(IDX_DIM), blk(A_KV_LORA), tok(A_KV_LORA), tok(IDX_DIM)],
        out_shape=[jax.ShapeDtypeStruct((bsz, nblk, IDX_DIM, H * Q), bf16),
                   jax.ShapeDtypeStruct((bsz, nblk, A_KV_LORA, H * Q), bf16),
                   jax.ShapeDtypeStruct((bsz, seq, A_KV_LORA), bf16),
                   jax.ShapeDtypeStruct((bsz, seq, IDX_DIM), bf16)],
        name="dsa_prep",
    )(cq, ckv, kw, g_cq.reshape(1, -1), g_ckv.reshape(1, -1), g_kidx.reshape(1, -1),
      wiqt, wuqt, wuk)
    ckvp = jnp.pad(ckvn, ((0, 0), (KEY_PAD, 0), (0, 0)))
    kidxp = jnp.pad(kidxn, ((0, 0), (KEY_PAD, 0), (0, 0)))
    half, max_exact = REL_BUCKETS // 2, REL_BUCKETS // 4
    saturation = max_exact * (REL_MAX_DIST / max_exact) ** ((half - 1 - max_exact) / (half - max_exact))
    assert KEY_PAD + 1 >= math.ceil(saturation)
    rel = (jnp.arange(T, dtype=jnp.int32)[:, None] - jnp.arange(Q, dtype=jnp.int32)[None, :]
           - KEY_PAD)
    bucket0 = t5_bucket(rel)
    bucket_far = t5_bucket(jnp.full((1,), -KEY_PAD - 1, jnp.int32))
    smem = pl.BlockSpec(memory_space=pltpu.SMEM)
    nt_max = (nblk - 1 + T // Q) // (T // Q)
    skey = seq + KEY_PAD
    return pl.pallas_call(
        partial(_dsa_main_kernel, topk=topk),
        grid=(bsz, nblk),
        in_specs=[blk(IDX_DIM), blk(A_KV_LORA),
                  pl.BlockSpec((1, IDX_HEADS, Q), lambda b, t: (b, 1, t)),
                  pl.BlockSpec((1, skey, IDX_DIM), lambda b, t: (b, 0, 0)),
                  pl.BlockSpec((1, skey, A_KV_LORA), lambda b, t: (b, 0, 0)),
                  full((T, Q)), smem, smem, full(wuvt.shape)],
        out_specs=tok(H * A_HEAD_DIM),
        out_shape=jax.ShapeDtypeStruct((bsz, seq, H * A_HEAD_DIM), f32),
        scratch_shapes=[pltpu.VMEM((H, T, Q), f32),
                        pltpu.VMEM((nt_max, T, Q), jnp.int32),
                        pltpu.VMEM((T, H * Q), f32),
                        pltpu.VMEM((A_KV_LORA, H * Q), f32),
                        pltpu.VMEM((1, H * Q), f32),
                        pltpu.VMEM((1, H * Q), f32)],
        compiler_params=pltpu.CompilerParams(
            dimension_semantics=("arbitrary", "arbitrary"), vmem_limit_bytes=VMEM_LIMIT),
        name="dsa_main",
    )(a_t, qlat_t, widx_t, kidxp, ckvp, bucket0, bucket_far, rel_bias.astype(f32), wuvt)


PEER_SCORE_TOKENS = 256
PEER_GATHER_TOKENS = 128
PEER_SLOTS = P_HEADS * P_TOPK
WORDS_PER_ROW = 4
PEER_TILES = 32
PEER_GROUP = 2


def _top16(s, order=None, payload=None):
    if order is None:
        order = lax.broadcasted_iota(jnp.int32, s.shape, 0).astype(jnp.float32)
    vals, picks = [], []
    for _ in range(P_TOPK):
        m = jnp.max(s, axis=0, keepdims=True)
        pos = jnp.min(jnp.where(s == m, order, float(N_EXPERTS)), axis=0, keepdims=True)
        hit = order == pos
        vals.append(m)
        if payload is None:
            picks.append(pos)
        else:
            picks.append(jnp.max(jnp.where(hit, payload, -1.0), axis=0, keepdims=True))
        s = jnp.where(hit, -jnp.inf, s)
    return vals, picks


def _staircase_candidates(v1, i1, v2, i2):
    v1m, i1m = jnp.concatenate(v1, axis=0), jnp.concatenate(i1, axis=0)
    v2m, i2m = jnp.concatenate(v2, axis=0), jnp.concatenate(i2, axis=0)
    t = v1m.shape[1]
    sub8 = lax.broadcasted_iota(jnp.int32, (8, t), 0).astype(jnp.float32)
    cand, cidx, rank = [], [], []
    for a in range(8):
        cand.append(v1[a] + v2m[:8])
        cidx.append(i1[a] * float(N_KEYS) + i2m[:8])
        rank.append(sub8 + float(a * P_TOPK))
    cand.append(v1[0] + v2m[8:])
    cidx.append(i1[0] * float(N_KEYS) + i2m[8:])
    rank.append(sub8 + 8.0)
    cand.append(v1m[8:] + v2[0])
    cidx.append(i1m[8:] * float(N_KEYS) + i2[0])
    rank.append((sub8 + 8.0) * float(P_TOPK))
    cat = lambda xs: jnp.concatenate(xs, axis=0)
    return cat(cand), cat(rank), cat(cidx)


def _peer_score_kernel(x_ref, g_ref, wpqt_ref, sk_ref, hn_ref, eidx_ref, gate_ref):
    f32, bf16 = jnp.float32, jnp.bfloat16
    h = _rms(x_ref[...], g_ref[...])
    hb = h.astype(bf16)
    for s in range(h.shape[1] // 128):
        hn_ref[pl.ds(s, h.shape[0], stride=8), :] = h[:, s * 128:(s + 1) * 128]
    qrt = _nt_dot(wpqt_ref[...], hb)
    half = P_QDIM // 2
    e_rows, g_rows = [], []
    for hd in range(P_HEADS):
        tops = []
        for p in range(2):
            qhp = qrt[(hd * 2 + p) * half:(hd * 2 + p + 1) * half, :].astype(bf16)
            s = jnp.dot(sk_ref[hd * 2 + p], qhp, preferred_element_type=f32)
            tops.append(_top16(s))
        (v1, i1), (v2, i2) = tops
        best, be = _top16(*_staircase_candidates(v1, i1, v2, i2))
        ex = [jnp.exp(b - best[0]) for b in best]
        den = ex[0]
        for k in range(1, P_TOPK):
            den = den + ex[k]
        inv = 1.0 / den
        e_rows += be
        g_rows += [x * inv for x in ex]
    eidx_ref[...] = (jnp.concatenate(e_rows, axis=0).T * float(WORDS_PER_ROW)).astype(jnp.int32)
    gate_ref[...] = jnp.concatenate(g_rows, axis=0).T


def _diag_mask():
    r = lax.broadcasted_iota(jnp.int32, (8, PEER_SLOTS * 8), 0)
    c = lax.broadcasted_iota(jnp.int32, (8, PEER_SLOTS * 8), 1)
    return (c & 7) == r


def _gather_group(idx_ref, tab_ref, g_refs, t0):
    views = [idx_ref.at[t0 + u] for u in range(len(g_refs))]
    for r in range(PEER_SLOTS):
        for view, g_ref in zip(views, g_refs):
            row0 = pl.multiple_of(view[r], WORDS_PER_ROW)
            g_ref[r * WORDS_PER_ROW:(r + 1) * WORDS_PER_ROW, :] = tab_ref[
                pl.ds(row0, WORDS_PER_ROW), :]


def _gather_pipeline(idx_ref, tab_ref, tiles, n_tokens, consume):
    grp = PEER_GROUP
    groups = len(tiles) // grp
    for g_ref in tiles[-grp:]:
        g_ref[...] = jnp.zeros(g_ref.shape, g_ref.dtype)

    def drain(group_tiles, group):
        for u in range(grp):
            consume(group_tiles[u], group * (grp // 2) + u // 2, u % 2)

    def trip(i, carry):
        for k in range(groups):
            group = groups * i + k
            _gather_group(idx_ref, tab_ref, tiles[k * grp:(k + 1) * grp], group * grp)
            drain(tiles[(k - 1) * grp:k * grp] if k else tiles[-grp:], jnp.maximum(group - 1, 0))
        return carry

    lax.fori_loop(0, n_tokens // len(tiles), trip, 0)
    drain(tiles[-grp:], n_tokens // grp - 1)


def _peer_act_kernel(idx_ref, hn_ref, tab_ref, a_ref, m_ref, *tiles):
    f32, bf16 = jnp.float32, jnp.bfloat16
    tb = a_ref.shape[0]

    def dots(g_ref, tp, u):
        hp = hn_ref[pl.ds(pl.multiple_of(tp * 16, 16), 16), :].astype(bf16)
        m = _nt_dot(hp, pltpu.bitcast(g_ref[...], bf16))
        m_ref[pl.ds(pl.multiple_of((tp * 2 + u) * 8, 8), 8), :] = m[u * 8:(u + 1) * 8, :]

    _gather_pipeline(idx_ref, tab_ref, tiles, tb, dots)
    m3 = m_ref[...].reshape(tb, 8, PEER_SLOTS * 8)
    z = jnp.sum(jnp.where(_diag_mask()[None], m3, 0.0), axis=1)
    rr = lax.broadcasted_iota(jnp.int32, (PEER_SLOTS * 8, PEER_SLOTS), 0)
    cc = lax.broadcasted_iota(jnp.int32, (PEER_SLOTS * 8, PEER_SLOTS), 1)
    pool = jnp.where((rr >> 3) == cc, 1.0, 0.0).astype(bf16)
    z_hi = z.astype(bf16)
    z_lo = (z - z_hi.astype(f32)).astype(bf16)
    a_ref[...] = (jnp.dot(z_hi, pool, preferred_element_type=f32)
                  + jnp.dot(z_lo, pool, preferred_element_type=f32))


def _peer_out_kernel(idx_ref, a_ref, gate_ref, tab_ref, o_ref, w_ref, *tiles):
    f32, bf16 = jnp.float32, jnp.bfloat16
    tb = a_ref.shape[0]
    a = a_ref[...]
    act = 0.5 * a * (1.0 + lax.erf(a * (2.0 ** -0.5)))
    wgt = (gate_ref[...] * act).astype(bf16)
    rr = lax.broadcasted_iota(jnp.int32, (PEER_SLOTS, PEER_SLOTS * 8), 0)
    cc = lax.broadcasted_iota(jnp.int32, (PEER_SLOTS, PEER_SLOTS * 8), 1)
    expand = jnp.where((cc >> 3) == rr, 1.0, 0.0).astype(bf16)
    w_ref[...] = jnp.dot(wgt, expand, preferred_element_type=f32)
    diag = _diag_mask()

    def combine(g_ref, tp, u):
        t = tp * 2 + u
        wrow = jnp.broadcast_to(w_ref[pl.ds(t, 1), :], (8, PEER_SLOTS * 8))
        wsel = jnp.where(diag, wrow, 0.0).astype(bf16)
        o_ref[pl.ds(pl.multiple_of(t * 8, 8), 8), :] = jnp.dot(
            wsel, pltpu.bitcast(g_ref[...], bf16), preferred_element_type=f32)

    _gather_pipeline(idx_ref, tab_ref, tiles, tb, combine)


PACK_ROWS = 512


def _pack_table_kernel(x_ref, o_ref):
    rows = x_ref.shape[0]
    bf16_bits = lambda v: lax.bitcast_convert_type(
        v.astype(jnp.bfloat16).astype(jnp.float32), jnp.int32)
    for c in range(WORDS_PER_ROW):
        lo = bf16_bits(x_ref[:, (2 * c) * 128:(2 * c + 1) * 128])
        hi = bf16_bits(x_ref[:, (2 * c + 1) * 128:(2 * c + 2) * 128])
        word = (hi & jnp.int32(-65536)) | lax.shift_right_logical(lo, 16)
        o_ref[pl.ds(c, rows, stride=WORDS_PER_ROW), :] = word


def _pack_table(tab):
    n_e, d = tab.shape
    return pl.pallas_call(
        _pack_table_kernel,
        grid=(n_e // PACK_ROWS,),
        in_specs=[pl.BlockSpec((PACK_ROWS, d), lambda i: (i, 0))],
        out_specs=pl.BlockSpec((PACK_ROWS * WORDS_PER_ROW, 128), lambda i: (i, 0)),
        out_shape=jax.ShapeDtypeStruct((n_e * WORDS_PER_ROW, 128), jnp.int32),
        name="pack_table",
    )(tab)


def peer_pallas(x, g_ffn, w_pq, sub_keys, u_emb, v_emb):
    n_tok, d = x.shape
    f32, bf16 = jnp.float32, jnp.bfloat16
    ts, tg = PEER_SCORE_TOKENS, PEER_GATHER_TOKENS
    wpqt = w_pq.reshape(d, P_HEADS * P_QDIM).T.astype(bf16)
    sk = sub_keys.reshape(P_HEADS * 2, N_KEYS, P_QDIM // 2).astype(bf16)
    hn, eidx, gate = pl.pallas_call(
        _peer_score_kernel,
        grid=(n_tok // ts,),
        in_specs=[pl.BlockSpec((ts, d), lambda i: (i, 0)),
                  pl.BlockSpec((1, d), lambda i: (0, 0)),
                  pl.BlockSpec(wpqt.shape, lambda i: (0, 0)),
                  pl.BlockSpec(sk.shape, lambda i: (0, 0, 0))],
        out_specs=[pl.BlockSpec((ts * 8, 128), lambda i: (i, 0)),
                   pl.BlockSpec((ts, PEER_SLOTS), lambda i: (i, 0)),
                   pl.BlockSpec((ts, PEER_SLOTS), lambda i: (i, 0))],
        out_shape=[jax.ShapeDtypeStruct((n_tok * 8, 128), f32),
                   jax.ShapeDtypeStruct((n_tok, PEER_SLOTS), jnp.int32),
                   jax.ShapeDtypeStruct((n_tok, PEER_SLOTS), f32)],
        compiler_params=pltpu.CompilerParams(vmem_limit_bytes=VMEM_LIMIT),
        name="peer_score",
    )(x, g_ffn.reshape(1, d), wpqt, sk)
    utab, vtab = _pack_table(u_emb), _pack_table(v_emb)
    idx_spec = pl.BlockSpec((tg, PEER_SLOTS), lambda i: (i, 0), memory_space=pltpu.SMEM)
    tab_spec = pl.BlockSpec(memory_space=pltpu.VMEM)
    slot_spec = pl.BlockSpec((tg, PEER_SLOTS), lambda i: (i, 0))
    row_spec = pl.BlockSpec((tg * 8, 128), lambda i: (i, 0))
    gbuf = pltpu.VMEM((PEER_SLOTS * WORDS_PER_ROW, 128), jnp.int32)
    cparams = pltpu.CompilerParams(vmem_limit_bytes=VMEM_LIMIT)
    act = pl.pallas_call(
        _peer_act_kernel,
        grid=(n_tok // tg,),
        in_specs=[idx_spec, row_spec, tab_spec],
        out_specs=slot_spec,
        out_shape=jax.ShapeDtypeStruct((n_tok, PEER_SLOTS), f32),
        scratch_shapes=[pltpu.VMEM((tg * 8, PEER_SLOTS * 8), f32)] + [gbuf] * PEER_TILES,
        compiler_params=cparams,
        name="peer_act",
    )(eidx, hn, utab)
    out = pl.pallas_call(
        _peer_out_kernel,
        grid=(n_tok // tg,),
        in_specs=[idx_spec, slot_spec, slot_spec, tab_spec],
        out_specs=row_spec,
        out_shape=jax.ShapeDtypeStruct((n_tok * 8, 128), f32),
        scratch_shapes=[pltpu.VMEM((tg, PEER_SLOTS * 8), f32)] + [gbuf] * PEER_TILES,
        compiler_params=cparams,
        name="peer_out",
    )(eidx, act, gate, vtab)
    return out


def _token_rows(y_ref):
    tm = y_ref.shape[0] // 8
    return jnp.concatenate([y_ref[pl.ds(s, tm, stride=8), :] for s in range(8)], axis=1)


def _final_kernel(x_ref, y_ref, g_ref, o_ref):
    o_ref[...] = _rms(x_ref[...] + _token_rows(y_ref), g_ref[...])


def _residual_kernel(x_ref, y_ref, o_ref):
    o_ref[...] = x_ref[...] + _token_rows(y_ref)


def residual_pallas(x, y8, g=None):
    n, d = x.shape
    tm = 512
    row = pl.BlockSpec((tm, d), lambda i: (i, 0))
    row8 = pl.BlockSpec((tm * 8, 128), lambda i: (i, 0))
    if g is None:
        kern, extra, extra_specs, name = _residual_kernel, (), [], "residual"
    else:
        kern, extra, name = _final_kernel, (g.reshape(1, d),), "final_rmsnorm"
        extra_specs = [pl.BlockSpec((1, d), lambda i: (0, 0))]
    return pl.pallas_call(
        kern,
        grid=(n // tm,),
        in_specs=[row, row8] + extra_specs,
        out_specs=row,
        out_shape=jax.ShapeDtypeStruct((n, d), x.dtype),
        name=name,
    )(x, y8, *extra)


IN_PROJ_TOKENS = 512
GDN_COLS = 3 * B_QK
GATE_ROWS = 8


def _in_proj_kernel(x_ref, g_ref, wa_ref, wkw_ref, wqkv_ref, wz_ref, wab_ref, wabt_ref,
                    cq_ref, ckv_ref, kw_ref, qkv_ref, z_ref, ab_ref, abt_ref):
    f32 = jnp.float32
    hb = _rms(x_ref[0], g_ref[...]).astype(jnp.bfloat16)
    a = jnp.dot(hb, wa_ref[...], preferred_element_type=f32)
    cq_ref[0] = a[:, :A_Q_LORA]
    ckv_ref[0] = a[:, A_Q_LORA:]
    kw_ref[0] = jnp.dot(hb, wkw_ref[...], preferred_element_type=f32)
    qkv_ref[0] = jnp.dot(hb, wqkv_ref[...], preferred_element_type=f32)
    z_ref[0] = jnp.dot(hb, wz_ref[...], preferred_element_type=f32)
    ab_ref[0] = jnp.dot(hb, wab_ref[...], preferred_element_type=f32)
    abt_ref[0] = _nt_dot(wabt_ref[...], hb)


def in_proj_pallas(x, g_mix, w_in):
    bsz, seq, d = x.shape
    f32, bf16 = jnp.float32, jnp.bfloat16
    tm = min(IN_PROJ_TOKENS, seq)
    o = np.cumsum((0,) + COL_WIDTHS)
    wb = w_in.astype(bf16)
    pad_cols = lambda w: jnp.pad(w, ((0, 0), (0, 128 - w.shape[1])))
    wa = wb[:, o[0]:o[2]]
    wkw = pad_cols(wb[:, o[2]:o[4]])
    wqkv = wb[:, o[4]:o[7]]
    wz = wb[:, o[7]:o[8]]
    wab = pad_cols(wb[:, o[8]:o[10]])
    wabt = jnp.concatenate([wb[:, o[8]:o[10]], wb[:, o[3]:o[4]]], axis=1).T
    full = lambda w: pl.BlockSpec(w.shape, lambda b, t: (0, 0))
    tok = lambda w: pl.BlockSpec((1, tm, w), lambda b, t: (b, t, 0))
    shp = lambda w: jax.ShapeDtypeStruct((bsz, seq, w), f32)
    return pl.pallas_call(
        _in_proj_kernel,
        grid=(bsz, seq // tm),
        in_specs=[tok(d), pl.BlockSpec((1, d), lambda b, t: (0, 0)),
                  full(wa), full(wkw), full(wqkv), full(wz), full(wab), full(wabt)],
        out_specs=[tok(A_Q_LORA), tok(A_KV_LORA), tok(128), tok(GDN_COLS), tok(B_QK), tok(128),
                   pl.BlockSpec((1, GATE_ROWS + IDX_HEADS, tm), lambda b, t: (b, 0, t))],
        out_shape=[shp(A_Q_LORA), shp(A_KV_LORA), shp(128), shp(GDN_COLS), shp(B_QK), shp(128),
                   jax.ShapeDtypeStruct((bsz, GATE_ROWS + IDX_HEADS, seq), f32)],
        compiler_params=pltpu.CompilerParams(vmem_limit_bytes=VMEM_LIMIT),
        name="in_proj",
    )(x, g_mix.reshape(1, d), wa, wkw, wqkv, wz, wab, wabt)


def _softplus(x):
    return jnp.maximum(x, 0.0) + jnp.log1p(jnp.exp(-jnp.abs(x)))


def _sigmoid(x):
    return 1.0 / (1.0 + jnp.exp(-x))


def _gdn_gates(pre, a_log, dt_bias, is_decay):
    g = -jnp.exp(a_log) * _softplus(pre + dt_bias)
    return jnp.where(is_decay, g, _sigmoid(pre))


def _gdn_prep_kernel(qkv_ref, halo_ref, cw_ref, ab_ref, abt_ref, alc_ref, dtc_ref, alr_ref, dtr_ref,
                     q_ref, k_ref, v_ref, gc_ref, gr_ref):
    tm = qkv_ref.shape[1]
    x = qkv_ref[0]
    halo = jnp.where(pl.program_id(1) > 0, halo_ref[0], 0.0)
    full = jnp.concatenate([halo, x], axis=0)
    y = x * cw_ref[CONV_WIDTH - 1:CONV_WIDTH, :]
    for back in range(1, CONV_WIDTH):
        shifted = pltpu.roll(full, back, axis=0)[8:, :]
        y = y + shifted * cw_ref[CONV_WIDTH - 1 - back:CONV_WIDTH - back, :]
    y = y * _sigmoid(y)
    for h in range(B_HEADS):
        cols = slice(h * B_HEAD_DIM, (h + 1) * B_HEAD_DIM)
        qh = y[:, h * B_HEAD_DIM:(h + 1) * B_HEAD_DIM]
        kh = y[:, B_QK + h * B_HEAD_DIM:B_QK + (h + 1) * B_HEAD_DIM]
        q_ref[0, :, cols] = qh * lax.rsqrt(
            jnp.sum(qh * qh, axis=-1, keepdims=True) + EPS) * (B_HEAD_DIM ** -0.5)
        k_ref[0, :, cols] = kh * lax.rsqrt(jnp.sum(kh * kh, axis=-1, keepdims=True) + EPS)
    v_ref[0] = y[:, 2 * B_QK:]
    lane = lax.broadcasted_iota(jnp.int32, (tm, 128), 1)
    gates_c = _gdn_gates(ab_ref[0], alc_ref[...], dtc_ref[...], lane < B_HEADS)
    row = lax.broadcasted_iota(jnp.int32, (GATE_ROWS, tm), 0)
    gates_r = _gdn_gates(abt_ref[0], alr_ref[...], dtr_ref[...], row < B_HEADS)
    ti = lax.broadcasted_iota(jnp.int32, (tm, tm), 0)
    tj = lax.broadcasted_iota(jnp.int32, (tm, tm), 1)
    same_chunk = (ti // CHUNK) == (tj // CHUNK)
    hi = lax.Precision.HIGHEST
    lower = jnp.where(same_chunk & (tj <= ti), 1.0, 0.0)
    upper = jnp.where(same_chunk & (ti <= tj), 1.0, 0.0)
    cum_c = jnp.dot(lower, gates_c, preferred_element_type=jnp.float32, precision=hi)
    cum_r = jnp.dot(gates_r, upper, preferred_element_type=jnp.float32, precision=hi)
    gc_ref[0] = jnp.where(lane < B_HEADS, cum_c, gates_c)
    gr_ref[0] = jnp.where(row < B_HEADS, cum_r, gates_r)


def _gdn_main_kernel(q_ref, k_ref, v_ref, z_ref, gc_ref, gr_ref, gon_ref, o_ref, s_ref):
    f32, bf16 = jnp.float32, jnp.bfloat16
    C, Dh = CHUNK, B_HEAD_DIM

    @pl.when(pl.program_id(1) == 0)
    def _():
        s_ref[...] = jnp.zeros(s_ref.shape, f32)

    ii = lax.broadcasted_iota(jnp.int32, (C, C), 0)
    jj = lax.broadcasted_iota(jnp.int32, (C, C), 1)
    causal = ii >= jj
    strict = ii > jj
    eye = jnp.where(ii == jj, 1.0, 0.0)
    mm = lambda a, b: jnp.dot(a, b, preferred_element_type=f32)
    n_chunks = q_ref.shape[1] // C
    units = []
    for c in range(n_chunks):
        rows = slice(c * C, (c + 1) * C)
        gates_c = gc_ref[0, rows, :]
        gates_r = gr_ref[0, :, c * C:(c + 1) * C]
        for h in range(B_HEADS):
            cols = slice(h * Dh, (h + 1) * Dh)
            gcum = jnp.broadcast_to(gates_c[:, h:h + 1], (C, Dh))
            beta = jnp.broadcast_to(gates_c[:, B_HEADS + h:B_HEADS + h + 1], (C, Dh))
            gcum_r = jnp.broadcast_to(gates_r[h:h + 1, :], (C, C))
            decay = jnp.where(causal, jnp.exp(jnp.minimum(gcum[:, :C] - gcum_r, 0.0)), 0.0)
            q, k, v = q_ref[0, rows, cols], k_ref[0, rows, cols], v_ref[0, rows, cols]
            qb, kb = q.astype(bf16), k.astype(bf16)
            kk = _nt_dot(kb, kb)
            qk = _nt_dot(qb, kb)
            egc = jnp.exp(gcum)
            g_last = gcum[C - 1:C, :]
            units.append(dict(
                c=c, h=h, rows=rows, cols=cols,
                neg_m=jnp.where(strict, -(beta[:, :C] * kk * decay), 0.0),
                rhs=jnp.concatenate([v * beta, k * (beta * egc)], axis=1),
                q_dec=(q * egc).astype(bf16), intra=(qk * decay).astype(bf16),
                k_tail=(k * jnp.exp(g_last - gcum)).astype(bf16), chunk_decay=jnp.exp(g_last)))
    def halves(a):
        a_h = a.astype(bf16)
        return a_h, (a - a_h.astype(f32)).astype(bf16)

    def mm3(a, b):
        (a_h, a_l), (b_h, b_l) = a, b
        return mm(a_h, b_h) + (mm(a_h, b_l) + mm(a_l, b_h))

    t_inv = [eye + un["neg_m"] for un in units]
    p_halves = [halves(un["neg_m"]) for un in units]
    for _ in range(5):
        p_halves = [halves(mm3(ph, ph)) for ph in p_halves]
        t_inv = [t + mm3(ph, halves(t)) for ph, t in zip(p_halves, t_inv)]
    sols = [mm3(halves(t), halves(un["rhs"])) for t, un in zip(t_inv, units)]
    states = [s_ref[h] for h in range(B_HEADS)]
    for c in range(n_chunks):
        group = [(un, sol) for un, sol in zip(units, sols) if un["c"] == c]
        sbs = [states[un["h"]].astype(bf16) for un, _ in group]
        vbs = [(sol[:, :Dh] - mm(sol[:, Dh:].astype(bf16), sb)).astype(bf16)
               for (un, sol), sb in zip(group, sbs)]
        outs = [mm(un["q_dec"], sb) + mm(un["intra"], vb)
                for (un, _), sb, vb in zip(group, sbs, vbs)]
        for (un, _), vb in zip(group, vbs):
            states[un["h"]] = states[un["h"]] * un["chunk_decay"] + lax.dot_general(
                un["k_tail"], vb, (((0,), (0,)), ((), ())), preferred_element_type=f32)
        for (un, _), o in zip(group, outs):
            o = o * lax.rsqrt(jnp.mean(o * o, axis=-1, keepdims=True) + EPS) * gon_ref[...]
            zz = z_ref[0, un["rows"], un["cols"]]
            o_ref[0, un["rows"], un["cols"]] = o * (zz * _sigmoid(zz))
    for h in range(B_HEADS):
        s_ref[h] = states[h]


def gdn_pallas(qkv, z, ab, abt, conv_w, a_log, dt_bias, g_onorm):
    bsz, seq, _ = qkv.shape
    f32 = jnp.float32
    tm = min(256, seq)
    zero4 = jnp.zeros((B_HEADS,), f32)
    lane_row = lambda v: jnp.pad(jnp.concatenate([v.astype(f32), zero4]), (0, 120)).reshape(1, 128)
    sub_col = lambda v: jnp.concatenate([v.astype(f32), zero4]).reshape(GATE_ROWS, 1)
    tok = lambda w, t=tm: pl.BlockSpec((1, t, w), lambda b, i: (b, i, 0))
    const = lambda shape: pl.BlockSpec(shape, lambda b, i: (0,) * len(shape))
    shp = lambda w: jax.ShapeDtypeStruct((bsz, seq, w), f32)
    q, k, v, gc, gr = pl.pallas_call(
        _gdn_prep_kernel,
        grid=(bsz, seq // tm),
        in_specs=[tok(GDN_COLS),
                  pl.BlockSpec((1, 8, GDN_COLS), lambda b, i: (b, jnp.maximum(i * (tm // 8) - 1, 0), 0)),
                  const((CONV_WIDTH, GDN_COLS)), tok(128),
                  pl.BlockSpec((1, GATE_ROWS, tm), lambda b, i: (b, 0, i)),
                  const((1, 128)), const((1, 128)), const((GATE_ROWS, 1)), const((GATE_ROWS, 1))],
        out_specs=[tok(B_QK), tok(B_QK), tok(B_QK), tok(128),
                   pl.BlockSpec((1, GATE_ROWS, tm), lambda b, i: (b, 0, i))],
        out_shape=[shp(B_QK), shp(B_QK), shp(B_QK), shp(128),
                   jax.ShapeDtypeStruct((bsz, GATE_ROWS, seq), f32)],
        compiler_params=pltpu.CompilerParams(vmem_limit_bytes=VMEM_LIMIT),
        name="gdn_prep",
    )(qkv, qkv, conv_w.astype(f32), ab, abt, lane_row(a_log), lane_row(dt_bias),
      sub_col(a_log), sub_col(dt_bias))
    ts = 2 * CHUNK
    return pl.pallas_call(
        _gdn_main_kernel,
        grid=(bsz, seq // ts),
        in_specs=[tok(B_QK, ts), tok(B_QK, ts), tok(B_QK, ts), tok(B_QK, ts), tok(128, ts),
                  pl.BlockSpec((1, GATE_ROWS, ts), lambda b, i: (b, 0, i)),
                  const((1, B_HEAD_DIM))],
        out_specs=tok(B_QK, ts),
        out_shape=shp(B_QK),
        scratch_shapes=[pltpu.VMEM((B_HEADS, B_HEAD_DIM, B_HEAD_DIM), f32)],
        compiler_params=pltpu.CompilerParams(dimension_semantics=("arbitrary", "arbitrary")),
        name="gdn_main",
    )(q, k, v, z, gc, gr, g_onorm.astype(f32).reshape(1, B_HEAD_DIM))


def _mem_kv_kernel(mem_ref, g_ref, wk_ref, wv_ref, k_ref, v_ref):
    f32, bf16 = jnp.float32, jnp.bfloat16
    mn = _rms(mem_ref[0], g_ref[...]).astype(bf16)
    k = jnp.dot(mn, wk_ref[...], preferred_element_type=f32)
    v = jnp.dot(mn, wv_ref[...], preferred_element_type=f32)
    for h in range(X_HEADS):
        cols = slice(h * X_HEAD_DIM, (h + 1) * X_HEAD_DIM)
        k_ref[0, h] = k[:, cols].astype(bf16)
        v_ref[0, h] = v[:, cols].astype(bf16)


def _mid_kernel(x_ref, oa_ref, ob_ref, wo_ref, gx_ref, wq_ref, k_ref, v_ref, wox_ref, o_ref):
    f32, bf16 = jnp.float32, jnp.bfloat16
    na = oa_ref.shape[2]
    x1 = (x_ref[0]
          + jnp.dot(oa_ref[0].astype(bf16), wo_ref[:na, :], preferred_element_type=f32)
          + jnp.dot(ob_ref[0].astype(bf16), wo_ref[na:, :], preferred_element_type=f32))
    hq = _rms(x1, gx_ref[...]).astype(bf16)
    q = jnp.dot(hq, wq_ref[...], preferred_element_type=f32)
    heads = []
    for h in range(X_HEADS):
        qh = q[:, h * X_HEAD_DIM:(h + 1) * X_HEAD_DIM].astype(bf16)
        lg = _nt_dot(qh, k_ref[0, h]) * (X_HEAD_DIM ** -0.5)
        p = jnp.exp(lg - jnp.max(lg, axis=-1, keepdims=True))
        p = (p / jnp.sum(p, axis=-1, keepdims=True)).astype(bf16)
        heads.append(jnp.dot(p, v_ref[0, h], preferred_element_type=f32).astype(bf16))
    o = jnp.concatenate(heads, axis=1)
    o_ref[0] = x1 + jnp.dot(o, wox_ref[...], preferred_element_type=f32)


def mid_pallas(x, o_a, o_b, w_out, g_cross, mem, g_mem, wq, wk, wv, wo):
    bsz, seq, d = x.shape
    f32, bf16 = jnp.float32, jnp.bfloat16
    hx = X_HEADS * X_HEAD_DIM
    m_len = mem.shape[1]
    const2 = lambda shape: pl.BlockSpec(shape, lambda b: (0,) * len(shape))
    kv_spec = pl.BlockSpec((1, X_HEADS, m_len, X_HEAD_DIM), lambda b: (b, 0, 0, 0))
    kv_shape = jax.ShapeDtypeStruct((bsz, X_HEADS, m_len, X_HEAD_DIM), bf16)
    k, v = pl.pallas_call(
        _mem_kv_kernel,
        grid=(bsz,),
        in_specs=[pl.BlockSpec((1, m_len, d), lambda b: (b, 0, 0)), const2((1, d)),
                  const2((d, hx)), const2((d, hx))],
        out_specs=[kv_spec, kv_spec],
        out_shape=[kv_shape, kv_shape],
        name="mem_kv",
    )(mem, g_mem.reshape(1, d), wk.reshape(d, hx).astype(bf16), wv.reshape(d, hx).astype(bf16))
    tm = min(512, seq)
    tok = lambda w: pl.BlockSpec((1, tm, w), lambda b, t: (b, t, 0))
    const = lambda shape: pl.BlockSpec(shape, lambda b, t: (0,) * len(shape))
    kv_spec2 = pl.BlockSpec((1, X_HEADS, m_len, X_HEAD_DIM), lambda b, t: (b, 0, 0, 0))
    return pl.pallas_call(
        _mid_kernel,
        grid=(bsz, seq // tm),
        in_specs=[tok(d), tok(o_a.shape[2]), tok(o_b.shape[2]), const((MIX_WIDTH, d)),
                  const((1, d)), const((d, hx)), kv_spec2, kv_spec2, const((hx, d))],
        out_specs=tok(d),
        out_shape=jax.ShapeDtypeStruct((bsz, seq, d), f32),
        compiler_params=pltpu.CompilerParams(vmem_limit_bytes=VMEM_LIMIT),
        name="mid",
    )(x, o_a, o_b, w_out.astype(bf16), g_cross.reshape(1, d), wq.reshape(d, hx).astype(bf16),
      k, v, wo.reshape(hx, d).astype(bf16))


def kernel(x, mem, g_mix, w_in, g_cq, g_ckv, g_kidx, w_uq, w_iq, w_uk, w_uv, rel_bias, conv_w, A_log, dt_bias, g_onorm, w_out, g_cross, g_mem, wq_x, wk_x, wv_x, wo_x, g_ffn, w_pq, sub_keys, u_emb, v_emb, g_final):
    bsz, seq, d = x.shape
    for l in range(DEPTH):
        cq, ckv, kw, qkv, z, ab, abt = in_proj_pallas(x, g_mix[l], w_in[l])
        o_a = dsa_pallas(cq, ckv, kw, abt, g_cq[l], g_ckv[l], g_kidx[l],
                         w_uq[l], w_iq[l], w_uk[l], w_uv[l], rel_bias)
        o_b = gdn_pallas(qkv, z, ab, abt, conv_w[l], A_log[l], dt_bias[l], g_onorm[l])
        x = mid_pallas(x, o_a, o_b, w_out[l], g_cross[l], mem, g_mem[l],
                       wq_x[l], wk_x[l], wv_x[l], wo_x[l])
        xf = x.reshape(bsz * seq, d)
        y8 = peer_pallas(xf, g_ffn[l], w_pq[l], sub_keys[l], u_emb[l], v_emb[l])
        if l + 1 < DEPTH:
            x = residual_pallas(xf, y8).reshape(bsz, seq, d)
    return residual_pallas(xf, y8, g_final).reshape(bsz, seq, d)
```

```python
import math
from functools import partial
import jax
import jax.numpy as jnp
from jax import lax
import numpy as np
from jax.experimental import pallas as pl
from jax.experimental.pallas import tpu as pltpu

DEPTH = 1

CHUNK = 64
Q_BLOCK = 128
EPS = 1e-6

A_HEADS = 8
A_HEAD_DIM = 64
A_Q_LORA = 256
A_KV_LORA = 256
IDX_HEADS = 8
IDX_DIM = 64
IDX_TOPK_MAX = 256
ATTN_SCALE = A_HEAD_DIM ** -0.5
IDX_SCALE = (IDX_HEADS * IDX_DIM) ** -0.5

B_HEADS = 4
B_HEAD_DIM = 128
B_QK = B_HEADS * B_HEAD_DIM
CONV_WIDTH = 4

REL_BUCKETS = 32
REL_MAX_DIST = 128

X_HEADS = 4
X_HEAD_DIM = 128

P_HEADS = 8
N_KEYS = 128
N_EXPERTS = N_KEYS * N_KEYS
P_TOPK = 16
P_QDIM = 256

COL_WIDTHS = (A_Q_LORA, A_KV_LORA, IDX_DIM, IDX_HEADS, B_QK, B_QK, B_QK, B_QK, B_HEADS, B_HEADS)
MIX_WIDTH = A_HEADS * A_HEAD_DIM + B_HEADS * B_HEAD_DIM


def t5_bucket(rel):
    half = REL_BUCKETS // 2
    max_exact = half // 2
    n = jnp.abs(rel)
    nf = jnp.maximum(n, max_exact).astype(jnp.float32)
    large = max_exact + (jnp.log(nf / max_exact) / math.log(REL_MAX_DIST / max_exact)
                         * (half - max_exact)).astype(jnp.int32)
    large = jnp.minimum(large, half - 1)
    return jnp.where(rel > 0, half, 0) + jnp.where(n < max_exact, n, large)


INT_MIN = -2147483648
NEG_BIG = -1e30
KEY_TILE = 512
KEY_PAD = KEY_TILE - Q_BLOCK
DSA_PREP_BLOCKS = 4
VMEM_LIMIT = 56 * 1024 * 1024


def _rms(x, g):
    return x * lax.rsqrt(jnp.mean(x * x, axis=-1, keepdims=True) + EPS) * g


def _nt_dot(a, b):
    return lax.dot_general(a, b, (((1,), (1,)), ((), ())), preferred_element_type=jnp.float32)


def _dsa_prep_kernel(cq_ref, ckv_ref, kw_ref, gcq_ref, gckv_ref, gk_ref,
                     wiqt_ref, wuqt_ref, wuk_ref,
                     at_ref, qlatt_ref, ckvn_ref, kidxn_ref):
    f32, bf16 = jnp.float32, jnp.bfloat16
    Q = Q_BLOCK
    kw = kw_ref[0]
    cqn = _rms(cq_ref[0], gcq_ref[...]).astype(bf16)
    qit = _nt_dot(wiqt_ref[...], cqn)
    qt = _nt_dot(wuqt_ref[...], cqn)
    for blk in range(at_ref.shape[1]):
        toks = slice(blk * Q, (blk + 1) * Q)
        for h in range(A_HEADS):
            cols = slice(h * Q, (h + 1) * Q)
            at_ref[0, blk, :, cols] = qit[h * IDX_DIM:(h + 1) * IDX_DIM, toks].astype(bf16)
            qh = qt[h * A_HEAD_DIM:(h + 1) * A_HEAD_DIM, toks].astype(bf16)
            ql = jnp.dot(wuk_ref[h], qh, preferred_element_type=f32) * ATTN_SCALE
            qlatt_ref[0, blk, :, cols] = ql.astype(bf16)
    ckvn_ref[0] = _rms(ckv_ref[0], gckv_ref[...]).astype(bf16)
    kidxn_ref[0] = _rms(kw[:, :IDX_DIM], gk_ref[...]).astype(bf16)


def _dsa_main_kernel(at_ref, qlatt_ref, wt_ref, kidx_ref, ckv_ref, bucket_ref, far_ref, rb_ref,
                     wuvt_ref, o_ref, bias0_ref, sc_ref, big_ref, acc_ref, m_ref, l_ref, *, topk):
    f32, i32, bf16 = jnp.float32, jnp.int32, jnp.bfloat16
    Q, T = Q_BLOCK, KEY_TILE
    i = pl.program_id(1)

    @pl.when(jnp.logical_and(pl.program_id(0) == 0, i == 0))
    def _():
        bucket = bucket_ref[...]
        for h in range(A_HEADS):
            far_bias = rb_ref[far_ref[0], h]
            bias = jnp.zeros((T, Q), f32)
            for b in range(REL_BUCKETS):
                bias = jnp.where(bucket == b, rb_ref[b, h] - far_bias, bias)
            bias0_ref[h] = bias
    e = (i + 1) * Q
    nt = (i + T // Q) // (T // Q)
    kf = jnp.float32(topk)
    wsc = wt_ref[0] * IDX_SCALE
    sub = lax.broadcasted_iota(i32, (T, Q), 0)
    qlane = lax.broadcasted_iota(i32, (T, Q), 1)
    limit = ((i * Q + qlane) // CHUNK + 1) * CHUNK
    at = at_ref[0, 0]

    def tile_start(j):
        return pl.multiple_of(e + KEY_PAD - (j + 1) * T, 128)

    def key_pos(j):
        return tile_start(j) - KEY_PAD + sub

    def score_tile(j, carry):
        kid = kidx_ref[0, pl.ds(tile_start(j), T), :]
        big_ref[...] = jnp.dot(kid, at, preferred_element_type=f32)
        s = jnp.zeros((T, Q), f32)
        for h in range(IDX_HEADS):
            s = s + wsc[h:h + 1, :] * jnp.maximum(big_ref[:, h * Q:(h + 1) * Q], 0.0)
        bits = lax.bitcast_convert_type(s, i32)
        key = jnp.where(bits < 0, bits ^ jnp.int32(0x7FFFFFFF), bits)
        key = jnp.where(s == 0.0, 0, key)
        kpos = key_pos(j)
        key = jnp.where(kpos >= 0, key, INT_MIN)
        sc_ref[j] = jnp.where(kpos < limit, key, INT_MIN)
        return carry

    lax.fori_loop(0, nt, score_tile, 0)

    def count_where(pred):
        def body(j, cnt):
            hit = jnp.where(pred(sc_ref[j], key_pos(j)), 1.0, 0.0)
            return cnt + jnp.sum(hit.reshape(T // 64, 64, Q), axis=0)
        cnt = lax.fori_loop(0, nt, body, jnp.zeros((64, Q), f32))
        return jnp.sum(cnt, axis=0, keepdims=True)

    def bit_body(b, carry):
        u, cacc = carry
        cand = u | lax.shift_left(jnp.int32(1), 31 - b)
        tvec = cand ^ jnp.int32(INT_MIN)
        tot = count_where(lambda k, kpos: k >= tvec)
        ok = tot >= kf
        return jnp.where(ok, cand, u), jnp.where(ok, tot, cacc)

    u, cacc = lax.fori_loop(0, 32, bit_body, (jnp.zeros((1, Q), i32), jnp.zeros((1, Q), f32)))
    thr = jnp.maximum(u ^ jnp.int32(INT_MIN), INT_MIN + 1)
    overflow = jnp.where(u != 0, cacc, 0.0) > kf
    n_over = jnp.max(jnp.where(overflow, 1.0, 0.0), axis=1, keepdims=True)[0, 0]

    @pl.when(n_over > 0.0)
    def _():
        need = kf - count_where(lambda k, kpos: k > thr)

        def cut_body(b, cut):
            cand = cut | lax.shift_left(jnp.int32(1), 14 - b)
            cnt = count_where(lambda k, kpos: jnp.where(k == thr, kpos, cand) < cand)
            return jnp.where(cnt <= need, cand, cut)

        cut = lax.fori_loop(0, 15, cut_body, jnp.zeros((1, Q), i32))

        def drop_tile(j, carry):
            k = sc_ref[j]
            drop = jnp.where(k == thr, key_pos(j), -1) >= cut
            sc_ref[j] = jnp.where(drop, INT_MIN, k)
            return carry

        lax.fori_loop(0, nt, drop_tile, 0)

    m_ref[...] = jnp.full(m_ref.shape, NEG_BIG, f32)
    l_ref[...] = jnp.zeros(l_ref.shape, f32)
    acc_ref[...] = jnp.zeros(acc_ref.shape, f32)
    qlatt = qlatt_ref[0, 0]

    def attn_tile(j, with_bias):
        kv = ckv_ref[0, pl.ds(tile_start(j), T), :]
        kvt = kv.T
        mask_add = jnp.where(sc_ref[j] >= thr, 0.0, NEG_BIG)
        pair_cols = [slice(g * 2 * Q, (g + 1) * 2 * Q) for g in range(A_HEADS // 2)]
        logits = lambda g: jnp.dot(kv, qlatt[:, pair_cols[g]], preferred_element_type=f32)
        x_next = logits(0)
        for g in range(A_HEADS // 2):
            x_pair = x_next
            if g + 1 < A_HEADS // 2:
                x_next = logits(g + 1)
            ps, alphas = [], []
            for u in range(2):
                h = 2 * g + u
                cols = slice(h * Q, (h + 1) * Q)
                x = x_pair[:, u * Q:(u + 1) * Q] + mask_add
                if with_bias:
                    x = x + bias0_ref[h]
                m_prev = m_ref[:, cols]
                m_new = jnp.maximum(m_prev, jnp.max(x, axis=0, keepdims=True))
                p = jnp.exp(x - m_new)
                alpha = jnp.exp(m_prev - m_new)
                l_ref[:, cols] = alpha * l_ref[:, cols] + jnp.sum(p, axis=0, keepdims=True)
                m_ref[:, cols] = m_new
                ps.append(p.astype(bf16))
                alphas.append(alpha)
            pv = jnp.dot(kvt, jnp.concatenate(ps, axis=1), preferred_element_type=f32)
            acc_ref[:, pair_cols[g]] = (
                jnp.concatenate(alphas, axis=1) * acc_ref[:, pair_cols[g]] + pv)

    attn_tile(0, True)

    def attn_body(j, carry):
        attn_tile(j, False)
        return carry

    lax.fori_loop(1, nt, attn_body, 0)

    inv_l = 1.0 / l_ref[...]
    outs = []
    for h in range(A_HEADS):
        cols = slice(h * Q, (h + 1) * Q)
        o_lat_t = (acc_ref[:, cols] * inv_l[:, cols]).astype(bf16)
        outs.append(jnp.dot(wuvt_ref[h], o_lat_t, preferred_element_type=f32))
    o_ref[0] = jnp.concatenate(outs, axis=0).T


def dsa_pallas(cq, ckv, kw, widx_t, g_cq, g_ckv, g_kidx, w_uq, w_iq, w_uk, w_uv, rel_bias):
    bsz, seq, _ = cq.shape
    f32, bf16 = jnp.float32, jnp.bfloat16
    Q, T, H = Q_BLOCK, KEY_TILE, A_HEADS
    nblk = seq // Q
    topk = min(IDX_TOPK_MAX, seq // 4)
    wiqt = w_iq.reshape(A_Q_LORA, IDX_HEADS * IDX_DIM).T.astype(bf16)
    wuqt = w_uq.reshape(A_Q_LORA, H * A_HEAD_DIM).T.astype(bf16)
    wuk = jnp.transpose(w_uk, (1, 0, 2)).astype(bf16)
    wuvt = jnp.transpose(w_uv, (1, 2, 0)).astype(bf16)
    tok = lambda w: pl.BlockSpec((1, Q, w), lambda b, t: (b, t, 0))
    full = lambda shape: pl.BlockSpec(shape, lambda b, t: (0,) * len(shape))
    blk = lambda r: pl.BlockSpec((1, 1, r, H * Q), lambda b, t: (b, t, 0, 0))
    nb = math.gcd(DSA_PREP_BLOCKS, nblk)
    ptok = lambda w: pl.BlockSpec((1, nb * Q, w), lambda b, t: (b, t, 0))
    pblk = lambda r: pl.BlockSpec((1, nb, r, H * Q), lambda b, t: (b, t, 0, 0))
    a_t, qlat_t, ckvn, kidxn = pl.pallas_call(
        _dsa_prep_kernel,
        grid=(bsz, nblk // nb),
        in_specs=[ptok(A_Q_LORA), ptok(A_KV_LORA), ptok(128),
                  full((1, A_Q_LORA)), full((1, A_KV_LORA)), full((1, IDX_DIM)),
                  full(wiqt.shape), full(wuqt.shape), full(wuk.shape)],
        out_specs=[pblk(IDX_DIM), pblk(A_KV_LORA), ptok(A_KV_LORA), ptok(IDX_DIM)],
        out_shape=[jax.ShapeDtypeStruct((bsz, nblk, IDX_DIM, H * Q), bf16),
                   jax.ShapeDtypeStruct((bsz, nblk, A_KV_LORA, H * Q), bf16),
                   jax.ShapeDtypeStruct((bsz, seq, A_KV_LORA), bf16),
                   jax.ShapeDtypeStruct((bsz, seq, IDX_DIM), bf16)],
        name="dsa_prep",
    )(cq, ckv, kw, g_cq.reshape(1, -1), g_ckv.reshape(1, -1), g_kidx.reshape(1, -1),
      wiqt, wuqt, wuk)
    ckvp = jnp.pad(ckvn, ((0, 0), (KEY_PAD, 0), (0, 0)))
    kidxp = jnp.pad(kidxn, ((0, 0), (KEY_PAD, 0), (0, 0)))
    half, max_exact = REL_BUCKETS // 2, REL_BUCKETS // 4
    saturation = max_exact * (REL_MAX_DIST / max_exact) ** ((half - 1 - max_exact) / (half - max_exact))
    assert KEY_PAD + 1 >= math.ceil(saturation)
    rel = (jnp.arange(T, dtype=jnp.int32)[:, None] - jnp.arange(Q, dtype=jnp.int32)[None, :]
           - KEY_PAD)
    bucket0 = t5_bucket(rel)
    bucket_far = t5_bucket(jnp.full((1,), -KEY_PAD - 1, jnp.int32))
    smem = pl.BlockSpec(memory_space=pltpu.SMEM)
    nt_max = (nblk - 1 + T // Q) // (T // Q)
    skey = seq + KEY_PAD
    return pl.pallas_call(
        partial(_dsa_main_kernel, topk=topk),
        grid=(bsz, nblk),
        in_specs=[blk(IDX_DIM), blk(A_KV_LORA),
                  pl.BlockSpec((1, IDX_HEADS, Q), lambda b, t: (b, 1, t)),
                  pl.BlockSpec((1, skey, IDX_DIM), lambda b, t: (b, 0, 0)),
                  pl.BlockSpec((1, skey, A_KV_LORA), lambda b, t: (b, 0, 0)),
                  full((T, Q)), smem, smem, full(wuvt.shape)],
        out_specs=tok(H * A_HEAD_DIM),
        out_shape=jax.ShapeDtypeStruct((bsz, seq, H * A_HEAD_DIM), f32),
        scratch_shapes=[pltpu.VMEM((H, T, Q), f32),
                        pltpu.VMEM((nt_max, T, Q), jnp.int32),
                        pltpu.VMEM((T, H * Q), f32),
                        pltpu.VMEM((A_KV_LORA, H * Q), f32),
                        pltpu.VMEM((1, H * Q), f32),
                        pltpu.VMEM((1, H * Q), f32)],
        compiler_params=pltpu.CompilerParams(
            dimension_semantics=("arbitrary", "arbitrary"), vmem_limit_bytes=VMEM_LIMIT),
        name="dsa_main",
    )(a_t, qlat_t, widx_t, kidxp, ckvp, bucket0, bucket_far, rel_bias.astype(f32), wuvt)


PEER_SCORE_TOKENS = 256
PEER_GATHER_TOKENS = 128
PEER_SLOTS = P_HEADS * P_TOPK
WORDS_PER_ROW = 4
PEER_TILES = 32
PEER_GROUP = 2


def _top16(s, order=None, payload=None):
    if order is None:
        order = lax.broadcasted_iota(jnp.int32, s.shape, 0).astype(jnp.float32)
    vals, picks = [], []
    for _ in range(P_TOPK):
        m = jnp.max(s, axis=0, keepdims=True)
        pos = jnp.min(jnp.where(s == m, order, float(N_EXPERTS)), axis=0, keepdims=True)
        hit = order == pos
        vals.append(m)
        if payload is None:
            picks.append(pos)
        else:
            picks.append(jnp.max(jnp.where(hit, payload, -1.0), axis=0, keepdims=True))
        s = jnp.where(hit, -jnp.inf, s)
    return vals, picks


def _staircase_candidates(v1, i1, v2, i2):
    v1m, i1m = jnp.concatenate(v1, axis=0), jnp.concatenate(i1, axis=0)
    v2m, i2m = jnp.concatenate(v2, axis=0), jnp.concatenate(i2, axis=0)
    t = v1m.shape[1]
    sub8 = lax.broadcasted_iota(jnp.int32, (8, t), 0).astype(jnp.float32)
    cand, cidx, rank = [], [], []
    for a in range(8):
        cand.append(v1[a] + v2m[:8])
        cidx.append(i1[a] * float(N_KEYS) + i2m[:8])
        rank.append(sub8 + float(a * P_TOPK))
    cand.append(v1[0] + v2m[8:])
    cidx.append(i1[0] * float(N_KEYS) + i2m[8:])
    rank.append(sub8 + 8.0)
    cand.append(v1m[8:] + v2[0])
    cidx.append(i1m[8:] * float(N_KEYS) + i2[0])
    rank.append((sub8 + 8.0) * float(P_TOPK))
    cat = lambda xs: jnp.concatenate(xs, axis=0)
    return cat(cand), cat(rank), cat(cidx)


def _peer_score_kernel(x_ref, g_ref, wpqt_ref, sk_ref, hn_ref, eidx_ref, gate_ref):
    f32, bf16 = jnp.float32, jnp.bfloat16
    h = _rms(x_ref[...], g_ref[...])
    hb = h.astype(bf16)
    for s in range(h.shape[1] // 128):
        hn_ref[pl.ds(s, h.shape[0], stride=8), :] = h[:, s * 128:(s + 1) * 128]
    qrt = _nt_dot(wpqt_ref[...], hb)
    half = P_QDIM // 2
    e_rows, g_rows = [], []
    for hd in range(P_HEADS):
        tops = []
        for p in range(2):
            qhp = qrt[(hd * 2 + p) * half:(hd * 2 + p + 1) * half, :].astype(bf16)
            s = jnp.dot(sk_ref[hd * 2 + p], qhp, preferred_element_type=f32)
            tops.append(_top16(s))
        (v1, i1), (v2, i2) = tops
        best, be = _top16(*_staircase_candidates(v1, i1, v2, i2))
        ex = [jnp.exp(b - best[0]) for b in best]
        den = ex[0]
        for k in range(1, P_TOPK):
            den = den + ex[k]
        inv = 1.0 / den
        e_rows += be
        g_rows += [x * inv for x in ex]
    eidx_ref[...] = (jnp.concatenate(e_rows, axis=0).T * float(WORDS_PER_ROW)).astype(jnp.int32)
    gate_ref[...] = jnp.concatenate(g_rows, axis=0).T


def _diag_mask():
    r = lax.broadcasted_iota(jnp.int32, (8, PEER_SLOTS * 8), 0)
    c = lax.broadcasted_iota(jnp.int32, (8, PEER_SLOTS * 8), 1)
    return (c & 7) == r


def _gather_group(idx_ref, tab_ref, g_refs, t0):
    views = [idx_ref.at[t0 + u] for u in range(len(g_refs))]
    for r in range(PEER_SLOTS):
        for view, g_ref in zip(views, g_refs):
            row0 = pl.multiple_of(view[r], WORDS_PER_ROW)
            g_ref[r * WORDS_PER_ROW:(r + 1) * WORDS_PER_ROW, :] = tab_ref[
                pl.ds(row0, WORDS_PER_ROW), :]


def _gather_pipeline(idx_ref, tab_ref, tiles, n_tokens, consume):
    grp = PEER_GROUP
    groups = len(tiles) // grp
    for g_ref in tiles[-grp:]:
        g_ref[...] = jnp.zeros(g_ref.shape, g_ref.dtype)

    def drain(group_tiles, group):
        for u in range(grp):
            consume(group_tiles[u], group * (grp // 2) + u // 2, u % 2)

    def trip(i, carry):
        for k in range(groups):
            group = groups * i + k
            _gather_group(idx_ref, tab_ref, tiles[k * grp:(k + 1) * grp], group * grp)
            drain(tiles[(k - 1) * grp:k * grp] if k else tiles[-grp:], jnp.maximum(group - 1, 0))
        return carry

    lax.fori_loop(0, n_tokens // len(tiles), trip, 0)
    drain(tiles[-grp:], n_tokens // grp - 1)


def _peer_act_kernel(idx_ref, hn_ref, tab_ref, a_ref, m_ref, *tiles):
    f32, bf16 = jnp.float32, jnp.bfloat16
    tb = a_ref.shape[0]

    def dots(g_ref, tp, u):
        hp = hn_ref[pl.ds(pl.multiple_of(tp * 16, 16), 16), :].astype(bf16)
        m = _nt_dot(hp, pltpu.bitcast(g_ref[...], bf16))
        m_ref[pl.ds(pl.multiple_of((tp * 2 + u) * 8, 8), 8), :] = m[u * 8:(u + 1) * 8, :]

    _gather_pipeline(idx_ref, tab_ref, tiles, tb, dots)
    m3 = m_ref[...].reshape(tb, 8, PEER_SLOTS * 8)
    z = jnp.sum(jnp.where(_diag_mask()[None], m3, 0.0), axis=1)
    rr = lax.broadcasted_iota(jnp.int32, (PEER_SLOTS * 8, PEER_SLOTS), 0)
    cc = lax.broadcasted_iota(jnp.int32, (PEER_SLOTS * 8, PEER_SLOTS), 1)
    pool = jnp.where((rr >> 3) == cc, 1.0, 0.0).astype(bf16)
    z_hi = z.astype(bf16)
    z_lo = (z - z_hi.astype(f32)).astype(bf16)
    a_ref[...] = (jnp.dot(z_hi, pool, preferred_element_type=f32)
                  + jnp.dot(z_lo, pool, preferred_element_type=f32))


def _peer_out_kernel(idx_ref, a_ref, gate_ref, tab_ref, o_ref, w_ref, *tiles):
    f32, bf16 = jnp.float32, jnp.bfloat16
    tb = a_ref.shape[0]
    a = a_ref[...]
    act = 0.5 * a * (1.0 + lax.erf(a * (2.0 ** -0.5)))
    wgt = (gate_ref[...] * act).astype(bf16)
    rr = lax.broadcasted_iota(jnp.int32, (PEER_SLOTS, PEER_SLOTS * 8), 0)
    cc = lax.broadcasted_iota(jnp.int32, (PEER_SLOTS, PEER_SLOTS * 8), 1)
    expand = jnp.where((cc >> 3) == rr, 1.0, 0.0).astype(bf16)
    w_ref[...] = jnp.dot(wgt, expand, preferred_element_type=f32)
    diag = _diag_mask()

    def combine(g_ref, tp, u):
        t = tp * 2 + u
        wrow = jnp.broadcast_to(w_ref[pl.ds(t, 1), :], (8, PEER_SLOTS * 8))
        wsel = jnp.where(diag, wrow, 0.0).astype(bf16)
        o_ref[pl.ds(pl.multiple_of(t * 8, 8), 8), :] = jnp.dot(
            wsel, pltpu.bitcast(g_ref[...], bf16), preferred_element_type=f32)

    _gather_pipeline(idx_ref, tab_ref, tiles, tb, combine)


PACK_ROWS = 512


def _pack_table_kernel(x_ref, o_ref):
    rows = x_ref.shape[0]
    bf16_bits = lambda v: lax.bitcast_convert_type(
        v.astype(jnp.bfloat16).astype(jnp.float32), jnp.int32)
    for c in range(WORDS_PER_ROW):
        lo = bf16_bits(x_ref[:, (2 * c) * 128:(2 * c + 1) * 128])
        hi = bf16_bits(x_ref[:, (2 * c + 1) * 128:(2 * c + 2) * 128])
        word = (hi & jnp.int32(-65536)) | lax.shift_right_logical(lo, 16)
        o_ref[pl.ds(c, rows, stride=WORDS_PER_ROW), :] = word


def _pack_table(tab):
    n_e, d = tab.shape
    return pl.pallas_call(
        _pack_table_kernel,
        grid=(n_e // PACK_ROWS,),
        in_specs=[pl.BlockSpec((PACK_ROWS, d), lambda i: (i, 0))],
        out_specs=pl.BlockSpec((PACK_ROWS * WORDS_PER_ROW, 128), lambda i: (i, 0)),
        out_shape=jax.ShapeDtypeStruct((n_e * WORDS_PER_ROW, 128), jnp.int32),
        name="pack_table",
    )(tab)


def peer_pallas(x, g_ffn, w_pq, sub_keys, u_emb, v_emb):
    n_tok, d = x.shape
    f32, bf16 = jnp.float32, jnp.bfloat16
    ts, tg = PEER_SCORE_TOKENS, PEER_GATHER_TOKENS
    wpqt = w_pq.reshape(d, P_HEADS * P_QDIM).T.astype(bf16)
    sk = sub_keys.reshape(P_HEADS * 2, N_KEYS, P_QDIM // 2).astype(bf16)
    hn, eidx, gate = pl.pallas_call(
        _peer_score_kernel,
        grid=(n_tok // ts,),
        in_specs=[pl.BlockSpec((ts, d), lambda i: (i, 0)),
                  pl.BlockSpec((1, d), lambda i: (0, 0)),
                  pl.BlockSpec(wpqt.shape, lambda i: (0, 0)),
                  pl.BlockSpec(sk.shape, lambda i: (0, 0, 0))],
        out_specs=[pl.BlockSpec((ts * 8, 128), lambda i: (i, 0)),
                   pl.BlockSpec((ts, PEER_SLOTS), lambda i: (i, 0)),
                   pl.BlockSpec((ts, PEER_SLOTS), lambda i: (i, 0))],
        out_shape=[jax.ShapeDtypeStruct((n_tok * 8, 128), f32),
                   jax.ShapeDtypeStruct((n_tok, PEER_SLOTS), jnp.int32),
                   jax.ShapeDtypeStruct((n_tok, PEER_SLOTS), f32)],
        compiler_params=pltpu.CompilerParams(vmem_limit_bytes=VMEM_LIMIT),
        name="peer_score",
    )(x, g_ffn.reshape(1, d), wpqt, sk)
    utab, vtab = _pack_table(u_emb), _pack_table(v_emb)
    idx_spec = pl.BlockSpec((tg, PEER_SLOTS), lambda i: (i, 0), memory_space=pltpu.SMEM)
    tab_spec = pl.BlockSpec(memory_space=pltpu.VMEM)
    slot_spec = pl.BlockSpec((tg, PEER_SLOTS), lambda i: (i, 0))
    row_spec = pl.BlockSpec((tg * 8, 128), lambda i: (i, 0))
    gbuf = pltpu.VMEM((PEER_SLOTS * WORDS_PER_ROW, 128), jnp.int32)
    cparams = pltpu.CompilerParams(vmem_limit_bytes=VMEM_LIMIT)
    act = pl.pallas_call(
        _peer_act_kernel,
        grid=(n_tok // tg,),
        in_specs=[idx_spec, row_spec, tab_spec],
        out_specs=slot_spec,
        out_shape=jax.ShapeDtypeStruct((n_tok, PEER_SLOTS), f32),
        scratch_shapes=[pltpu.VMEM((tg * 8, PEER_SLOTS * 8), f32)] + [gbuf] * PEER_TILES,
        compiler_params=cparams,
        name="peer_act",
    )(eidx, hn, utab)
    out = pl.pallas_call(
        _peer_out_kernel,
        grid=(n_tok // tg,),
        in_specs=[idx_spec, slot_spec, slot_spec, tab_spec],
        out_specs=row_spec,
        out_shape=jax.ShapeDtypeStruct((n_tok * 8, 128), f32),
        scratch_shapes=[pltpu.VMEM((tg, PEER_SLOTS * 8), f32)] + [gbuf] * PEER_TILES,
        compiler_params=cparams,
        name="peer_out",
    )(eidx, act, gate, vtab)
    return out


def _token_rows(y_ref):
    tm = y_ref.shape[0] // 8
    return jnp.concatenate([y_ref[pl.ds(s, tm, stride=8), :] for s in range(8)], axis=1)


def _final_kernel(x_ref, y_ref, g_ref, o_ref):
    o_ref[...] = _rms(x_ref[...] + _token_rows(y_ref), g_ref[...])


def _residual_kernel(x_ref, y_ref, o_ref):
    o_ref[...] = x_ref[...] + _token_rows(y_ref)


def residual_pallas(x, y8, g=None):
    n, d = x.shape
    tm = 512
    row = pl.BlockSpec((tm, d), lambda i: (i, 0))
    row8 = pl.BlockSpec((tm * 8, 128), lambda i: (i, 0))
    if g is None:
        kern, extra, extra_specs, name = _residual_kernel, (), [], "residual"
    else:
        kern, extra, name = _final_kernel, (g.reshape(1, d),), "final_rmsnorm"
        extra_specs = [pl.BlockSpec((1, d), lambda i: (0, 0))]
    return pl.pallas_call(
        kern,
        grid=(n // tm,),
        in_specs=[row, row8] + extra_specs,
        out_specs=row,
        out_shape=jax.ShapeDtypeStruct((n, d), x.dtype),
        name=name,
    )(x, y8, *extra)


IN_PROJ_TOKENS = 512
GDN_COLS = 3 * B_QK
GATE_ROWS = 8


def _in_proj_kernel(x_ref, g_ref, wa_ref, wkw_ref, wqkv_ref, wz_ref, wab_ref, wabt_ref,
                    cq_ref, ckv_ref, kw_ref, qkv_ref, z_ref, ab_ref, abt_ref):
    f32 = jnp.float32
    hb = _rms(x_ref[0], g_ref[...]).astype(jnp.bfloat16)
    a = jnp.dot(hb, wa_ref[...], preferred_element_type=f32)
    cq_ref[0] = a[:, :A_Q_LORA]
    ckv_ref[0] = a[:, A_Q_LORA:]
    kw_ref[0] = jnp.dot(hb, wkw_ref[...], preferred_element_type=f32)
    qkv_ref[0] = jnp.dot(hb, wqkv_ref[...], preferred_element_type=f32)
    z_ref[0] = jnp.dot(hb, wz_ref[...], preferred_element_type=f32)
    ab_ref[0] = jnp.dot(hb, wab_ref[...], preferred_element_type=f32)
    abt_ref[0] = _nt_dot(wabt_ref[...], hb)


def in_proj_pallas(x, g_mix, w_in):
    bsz, seq, d = x.shape
    f32, bf16 = jnp.float32, jnp.bfloat16
    tm = min(IN_PROJ_TOKENS, seq)
    o = np.cumsum((0,) + COL_WIDTHS)
    wb = w_in.astype(bf16)
    pad_cols = lambda w: jnp.pad(w, ((0, 0), (0, 128 - w.shape[1])))
    wa = wb[:, o[0]:o[2]]
    wkw = pad_cols(wb[:, o[2]:o[4]])
    wqkv = wb[:, o[4]:o[7]]
    wz = wb[:, o[7]:o[8]]
    wab = pad_cols(wb[:, o[8]:o[10]])
    wabt = jnp.concatenate([wb[:, o[8]:o[10]], wb[:, o[3]:o[4]]], axis=1).T
    full = lambda w: pl.BlockSpec(w.shape, lambda b, t: (0, 0))
    tok = lambda w: pl.BlockSpec((1, tm, w), lambda b, t: (b, t, 0))
    shp = lambda w: jax.ShapeDtypeStruct((bsz, seq, w), f32)
    return pl.pallas_call(
        _in_proj_kernel,
        grid=(bsz, seq // tm),
        in_specs=[tok(d), pl.BlockSpec((1, d), lambda b, t: (0, 0)),
                  full(wa), full(wkw), full(wqkv), full(wz), full(wab), full(wabt)],
        out_specs=[tok(A_Q_LORA), tok(A_KV_LORA), tok(128), tok(GDN_COLS), tok(B_QK), tok(128),
                   pl.BlockSpec((1, GATE_ROWS + IDX_HEADS, tm), lambda b, t: (b, 0, t))],
        out_shape=[shp(A_Q_LORA), shp(A_KV_LORA), shp(128), shp(GDN_COLS), shp(B_QK), shp(128),
                   jax.ShapeDtypeStruct((bsz, GATE_ROWS + IDX_HEADS, seq), f32)],
        compiler_params=pltpu.CompilerParams(vmem_limit_bytes=VMEM_LIMIT),
        name="in_proj",
    )(x, g_mix.reshape(1, d), wa, wkw, wqkv, wz, wab, wabt)


def _softplus(x):
    return jnp.maximum(x, 0.0) + jnp.log1p(jnp.exp(-jnp.abs(x)))


def _sigmoid(x):
    return 1.0 / (1.0 + jnp.exp(-x))


def _gdn_gates(pre, a_log, dt_bias, is_decay):
    g = -jnp.exp(a_log) * _softplus(pre + dt_bias)
    return jnp.where(is_decay, g, _sigmoid(pre))


def _gdn_prep_kernel(qkv_ref, halo_ref, cw_ref, ab_ref, abt_ref, alc_ref, dtc_ref, alr_ref, dtr_ref,
                     q_ref, k_ref, v_ref, gc_ref, gr_ref):
    tm = qkv_ref.shape[1]
    x = qkv_ref[0]
    halo = jnp.where(pl.program_id(1) > 0, halo_ref[0], 0.0)
    full = jnp.concatenate([halo, x], axis=0)
    y = x * cw_ref[CONV_WIDTH - 1:CONV_WIDTH, :]
    for back in range(1, CONV_WIDTH):
        shifted = pltpu.roll(full, back, axis=0)[8:, :]
        y = y + shifted * cw_ref[CONV_WIDTH - 1 - back:CONV_WIDTH - back, :]
    y = y * _sigmoid(y)
    for h in range(B_HEADS):
        cols = slice(h * B_HEAD_DIM, (h + 1) * B_HEAD_DIM)
        qh = y[:, h * B_HEAD_DIM:(h + 1) * B_HEAD_DIM]
        kh = y[:, B_QK + h * B_HEAD_DIM:B_QK + (h + 1) * B_HEAD_DIM]
        q_ref[0, :, cols] = qh * lax.rsqrt(
            jnp.sum(qh * qh, axis=-1, keepdims=True) + EPS) * (B_HEAD_DIM ** -0.5)
        k_ref[0, :, cols] = kh * lax.rsqrt(jnp.sum(kh * kh, axis=-1, keepdims=True) + EPS)
    v_ref[0] = y[:, 2 * B_QK:]
    lane = lax.broadcasted_iota(jnp.int32, (tm, 128), 1)
    gates_c = _gdn_gates(ab_ref[0], alc_ref[...], dtc_ref[...], lane < B_HEADS)
    row = lax.broadcasted_iota(jnp.int32, (GATE_ROWS, tm), 0)
    gates_r = _gdn_gates(abt_ref[0], alr_ref[...], dtr_ref[...], row < B_HEADS)
    ti = lax.broadcasted_iota(jnp.int32, (tm, tm), 0)
    tj = lax.broadcasted_iota(jnp.int32, (tm, tm), 1)
    same_chunk = (ti // CHUNK) == (tj // CHUNK)
    hi = lax.Precision.HIGHEST
    lower = jnp.where(same_chunk & (tj <= ti), 1.0, 0.0)
    upper = jnp.where(same_chunk & (ti <= tj), 1.0, 0.0)
    cum_c = jnp.dot(lower, gates_c, preferred_element_type=jnp.float32, precision=hi)
    cum_r = jnp.dot(gates_r, upper, preferred_element_type=jnp.float32, precision=hi)
    gc_ref[0] = jnp.where(lane < B_HEADS, cum_c, gates_c)
    gr_ref[0] = jnp.where(row < B_HEADS, cum_r, gates_r)


def _gdn_main_kernel(q_ref, k_ref, v_ref, z_ref, gc_ref, gr_ref, gon_ref, o_ref, s_ref):
    f32, bf16 = jnp.float32, jnp.bfloat16
    C, Dh = CHUNK, B_HEAD_DIM

    @pl.when(pl.program_id(1) == 0)
    def _():
        s_ref[...] = jnp.zeros(s_ref.shape, f32)

    ii = lax.broadcasted_iota(jnp.int32, (C, C), 0)
    jj = lax.broadcasted_iota(jnp.int32, (C, C), 1)
    causal = ii >= jj
    strict = ii > jj
    eye = jnp.where(ii == jj, 1.0, 0.0)
    mm = lambda a, b: jnp.dot(a, b, preferred_element_type=f32)
    n_chunks = q_ref.shape[1] // C
    units = []
    for c in range(n_chunks):
        rows = slice(c * C, (c + 1) * C)
        gates_c = gc_ref[0, rows, :]
        gates_r = gr_ref[0, :, c * C:(c + 1) * C]
        for h in range(B_HEADS):
            cols = slice(h * Dh, (h + 1) * Dh)
            gcum = jnp.broadcast_to(gates_c[:, h:h + 1], (C, Dh))
            beta = jnp.broadcast_to(gates_c[:, B_HEADS + h:B_HEADS + h + 1], (C, Dh))
            gcum_r = jnp.broadcast_to(gates_r[h:h + 1, :], (C, C))
            decay = jnp.where(causal, jnp.exp(jnp.minimum(gcum[:, :C] - gcum_r, 0.0)), 0.0)
            q, k, v = q_ref[0, rows, cols], k_ref[0, rows, cols], v_ref[0, rows, cols]
            qb, kb = q.astype(bf16), k.astype(bf16)
            kk = _nt_dot(kb, kb)
            qk = _nt_dot(qb, kb)
            egc = jnp.exp(gcum)
            g_last = gcum[C - 1:C, :]
            units.append(dict(
                c=c, h=h, rows=rows, cols=cols,
                neg_m=jnp.where(strict, -(beta[:, :C] * kk * decay), 0.0),
                rhs=jnp.concatenate([v * beta, k * (beta * egc)], axis=1),
                q_dec=(q * egc).astype(bf16), intra=(qk * decay).astype(bf16),
                k_tail=(k * jnp.exp(g_last - gcum)).astype(bf16), chunk_decay=jnp.exp(g_last)))
    def halves(a):
        a_h = a.astype(bf16)
        return a_h, (a - a_h.astype(f32)).astype(bf16)

    def mm3(a, b):
        (a_h, a_l), (b_h, b_l) = a, b
        return mm(a_h, b_h) + (mm(a_h, b_l) + mm(a_l, b_h))

    t_inv = [eye + un["neg_m"] for un in units]
    p_halves = [halves(un["neg_m"]) for un in units]
    for _ in range(5):
        p_halves = [halves(mm3(ph, ph)) for ph in p_halves]
        t_inv = [t + mm3(ph, halves(t)) for ph, t in zip(p_halves, t_inv)]
    sols = [mm3(halves(t), halves(un["rhs"])) for t, un in zip(t_inv, units)]
    states = [s_ref[h] for h in range(B_HEADS)]
    for c in range(n_chunks):
        group = [(un, sol) for un, sol in zip(units, sols) if un["c"] == c]
        sbs = [states[un["h"]].astype(bf16) for un, _ in group]
        vbs = [(sol[:, :Dh] - mm(sol[:, Dh:].astype(bf16), sb)).astype(bf16)
               for (un, sol), sb in zip(group, sbs)]
        outs = [mm(un["q_dec"], sb) + mm(un["intra"], vb)
                for (un, _), sb, vb in zip(group, sbs, vbs)]
        for (un, _), vb in zip(group, vbs):
            states[un["h"]] = states[un["h"]] * un["chunk_decay"] + lax.dot_general(
                un["k_tail"], vb, (((0,), (0,)), ((), ())), preferred_element_type=f32)
        for (un, _), o in zip(group, outs):
            o = o * lax.rsqrt(jnp.mean(o * o, axis=-1, keepdims=True) + EPS) * gon_ref[...]
            zz = z_ref[0, un["rows"], un["cols"]]
            o_ref[0, un["rows"], un["cols"]] = o * (zz * _sigmoid(zz))
    for h in range(B_HEADS):
        s_ref[h] = states[h]


def gdn_pallas(qkv, z, ab, abt, conv_w, a_log, dt_bias, g_onorm):
    bsz, seq, _ = qkv.shape
    f32 = jnp.float32
    tm = min(256, seq)
    zero4 = jnp.zeros((B_HEADS,), f32)
    lane_row = lambda v: jnp.pad(jnp.concatenate([v.astype(f32), zero4]), (0, 120)).reshape(1, 128)
    sub_col = lambda v: jnp.concatenate([v.astype(f32), zero4]).reshape(GATE_ROWS, 1)
    tok = lambda w, t=tm: pl.BlockSpec((1, t, w), lambda b, i: (b, i, 0))
    const = lambda shape: pl.BlockSpec(shape, lambda b, i: (0,) * len(shape))
    shp = lambda w: jax.ShapeDtypeStruct((bsz, seq, w), f32)
    q, k, v, gc, gr = pl.pallas_call(
        _gdn_prep_kernel,
        grid=(bsz, seq // tm),
        in_specs=[tok(GDN_COLS),
                  pl.BlockSpec((1, 8, GDN_COLS), lambda b, i: (b, jnp.maximum(i * (tm // 8) - 1, 0), 0)),
                  const((CONV_WIDTH, GDN_COLS)), tok(128),
                  pl.BlockSpec((1, GATE_ROWS, tm), lambda b, i: (b, 0, i)),
                  const((1, 128)), const((1, 128)), const((GATE_ROWS, 1)), const((GATE_ROWS, 1))],
        out_specs=[tok(B_QK), tok(B_QK), tok(B_QK), tok(128),
                   pl.BlockSpec((1, GATE_ROWS, tm), lambda b, i: (b, 0, i))],
        out_shape=[shp(B_QK), shp(B_QK), shp(B_QK), shp(128),
                   jax.ShapeDtypeStruct((bsz, GATE_ROWS, seq), f32)],
        compiler_params=pltpu.CompilerParams(vmem_limit_bytes=VMEM_LIMIT),
        name="gdn_prep",
    )(qkv, qkv, conv_w.astype(f32), ab, abt, lane_row(a_log), lane_row(dt_bias),
      sub_col(a_log), sub_col(dt_bias))
    ts = 2 * CHUNK
    return pl.pallas_call(
        _gdn_main_kernel,
        grid=(bsz, seq // ts),
        in_specs=[tok(B_QK, ts), tok(B_QK, ts), tok(B_QK, ts), tok(B_QK, ts), tok(128, ts),
                  pl.BlockSpec((1, GATE_ROWS, ts), lambda b, i: (b, 0, i)),
                  const((1, B_HEAD_DIM))],
        out_specs=tok(B_QK, ts),
        out_shape=shp(B_QK),
        scratch_shapes=[pltpu.VMEM((B_HEADS, B_HEAD_DIM, B_HEAD_DIM), f32)],
        compiler_params=pltpu.CompilerParams(dimension_semantics=("arbitrary", "arbitrary")),
        name="gdn_main",
    )(q, k, v, z, gc, gr, g_onorm.astype(f32).reshape(1, B_HEAD_DIM))


def _mem_kv_kernel(mem_ref, g_ref, wk_ref, wv_ref, k_ref, v_ref):
    f32, bf16 = jnp.float32, jnp.bfloat16
    mn = _rms(mem_ref[0], g_ref[...]).astype(bf16)
    k = jnp.dot(mn, wk_ref[...], preferred_element_type=f32)
    v = jnp.dot(mn, wv_ref[...], preferred_element_type=f32)
    for h in range(X_HEADS):
        cols = slice(h * X_HEAD_DIM, (h + 1) * X_HEAD_DIM)
        k_ref[0, h] = k[:, cols].astype(bf16)
        v_ref[0, h] = v[:, cols].astype(bf16)


def _mid_kernel(x_ref, oa_ref, ob_ref, wo_ref, gx_ref, wq_ref, k_ref, v_ref, wox_ref, o_ref):
    f32, bf16 = jnp.float32, jnp.bfloat16
    na = oa_ref.shape[2]
    x1 = (x_ref[0]
          + jnp.dot(oa_ref[0].astype(bf16), wo_ref[:na, :], preferred_element_type=f32)
          + jnp.dot(ob_ref[0].astype(bf16), wo_ref[na:, :], preferred_element_type=f32))
    hq = _rms(x1, gx_ref[...]).astype(bf16)
    q = jnp.dot(hq, wq_ref[...], preferred_element_type=f32)
    heads = []
    for h in range(X_HEADS):
        qh = q[:, h * X_HEAD_DIM:(h + 1) * X_HEAD_DIM].astype(bf16)
        lg = _nt_dot(qh, k_ref[0, h]) * (X_HEAD_DIM ** -0.5)
        p = jnp.exp(lg - jnp.max(lg, axis=-1, keepdims=True))
        p = (p / jnp.sum(p, axis=-1, keepdims=True)).astype(bf16)
        heads.append(jnp.dot(p, v_ref[0, h], preferred_element_type=f32).astype(bf16))
    o = jnp.concatenate(heads, axis=1)
    o_ref[0] = x1 + jnp.dot(o, wox_ref[...], preferred_element_type=f32)


def mid_pallas(x, o_a, o_b, w_out, g_cross, mem, g_mem, wq, wk, wv, wo):
    bsz, seq, d = x.shape
    f32, bf16 = jnp.float32, jnp.bfloat16
    hx = X_HEADS * X_HEAD_DIM
    m_len = mem.shape[1]
    const2 = lambda shape: pl.BlockSpec(shape, lambda b: (0,) * len(shape))
    kv_spec = pl.BlockSpec((1, X_HEADS, m_len, X_HEAD_DIM), lambda b: (b, 0, 0, 0))
    kv_shape = jax.ShapeDtypeStruct((bsz, X_HEADS, m_len, X_HEAD_DIM), bf16)
    k, v = pl.pallas_call(
        _mem_kv_kernel,
        grid=(bsz,),
        in_specs=[pl.BlockSpec((1, m_len, d), lambda b: (b, 0, 0)), const2((1, d)),
                  const2((d, hx)), const2((d, hx))],
        out_specs=[kv_spec, kv_spec],
        out_shape=[kv_shape, kv_shape],
        name="mem_kv",
    )(mem, g_mem.reshape(1, d), wk.reshape(d, hx).astype(bf16), wv.reshape(d, hx).astype(bf16))
    tm = min(512, seq)
    tok = lambda w: pl.BlockSpec((1, tm, w), lambda b, t: (b, t, 0))
    const = lambda shape: pl.BlockSpec(shape, lambda b, t: (0,) * len(shape))
    kv_spec2 = pl.BlockSpec((1, X_HEADS, m_len, X_HEAD_DIM), lambda b, t: (b, 0, 0, 0))
    return pl.pallas_call(
        _mid_kernel,
        grid=(bsz, seq // tm),
        in_specs=[tok(d), tok(o_a.shape[2]), tok(o_b.shape[2]), const((MIX_WIDTH, d)),
                  const((1, d)), const((d, hx)), kv_spec2, kv_spec2, const((hx, d))],
        out_specs=tok(d),
        out_shape=jax.ShapeDtypeStruct((bsz, seq, d), f32),
        compiler_params=pltpu.CompilerParams(vmem_limit_bytes=VMEM_LIMIT),
        name="mid",
    )(x, o_a, o_b, w_out.astype(bf16), g_cross.reshape(1, d), wq.reshape(d, hx).astype(bf16),
      k, v, wo.reshape(hx, d).astype(bf16))


def kernel(x, mem, g_mix, w_in, g_cq, g_ckv, g_kidx, w_uq, w_iq, w_uk, w_uv, rel_bias, conv_w, A_log, dt_bias, g_onorm, w_out, g_cross, g_mem, wq_x, wk_x, wv_x, wo_x, g_ffn, w_pq, sub_keys, u_emb, v_emb, g_final):
    bsz, seq, d = x.shape
    for l in range(DEPTH):
        cq, ckv, kw, qkv, z, ab, abt = in_proj_pallas(x, g_mix[l], w_in[l])
        o_a = dsa_pallas(cq, ckv, kw, abt, g_cq[l], g_ckv[l], g_kidx[l],
                         w_uq[l], w_iq[l], w_uk[l], w_uv[l], rel_bias)
        o_b = gdn_pallas(qkv, z, ab, abt, conv_w[l], A_log[l], dt_bias[l], g_onorm[l])
        x = mid_pallas(x, o_a, o_b, w_out[l], g_cross[l], mem, g_mem[l],
                       wq_x[l], wk_x[l], wv_x[l], wo_x[l])
        xf = x.reshape(bsz * seq, d)
        y8 = peer_pallas(xf, g_ffn[l], w_pq[l], sub_keys[l], u_emb[l], v_emb[l])
        if l + 1 < DEPTH:
            x = residual_pallas(xf, y8).reshape(bsz, seq, d)
    return residual_pallas(xf, y8, g_final).reshape(bsz, seq, d)
```

```python
import math
from functools import partial
import jax
import jax.numpy as jnp
from jax import lax
import numpy as np
from jax.experimental import pallas as pl
from jax.experimental.pallas import tpu as pltpu

DEPTH = 1

CHUNK = 64
Q_BLOCK = 128
EPS = 1e-6

A_HEADS = 8
A_HEAD_DIM = 64
A_Q_LORA = 256
A_KV_LORA = 256
IDX_HEADS = 8
IDX_DIM = 64
IDX_TOPK_MAX = 256
ATTN_SCALE = A_HEAD_DIM ** -0.5
IDX_SCALE = (IDX_HEADS * IDX_DIM) ** -0.5

B_HEADS = 4
B_HEAD_DIM = 128
B_QK = B_HEADS * B_HEAD_DIM
CONV_WIDTH = 4

REL_BUCKETS = 32
REL_MAX_DIST = 128

X_HEADS = 4
X_HEAD_DIM = 128

P_HEADS = 8
N_KEYS = 128
N_EXPERTS = N_KEYS * N_KEYS
P_TOPK = 16
P_QDIM = 256

COL_WIDTHS = (A_Q_LORA, A_KV_LORA, IDX_DIM, IDX_HEADS, B_QK, B_QK, B_QK, B_QK, B_HEADS, B_HEADS)
MIX_WIDTH = A_HEADS * A_HEAD_DIM + B_HEADS * B_HEAD_DIM


def t5_bucket(rel):
    half = REL_BUCKETS // 2
    max_exact = half // 2
    n = jnp.abs(rel)
    nf = jnp.maximum(n, max_exact).astype(jnp.float32)
    large = max_exact + (jnp.log(nf / max_exact) / math.log(REL_MAX_DIST / max_exact)
                         * (half - max_exact)).astype(jnp.int32)
    large = jnp.minimum(large, half - 1)
    return jnp.where(rel > 0, half, 0) + jnp.where(n < max_exact, n, large)


INT_MIN = -2147483648
NEG_BIG = -1e30
KEY_TILE = 512
KEY_PAD = KEY_TILE - Q_BLOCK
DSA_PREP_BLOCKS = 4
VMEM_LIMIT = 56 * 1024 * 1024


def _rms(x, g):
    return x * lax.rsqrt(jnp.mean(x * x, axis=-1, keepdims=True) + EPS) * g


def _nt_dot(a, b):
    return lax.dot_general(a, b, (((1,), (1,)), ((), ())), preferred_element_type=jnp.float32)


def _dsa_prep_kernel(cq_ref, ckv_ref, kw_ref, gcq_ref, gckv_ref, gk_ref,
                     wiqt_ref, wuqt_ref, wuk_ref,
                     at_ref, qlatt_ref, ckvn_ref, kidxn_ref):
    f32, bf16 = jnp.float32, jnp.bfloat16
    Q = Q_BLOCK
    kw = kw_ref[0]
    cqn = _rms(cq_ref[0], gcq_ref[...]).astype(bf16)
    qit = _nt_dot(wiqt_ref[...], cqn)
    qt = _nt_dot(wuqt_ref[...], cqn)
    for blk in range(at_ref.shape[1]):
        toks = slice(blk * Q, (blk + 1) * Q)
        for h in range(A_HEADS):
            cols = slice(h * Q, (h + 1) * Q)
            at_ref[0, blk, :, cols] = qit[h * IDX_DIM:(h + 1) * IDX_DIM, toks].astype(bf16)
            qh = qt[h * A_HEAD_DIM:(h + 1) * A_HEAD_DIM, toks].astype(bf16)
            ql = jnp.dot(wuk_ref[h], qh, preferred_element_type=f32) * ATTN_SCALE
            qlatt_ref[0, blk, :, cols] = ql.astype(bf16)
    ckvn_ref[0] = _rms(ckv_ref[0], gckv_ref[...]).astype(bf16)
    kidxn_ref[0] = _rms(kw[:, :IDX_DIM], gk_ref[...]).astype(bf16)


def _dsa_main_kernel(at_ref, qlatt_ref, wt_ref, kidx_ref, ckv_ref, bucket_ref, far_ref, rb_ref,
                     wuvt_ref, o_ref, bias0_ref, sc_ref, big_ref, acc_ref, m_ref, l_ref, *, topk):
    f32, i32, bf16 = jnp.float32, jnp.int32, jnp.bfloat16
    Q, T = Q_BLOCK, KEY_TILE
    i = pl.program_id(1)

    @pl.when(jnp.logical_and(pl.program_id(0) == 0, i == 0))
    def _():
        bucket = bucket_ref[...]
        for h in range(A_HEADS):
            far_bias = rb_ref[far_ref[0], h]
            bias = jnp.zeros((T, Q), f32)
            for b in range(REL_BUCKETS):
                bias = jnp.where(bucket == b, rb_ref[b, h] - far_bias, bias)
            bias0_ref[h] = bias
    e = (i + 1) * Q
    nt = (i + T // Q) // (T // Q)
    kf = jnp.float32(topk)
    wsc = wt_ref[0] * IDX_SCALE
    sub = lax.broadcasted_iota(i32, (T, Q), 0)
    qlane = lax.broadcasted_iota(i32, (T, Q), 1)
    limit = ((i * Q + qlane) // CHUNK + 1) * CHUNK
    at = at_ref[0, 0]

    def tile_start(j):
        return pl.multiple_of(e + KEY_PAD - (j + 1) * T, 128)

    def key_pos(j):
        return tile_start(j) - KEY_PAD + sub

    def score_tile(j, carry):
        kid = kidx_ref[0, pl.ds(tile_start(j), T), :]
        big_ref[...] = jnp.dot(kid, at, preferred_element_type=f32)
        s = jnp.zeros((T, Q), f32)
        for h in range(IDX_HEADS):
            s = s + wsc[h:h + 1, :] * jnp.maximum(big_ref[:, h * Q:(h + 1) * Q], 0.0)
        bits = lax.bitcast_convert_type(s, i32)
        key = jnp.where(bits < 0, bits ^ jnp.int32(0x7FFFFFFF), bits)
        key = jnp.where(s == 0.0, 0, key)
        kpos = key_pos(j)
        key = jnp.where(kpos >= 0, key, INT_MIN)
        sc_ref[j] = jnp.where(kpos < limit, key, INT_MIN)
        return carry

    lax.fori_loop(0, nt, score_tile, 0)

    def count_where(pred):
        def body(j, cnt):
            hit = pred(sc_ref[j], key_pos(j))
            for part in range(T // 64):
                rows = slice(part * 64, (part + 1) * 64)
                cnt = jnp.where(hit[rows], cnt + 1.0, cnt)
            return cnt
        cnt = lax.fori_loop(0, nt, body, jnp.zeros((64, Q), f32))
        return jnp.sum(cnt, axis=0, keepdims=True)

    def bit_body(b, carry):
        u, cacc = carry
        cand = u | lax.shift_left(jnp.int32(1), 31 - b)
        tvec = cand ^ jnp.int32(INT_MIN)
        tot = count_where(lambda k, kpos: k >= tvec)
        ok = tot >= kf
        return jnp.where(ok, cand, u), jnp.where(ok, tot, cacc)

    u, cacc = lax.fori_loop(0, 32, bit_body, (jnp.zeros((1, Q), i32), jnp.zeros((1, Q), f32)))
    thr = jnp.maximum(u ^ jnp.int32(INT_MIN), INT_MIN + 1)
    overflow = jnp.where(u != 0, cacc, 0.0) > kf
    n_over = jnp.max(jnp.where(overflow, 1.0, 0.0), axis=1, keepdims=True)[0, 0]

    @pl.when(n_over > 0.0)
    def _():
        need = kf - count_where(lambda k, kpos: k > thr)

        def cut_body(b, cut):
            cand = cut | lax.shift_left(jnp.int32(1), 14 - b)
            cnt = count_where(lambda k, kpos: jnp.where(k == thr, kpos, cand) < cand)
            return jnp.where(cnt <= need, cand, cut)

        cut = lax.fori_loop(0, 15, cut_body, jnp.zeros((1, Q), i32))

        def drop_tile(j, carry):
            k = sc_ref[j]
            drop = jnp.where(k == thr, key_pos(j), -1) >= cut
            sc_ref[j] = jnp.where(drop, INT_MIN, k)
            return carry

        lax.fori_loop(0, nt, drop_tile, 0)

    m_ref[...] = jnp.full(m_ref.shape, NEG_BIG, f32)
    l_ref[...] = jnp.zeros(l_ref.shape, f32)
    acc_ref[...] = jnp.zeros(acc_ref.shape, f32)
    qlatt = qlatt_ref[0, 0]

    def attn_tile(j, with_bias):
        kv = ckv_ref[0, pl.ds(tile_start(j), T), :]
        kvt = kv.T
        mask_add = jnp.where(sc_ref[j] >= thr, 0.0, NEG_BIG)
        pair_cols = [slice(g * 2 * Q, (g + 1) * 2 * Q) for g in range(A_HEADS // 2)]
        logits = lambda g: jnp.dot(kv, qlatt[:, pair_cols[g]], preferred_element_type=f32)
        x_next = logits(0)
        for g in range(A_HEADS // 2):
            x_pair = x_next
            if g + 1 < A_HEADS // 2:
                x_next = logits(g + 1)
            ps, alphas = [], []
            for u in range(2):
                h = 2 * g + u
                cols = slice(h * Q, (h + 1) * Q)
                x = x_pair[:, u * Q:(u + 1) * Q] + mask_add
                if with_bias:
                    x = x + bias0_ref[h]
                m_prev = m_ref[:, cols]
                m_new = jnp.maximum(m_prev, jnp.max(x, axis=0, keepdims=True))
                p = jnp.exp(x - m_new)
                alpha = jnp.exp(m_prev - m_new)
                l_ref[:, cols] = alpha * l_ref[:, cols] + jnp.sum(p, axis=0, keepdims=True)
                m_ref[:, cols] = m_new
                ps.append(p.astype(bf16))
                alphas.append(alpha)
            pv = jnp.dot(kvt, jnp.concatenate(ps, axis=1), preferred_element_type=f32)
            acc_ref[:, pair_cols[g]] = (
                jnp.concatenate(alphas, axis=1) * acc_ref[:, pair_cols[g]] + pv)

    attn_tile(0, True)

    def attn_body(j, carry):
        attn_tile(j, False)
        return carry

    lax.fori_loop(1, nt, attn_body, 0)

    inv_l = 1.0 / l_ref[...]
    outs = []
    for h in range(A_HEADS):
        cols = slice(h * Q, (h + 1) * Q)
        o_lat_t = (acc_ref[:, cols] * inv_l[:, cols]).astype(bf16)
        outs.append(jnp.dot(wuvt_ref[h], o_lat_t, preferred_element_type=f32))
    o_ref[0] = jnp.concatenate(outs, axis=0).T


def dsa_pallas(cq, ckv, kw, widx_t, g_cq, g_ckv, g_kidx, w_uq, w_iq, w_uk, w_uv, rel_bias):
    bsz, seq, _ = cq.shape
    f32, bf16 = jnp.float32, jnp.bfloat16
    Q, T, H = Q_BLOCK, KEY_TILE, A_HEADS
    nblk = seq // Q
    topk = min(IDX_TOPK_MAX, seq // 4)
    wiqt = w_iq.reshape(A_Q_LORA, IDX_HEADS * IDX_DIM).T.astype(bf16)
    wuqt = w_uq.reshape(A_Q_LORA, H * A_HEAD_DIM).T.astype(bf16)
    wuk = jnp.transpose(w_uk, (1, 0, 2)).astype(bf16)
    wuvt = jnp.transpose(w_uv, (1, 2, 0)).astype(bf16)
    tok = lambda w: pl.BlockSpec((1, Q, w), lambda b, t: (b, t, 0))
    full = lambda shape: pl.BlockSpec(shape, lambda b, t: (0,) * len(shape))
    blk = lambda r: pl.BlockSpec((1, 1, r, H * Q), lambda b, t: (b, t, 0, 0))
    nb = math.gcd(DSA_PREP_BLOCKS, nblk)
    ptok = lambda w: pl.BlockSpec((1, nb * Q, w), lambda b, t: (b, t, 0))
    pblk = lambda r: pl.BlockSpec((1, nb, r, H * Q), lambda b, t: (b, t, 0, 0))
    a_t, qlat_t, ckvn, kidxn = pl.pallas_call(
        _dsa_prep_kernel,
        grid=(bsz, nblk // nb),
        in_specs=[ptok(A_Q_LORA), ptok(A_KV_LORA), ptok(128),
                  full((1, A_Q_LORA)), full((1, A_KV_LORA)), full((1, IDX_DIM)),
                  full(wiqt.shape), full(wuqt.shape), full(wuk.shape)],
        out_specs=[pblk(IDX_DIM), pblk(A_KV_LORA), ptok(A_KV_LORA), ptok(IDX_DIM)],
        out_shape=[jax.ShapeDtypeStruct((bsz, nblk, IDX_DIM, H * Q), bf16),
                   jax.ShapeDtypeStruct((bsz, nblk, A_KV_LORA, H * Q), bf16),
                   jax.ShapeDtypeStruct((bsz, seq, A_KV_LORA), bf16),
                   jax.ShapeDtypeStruct((bsz, seq, IDX_DIM), bf16)],
        name="dsa_prep",
    )(cq, ckv, kw, g_cq.reshape(1, -1), g_ckv.reshape(1, -1), g_kidx.reshape(1, -1),
      wiqt, wuqt, wuk)
    ckvp = jnp.pad(ckvn, ((0, 0), (KEY_PAD, 0), (0, 0)))
    kidxp = jnp.pad(kidxn, ((0, 0), (KEY_PAD, 0), (0, 0)))
    half, max_exact = REL_BUCKETS // 2, REL_BUCKETS // 4
    saturation = max_exact * (REL_MAX_DIST / max_exact) ** ((half - 1 - max_exact) / (half - max_exact))
    assert KEY_PAD + 1 >= math.ceil(saturation)
    rel = (jnp.arange(T, dtype=jnp.int32)[:, None] - jnp.arange(Q, dtype=jnp.int32)[None, :]
           - KEY_PAD)
    bucket0 = t5_bucket(rel)
    bucket_far = t5_bucket(jnp.full((1,), -KEY_PAD - 1, jnp.int32))
    smem = pl.BlockSpec(memory_space=pltpu.SMEM)
    nt_max = (nblk - 1 + T // Q) // (T // Q)
    skey = seq + KEY_PAD
    return pl.pallas_call(
        partial(_dsa_main_kernel, topk=topk),
        grid=(bsz, nblk),
        in_specs=[blk(IDX_DIM), blk(A_KV_LORA),
                  pl.BlockSpec((1, IDX_HEADS, Q), lambda b, t: (b, 1, t)),
                  pl.BlockSpec((1, skey, IDX_DIM), lambda b, t: (b, 0, 0)),
                  pl.BlockSpec((1, skey, A_KV_LORA), lambda b, t: (b, 0, 0)),
                  full((T, Q)), smem, smem, full(wuvt.shape)],
        out_specs=tok(H * A_HEAD_DIM),
        out_shape=jax.ShapeDtypeStruct((bsz, seq, H * A_HEAD_DIM), f32),
        scratch_shapes=[pltpu.VMEM((H, T, Q), f32),
                        pltpu.VMEM((nt_max, T, Q), jnp.int32),
                        pltpu.VMEM((T, H * Q), f32),
                        pltpu.VMEM((A_KV_LORA, H * Q), f32),
                        pltpu.VMEM((1, H * Q), f32),
                        pltpu.VMEM((1, H * Q), f32)],
        compiler_params=pltpu.CompilerParams(
            dimension_semantics=("arbitrary", "arbitrary"), vmem_limit_bytes=VMEM_LIMIT),
        name="dsa_main",
    )(a_t, qlat_t, widx_t, kidxp, ckvp, bucket0, bucket_far, rel_bias.astype(f32), wuvt)


PEER_SCORE_TOKENS = 256
PEER_GATHER_TOKENS = 128
PEER_SLOTS = P_HEADS * P_TOPK
WORDS_PER_ROW = 4
PEER_TILES = 32
PEER_GROUP = 2


def _top16(s, order=None, payload=None):
    if order is None:
        order = lax.broadcasted_iota(jnp.int32, s.shape, 0).astype(jnp.float32)
    vals, picks = [], []
    for _ in range(P_TOPK):
        m = jnp.max(s, axis=0, keepdims=True)
        pos = jnp.min(jnp.where(s == m, order, float(N_EXPERTS)), axis=0, keepdims=True)
        hit = order == pos
        vals.append(m)
        if payload is None:
            picks.append(pos)
        else:
            picks.append(jnp.max(jnp.where(hit, payload, -1.0), axis=0, keepdims=True))
        s = jnp.where(hit, -jnp.inf, s)
    return vals, picks


def _staircase_candidates(v1, i1, v2, i2):
    v1m, i1m = jnp.concatenate(v1, axis=0), jnp.concatenate(i1, axis=0)
    v2m, i2m = jnp.concatenate(v2, axis=0), jnp.concatenate(i2, axis=0)
    t = v1m.shape[1]
    sub8 = lax.broadcasted_iota(jnp.int32, (8, t), 0).astype(jnp.float32)
    cand, cidx, rank = [], [], []
    for a in range(8):
        cand.append(v1[a] + v2m[:8])
        cidx.append(i1[a] * float(N_KEYS) + i2m[:8])
        rank.append(sub8 + float(a * P_TOPK))
    cand.append(v1[0] + v2m[8:])
    cidx.append(i1[0] * float(N_KEYS) + i2m[8:])
    rank.append(sub8 + 8.0)
    cand.append(v1m[8:] + v2[0])
    cidx.append(i1m[8:] * float(N_KEYS) + i2[0])
    rank.append((sub8 + 8.0) * float(P_TOPK))
    cat = lambda xs: jnp.concatenate(xs, axis=0)
    return cat(cand), cat(rank), cat(cidx)


def _peer_score_kernel(x_ref, g_ref, wpqt_ref, sk_ref, hn_ref, eidx_ref, gate_ref):
    f32, bf16 = jnp.float32, jnp.bfloat16
    h = _rms(x_ref[...], g_ref[...])
    hb = h.astype(bf16)
    for s in range(h.shape[1] // 128):
        hn_ref[pl.ds(s, h.shape[0], stride=8), :] = h[:, s * 128:(s + 1) * 128]
    qrt = _nt_dot(wpqt_ref[...], hb)
    half = P_QDIM // 2
    e_rows, g_rows = [], []
    for hd in range(P_HEADS):
        tops = []
        for p in range(2):
            qhp = qrt[(hd * 2 + p) * half:(hd * 2 + p + 1) * half, :].astype(bf16)
            s = jnp.dot(sk_ref[hd * 2 + p], qhp, preferred_element_type=f32)
            tops.append(_top16(s))
        (v1, i1), (v2, i2) = tops
        best, be = _top16(*_staircase_candidates(v1, i1, v2, i2))
        ex = [jnp.exp(b - best[0]) for b in best]
        den = ex[0]
        for k in range(1, P_TOPK):
            den = den + ex[k]
        inv = 1.0 / den
        e_rows += be
        g_rows += [x * inv for x in ex]
    eidx_ref[...] = (jnp.concatenate(e_rows, axis=0).T * float(WORDS_PER_ROW)).astype(jnp.int32)
    gate_ref[...] = jnp.concatenate(g_rows, axis=0).T


def _diag_mask():
    r = lax.broadcasted_iota(jnp.int32, (8, PEER_SLOTS * 8), 0)
    c = lax.broadcasted_iota(jnp.int32, (8, PEER_SLOTS * 8), 1)
    return (c & 7) == r


def _gather_group(idx_ref, tab_ref, g_refs, t0):
    views = [idx_ref.at[t0 + u] for u in range(len(g_refs))]
    for r in range(PEER_SLOTS):
        for view, g_ref in zip(views, g_refs):
            row0 = pl.multiple_of(view[r], WORDS_PER_ROW)
            g_ref[r * WORDS_PER_ROW:(r + 1) * WORDS_PER_ROW, :] = tab_ref[
                pl.ds(row0, WORDS_PER_ROW), :]


def _gather_pipeline(idx_ref, tab_ref, tiles, n_tokens, consume):
    grp = PEER_GROUP
    groups = len(tiles) // grp
    for g_ref in tiles[-grp:]:
        g_ref[...] = jnp.zeros(g_ref.shape, g_ref.dtype)

    def drain(group_tiles, group):
        for u in range(grp):
            consume(group_tiles[u], group * (grp // 2) + u // 2, u % 2)

    def trip(i, carry):
        for k in range(groups):
            group = groups * i + k
            _gather_group(idx_ref, tab_ref, tiles[k * grp:(k + 1) * grp], group * grp)
            drain(tiles[(k - 1) * grp:k * grp] if k else tiles[-grp:], jnp.maximum(group - 1, 0))
        return carry

    lax.fori_loop(0, n_tokens // len(tiles), trip, 0)
    drain(tiles[-grp:], n_tokens // grp - 1)


def _peer_act_kernel(idx_ref, hn_ref, tab_ref, a_ref, m_ref, *tiles):
    f32, bf16 = jnp.float32, jnp.bfloat16
    tb = a_ref.shape[0]

    def dots(g_ref, tp, u):
        hp = hn_ref[pl.ds(pl.multiple_of(tp * 16, 16), 16), :].astype(bf16)
        m = _nt_dot(hp, pltpu.bitcast(g_ref[...], bf16))
        m_ref[pl.ds(pl.multiple_of((tp * 2 + u) * 8, 8), 8), :] = m[u * 8:(u + 1) * 8, :]

    _gather_pipeline(idx_ref, tab_ref, tiles, tb, dots)
    m3 = m_ref[...].reshape(tb, 8, PEER_SLOTS * 8)
    z = jnp.sum(jnp.where(_diag_mask()[None], m3, 0.0), axis=1)
    rr = lax.broadcasted_iota(jnp.int32, (PEER_SLOTS * 8, PEER_SLOTS), 0)
    cc = lax.broadcasted_iota(jnp.int32, (PEER_SLOTS * 8, PEER_SLOTS), 1)
    pool = jnp.where((rr >> 3) == cc, 1.0, 0.0).astype(bf16)
    z_hi = z.astype(bf16)
    z_lo = (z - z_hi.astype(f32)).astype(bf16)
    a_ref[...] = (jnp.dot(z_hi, pool, preferred_element_type=f32)
                  + jnp.dot(z_lo, pool, preferred_element_type=f32))


def _peer_out_kernel(idx_ref, a_ref, gate_ref, tab_ref, o_ref, w_ref, *tiles):
    f32, bf16 = jnp.float32, jnp.bfloat16
    tb = a_ref.shape[0]
    a = a_ref[...]
    act = 0.5 * a * (1.0 + lax.erf(a * (2.0 ** -0.5)))
    wgt = (gate_ref[...] * act).astype(bf16)
    rr = lax.broadcasted_iota(jnp.int32, (PEER_SLOTS, PEER_SLOTS * 8), 0)
    cc = lax.broadcasted_iota(jnp.int32, (PEER_SLOTS, PEER_SLOTS * 8), 1)
    expand = jnp.where((cc >> 3) == rr, 1.0, 0.0).astype(bf16)
    w_ref[...] = jnp.dot(wgt, expand, preferred_element_type=f32)
    diag = _diag_mask()

    def combine(g_ref, tp, u):
        t = tp * 2 + u
        wrow = jnp.broadcast_to(w_ref[pl.ds(t, 1), :], (8, PEER_SLOTS * 8))
        wsel = jnp.where(diag, wrow, 0.0).astype(bf16)
        o_ref[pl.ds(pl.multiple_of(t * 8, 8), 8), :] = jnp.dot(
            wsel, pltpu.bitcast(g_ref[...], bf16), preferred_element_type=f32)

    _gather_pipeline(idx_ref, tab_ref, tiles, tb, combine)


PACK_ROWS = 512


def _pack_table_kernel(x_ref, o_ref):
    rows = x_ref.shape[0]
    bf16_bits = lambda v: lax.bitcast_convert_type(
        v.astype(jnp.bfloat16).astype(jnp.float32), jnp.int32)
    for c in range(WORDS_PER_ROW):
        lo = bf16_bits(x_ref[:, (2 * c) * 128:(2 * c + 1) * 128])
        hi = bf16_bits(x_ref[:, (2 * c + 1) * 128:(2 * c + 2) * 128])
        word = (hi & jnp.int32(-65536)) | lax.shift_right_logical(lo, 16)
        o_ref[pl.ds(c, rows, stride=WORDS_PER_ROW), :] = word


def _pack_table(tab):
    n_e, d = tab.shape
    return pl.pallas_call(
        _pack_table_kernel,
        grid=(n_e // PACK_ROWS,),
        in_specs=[pl.BlockSpec((PACK_ROWS, d), lambda i: (i, 0))],
        out_specs=pl.BlockSpec((PACK_ROWS * WORDS_PER_ROW, 128), lambda i: (i, 0)),
        out_shape=jax.ShapeDtypeStruct((n_e * WORDS_PER_ROW, 128), jnp.int32),
        name="pack_table",
    )(tab)


def peer_pallas(x, g_ffn, w_pq, sub_keys, u_emb, v_emb):
    n_tok, d = x.shape
    f32, bf16 = jnp.float32, jnp.bfloat16
    ts, tg = PEER_SCORE_TOKENS, PEER_GATHER_TOKENS
    wpqt = w_pq.reshape(d, P_HEADS * P_QDIM).T.astype(bf16)
    sk = sub_keys.reshape(P_HEADS * 2, N_KEYS, P_QDIM // 2).astype(bf16)
    hn, eidx, gate = pl.pallas_call(
        _peer_score_kernel,
        grid=(n_tok // ts,),
        in_specs=[pl.BlockSpec((ts, d), lambda i: (i, 0)),
                  pl.BlockSpec((1, d), lambda i: (0, 0)),
                  pl.BlockSpec(wpqt.shape, lambda i: (0, 0)),
                  pl.BlockSpec(sk.shape, lambda i: (0, 0, 0))],
        out_specs=[pl.BlockSpec((ts * 8, 128), lambda i: (i, 0)),
                   pl.BlockSpec((ts, PEER_SLOTS), lambda i: (i, 0)),
                   pl.BlockSpec((ts, PEER_SLOTS), lambda i: (i, 0))],
        out_shape=[jax.ShapeDtypeStruct((n_tok * 8, 128), f32),
                   jax.ShapeDtypeStruct((n_tok, PEER_SLOTS), jnp.int32),
                   jax.ShapeDtypeStruct((n_tok, PEER_SLOTS), f32)],
        compiler_params=pltpu.CompilerParams(vmem_limit_bytes=VMEM_LIMIT),
        name="peer_score",
    )(x, g_ffn.reshape(1, d), wpqt, sk)
    utab, vtab = _pack_table(u_emb), _pack_table(v_emb)
    idx_spec = pl.BlockSpec((tg, PEER_SLOTS), lambda i: (i, 0), memory_space=pltpu.SMEM)
    tab_spec = pl.BlockSpec(memory_space=pltpu.VMEM)
    slot_spec = pl.BlockSpec((tg, PEER_SLOTS), lambda i: (i, 0))
    row_spec = pl.BlockSpec((tg * 8, 128), lambda i: (i, 0))
    gbuf = pltpu.VMEM((PEER_SLOTS * WORDS_PER_ROW, 128), jnp.int32)
    cparams = pltpu.CompilerParams(vmem_limit_bytes=VMEM_LIMIT)
    act = pl.pallas_call(
        _peer_act_kernel,
        grid=(n_tok // tg,),
        in_specs=[idx_spec, row_spec, tab_spec],
        out_specs=slot_spec,
        out_shape=jax.ShapeDtypeStruct((n_tok, PEER_SLOTS), f32),
        scratch_shapes=[pltpu.VMEM((tg * 8, PEER_SLOTS * 8), f32)] + [gbuf] * PEER_TILES,
        compiler_params=cparams,
        name="peer_act",
    )(eidx, hn, utab)
    out = pl.pallas_call(
        _peer_out_kernel,
        grid=(n_tok // tg,),
        in_specs=[idx_spec, slot_spec, slot_spec, tab_spec],
        out_specs=row_spec,
        out_shape=jax.ShapeDtypeStruct((n_tok * 8, 128), f32),
        scratch_shapes=[pltpu.VMEM((tg, PEER_SLOTS * 8), f32)] + [gbuf] * PEER_TILES,
        compiler_params=cparams,
        name="peer_out",
    )(eidx, act, gate, vtab)
    return out


def _token_rows(y_ref):
    tm = y_ref.shape[0] // 8
    return jnp.concatenate([y_ref[pl.ds(s, tm, stride=8), :] for s in range(8)], axis=1)


def _final_kernel(x_ref, y_ref, g_ref, o_ref):
    o_ref[...] = _rms(x_ref[...] + _token_rows(y_ref), g_ref[...])


def _residual_kernel(x_ref, y_ref, o_ref):
    o_ref[...] = x_ref[...] + _token_rows(y_ref)


def residual_pallas(x, y8, g=None):
    n, d = x.shape
    tm = 512
    row = pl.BlockSpec((tm, d), lambda i: (i, 0))
    row8 = pl.BlockSpec((tm * 8, 128), lambda i: (i, 0))
    if g is None:
        kern, extra, extra_specs, name = _residual_kernel, (), [], "residual"
    else:
        kern, extra, name = _final_kernel, (g.reshape(1, d),), "final_rmsnorm"
        extra_specs = [pl.BlockSpec((1, d), lambda i: (0, 0))]
    return pl.pallas_call(
        kern,
        grid=(n // tm,),
        in_specs=[row, row8] + extra_specs,
        out_specs=row,
        out_shape=jax.ShapeDtypeStruct((n, d), x.dtype),
        name=name,
    )(x, y8, *extra)


IN_PROJ_TOKENS = 512
GDN_COLS = 3 * B_QK
GATE_ROWS = 8


def _in_proj_kernel(x_ref, g_ref, wa_ref, wkw_ref, wqkv_ref, wz_ref, wab_ref, wabt_ref,
                    cq_ref, ckv_ref, kw_ref, qkv_ref, z_ref, ab_ref, abt_ref):
    f32 = jnp.float32
    hb = _rms(x_ref[0], g_ref[...]).astype(jnp.bfloat16)
    a = jnp.dot(hb, wa_ref[...], preferred_element_type=f32)
    cq_ref[0] = a[:, :A_Q_LORA]
    ckv_ref[0] = a[:, A_Q_LORA:]
    kw_ref[0] = jnp.dot(hb, wkw_ref[...], preferred_element_type=f32)
    qkv_ref[0] = jnp.dot(hb, wqkv_ref[...], preferred_element_type=f32)
    z_ref[0] = jnp.dot(hb, wz_ref[...], preferred_element_type=f32)
    ab_ref[0] = jnp.dot(hb, wab_ref[...], preferred_element_type=f32)
    abt_ref[0] = _nt_dot(wabt_ref[...], hb)


def in_proj_pallas(x, g_mix, w_in):
    bsz, seq, d = x.shape
    f32, bf16 = jnp.float32, jnp.bfloat16
    tm = min(IN_PROJ_TOKENS, seq)
    o = np.cumsum((0,) + COL_WIDTHS)
    wb = w_in.astype(bf16)
    pad_cols = lambda w: jnp.pad(w, ((0, 0), (0, 128 - w.shape[1])))
    wa = wb[:, o[0]:o[2]]
    wkw = pad_cols(wb[:, o[2]:o[4]])
    wqkv = wb[:, o[4]:o[7]]
    wz = wb[:, o[7]:o[8]]
    wab = pad_cols(wb[:, o[8]:o[10]])
    wabt = jnp.concatenate([wb[:, o[8]:o[10]], wb[:, o[3]:o[4]]], axis=1).T
    full = lambda w: pl.BlockSpec(w.shape, lambda b, t: (0, 0))
    tok = lambda w: pl.BlockSpec((1, tm, w), lambda b, t: (b, t, 0))
    shp = lambda w: jax.ShapeDtypeStruct((bsz, seq, w), f32)
    return pl.pallas_call(
        _in_proj_kernel,
        grid=(bsz, seq // tm),
        in_specs=[tok(d), pl.BlockSpec((1, d), lambda b, t: (0, 0)),
                  full(wa), full(wkw), full(wqkv), full(wz), full(wab), full(wabt)],
        out_specs=[tok(A_Q_LORA), tok(A_KV_LORA), tok(128), tok(GDN_COLS), tok(B_QK), tok(128),
                   pl.BlockSpec((1, GATE_ROWS + IDX_HEADS, tm), lambda b, t: (b, 0, t))],
        out_shape=[shp(A_Q_LORA), shp(A_KV_LORA), shp(128), shp(GDN_COLS), shp(B_QK), shp(128),
                   jax.ShapeDtypeStruct((bsz, GATE_ROWS + IDX_HEADS, seq), f32)],
        compiler_params=pltpu.CompilerParams(vmem_limit_bytes=VMEM_LIMIT),
        name="in_proj",
    )(x, g_mix.reshape(1, d), wa, wkw, wqkv, wz, wab, wabt)


def _softplus(x):
    return jnp.maximum(x, 0.0) + jnp.log1p(jnp.exp(-jnp.abs(x)))


def _sigmoid(x):
    return 1.0 / (1.0 + jnp.exp(-x))


def _gdn_gates(pre, a_log, dt_bias, is_decay):
    g = -jnp.exp(a_log) * _softplus(pre + dt_bias)
    return jnp.where(is_decay, g, _sigmoid(pre))


def _gdn_prep_kernel(qkv_ref, halo_ref, cw_ref, ab_ref, abt_ref, alc_ref, dtc_ref, alr_ref, dtr_ref,
                     q_ref, k_ref, v_ref, gc_ref, gr_ref):
    tm = qkv_ref.shape[1]
    x = qkv_ref[0]
    halo = jnp.where(pl.program_id(1) > 0, halo_ref[0], 0.0)
    full = jnp.concatenate([halo, x], axis=0)
    y = x * cw_ref[CONV_WIDTH - 1:CONV_WIDTH, :]
    for back in range(1, CONV_WIDTH):
        shifted = pltpu.roll(full, back, axis=0)[8:, :]
        y = y + shifted * cw_ref[CONV_WIDTH - 1 - back:CONV_WIDTH - back, :]
    y = y * _sigmoid(y)
    for h in range(B_HEADS):
        cols = slice(h * B_HEAD_DIM, (h + 1) * B_HEAD_DIM)
        qh = y[:, h * B_HEAD_DIM:(h + 1) * B_HEAD_DIM]
        kh = y[:, B_QK + h * B_HEAD_DIM:B_QK + (h + 1) * B_HEAD_DIM]
        q_ref[0, :, cols] = qh * lax.rsqrt(
            jnp.sum(qh * qh, axis=-1, keepdims=True) + EPS) * (B_HEAD_DIM ** -0.5)
        k_ref[0, :, cols] = kh * lax.rsqrt(jnp.sum(kh * kh, axis=-1, keepdims=True) + EPS)
    v_ref[0] = y[:, 2 * B_QK:]
    lane = lax.broadcasted_iota(jnp.int32, (tm, 128), 1)
    gates_c = _gdn_gates(ab_ref[0], alc_ref[...], dtc_ref[...], lane < B_HEADS)
    row = lax.broadcasted_iota(jnp.int32, (GATE_ROWS, tm), 0)
    gates_r = _gdn_gates(abt_ref[0], alr_ref[...], dtr_ref[...], row < B_HEADS)
    ti = lax.broadcasted_iota(jnp.int32, (tm, tm), 0)
    tj = lax.broadcasted_iota(jnp.int32, (tm, tm), 1)
    same_chunk = (ti // CHUNK) == (tj // CHUNK)
    hi = lax.Precision.HIGHEST
    lower = jnp.where(same_chunk & (tj <= ti), 1.0, 0.0)
    upper = jnp.where(same_chunk & (ti <= tj), 1.0, 0.0)
    cum_c = jnp.dot(lower, gates_c, preferred_element_type=jnp.float32, precision=hi)
    cum_r = jnp.dot(gates_r, upper, preferred_element_type=jnp.float32, precision=hi)
    gc_ref[0] = jnp.where(lane < B_HEADS, cum_c, gates_c)
    gr_ref[0] = jnp.where(row < B_HEADS, cum_r, gates_r)


def _gdn_main_kernel(q_ref, k_ref, v_ref, z_ref, gc_ref, gr_ref, gon_ref, o_ref, s_ref):
    f32, bf16 = jnp.float32, jnp.bfloat16
    C, Dh = CHUNK, B_HEAD_DIM

    @pl.when(pl.program_id(1) == 0)
    def _():
        s_ref[...] = jnp.zeros(s_ref.shape, f32)

    ii = lax.broadcasted_iota(jnp.int32, (C, C), 0)
    jj = lax.broadcasted_iota(jnp.int32, (C, C), 1)
    causal = ii >= jj
    strict = ii > jj
    eye = jnp.where(ii == jj, 1.0, 0.0)
    mm = lambda a, b: jnp.dot(a, b, preferred_element_type=f32)
    n_chunks = q_ref.shape[1] // C
    units = []
    for c in range(n_chunks):
        rows = slice(c * C, (c + 1) * C)
        gates_c = gc_ref[0, rows, :]
        gates_r = gr_ref[0, :, c * C:(c + 1) * C]
        for h in range(B_HEADS):
            cols = slice(h * Dh, (h + 1) * Dh)
            gcum = jnp.broadcast_to(gates_c[:, h:h + 1], (C, Dh))
            beta = jnp.broadcast_to(gates_c[:, B_HEADS + h:B_HEADS + h + 1], (C, Dh))
            gcum_r = jnp.broadcast_to(gates_r[h:h + 1, :], (C, C))
            decay = jnp.where(causal, jnp.exp(jnp.minimum(gcum[:, :C] - gcum_r, 0.0)), 0.0)
            q, k, v = q_ref[0, rows, cols], k_ref[0, rows, cols], v_ref[0, rows, cols]
            qb, kb = q.astype(bf16), k.astype(bf16)
            kk = _nt_dot(kb, kb)
            qk = _nt_dot(qb, kb)
            egc = jnp.exp(gcum)
            g_last = gcum[C - 1:C, :]
            units.append(dict(
                c=c, h=h, rows=rows, cols=cols,
                neg_m=jnp.where(strict, -(beta[:, :C] * kk * decay), 0.0),
                rhs=jnp.concatenate([v * beta, k * (beta * egc)], axis=1),
                q_dec=(q * egc).astype(bf16), intra=(qk * decay).astype(bf16),
                k_tail=(k * jnp.exp(g_last - gcum)).astype(bf16), chunk_decay=jnp.exp(g_last)))
    def halves(a):
        a_h = a.astype(bf16)
        return a_h, (a - a_h.astype(f32)).astype(bf16)

    def mm3(a, b):
        (a_h, a_l), (b_h, b_l) = a, b
        return mm(a_h, b_h) + (mm(a_h, b_l) + mm(a_l, b_h))

    t_inv = [eye + un["neg_m"] for un in units]
    p_halves = [halves(un["neg_m"]) for un in units]
    for _ in range(5):
        p_halves = [halves(mm3(ph, ph)) for ph in p_halves]
        t_inv = [t + mm3(ph, halves(t)) for ph, t in zip(p_halves, t_inv)]
    sols = [mm3(halves(t), halves(un["rhs"])) for t, un in zip(t_inv, units)]
    states = [s_ref[h] for h in range(B_HEADS)]
    for c in range(n_chunks):
        group = [(un, sol) for un, sol in zip(units, sols) if un["c"] == c]
        sbs = [states[un["h"]].astype(bf16) for un, _ in group]
        vbs = [(sol[:, :Dh] - mm(sol[:, Dh:].astype(bf16), sb)).astype(bf16)
               for (un, sol), sb in zip(group, sbs)]
        outs = [mm(un["q_dec"], sb) + mm(un["intra"], vb)
                for (un, _), sb, vb in zip(group, sbs, vbs)]
        for (un, _), vb in zip(group, vbs):
            states[un["h"]] = states[un["h"]] * un["chunk_decay"] + lax.dot_general(
                un["k_tail"], vb, (((0,), (0,)), ((), ())), preferred_element_type=f32)
        for (un, _), o in zip(group, outs):
            o = o * lax.rsqrt(jnp.mean(o * o, axis=-1, keepdims=True) + EPS) * gon_ref[...]
            zz = z_ref[0, un["rows"], un["cols"]]
            o_ref[0, un["rows"], un["cols"]] = o * (zz * _sigmoid(zz))
    for h in range(B_HEADS):
        s_ref[h] = states[h]


def gdn_pallas(qkv, z, ab, abt, conv_w, a_log, dt_bias, g_onorm):
    bsz, seq, _ = qkv.shape
    f32 = jnp.float32
    tm = min(256, seq)
    zero4 = jnp.zeros((B_HEADS,), f32)
    lane_row = lambda v: jnp.pad(jnp.concatenate([v.astype(f32), zero4]), (0, 120)).reshape(1, 128)
    sub_col = lambda v: jnp.concatenate([v.astype(f32), zero4]).reshape(GATE_ROWS, 1)
    tok = lambda w, t=tm: pl.BlockSpec((1, t, w), lambda b, i: (b, i, 0))
    const = lambda shape: pl.BlockSpec(shape, lambda b, i: (0,) * len(shape))
    shp = lambda w: jax.ShapeDtypeStruct((bsz, seq, w), f32)
    q, k, v, gc, gr = pl.pallas_call(
        _gdn_prep_kernel,
        grid=(bsz, seq // tm),
        in_specs=[tok(GDN_COLS),
                  pl.BlockSpec((1, 8, GDN_COLS), lambda b, i: (b, jnp.maximum(i * (tm // 8) - 1, 0), 0)),
                  const((CONV_WIDTH, GDN_COLS)), tok(128),
                  pl.BlockSpec((1, GATE_ROWS, tm), lambda b, i: (b, 0, i)),
                  const((1, 128)), const((1, 128)), const((GATE_ROWS, 1)), const((GATE_ROWS, 1))],
        out_specs=[tok(B_QK), tok(B_QK), tok(B_QK), tok(128),
                   pl.BlockSpec((1, GATE_ROWS, tm), lambda b, i: (b, 0, i))],
        out_shape=[shp(B_QK), shp(B_QK), shp(B_QK), shp(128),
                   jax.ShapeDtypeStruct((bsz, GATE_ROWS, seq), f32)],
        compiler_params=pltpu.CompilerParams(vmem_limit_bytes=VMEM_LIMIT),
        name="gdn_prep",
    )(qkv, qkv, conv_w.astype(f32), ab, abt, lane_row(a_log), lane_row(dt_bias),
      sub_col(a_log), sub_col(dt_bias))
    ts = 2 * CHUNK
    return pl.pallas_call(
        _gdn_main_kernel,
        grid=(bsz, seq // ts),
        in_specs=[tok(B_QK, ts), tok(B_QK, ts), tok(B_QK, ts), tok(B_QK, ts), tok(128, ts),
                  pl.BlockSpec((1, GATE_ROWS, ts), lambda b, i: (b, 0, i)),
                  const((1, B_HEAD_DIM))],
        out_specs=tok(B_QK, ts),
        out_shape=shp(B_QK),
        scratch_shapes=[pltpu.VMEM((B_HEADS, B_HEAD_DIM, B_HEAD_DIM), f32)],
        compiler_params=pltpu.CompilerParams(dimension_semantics=("arbitrary", "arbitrary")),
        name="gdn_main",
    )(q, k, v, z, gc, gr, g_onorm.astype(f32).reshape(1, B_HEAD_DIM))


def _mem_kv_kernel(mem_ref, g_ref, wk_ref, wv_ref, k_ref, v_ref):
    f32, bf16 = jnp.float32, jnp.bfloat16
    mn = _rms(mem_ref[0], g_ref[...]).astype(bf16)
    k = jnp.dot(mn, wk_ref[...], preferred_element_type=f32)
    v = jnp.dot(mn, wv_ref[...], preferred_element_type=f32)
    for h in range(X_HEADS):
        cols = slice(h * X_HEAD_DIM, (h + 1) * X_HEAD_DIM)
        k_ref[0, h] = k[:, cols].astype(bf16)
        v_ref[0, h] = v[:, cols].astype(bf16)


def _mid_kernel(x_ref, oa_ref, ob_ref, wo_ref, gx_ref, wq_ref, k_ref, v_ref, wox_ref, o_ref):
    f32, bf16 = jnp.float32, jnp.bfloat16
    na = oa_ref.shape[2]
    x1 = (x_ref[0]
          + jnp.dot(oa_ref[0].astype(bf16), wo_ref[:na, :], preferred_element_type=f32)
          + jnp.dot(ob_ref[0].astype(bf16), wo_ref[na:, :], preferred_element_type=f32))
    hq = _rms(x1, gx_ref[...]).astype(bf16)
    q = jnp.dot(hq, wq_ref[...], preferred_element_type=f32)
    heads = []
    for h in range(X_HEADS):
        qh = q[:, h * X_HEAD_DIM:(h + 1) * X_HEAD_DIM].astype(bf16)
        lg = _nt_dot(qh, k_ref[0, h]) * (X_HEAD_DIM ** -0.5)
        p = jnp.exp(lg - jnp.max(lg, axis=-1, keepdims=True))
        p = (p / jnp.sum(p, axis=-1, keepdims=True)).astype(bf16)
        heads.append(jnp.dot(p, v_ref[0, h], preferred_element_type=f32).astype(bf16))
    o = jnp.concatenate(heads, axis=1)
    o_ref[0] = x1 + jnp.dot(o, wox_ref[...], preferred_element_type=f32)


def mid_pallas(x, o_a, o_b, w_out, g_cross, mem, g_mem, wq, wk, wv, wo):
    bsz, seq, d = x.shape
    f32, bf16 = jnp.float32, jnp.bfloat16
    hx = X_HEADS * X_HEAD_DIM
    m_len = mem.shape[1]
    const2 = lambda shape: pl.BlockSpec(shape, lambda b: (0,) * len(shape))
    kv_spec = pl.BlockSpec((1, X_HEADS, m_len, X_HEAD_DIM), lambda b: (b, 0, 0, 0))
    kv_shape = jax.ShapeDtypeStruct((bsz, X_HEADS, m_len, X_HEAD_DIM), bf16)
    k, v = pl.pallas_call(
        _mem_kv_kernel,
        grid=(bsz,),
        in_specs=[pl.BlockSpec((1, m_len, d), lambda b: (b, 0, 0)), const2((1, d)),
                  const2((d, hx)), const2((d, hx))],
        out_specs=[kv_spec, kv_spec],
        out_shape=[kv_shape, kv_shape],
        name="mem_kv",
    )(mem, g_mem.reshape(1, d), wk.reshape(d, hx).astype(bf16), wv.reshape(d, hx).astype(bf16))
    tm = min(512, seq)
    tok = lambda w: pl.BlockSpec((1, tm, w), lambda b, t: (b, t, 0))
    const = lambda shape: pl.BlockSpec(shape, lambda b, t: (0,) * len(shape))
    kv_spec2 = pl.BlockSpec((1, X_HEADS, m_len, X_HEAD_DIM), lambda b, t: (b, 0, 0, 0))
    return pl.pallas_call(
        _mid_kernel,
        grid=(bsz, seq // tm),
        in_specs=[tok(d), tok(o_a.shape[2]), tok(o_b.shape[2]), const((MIX_WIDTH, d)),
                  const((1, d)), const((d, hx)), kv_spec2, kv_spec2, const((hx, d))],
        out_specs=tok(d),
        out_shape=jax.ShapeDtypeStruct((bsz, seq, d), f32),
        compiler_params=pltpu.CompilerParams(vmem_limit_bytes=VMEM_LIMIT),
        name="mid",
    )(x, o_a, o_b, w_out.astype(bf16), g_cross.reshape(1, d), wq.reshape(d, hx).astype(bf16),
      k, v, wo.reshape(hx, d).astype(bf16))


def kernel(x, mem, g_mix, w_in, g_cq, g_ckv, g_kidx, w_uq, w_iq, w_uk, w_uv, rel_bias, conv_w, A_log, dt_bias, g_onorm, w_out, g_cross, g_mem, wq_x, wk_x, wv_x, wo_x, g_ffn, w_pq, sub_keys, u_emb, v_emb, g_final):
    bsz, seq, d = x.shape
    for l in range(DEPTH):
        cq, ckv, kw, qkv, z, ab, abt = in_proj_pallas(x, g_mix[l], w_in[l])
        o_a = dsa_pallas(cq, ckv, kw, abt, g_cq[l], g_ckv[l], g_kidx[l],
                         w_uq[l], w_iq[l], w_uk[l], w_uv[l], rel_bias)
        o_b = gdn_pallas(qkv, z, ab, abt, conv_w[l], A_log[l], dt_bias[l], g_onorm[l])
        x = mid_pallas(x, o_a, o_b, w_out[l], g_cross[l], mem, g_mem[l],
                       wq_x[l], wk_x[l], wv_x[l], wo_x[l])
        xf = x.reshape(bsz * seq, d)
        y8 = peer_pallas(xf, g_ffn[l], w_pq[l], sub_keys[l], u_emb[l], v_emb[l])
        if l + 1 < DEPTH:
            x = residual_pallas(xf, y8).reshape(bsz, seq, d)
    return residual_pallas(xf, y8, g_final).reshape(bsz, seq, d)
```

```python
import math
from functools import partial
import jax
import jax.numpy as jnp
from jax import lax
import numpy as np
from jax.experimental import pallas as pl
from jax.experimental.pallas import tpu as pltpu

DEPTH = 1

CHUNK = 64
Q_BLOCK = 128
EPS = 1e-6

A_HEADS = 8
A_HEAD_DIM = 64
A_Q_LORA = 256
A_KV_LORA = 256
IDX_HEADS = 8
IDX_DIM = 64
IDX_TOPK_MAX = 256
ATTN_SCALE = A_HEAD_DIM ** -0.5
IDX_SCALE = (IDX_HEADS * IDX_DIM) ** -0.5

B_HEADS = 4
B_HEAD_DIM = 128
B_QK = B_HEADS * B_HEAD_DIM
CONV_WIDTH = 4

REL_BUCKETS = 32
REL_MAX_DIST = 128

X_HEADS = 4
X_HEAD_DIM = 128

P_HEADS = 8
N_KEYS = 128
N_EXPERTS = N_KEYS * N_KEYS
P_TOPK = 16
P_QDIM = 256

COL_WIDTHS = (A_Q_LORA, A_KV_LORA, IDX_DIM, IDX_HEADS, B_QK, B_QK, B_QK, B_QK, B_HEADS, B_HEADS)
MIX_WIDTH = A_HEADS * A_HEAD_DIM + B_HEADS * B_HEAD_DIM


def t5_bucket(rel):
    half = REL_BUCKETS // 2
    max_exact = half // 2
    n = jnp.abs(rel)
    nf = jnp.maximum(n, max_exact).astype(jnp.float32)
    large = max_exact + (jnp.log(nf / max_exact) / math.log(REL_MAX_DIST / max_exact)
                         * (half - max_exact)).astype(jnp.int32)
    large = jnp.minimum(large, half - 1)
    return jnp.where(rel > 0, half, 0) + jnp.where(n < max_exact, n, large)


INT_MIN = -2147483648
NEG_BIG = -1e30
KEY_TILE = 512
KEY_PAD = KEY_TILE - Q_BLOCK
DSA_PREP_BLOCKS = 4
VMEM_LIMIT = 56 * 1024 * 1024


def _rms(x, g):
    return x * lax.rsqrt(jnp.mean(x * x, axis=-1, keepdims=True) + EPS) * g


def _nt_dot(a, b):
    return lax.dot_general(a, b, (((1,), (1,)), ((), ())), preferred_element_type=jnp.float32)


def _dsa_prep_kernel(cq_ref, ckv_ref, kw_ref, gcq_ref, gckv_ref, gk_ref,
                     wiqt_ref, wuqt_ref, wuk_ref,
                     at_ref, qlatt_ref, ckvn_ref, kidxn_ref):
    f32, bf16 = jnp.float32, jnp.bfloat16
    Q = Q_BLOCK
    kw = kw_ref[0]
    cqn = _rms(cq_ref[0], gcq_ref[...]).astype(bf16)
    qit = _nt_dot(wiqt_ref[...], cqn)
    qt = _nt_dot(wuqt_ref[...], cqn)
    for blk in range(at_ref.shape[1]):
        toks = slice(blk * Q, (blk + 1) * Q)
        for h in range(A_HEADS):
            cols = slice(h * Q, (h + 1) * Q)
            at_ref[0, blk, :, cols] = qit[h * IDX_DIM:(h + 1) * IDX_DIM, toks].astype(bf16)
            qh = qt[h * A_HEAD_DIM:(h + 1) * A_HEAD_DIM, toks].astype(bf16)
            ql = jnp.dot(wuk_ref[h], qh, preferred_element_type=f32) * ATTN_SCALE
            qlatt_ref[0, blk, :, cols] = ql.astype(bf16)
    ckvn_ref[0] = _rms(ckv_ref[0], gckv_ref[...]).astype(bf16)
    kidxn_ref[0] = _rms(kw[:, :IDX_DIM], gk_ref[...]).astype(bf16)


def _dsa_main_kernel(at_ref, qlatt_ref, wt_ref, kidx_ref, ckv_ref, bucket_ref, far_ref, rb_ref,
                     wuvt_ref, o_ref, bias0_ref, sc_ref, acc_ref, m_ref, l_ref, *, topk):
    f32, i32, bf16 = jnp.float32, jnp.int32, jnp.bfloat16
    Q, T = Q_BLOCK, KEY_TILE
    i = pl.program_id(1)

    @pl.when(jnp.logical_and(pl.program_id(0) == 0, i == 0))
    def _():
        bucket = bucket_ref[...]
        for h in range(A_HEADS):
            far_bias = rb_ref[far_ref[0], h]
            bias = jnp.zeros((T, Q), f32)
            for b in range(REL_BUCKETS):
                bias = jnp.where(bucket == b, rb_ref[b, h] - far_bias, bias)
            bias0_ref[h] = bias
    e = (i + 1) * Q
    nt = (i + T // Q) // (T // Q)
    kf = jnp.float32(topk)
    wsc = wt_ref[0] * IDX_SCALE
    sub = lax.broadcasted_iota(i32, (T, Q), 0)
    qlane = lax.broadcasted_iota(i32, (T, Q), 1)
    limit = ((i * Q + qlane) // CHUNK + 1) * CHUNK
    at = at_ref[0, 0]

    def tile_start(j):
        return pl.multiple_of(e + KEY_PAD - (j + 1) * T, 128)

    def key_pos(j):
        return tile_start(j) - KEY_PAD + sub

    def score_tile(j, carry):
        kid = kidx_ref[0, pl.ds(tile_start(j), T), :]
        s = jnp.zeros((T, Q), f32)
        for g in range(IDX_HEADS // 2):
            d = jnp.dot(kid, at[:, g * 2 * Q:(g + 1) * 2 * Q], preferred_element_type=f32)
            for u in range(2):
                h = 2 * g + u
                s = s + wsc[h:h + 1, :] * jnp.maximum(d[:, u * Q:(u + 1) * Q], 0.0)
        bits = lax.bitcast_convert_type(s, i32)
        key = jnp.where(bits < 0, bits ^ jnp.int32(0x7FFFFFFF), bits)
        key = jnp.where(s == 0.0, 0, key)
        kpos = key_pos(j)
        key = jnp.where(kpos >= 0, key, INT_MIN)
        sc_ref[j] = jnp.where(kpos < limit, key, INT_MIN)
        return carry

    lax.fori_loop(0, nt, score_tile, 0)

    def count_where(pred):
        def body(j, cnt):
            hit = pred(sc_ref[j], key_pos(j))
            for part in range(T // 64):
                rows = slice(part * 64, (part + 1) * 64)
                cnt = jnp.where(hit[rows], cnt + 1.0, cnt)
            return cnt
        cnt = lax.fori_loop(0, nt, body, jnp.zeros((64, Q), f32))
        return jnp.sum(cnt, axis=0, keepdims=True)

    def bit_body(b, carry):
        u, cacc = carry
        cand = u | lax.shift_left(jnp.int32(1), 31 - b)
        tvec = cand ^ jnp.int32(INT_MIN)
        tot = count_where(lambda k, kpos: k >= tvec)
        ok = tot >= kf
        return jnp.where(ok, cand, u), jnp.where(ok, tot, cacc)

    u, cacc = lax.fori_loop(0, 32, bit_body, (jnp.zeros((1, Q), i32), jnp.zeros((1, Q), f32)))
    thr = jnp.maximum(u ^ jnp.int32(INT_MIN), INT_MIN + 1)
    overflow = jnp.where(u != 0, cacc, 0.0) > kf
    n_over = jnp.max(jnp.where(overflow, 1.0, 0.0), axis=1, keepdims=True)[0, 0]

    @pl.when(n_over > 0.0)
    def _():
        need = kf - count_where(lambda k, kpos: k > thr)

        def cut_body(b, cut):
            cand = cut | lax.shift_left(jnp.int32(1), 14 - b)
            cnt = count_where(lambda k, kpos: jnp.where(k == thr, kpos, cand) < cand)
            return jnp.where(cnt <= need, cand, cut)

        cut = lax.fori_loop(0, 15, cut_body, jnp.zeros((1, Q), i32))

        def drop_tile(j, carry):
            k = sc_ref[j]
            drop = jnp.where(k == thr, key_pos(j), -1) >= cut
            sc_ref[j] = jnp.where(drop, INT_MIN, k)
            return carry

        lax.fori_loop(0, nt, drop_tile, 0)

    m_ref[...] = jnp.full(m_ref.shape, NEG_BIG, f32)
    l_ref[...] = jnp.zeros(l_ref.shape, f32)
    acc_ref[...] = jnp.zeros(acc_ref.shape, f32)
    qlatt = qlatt_ref[0, 0]

    def attn_tile(j, with_bias):
        kv = ckv_ref[0, pl.ds(tile_start(j), T), :]
        kvt = kv.T
        mask_add = jnp.where(sc_ref[j] >= thr, 0.0, NEG_BIG)
        pair_cols = [slice(g * 2 * Q, (g + 1) * 2 * Q) for g in range(A_HEADS // 2)]
        logits = lambda g: jnp.dot(kv, qlatt[:, pair_cols[g]], preferred_element_type=f32)
        x_next = logits(0)
        for g in range(A_HEADS // 2):
            x_pair = x_next
            if g + 1 < A_HEADS // 2:
                x_next = logits(g + 1)
            ps, alphas = [], []
            for u in range(2):
                h = 2 * g + u
                cols = slice(h * Q, (h + 1) * Q)
                x = x_pair[:, u * Q:(u + 1) * Q] + mask_add
                if with_bias:
                    x = x + bias0_ref[h]
                m_prev = m_ref[:, cols]
                m_new = jnp.maximum(m_prev, jnp.max(x, axis=0, keepdims=True))
                p = jnp.exp(x - m_new)
                alpha = jnp.exp(m_prev - m_new)
                l_ref[:, cols] = alpha * l_ref[:, cols] + jnp.sum(p, axis=0, keepdims=True)
                m_ref[:, cols] = m_new
                ps.append(p.astype(bf16))
                alphas.append(alpha)
            pv = jnp.dot(kvt, jnp.concatenate(ps, axis=1), preferred_element_type=f32)
            acc_ref[:, pair_cols[g]] = (
                jnp.concatenate(alphas, axis=1) * acc_ref[:, pair_cols[g]] + pv)

    attn_tile(0, True)

    def attn_body(j, carry):
        attn_tile(j, False)
        return carry

    lax.fori_loop(1, nt, attn_body, 0)

    inv_l = 1.0 / l_ref[...]
    outs = []
    for h in range(A_HEADS):
        cols = slice(h * Q, (h + 1) * Q)
        o_lat_t = (acc_ref[:, cols] * inv_l[:, cols]).astype(bf16)
        outs.append(jnp.dot(wuvt_ref[h], o_lat_t, preferred_element_type=f32))
    o_ref[0] = jnp.concatenate(outs, axis=0).T


def dsa_pallas(cq, ckv, kw, widx_t, g_cq, g_ckv, g_kidx, w_uq, w_iq, w_uk, w_uv, rel_bias):
    bsz, seq, _ = cq.shape
    f32, bf16 = jnp.float32, jnp.bfloat16
    Q, T, H = Q_BLOCK, KEY_TILE, A_HEADS
    nblk = seq // Q
    topk = min(IDX_TOPK_MAX, seq // 4)
    wiqt = w_iq.reshape(A_Q_LORA, IDX_HEADS * IDX_DIM).T.astype(bf16)
    wuqt = w_uq.reshape(A_Q_LORA, H * A_HEAD_DIM).T.astype(bf16)
    wuk = jnp.transpose(w_uk, (1, 0, 2)).astype(bf16)
    wuvt = jnp.transpose(w_uv, (1, 2, 0)).astype(bf16)
    tok = lambda w: pl.BlockSpec((1, Q, w), lambda b, t: (b, t, 0))
    full = lambda shape: pl.BlockSpec(shape, lambda b, t: (0,) * len(shape))
    blk = lambda r: pl.BlockSpec((1, 1, r, H * Q), lambda b, t: (b, t, 0, 0))
    nb = math.gcd(DSA_PREP_BLOCKS, nblk)
    ptok = lambda w: pl.BlockSpec((1, nb * Q, w), lambda b, t: (b, t, 0))
    pblk = lambda r: pl.BlockSpec((1, nb, r, H * Q), lambda b, t: (b, t, 0, 0))
    a_t, qlat_t, ckvn, kidxn = pl.pallas_call(
        _dsa_prep_kernel,
        grid=(bsz, nblk // nb),
        in_specs=[ptok(A_Q_LORA), ptok(A_KV_LORA), ptok(128),
                  full((1, A_Q_LORA)), full((1, A_KV_LORA)), full((1, IDX_DIM)),
                  full(wiqt.shape), full(wuqt.shape), full(wuk.shape)],
        out_specs=[pblk(IDX_DIM), pblk(A_KV_LORA), ptok(A_KV_LORA), ptok(IDX_DIM)],
        out_shape=[jax.ShapeDtypeStruct((bsz, nblk, IDX_DIM, H * Q), bf16),
                   jax.ShapeDtypeStruct((bsz, nblk, A_KV_LORA, H * Q), bf16),
                   jax.ShapeDtypeStruct((bsz, seq, A_KV_LORA), bf16),
                   jax.ShapeDtypeStruct((bsz, seq, IDX_DIM), bf16)],
        name="dsa_prep",
    )(cq, ckv, kw, g_cq.reshape(1, -1), g_ckv.reshape(1, -1), g_kidx.reshape(1, -1),
      wiqt, wuqt, wuk)
    ckvp = jnp.pad(ckvn, ((0, 0), (KEY_PAD, 0), (0, 0)))
    kidxp = jnp.pad(kidxn, ((0, 0), (KEY_PAD, 0), (0, 0)))
    half, max_exact = REL_BUCKETS // 2, REL_BUCKETS // 4
    saturation = max_exact * (REL_MAX_DIST / max_exact) ** ((half - 1 - max_exact) / (half - max_exact))
    assert KEY_PAD + 1 >= math.ceil(saturation)
    rel = (jnp.arange(T, dtype=jnp.int32)[:, None] - jnp.arange(Q, dtype=jnp.int32)[None, :]
           - KEY_PAD)
    bucket0 = t5_bucket(rel)
    bucket_far = t5_bucket(jnp.full((1,), -KEY_PAD - 1, jnp.int32))
    smem = pl.BlockSpec(memory_space=pltpu.SMEM)
    nt_max = (nblk - 1 + T // Q) // (T // Q)
    skey = seq + KEY_PAD
    return pl.pallas_call(
        partial(_dsa_main_kernel, topk=topk),
        grid=(bsz, nblk),
        in_specs=[blk(IDX_DIM), blk(A_KV_LORA),
                  pl.BlockSpec((1, IDX_HEADS, Q), lambda b, t: (b, 1, t)),
                  pl.BlockSpec((1, skey, IDX_DIM), lambda b, t: (b, 0, 0)),
                  pl.BlockSpec((1, skey, A_KV_LORA), lambda b, t: (b, 0, 0)),
                  full((T, Q)), smem, smem, full(wuvt.shape)],
        out_specs=tok(H * A_HEAD_DIM),
        out_shape=jax.ShapeDtypeStruct((bsz, seq, H * A_HEAD_DIM), f32),
        scratch_shapes=[pltpu.VMEM((H, T, Q), f32),
                        pltpu.VMEM((nt_max, T, Q), jnp.int32),
                        pltpu.VMEM((A_KV_LORA, H * Q), f32),
                        pltpu.VMEM((1, H * Q), f32),
                        pltpu.VMEM((1, H * Q), f32)],
        compiler_params=pltpu.CompilerParams(
            dimension_semantics=("arbitrary", "arbitrary"), vmem_limit_bytes=VMEM_LIMIT),
        name="dsa_main",
    )(a_t, qlat_t, widx_t, kidxp, ckvp, bucket0, bucket_far, rel_bias.astype(f32), wuvt)


PEER_SCORE_TOKENS = 256
PEER_GATHER_TOKENS = 128
PEER_SLOTS = P_HEADS * P_TOPK
WORDS_PER_ROW = 4
PEER_TILES = 32
PEER_GROUP = 2


def _top16(s, order=None, payload=None):
    if order is None:
        order = lax.broadcasted_iota(jnp.int32, s.shape, 0).astype(jnp.float32)
    vals, picks = [], []
    for _ in range(P_TOPK):
        m = jnp.max(s, axis=0, keepdims=True)
        pos = jnp.min(jnp.where(s == m, order, float(N_EXPERTS)), axis=0, keepdims=True)
        hit = order == pos
        vals.append(m)
        if payload is None:
            picks.append(pos)
        else:
            picks.append(jnp.max(jnp.where(hit, payload, -1.0), axis=0, keepdims=True))
        s = jnp.where(hit, -jnp.inf, s)
    return vals, picks


def _staircase_candidates(v1, i1, v2, i2):
    v1m, i1m = jnp.concatenate(v1, axis=0), jnp.concatenate(i1, axis=0)
    v2m, i2m = jnp.concatenate(v2, axis=0), jnp.concatenate(i2, axis=0)
    t = v1m.shape[1]
    sub8 = lax.broadcasted_iota(jnp.int32, (8, t), 0).astype(jnp.float32)
    cand, cidx, rank = [], [], []
    for a in range(8):
        cand.append(v1[a] + v2m[:8])
        cidx.append(i1[a] * float(N_KEYS) + i2m[:8])
        rank.append(sub8 + float(a * P_TOPK))
    cand.append(v1[0] + v2m[8:])
    cidx.append(i1[0] * float(N_KEYS) + i2m[8:])
    rank.append(sub8 + 8.0)
    cand.append(v1m[8:] + v2[0])
    cidx.append(i1m[8:] * float(N_KEYS) + i2[0])
    rank.append((sub8 + 8.0) * float(P_TOPK))
    cat = lambda xs: jnp.concatenate(xs, axis=0)
    return cat(cand), cat(rank), cat(cidx)


def _peer_score_kernel(x_ref, g_ref, wpqt_ref, sk_ref, hn_ref, eidx_ref, gate_ref):
    f32, bf16 = jnp.float32, jnp.bfloat16
    h = _rms(x_ref[...], g_ref[...])
    hb = h.astype(bf16)
    for s in range(h.shape[1] // 128):
        hn_ref[pl.ds(s, h.shape[0], stride=8), :] = h[:, s * 128:(s + 1) * 128]
    qrt = _nt_dot(wpqt_ref[...], hb)
    half = P_QDIM // 2
    e_rows, g_rows = [], []
    for hd in range(P_HEADS):
        tops = []
        for p in range(2):
            qhp = qrt[(hd * 2 + p) * half:(hd * 2 + p + 1) * half, :].astype(bf16)
            s = jnp.dot(sk_ref[hd * 2 + p], qhp, preferred_element_type=f32)
            tops.append(_top16(s))
        (v1, i1), (v2, i2) = tops
        best, be = _top16(*_staircase_candidates(v1, i1, v2, i2))
        ex = [jnp.exp(b - best[0]) for b in best]
        den = ex[0]
        for k in range(1, P_TOPK):
            den = den + ex[k]
        inv = 1.0 / den
        e_rows += be
        g_rows += [x * inv for x in ex]
    eidx_ref[...] = (jnp.concatenate(e_rows, axis=0).T * float(WORDS_PER_ROW)).astype(jnp.int32)
    gate_ref[...] = jnp.concatenate(g_rows, axis=0).T


def _diag_mask():
    r = lax.broadcasted_iota(jnp.int32, (8, PEER_SLOTS * 8), 0)
    c = lax.broadcasted_iota(jnp.int32, (8, PEER_SLOTS * 8), 1)
    return (c & 7) == r


def _gather_group(idx_ref, tab_ref, g_refs, t0):
    views = [idx_ref.at[t0 + u] for u in range(len(g_refs))]
    for r in range(PEER_SLOTS):
        for view, g_ref in zip(views, g_refs):
            row0 = pl.multiple_of(view[r], WORDS_PER_ROW)
            g_ref[r * WORDS_PER_ROW:(r + 1) * WORDS_PER_ROW, :] = tab_ref[
                pl.ds(row0, WORDS_PER_ROW), :]


def _gather_pipeline(idx_ref, tab_ref, tiles, n_tokens, consume):
    grp = PEER_GROUP
    groups = len(tiles) // grp
    for g_ref in tiles[-grp:]:
        g_ref[...] = jnp.zeros(g_ref.shape, g_ref.dtype)

    def drain(group_tiles, group):
        for u in range(grp):
            consume(group_tiles[u], group * (grp // 2) + u // 2, u % 2)

    def trip(i, carry):
        for k in range(groups):
            group = groups * i + k
            _gather_group(idx_ref, tab_ref, tiles[k * grp:(k + 1) * grp], group * grp)
            drain(tiles[(k - 1) * grp:k * grp] if k else tiles[-grp:], jnp.maximum(group - 1, 0))
        return carry

    lax.fori_loop(0, n_tokens // len(tiles), trip, 0)
    drain(tiles[-grp:], n_tokens // grp - 1)


def _peer_act_kernel(idx_ref, hn_ref, tab_ref, a_ref, m_ref, *tiles):
    f32, bf16 = jnp.float32, jnp.bfloat16
    tb = a_ref.shape[0]

    def dots(g_ref, tp, u):
        hp = hn_ref[pl.ds(pl.multiple_of(tp * 16, 16), 16), :].astype(bf16)
        m = _nt_dot(hp, pltpu.bitcast(g_ref[...], bf16))
        m_ref[pl.ds(pl.multiple_of((tp * 2 + u) * 8, 8), 8), :] = m[u * 8:(u + 1) * 8, :]

    _gather_pipeline(idx_ref, tab_ref, tiles, tb, dots)
    m3 = m_ref[...].reshape(tb, 8, PEER_SLOTS * 8)
    z = jnp.sum(jnp.where(_diag_mask()[None], m3, 0.0), axis=1)
    rr = lax.broadcasted_iota(jnp.int32, (PEER_SLOTS * 8, PEER_SLOTS), 0)
    cc = lax.broadcasted_iota(jnp.int32, (PEER_SLOTS * 8, PEER_SLOTS), 1)
    pool = jnp.where((rr >> 3) == cc, 1.0, 0.0).astype(bf16)
    z_hi = z.astype(bf16)
    z_lo = (z - z_hi.astype(f32)).astype(bf16)
    a_ref[...] = (jnp.dot(z_hi, pool, preferred_element_type=f32)
                  + jnp.dot(z_lo, pool, preferred_element_type=f32))


def _peer_out_kernel(idx_ref, a_ref, gate_ref, tab_ref, o_ref, w_ref, *tiles):
    f32, bf16 = jnp.float32, jnp.bfloat16
    tb = a_ref.shape[0]
    a = a_ref[...]
    act = 0.5 * a * (1.0 + lax.erf(a * (2.0 ** -0.5)))
    wgt = (gate_ref[...] * act).astype(bf16)
    rr = lax.broadcasted_iota(jnp.int32, (PEER_SLOTS, PEER_SLOTS * 8), 0)
    cc = lax.broadcasted_iota(jnp.int32, (PEER_SLOTS, PEER_SLOTS * 8), 1)
    expand = jnp.where((cc >> 3) == rr, 1.0, 0.0).astype(bf16)
    w_ref[...] = jnp.dot(wgt, expand, preferred_element_type=f32)
    diag = _diag_mask()

    def combine(g_ref, tp, u):
        t = tp * 2 + u
        wrow = jnp.broadcast_to(w_ref[pl.ds(t, 1), :], (8, PEER_SLOTS * 8))
        wsel = jnp.where(diag, wrow, 0.0).astype(bf16)
        o_ref[pl.ds(pl.multiple_of(t * 8, 8), 8), :] = jnp.dot(
            wsel, pltpu.bitcast(g_ref[...], bf16), preferred_element_type=f32)

    _gather_pipeline(idx_ref, tab_ref, tiles, tb, combine)


PACK_ROWS = 512


def _pack_table_kernel(x_ref, o_ref):
    rows = x_ref.shape[0]
    bf16_bits = lambda v: lax.bitcast_convert_type(
        v.astype(jnp.bfloat16).astype(jnp.float32), jnp.int32)
    for c in range(WORDS_PER_ROW):
        lo = bf16_bits(x_ref[:, (2 * c) * 128:(2 * c + 1) * 128])
        hi = bf16_bits(x_ref[:, (2 * c + 1) * 128:(2 * c + 2) * 128])
        word = (hi & jnp.int32(-65536)) | lax.shift_right_logical(lo, 16)
        o_ref[pl.ds(c, rows, stride=WORDS_PER_ROW), :] = word


def _pack_table(tab):
    n_e, d = tab.shape
    return pl.pallas_call(
        _pack_table_kernel,
        grid=(n_e // PACK_ROWS,),
        in_specs=[pl.BlockSpec((PACK_ROWS, d), lambda i: (i, 0))],
        out_specs=pl.BlockSpec((PACK_ROWS * WORDS_PER_ROW, 128), lambda i: (i, 0)),
        out_shape=jax.ShapeDtypeStruct((n_e * WORDS_PER_ROW, 128), jnp.int32),
        name="pack_table",
    )(tab)


def peer_pallas(x, g_ffn, w_pq, sub_keys, u_emb, v_emb):
    n_tok, d = x.shape
    f32, bf16 = jnp.float32, jnp.bfloat16
    ts, tg = PEER_SCORE_TOKENS, PEER_GATHER_TOKENS
    wpqt = w_pq.reshape(d, P_HEADS * P_QDIM).T.astype(bf16)
    sk = sub_keys.reshape(P_HEADS * 2, N_KEYS, P_QDIM // 2).astype(bf16)
    hn, eidx, gate = pl.pallas_call(
        _peer_score_kernel,
        grid=(n_tok // ts,),
        in_specs=[pl.BlockSpec((ts, d), lambda i: (i, 0)),
                  pl.BlockSpec((1, d), lambda i: (0, 0)),
                  pl.BlockSpec(wpqt.shape, lambda i: (0, 0)),
                  pl.BlockSpec(sk.shape, lambda i: (0, 0, 0))],
        out_specs=[pl.BlockSpec((ts * 8, 128), lambda i: (i, 0)),
                   pl.BlockSpec((ts, PEER_SLOTS), lambda i: (i, 0)),
                   pl.BlockSpec((ts, PEER_SLOTS), lambda i: (i, 0))],
        out_shape=[jax.ShapeDtypeStruct((n_tok * 8, 128), f32),
                   jax.ShapeDtypeStruct((n_tok, PEER_SLOTS), jnp.int32),
                   jax.ShapeDtypeStruct((n_tok, PEER_SLOTS), f32)],
        compiler_params=pltpu.CompilerParams(vmem_limit_bytes=VMEM_LIMIT),
        name="peer_score",
    )(x, g_ffn.reshape(1, d), wpqt, sk)
    utab, vtab = _pack_table(u_emb), _pack_table(v_emb)
    idx_spec = pl.BlockSpec((tg, PEER_SLOTS), lambda i: (i, 0), memory_space=pltpu.SMEM)
    tab_spec = pl.BlockSpec(memory_space=pltpu.VMEM)
    slot_spec = pl.BlockSpec((tg, PEER_SLOTS), lambda i: (i, 0))
    row_spec = pl.BlockSpec((tg * 8, 128), lambda i: (i, 0))
    gbuf = pltpu.VMEM((PEER_SLOTS * WORDS_PER_ROW, 128), jnp.int32)
    cparams = pltpu.CompilerParams(vmem_limit_bytes=VMEM_LIMIT)
    act = pl.pallas_call(
        _peer_act_kernel,
        grid=(n_tok // tg,),
        in_specs=[idx_spec, row_spec, tab_spec],
        out_specs=slot_spec,
        out_shape=jax.ShapeDtypeStruct((n_tok, PEER_SLOTS), f32),
        scratch_shapes=[pltpu.VMEM((tg * 8, PEER_SLOTS * 8), f32)] + [gbuf] * PEER_TILES,
        compiler_params=cparams,
        name="peer_act",
    )(eidx, hn, utab)
    out = pl.pallas_call(
        _peer_out_kernel,
        grid=(n_tok // tg,),
        in_specs=[idx_spec, slot_spec, slot_spec, tab_spec],
        out_specs=row_spec,
        out_shape=jax.ShapeDtypeStruct((n_tok * 8, 128), f32),
        scratch_shapes=[pltpu.VMEM((tg, PEER_SLOTS * 8), f32)] + [gbuf] * PEER_TILES,
        compiler_params=cparams,
        name="peer_out",
    )(eidx, act, gate, vtab)
    return out


def _token_rows(y_ref):
    tm = y_ref.shape[0] // 8
    return jnp.concatenate([y_ref[pl.ds(s, tm, stride=8), :] for s in range(8)], axis=1)


def _final_kernel(x_ref, y_ref, g_ref, o_ref):
    o_ref[...] = _rms(x_ref[...] + _token_rows(y_ref), g_ref[...])


def _residual_kernel(x_ref, y_ref, o_ref):
    o_ref[...] = x_ref[...] + _token_rows(y_ref)


def residual_pallas(x, y8, g=None):
    n, d = x.shape
    tm = 512
    row = pl.BlockSpec((tm, d), lambda i: (i, 0))
    row8 = pl.BlockSpec((tm * 8, 128), lambda i: (i, 0))
    if g is None:
        kern, extra, extra_specs, name = _residual_kernel, (), [], "residual"
    else:
        kern, extra, name = _final_kernel, (g.reshape(1, d),), "final_rmsnorm"
        extra_specs = [pl.BlockSpec((1, d), lambda i: (0, 0))]
    return pl.pallas_call(
        kern,
        grid=(n // tm,),
        in_specs=[row, row8] + extra_specs,
        out_specs=row,
        out_shape=jax.ShapeDtypeStruct((n, d), x.dtype),
        name=name,
    )(x, y8, *extra)


IN_PROJ_TOKENS = 512
GDN_COLS = 3 * B_QK
GATE_ROWS = 8


def _in_proj_kernel(x_ref, g_ref, wa_ref, wkw_ref, wqkv_ref, wz_ref, wab_ref, wabt_ref,
                    cq_ref, ckv_ref, kw_ref, qkv_ref, z_ref, ab_ref, abt_ref):
    f32 = jnp.float32
    hb = _rms(x_ref[0], g_ref[...]).astype(jnp.bfloat16)
    a = jnp.dot(hb, wa_ref[...], preferred_element_type=f32)
    cq_ref[0] = a[:, :A_Q_LORA]
    ckv_ref[0] = a[:, A_Q_LORA:]
    kw_ref[0] = jnp.dot(hb, wkw_ref[...], preferred_element_type=f32)
    qkv_ref[0] = jnp.dot(hb, wqkv_ref[...], preferred_element_type=f32)
    z_ref[0] = jnp.dot(hb, wz_ref[...], preferred_element_type=f32)
    ab_ref[0] = jnp.dot(hb, wab_ref[...], preferred_element_type=f32)
    abt_ref[0] = _nt_dot(wabt_ref[...], hb)


def in_proj_pallas(x, g_mix, w_in):
    bsz, seq, d = x.shape
    f32, bf16 = jnp.float32, jnp.bfloat16
    tm = min(IN_PROJ_TOKENS, seq)
    o = np.cumsum((0,) + COL_WIDTHS)
    wb = w_in.astype(bf16)
    pad_cols = lambda w: jnp.pad(w, ((0, 0), (0, 128 - w.shape[1])))
    wa = wb[:, o[0]:o[2]]
    wkw = pad_cols(wb[:, o[2]:o[4]])
    wqkv = wb[:, o[4]:o[7]]
    wz = wb[:, o[7]:o[8]]
    wab = pad_cols(wb[:, o[8]:o[10]])
    wabt = jnp.concatenate([wb[:, o[8]:o[10]], wb[:, o[3]:o[4]]], axis=1).T
    full = lambda w: pl.BlockSpec(w.shape, lambda b, t: (0, 0))
    tok = lambda w: pl.BlockSpec((1, tm, w), lambda b, t: (b, t, 0))
    shp = lambda w: jax.ShapeDtypeStruct((bsz, seq, w), f32)
    return pl.pallas_call(
        _in_proj_kernel,
        grid=(bsz, seq // tm),
        in_specs=[tok(d), pl.BlockSpec((1, d), lambda b, t: (0, 0)),
                  full(wa), full(wkw), full(wqkv), full(wz), full(wab), full(wabt)],
        out_specs=[tok(A_Q_LORA), tok(A_KV_LORA), tok(128), tok(GDN_COLS), tok(B_QK), tok(128),
                   pl.BlockSpec((1, GATE_ROWS + IDX_HEADS, tm), lambda b, t: (b, 0, t))],
        out_shape=[shp(A_Q_LORA), shp(A_KV_LORA), shp(128), shp(GDN_COLS), shp(B_QK), shp(128),
                   jax.ShapeDtypeStruct((bsz, GATE_ROWS + IDX_HEADS, seq), f32)],
        compiler_params=pltpu.CompilerParams(vmem_limit_bytes=VMEM_LIMIT),
        name="in_proj",
    )(x, g_mix.reshape(1, d), wa, wkw, wqkv, wz, wab, wabt)


def _softplus(x):
    return jnp.maximum(x, 0.0) + jnp.log1p(jnp.exp(-jnp.abs(x)))


def _sigmoid(x):
    return 1.0 / (1.0 + jnp.exp(-x))


def _gdn_gates(pre, a_log, dt_bias, is_decay):
    g = -jnp.exp(a_log) * _softplus(pre + dt_bias)
    return jnp.where(is_decay, g, _sigmoid(pre))


def _gdn_prep_kernel(qkv_ref, halo_ref, cw_ref, ab_ref, abt_ref, alc_ref, dtc_ref, alr_ref, dtr_ref,
                     q_ref, k_ref, v_ref, gc_ref, gr_ref):
    tm = qkv_ref.shape[1]
    x = qkv_ref[0]
    halo = jnp.where(pl.program_id(1) > 0, halo_ref[0], 0.0)
    full = jnp.concatenate([halo, x], axis=0)
    y = x * cw_ref[CONV_WIDTH - 1:CONV_WIDTH, :]
    for back in range(1, CONV_WIDTH):
        shifted = pltpu.roll(full, back, axis=0)[8:, :]
        y = y + shifted * cw_ref[CONV_WIDTH - 1 - back:CONV_WIDTH - back, :]
    y = y * _sigmoid(y)
    for h in range(B_HEADS):
        cols = slice(h * B_HEAD_DIM, (h + 1) * B_HEAD_DIM)
        qh = y[:, h * B_HEAD_DIM:(h + 1) * B_HEAD_DIM]
        kh = y[:, B_QK + h * B_HEAD_DIM:B_QK + (h + 1) * B_HEAD_DIM]
        q_ref[0, :, cols] = qh * lax.rsqrt(
            jnp.sum(qh * qh, axis=-1, keepdims=True) + EPS) * (B_HEAD_DIM ** -0.5)
        k_ref[0, :, cols] = kh * lax.rsqrt(jnp.sum(kh * kh, axis=-1, keepdims=True) + EPS)
    v_ref[0] = y[:, 2 * B_QK:]
    lane = lax.broadcasted_iota(jnp.int32, (tm, 128), 1)
    gates_c = _gdn_gates(ab_ref[0], alc_ref[...], dtc_ref[...], lane < B_HEADS)
    row = lax.broadcasted_iota(jnp.int32, (GATE_ROWS, tm), 0)
    gates_r = _gdn_gates(abt_ref[0], alr_ref[...], dtr_ref[...], row < B_HEADS)
    ti = lax.broadcasted_iota(jnp.int32, (tm, tm), 0)
    tj = lax.broadcasted_iota(jnp.int32, (tm, tm), 1)
    same_chunk = (ti // CHUNK) == (tj // CHUNK)
    hi = lax.Precision.HIGHEST
    lower = jnp.where(same_chunk & (tj <= ti), 1.0, 0.0)
    upper = jnp.where(same_chunk & (ti <= tj), 1.0, 0.0)
    cum_c = jnp.dot(lower, gates_c, preferred_element_type=jnp.float32, precision=hi)
    cum_r = jnp.dot(gates_r, upper, preferred_element_type=jnp.float32, precision=hi)
    gc_ref[0] = jnp.where(lane < B_HEADS, cum_c, gates_c)
    gr_ref[0] = jnp.where(row < B_HEADS, cum_r, gates_r)


def _gdn_main_kernel(q_ref, k_ref, v_ref, z_ref, gc_ref, gr_ref, gon_ref, o_ref, s_ref):
    f32, bf16 = jnp.float32, jnp.bfloat16
    C, Dh = CHUNK, B_HEAD_DIM

    @pl.when(pl.program_id(1) == 0)
    def _():
        s_ref[...] = jnp.zeros(s_ref.shape, f32)

    ii = lax.broadcasted_iota(jnp.int32, (C, C), 0)
    jj = lax.broadcasted_iota(jnp.int32, (C, C), 1)
    causal = ii >= jj
    strict = ii > jj
    eye = jnp.where(ii == jj, 1.0, 0.0)
    mm = lambda a, b: jnp.dot(a, b, preferred_element_type=f32)
    n_chunks = q_ref.shape[1] // C
    units = []
    for c in range(n_chunks):
        rows = slice(c * C, (c + 1) * C)
        gates_c = gc_ref[0, rows, :]
        gates_r = gr_ref[0, :, c * C:(c + 1) * C]
        for h in range(B_HEADS):
            cols = slice(h * Dh, (h + 1) * Dh)
            gcum = jnp.broadcast_to(gates_c[:, h:h + 1], (C, Dh))
            beta = jnp.broadcast_to(gates_c[:, B_HEADS + h:B_HEADS + h + 1], (C, Dh))
            gcum_r = jnp.broadcast_to(gates_r[h:h + 1, :], (C, C))
            decay = jnp.where(causal, jnp.exp(jnp.minimum(gcum[:, :C] - gcum_r, 0.0)), 0.0)
            q, k, v = q_ref[0, rows, cols], k_ref[0, rows, cols], v_ref[0, rows, cols]
            qb, kb = q.astype(bf16), k.astype(bf16)
            kk = _nt_dot(kb, kb)
            qk = _nt_dot(qb, kb)
            egc = jnp.exp(gcum)
            g_last = gcum[C - 1:C, :]
            units.append(dict(
                c=c, h=h, rows=rows, cols=cols,
                neg_m=jnp.where(strict, -(beta[:, :C] * kk * decay), 0.0),
                rhs=jnp.concatenate([v * beta, k * (beta * egc)], axis=1),
                q_dec=(q * egc).astype(bf16), intra=(qk * decay).astype(bf16),
                k_tail=(k * jnp.exp(g_last - gcum)).astype(bf16), chunk_decay=jnp.exp(g_last)))
    def halves(a):
        a_h = a.astype(bf16)
        return a_h, (a - a_h.astype(f32)).astype(bf16)

    def mm3(a, b):
        (a_h, a_l), (b_h, b_l) = a, b
        return mm(a_h, b_h) + (mm(a_h, b_l) + mm(a_l, b_h))

    t_inv = [eye + un["neg_m"] for un in units]
    p_halves = [halves(un["neg_m"]) for un in units]
    for _ in range(5):
        p_halves = [halves(mm3(ph, ph)) for ph in p_halves]
        t_inv = [t + mm3(ph, halves(t)) for ph, t in zip(p_halves, t_inv)]
    sols = [mm3(halves(t), halves(un["rhs"])) for t, un in zip(t_inv, units)]
    states = [s_ref[h] for h in range(B_HEADS)]
    for c in range(n_chunks):
        group = [(un, sol) for un, sol in zip(units, sols) if un["c"] == c]
        sbs = [states[un["h"]].astype(bf16) for un, _ in group]
        vbs = [(sol[:, :Dh] - mm(sol[:, Dh:].astype(bf16), sb)).astype(bf16)
               for (un, sol), sb in zip(group, sbs)]
        outs = [mm(un["q_dec"], sb) + mm(un["intra"], vb)
                for (un, _), sb, vb in zip(group, sbs, vbs)]
        for (un, _), vb in zip(group, vbs):
            states[un["h"]] = states[un["h"]] * un["chunk_decay"] + lax.dot_general(
                un["k_tail"], vb, (((0,), (0,)), ((), ())), preferred_element_type=f32)
        for (un, _), o in zip(group, outs):
            o = o * lax.rsqrt(jnp.mean(o * o, axis=-1, keepdims=True) + EPS) * gon_ref[...]
            zz = z_ref[0, un["rows"], un["cols"]]
            o_ref[0, un["rows"], un["cols"]] = o * (zz * _sigmoid(zz))
    for h in range(B_HEADS):
        s_ref[h] = states[h]


def gdn_pallas(qkv, z, ab, abt, conv_w, a_log, dt_bias, g_onorm):
    bsz, seq, _ = qkv.shape
    f32 = jnp.float32
    tm = min(256, seq)
    zero4 = jnp.zeros((B_HEADS,), f32)
    lane_row = lambda v: jnp.pad(jnp.concatenate([v.astype(f32), zero4]), (0, 120)).reshape(1, 128)
    sub_col = lambda v: jnp.concatenate([v.astype(f32), zero4]).reshape(GATE_ROWS, 1)
    tok = lambda w, t=tm: pl.BlockSpec((1, t, w), lambda b, i: (b, i, 0))
    const = lambda shape: pl.BlockSpec(shape, lambda b, i: (0,) * len(shape))
    shp = lambda w: jax.ShapeDtypeStruct((bsz, seq, w), f32)
    q, k, v, gc, gr = pl.pallas_call(
        _gdn_prep_kernel,
        grid=(bsz, seq // tm),
        in_specs=[tok(GDN_COLS),
                  pl.BlockSpec((1, 8, GDN_COLS), lambda b, i: (b, jnp.maximum(i * (tm // 8) - 1, 0), 0)),
                  const((CONV_WIDTH, GDN_COLS)), tok(128),
                  pl.BlockSpec((1, GATE_ROWS, tm), lambda b, i: (b, 0, i)),
                  const((1, 128)), const((1, 128)), const((GATE_ROWS, 1)), const((GATE_ROWS, 1))],
        out_specs=[tok(B_QK), tok(B_QK), tok(B_QK), tok(128),
                   pl.BlockSpec((1, GATE_ROWS, tm), lambda b, i: (b, 0, i))],
        out_shape=[shp(B_QK), shp(B_QK), shp(B_QK), shp(128),
                   jax.ShapeDtypeStruct((bsz, GATE_ROWS, seq), f32)],
        compiler_params=pltpu.CompilerParams(vmem_limit_bytes=VMEM_LIMIT),
        name="gdn_prep",
    )(qkv, qkv, conv_w.astype(f32), ab, abt, lane_row(a_log), lane_row(dt_bias),
      sub_col(a_log), sub_col(dt_bias))
    ts = 2 * CHUNK
    return pl.pallas_call(
        _gdn_main_kernel,
        grid=(bsz, seq // ts),
        in_specs=[tok(B_QK, ts), tok(B_QK, ts), tok(B_QK, ts), tok(B_QK, ts), tok(128, ts),
                  pl.BlockSpec((1, GATE_ROWS, ts), lambda b, i: (b, 0, i)),
                  const((1, B_HEAD_DIM))],
        out_specs=tok(B_QK, ts),
        out_shape=shp(B_QK),
        scratch_shapes=[pltpu.VMEM((B_HEADS, B_HEAD_DIM, B_HEAD_DIM), f32)],
        compiler_params=pltpu.CompilerParams(dimension_semantics=("arbitrary", "arbitrary")),
        name="gdn_main",
    )(q, k, v, z, gc, gr, g_onorm.astype(f32).reshape(1, B_HEAD_DIM))


def _mem_kv_kernel(mem_ref, g_ref, wk_ref, wv_ref, k_ref, v_ref):
    f32, bf16 = jnp.float32, jnp.bfloat16
    mn = _rms(mem_ref[0], g_ref[...]).astype(bf16)
    k = jnp.dot(mn, wk_ref[...], preferred_element_type=f32)
    v = jnp.dot(mn, wv_ref[...], preferred_element_type=f32)
    for h in range(X_HEADS):
        cols = slice(h * X_HEAD_DIM, (h + 1) * X_HEAD_DIM)
        k_ref[0, h] = k[:, cols].astype(bf16)
        v_ref[0, h] = v[:, cols].astype(bf16)


def _mid_kernel(x_ref, oa_ref, ob_ref, wo_ref, gx_ref, wq_ref, k_ref, v_ref, wox_ref, o_ref):
    f32, bf16 = jnp.float32, jnp.bfloat16
    na = oa_ref.shape[2]
    x1 = (x_ref[0]
          + jnp.dot(oa_ref[0].astype(bf16), wo_ref[:na, :], preferred_element_type=f32)
          + jnp.dot(ob_ref[0].astype(bf16), wo_ref[na:, :], preferred_element_type=f32))
    hq = _rms(x1, gx_ref[...]).astype(bf16)
    q = jnp.dot(hq, wq_ref[...], preferred_element_type=f32)
    heads = []
    for h in range(X_HEADS):
        qh = q[:, h * X_HEAD_DIM:(h + 1) * X_HEAD_DIM].astype(bf16)
        lg = _nt_dot(qh, k_ref[0, h]) * (X_HEAD_DIM ** -0.5)
        p = jnp.exp(lg - jnp.max(lg, axis=-1, keepdims=True))
        p = (p / jnp.sum(p, axis=-1, keepdims=True)).astype(bf16)
        heads.append(jnp.dot(p, v_ref[0, h], preferred_element_type=f32).astype(bf16))
    o = jnp.concatenate(heads, axis=1)
    o_ref[0] = x1 + jnp.dot(o, wox_ref[...], preferred_element_type=f32)


def mid_pallas(x, o_a, o_b, w_out, g_cross, mem, g_mem, wq, wk, wv, wo):
    bsz, seq, d = x.shape
    f32, bf16 = jnp.float32, jnp.bfloat16
    hx = X_HEADS * X_HEAD_DIM
    m_len = mem.shape[1]
    const2 = lambda shape: pl.BlockSpec(shape, lambda b: (0,) * len(shape))
    kv_spec = pl.BlockSpec((1, X_HEADS, m_len, X_HEAD_DIM), lambda b: (b, 0, 0, 0))
    kv_shape = jax.ShapeDtypeStruct((bsz, X_HEADS, m_len, X_HEAD_DIM), bf16)
    k, v = pl.pallas_call(
        _mem_kv_kernel,
        grid=(bsz,),
        in_specs=[pl.BlockSpec((1, m_len, d), lambda b: (b, 0, 0)), const2((1, d)),
                  const2((d, hx)), const2((d, hx))],
        out_specs=[kv_spec, kv_spec],
        out_shape=[kv_shape, kv_shape],
        name="mem_kv",
    )(mem, g_mem.reshape(1, d), wk.reshape(d, hx).astype(bf16), wv.reshape(d, hx).astype(bf16))
    tm = min(512, seq)
    tok = lambda w: pl.BlockSpec((1, tm, w), lambda b, t: (b, t, 0))
    const = lambda shape: pl.BlockSpec(shape, lambda b, t: (0,) * len(shape))
    kv_spec2 = pl.BlockSpec((1, X_HEADS, m_len, X_HEAD_DIM), lambda b, t: (b, 0, 0, 0))
    return pl.pallas_call(
        _mid_kernel,
        grid=(bsz, seq // tm),
        in_specs=[tok(d), tok(o_a.shape[2]), tok(o_b.shape[2]), const((MIX_WIDTH, d)),
                  const((1, d)), const((d, hx)), kv_spec2, kv_spec2, const((hx, d))],
        out_specs=tok(d),
        out_shape=jax.ShapeDtypeStruct((bsz, seq, d), f32),
        compiler_params=pltpu.CompilerParams(vmem_limit_bytes=VMEM_LIMIT),
        name="mid",
    )(x, o_a, o_b, w_out.astype(bf16), g_cross.reshape(1, d), wq.reshape(d, hx).astype(bf16),
      k, v, wo.reshape(hx, d).astype(bf16))


def kernel(x, mem, g_mix, w_in, g_cq, g_ckv, g_kidx, w_uq, w_iq, w_uk, w_uv, rel_bias, conv_w, A_log, dt_bias, g_onorm, w_out, g_cross, g_mem, wq_x, wk_x, wv_x, wo_x, g_ffn, w_pq, sub_keys, u_emb, v_emb, g_final):
    bsz, seq, d = x.shape
    for l in range(DEPTH):
        cq, ckv, kw, qkv, z, ab, abt = in_proj_pallas(x, g_mix[l], w_in[l])
        o_a = dsa_pallas(cq, ckv, kw, abt, g_cq[l], g_ckv[l], g_kidx[l],
                         w_uq[l], w_iq[l], w_uk[l], w_uv[l], rel_bias)
        o_b = gdn_pallas(qkv, z, ab, abt, conv_w[l], A_log[l], dt_bias[l], g_onorm[l])
        x = mid_pallas(x, o_a, o_b, w_out[l], g_cross[l], mem, g_mem[l],
                       wq_x[l], wk_x[l], wv_x[l], wo_x[l])
        xf = x.reshape(bsz * seq, d)
        y8 = peer_pallas(xf, g_ffn[l], w_pq[l], sub_keys[l], u_emb[l], v_emb[l])
        if l + 1 < DEPTH:
            x = residual_pallas(xf, y8).reshape(bsz, seq, d)
    return residual_pallas(xf, y8, g_final).reshape(bsz, seq, d)
```

```python
import math
from functools import partial
import jax
import jax.numpy as jnp
from jax import lax
import numpy as np
from jax.experimental import pallas as pl
from jax.experimental.pallas import tpu as pltpu

DEPTH = 1

CHUNK = 64
Q_BLOCK = 128
EPS = 1e-6

A_HEADS = 8
A_HEAD_DIM = 64
A_Q_LORA = 256
A_KV_LORA = 256
IDX_HEADS = 8
IDX_DIM = 64
IDX_TOPK_MAX = 256
ATTN_SCALE = A_HEAD_DIM ** -0.5
IDX_SCALE = (IDX_HEADS * IDX_DIM) ** -0.5

B_HEADS = 4
B_HEAD_DIM = 128
B_QK = B_HEADS * B_HEAD_DIM
CONV_WIDTH = 4

REL_BUCKETS = 32
REL_MAX_DIST = 128

X_HEADS = 4
X_HEAD_DIM = 128

P_HEADS = 8
N_KEYS = 128
N_EXPERTS = N_KEYS * N_KEYS
P_TOPK = 16
P_QDIM = 256

COL_WIDTHS = (A_Q_LORA, A_KV_LORA, IDX_DIM, IDX_HEADS, B_QK, B_QK, B_QK, B_QK, B_HEADS, B_HEADS)
MIX_WIDTH = A_HEADS * A_HEAD_DIM + B_HEADS * B_HEAD_DIM


def t5_bucket(rel):
    half = REL_BUCKETS // 2
    max_exact = half // 2
    n = jnp.abs(rel)
    nf = jnp.maximum(n, max_exact).astype(jnp.float32)
    large = max_exact + (jnp.log(nf / max_exact) / math.log(REL_MAX_DIST / max_exact)
                         * (half - max_exact)).astype(jnp.int32)
    large = jnp.minimum(large, half - 1)
    return jnp.where(rel > 0, half, 0) + jnp.where(n < max_exact, n, large)


INT_MIN = -2147483648
MIN_NORMAL_BITS = 0x00800000
NEG_BIG = -1e30
KEY_TILE = 512
KEY_PAD = KEY_TILE - Q_BLOCK
DSA_PREP_BLOCKS = 4
VMEM_LIMIT = 56 * 1024 * 1024


def _rms(x, g):
    return x * lax.rsqrt(jnp.mean(x * x, axis=-1, keepdims=True) + EPS) * g


def _nt_dot(a, b):
    return lax.dot_general(a, b, (((1,), (1,)), ((), ())), preferred_element_type=jnp.float32)


def _dsa_prep_kernel(cq_ref, ckv_ref, kw_ref, gcq_ref, gckv_ref, gk_ref,
                     wiqt_ref, wuqt_ref, wuk_ref,
                     at_ref, qlatt_ref, ckvn_ref, kidxn_ref):
    f32, bf16 = jnp.float32, jnp.bfloat16
    Q = Q_BLOCK
    kw = kw_ref[0]
    cqn = _rms(cq_ref[0], gcq_ref[...]).astype(bf16)
    qit = _nt_dot(wiqt_ref[...], cqn)
    qt = _nt_dot(wuqt_ref[...], cqn)
    for blk in range(at_ref.shape[1]):
        toks = slice(blk * Q, (blk + 1) * Q)
        for h in range(A_HEADS):
            cols = slice(h * Q, (h + 1) * Q)
            at_ref[0, blk, :, cols] = qit[h * IDX_DIM:(h + 1) * IDX_DIM, toks].astype(bf16)
            qh = qt[h * A_HEAD_DIM:(h + 1) * A_HEAD_DIM, toks].astype(bf16)
            ql = jnp.dot(wuk_ref[h], qh, preferred_element_type=f32) * ATTN_SCALE
            qlatt_ref[0, blk, :, cols] = ql.astype(bf16)
    ckvn_ref[0] = _rms(ckv_ref[0], gckv_ref[...]).astype(bf16)
    kidxn_ref[0] = _rms(kw[:, :IDX_DIM], gk_ref[...]).astype(bf16)


def _dsa_main_kernel(at_ref, qlatt_ref, wt_ref, kidx_ref, ckv_ref, bucket_ref, far_ref, rb_ref,
                     wuvt_ref, o_ref, bias0_ref, sc_ref, hb_ref, acc_ref, m_ref, l_ref, *, topk):
    f32, i32, bf16 = jnp.float32, jnp.int32, jnp.bfloat16
    Q, T = Q_BLOCK, KEY_TILE
    i = pl.program_id(1)

    @pl.when(jnp.logical_and(pl.program_id(0) == 0, i == 0))
    def _():
        bucket = bucket_ref[...]
        for h in range(A_HEADS):
            far_bias = rb_ref[far_ref[0], h]
            bias = jnp.zeros((T, Q), f32)
            for b in range(REL_BUCKETS):
                bias = jnp.where(bucket == b, rb_ref[b, h] - far_bias, bias)
            bias0_ref[h] = bias
    e = (i + 1) * Q
    nt = (i + T // Q) // (T // Q)
    kf = jnp.float32(topk)
    wsc = wt_ref[0] * IDX_SCALE
    sub = lax.broadcasted_iota(i32, (T, Q), 0)
    qlane = lax.broadcasted_iota(i32, (T, Q), 1)
    limit = ((i * Q + qlane) // CHUNK + 1) * CHUNK
    at = at_ref[0, 0]

    def tile_start(j):
        return pl.multiple_of(e + KEY_PAD - (j + 1) * T, 128)

    def key_pos(j):
        return tile_start(j) - KEY_PAD + sub

    def score_tile(j, carry):
        kid = kidx_ref[0, pl.ds(tile_start(j), T), :]
        s = jnp.zeros((T, Q), f32)
        for g in range(IDX_HEADS // 2):
            d = jnp.dot(kid, at[:, g * 2 * Q:(g + 1) * 2 * Q], preferred_element_type=f32)
            for u in range(2):
                h = 2 * g + u
                s = s + wsc[h:h + 1, :] * jnp.maximum(d[:, u * Q:(u + 1) * Q], 0.0)
        bits = lax.bitcast_convert_type(s, i32)
        key = jnp.where(bits < 0, bits ^ jnp.int32(0x7FFFFFFF), bits)
        key = jnp.where(s == 0.0, 0, key)
        kpos = key_pos(j)
        key = jnp.where(kpos >= 0, key, INT_MIN)
        key = jnp.where(kpos < limit, key, INT_MIN)
        sc_ref[j] = key
        cut = lax.bitcast_convert_type(jnp.where(s == 0.0, 0, bits) & jnp.int32(-65536), f32)
        hb_ref[j] = jnp.where(key == INT_MIN, -jnp.inf, cut).astype(bf16)
        return carry

    lax.fori_loop(0, nt, score_tile, 0)

    def high_body(b, carry):
        v, cacc = carry
        cand = v | lax.shift_left(jnp.int32(1), 15 - b)
        khi = lax.shift_left(cand - 32768, 16)
        fbits = jnp.where(khi < 0, khi ^ jnp.int32(0x7FFFFFFF), khi) & jnp.int32(-65536)
        fbits = jnp.where(jnp.logical_and(khi > 0, khi < MIN_NORMAL_BITS), MIN_NORMAL_BITS, fbits)
        cf = lax.bitcast_convert_type(fbits, f32).astype(bf16)

        def body(j, cnt):
            hit = hb_ref[j] >= cf
            for part in range(T // 64):
                rows = slice(part * 64, (part + 1) * 64)
                cnt = jnp.where(hit[rows], cnt + jnp.bfloat16(1), cnt)
            return cnt

        cnt = lax.fori_loop(0, nt, body, jnp.zeros((64, Q), bf16))
        tot = jnp.sum(cnt.astype(f32), axis=0, keepdims=True)
        ok = tot >= kf
        return jnp.where(ok, cand, v), jnp.where(ok, tot, cacc)

    v_hi, cacc_hi = lax.fori_loop(0, 16, high_body,
                                  (jnp.zeros((1, Q), i32), jnp.zeros((1, Q), f32)))

    def count_where(pred):
        def body(j, cnt):
            hit = pred(sc_ref[j], key_pos(j))
            for part in range(T // 64):
                rows = slice(part * 64, (part + 1) * 64)
                cnt = jnp.where(hit[rows], cnt + 1.0, cnt)
            return cnt
        cnt = lax.fori_loop(0, nt, body, jnp.zeros((64, Q), f32))
        return jnp.sum(cnt, axis=0, keepdims=True)

    def bit_body(b, carry):
        u, cacc = carry
        cand = u | lax.shift_left(jnp.int32(1), 31 - b)
        tvec = cand ^ jnp.int32(INT_MIN)
        tot = count_where(lambda k, kpos: k >= tvec)
        ok = tot >= kf
        return jnp.where(ok, cand, u), jnp.where(ok, tot, cacc)

    u, cacc = lax.fori_loop(16, 32, bit_body, (lax.shift_left(v_hi, 16), cacc_hi))
    thr = jnp.maximum(u ^ jnp.int32(INT_MIN), INT_MIN + 1)
    overflow = jnp.where(u != 0, cacc, 0.0) > kf
    n_over = jnp.max(jnp.where(overflow, 1.0, 0.0), axis=1, keepdims=True)[0, 0]

    @pl.when(n_over > 0.0)
    def _():
        need = kf - count_where(lambda k, kpos: k > thr)

        def cut_body(b, cut):
            cand = cut | lax.shift_left(jnp.int32(1), 14 - b)
            cnt = count_where(lambda k, kpos: jnp.where(k == thr, kpos, cand) < cand)
            return jnp.where(cnt <= need, cand, cut)

        cut = lax.fori_loop(0, 15, cut_body, jnp.zeros((1, Q), i32))

        def drop_tile(j, carry):
            k = sc_ref[j]
            drop = jnp.where(k == thr, key_pos(j), -1) >= cut
            sc_ref[j] = jnp.where(drop, INT_MIN, k)
            return carry

        lax.fori_loop(0, nt, drop_tile, 0)

    m_ref[...] = jnp.full(m_ref.shape, NEG_BIG, f32)
    l_ref[...] = jnp.zeros(l_ref.shape, f32)
    acc_ref[...] = jnp.zeros(acc_ref.shape, f32)
    qlatt = qlatt_ref[0, 0]

    def attn_tile(j, with_bias):
        kv = ckv_ref[0, pl.ds(tile_start(j), T), :]
        kvt = kv.T
        mask_add = jnp.where(sc_ref[j] >= thr, 0.0, NEG_BIG)
        pair_cols = [slice(g * 2 * Q, (g + 1) * 2 * Q) for g in range(A_HEADS // 2)]
        logits = lambda g: jnp.dot(kv, qlatt[:, pair_cols[g]], preferred_element_type=f32)
        x_next = logits(0)
        for g in range(A_HEADS // 2):
            x_pair = x_next
            if g + 1 < A_HEADS // 2:
                x_next = logits(g + 1)
            ps, alphas = [], []
            for u in range(2):
                h = 2 * g + u
                cols = slice(h * Q, (h + 1) * Q)
                x = x_pair[:, u * Q:(u + 1) * Q] + mask_add
                if with_bias:
                    x = x + bias0_ref[h]
                m_prev = m_ref[:, cols]
                m_new = jnp.maximum(m_prev, jnp.max(x, axis=0, keepdims=True))
                p = jnp.exp(x - m_new)
                alpha = jnp.exp(m_prev - m_new)
                l_ref[:, cols] = alpha * l_ref[:, cols] + jnp.sum(p, axis=0, keepdims=True)
                m_ref[:, cols] = m_new
                ps.append(p.astype(bf16))
                alphas.append(alpha)
            pv = jnp.dot(kvt, jnp.concatenate(ps, axis=1), preferred_element_type=f32)
            acc_ref[:, pair_cols[g]] = (
                jnp.concatenate(alphas, axis=1) * acc_ref[:, pair_cols[g]] + pv)

    attn_tile(0, True)

    def attn_body(j, carry):
        attn_tile(j, False)
        return carry

    lax.fori_loop(1, nt, attn_body, 0)

    inv_l = 1.0 / l_ref[...]
    outs = []
    for h in range(A_HEADS):
        cols = slice(h * Q, (h + 1) * Q)
        o_lat_t = (acc_ref[:, cols] * inv_l[:, cols]).astype(bf16)
        outs.append(jnp.dot(wuvt_ref[h], o_lat_t, preferred_element_type=f32))
    o_ref[0] = jnp.concatenate(outs, axis=0).T


def dsa_pallas(cq, ckv, kw, widx_t, g_cq, g_ckv, g_kidx, w_uq, w_iq, w_uk, w_uv, rel_bias):
    bsz, seq, _ = cq.shape
    f32, bf16 = jnp.float32, jnp.bfloat16
    Q, T, H = Q_BLOCK, KEY_TILE, A_HEADS
    nblk = seq // Q
    topk = min(IDX_TOPK_MAX, seq // 4)
    wiqt = w_iq.reshape(A_Q_LORA, IDX_HEADS * IDX_DIM).T.astype(bf16)
    wuqt = w_uq.reshape(A_Q_LORA, H * A_HEAD_DIM).T.astype(bf16)
    wuk = jnp.transpose(w_uk, (1, 0, 2)).astype(bf16)
    wuvt = jnp.transpose(w_uv, (1, 2, 0)).astype(bf16)
    tok = lambda w: pl.BlockSpec((1, Q, w), lambda b, t: (b, t, 0))
    full = lambda shape: pl.BlockSpec(shape, lambda b, t: (0,) * len(shape))
    blk = lambda r: pl.BlockSpec((1, 1, r, H * Q), lambda b, t: (b, t, 0, 0))
    nb = math.gcd(DSA_PREP_BLOCKS, nblk)
    ptok = lambda w: pl.BlockSpec((1, nb * Q, w), lambda b, t: (b, t, 0))
    pblk = lambda r: pl.BlockSpec((1, nb, r, H * Q), lambda b, t: (b, t, 0, 0))
    a_t, qlat_t, ckvn, kidxn = pl.pallas_call(
        _dsa_prep_kernel,
        grid=(bsz, nblk // nb),
        in_specs=[ptok(A_Q_LORA), ptok(A_KV_LORA), ptok(128),
                  full((1, A_Q_LORA)), full((1, A_KV_LORA)), full((1, IDX_DIM)),
                  full(wiqt.shape), full(wuqt.shape), full(wuk.shape)],
        out_specs=[pblk(IDX_DIM), pblk(A_KV_LORA), ptok(A_KV_LORA), ptok(IDX_DIM)],
        out_shape=[jax.ShapeDtypeStruct((bsz, nblk, IDX_DIM, H * Q), bf16),
                   jax.ShapeDtypeStruct((bsz, nblk, A_KV_LORA, H * Q), bf16),
                   jax.ShapeDtypeStruct((bsz, seq, A_KV_LORA), bf16),
                   jax.ShapeDtypeStruct((bsz, seq, IDX_DIM), bf16)],
        name="dsa_prep",
    )(cq, ckv, kw, g_cq.reshape(1, -1), g_ckv.reshape(1, -1), g_kidx.reshape(1, -1),
      wiqt, wuqt, wuk)
    ckvp = jnp.pad(ckvn, ((0, 0), (KEY_PAD, 0), (0, 0)))
    kidxp = jnp.pad(kidxn, ((0, 0), (KEY_PAD, 0), (0, 0)))
    half, max_exact = REL_BUCKETS // 2, REL_BUCKETS // 4
    saturation = max_exact * (REL_MAX_DIST / max_exact) ** ((half - 1 - max_exact) / (half - max_exact))
    assert KEY_PAD + 1 >= math.ceil(saturation)
    rel = (jnp.arange(T, dtype=jnp.int32)[:, None] - jnp.arange(Q, dtype=jnp.int32)[None, :]
           - KEY_PAD)
    bucket0 = t5_bucket(rel)
    bucket_far = t5_bucket(jnp.full((1,), -KEY_PAD - 1, jnp.int32))
    smem = pl.BlockSpec(memory_space=pltpu.SMEM)
    nt_max = (nblk - 1 + T // Q) // (T // Q)
    skey = seq + KEY_PAD
    return pl.pallas_call(
        partial(_dsa_main_kernel, topk=topk),
        grid=(bsz, nblk),
        in_specs=[blk(IDX_DIM), blk(A_KV_LORA),
                  pl.BlockSpec((1, IDX_HEADS, Q), lambda b, t: (b, 1, t)),
                  pl.BlockSpec((1, skey, IDX_DIM), lambda b, t: (b, 0, 0)),
                  pl.BlockSpec((1, skey, A_KV_LORA), lambda b, t: (b, 0, 0)),
                  full((T, Q)), smem, smem, full(wuvt.shape)],
        out_specs=tok(H * A_HEAD_DIM),
        out_shape=jax.ShapeDtypeStruct((bsz, seq, H * A_HEAD_DIM), f32),
        scratch_shapes=[pltpu.VMEM((H, T, Q), f32),
                        pltpu.VMEM((nt_max, T, Q), jnp.int32),
                        pltpu.VMEM((nt_max, T, Q), bf16),
                        pltpu.VMEM((A_KV_LORA, H * Q), f32),
                        pltpu.VMEM((1, H * Q), f32),
                        pltpu.VMEM((1, H * Q), f32)],
        compiler_params=pltpu.CompilerParams(
            dimension_semantics=("arbitrary", "arbitrary"), vmem_limit_bytes=VMEM_LIMIT),
        name="dsa_main",
    )(a_t, qlat_t, widx_t, kidxp, ckvp, bucket0, bucket_far, rel_bias.astype(f32), wuvt)


PEER_SCORE_TOKENS = 256
PEER_GATHER_TOKENS = 128
PEER_SLOTS = P_HEADS * P_TOPK
WORDS_PER_ROW = 4
PEER_TILES = 32
PEER_GROUP = 2


def _top16(s, order=None, payload=None):
    if order is None:
        order = lax.broadcasted_iota(jnp.int32, s.shape, 0).astype(jnp.float32)
    vals, picks = [], []
    for _ in range(P_TOPK):
        m = jnp.max(s, axis=0, keepdims=True)
        pos = jnp.min(jnp.where(s == m, order, float(N_EXPERTS)), axis=0, keepdims=True)
        hit = order == pos
        vals.append(m)
        if payload is None:
            picks.append(pos)
        else:
            picks.append(jnp.max(jnp.where(hit, payload, -1.0), axis=0, keepdims=True))
        s = jnp.where(hit, -jnp.inf, s)
    return vals, picks


def _staircase_candidates(v1, i1, v2, i2):
    v1m, i1m = jnp.concatenate(v1, axis=0), jnp.concatenate(i1, axis=0)
    v2m, i2m = jnp.concatenate(v2, axis=0), jnp.concatenate(i2, axis=0)
    t = v1m.shape[1]
    sub8 = lax.broadcasted_iota(jnp.int32, (8, t), 0).astype(jnp.float32)
    cand, cidx, rank = [], [], []
    for a in range(8):
        cand.append(v1[a] + v2m[:8])
        cidx.append(i1[a] * float(N_KEYS) + i2m[:8])
        rank.append(sub8 + float(a * P_TOPK))
    cand.append(v1[0] + v2m[8:])
    cidx.append(i1[0] * float(N_KEYS) + i2m[8:])
    rank.append(sub8 + 8.0)
    cand.append(v1m[8:] + v2[0])
    cidx.append(i1m[8:] * float(N_KEYS) + i2[0])
    rank.append((sub8 + 8.0) * float(P_TOPK))
    cat = lambda xs: jnp.concatenate(xs, axis=0)
    return cat(cand), cat(rank), cat(cidx)


def _peer_score_kernel(x_ref, g_ref, wpqt_ref, sk_ref, hn_ref, eidx_ref, gate_ref):
    f32, bf16 = jnp.float32, jnp.bfloat16
    h = _rms(x_ref[...], g_ref[...])
    hb = h.astype(bf16)
    for s in range(h.shape[1] // 128):
        hn_ref[pl.ds(s, h.shape[0], stride=8), :] = h[:, s * 128:(s + 1) * 128]
    qrt = _nt_dot(wpqt_ref[...], hb)
    half = P_QDIM // 2
    e_rows, g_rows = [], []
    for hd in range(P_HEADS):
        tops = []
        for p in range(2):
            qhp = qrt[(hd * 2 + p) * half:(hd * 2 + p + 1) * half, :].astype(bf16)
            s = jnp.dot(sk_ref[hd * 2 + p], qhp, preferred_element_type=f32)
            tops.append(_top16(s))
        (v1, i1), (v2, i2) = tops
        best, be = _top16(*_staircase_candidates(v1, i1, v2, i2))
        ex = [jnp.exp(b - best[0]) for b in best]
        den = ex[0]
        for k in range(1, P_TOPK):
            den = den + ex[k]
        inv = 1.0 / den
        e_rows += be
        g_rows += [x * inv for x in ex]
    eidx_ref[...] = (jnp.concatenate(e_rows, axis=0).T * float(WORDS_PER_ROW)).astype(jnp.int32)
    gate_ref[...] = jnp.concatenate(g_rows, axis=0).T


def _diag_mask():
    r = lax.broadcasted_iota(jnp.int32, (8, PEER_SLOTS * 8), 0)
    c = lax.broadcasted_iota(jnp.int32, (8, PEER_SLOTS * 8), 1)
    return (c & 7) == r


def _gather_group(idx_ref, tab_ref, g_refs, t0):
    views = [idx_ref.at[t0 + u] for u in range(len(g_refs))]
    for r in range(PEER_SLOTS):
        for view, g_ref in zip(views, g_refs):
            row0 = pl.multiple_of(view[r], WORDS_PER_ROW)
            g_ref[r * WORDS_PER_ROW:(r + 1) * WORDS_PER_ROW, :] = tab_ref[
                pl.ds(row0, WORDS_PER_ROW), :]


def _gather_pipeline(idx_ref, tab_ref, tiles, n_tokens, consume):
    grp = PEER_GROUP
    groups = len(tiles) // grp
    for g_ref in tiles[-grp:]:
        g_ref[...] = jnp.zeros(g_ref.shape, g_ref.dtype)

    def drain(group_tiles, group):
        for u in range(grp):
            consume(group_tiles[u], group * (grp // 2) + u // 2, u % 2)

    def trip(i, carry):
        for k in range(groups):
            group = groups * i + k
            _gather_group(idx_ref, tab_ref, tiles[k * grp:(k + 1) * grp], group * grp)
            drain(tiles[(k - 1) * grp:k * grp] if k else tiles[-grp:], jnp.maximum(group - 1, 0))
        return carry

    lax.fori_loop(0, n_tokens // len(tiles), trip, 0)
    drain(tiles[-grp:], n_tokens // grp - 1)


def _peer_act_kernel(idx_ref, hn_ref, tab_ref, a_ref, m_ref, *tiles):
    f32, bf16 = jnp.float32, jnp.bfloat16
    tb = a_ref.shape[0]

    def dots(g_ref, tp, u):
        hp = hn_ref[pl.ds(pl.multiple_of(tp * 16, 16), 16), :].astype(bf16)
        m = _nt_dot(hp, pltpu.bitcast(g_ref[...], bf16))
        m_ref[pl.ds(pl.multiple_of((tp * 2 + u) * 8, 8), 8), :] = m[u * 8:(u + 1) * 8, :]

    _gather_pipeline(idx_ref, tab_ref, tiles, tb, dots)
    m3 = m_ref[...].reshape(tb, 8, PEER_SLOTS * 8)
    z = jnp.sum(jnp.where(_diag_mask()[None], m3, 0.0), axis=1)
    rr = lax.broadcasted_iota(jnp.int32, (PEER_SLOTS * 8, PEER_SLOTS), 0)
    cc = lax.broadcasted_iota(jnp.int32, (PEER_SLOTS * 8, PEER_SLOTS), 1)
    pool = jnp.where((rr >> 3) == cc, 1.0, 0.0).astype(bf16)
    z_hi = z.astype(bf16)
    z_lo = (z - z_hi.astype(f32)).astype(bf16)
    a_ref[...] = (jnp.dot(z_hi, pool, preferred_element_type=f32)
                  + jnp.dot(z_lo, pool, preferred_element_type=f32))


def _peer_out_kernel(idx_ref, a_ref, gate_ref, tab_ref, o_ref, w_ref, *tiles):
    f32, bf16 = jnp.float32, jnp.bfloat16
    tb = a_ref.shape[0]
    a = a_ref[...]
    act = 0.5 * a * (1.0 + lax.erf(a * (2.0 ** -0.5)))
    wgt = (gate_ref[...] * act).astype(bf16)
    rr = lax.broadcasted_iota(jnp.int32, (PEER_SLOTS, PEER_SLOTS * 8), 0)
    cc = lax.broadcasted_iota(jnp.int32, (PEER_SLOTS, PEER_SLOTS * 8), 1)
    expand = jnp.where((cc >> 3) == rr, 1.0, 0.0).astype(bf16)
    w_ref[...] = jnp.dot(wgt, expand, preferred_element_type=f32)
    diag = _diag_mask()

    def combine(g_ref, tp, u):
        t = tp * 2 + u
        wrow = jnp.broadcast_to(w_ref[pl.ds(t, 1), :], (8, PEER_SLOTS * 8))
        wsel = jnp.where(diag, wrow, 0.0).astype(bf16)
        o_ref[pl.ds(pl.multiple_of(t * 8, 8), 8), :] = jnp.dot(
            wsel, pltpu.bitcast(g_ref[...], bf16), preferred_element_type=f32)

    _gather_pipeline(idx_ref, tab_ref, tiles, tb, combine)


PACK_ROWS = 512


def _pack_table_kernel(x_ref, o_ref):
    rows = x_ref.shape[0]
    bf16_bits = lambda v: lax.bitcast_convert_type(
        v.astype(jnp.bfloat16).astype(jnp.float32), jnp.int32)
    for c in range(WORDS_PER_ROW):
        lo = bf16_bits(x_ref[:, (2 * c) * 128:(2 * c + 1) * 128])
        hi = bf16_bits(x_ref[:, (2 * c + 1) * 128:(2 * c + 2) * 128])
        word = (hi & jnp.int32(-65536)) | lax.shift_right_logical(lo, 16)
        o_ref[pl.ds(c, rows, stride=WORDS_PER_ROW), :] = word


def _pack_table(tab):
    n_e, d = tab.shape
    return pl.pallas_call(
        _pack_table_kernel,
        grid=(n_e // PACK_ROWS,),
        in_specs=[pl.BlockSpec((PACK_ROWS, d), lambda i: (i, 0))],
        out_specs=pl.BlockSpec((PACK_ROWS * WORDS_PER_ROW, 128), lambda i: (i, 0)),
        out_shape=jax.ShapeDtypeStruct((n_e * WORDS_PER_ROW, 128), jnp.int32),
        name="pack_table",
    )(tab)


def peer_pallas(x, g_ffn, w_pq, sub_keys, u_emb, v_emb):
    n_tok, d = x.shape
    f32, bf16 = jnp.float32, jnp.bfloat16
    ts, tg = PEER_SCORE_TOKENS, PEER_GATHER_TOKENS
    wpqt = w_pq.reshape(d, P_HEADS * P_QDIM).T.astype(bf16)
    sk = sub_keys.reshape(P_HEADS * 2, N_KEYS, P_QDIM // 2).astype(bf16)
    hn, eidx, gate = pl.pallas_call(
        _peer_score_kernel,
        grid=(n_tok // ts,),
        in_specs=[pl.BlockSpec((ts, d), lambda i: (i, 0)),
                  pl.BlockSpec((1, d), lambda i: (0, 0)),
                  pl.BlockSpec(wpqt.shape, lambda i: (0, 0)),
                  pl.BlockSpec(sk.shape, lambda i: (0, 0, 0))],
        out_specs=[pl.BlockSpec((ts * 8, 128), lambda i: (i, 0)),
                   pl.BlockSpec((ts, PEER_SLOTS), lambda i: (i, 0)),
                   pl.BlockSpec((ts, PEER_SLOTS), lambda i: (i, 0))],
        out_shape=[jax.ShapeDtypeStruct((n_tok * 8, 128), f32),
                   jax.ShapeDtypeStruct((n_tok, PEER_SLOTS), jnp.int32),
                   jax.ShapeDtypeStruct((n_tok, PEER_SLOTS), f32)],
        compiler_params=pltpu.CompilerParams(vmem_limit_bytes=VMEM_LIMIT),
        name="peer_score",
    )(x, g_ffn.reshape(1, d), wpqt, sk)
    utab, vtab = _pack_table(u_emb), _pack_table(v_emb)
    idx_spec = pl.BlockSpec((tg, PEER_SLOTS), lambda i: (i, 0), memory_space=pltpu.SMEM)
    tab_spec = pl.BlockSpec(memory_space=pltpu.VMEM)
    slot_spec = pl.BlockSpec((tg, PEER_SLOTS), lambda i: (i, 0))
    row_spec = pl.BlockSpec((tg * 8, 128), lambda i: (i, 0))
    gbuf = pltpu.VMEM((PEER_SLOTS * WORDS_PER_ROW, 128), jnp.int32)
    cparams = pltpu.CompilerParams(vmem_limit_bytes=VMEM_LIMIT)
    act = pl.pallas_call(
        _peer_act_kernel,
        grid=(n_tok // tg,),
        in_specs=[idx_spec, row_spec, tab_spec],
        out_specs=slot_spec,
        out_shape=jax.ShapeDtypeStruct((n_tok, PEER_SLOTS), f32),
        scratch_shapes=[pltpu.VMEM((tg * 8, PEER_SLOTS * 8), f32)] + [gbuf] * PEER_TILES,
        compiler_params=cparams,
        name="peer_act",
    )(eidx, hn, utab)
    out = pl.pallas_call(
        _peer_out_kernel,
        grid=(n_tok // tg,),
        in_specs=[idx_spec, slot_spec, slot_spec, tab_spec],
        out_specs=row_spec,
        out_shape=jax.ShapeDtypeStruct((n_tok * 8, 128), f32),
        scratch_shapes=[pltpu.VMEM((tg, PEER_SLOTS * 8), f32)] + [gbuf] * PEER_TILES,
        compiler_params=cparams,
        name="peer_out",
    )(eidx, act, gate, vtab)
    return out


def _token_rows(y_ref):
    tm = y_ref.shape[0] // 8
    return jnp.concatenate([y_ref[pl.ds(s, tm, stride=8), :] for s in range(8)], axis=1)


def _final_kernel(x_ref, y_ref, g_ref, o_ref):
    o_ref[...] = _rms(x_ref[...] + _token_rows(y_ref), g_ref[...])


def _residual_kernel(x_ref, y_ref, o_ref):
    o_ref[...] = x_ref[...] + _token_rows(y_ref)


def residual_pallas(x, y8, g=None):
    n, d = x.shape
    tm = 512
    row = pl.BlockSpec((tm, d), lambda i: (i, 0))
    row8 = pl.BlockSpec((tm * 8, 128), lambda i: (i, 0))
    if g is None:
        kern, extra, extra_specs, name = _residual_kernel, (), [], "residual"
    else:
        kern, extra, name = _final_kernel, (g.reshape(1, d),), "final_rmsnorm"
        extra_specs = [pl.BlockSpec((1, d), lambda i: (0, 0))]
    return pl.pallas_call(
        kern,
        grid=(n // tm,),
        in_specs=[row, row8] + extra_specs,
        out_specs=row,
        out_shape=jax.ShapeDtypeStruct((n, d), x.dtype),
        name=name,
    )(x, y8, *extra)


IN_PROJ_TOKENS = 512
GDN_COLS = 3 * B_QK
GATE_ROWS = 8


def _in_proj_kernel(x_ref, g_ref, wa_ref, wkw_ref, wqkv_ref, wz_ref, wab_ref, wabt_ref,
                    cq_ref, ckv_ref, kw_ref, qkv_ref, z_ref, ab_ref, abt_ref):
    f32 = jnp.float32
    hb = _rms(x_ref[0], g_ref[...]).astype(jnp.bfloat16)
    a = jnp.dot(hb, wa_ref[...], preferred_element_type=f32)
    cq_ref[0] = a[:, :A_Q_LORA]
    ckv_ref[0] = a[:, A_Q_LORA:]
    kw_ref[0] = jnp.dot(hb, wkw_ref[...], preferred_element_type=f32)
    qkv_ref[0] = jnp.dot(hb, wqkv_ref[...], preferred_element_type=f32)
    z_ref[0] = jnp.dot(hb, wz_ref[...], preferred_element_type=f32)
    ab_ref[0] = jnp.dot(hb, wab_ref[...], preferred_element_type=f32)
    abt_ref[0] = _nt_dot(wabt_ref[...], hb)


def in_proj_pallas(x, g_mix, w_in):
    bsz, seq, d = x.shape
    f32, bf16 = jnp.float32, jnp.bfloat16
    tm = min(IN_PROJ_TOKENS, seq)
    o = np.cumsum((0,) + COL_WIDTHS)
    wb = w_in.astype(bf16)
    pad_cols = lambda w: jnp.pad(w, ((0, 0), (0, 128 - w.shape[1])))
    wa = wb[:, o[0]:o[2]]
    wkw = pad_cols(wb[:, o[2]:o[4]])
    wqkv = wb[:, o[4]:o[7]]
    wz = wb[:, o[7]:o[8]]
    wab = pad_cols(wb[:, o[8]:o[10]])
    wabt = jnp.concatenate([wb[:, o[8]:o[10]], wb[:, o[3]:o[4]]], axis=1).T
    full = lambda w: pl.BlockSpec(w.shape, lambda b, t: (0, 0))
    tok = lambda w: pl.BlockSpec((1, tm, w), lambda b, t: (b, t, 0))
    shp = lambda w: jax.ShapeDtypeStruct((bsz, seq, w), f32)
    return pl.pallas_call(
        _in_proj_kernel,
        grid=(bsz, seq // tm),
        in_specs=[tok(d), pl.BlockSpec((1, d), lambda b, t: (0, 0)),
                  full(wa), full(wkw), full(wqkv), full(wz), full(wab), full(wabt)],
        out_specs=[tok(A_Q_LORA), tok(A_KV_LORA), tok(128), tok(GDN_COLS), tok(B_QK), tok(128),
                   pl.BlockSpec((1, GATE_ROWS + IDX_HEADS, tm), lambda b, t: (b, 0, t))],
        out_shape=[shp(A_Q_LORA), shp(A_KV_LORA), shp(128), shp(GDN_COLS), shp(B_QK), shp(128),
                   jax.ShapeDtypeStruct((bsz, GATE_ROWS + IDX_HEADS, seq), f32)],
        compiler_params=pltpu.CompilerParams(vmem_limit_bytes=VMEM_LIMIT),
        name="in_proj",
    )(x, g_mix.reshape(1, d), wa, wkw, wqkv, wz, wab, wabt)


def _softplus(x):
    return jnp.maximum(x, 0.0) + jnp.log1p(jnp.exp(-jnp.abs(x)))


def _sigmoid(x):
    return 1.0 / (1.0 + jnp.exp(-x))


def _gdn_gates(pre, a_log, dt_bias, is_decay):
    g = -jnp.exp(a_log) * _softplus(pre + dt_bias)
    return jnp.where(is_decay, g, _sigmoid(pre))


def _gdn_prep_kernel(qkv_ref, halo_ref, cw_ref, ab_ref, abt_ref, alc_ref, dtc_ref, alr_ref, dtr_ref,
                     q_ref, k_ref, v_ref, gc_ref, gr_ref):
    tm = qkv_ref.shape[1]
    x = qkv_ref[0]
    halo = jnp.where(pl.program_id(1) > 0, halo_ref[0], 0.0)
    full = jnp.concatenate([halo, x], axis=0)
    y = x * cw_ref[CONV_WIDTH - 1:CONV_WIDTH, :]
    for back in range(1, CONV_WIDTH):
        shifted = pltpu.roll(full, back, axis=0)[8:, :]
        y = y + shifted * cw_ref[CONV_WIDTH - 1 - back:CONV_WIDTH - back, :]
    y = y * _sigmoid(y)
    for h in range(B_HEADS):
        cols = slice(h * B_HEAD_DIM, (h + 1) * B_HEAD_DIM)
        qh = y[:, h * B_HEAD_DIM:(h + 1) * B_HEAD_DIM]
        kh = y[:, B_QK + h * B_HEAD_DIM:B_QK + (h + 1) * B_HEAD_DIM]
        q_ref[0, :, cols] = qh * lax.rsqrt(
            jnp.sum(qh * qh, axis=-1, keepdims=True) + EPS) * (B_HEAD_DIM ** -0.5)
        k_ref[0, :, cols] = kh * lax.rsqrt(jnp.sum(kh * kh, axis=-1, keepdims=True) + EPS)
    v_ref[0] = y[:, 2 * B_QK:]
    lane = lax.broadcasted_iota(jnp.int32, (tm, 128), 1)
    gates_c = _gdn_gates(ab_ref[0], alc_ref[...], dtc_ref[...], lane < B_HEADS)
    row = lax.broadcasted_iota(jnp.int32, (GATE_ROWS, tm), 0)
    gates_r = _gdn_gates(abt_ref[0], alr_ref[...], dtr_ref[...], row < B_HEADS)
    ti = lax.broadcasted_iota(jnp.int32, (tm, tm), 0)
    tj = lax.broadcasted_iota(jnp.int32, (tm, tm), 1)
    same_chunk = (ti // CHUNK) == (tj // CHUNK)
    hi = lax.Precision.HIGHEST
    lower = jnp.where(same_chunk & (tj <= ti), 1.0, 0.0)
    upper = jnp.where(same_chunk & (ti <= tj), 1.0, 0.0)
    cum_c = jnp.dot(lower, gates_c, preferred_element_type=jnp.float32, precision=hi)
    cum_r = jnp.dot(gates_r, upper, preferred_element_type=jnp.float32, precision=hi)
    gc_ref[0] = jnp.where(lane < B_HEADS, cum_c, gates_c)
    gr_ref[0] = jnp.where(row < B_HEADS, cum_r, gates_r)


def _gdn_main_kernel(q_ref, k_ref, v_ref, z_ref, gc_ref, gr_ref, gon_ref, o_ref, s_ref):
    f32, bf16 = jnp.float32, jnp.bfloat16
    C, Dh = CHUNK, B_HEAD_DIM

    @pl.when(pl.program_id(1) == 0)
    def _():
        s_ref[...] = jnp.zeros(s_ref.shape, f32)

    ii = lax.broadcasted_iota(jnp.int32, (C, C), 0)
    jj = lax.broadcasted_iota(jnp.int32, (C, C), 1)
    causal = ii >= jj
    strict = ii > jj
    eye = jnp.where(ii == jj, 1.0, 0.0)
    mm = lambda a, b: jnp.dot(a, b, preferred_element_type=f32)
    n_chunks = q_ref.shape[1] // C
    units = []
    for c in range(n_chunks):
        rows = slice(c * C, (c + 1) * C)
        gates_c = gc_ref[0, rows, :]
        gates_r = gr_ref[0, :, c * C:(c + 1) * C]
        for h in range(B_HEADS):
            cols = slice(h * Dh, (h + 1) * Dh)
            gcum = jnp.broadcast_to(gates_c[:, h:h + 1], (C, Dh))
            beta = jnp.broadcast_to(gates_c[:, B_HEADS + h:B_HEADS + h + 1], (C, Dh))
            gcum_r = jnp.broadcast_to(gates_r[h:h + 1, :], (C, C))
            decay = jnp.where(causal, jnp.exp(jnp.minimum(gcum[:, :C] - gcum_r, 0.0)), 0.0)
            q, k, v = q_ref[0, rows, cols], k_ref[0, rows, cols], v_ref[0, rows, cols]
            qb, kb = q.astype(bf16), k.astype(bf16)
            kk = _nt_dot(kb, kb)
            qk = _nt_dot(qb, kb)
            egc = jnp.exp(gcum)
            g_last = gcum[C - 1:C, :]
            units.append(dict(
                c=c, h=h, rows=rows, cols=cols,
                neg_m=jnp.where(strict, -(beta[:, :C] * kk * decay), 0.0),
                rhs=jnp.concatenate([v * beta, k * (beta * egc)], axis=1),
                q_dec=(q * egc).astype(bf16), intra=(qk * decay).astype(bf16),
                k_tail=(k * jnp.exp(g_last - gcum)).astype(bf16), chunk_decay=jnp.exp(g_last)))
    def halves(a):
        a_h = a.astype(bf16)
        return a_h, (a - a_h.astype(f32)).astype(bf16)

    def mm3(a, b):
        (a_h, a_l), (b_h, b_l) = a, b
        return mm(a_h, b_h) + (mm(a_h, b_l) + mm(a_l, b_h))

    t_inv = [eye + un["neg_m"] for un in units]
    p_halves = [halves(un["neg_m"]) for un in units]
    for _ in range(5):
        p_halves = [halves(mm3(ph, ph)) for ph in p_halves]
        t_inv = [t + mm3(ph, halves(t)) for ph, t in zip(p_halves, t_inv)]
    sols = [mm3(halves(t), halves(un["rhs"])) for t, un in zip(t_inv, units)]
    states = [s_ref[h] for h in range(B_HEADS)]
    for c in range(n_chunks):
        group = [(un, sol) for un, sol in zip(units, sols) if un["c"] == c]
        sbs = [states[un["h"]].astype(bf16) for un, _ in group]
        vbs = [(sol[:, :Dh] - mm(sol[:, Dh:].astype(bf16), sb)).astype(bf16)
               for (un, sol), sb in zip(group, sbs)]
        outs = [mm(un["q_dec"], sb) + mm(un["intra"], vb)
                for (un, _), sb, vb in zip(group, sbs, vbs)]
        for (un, _), vb in zip(group, vbs):
            states[un["h"]] = states[un["h"]] * un["chunk_decay"] + lax.dot_general(
                un["k_tail"], vb, (((0,), (0,)), ((), ())), preferred_element_type=f32)
        for (un, _), o in zip(group, outs):
            o = o * lax.rsqrt(jnp.mean(o * o, axis=-1, keepdims=True) + EPS) * gon_ref[...]
            zz = z_ref[0, un["rows"], un["cols"]]
            o_ref[0, un["rows"], un["cols"]] = o * (zz * _sigmoid(zz))
    for h in range(B_HEADS):
        s_ref[h] = states[h]


def gdn_pallas(qkv, z, ab, abt, conv_w, a_log, dt_bias, g_onorm):
    bsz, seq, _ = qkv.shape
    f32 = jnp.float32
    tm = min(256, seq)
    zero4 = jnp.zeros((B_HEADS,), f32)
    lane_row = lambda v: jnp.pad(jnp.concatenate([v.astype(f32), zero4]), (0, 120)).reshape(1, 128)
    sub_col = lambda v: jnp.concatenate([v.astype(f32), zero4]).reshape(GATE_ROWS, 1)
    tok = lambda w, t=tm: pl.BlockSpec((1, t, w), lambda b, i: (b, i, 0))
    const = lambda shape: pl.BlockSpec(shape, lambda b, i: (0,) * len(shape))
    shp = lambda w: jax.ShapeDtypeStruct((bsz, seq, w), f32)
    q, k, v, gc, gr = pl.pallas_call(
        _gdn_prep_kernel,
        grid=(bsz, seq // tm),
        in_specs=[tok(GDN_COLS),
                  pl.BlockSpec((1, 8, GDN_COLS), lambda b, i: (b, jnp.maximum(i * (tm // 8) - 1, 0), 0)),
                  const((CONV_WIDTH, GDN_COLS)), tok(128),
                  pl.BlockSpec((1, GATE_ROWS, tm), lambda b, i: (b, 0, i)),
                  const((1, 128)), const((1, 128)), const((GATE_ROWS, 1)), const((GATE_ROWS, 1))],
        out_specs=[tok(B_QK), tok(B_QK), tok(B_QK), tok(128),
                   pl.BlockSpec((1, GATE_ROWS, tm), lambda b, i: (b, 0, i))],
        out_shape=[shp(B_QK), shp(B_QK), shp(B_QK), shp(128),
                   jax.ShapeDtypeStruct((bsz, GATE_ROWS, seq), f32)],
        compiler_params=pltpu.CompilerParams(vmem_limit_bytes=VMEM_LIMIT),
        name="gdn_prep",
    )(qkv, qkv, conv_w.astype(f32), ab, abt, lane_row(a_log), lane_row(dt_bias),
      sub_col(a_log), sub_col(dt_bias))
    ts = 2 * CHUNK
    return pl.pallas_call(
        _gdn_main_kernel,
        grid=(bsz, seq // ts),
        in_specs=[tok(B_QK, ts), tok(B_QK, ts), tok(B_QK, ts), tok(B_QK, ts), tok(128, ts),
                  pl.BlockSpec((1, GATE_ROWS, ts), lambda b, i: (b, 0, i)),
                  const((1, B_HEAD_DIM))],
        out_specs=tok(B_QK, ts),
        out_shape=shp(B_QK),
        scratch_shapes=[pltpu.VMEM((B_HEADS, B_HEAD_DIM, B_HEAD_DIM), f32)],
        compiler_params=pltpu.CompilerParams(dimension_semantics=("arbitrary", "arbitrary")),
        name="gdn_main",
    )(q, k, v, z, gc, gr, g_onorm.astype(f32).reshape(1, B_HEAD_DIM))


def _mem_kv_kernel(mem_ref, g_ref, wk_ref, wv_ref, k_ref, v_ref):
    f32, bf16 = jnp.float32, jnp.bfloat16
    mn = _rms(mem_ref[0], g_ref[...]).astype(bf16)
    k = jnp.dot(mn, wk_ref[...], preferred_element_type=f32)
    v = jnp.dot(mn, wv_ref[...], preferred_element_type=f32)
    for h in range(X_HEADS):
        cols = slice(h * X_HEAD_DIM, (h + 1) * X_HEAD_DIM)
        k_ref[0, h] = k[:, cols].astype(bf16)
        v_ref[0, h] = v[:, cols].astype(bf16)


def _mid_kernel(x_ref, oa_ref, ob_ref, wo_ref, gx_ref, wq_ref, k_ref, v_ref, wox_ref, o_ref):
    f32, bf16 = jnp.float32, jnp.bfloat16
    na = oa_ref.shape[2]
    x1 = (x_ref[0]
          + jnp.dot(oa_ref[0].astype(bf16), wo_ref[:na, :], preferred_element_type=f32)
          + jnp.dot(ob_ref[0].astype(bf16), wo_ref[na:, :], preferred_element_type=f32))
    hq = _rms(x1, gx_ref[...]).astype(bf16)
    q = jnp.dot(hq, wq_ref[...], preferred_element_type=f32)
    heads = []
    for h in range(X_HEADS):
        qh = q[:, h * X_HEAD_DIM:(h + 1) * X_HEAD_DIM].astype(bf16)
        lg = _nt_dot(qh, k_ref[0, h]) * (X_HEAD_DIM ** -0.5)
        p = jnp.exp(lg - jnp.max(lg, axis=-1, keepdims=True))
        p = (p / jnp.sum(p, axis=-1, keepdims=True)).astype(bf16)
        heads.append(jnp.dot(p, v_ref[0, h], preferred_element_type=f32).astype(bf16))
    o = jnp.concatenate(heads, axis=1)
    o_ref[0] = x1 + jnp.dot(o, wox_ref[...], preferred_element_type=f32)


def mid_pallas(x, o_a, o_b, w_out, g_cross, mem, g_mem, wq, wk, wv, wo):
    bsz, seq, d = x.shape
    f32, bf16 = jnp.float32, jnp.bfloat16
    hx = X_HEADS * X_HEAD_DIM
    m_len = mem.shape[1]
    const2 = lambda shape: pl.BlockSpec(shape, lambda b: (0,) * len(shape))
    kv_spec = pl.BlockSpec((1, X_HEADS, m_len, X_HEAD_DIM), lambda b: (b, 0, 0, 0))
    kv_shape = jax.ShapeDtypeStruct((bsz, X_HEADS, m_len, X_HEAD_DIM), bf16)
    k, v = pl.pallas_call(
        _mem_kv_kernel,
        grid=(bsz,),
        in_specs=[pl.BlockSpec((1, m_len, d), lambda b: (b, 0, 0)), const2((1, d)),
                  const2((d, hx)), const2((d, hx))],
        out_specs=[kv_spec, kv_spec],
        out_shape=[kv_shape, kv_shape],
        name="mem_kv",
    )(mem, g_mem.reshape(1, d), wk.reshape(d, hx).astype(bf16), wv.reshape(d, hx).astype(bf16))
    tm = min(512, seq)
    tok = lambda w: pl.BlockSpec((1, tm, w), lambda b, t: (b, t, 0))
    const = lambda shape: pl.BlockSpec(shape, lambda b, t: (0,) * len(shape))
    kv_spec2 = pl.BlockSpec((1, X_HEADS, m_len, X_HEAD_DIM), lambda b, t: (b, 0, 0, 0))
    return pl.pallas_call(
        _mid_kernel,
        grid=(bsz, seq // tm),
        in_specs=[tok(d), tok(o_a.shape[2]), tok(o_b.shape[2]), const((MIX_WIDTH, d)),
                  const((1, d)), const((d, hx)), kv_spec2, kv_spec2, const((hx, d))],
        out_specs=tok(d),
        out_shape=jax.ShapeDtypeStruct((bsz, seq, d), f32),
        compiler_params=pltpu.CompilerParams(vmem_limit_bytes=VMEM_LIMIT),
        name="mid",
    )(x, o_a, o_b, w_out.astype(bf16), g_cross.reshape(1, d), wq.reshape(d, hx).astype(bf16),
      k, v, wo.reshape(hx, d).astype(bf16))


def kernel(x, mem, g_mix, w_in, g_cq, g_ckv, g_kidx, w_uq, w_iq, w_uk, w_uv, rel_bias, conv_w, A_log, dt_bias, g_onorm, w_out, g_cross, g_mem, wq_x, wk_x, wv_x, wo_x, g_ffn, w_pq, sub_keys, u_emb, v_emb, g_final):
    bsz, seq, d = x.shape
    for l in range(DEPTH):
        cq, ckv, kw, qkv, z, ab, abt = in_proj_pallas(x, g_mix[l], w_in[l])
        o_a = dsa_pallas(cq, ckv, kw, abt, g_cq[l], g_ckv[l], g_kidx[l],
                         w_uq[l], w_iq[l], w_uk[l], w_uv[l], rel_bias)
        o_b = gdn_pallas(qkv, z, ab, abt, conv_w[l], A_log[l], dt_bias[l], g_onorm[l])
        x = mid_pallas(x, o_a, o_b, w_out[l], g_cross[l], mem, g_mem[l],
                       wq_x[l], wk_x[l], wv_x[l], wo_x[l])
        xf = x.reshape(bsz * seq, d)
        y8 = peer_pallas(xf, g_ffn[l], w_pq[l], sub_keys[l], u_emb[l], v_emb[l])
        if l + 1 < DEPTH:
            x = residual_pallas(xf, y8).reshape(bsz, seq, d)
    return residual_pallas(xf, y8, g_final).reshape(bsz, seq, d)
```

```python
import math
from functools import partial
import jax
import jax.numpy as jnp
from jax import lax
import numpy as np
from jax.experimental import pallas as pl
from jax.experimental.pallas import tpu as pltpu

DEPTH = 1

CHUNK = 64
Q_BLOCK = 128
EPS = 1e-6

A_HEADS = 8
A_HEAD_DIM = 64
A_Q_LORA = 256
A_KV_LORA = 256
IDX_HEADS = 8
IDX_DIM = 64
IDX_TOPK_MAX = 256
ATTN_SCALE = A_HEAD_DIM ** -0.5
IDX_SCALE = (IDX_HEADS * IDX_DIM) ** -0.5

B_HEADS = 4
B_HEAD_DIM = 128
B_QK = B_HEADS * B_HEAD_DIM
CONV_WIDTH = 4

REL_BUCKETS = 32
REL_MAX_DIST = 128

X_HEADS = 4
X_HEAD_DIM = 128

P_HEADS = 8
N_KEYS = 128
N_EXPERTS = N_KEYS * N_KEYS
P_TOPK = 16
P_QDIM = 256

COL_WIDTHS = (A_Q_LORA, A_KV_LORA, IDX_DIM, IDX_HEADS, B_QK, B_QK, B_QK, B_QK, B_HEADS, B_HEADS)
MIX_WIDTH = A_HEADS * A_HEAD_DIM + B_HEADS * B_HEAD_DIM


def t5_bucket(rel):
    half = REL_BUCKETS // 2
    max_exact = half // 2
    n = jnp.abs(rel)
    nf = jnp.maximum(n, max_exact).astype(jnp.float32)
    large = max_exact + (jnp.log(nf / max_exact) / math.log(REL_MAX_DIST / max_exact)
                         * (half - max_exact)).astype(jnp.int32)
    large = jnp.minimum(large, half - 1)
    return jnp.where(rel > 0, half, 0) + jnp.where(n < max_exact, n, large)


INT_MIN = -2147483648
NEG_BIG = -1e30
KEY_TILE = 512
KEY_PAD = KEY_TILE - Q_BLOCK
DSA_PREP_BLOCKS = 4
VMEM_LIMIT = 56 * 1024 * 1024


def _rms(x, g):
    return x * lax.rsqrt(jnp.mean(x * x, axis=-1, keepdims=True) + EPS) * g


def _nt_dot(a, b):
    return lax.dot_general(a, b, (((1,), (1,)), ((), ())), preferred_element_type=jnp.float32)


def _dsa_prep_kernel(cq_ref, ckv_ref, kw_ref, gcq_ref, gckv_ref, gk_ref,
                     wiqt_ref, wuqt_ref, wuk_ref,
                     at_ref, qlatt_ref, ckvn_ref, kidxn_ref):
    f32, bf16 = jnp.float32, jnp.bfloat16
    Q = Q_BLOCK
    kw = kw_ref[0]
    cqn = _rms(cq_ref[0], gcq_ref[...]).astype(bf16)
    qit = _nt_dot(wiqt_ref[...], cqn)
    qt = _nt_dot(wuqt_ref[...], cqn)
    for blk in range(at_ref.shape[1]):
        toks = slice(blk * Q, (blk + 1) * Q)
        for h in range(A_HEADS):
            cols = slice(h * Q, (h + 1) * Q)
            at_ref[0, blk, :, cols] = qit[h * IDX_DIM:(h + 1) * IDX_DIM, toks].astype(bf16)
            qh = qt[h * A_HEAD_DIM:(h + 1) * A_HEAD_DIM, toks].astype(bf16)
            ql = jnp.dot(wuk_ref[h], qh, preferred_element_type=f32) * ATTN_SCALE
            qlatt_ref[0, blk, :, cols] = ql.astype(bf16)
    ckvn_ref[0] = _rms(ckv_ref[0], gckv_ref[...]).astype(bf16)
    kidxn_ref[0] = _rms(kw[:, :IDX_DIM], gk_ref[...]).astype(bf16)


def _dsa_main_kernel(at_ref, qlatt_ref, wt_ref, kidx_ref, ckv_ref, bucket_ref, far_ref, rb_ref,
                     wuvt_ref, o_ref, bias0_ref, sc_ref, acc_ref, m_ref, l_ref, *, topk):
    f32, i32, bf16 = jnp.float32, jnp.int32, jnp.bfloat16
    Q, T = Q_BLOCK, KEY_TILE
    i = pl.program_id(1)

    @pl.when(jnp.logical_and(pl.program_id(0) == 0, i == 0))
    def _():
        bucket = bucket_ref[...]
        for h in range(A_HEADS):
            far_bias = rb_ref[far_ref[0], h]
            bias = jnp.zeros((T, Q), f32)
            for b in range(REL_BUCKETS):
                bias = jnp.where(bucket == b, rb_ref[b, h] - far_bias, bias)
            bias0_ref[h] = bias
    e = (i + 1) * Q
    nt = (i + T // Q) // (T // Q)
    kf = jnp.float32(topk)
    wsc = wt_ref[0] * IDX_SCALE
    sub = lax.broadcasted_iota(i32, (T, Q), 0)
    qlane = lax.broadcasted_iota(i32, (T, Q), 1)
    limit = ((i * Q + qlane) // CHUNK + 1) * CHUNK
    at = at_ref[0, 0]

    def tile_start(j):
        return pl.multiple_of(e + KEY_PAD - (j + 1) * T, 128)

    def key_pos(j):
        return tile_start(j) - KEY_PAD + sub

    def score_tile(j, carry):
        kid = kidx_ref[0, pl.ds(tile_start(j), T), :]
        s = jnp.zeros((T, Q), f32)
        for g in range(IDX_HEADS // 2):
            d = jnp.dot(kid, at[:, g * 2 * Q:(g + 1) * 2 * Q], preferred_element_type=f32)
            for u in range(2):
                h = 2 * g + u
                s = s + wsc[h:h + 1, :] * jnp.maximum(d[:, u * Q:(u + 1) * Q], 0.0)
        bits = lax.bitcast_convert_type(s, i32)
        key = jnp.where(bits < 0, bits ^ jnp.int32(0x7FFFFFFF), bits)
        key = jnp.where(s == 0.0, 0, key)
        kpos = key_pos(j)
        key = jnp.where(kpos >= 0, key, INT_MIN)
        sc_ref[j] = jnp.where(kpos < limit, key, INT_MIN)
        return carry

    lax.fori_loop(0, nt, score_tile, 0)

    def count_where(pred):
        def body(j, cnt):
            hit = pred(sc_ref[j], key_pos(j))
            for part in range(T // 64):
                rows = slice(part * 64, (part + 1) * 64)
                cnt = jnp.where(hit[rows], cnt + 1.0, cnt)
            return cnt
        cnt = lax.fori_loop(0, nt, body, jnp.zeros((64, Q), f32))
        return jnp.sum(cnt, axis=0, keepdims=True)

    def bit_body(b, carry):
        u, cacc = carry
        cand = u | lax.shift_left(jnp.int32(1), 31 - b)
        tvec = cand ^ jnp.int32(INT_MIN)
        tot = count_where(lambda k, kpos: k >= tvec)
        ok = tot >= kf
        return jnp.where(ok, cand, u), jnp.where(ok, tot, cacc)

    u, cacc = lax.fori_loop(0, 32, bit_body, (jnp.zeros((1, Q), i32), jnp.zeros((1, Q), f32)))
    thr = jnp.maximum(u ^ jnp.int32(INT_MIN), INT_MIN + 1)
    overflow = jnp.where(u != 0, cacc, 0.0) > kf
    n_over = jnp.max(jnp.where(overflow, 1.0, 0.0), axis=1, keepdims=True)[0, 0]

    @pl.when(n_over > 0.0)
    def _():
        need = kf - count_where(lambda k, kpos: k > thr)

        def cut_body(b, cut):
            cand = cut | lax.shift_left(jnp.int32(1), 14 - b)
            cnt = count_where(lambda k, kpos: jnp.where(k == thr, kpos, cand) < cand)
            return jnp.where(cnt <= need, cand, cut)

        cut = lax.fori_loop(0, 15, cut_body, jnp.zeros((1, Q), i32))

        def drop_tile(j, carry):
            k = sc_ref[j]
            drop = jnp.where(k == thr, key_pos(j), -1) >= cut
            sc_ref[j] = jnp.where(drop, INT_MIN, k)
            return carry

        lax.fori_loop(0, nt, drop_tile, 0)

    m_ref[...] = jnp.full(m_ref.shape, NEG_BIG, f32)
    l_ref[...] = jnp.zeros(l_ref.shape, f32)
    acc_ref[...] = jnp.zeros(acc_ref.shape, f32)
    qlatt = qlatt_ref[0, 0]

    def attn_tile(j, with_bias):
        kv = ckv_ref[0, pl.ds(tile_start(j), T), :]
        kvt = kv.T
        mask_add = jnp.where(sc_ref[j] >= thr, 0.0, NEG_BIG)
        pair_cols = [slice(g * 2 * Q, (g + 1) * 2 * Q) for g in range(A_HEADS // 2)]
        logits = lambda g: jnp.dot(kv, qlatt[:, pair_cols[g]], preferred_element_type=f32)
        x_next = logits(0)
        for g in range(A_HEADS // 2):
            x_pair = x_next
            if g + 1 < A_HEADS // 2:
                x_next = logits(g + 1)
            ps, alphas = [], []
            for u in range(2):
                h = 2 * g + u
                cols = slice(h * Q, (h + 1) * Q)
                x = x_pair[:, u * Q:(u + 1) * Q] + mask_add
                if with_bias:
                    x = x + bias0_ref[h]
                m_prev = m_ref[:, cols]
                m_new = jnp.maximum(m_prev, jnp.max(x, axis=0, keepdims=True))
                p = jnp.exp(x - m_new)
                alpha = jnp.exp(m_prev - m_new)
                l_ref[:, cols] = alpha * l_ref[:, cols] + jnp.sum(p, axis=0, keepdims=True)
                m_ref[:, cols] = m_new
                ps.append(p.astype(bf16))
                alphas.append(alpha)
            pv = jnp.dot(kvt, jnp.concatenate(ps, axis=1), preferred_element_type=f32)
            acc_ref[:, pair_cols[g]] = (
                jnp.concatenate(alphas, axis=1) * acc_ref[:, pair_cols[g]] + pv)

    attn_tile(0, True)

    def attn_body(j, carry):
        attn_tile(j, False)
        return carry

    lax.fori_loop(1, nt, attn_body, 0)

    inv_l = 1.0 / l_ref[...]
    outs = []
    for h in range(A_HEADS):
        cols = slice(h * Q, (h + 1) * Q)
        o_lat_t = (acc_ref[:, cols] * inv_l[:, cols]).astype(bf16)
        outs.append(jnp.dot(wuvt_ref[h], o_lat_t, preferred_element_type=f32))
    o_ref[0] = jnp.concatenate(outs, axis=0).T


def dsa_pallas(cq, ckv, kw, widx_t, g_cq, g_ckv, g_kidx, w_uq, w_iq, w_uk, w_uv, rel_bias):
    bsz, seq, _ = cq.shape
    f32, bf16 = jnp.float32, jnp.bfloat16
    Q, T, H = Q_BLOCK, KEY_TILE, A_HEADS
    nblk = seq // Q
    topk = min(IDX_TOPK_MAX, seq // 4)
    wiqt = w_iq.reshape(A_Q_LORA, IDX_HEADS * IDX_DIM).T.astype(bf16)
    wuqt = w_uq.reshape(A_Q_LORA, H * A_HEAD_DIM).T.astype(bf16)
    wuk = jnp.transpose(w_uk, (1, 0, 2)).astype(bf16)
    wuvt = jnp.transpose(w_uv, (1, 2, 0)).astype(bf16)
    tok = lambda w: pl.BlockSpec((1, Q, w), lambda b, t: (b, t, 0))
    full = lambda shape: pl.BlockSpec(shape, lambda b, t: (0,) * len(shape))
    blk = lambda r: pl.BlockSpec((1, 1, r, H * Q), lambda b, t: (b, t, 0, 0))
    nb = math.gcd(DSA_PREP_BLOCKS, nblk)
    ptok = lambda w: pl.BlockSpec((1, nb * Q, w), lambda b, t: (b, t, 0))
    pblk = lambda r: pl.BlockSpec((1, nb, r, H * Q), lambda b, t: (b, t, 0, 0))
    a_t, qlat_t, ckvn, kidxn = pl.pallas_call(
        _dsa_prep_kernel,
        grid=(bsz, nblk // nb),
        in_specs=[ptok(A_Q_LORA), ptok(A_KV_LORA), ptok(128),
                  full((1, A_Q_LORA)), full((1, A_KV_LORA)), full((1, IDX_DIM)),
                  full(wiqt.shape), full(wuqt.shape), full(wuk.shape)],
        out_specs=[pblk(IDX_DIM), pblk(A_KV_LORA), ptok(A_KV_LORA), ptok(IDX_DIM)],
        out_shape=[jax.ShapeDtypeStruct((bsz, nblk, IDX_DIM, H * Q), bf16),
                   jax.ShapeDtypeStruct((bsz, nblk, A_KV_LORA, H * Q), bf16),
                   jax.ShapeDtypeStruct((bsz, seq, A_KV_LORA), bf16),
                   jax.ShapeDtypeStruct((bsz, seq, IDX_DIM), bf16)],
        name="dsa_prep",
    )(cq, ckv, kw, g_cq.reshape(1, -1), g_ckv.reshape(1, -1), g_kidx.reshape(1, -1),
      wiqt, wuqt, wuk)
    ckvp = jnp.pad(ckvn, ((0, 0), (KEY_PAD, 0), (0, 0)))
    kidxp = jnp.pad(kidxn, ((0, 0), (KEY_PAD, 0), (0, 0)))
    half, max_exact = REL_BUCKETS // 2, REL_BUCKETS // 4
    saturation = max_exact * (REL_MAX_DIST / max_exact) ** ((half - 1 - max_exact) / (half - max_exact))
    assert KEY_PAD + 1 >= math.ceil(saturation)
    rel = (jnp.arange(T, dtype=jnp.int32)[:, None] - jnp.arange(Q, dtype=jnp.int32)[None, :]
           - KEY_PAD)
    bucket0 = t5_bucket(rel)
    bucket_far = t5_bucket(jnp.full((1,), -KEY_PAD - 1, jnp.int32))
    smem = pl.BlockSpec(memory_space=pltpu.SMEM)
    nt_max = (nblk - 1 + T // Q) // (T // Q)
    skey = seq + KEY_PAD
    return pl.pallas_call(
        partial(_dsa_main_kernel, topk=topk),
        grid=(bsz, nblk),
        in_specs=[blk(IDX_DIM), blk(A_KV_LORA),
                  pl.BlockSpec((1, IDX_HEADS, Q), lambda b, t: (b, 1, t)),
                  pl.BlockSpec((1, skey, IDX_DIM), lambda b, t: (b, 0, 0)),
                  pl.BlockSpec((1, skey, A_KV_LORA), lambda b, t: (b, 0, 0)),
                  full((T, Q)), smem, smem, full(wuvt.shape)],
        out_specs=tok(H * A_HEAD_DIM),
        out_shape=jax.ShapeDtypeStruct((bsz, seq, H * A_HEAD_DIM), f32),
        scratch_shapes=[pltpu.VMEM((H, T, Q), f32),
                        pltpu.VMEM((nt_max, T, Q), jnp.int32),
                        pltpu.VMEM((A_KV_LORA, H * Q), f32),
                        pltpu.VMEM((1, H * Q), f32),
                        pltpu.VMEM((1, H * Q), f32)],
        compiler_params=pltpu.CompilerParams(
            dimension_semantics=("arbitrary", "arbitrary"), vmem_limit_bytes=VMEM_LIMIT),
        name="dsa_main",
    )(a_t, qlat_t, widx_t, kidxp, ckvp, bucket0, bucket_far, rel_bias.astype(f32), wuvt)


PEER_SCORE_TOKENS = 256
PEER_GATHER_TOKENS = 128
PEER_SLOTS = P_HEADS * P_TOPK
WORDS_PER_ROW = 4
PEER_TILES = 32
PEER_GROUP = 2


def _top16(s, order=None, payload=None):
    if order is None:
        order = lax.broadcasted_iota(jnp.int32, s.shape, 0).astype(jnp.float32)
    vals, picks = [], []
    for _ in range(P_TOPK):
        m = jnp.max(s, axis=0, keepdims=True)
        pos = jnp.min(jnp.where(s == m, order, float(N_EXPERTS)), axis=0, keepdims=True)
        hit = order == pos
        vals.append(m)
        if payload is None:
            picks.append(pos)
        else:
            picks.append(jnp.max(jnp.where(hit, payload, -1.0), axis=0, keepdims=True))
        s = jnp.where(hit, -jnp.inf, s)
    return vals, picks


def _staircase_candidates(v1, i1, v2, i2):
    v1m, i1m = jnp.concatenate(v1, axis=0), jnp.concatenate(i1, axis=0)
    v2m, i2m = jnp.concatenate(v2, axis=0), jnp.concatenate(i2, axis=0)
    t = v1m.shape[1]
    sub8 = lax.broadcasted_iota(jnp.int32, (8, t), 0).astype(jnp.float32)
    cand, cidx, rank = [], [], []
    for a in range(8):
        cand.append(v1[a] + v2m[:8])
        cidx.append(i1[a] * float(N_KEYS) + i2m[:8])
        rank.append(sub8 + float(a * P_TOPK))
    cand.append(v1[0] + v2m[8:])
    cidx.append(i1[0] * float(N_KEYS) + i2m[8:])
    rank.append(sub8 + 8.0)
    cand.append(v1m[8:] + v2[0])
    cidx.append(i1m[8:] * float(N_KEYS) + i2[0])
    rank.append((sub8 + 8.0) * float(P_TOPK))
    cat = lambda xs: jnp.concatenate(xs, axis=0)
    return cat(cand), cat(rank), cat(cidx)


def _peer_score_kernel(x_ref, g_ref, wpqt_ref, sk_ref, hn_ref, eidx_ref, gate_ref):
    f32, bf16 = jnp.float32, jnp.bfloat16
    h = _rms(x_ref[...], g_ref[...])
    hb = h.astype(bf16)
    for s in range(h.shape[1] // 128):
        hn_ref[pl.ds(s, h.shape[0], stride=8), :] = h[:, s * 128:(s + 1) * 128]
    qrt = _nt_dot(wpqt_ref[...], hb)
    half = P_QDIM // 2
    e_rows, g_rows = [], []
    for hd in range(P_HEADS):
        tops = []
        for p in range(2):
            qhp = qrt[(hd * 2 + p) * half:(hd * 2 + p + 1) * half, :].astype(bf16)
            s = jnp.dot(sk_ref[hd * 2 + p], qhp, preferred_element_type=f32)
            tops.append(_top16(s))
        (v1, i1), (v2, i2) = tops
        best, be = _top16(*_staircase_candidates(v1, i1, v2, i2))
        ex = [jnp.exp(b - best[0]) for b in best]
        den = ex[0]
        for k in range(1, P_TOPK):
            den = den + ex[k]
        inv = 1.0 / den
        e_rows += be
        g_rows += [x * inv for x in ex]
    eidx_ref[...] = (jnp.concatenate(e_rows, axis=0).T * float(WORDS_PER_ROW)).astype(jnp.int32)
    gate_ref[...] = jnp.concatenate(g_rows, axis=0).T


def _diag_mask():
    r = lax.broadcasted_iota(jnp.int32, (8, PEER_SLOTS * 8), 0)
    c = lax.broadcasted_iota(jnp.int32, (8, PEER_SLOTS * 8), 1)
    return (c & 7) == r


def _gather_group(idx_ref, tab_ref, g_refs, t0):
    views = [idx_ref.at[t0 + u] for u in range(len(g_refs))]
    for r in range(PEER_SLOTS):
        for view, g_ref in zip(views, g_refs):
            row0 = pl.multiple_of(view[r], WORDS_PER_ROW)
            g_ref[r * WORDS_PER_ROW:(r + 1) * WORDS_PER_ROW, :] = tab_ref[
                pl.ds(row0, WORDS_PER_ROW), :]


def _gather_pipeline(idx_ref, tab_ref, tiles, n_tokens, consume):
    grp = PEER_GROUP
    groups = len(tiles) // grp
    for g_ref in tiles[-grp:]:
        g_ref[...] = jnp.zeros(g_ref.shape, g_ref.dtype)

    def drain(group_tiles, group):
        for u in range(grp):
            consume(group_tiles[u], group * (grp // 2) + u // 2, u % 2)

    def trip(i, carry):
        for k in range(groups):
            group = groups * i + k
            _gather_group(idx_ref, tab_ref, tiles[k * grp:(k + 1) * grp], group * grp)
            drain(tiles[(k - 1) * grp:k * grp] if k else tiles[-grp:], jnp.maximum(group - 1, 0))
        return carry

    lax.fori_loop(0, n_tokens // len(tiles), trip, 0)
    drain(tiles[-grp:], n_tokens // grp - 1)


def _peer_act_kernel(idx_ref, hn_ref, tab_ref, a_ref, m_ref, *tiles):
    f32, bf16 = jnp.float32, jnp.bfloat16
    tb = a_ref.shape[0]

    def dots(g_ref, tp, u):
        hp = hn_ref[pl.ds(pl.multiple_of(tp * 16, 16), 16), :].astype(bf16)
        m = _nt_dot(hp, pltpu.bitcast(g_ref[...], bf16))
        m_ref[pl.ds(pl.multiple_of((tp * 2 + u) * 8, 8), 8), :] = m[u * 8:(u + 1) * 8, :]

    _gather_pipeline(idx_ref, tab_ref, tiles, tb, dots)
    m3 = m_ref[...].reshape(tb, 8, PEER_SLOTS * 8)
    z = jnp.sum(jnp.where(_diag_mask()[None], m3, 0.0), axis=1)
    rr = lax.broadcasted_iota(jnp.int32, (PEER_SLOTS * 8, PEER_SLOTS), 0)
    cc = lax.broadcasted_iota(jnp.int32, (PEER_SLOTS * 8, PEER_SLOTS), 1)
    pool = jnp.where((rr >> 3) == cc, 1.0, 0.0).astype(bf16)
    z_hi = z.astype(bf16)
    z_lo = (z - z_hi.astype(f32)).astype(bf16)
    a_ref[...] = (jnp.dot(z_hi, pool, preferred_element_type=f32)
                  + jnp.dot(z_lo, pool, preferred_element_type=f32))


def _peer_out_kernel(idx_ref, a_ref, gate_ref, tab_ref, o_ref, w_ref, *tiles):
    f32, bf16 = jnp.float32, jnp.bfloat16
    tb = a_ref.shape[0]
    a = a_ref[...]
    act = 0.5 * a * (1.0 + lax.erf(a * (2.0 ** -0.5)))
    wgt = (gate_ref[...] * act).astype(bf16)
    rr = lax.broadcasted_iota(jnp.int32, (PEER_SLOTS, PEER_SLOTS * 8), 0)
    cc = lax.broadcasted_iota(jnp.int32, (PEER_SLOTS, PEER_SLOTS * 8), 1)
    expand = jnp.where((cc >> 3) == rr, 1.0, 0.0).astype(bf16)
    w_ref[...] = jnp.dot(wgt, expand, preferred_element_type=f32)
    diag = _diag_mask()

    def combine(g_ref, tp, u):
        t = tp * 2 + u
        wrow = jnp.broadcast_to(w_ref[pl.ds(t, 1), :], (8, PEER_SLOTS * 8))
        wsel = jnp.where(diag, wrow, 0.0).astype(bf16)
        o_ref[pl.ds(pl.multiple_of(t * 8, 8), 8), :] = jnp.dot(
            wsel, pltpu.bitcast(g_ref[...], bf16), preferred_element_type=f32)

    _gather_pipeline(idx_ref, tab_ref, tiles, tb, combine)


PACK_ROWS = 512


def _pack_table_kernel(x_ref, o_ref):
    rows = x_ref.shape[0]
    bf16_bits = lambda v: lax.bitcast_convert_type(
        v.astype(jnp.bfloat16).astype(jnp.float32), jnp.int32)
    for c in range(WORDS_PER_ROW):
        lo = bf16_bits(x_ref[:, (2 * c) * 128:(2 * c + 1) * 128])
        hi = bf16_bits(x_ref[:, (2 * c + 1) * 128:(2 * c + 2) * 128])
        word = (hi & jnp.int32(-65536)) | lax.shift_right_logical(lo, 16)
        o_ref[pl.ds(c, rows, stride=WORDS_PER_ROW), :] = word


def _pack_table(tab):
    n_e, d = tab.shape
    return pl.pallas_call(
        _pack_table_kernel,
        grid=(n_e // PACK_ROWS,),
        in_specs=[pl.BlockSpec((PACK_ROWS, d), lambda i: (i, 0))],
        out_specs=pl.BlockSpec((PACK_ROWS * WORDS_PER_ROW, 128), lambda i: (i, 0)),
        out_shape=jax.ShapeDtypeStruct((n_e * WORDS_PER_ROW, 128), jnp.int32),
        name="pack_table",
    )(tab)


def peer_pallas(x, g_ffn, w_pq, sub_keys, u_emb, v_emb):
    n_tok, d = x.shape
    f32, bf16 = jnp.float32, jnp.bfloat16
    ts, tg = PEER_SCORE_TOKENS, PEER_GATHER_TOKENS
    wpqt = w_pq.reshape(d, P_HEADS * P_QDIM).T.astype(bf16)
    sk = sub_keys.reshape(P_HEADS * 2, N_KEYS, P_QDIM // 2).astype(bf16)
    hn, eidx, gate = pl.pallas_call(
        _peer_score_kernel,
        grid=(n_tok // ts,),
        in_specs=[pl.BlockSpec((ts, d), lambda i: (i, 0)),
                  pl.BlockSpec((1, d), lambda i: (0, 0)),
                  pl.BlockSpec(wpqt.shape, lambda i: (0, 0)),
                  pl.BlockSpec(sk.shape, lambda i: (0, 0, 0))],
        out_specs=[pl.BlockSpec((ts * 8, 128), lambda i: (i, 0)),
                   pl.BlockSpec((ts, PEER_SLOTS), lambda i: (i, 0)),
                   pl.BlockSpec((ts, PEER_SLOTS), lambda i: (i, 0))],
        out_shape=[jax.ShapeDtypeStruct((n_tok * 8, 128), f32),
                   jax.ShapeDtypeStruct((n_tok, PEER_SLOTS), jnp.int32),
                   jax.ShapeDtypeStruct((n_tok, PEER_SLOTS), f32)],
        compiler_params=pltpu.CompilerParams(vmem_limit_bytes=VMEM_LIMIT),
        name="peer_score",
    )(x, g_ffn.reshape(1, d), wpqt, sk)
    utab, vtab = _pack_table(u_emb), _pack_table(v_emb)
    idx_spec = pl.BlockSpec((tg, PEER_SLOTS), lambda i: (i, 0), memory_space=pltpu.SMEM)
    tab_spec = pl.BlockSpec(memory_space=pltpu.VMEM)
    slot_spec = pl.BlockSpec((tg, PEER_SLOTS), lambda i: (i, 0))
    row_spec = pl.BlockSpec((tg * 8, 128), lambda i: (i, 0))
    gbuf = pltpu.VMEM((PEER_SLOTS * WORDS_PER_ROW, 128), jnp.int32)
    cparams = pltpu.CompilerParams(vmem_limit_bytes=VMEM_LIMIT)
    act = pl.pallas_call(
        _peer_act_kernel,
        grid=(n_tok // tg,),
        in_specs=[idx_spec, row_spec, tab_spec],
        out_specs=slot_spec,
        out_shape=jax.ShapeDtypeStruct((n_tok, PEER_SLOTS), f32),
        scratch_shapes=[pltpu.VMEM((tg * 8, PEER_SLOTS * 8), f32)] + [gbuf] * PEER_TILES,
        compiler_params=cparams,
        name="peer_act",
    )(eidx, hn, utab)
    out = pl.pallas_call(
        _peer_out_kernel,
        grid=(n_tok // tg,),
        in_specs=[idx_spec, slot_spec, slot_spec, tab_spec],
        out_specs=row_spec,
        out_shape=jax.ShapeDtypeStruct((n_tok * 8, 128), f32),
        scratch_shapes=[pltpu.VMEM((tg, PEER_SLOTS * 8), f32)] + [gbuf] * PEER_TILES,
        compiler_params=cparams,
        name="peer_out",
    )(eidx, act, gate, vtab)
    return out


def _token_rows(y_ref):
    tm = y_ref.shape[0] // 8
    return jnp.concatenate([y_ref[pl.ds(s, tm, stride=8), :] for s in range(8)], axis=1)


def _final_kernel(x_ref, y_ref, g_ref, o_ref):
    o_ref[...] = _rms(x_ref[...] + _token_rows(y_ref), g_ref[...])


def _residual_kernel(x_ref, y_ref, o_ref):
    o_ref[...] = x_ref[...] + _token_rows(y_ref)


def residual_pallas(x, y8, g=None):
    n, d = x.shape
    tm = 512
    row = pl.BlockSpec((tm, d), lambda i: (i, 0))
    row8 = pl.BlockSpec((tm * 8, 128), lambda i: (i, 0))
    if g is None:
        kern, extra, extra_specs, name = _residual_kernel, (), [], "residual"
    else:
        kern, extra, name = _final_kernel, (g.reshape(1, d),), "final_rmsnorm"
        extra_specs = [pl.BlockSpec((1, d), lambda i: (0, 0))]
    return pl.pallas_call(
        kern,
        grid=(n // tm,),
        in_specs=[row, row8] + extra_specs,
        out_specs=row,
        out_shape=jax.ShapeDtypeStruct((n, d), x.dtype),
        name=name,
    )(x, y8, *extra)


IN_PROJ_TOKENS = 512
GDN_COLS = 3 * B_QK
GATE_ROWS = 8


def _in_proj_kernel(x_ref, g_ref, wa_ref, wkw_ref, wqkv_ref, wz_ref, wab_ref, wabt_ref,
                    cq_ref, ckv_ref, kw_ref, qkv_ref, z_ref, ab_ref, abt_ref):
    f32 = jnp.float32
    hb = _rms(x_ref[0], g_ref[...]).astype(jnp.bfloat16)
    a = jnp.dot(hb, wa_ref[...], preferred_element_type=f32)
    cq_ref[0] = a[:, :A_Q_LORA]
    ckv_ref[0] = a[:, A_Q_LORA:]
    kw_ref[0] = jnp.dot(hb, wkw_ref[...], preferred_element_type=f32)
    qkv_ref[0] = jnp.dot(hb, wqkv_ref[...], preferred_element_type=f32)
    z_ref[0] = jnp.dot(hb, wz_ref[...], preferred_element_type=f32)
    ab_ref[0] = jnp.dot(hb, wab_ref[...], preferred_element_type=f32)
    abt_ref[0] = _nt_dot(wabt_ref[...], hb)


def in_proj_pallas(x, g_mix, w_in):
    bsz, seq, d = x.shape
    f32, bf16 = jnp.float32, jnp.bfloat16
    tm = min(IN_PROJ_TOKENS, seq)
    o = np.cumsum((0,) + COL_WIDTHS)
    wb = w_in.astype(bf16)
    pad_cols = lambda w: jnp.pad(w, ((0, 0), (0, 128 - w.shape[1])))
    wa = wb[:, o[0]:o[2]]
    wkw = pad_cols(wb[:, o[2]:o[4]])
    wqkv = wb[:, o[4]:o[7]]
    wz = wb[:, o[7]:o[8]]
    wab = pad_cols(wb[:, o[8]:o[10]])
    wabt = jnp.concatenate([wb[:, o[8]:o[10]], wb[:, o[3]:o[4]]], axis=1).T
    full = lambda w: pl.BlockSpec(w.shape, lambda b, t: (0, 0))
    tok = lambda w: pl.BlockSpec((1, tm, w), lambda b, t: (b, t, 0))
    shp = lambda w: jax.ShapeDtypeStruct((bsz, seq, w), f32)
    return pl.pallas_call(
        _in_proj_kernel,
        grid=(bsz, seq // tm),
        in_specs=[tok(d), pl.BlockSpec((1, d), lambda b, t: (0, 0)),
                  full(wa), full(wkw), full(wqkv), full(wz), full(wab), full(wabt)],
        out_specs=[tok(A_Q_LORA), tok(A_KV_LORA), tok(128), tok(GDN_COLS), tok(B_QK), tok(128),
                   pl.BlockSpec((1, GATE_ROWS + IDX_HEADS, tm), lambda b, t: (b, 0, t))],
        out_shape=[shp(A_Q_LORA), shp(A_KV_LORA), shp(128), shp(GDN_COLS), shp(B_QK), shp(128),
                   jax.ShapeDtypeStruct((bsz, GATE_ROWS + IDX_HEADS, seq), f32)],
        compiler_params=pltpu.CompilerParams(vmem_limit_bytes=VMEM_LIMIT),
        name="in_proj",
    )(x, g_mix.reshape(1, d), wa, wkw, wqkv, wz, wab, wabt)


def _softplus(x):
    return jnp.maximum(x, 0.0) + jnp.log1p(jnp.exp(-jnp.abs(x)))


def _sigmoid(x):
    return 1.0 / (1.0 + jnp.exp(-x))


def _gdn_gates(pre, a_log, dt_bias, is_decay):
    g = -jnp.exp(a_log) * _softplus(pre + dt_bias)
    return jnp.where(is_decay, g, _sigmoid(pre))


def _gdn_prep_kernel(qkv_ref, halo_ref, cw_ref, ab_ref, abt_ref, alc_ref, dtc_ref, alr_ref, dtr_ref,
                     q_ref, k_ref, v_ref, gc_ref, gr_ref):
    tm = qkv_ref.shape[1]
    x = qkv_ref[0]
    halo = jnp.where(pl.program_id(1) > 0, halo_ref[0], 0.0)
    full = jnp.concatenate([halo, x], axis=0)
    y = x * cw_ref[CONV_WIDTH - 1:CONV_WIDTH, :]
    for back in range(1, CONV_WIDTH):
        shifted = pltpu.roll(full, back, axis=0)[8:, :]
        y = y + shifted * cw_ref[CONV_WIDTH - 1 - back:CONV_WIDTH - back, :]
    y = y * _sigmoid(y)
    for h in range(B_HEADS):
        cols = slice(h * B_HEAD_DIM, (h + 1) * B_HEAD_DIM)
        qh = y[:, h * B_HEAD_DIM:(h + 1) * B_HEAD_DIM]
        kh = y[:, B_QK + h * B_HEAD_DIM:B_QK + (h + 1) * B_HEAD_DIM]
        q_ref[0, :, cols] = qh * lax.rsqrt(
            jnp.sum(qh * qh, axis=-1, keepdims=True) + EPS) * (B_HEAD_DIM ** -0.5)
        k_ref[0, :, cols] = kh * lax.rsqrt(jnp.sum(kh * kh, axis=-1, keepdims=True) + EPS)
    v_ref[0] = y[:, 2 * B_QK:]
    lane = lax.broadcasted_iota(jnp.int32, (tm, 128), 1)
    gates_c = _gdn_gates(ab_ref[0], alc_ref[...], dtc_ref[...], lane < B_HEADS)
    row = lax.broadcasted_iota(jnp.int32, (GATE_ROWS, tm), 0)
    gates_r = _gdn_gates(abt_ref[0], alr_ref[...], dtr_ref[...], row < B_HEADS)
    ti = lax.broadcasted_iota(jnp.int32, (tm, tm), 0)
    tj = lax.broadcasted_iota(jnp.int32, (tm, tm), 1)
    same_chunk = (ti // CHUNK) == (tj // CHUNK)
    hi = lax.Precision.HIGHEST
    lower = jnp.where(same_chunk & (tj <= ti), 1.0, 0.0)
    upper = jnp.where(same_chunk & (ti <= tj), 1.0, 0.0)
    cum_c = jnp.dot(lower, gates_c, preferred_element_type=jnp.float32, precision=hi)
    cum_r = jnp.dot(gates_r, upper, preferred_element_type=jnp.float32, precision=hi)
    gc_ref[0] = jnp.where(lane < B_HEADS, cum_c, gates_c)
    gr_ref[0] = jnp.where(row < B_HEADS, cum_r, gates_r)


def _gdn_main_kernel(q_ref, k_ref, v_ref, z_ref, gc_ref, gr_ref, gon_ref, o_ref, s_ref):
    f32, bf16 = jnp.float32, jnp.bfloat16
    C, Dh = CHUNK, B_HEAD_DIM

    @pl.when(pl.program_id(1) == 0)
    def _():
        s_ref[...] = jnp.zeros(s_ref.shape, f32)

    ii = lax.broadcasted_iota(jnp.int32, (C, C), 0)
    jj = lax.broadcasted_iota(jnp.int32, (C, C), 1)
    causal = ii >= jj
    strict = ii > jj
    eye = jnp.where(ii == jj, 1.0, 0.0)
    mm = lambda a, b: jnp.dot(a, b, preferred_element_type=f32)
    n_chunks = q_ref.shape[1] // C
    units = []
    for c in range(n_chunks):
        rows = slice(c * C, (c + 1) * C)
        gates_c = gc_ref[0, rows, :]
        gates_r = gr_ref[0, :, c * C:(c + 1) * C]
        for h in range(B_HEADS):
            cols = slice(h * Dh, (h + 1) * Dh)
            gcum = jnp.broadcast_to(gates_c[:, h:h + 1], (C, Dh))
            beta = jnp.broadcast_to(gates_c[:, B_HEADS + h:B_HEADS + h + 1], (C, Dh))
            gcum_r = jnp.broadcast_to(gates_r[h:h + 1, :], (C, C))
            decay = jnp.where(causal, jnp.exp(jnp.minimum(gcum[:, :C] - gcum_r, 0.0)), 0.0)
            q, k, v = q_ref[0, rows, cols], k_ref[0, rows, cols], v_ref[0, rows, cols]
            qb, kb = q.astype(bf16), k.astype(bf16)
            kk = _nt_dot(kb, kb)
            qk = _nt_dot(qb, kb)
            egc = jnp.exp(gcum)
            g_last = gcum[C - 1:C, :]
            units.append(dict(
                c=c, h=h, rows=rows, cols=cols,
                neg_m=jnp.where(strict, -(beta[:, :C] * kk * decay), 0.0),
                rhs=jnp.concatenate([v * beta, k * (beta * egc)], axis=1),
                q_dec=(q * egc).astype(bf16), intra=(qk * decay).astype(bf16),
                k_tail=(k * jnp.exp(g_last - gcum)).astype(bf16), chunk_decay=jnp.exp(g_last)))
    def halves(a):
        a_h = a.astype(bf16)
        return a_h, (a - a_h.astype(f32)).astype(bf16)

    def mm3(a, b):
        (a_h, a_l), (b_h, b_l) = a, b
        return mm(a_h, b_h) + (mm(a_h, b_l) + mm(a_l, b_h))

    t_inv = [eye + un["neg_m"] for un in units]
    p_halves = [halves(un["neg_m"]) for un in units]
    for _ in range(5):
        p_halves = [halves(mm3(ph, ph)) for ph in p_halves]
        t_inv = [t + mm3(ph, halves(t)) for ph, t in zip(p_halves, t_inv)]
    sols = [mm3(halves(t), halves(un["rhs"])) for t, un in zip(t_inv, units)]
    states = [s_ref[h] for h in range(B_HEADS)]
    for c in range(n_chunks):
        group = [(un, sol) for un, sol in zip(units, sols) if un["c"] == c]
        sbs = [states[un["h"]].astype(bf16) for un, _ in group]
        vbs = [(sol[:, :Dh] - mm(sol[:, Dh:].astype(bf16), sb)).astype(bf16)
               for (un, sol), sb in zip(group, sbs)]
        outs = [mm(un["q_dec"], sb) + mm(un["intra"], vb)
                for (un, _), sb, vb in zip(group, sbs, vbs)]
        for (un, _), vb in zip(group, vbs):
            states[un["h"]] = states[un["h"]] * un["chunk_decay"] + lax.dot_general(
                un["k_tail"], vb, (((0,), (0,)), ((), ())), preferred_element_type=f32)
        for (un, _), o in zip(group, outs):
            o = o * lax.rsqrt(jnp.mean(o * o, axis=-1, keepdims=True) + EPS) * gon_ref[...]
            zz = z_ref[0, un["rows"], un["cols"]]
            o_ref[0, un["rows"], un["cols"]] = o * (zz * _sigmoid(zz))
    for h in range(B_HEADS):
        s_ref[h] = states[h]


def gdn_pallas(qkv, z, ab, abt, conv_w, a_log, dt_bias, g_onorm):
    bsz, seq, _ = qkv.shape
    f32 = jnp.float32
    tm = min(256, seq)
    zero4 = jnp.zeros((B_HEADS,), f32)
    lane_row = lambda v: jnp.pad(jnp.concatenate([v.astype(f32), zero4]), (0, 120)).reshape(1, 128)
    sub_col = lambda v: jnp.concatenate([v.astype(f32), zero4]).reshape(GATE_ROWS, 1)
    tok = lambda w, t=tm: pl.BlockSpec((1, t, w), lambda b, i: (b, i, 0))
    const = lambda shape: pl.BlockSpec(shape, lambda b, i: (0,) * len(shape))
    shp = lambda w: jax.ShapeDtypeStruct((bsz, seq, w), f32)
    q, k, v, gc, gr = pl.pallas_call(
        _gdn_prep_kernel,
        grid=(bsz, seq // tm),
        in_specs=[tok(GDN_COLS),
                  pl.BlockSpec((1, 8, GDN_COLS), lambda b, i: (b, jnp.maximum(i * (tm // 8) - 1, 0), 0)),
                  const((CONV_WIDTH, GDN_COLS)), tok(128),
                  pl.BlockSpec((1, GATE_ROWS, tm), lambda b, i: (b, 0, i)),
                  const((1, 128)), const((1, 128)), const((GATE_ROWS, 1)), const((GATE_ROWS, 1))],
        out_specs=[tok(B_QK), tok(B_QK), tok(B_QK), tok(128),
                   pl.BlockSpec((1, GATE_ROWS, tm), lambda b, i: (b, 0, i))],
        out_shape=[shp(B_QK), shp(B_QK), shp(B_QK), shp(128),
                   jax.ShapeDtypeStruct((bsz, GATE_ROWS, seq), f32)],
        compiler_params=pltpu.CompilerParams(vmem_limit_bytes=VMEM_LIMIT),
        name="gdn_prep",
    )(qkv, qkv, conv_w.astype(f32), ab, abt, lane_row(a_log), lane_row(dt_bias),
      sub_col(a_log), sub_col(dt_bias))
    ts = 4 * CHUNK
    return pl.pallas_call(
        _gdn_main_kernel,
        grid=(bsz, seq // ts),
        in_specs=[tok(B_QK, ts), tok(B_QK, ts), tok(B_QK, ts), tok(B_QK, ts), tok(128, ts),
                  pl.BlockSpec((1, GATE_ROWS, ts), lambda b, i: (b, 0, i)),
                  const((1, B_HEAD_DIM))],
        out_specs=tok(B_QK, ts),
        out_shape=shp(B_QK),
        scratch_shapes=[pltpu.VMEM((B_HEADS, B_HEAD_DIM, B_HEAD_DIM), f32)],
        compiler_params=pltpu.CompilerParams(dimension_semantics=("arbitrary", "arbitrary")),
        name="gdn_main",
    )(q, k, v, z, gc, gr, g_onorm.astype(f32).reshape(1, B_HEAD_DIM))


def _mem_kv_kernel(mem_ref, g_ref, wk_ref, wv_ref, k_ref, v_ref):
    f32, bf16 = jnp.float32, jnp.bfloat16
    mn = _rms(mem_ref[0], g_ref[...]).astype(bf16)
    k = jnp.dot(mn, wk_ref[...], preferred_element_type=f32)
    v = jnp.dot(mn, wv_ref[...], preferred_element_type=f32)
    for h in range(X_HEADS):
        cols = slice(h * X_HEAD_DIM, (h + 1) * X_HEAD_DIM)
        k_ref[0, h] = k[:, cols].astype(bf16)
        v_ref[0, h] = v[:, cols].astype(bf16)


def _mid_kernel(x_ref, oa_ref, ob_ref, wo_ref, gx_ref, wq_ref, k_ref, v_ref, wox_ref, o_ref):
    f32, bf16 = jnp.float32, jnp.bfloat16
    na = oa_ref.shape[2]
    x1 = (x_ref[0]
          + jnp.dot(oa_ref[0].astype(bf16), wo_ref[:na, :], preferred_element_type=f32)
          + jnp.dot(ob_ref[0].astype(bf16), wo_ref[na:, :], preferred_element_type=f32))
    hq = _rms(x1, gx_ref[...]).astype(bf16)
    q = jnp.dot(hq, wq_ref[...], preferred_element_type=f32)
    heads = []
    for h in range(X_HEADS):
        qh = q[:, h * X_HEAD_DIM:(h + 1) * X_HEAD_DIM].astype(bf16)
        lg = _nt_dot(qh, k_ref[0, h]) * (X_HEAD_DIM ** -0.5)
        p = jnp.exp(lg - jnp.max(lg, axis=-1, keepdims=True))
        p = (p / jnp.sum(p, axis=-1, keepdims=True)).astype(bf16)
        heads.append(jnp.dot(p, v_ref[0, h], preferred_element_type=f32).astype(bf16))
    o = jnp.concatenate(heads, axis=1)
    o_ref[0] = x1 + jnp.dot(o, wox_ref[...], preferred_element_type=f32)


def mid_pallas(x, o_a, o_b, w_out, g_cross, mem, g_mem, wq, wk, wv, wo):
    bsz, seq, d = x.shape
    f32, bf16 = jnp.float32, jnp.bfloat16
    hx = X_HEADS * X_HEAD_DIM
    m_len = mem.shape[1]
    const2 = lambda shape: pl.BlockSpec(shape, lambda b: (0,) * len(shape))
    kv_spec = pl.BlockSpec((1, X_HEADS, m_len, X_HEAD_DIM), lambda b: (b, 0, 0, 0))
    kv_shape = jax.ShapeDtypeStruct((bsz, X_HEADS, m_len, X_HEAD_DIM), bf16)
    k, v = pl.pallas_call(
        _mem_kv_kernel,
        grid=(bsz,),
        in_specs=[pl.BlockSpec((1, m_len, d), lambda b: (b, 0, 0)), const2((1, d)),
                  const2((d, hx)), const2((d, hx))],
        out_specs=[kv_spec, kv_spec],
        out_shape=[kv_shape, kv_shape],
        name="mem_kv",
    )(mem, g_mem.reshape(1, d), wk.reshape(d, hx).astype(bf16), wv.reshape(d, hx).astype(bf16))
    tm = min(512, seq)
    tok = lambda w: pl.BlockSpec((1, tm, w), lambda b, t: (b, t, 0))
    const = lambda shape: pl.BlockSpec(shape, lambda b, t: (0,) * len(shape))
    kv_spec2 = pl.BlockSpec((1, X_HEADS, m_len, X_HEAD_DIM), lambda b, t: (b, 0, 0, 0))
    return pl.pallas_call(
        _mid_kernel,
        grid=(bsz, seq // tm),
        in_specs=[tok(d), tok(o_a.shape[2]), tok(o_b.shape[2]), const((MIX_WIDTH, d)),
                  const((1, d)), const((d, hx)), kv_spec2, kv_spec2, const((hx, d))],
        out_specs=tok(d),
        out_shape=jax.ShapeDtypeStruct((bsz, seq, d), f32),
        compiler_params=pltpu.CompilerParams(vmem_limit_bytes=VMEM_LIMIT),
        name="mid",
    )(x, o_a, o_b, w_out.astype(bf16), g_cross.reshape(1, d), wq.reshape(d, hx).astype(bf16),
      k, v, wo.reshape(hx, d).astype(bf16))


def kernel(x, mem, g_mix, w_in, g_cq, g_ckv, g_kidx, w_uq, w_iq, w_uk, w_uv, rel_bias, conv_w, A_log, dt_bias, g_onorm, w_out, g_cross, g_mem, wq_x, wk_x, wv_x, wo_x, g_ffn, w_pq, sub_keys, u_emb, v_emb, g_final):
    bsz, seq, d = x.shape
    for l in range(DEPTH):
        cq, ckv, kw, qkv, z, ab, abt = in_proj_pallas(x, g_mix[l], w_in[l])
        o_a = dsa_pallas(cq, ckv, kw, abt, g_cq[l], g_ckv[l], g_kidx[l],
                         w_uq[l], w_iq[l], w_uk[l], w_uv[l], rel_bias)
        o_b = gdn_pallas(qkv, z, ab, abt, conv_w[l], A_log[l], dt_bias[l], g_onorm[l])
        x = mid_pallas(x, o_a, o_b, w_out[l], g_cross[l], mem, g_mem[l],
                       wq_x[l], wk_x[l], wv_x[l], wo_x[l])
        xf = x.reshape(bsz * seq, d)
        y8 = peer_pallas(xf, g_ffn[l], w_pq[l], sub_keys[l], u_emb[l], v_emb[l])
        if l + 1 < DEPTH:
            x = residual_pallas(xf, y8).reshape(bsz, seq, d)
    return residual_pallas(xf, y8, g_final).reshape(bsz, seq, d)
```
